```python
import jax, jax.numpy as jnp
from jax import lax
import numpy as np

D_MODEL = 1024
BATCH = 8
SEQ = 8192
DEPTH = 1

GLA_HEADS = 4
GLA_DK = 64
GLA_DV = 128
GLA_GATE_RANK = 16
GLA_GATE_TAU = 16.0
GLA_CHUNK = 64
GLA_QK = GLA_HEADS * GLA_DK
GLA_V = GLA_HEADS * GLA_DV
ATTN_HEADS = 8
ATTN_HEAD_DIM = 64
ATTN_DIM = ATTN_HEADS * ATTN_HEAD_DIM
DILATED_PAIRS = ((128, 1), (512, 4), (2048, 16))
ATTN_BLOCK = 128
MIX_WIDTH = GLA_V + ATTN_DIM
IN_SPLITS = (GLA_QK, GLA_QK, GLA_V, GLA_V, GLA_GATE_RANK, ATTN_DIM, ATTN_DIM, ATTN_DIM)
IN_WIDTH = sum(IN_SPLITS)
D_FF = 2816
CONV_WIDTH = 3
N_MOD = 6
EPS = 1e-6

kernel_name = "hybrid_gla_dilated_attn_convffn_block"


def rms_norm(x, g):
    xf = x.astype(jnp.float32)
    y = xf * lax.rsqrt(jnp.mean(xf * xf, axis=-1, keepdims=True) + EPS)
    return (y * g.astype(jnp.float32)).astype(x.dtype)


def alibi_slopes(n):
    return jnp.asarray([2.0 ** (-8.0 * (h + 1) / n) for h in range(n)], dtype=jnp.float32)


def gla_group(q, k, v, log_a, r, norm_g):
    B, S, H, _ = q.shape
    n = S // GLA_CHUNK

    def chunks(t):
        return t.reshape(B, n, GLA_CHUNK, H, t.shape[-1]).transpose(1, 0, 3, 2, 4).astype(jnp.float32)

    qc = chunks(q) * (GLA_DK ** -0.5)
    kc, vc, gc = chunks(k), chunks(v), chunks(log_a)
    causal = jnp.tril(jnp.ones((GLA_CHUNK, GLA_CHUNK), dtype=bool))

    def step(state, inp):
        qi, ki, vi, gi = inp
        b = jnp.cumsum(gi, axis=-2)
        b_last = b[..., -1, :]
        o_inter = jnp.einsum('bhck,bhkv->bhcv', qi * jnp.exp(b), state)
        diff = b[:, :, :, None, :] - b[:, :, None, :, :]
        decay = jnp.exp(jnp.where(causal[:, :, None], diff, -jnp.inf))
        scores = jnp.einsum('bhik,bhjk,bhijk->bhij', qi, ki, decay)
        o = o_inter + jnp.einsum('bhij,bhjv->bhiv', scores, vi)
        state = state * jnp.exp(b_last)[..., None] + jnp.einsum(
            'bhck,bhcv->bhkv', ki * jnp.exp(b_last[:, :, None, :] - b), vi)
        return state, o

    state0 = jnp.zeros((B, H, GLA_DK, GLA_DV), jnp.float32)
    _, o = lax.scan(step, state0, (qc, kc, vc, gc))
    o = o.transpose(1, 0, 3, 2, 4).reshape(B, S, H, GLA_DV)
    o = rms_norm(o, norm_g).reshape(B, S, H * GLA_DV)
    return (o * jax.nn.silu(r.astype(jnp.float32))).astype(r.dtype)


def dilated_branch(q, k, v, slopes, window, dilation):
    B, S, H, E = q.shape
    L = S // dilation
    span = window // dilation
    nb = -(-L // ATTN_BLOCK)
    pad = nb * ATTN_BLOCK - L

    def to_blocks(t):
        t = t.reshape(B, L, dilation, H, E).transpose(0, 2, 3, 1, 4)
        t = jnp.pad(t, ((0, 0), (0, 0), (0, 0), (0, pad), (0, 0)))
        return t.reshape(B, dilation, H, nb, ATTN_BLOCK, E)

    def with_prev(t):
        prev = jnp.pad(t, ((0, 0), (0, 0), (0, 0), (1, 0), (0, 0), (0, 0)))[:, :, :, :-1]
        return jnp.concatenate([prev, t], axis=4)

    qb = to_blocks(q)
    kw = with_prev(to_blocks(k))
    vw = with_prev(to_blocks(v))
    s = jnp.einsum('bdhnqe,bdhnke->bdhnqk', qb, kw).astype(jnp.float32) * (E ** -0.5)
    iq = jnp.arange(ATTN_BLOCK)[:, None]
    ik = jnp.arange(2 * ATTN_BLOCK)[None, :]
    rel = iq + ATTN_BLOCK - ik
    key_idx = jnp.arange(nb)[:, None, None] * ATTN_BLOCK + ik - ATTN_BLOCK
    valid = (rel >= 0) & (rel <= span) & (key_idx >= 0)
    alibi = -slopes[:, None, None, None] * (dilation * rel).astype(jnp.float32)
    s = jnp.where(valid, s + alibi, -jnp.inf)
    m = jnp.max(s, axis=-1, keepdims=True)
    p = jnp.exp(s - m)
    den = jnp.sum(p, axis=-1, keepdims=True)
    o = jnp.einsum('bdhnqk,bdhnke->bdhnqe', p, vw.astype(jnp.float32)) / den
    lse = (m + jnp.log(den))[..., 0]
    o = o.reshape(B, dilation, H, nb * ATTN_BLOCK, E)[:, :, :, :L]
    o = o.transpose(0, 3, 1, 2, 4).reshape(B, S, H, E)
    lse = lse.reshape(B, dilation, H, nb * ATTN_BLOCK)[:, :, :, :L]
    lse = lse.transpose(0, 3, 1, 2).reshape(B, S, H)
    return o, lse


def dilated_attention_group(q, k, v):
    slopes = alibi_slopes(ATTN_HEADS)
    outs, lses = [], []
    for window, dilation in DILATED_PAIRS:
        o, lse = dilated_branch(q, k, v, slopes, window, dilation)
        outs.append(o)
        lses.append(lse)
    weights = jax.nn.softmax(jnp.stack(lses, axis=0), axis=0)
    return jnp.einsum('gbsh,gbshe->bshe', weights, jnp.stack(outs, axis=0))


def causal_depthwise_conv(u, w, b):
    K, C = w.shape
    y = lax.conv_general_dilated(u, w[:, None, :], window_strides=(1,), padding=[(K - 1, 0)],
                                 dimension_numbers=('NWC', 'WIO', 'NWC'), feature_group_count=C)
    return y + b


def _fwd_setup_inputs(seed: int = 0) -> dict:
    key = jax.random.key(seed)
    ks = jax.random.split(key, 17)
    f32 = jnp.float32
    L = DEPTH

    def nrm(k, shape, scale):
        return jax.random.normal(k, shape, f32) * scale

    return {
        "x": nrm(ks[0], (BATCH, SEQ, D_MODEL), 1.0),
        "c": nrm(ks[1], (BATCH, D_MODEL), 1.0),
        "w_ada": nrm(ks[2], (L, D_MODEL, N_MOD * D_MODEL), 0.5 * D_MODEL ** -0.5),
        "b_ada": nrm(ks[3], (L, N_MOD * D_MODEL), 0.02),
        "norm1_g": 1.0 + nrm(ks[4], (L, D_MODEL), 0.02),
        "w_in": nrm(ks[5], (L, D_MODEL, IN_WIDTH), D_MODEL ** -0.5),
        "gla_w_gate": nrm(ks[6], (L, GLA_GATE_RANK, GLA_QK), GLA_GATE_RANK ** -0.5),
        "gla_b_gate": nrm(ks[7], (L, GLA_QK), 0.02),
        "gla_norm_g": 1.0 + nrm(ks[8], (L, GLA_DV), 0.02),
        "q_norm_g": 1.0 + nrm(ks[9], (L, ATTN_HEAD_DIM), 0.02),
        "k_norm_g": 1.0 + nrm(ks[10], (L, ATTN_HEAD_DIM), 0.02),
        "w_out": nrm(ks[11], (L, MIX_WIDTH, D_MODEL), MIX_WIDTH ** -0.5),
        "norm2_g": 1.0 + nrm(ks[12], (L, D_MODEL), 0.02),
        "w_up": nrm(ks[13], (L, D_MODEL, 2 * D_FF), D_MODEL ** -0.5),
        "conv_w": nrm(ks[14], (L, CONV_WIDTH, 2 * D_FF), CONV_WIDTH ** -0.5),
        "conv_b": nrm(ks[15], (L, 2 * D_FF), 0.02),
        "w_down": nrm(ks[16], (L, D_FF, D_MODEL), D_FF ** -0.5),
    }


def _fwd_reference(x, c, w_ada, b_ada, norm1_g, w_in, gla_w_gate, gla_b_gate, gla_norm_g,
              q_norm_g, k_norm_g, w_out, norm2_g, w_up, conv_w, conv_b, w_down):
    B, S, _ = x.shape
    cond = jax.nn.silu(c)
    split_at = np.cumsum(IN_SPLITS)[:-1].tolist()
    for l in range(DEPTH):
        mod = (cond @ w_ada[l] + b_ada[l])[:, None, :]
        sh1, sc1, g1, sh2, sc2, g2 = jnp.split(mod, N_MOD, axis=-1)

        h = rms_norm(x, norm1_g[l]) * (1 + sc1) + sh1
        proj = h @ w_in[l]
        gq, gk, gv, gr, glr, aq, ak, av = jnp.split(proj, split_at, axis=-1)
        log_a = jax.nn.log_sigmoid((glr @ gla_w_gate[l] + gla_b_gate[l]).astype(jnp.float32)) / GLA_GATE_TAU
        y_gla = gla_group(gq.reshape(B, S, GLA_HEADS, GLA_DK), gk.reshape(B, S, GLA_HEADS, GLA_DK),
                          gv.reshape(B, S, GLA_HEADS, GLA_DV), log_a.reshape(B, S, GLA_HEADS, GLA_DK),
                          gr, gla_norm_g[l])
        qa = rms_norm(aq.reshape(B, S, ATTN_HEADS, ATTN_HEAD_DIM), q_norm_g[l])
        ka = rms_norm(ak.reshape(B, S, ATTN_HEADS, ATTN_HEAD_DIM), k_norm_g[l])
        y_att = dilated_attention_group(qa, ka, av.reshape(B, S, ATTN_HEADS, ATTN_HEAD_DIM))
        mixed = jnp.concatenate([y_gla.astype(x.dtype), y_att.reshape(B, S, ATTN_DIM).astype(x.dtype)], axis=-1)
        x = x + g1 * (mixed @ w_out[l])

        h = rms_norm(x, norm2_g[l]) * (1 + sc2) + sh2
        u = causal_depthwise_conv(h @ w_up[l], conv_w[l], conv_b[l])
        u_gate, u_val = jnp.split(u, 2, axis=-1)
        x = x + g2 * ((jax.nn.silu(u_gate) * u_val) @ w_down[l])
    return x


import jax as _jax
import jax.numpy as _jnp

TWIN_FORMAT = 'train_step'
FWD_PARAMS = ['x', 'c', 'w_ada', 'b_ada', 'norm1_g', 'w_in', 'gla_w_gate', 'gla_b_gate', 'gla_norm_g', 'q_norm_g', 'k_norm_g', 'w_out', 'norm2_g', 'w_up', 'conv_w', 'conv_b', 'w_down']
TWIN_WEIGHTS = ['w_ada', 'b_ada', 'norm1_g', 'w_in', 'gla_w_gate', 'gla_b_gate', 'gla_norm_g', 'q_norm_g', 'k_norm_g', 'w_out', 'norm2_g', 'w_up', 'conv_w', 'conv_b', 'w_down']
TWIN_DIFF_INPUT = 'x'
TWIN_INPUTS = ['x', 'c', 'w_ada', 'b_ada', 'norm1_g', 'w_in', 'gla_w_gate', 'gla_b_gate', 'gla_norm_g', 'q_norm_g', 'k_norm_g', 'w_out', 'norm2_g', 'w_up', 'conv_w', 'conv_b', 'w_down', 'loss_target', 'm_w_ada', 'm_b_ada', 'm_norm1_g', 'm_w_in', 'm_gla_w_gate', 'm_gla_b_gate', 'm_gla_norm_g', 'm_q_norm_g', 'm_k_norm_g', 'm_w_out', 'm_norm2_g', 'm_w_up', 'm_conv_w', 'm_conv_b', 'm_w_down', 'v_w_ada', 'v_b_ada', 'v_norm1_g', 'v_w_in', 'v_gla_w_gate', 'v_gla_b_gate', 'v_gla_norm_g', 'v_q_norm_g', 'v_k_norm_g', 'v_w_out', 'v_norm2_g', 'v_w_up', 'v_conv_w', 'v_conv_b', 'v_w_down']
TWIN_OUTPUTS = ['loss', 'grad_x', 'grad_w_ada', 'grad_b_ada', 'grad_norm1_g', 'grad_w_in', 'grad_gla_w_gate', 'grad_gla_b_gate', 'grad_gla_norm_g', 'grad_q_norm_g', 'grad_k_norm_g', 'grad_w_out', 'grad_norm2_g', 'grad_w_up', 'grad_conv_w', 'grad_conv_b', 'grad_w_down', 'delta_w_ada', 'delta_b_ada', 'delta_norm1_g', 'delta_w_in', 'delta_gla_w_gate', 'delta_gla_b_gate', 'delta_gla_norm_g', 'delta_q_norm_g', 'delta_k_norm_g', 'delta_w_out', 'delta_norm2_g', 'delta_w_up', 'delta_conv_w', 'delta_conv_b', 'delta_w_down', 'new_m_w_ada', 'new_m_b_ada', 'new_m_norm1_g', 'new_m_w_in', 'new_m_gla_w_gate', 'new_m_gla_b_gate', 'new_m_gla_norm_g', 'new_m_q_norm_g', 'new_m_k_norm_g', 'new_m_w_out', 'new_m_norm2_g', 'new_m_w_up', 'new_m_conv_w', 'new_m_conv_b', 'new_m_w_down', 'new_v_w_ada', 'new_v_b_ada', 'new_v_norm1_g', 'new_v_w_in', 'new_v_gla_w_gate', 'new_v_gla_b_gate', 'new_v_gla_norm_g', 'new_v_q_norm_g', 'new_v_k_norm_g', 'new_v_w_out', 'new_v_norm2_g', 'new_v_w_up', 'new_v_conv_w', 'new_v_conv_b', 'new_v_w_down']
TWIN_LEAF_KINDS = {'loss': 'loss', 'grad_x': 'grad_x', 'grad_w_ada': 'grad_w', 'grad_b_ada': 'grad_w', 'grad_norm1_g': 'grad_w', 'grad_w_in': 'grad_w', 'grad_gla_w_gate': 'grad_w', 'grad_gla_b_gate': 'grad_w', 'grad_gla_norm_g': 'grad_w', 'grad_q_norm_g': 'grad_w', 'grad_k_norm_g': 'grad_w', 'grad_w_out': 'grad_w', 'grad_norm2_g': 'grad_w', 'grad_w_up': 'grad_w', 'grad_conv_w': 'grad_w', 'grad_conv_b': 'grad_w', 'grad_w_down': 'grad_w', 'delta_w_ada': 'delta_w', 'delta_b_ada': 'delta_w', 'delta_norm1_g': 'delta_w', 'delta_w_in': 'delta_w', 'delta_gla_w_gate': 'delta_w', 'delta_gla_b_gate': 'delta_w', 'delta_gla_norm_g': 'delta_w', 'delta_q_norm_g': 'delta_w', 'delta_k_norm_g': 'delta_w', 'delta_w_out': 'delta_w', 'delta_norm2_g': 'delta_w', 'delta_w_up': 'delta_w', 'delta_conv_w': 'delta_w', 'delta_conv_b': 'delta_w', 'delta_w_down': 'delta_w', 'new_m_w_ada': 'new_m', 'new_m_b_ada': 'new_m', 'new_m_norm1_g': 'new_m', 'new_m_w_in': 'new_m', 'new_m_gla_w_gate': 'new_m', 'new_m_gla_b_gate': 'new_m', 'new_m_gla_norm_g': 'new_m', 'new_m_q_norm_g': 'new_m', 'new_m_k_norm_g': 'new_m', 'new_m_w_out': 'new_m', 'new_m_norm2_g': 'new_m', 'new_m_w_up': 'new_m', 'new_m_conv_w': 'new_m', 'new_m_conv_b': 'new_m', 'new_m_w_down': 'new_m', 'new_v_w_ada': 'new_v', 'new_v_b_ada': 'new_v', 'new_v_norm1_g': 'new_v', 'new_v_w_in': 'new_v', 'new_v_gla_w_gate': 'new_v', 'new_v_gla_b_gate': 'new_v', 'new_v_gla_norm_g': 'new_v', 'new_v_q_norm_g': 'new_v', 'new_v_k_norm_g': 'new_v', 'new_v_w_out': 'new_v', 'new_v_norm2_g': 'new_v', 'new_v_w_up': 'new_v', 'new_v_conv_w': 'new_v', 'new_v_conv_b': 'new_v', 'new_v_w_down': 'new_v'}


def _forward(args):
    return _fwd_reference(*[args[k] for k in FWD_PARAMS])


def _output_shape():
    def fwd():
        inp = _fwd_setup_inputs(0)
        return _fwd_reference(*[inp[k] for k in FWD_PARAMS])
    out = _jax.eval_shape(fwd)
    return out.shape, out.dtype

N_MICROBATCH = 1
ADAM_LR = 0.001
ADAM_B1 = 0.9
ADAM_B2 = 0.999
ADAM_EPS = 1e-08
ADAM_WD = 0.01
ADAM_STEP = 10
PER_EXAMPLE_BATCH_AXIS = {'x': 0, 'c': 0, 'loss_target': 0}
SHARED_INPUTS = []
_WEIGHT_DTYPES = {'w_ada': _jnp.float32, 'b_ada': _jnp.float32, 'norm1_g': _jnp.float32, 'w_in': _jnp.float32, 'gla_w_gate': _jnp.float32, 'gla_b_gate': _jnp.float32, 'gla_norm_g': _jnp.float32, 'q_norm_g': _jnp.float32, 'k_norm_g': _jnp.float32, 'w_out': _jnp.float32, 'norm2_g': _jnp.float32, 'w_up': _jnp.float32, 'conv_w': _jnp.float32, 'conv_b': _jnp.float32, 'w_down': _jnp.float32}
MOMENT_SCALE = {'w_ada': 1.373448e+00, 'b_ada': 3.747137e+00, 'norm1_g': 1.673370e+00, 'w_in': 1.414812e-01, 'gla_w_gate': 2.548197e-02, 'gla_b_gate': 7.051496e-02, 'gla_norm_g': 1.011971e+01, 'q_norm_g': 1.176834e+00, 'k_norm_g': 1.174914e+00, 'w_out': 1.567145e-01, 'norm2_g': 6.630490e+00, 'w_up': 1.630962e-01, 'conv_w': 1.008421e+00, 'conv_b': 8.093521e-01, 'w_down': 1.218344e-01}


def _to_microbatches(a, axis):
    t = _jnp.moveaxis(a, axis, 0)
    t = t.reshape((N_MICROBATCH, t.shape[0] // N_MICROBATCH) + t.shape[1:])
    return _jnp.moveaxis(t, 1, axis + 1)


def setup_inputs(seed: int = 0) -> dict:
    inp = _fwd_setup_inputs(seed)
    key = _jax.random.fold_in(_jax.random.key(seed), 7919)
    shape, _ = _output_shape()
    out = dict(inp)
    out["loss_target"] = _jax.random.normal(_jax.random.fold_in(key, 0), shape, _jnp.float32)
    for i, name in enumerate(TWIN_WEIGHTS):
        w = inp[name].astype(_jnp.float32)
        if MOMENT_SCALE is None:
            s = _jnp.sqrt(_jnp.mean(_jnp.square(w)) + 1e-30)
        else:
            s = MOMENT_SCALE[name]
        km, kv = _jax.random.split(_jax.random.fold_in(key, i + 1))
        out[name] = w
        out["m_" + name] = s * _jax.random.normal(km, w.shape, _jnp.float32)
        out["v_" + name] = (s * s) * _jax.random.uniform(kv, w.shape, _jnp.float32, 0.5, 1.5)
    if N_MICROBATCH > 1:
        for name, axis in PER_EXAMPLE_BATCH_AXIS.items():
            out[name] = _to_microbatches(out[name], axis)
    return {'x': out['x'], 'c': out['c'], 'w_ada': out['w_ada'], 'b_ada': out['b_ada'], 'norm1_g': out['norm1_g'], 'w_in': out['w_in'], 'gla_w_gate': out['gla_w_gate'], 'gla_b_gate': out['gla_b_gate'], 'gla_norm_g': out['gla_norm_g'], 'q_norm_g': out['q_norm_g'], 'k_norm_g': out['k_norm_g'], 'w_out': out['w_out'], 'norm2_g': out['norm2_g'], 'w_up': out['w_up'], 'conv_w': out['conv_w'], 'conv_b': out['conv_b'], 'w_down': out['w_down'], 'loss_target': out['loss_target'], 'm_w_ada': out['m_w_ada'], 'm_b_ada': out['m_b_ada'], 'm_norm1_g': out['m_norm1_g'], 'm_w_in': out['m_w_in'], 'm_gla_w_gate': out['m_gla_w_gate'], 'm_gla_b_gate': out['m_gla_b_gate'], 'm_gla_norm_g': out['m_gla_norm_g'], 'm_q_norm_g': out['m_q_norm_g'], 'm_k_norm_g': out['m_k_norm_g'], 'm_w_out': out['m_w_out'], 'm_norm2_g': out['m_norm2_g'], 'm_w_up': out['m_w_up'], 'm_conv_w': out['m_conv_w'], 'm_conv_b': out['m_conv_b'], 'm_w_down': out['m_w_down'], 'v_w_ada': out['v_w_ada'], 'v_b_ada': out['v_b_ada'], 'v_norm1_g': out['v_norm1_g'], 'v_w_in': out['v_w_in'], 'v_gla_w_gate': out['v_gla_w_gate'], 'v_gla_b_gate': out['v_gla_b_gate'], 'v_gla_norm_g': out['v_gla_norm_g'], 'v_q_norm_g': out['v_q_norm_g'], 'v_k_norm_g': out['v_k_norm_g'], 'v_w_out': out['v_w_out'], 'v_norm2_g': out['v_norm2_g'], 'v_w_up': out['v_w_up'], 'v_conv_w': out['v_conv_w'], 'v_conv_b': out['v_conv_b'], 'v_w_down': out['v_w_down']}


def _loss(weights, diff, rest, loss_target):
    with _jax.named_scope("forward"):
        args = {**rest, TWIN_DIFF_INPUT: diff, **{k: w.astype(_WEIGHT_DTYPES[k]) for k, w in weights.items()}}
        y = _forward(args)
    with _jax.named_scope("loss_head"):
        err = _jnp.square(y.astype(_jnp.float32) - loss_target)
        return 0.5 * _jnp.sum(_jnp.mean(err, axis=-1)) if err.ndim else 0.5 * err


def _adamw(w, g, m, v):
    m = ADAM_B1 * m + (1.0 - ADAM_B1) * g
    v = ADAM_B2 * v + (1.0 - ADAM_B2) * _jnp.square(g)
    m_hat = m / (1.0 - ADAM_B1 ** ADAM_STEP)
    v_hat = v / (1.0 - ADAM_B2 ** ADAM_STEP)
    delta = -ADAM_LR * (m_hat / (_jnp.sqrt(v_hat) + ADAM_EPS) + ADAM_WD * w)
    return delta, m, v


def reference(x, c, w_ada, b_ada, norm1_g, w_in, gla_w_gate, gla_b_gate, gla_norm_g, q_norm_g, k_norm_g, w_out, norm2_g, w_up, conv_w, conv_b, w_down, loss_target, m_w_ada, m_b_ada, m_norm1_g, m_w_in, m_gla_w_gate, m_gla_b_gate, m_gla_norm_g, m_q_norm_g, m_k_norm_g, m_w_out, m_norm2_g, m_w_up, m_conv_w, m_conv_b, m_w_down, v_w_ada, v_b_ada, v_norm1_g, v_w_in, v_gla_w_gate, v_gla_b_gate, v_gla_norm_g, v_q_norm_g, v_k_norm_g, v_w_out, v_norm2_g, v_w_up, v_conv_w, v_conv_b, v_w_down):
    given = dict(x=x, c=c, w_ada=w_ada, b_ada=b_ada, norm1_g=norm1_g, w_in=w_in, gla_w_gate=gla_w_gate, gla_b_gate=gla_b_gate, gla_norm_g=gla_norm_g, q_norm_g=q_norm_g, k_norm_g=k_norm_g, w_out=w_out, norm2_g=norm2_g, w_up=w_up, conv_w=conv_w, conv_b=conv_b, w_down=w_down, loss_target=loss_target, m_w_ada=m_w_ada, m_b_ada=m_b_ada, m_norm1_g=m_norm1_g, m_w_in=m_w_in, m_gla_w_gate=m_gla_w_gate, m_gla_b_gate=m_gla_b_gate, m_gla_norm_g=m_gla_norm_g, m_q_norm_g=m_q_norm_g, m_k_norm_g=m_k_norm_g, m_w_out=m_w_out, m_norm2_g=m_norm2_g, m_w_up=m_w_up, m_conv_w=m_conv_w, m_conv_b=m_conv_b, m_w_down=m_w_down, v_w_ada=v_w_ada, v_b_ada=v_b_ada, v_norm1_g=v_norm1_g, v_w_in=v_w_in, v_gla_w_gate=v_gla_w_gate, v_gla_b_gate=v_gla_b_gate, v_gla_norm_g=v_gla_norm_g, v_q_norm_g=v_q_norm_g, v_k_norm_g=v_k_norm_g, v_w_out=v_w_out, v_norm2_g=v_norm2_g, v_w_up=v_w_up, v_conv_w=v_conv_w, v_conv_b=v_conv_b, v_w_down=v_w_down)
    weights = {n: given[n] for n in TWIN_WEIGHTS}
    shared = {n: given[n] for n in SHARED_INPUTS}
    per_example = {n: given[n] for n in ['x', 'c']}
    grad_fn = _jax.value_and_grad(_loss, argnums=(0, 1))

    def one_microbatch(ex, loss_target):
        ex = dict(ex)
        diff = ex.pop(TWIN_DIFF_INPUT)
        return grad_fn(weights, diff, {**shared, **ex}, loss_target)

    if N_MICROBATCH == 1:
        loss, (grad_w, grad_x) = one_microbatch(per_example, given["loss_target"])
    else:
        def body(carry, xs):
            loss_sum, grad_sum = carry
            l_k, (gw_k, gx_k) = one_microbatch(xs[0], xs[1])
            with _jax.named_scope("update"):
                return (loss_sum + l_k, _jax.tree.map(_jnp.add, grad_sum, gw_k)), gx_k

        init = (_jnp.zeros((), _jnp.float32), _jax.tree.map(_jnp.zeros_like, weights))
        (loss, grad_w), grad_x = _jax.lax.scan(body, init, (per_example, given["loss_target"]))
    with _jax.named_scope("update"):
        delta_w, new_m, new_v = {}, {}, {}
        for n in TWIN_WEIGHTS:
            delta_w[n], new_m[n], new_v[n] = _adamw(weights[n], grad_w[n], given["m_" + n], given["v_" + n])
    return (loss, grad_x, *[grad_w[n] for n in TWIN_WEIGHTS], *[delta_w[n] for n in TWIN_WEIGHTS],
            *[new_m[n] for n in TWIN_WEIGHTS], *[new_v[n] for n in TWIN_WEIGHTS])
```

```python
import functools
import math

import jax
import jax.numpy as jnp
from jax import lax
from jax.experimental import pallas as pl
from jax.experimental.pallas import tpu as pltpu

F32, BF16 = jnp.float32, jnp.bfloat16
HI = lax.Precision.HIGHEST
EPS = 1e-6
NEG = -1e30

N_DEV = 8
GLA_HEADS, GLA_DK, GLA_DV, GLA_RANK, GLA_TAU, GLA_CHUNK = 4, 64, 128, 16, 16.0, 64
ATTN_HEADS, ATTN_HD, ATTN_BLOCK = 8, 64, 128
DILATIONS = (1, 4, 16)
GLA_QK, GLA_V, ATTN_DIM = GLA_HEADS * GLA_DK, GLA_HEADS * GLA_DV, ATTN_HEADS * ATTN_HD
O_GQ, O_GK, O_GV, O_GR, O_AQ, O_AK, O_AV, O_GLR = 0, 256, 512, 1024, 1536, 2048, 2560, 3072
PROJ_W = 3200
LANE = 128
GLR_SRC = 2 * GLA_QK + 2 * GLA_V

ADAM_LR, ADAM_B1, ADAM_B2, ADAM_EPS, ADAM_WD, ADAM_STEP = 0.001, 0.9, 0.999, 1e-08, 0.01, 10

VMEM_LIMIT = 56 * 1024 * 1024
SUM_BLOCK_ELEMS = 256 * 1024


def _cp(*sem):
    return pltpu.CompilerParams(dimension_semantics=sem, vmem_limit_bytes=VMEM_LIMIT)


def _dot(a, b, dims, precision=None):
    return lax.dot_general(a, b, (dims, ((), ())), preferred_element_type=F32, precision=precision)


NN, NT, TN = ((1,), (0,)), ((1,), (1,)), ((0,), (0,))


def _sigmoid(z):
    return 1.0 / (1.0 + jnp.exp(-z))


def _exchange(arrays, gather, name):
    n = len(arrays)
    out_shapes = [jax.ShapeDtypeStruct((N_DEV,) + (a.shape if g else a.shape[1:]), a.dtype) for a, g in zip(arrays, gather)]

    def body(*refs):
        ins, outs = refs[:n], refs[n:2 * n]
        send_sems, recv_sems, local_sems = refs[2 * n:]
        x, y, c = lax.axis_index("x"), lax.axis_index("y"), lax.axis_index("c")
        me = 4 * x + 2 * y + c
        copies = []
        for a in range(n):
            for p in range(1, N_DEV):
                px, py, pc = x ^ (p >> 2), y ^ ((p >> 1) & 1), c ^ (p & 1)
                peer = 4 * px + 2 * py + pc
                k = a * (N_DEV - 1) + p - 1
                cp = pltpu.make_async_remote_copy(
                    src_ref=ins[a] if gather[a] else ins[a].at[peer], dst_ref=outs[a].at[me],
                    send_sem=send_sems.at[k], recv_sem=recv_sems.at[k],
                    device_id=(px, py, pc), device_id_type=pl.DeviceIdType.MESH)
                cp.start()
                copies.append(cp)
            own = pltpu.make_async_copy(ins[a] if gather[a] else ins[a].at[me], outs[a].at[me], local_sems.at[a])
            own.start()
            copies.append(own)
        for cp in copies:
            cp.wait()

    anyspec = pl.BlockSpec(memory_space=pl.ANY)
    return pl.pallas_call(
        body, out_shape=out_shapes, in_specs=[anyspec] * n, out_specs=[anyspec] * n,
        scratch_shapes=[pltpu.SemaphoreType.DMA((n * (N_DEV - 1),)), pltpu.SemaphoreType.DMA((n * (N_DEV - 1),)),
                        pltpu.SemaphoreType.DMA((n,))],
        name=name)(*arrays)


def _sum_slots(x, name):
    _, R, C = x.shape
    tr = max(t for t in range(8, min(SUM_BLOCK_ELEMS // C, R) + 1, 8) if R % t == 0)

    def body(x_ref, o_ref):
        acc = x_ref[0]
        for s in range(1, N_DEV):
            acc = acc + x_ref[s]
        o_ref[...] = acc

    return pl.pallas_call(
        body, grid=(R // tr,), in_specs=[pl.BlockSpec((N_DEV, tr, C), lambda i: (0, i, 0))],
        out_specs=pl.BlockSpec((tr, C), lambda i: (i, 0)), out_shape=jax.ShapeDtypeStruct((R, C), x.dtype),
        compiler_params=_cp("parallel"), name=name)(x)


def _mm(a, b, mode, tm, tn, tk, out_dtype, name):
    if mode == NN:
        (M, K), N = a.shape, b.shape[1]
    elif mode == NT:
        (M, K), N = a.shape, b.shape[0]
    else:
        (K, M), N = a.shape, b.shape[1]
    tm, tn, tk = min(tm, M), min(tn, N), min(tk, K)
    assert M % tm == 0 and N % tn == 0 and K % tk == 0, (name, M, N, K, tm, tn, tk)
    nk = K // tk
    if mode == NN:
        a_spec = pl.BlockSpec((tm, tk), lambda i, j, k: (i, k))
        b_spec = pl.BlockSpec((tk, tn), lambda i, j, k: (k, j))
    elif mode == NT:
        a_spec = pl.BlockSpec((tm, tk), lambda i, j, k: (i, k))
        b_spec = pl.BlockSpec((tn, tk), lambda i, j, k: (j, k))
    else:
        a_spec = pl.BlockSpec((tk, tm), lambda i, j, k: (k, i))
        b_spec = pl.BlockSpec((tk, tn), lambda i, j, k: (k, j))

    def body(a_ref, b_ref, o_ref, *acc):
        p = _dot(a_ref[...].astype(BF16), b_ref[...].astype(BF16), mode)
        if nk == 1:
            o_ref[...] = p.astype(out_dtype)
        else:
            acc_ref, = acc
            k = pl.program_id(2)

            @pl.when(k == 0)
            def _():
                acc_ref[...] = p

            @pl.when(k > 0)
            def _():
                acc_ref[...] += p

            @pl.when(k == nk - 1)
            def _():
                o_ref[...] = acc_ref[...].astype(out_dtype)

    return pl.pallas_call(
        body, grid=(M // tm, N // tn, nk), in_specs=[a_spec, b_spec],
        out_specs=pl.BlockSpec((tm, tn), lambda i, j, k: (i, j)),
        out_shape=jax.ShapeDtypeStruct((M, N), out_dtype),
        scratch_shapes=[] if nk == 1 else [pltpu.VMEM((tm, tn), F32)],
        compiler_params=_cp("parallel", "parallel", "arbitrary"), name=name)(a, b)


def _ada_fwd(c_all, w_shard, b_shard):
    Nc = w_shard.shape[1]

    def body(c_ref, w_ref, b_ref, o_ref):
        cv = c_ref[...]
        o_ref[...] = _dot(cv * _sigmoid(cv), w_ref[...], NN, HI) + b_ref[...]

    return pl.pallas_call(body, out_shape=jax.ShapeDtypeStruct((N_DEV, Nc), F32), name="ada_fwd",
                          compiler_params=pltpu.CompilerParams(vmem_limit_bytes=VMEM_LIMIT))(c_all, w_shard, b_shard)


def _ada_bwd(c_all, dmod_shard):
    D, Nc = c_all.shape[1], dmod_shard.shape[1]

    def body(c_ref, d_ref, o_ref):
        cv = c_ref[...]
        o_ref[...] = _dot(cv * _sigmoid(cv), d_ref[...], TN, HI)

    return pl.pallas_call(body, out_shape=jax.ShapeDtypeStruct((D, Nc), F32), name="ada_bwd",
                          compiler_params=pltpu.CompilerParams(vmem_limit_bytes=VMEM_LIMIT))(c_all, dmod_shard)


def _row_spec(ts, D):
    return pl.BlockSpec((ts, D), lambda i: (i, 0))


def _vec_spec(D):
    return pl.BlockSpec((1, D), lambda i: (0, 0))


def _rms_mod(x, ng, sc, sh, ts, name):
    S, D = x.shape

    def body(x_ref, ng_ref, sc_ref, sh_ref, h_ref):
        xv = x_ref[...]
        r = lax.rsqrt(jnp.mean(xv * xv, axis=-1, keepdims=True) + EPS)
        h_ref[...] = (xv * r * ng_ref[...] * (1.0 + sc_ref[...]) + sh_ref[...]).astype(BF16)

    return pl.pallas_call(
        body, grid=(S // ts,), in_specs=[_row_spec(ts, D)] + [_vec_spec(D)] * 3, out_specs=_row_spec(ts, D),
        out_shape=jax.ShapeDtypeStruct((S, D), BF16), compiler_params=_cp("parallel"), name=name)(x, ng, sc, sh)


def _resid_rms_mod(x, t, g, ng, sc, sh, ts, name):
    S, D = x.shape

    def body(x_ref, t_ref, g_ref, ng_ref, sc_ref, sh_ref, x2_ref, h_ref):
        xv = x_ref[...] + g_ref[...] * t_ref[...]
        x2_ref[...] = xv
        r = lax.rsqrt(jnp.mean(xv * xv, axis=-1, keepdims=True) + EPS)
        h_ref[...] = (xv * r * ng_ref[...] * (1.0 + sc_ref[...]) + sh_ref[...]).astype(BF16)

    return pl.pallas_call(
        body, grid=(S // ts,), in_specs=[_row_spec(ts, D)] * 2 + [_vec_spec(D)] * 4,
        out_specs=[_row_spec(ts, D)] * 2,
        out_shape=[jax.ShapeDtypeStruct((S, D), F32), jax.ShapeDtypeStruct((S, D), BF16)],
        compiler_params=_cp("parallel"), name=name)(x, t, g, ng, sc, sh)


def _rms_mod_bwd(dh, xin, dres, ng, sc, ts, name, t_prev=None, g_prev=None):
    S, D = xin.shape
    chain = t_prev is not None

    def body(*refs):
        if chain:
            dh_ref, x_ref, dr_ref, ng_ref, sc_ref, t_ref, g_ref, dx_ref, sums_ref, dt_ref = refs
        else:
            dh_ref, x_ref, dr_ref, ng_ref, sc_ref, dx_ref, sums_ref = refs
        i = pl.program_id(0)
        xv, dhv = x_ref[...], dh_ref[...]
        r = lax.rsqrt(jnp.mean(xv * xv, axis=-1, keepdims=True) + EPS)
        xh = xv * r
        ngv, scv = ng_ref[...], sc_ref[...]
        dxh = dhv * (ngv * (1.0 + scv))
        dx = dr_ref[...] + r * (dxh - xh * jnp.mean(dxh * xh, axis=-1, keepdims=True))
        dx_ref[...] = dx
        dhx = dhv * xh
        rows = [jnp.sum(dhv, axis=0, keepdims=True), jnp.sum(dhx * ngv, axis=0, keepdims=True),
                jnp.sum(dhx * (1.0 + scv), axis=0, keepdims=True)]
        if chain:
            dt_ref[...] = (dx * g_ref[...]).astype(BF16)
            rows.append(jnp.sum(dx * t_ref[...], axis=0, keepdims=True))
        rows.append(jnp.zeros((8 - len(rows), D), F32))
        part = jnp.concatenate(rows, axis=0)

        @pl.when(i == 0)
        def _():
            sums_ref[...] = part

        @pl.when(i > 0)
        def _():
            sums_ref[...] += part

    row, vec = _row_spec(ts, D), _vec_spec(D)
    sums_spec = pl.BlockSpec((8, D), lambda i: (0, 0))
    ins = [dh, xin, dres, ng, sc] + ([t_prev, g_prev] if chain else [])
    in_specs = [row, row, row, vec, vec] + ([row, vec] if chain else [])
    out_specs = [row, sums_spec] + ([row] if chain else [])
    out_shape = [jax.ShapeDtypeStruct((S, D), F32), jax.ShapeDtypeStruct((8, D), F32)] + (
        [jax.ShapeDtypeStruct((S, D), BF16)] if chain else [])
    return pl.pallas_call(body, grid=(S // ts,), in_specs=in_specs, out_specs=out_specs, out_shape=out_shape,
                          compiler_params=_cp("arbitrary"), name=name)(*ins)


def _gate_fwd(proj, wg_p, bg, ts):
    S = proj.shape[0]

    def body(glr_ref, w_ref, b_ref, la_ref):
        z = _dot(glr_ref[...], w_ref[...], NN, HI) + b_ref[...]
        la_ref[...] = (jnp.minimum(z, 0.0) - jnp.log(1.0 + jnp.exp(-jnp.abs(z)))) * (1.0 / GLA_TAU)

    return pl.pallas_call(
        body, grid=(S // ts,),
        in_specs=[pl.BlockSpec((ts, LANE), lambda i: (i, O_GLR // LANE)), pl.BlockSpec((LANE, GLA_QK), lambda i: (0, 0)),
                  pl.BlockSpec((1, GLA_QK), lambda i: (0, 0))],
        out_specs=pl.BlockSpec((ts, GLA_QK), lambda i: (i, 0)), out_shape=jax.ShapeDtypeStruct((S, GLA_QK), F32),
        compiler_params=_cp("parallel"), name="gla_gate_fwd")(proj, wg_p, bg)


def _gate_bwd(dla, la, proj, wg_p, ts):
    S = proj.shape[0]

    def body(dla_ref, la_ref, glr_ref, w_ref, dglr_ref, gw_ref, gb_ref):
        i = pl.program_id(0)
        dz = dla_ref[...] * (1.0 / GLA_TAU) * (1.0 - jnp.exp(GLA_TAU * la_ref[...]))
        dglr_ref[...] = _dot(dz, w_ref[...], NT, HI).astype(BF16)
        gw = _dot(glr_ref[...], dz, TN, HI)
        gb = jnp.concatenate([jnp.sum(dz, axis=0, keepdims=True), jnp.zeros((7, GLA_QK), F32)], axis=0)

        @pl.when(i == 0)
        def _():
            gw_ref[...] = gw
            gb_ref[...] = gb

        @pl.when(i > 0)
        def _():
            gw_ref[...] += gw
            gb_ref[...] += gb

    return pl.pallas_call(
        body, grid=(S // ts,),
        in_specs=[pl.BlockSpec((ts, GLA_QK), lambda i: (i, 0)), pl.BlockSpec((ts, GLA_QK), lambda i: (i, 0)),
                  pl.BlockSpec((ts, LANE), lambda i: (i, O_GLR // LANE)), pl.BlockSpec((LANE, GLA_QK), lambda i: (0, 0))],
        out_specs=[pl.BlockSpec((ts, LANE), lambda i: (i, 0)), pl.BlockSpec((LANE, GLA_QK), lambda i: (0, 0)),
                   pl.BlockSpec((8, GLA_QK), lambda i: (0, 0))],
        out_shape=[jax.ShapeDtypeStruct((S, LANE), BF16), jax.ShapeDtypeStruct((LANE, GLA_QK), F32),
                   jax.ShapeDtypeStruct((8, GLA_QK), F32)],
        compiler_params=_cp("arbitrary"), name="gla_gate_bwd")(dla, la, proj, wg_p)


def _tri(lower):
    r = lax.broadcasted_iota(jnp.int32, (GLA_CHUNK, GLA_CHUNK), 0)
    c = lax.broadcasted_iota(jnp.int32, (GLA_CHUNK, GLA_CHUNK), 1)
    return jnp.where((r >= c) if lower else (c >= r), 1.0, 0.0).astype(F32)


def _gla_fwd(q, k, la, v, tb):
    H, S, _ = q.shape
    C = GLA_CHUNK
    tb = min(tb, S)
    nbc = tb // C
    scale = GLA_DK ** -0.5

    def body(q_ref, k_ref, la_ref, v_ref, o_ref, st_ref, state, b_scr):
        @pl.when(pl.program_id(1) == 0)
        def _():
            state[...] = jnp.zeros_like(state)

        tril = _tri(True)
        rows = lax.broadcasted_iota(jnp.int32, (C, 1), 0)

        def chunk(ci, carry):
            base = pl.multiple_of(ci * C, C)
            sl = pl.ds(base, C)
            qs = q_ref[0, sl, :] * scale
            kc = k_ref[0, sl, :]
            vc = v_ref[0, sl, :]
            b = _dot(tril, la_ref[0, sl, :], NN, HI)
            b_scr[...] = b
            bl = b[C - 1:C, :]
            st = state[...]
            st_ref[0, ci] = st
            o = _dot((qs * jnp.exp(b)).astype(BF16), st.astype(BF16), NT)

            def jstep(j, o):
                kj = k_ref[0, pl.ds(base + j, 1), :]
                vj = v_ref[0, pl.ds(base + j, 1), :]
                bj = b_scr[pl.ds(j, 1), :]
                e = jnp.exp(jnp.where(rows >= j, b - bj, NEG))
                col = jnp.sum(qs * (kj * e), axis=-1, keepdims=True)
                return o + col * vj

            o = lax.fori_loop(0, C, jstep, o)
            o_ref[0, sl, :] = o
            kd = kc * jnp.exp(bl - b)
            state[...] = st * jnp.exp(bl) + _dot(vc.astype(BF16), kd.astype(BF16), TN)
            return carry

        lax.fori_loop(0, nbc, chunk, 0)

    qk_spec = pl.BlockSpec((1, tb, GLA_DK), lambda h, i: (h, i, 0))
    v_spec = pl.BlockSpec((1, tb, GLA_DV), lambda h, i: (h, i, 0))
    return pl.pallas_call(
        body, grid=(H, S // tb), in_specs=[qk_spec, qk_spec, qk_spec, v_spec],
        out_specs=[v_spec, pl.BlockSpec((1, nbc, GLA_DV, GLA_DK), lambda h, i: (h, i, 0, 0))],
        out_shape=[jax.ShapeDtypeStruct((H, S, GLA_DV), F32), jax.ShapeDtypeStruct((H, S // C, GLA_DV, GLA_DK), F32)],
        scratch_shapes=[pltpu.VMEM((GLA_DV, GLA_DK), F32), pltpu.VMEM((C, GLA_DK), F32)],
        compiler_params=_cp("parallel", "arbitrary"), name="gla_fwd")(q, k, la, v)


def _gla_bwd(q, k, la, v, do, states, tb):
    H, S, _ = q.shape
    C = GLA_CHUNK
    tb = min(tb, S)
    nbc = tb // C
    nblk = S // tb
    scale = GLA_DK ** -0.5

    def body(q_ref, k_ref, la_ref, v_ref, do_ref, st_ref, dq_ref, dk_ref, dla_ref, dv_ref, dstate, b_scr, dk_scr, dv_scr):
        @pl.when(pl.program_id(1) == 0)
        def _():
            dstate[...] = jnp.zeros_like(dstate)

        tril, triu = _tri(True), _tri(False)
        rows = lax.broadcasted_iota(jnp.int32, (C, 1), 0)

        def chunk(cc, carry):
            ci = nbc - 1 - cc
            base = pl.multiple_of(ci * C, C)
            sl = pl.ds(base, C)
            qs = q_ref[0, sl, :] * scale
            kc = k_ref[0, sl, :]
            vc = v_ref[0, sl, :]
            doc = do_ref[0, sl, :]
            b = _dot(tril, la_ref[0, sl, :], NN, HI)
            b_scr[...] = b
            bl = b[C - 1:C, :]
            eb = jnp.exp(b)
            ebl_b = jnp.exp(bl - b)
            ebl = jnp.exp(bl)
            st = st_ref[0, ci]
            dst = dstate[...]
            dst_b = dst.astype(BF16)
            dq = _dot(doc.astype(BF16), st.astype(BF16), NN) * eb
            dk = _dot(vc.astype(BF16), dst_b, NN) * ebl_b
            dv = _dot((kc * ebl_b).astype(BF16), dst_b, NT)
            dbl = jnp.sum(dst * st, axis=0, keepdims=True) * ebl + jnp.sum(kc * dk, axis=0, keepdims=True)

            def jstep(j, dq):
                kj = k_ref[0, pl.ds(base + j, 1), :]
                vj = v_ref[0, pl.ds(base + j, 1), :]
                bj = b_scr[pl.ds(j, 1), :]
                e = jnp.exp(jnp.where(rows >= j, b - bj, NEG))
                p = kj * e
                da_col = jnp.sum(doc * vj, axis=-1, keepdims=True)
                a_col = jnp.sum(qs * p, axis=-1, keepdims=True)
                dk_scr[pl.ds(j, 1), :] = jnp.sum(da_col * (qs * e), axis=0, keepdims=True)
                dv_scr[pl.ds(j, 1), :] = jnp.sum(a_col * doc, axis=0, keepdims=True)
                return dq + da_col * p

            dq = lax.fori_loop(0, C, jstep, dq)
            dk = dk + dk_scr[...]
            dv = dv + dv_scr[...]
            db = qs * dq - kc * dk
            db = jnp.where(rows == C - 1, db + dbl, db)
            dq_ref[0, sl, :] = dq * scale
            dk_ref[0, sl, :] = dk
            dv_ref[0, sl, :] = dv
            dla_ref[0, sl, :] = _dot(triu, db, NN, HI)
            dstate[...] = dst * ebl + _dot(doc.astype(BF16), (qs * eb).astype(BF16), TN)
            return carry

        lax.fori_loop(0, nbc, chunk, 0)

    qk_spec = pl.BlockSpec((1, tb, GLA_DK), lambda h, i: (h, nblk - 1 - i, 0))
    v_spec = pl.BlockSpec((1, tb, GLA_DV), lambda h, i: (h, nblk - 1 - i, 0))
    st_spec = pl.BlockSpec((1, nbc, GLA_DV, GLA_DK), lambda h, i: (h, nblk - 1 - i, 0, 0))
    qk_shape, v_shape = jax.ShapeDtypeStruct((H, S, GLA_DK), F32), jax.ShapeDtypeStruct((H, S, GLA_DV), F32)
    return pl.pallas_call(
        body, grid=(H, nblk), in_specs=[qk_spec, qk_spec, qk_spec, v_spec, v_spec, st_spec],
        out_specs=[qk_spec, qk_spec, qk_spec, v_spec], out_shape=[qk_shape, qk_shape, qk_shape, v_shape],
        scratch_shapes=[pltpu.VMEM((GLA_DV, GLA_DK), F32), pltpu.VMEM((C, GLA_DK), F32), pltpu.VMEM((C, GLA_DK), F32),
                        pltpu.VMEM((C, GLA_DV), F32)],
        compiler_params=_cp("parallel", "arbitrary"), name="gla_bwd")(q, k, la, v, do, states)


def _gla_out(o, proj, gng, ts):
    H, S, _ = o.shape

    def body(o_ref, gr_ref, g_ref, y_ref):
        for h in range(H):
            ov = o_ref[h]
            grv = gr_ref[:, h * GLA_DV:(h + 1) * GLA_DV]
            r = lax.rsqrt(jnp.mean(ov * ov, axis=-1, keepdims=True) + EPS)
            y_ref[:, h * GLA_DV:(h + 1) * GLA_DV] = (ov * r * g_ref[...] * (grv * _sigmoid(grv))).astype(BF16)

    return pl.pallas_call(
        body, grid=(S // ts,),
        in_specs=[pl.BlockSpec((H, ts, GLA_DV), lambda i: (0, i, 0)), pl.BlockSpec((ts, GLA_V), lambda i: (i, O_GR // GLA_V)),
                  pl.BlockSpec((1, GLA_DV), lambda i: (0, 0))],
        out_specs=pl.BlockSpec((ts, GLA_V), lambda i: (i, 0)), out_shape=jax.ShapeDtypeStruct((S, GLA_V), BF16),
        compiler_params=_cp("parallel"), name="gla_out_fwd")(o, proj, gng)


def _gla_out_bwd(dmixed, o, proj, gng, ts):
    H, S, _ = o.shape

    def body(dy_ref, o_ref, gr_ref, g_ref, do_ref, dgr_ref, gg_ref):
        i = pl.program_id(0)
        gsum = jnp.zeros((1, GLA_DV), F32)
        for h in range(H):
            cols = slice(h * GLA_DV, (h + 1) * GLA_DV)
            ov, grv, dy = o_ref[h], gr_ref[:, cols], dy_ref[:, cols]
            r = lax.rsqrt(jnp.mean(ov * ov, axis=-1, keepdims=True) + EPS)
            oh = ov * r
            sg = _sigmoid(grv)
            silu = grv * sg
            don = dy * silu
            dgr_ref[:, cols] = (dy * (oh * g_ref[...]) * (sg * (1.0 + grv * (1.0 - sg)))).astype(BF16)
            gsum = gsum + jnp.sum(don * oh, axis=0, keepdims=True)
            doh = don * g_ref[...]
            do_ref[h] = r * (doh - oh * jnp.mean(doh * oh, axis=-1, keepdims=True))
        part = jnp.concatenate([gsum, jnp.zeros((7, GLA_DV), F32)], axis=0)

        @pl.when(i == 0)
        def _():
            gg_ref[...] = part

        @pl.when(i > 0)
        def _():
            gg_ref[...] += part

    return pl.pallas_call(
        body, grid=(S // ts,),
        in_specs=[pl.BlockSpec((ts, GLA_V), lambda i: (i, 0)), pl.BlockSpec((H, ts, GLA_DV), lambda i: (0, i, 0)),
                  pl.BlockSpec((ts, GLA_V), lambda i: (i, O_GR // GLA_V)), pl.BlockSpec((1, GLA_DV), lambda i: (0, 0))],
        out_specs=[pl.BlockSpec((H, ts, GLA_DV), lambda i: (0, i, 0)), pl.BlockSpec((ts, GLA_V), lambda i: (i, 0)),
                   pl.BlockSpec((8, GLA_DV), lambda i: (0, 0))],
        out_shape=[jax.ShapeDtypeStruct((H, S, GLA_DV), F32), jax.ShapeDtypeStruct((S, GLA_V), BF16),
                   jax.ShapeDtypeStruct((8, GLA_DV), F32)],
        compiler_params=_cp("arbitrary"), name="gla_out_bwd")(dmixed, o, proj, gng)


def _seg_matrix(width, seg, value):
    r = lax.broadcasted_iota(jnp.int32, (width, width), 0) // seg
    c = lax.broadcasted_iota(jnp.int32, (width, width), 1) // seg
    return jnp.where(r == c, value, 0.0).astype(F32)


def _head_norm(proj, qg, kg, ts):
    S = proj.shape[0]
    W = ATTN_DIM

    def body(q_ref, k_ref, v_ref, qg_ref, kg_ref, qn_ref, kn_ref, vb_ref):
        seg = _seg_matrix(W, ATTN_HD, 1.0 / ATTN_HD)
        for x_ref, g_ref, o_ref in ((q_ref, qg_ref, qn_ref), (k_ref, kg_ref, kn_ref)):
            xv = x_ref[...]
            ms = _dot(xv * xv, seg, NN, HI)
            o_ref[...] = (xv * lax.rsqrt(ms + EPS) * g_ref[...]).astype(BF16)
        vb_ref[...] = v_ref[...].astype(BF16)

    blk = lambda off: pl.BlockSpec((ts, W), lambda i: (i, off // W))
    out = pl.BlockSpec((ts, W), lambda i: (i, 0))
    vec = pl.BlockSpec((1, W), lambda i: (0, 0))
    return pl.pallas_call(
        body, grid=(S // ts,), in_specs=[blk(O_AQ), blk(O_AK), blk(O_AV), vec, vec], out_specs=[out] * 3,
        out_shape=[jax.ShapeDtypeStruct((S, W), BF16)] * 3, compiler_params=_cp("parallel"), name="attn_head_norm")(
            proj, proj, proj, qg, kg)


def _slope(head):
    one = jnp.ones((1, 1), jnp.int32)
    return 1.0 / jnp.left_shift(one, one * (head + 1)).astype(F32)


def _attn_views(arrs, d):
    S, W = arrs[0].shape
    return [a.reshape(S // d, d * W) for a in arrs]


def _attn_fwd(qn, kn, vb, d):
    S, W = qn.shape
    L = S // d
    nb = L // ATTN_BLOCK
    B = ATTN_BLOCK
    hp_n = W // LANE
    q2, k2, v2 = _attn_views([qn, kn, vb], d)

    def body(q_ref, kp_ref, kc_ref, vp_ref, vc_ref, o_ref, l_ref):
        hp, n = pl.program_id(1), pl.program_id(2)
        lo = lax.broadcasted_iota(jnp.int32, (1, LANE), 1) < ATTN_HD
        qv = q_ref[...]
        kv = jnp.concatenate([kp_ref[...], kc_ref[...]], axis=0)
        vv = jnp.concatenate([vp_ref[...], vc_ref[...]], axis=0)
        iq = lax.broadcasted_iota(jnp.int32, (B, 2 * B), 0)
        ik = lax.broadcasted_iota(jnp.int32, (B, 2 * B), 1)
        rel = iq + B - ik
        valid = (rel >= 0) & (rel <= B) & ((ik >= B) | (n > 0))
        relf = (d * rel).astype(F32)
        outs, lses = [], []
        for h in range(2):
            qm = jnp.where(lo == (h == 0), qv, jnp.zeros_like(qv))
            s = _dot(qm, kv, NT) * (ATTN_HD ** -0.5) - _slope(hp * 2 + h) * relf
            s = jnp.where(valid, s, NEG)
            m = jnp.max(s, axis=-1, keepdims=True)
            p = jnp.exp(s - m)
            den = jnp.sum(p, axis=-1, keepdims=True)
            outs.append(_dot(p.astype(BF16), vv, NN) / den)
            lses.append(m + jnp.log(den))
        o_ref[...] = jnp.where(lo, outs[0], outs[1])
        l_ref[...] = jnp.where(lo, lses[0], lses[1])

    cur = pl.BlockSpec((B, LANE), lambda r, hp, n: (n, r * hp_n + hp))
    prev = pl.BlockSpec((B, LANE), lambda r, hp, n: (jnp.maximum(n - 1, 0), r * hp_n + hp))
    o, l = pl.pallas_call(
        body, grid=(d, hp_n, nb), in_specs=[cur, prev, cur, prev, cur], out_specs=[cur, cur],
        out_shape=[jax.ShapeDtypeStruct((L, d * W), F32)] * 2,
        compiler_params=_cp("parallel", "parallel", "arbitrary"), name=f"attn_fwd_d{d}")(q2, k2, k2, v2, v2)
    return o.reshape(S, W), l.reshape(S, W)


def _attn_merge(os_, ls_, ts):
    S, W = os_[0].shape

    def body(o1, o2, o3, l1, l2, l3, yb_ref, y_ref, lse_ref):
        a, b, c = l1[...], l2[...], l3[...]
        m = jnp.maximum(jnp.maximum(a, b), c)
        ea, eb, ec = jnp.exp(a - m), jnp.exp(b - m), jnp.exp(c - m)
        tot = ea + eb + ec
        y = (ea * o1[...] + eb * o2[...] + ec * o3[...]) / tot
        y_ref[...] = y
        yb_ref[...] = y.astype(BF16)
        lse_ref[...] = m + jnp.log(tot)

    spec = pl.BlockSpec((ts, W), lambda i: (i, 0))
    return pl.pallas_call(
        body, grid=(S // ts,), in_specs=[spec] * 6, out_specs=[spec] * 3,
        out_shape=[jax.ShapeDtypeStruct((S, W), BF16), jax.ShapeDtypeStruct((S, W), F32), jax.ShapeDtypeStruct((S, W), F32)],
        compiler_params=_cp("parallel"), name="attn_merge")(*os_, *ls_)


def _attn_delta(dmixed, y, ts):
    S, W = y.shape

    def body(dy_ref, y_ref, d_ref):
        d_ref[...] = _dot(dy_ref[...] * y_ref[...], _seg_matrix(W, ATTN_HD, 1.0), NN, HI)

    return pl.pallas_call(
        body, grid=(S // ts,), in_specs=[pl.BlockSpec((ts, W), lambda i: (i, 1)), pl.BlockSpec((ts, W), lambda i: (i, 0))],
        out_specs=pl.BlockSpec((ts, W), lambda i: (i, 0)), out_shape=jax.ShapeDtypeStruct((S, W), F32),
        compiler_params=_cp("parallel"), name="attn_delta")(dmixed, y)


def _attn_dq(qn, kn, vb, dmixed, lse, delta, d):
    S, W = qn.shape
    L = S // d
    nb = L // ATTN_BLOCK
    B = ATTN_BLOCK
    hp_n = W // LANE
    q2, k2, v2, l2, de2 = _attn_views([qn, kn, vb, lse, delta], d)
    dy2 = dmixed.reshape(L, d * 2 * W)

    def body(q_ref, kp_ref, kc_ref, vp_ref, vc_ref, dy_ref, l_ref, de_ref, dq_ref):
        hp, n = pl.program_id(1), pl.program_id(2)
        lo = lax.broadcasted_iota(jnp.int32, (1, LANE), 1) < ATTN_HD
        qv, dyv = q_ref[...], dy_ref[...]
        kv = jnp.concatenate([kp_ref[...], kc_ref[...]], axis=0)
        vv = jnp.concatenate([vp_ref[...], vc_ref[...]], axis=0)
        iq = lax.broadcasted_iota(jnp.int32, (B, 2 * B), 0)
        ik = lax.broadcasted_iota(jnp.int32, (B, 2 * B), 1)
        rel = iq + B - ik
        valid = (rel >= 0) & (rel <= B) & ((ik >= B) | (n > 0))
        relf = (d * rel).astype(F32)
        outs = []
        for h in range(2):
            sel = lo == (h == 0)
            qm = jnp.where(sel, qv, jnp.zeros_like(qv))
            dym = jnp.where(sel, dyv, 0.0).astype(BF16)
            lse_h = l_ref[:, h * ATTN_HD:h * ATTN_HD + 1]
            del_h = de_ref[:, h * ATTN_HD:h * ATTN_HD + 1]
            s = _dot(qm, kv, NT) * (ATTN_HD ** -0.5) - _slope(hp * 2 + h) * relf
            p = jnp.exp(jnp.where(valid, s, NEG) - lse_h)
            ds = p * (_dot(dym, vv, NT) - del_h)
            outs.append(_dot(ds.astype(BF16), kv, NN) * (ATTN_HD ** -0.5))
        dq_ref[...] = jnp.where(lo, outs[0], outs[1])

    cur = pl.BlockSpec((B, LANE), lambda r, hp, n: (n, r * hp_n + hp))
    prev = pl.BlockSpec((B, LANE), lambda r, hp, n: (jnp.maximum(n - 1, 0), r * hp_n + hp))
    dy_spec = pl.BlockSpec((B, LANE), lambda r, hp, n: (n, r * 2 * hp_n + hp_n + hp))
    dq = pl.pallas_call(
        body, grid=(d, hp_n, nb), in_specs=[cur, prev, cur, prev, cur, dy_spec, cur, cur], out_specs=cur,
        out_shape=jax.ShapeDtypeStruct((L, d * W), F32),
        compiler_params=_cp("parallel", "parallel", "arbitrary"), name=f"attn_dq_d{d}")(q2, k2, k2, v2, v2, dy2, l2, de2)
    return dq.reshape(S, W)


def _attn_dkv(qn, kn, vb, dmixed, lse, delta, d):
    S, W = qn.shape
    L = S // d
    nb = L // ATTN_BLOCK
    B = ATTN_BLOCK
    hp_n = W // LANE
    q2, k2, v2, l2, de2 = _attn_views([qn, kn, vb, lse, delta], d)
    dy2 = dmixed.reshape(L, d * 2 * W)

    def body(k_ref, v_ref, qc_ref, qn_ref, dyc_ref, dyn_ref, lc_ref, ln_ref, dec_ref, den_ref, dk_ref, dv_ref):
        hp, n = pl.program_id(1), pl.program_id(2)
        lo = lax.broadcasted_iota(jnp.int32, (1, LANE), 1) < ATTN_HD
        kv, vv = k_ref[...], v_ref[...]
        iq = lax.broadcasted_iota(jnp.int32, (B, B), 0)
        ik = lax.broadcasted_iota(jnp.int32, (B, B), 1)
        dk = jnp.zeros((B, LANE), F32)
        dv = jnp.zeros((B, LANE), F32)
        for nxt, q_ref, dy_ref, l_ref, de_ref in ((0, qc_ref, dyc_ref, lc_ref, dec_ref), (1, qn_ref, dyn_ref, ln_ref, den_ref)):
            rel = iq - ik + nxt * B
            valid = (rel >= 0) & (rel <= B)
            if nxt:
                valid = valid & (n + 1 < nb)
            relf = (d * rel).astype(F32)
            qv, dyv = q_ref[...], dy_ref[...]
            for h in range(2):
                sel = lo == (h == 0)
                qm = jnp.where(sel, qv, jnp.zeros_like(qv))
                dym = jnp.where(sel, dyv, 0.0).astype(BF16)
                lse_h = l_ref[:, h * ATTN_HD:h * ATTN_HD + 1]
                del_h = de_ref[:, h * ATTN_HD:h * ATTN_HD + 1]
                s = _dot(qm, kv, NT) * (ATTN_HD ** -0.5) - _slope(hp * 2 + h) * relf
                p = jnp.exp(jnp.where(valid, s, NEG) - lse_h)
                dv = dv + _dot(p.astype(BF16), dym, TN)
                ds = p * (_dot(dym, vv, NT) - del_h)
                dk = dk + _dot(ds.astype(BF16), qm, TN) * (ATTN_HD ** -0.5)
        dk_ref[...] = dk
        dv_ref[...] = dv

    cur = pl.BlockSpec((B, LANE), lambda r, hp, n: (n, r * hp_n + hp))
    nxt = pl.BlockSpec((B, LANE), lambda r, hp, n: (jnp.minimum(n + 1, nb - 1), r * hp_n + hp))
    dy_cur = pl.BlockSpec((B, LANE), lambda r, hp, n: (n, r * 2 * hp_n + hp_n + hp))
    dy_nxt = pl.BlockSpec((B, LANE), lambda r, hp, n: (jnp.minimum(n + 1, nb - 1), r * 2 * hp_n + hp_n + hp))
    dk, dv = pl.pallas_call(
        body, grid=(d, hp_n, nb), in_specs=[cur, cur, cur, nxt, dy_cur, dy_nxt, cur, nxt, cur, nxt], out_specs=[cur, cur],
        out_shape=[jax.ShapeDtypeStruct((L, d * W), F32)] * 2,
        compiler_params=_cp("parallel", "parallel", "arbitrary"), name=f"attn_dkv_d{d}")(
            k2, v2, q2, q2, dy2, dy2, l2, l2, de2, de2)
    return dk.reshape(S, W), dv.reshape(S, W)


def _attn_post(dqs, dks, dvs, proj, qg, kg, ts):
    S = proj.shape[0]
    W = ATTN_DIM

    def body(dq1, dq2, dq3, dk1, dk2, dk3, dv1, dv2, dv3, aq_ref, ak_ref, qg_ref, kg_ref, daq_ref, dak_ref, dav_ref, gg_ref):
        i = pl.program_id(0)
        seg = _seg_matrix(W, ATTN_HD, 1.0 / ATTN_HD)
        gsums = []
        for (d1, d2, d3), x_ref, g_ref, o_ref in (((dq1, dq2, dq3), aq_ref, qg_ref, daq_ref), ((dk1, dk2, dk3), ak_ref, kg_ref, dak_ref)):
            dy = d1[...] + d2[...] + d3[...]
            xv = x_ref[...]
            r = lax.rsqrt(_dot(xv * xv, seg, NN, HI) + EPS)
            xh = xv * r
            dxh = dy * g_ref[...]
            o_ref[...] = (r * (dxh - xh * _dot(dxh * xh, seg, NN, HI))).astype(BF16)
            gsums.append(jnp.sum(dy * xh, axis=0, keepdims=True))
        dav_ref[...] = (dv1[...] + dv2[...] + dv3[...]).astype(BF16)
        part = jnp.concatenate(gsums + [jnp.zeros((6, W), F32)], axis=0)

        @pl.when(i == 0)
        def _():
            gg_ref[...] = part

        @pl.when(i > 0)
        def _():
            gg_ref[...] += part

    row = pl.BlockSpec((ts, W), lambda i: (i, 0))
    blk = lambda off: pl.BlockSpec((ts, W), lambda i: (i, off // W))
    vec = pl.BlockSpec((1, W), lambda i: (0, 0))
    return pl.pallas_call(
        body, grid=(S // ts,), in_specs=[row] * 9 + [blk(O_AQ), blk(O_AK), vec, vec],
        out_specs=[row, row, row, pl.BlockSpec((8, W), lambda i: (0, 0))],
        out_shape=[jax.ShapeDtypeStruct((S, W), BF16)] * 3 + [jax.ShapeDtypeStruct((8, W), F32)],
        compiler_params=_cp("arbitrary"), name="attn_post")(*dqs, *dks, *dvs, proj, proj, qg, kg)


def _shift_down(cur, halo, n):
    ts = cur.shape[0]
    rows = lax.broadcasted_iota(jnp.int32, (ts, 1), 0)
    out = pltpu.roll(cur, n, 0)
    for t in range(n):
        out = jnp.where(rows == t, halo[8 - n + t:8 - n + t + 1, :], out)
    return out


def _shift_up(cur, halo, n):
    ts = cur.shape[0]
    rows = lax.broadcasted_iota(jnp.int32, (ts, 1), 0)
    out = pltpu.roll(cur, ts - n, 0)
    for t in range(n):
        out = jnp.where(rows == ts - n + t, halo[t:t + 1, :], out)
    return out


def _conv(cur, halo, w, b):
    return b + w[0:1, :] * _shift_down(cur, halo, 2) + w[1:2, :] * _shift_down(cur, halo, 1) + w[2:3, :] * cur


def _conv_swiglu(u0, conv_w8, conv_b, ts, tc):
    S, F2 = u0.shape
    F = F2 // 2
    nc = F // tc
    hb = ts // 8

    def body(ug_ref, ugh_ref, uv_ref, uvh_ref, wg_ref, wv_ref, bg_ref, bv_ref, a_ref):
        first = pl.program_id(0) == 0
        ugh = jnp.where(first, 0.0, ugh_ref[...])
        uvh = jnp.where(first, 0.0, uvh_ref[...])
        g = _conv(ug_ref[...], ugh, wg_ref[...], bg_ref[...])
        v = _conv(uv_ref[...], uvh, wv_ref[...], bv_ref[...])
        a_ref[...] = (g * _sigmoid(g) * v).astype(BF16)

    main = lambda off: pl.BlockSpec((ts, tc), lambda i, j: (i, j + off))
    halo = lambda off: pl.BlockSpec((8, tc), lambda i, j: (jnp.maximum(i * hb - 1, 0), j + off))
    wspec = lambda off: pl.BlockSpec((8, tc), lambda i, j: (0, j + off))
    bspec = lambda off: pl.BlockSpec((1, tc), lambda i, j: (0, j + off))
    return pl.pallas_call(
        body, grid=(S // ts, nc),
        in_specs=[main(0), halo(0), main(nc), halo(nc), wspec(0), wspec(nc), bspec(0), bspec(nc)],
        out_specs=pl.BlockSpec((ts, tc), lambda i, j: (i, j)), out_shape=jax.ShapeDtypeStruct((S, F), BF16),
        compiler_params=_cp("parallel", "parallel"), name="conv_swiglu")(u0, u0, u0, u0, conv_w8, conv_w8, conv_b, conv_b)


def _ffn_du(da, u0, conv_w8, conv_b, ts, tc):
    S, F2 = u0.shape
    F = F2 // 2
    nc = F // tc
    hb = ts // 8

    def body(da_ref, uo_ref, uoh_ref, up_ref, uph_ref, wo_ref, wp_ref, bo_ref, bp_ref, du_ref, sums_ref):
        j, i = pl.program_id(0), pl.program_id(1)
        first = i == 0
        uo, uoh = uo_ref[...], jnp.where(first, 0.0, uoh_ref[...])
        s2, s1 = _shift_down(uo, uoh, 2), _shift_down(uo, uoh, 1)
        wo = wo_ref[...]
        own = bo_ref[...] + wo[0:1, :] * s2 + wo[1:2, :] * s1 + wo[2:3, :] * uo
        partner = _conv(up_ref[...], jnp.where(first, 0.0, uph_ref[...]), wp_ref[...], bp_ref[...])
        dav = da_ref[...]
        sg_own, sg_par = _sigmoid(own), _sigmoid(partner)
        du_gate = dav * partner * (sg_own * (1.0 + own * (1.0 - sg_own)))
        du_val = dav * (partner * sg_par)
        du = jnp.where(j < nc, du_gate, du_val)
        du_ref[...] = du
        part = jnp.concatenate([jnp.sum(du * s2, axis=0, keepdims=True), jnp.sum(du * s1, axis=0, keepdims=True),
                                jnp.sum(du * uo, axis=0, keepdims=True), jnp.sum(du, axis=0, keepdims=True),
                                jnp.zeros((4, tc), F32)], axis=0)

        @pl.when(first)
        def _():
            sums_ref[...] = part

        @pl.when(i > 0)
        def _():
            sums_ref[...] += part

    partner = lambda j: (j + nc) % (2 * nc)
    main_o = pl.BlockSpec((ts, tc), lambda j, i: (i, j))
    halo_o = pl.BlockSpec((8, tc), lambda j, i: (jnp.maximum(i * hb - 1, 0), j))
    main_p = pl.BlockSpec((ts, tc), lambda j, i: (i, partner(j)))
    halo_p = pl.BlockSpec((8, tc), lambda j, i: (jnp.maximum(i * hb - 1, 0), partner(j)))
    return pl.pallas_call(
        body, grid=(2 * nc, S // ts),
        in_specs=[pl.BlockSpec((ts, tc), lambda j, i: (i, j % nc)), main_o, halo_o, main_p, halo_p,
                  pl.BlockSpec((8, tc), lambda j, i: (0, j)), pl.BlockSpec((8, tc), lambda j, i: (0, partner(j))),
                  pl.BlockSpec((1, tc), lambda j, i: (0, j)), pl.BlockSpec((1, tc), lambda j, i: (0, partner(j)))],
        out_specs=[main_o, pl.BlockSpec((8, tc), lambda j, i: (0, j))],
        out_shape=[jax.ShapeDtypeStruct((S, F2), F32), jax.ShapeDtypeStruct((8, F2), F32)],
        compiler_params=_cp("parallel", "arbitrary"), name="ffn_du")(da, u0, u0, u0, u0, conv_w8, conv_w8, conv_b, conv_b)


def _ffn_du0(du, conv_w8, ts, tc):
    S, F2 = du.shape
    hb = ts // 8
    nrow = S // ts

    def body(du_ref, duh_ref, w_ref, o_ref):
        last = pl.program_id(0) == nrow - 1
        cur, halo, w = du_ref[...], jnp.where(last, 0.0, duh_ref[...]), w_ref[...]
        o_ref[...] = (w[2:3, :] * cur + w[1:2, :] * _shift_up(cur, halo, 1) + w[0:1, :] * _shift_up(cur, halo, 2)).astype(BF16)

    return pl.pallas_call(
        body, grid=(nrow, F2 // tc),
        in_specs=[pl.BlockSpec((ts, tc), lambda i, j: (i, j)),
                  pl.BlockSpec((8, tc), lambda i, j: (jnp.minimum((i + 1) * hb, S // 8 - 1), j)),
                  pl.BlockSpec((8, tc), lambda i, j: (0, j))],
        out_specs=pl.BlockSpec((ts, tc), lambda i, j: (i, j)), out_shape=jax.ShapeDtypeStruct((S, F2), BF16),
        compiler_params=_cp("parallel", "parallel"), name="ffn_du0")(du, du, conv_w8)


def _loss_resid(x2, t2, g2, target, ts):
    S, D = x2.shape

    def body(x_ref, t_ref, g_ref, y_ref, dx_ref, dt_ref, sums_ref):
        i = pl.program_id(0)
        tv, gv = t_ref[...], g_ref[...]
        e = x_ref[...] + gv * tv - y_ref[...]
        dx = e * (1.0 / D)
        dx_ref[...] = dx
        dt_ref[...] = (dx * gv).astype(BF16)
        part = jnp.concatenate([jnp.sum(e * e, axis=0, keepdims=True), jnp.sum(dx * tv, axis=0, keepdims=True),
                                jnp.zeros((6, D), F32)], axis=0)

        @pl.when(i == 0)
        def _():
            sums_ref[...] = part

        @pl.when(i > 0)
        def _():
            sums_ref[...] += part

    row, vec = _row_spec(ts, D), _vec_spec(D)
    return pl.pallas_call(
        body, grid=(S // ts,), in_specs=[row, row, vec, row], out_specs=[row, row, pl.BlockSpec((8, D), lambda i: (0, 0))],
        out_shape=[jax.ShapeDtypeStruct((S, D), F32), jax.ShapeDtypeStruct((S, D), BF16), jax.ShapeDtypeStruct((8, D), F32)],
        compiler_params=_cp("arbitrary"), name="loss_resid")(x2, t2, g2, target)


def _adamw(w, g, m, v, name):
    shape = w.shape
    n = math.prod(shape)
    view = (n // LANE, LANE) if n % LANE == 0 else (math.prod(shape[:-1]), shape[-1])
    R, C = view
    tr = R
    for cand in (1024, 512, 256):
        if R > cand and R % cand == 0:
            tr = cand
            break

    def body(w_ref, g_ref, m_ref, v_ref, d_ref, nm_ref, nv_ref):
        gv = g_ref[...]
        nm = ADAM_B1 * m_ref[...] + (1.0 - ADAM_B1) * gv
        nv = ADAM_B2 * v_ref[...] + (1.0 - ADAM_B2) * (gv * gv)
        m_hat = nm / (1.0 - ADAM_B1 ** ADAM_STEP)
        v_hat = nv / (1.0 - ADAM_B2 ** ADAM_STEP)
        d_ref[...] = -ADAM_LR * (m_hat / (jnp.sqrt(v_hat) + ADAM_EPS) + ADAM_WD * w_ref[...])
        nm_ref[...] = nm
        nv_ref[...] = nv

    spec = pl.BlockSpec((tr, C), lambda i: (i, 0))
    outs = pl.pallas_call(
        body, grid=(R // tr,), in_specs=[spec] * 4, out_specs=[spec] * 3, out_shape=[jax.ShapeDtypeStruct(view, F32)] * 3,
        compiler_params=_cp("parallel"), name=name)(*[a.reshape(view) for a in (w, g, m, v)])
    return [o.reshape(shape) for o in outs]


def _pad_rows8(a):
    return jnp.concatenate([a, jnp.zeros((8 - a.shape[0], a.shape[1]), a.dtype)], axis=0)


def _to_heads(a, heads):
    S, W = a.shape
    return a.reshape(S, heads, W // heads).transpose(1, 0, 2)


def _from_heads(a):
    H, S, E = a.shape
    return a.transpose(1, 0, 2).reshape(S, H * E)


def _local_step(x, target, mod, n1g, w_in_p, wg_p, bg, gng, qng, kng, w_out, n2g, w_up, conv_w, conv_b, w_down):
    S, D = x.shape
    F = w_down.shape[0]
    ts = min(512, S)
    sh1, sc1, g1, sh2, sc2, g2 = [mod[i:i + 1] for i in range(6)]
    conv_w8 = _pad_rows8(conv_w)
    qg_t, kg_t = jnp.tile(qng, (1, ATTN_HEADS)), jnp.tile(kng, (1, ATTN_HEADS))

    h1 = _rms_mod(x, n1g, sc1, sh1, ts, "rms_mod1")
    proj = _mm(h1, w_in_p, NN, 512, 640, 1024, F32, "mm_in")
    la = _gate_fwd(proj, wg_p, bg, ts)
    gq, gk, gv = proj[:, O_GQ:O_GK], proj[:, O_GK:O_GV], proj[:, O_GV:O_GR]
    qh, kh, lah, vh = _to_heads(gq, GLA_HEADS), _to_heads(gk, GLA_HEADS), _to_heads(la, GLA_HEADS), _to_heads(gv, GLA_HEADS)
    o_gla, states = _gla_fwd(qh, kh, lah, vh, 512)
    y_gla = _gla_out(o_gla, proj, gng, ts)
    qn, kn, vb = _head_norm(proj, qg_t, kg_t, ts)
    branches = [_attn_fwd(qn, kn, vb, d) for d in DILATIONS]
    y_att_b, y_att, lse = _attn_merge([b[0] for b in branches], [b[1] for b in branches], ts)
    mixed = jnp.concatenate([y_gla, y_att_b], axis=1)
    t1 = _mm(mixed, w_out, NN, 512, 1024, 1024, F32, "mm_out")
    x2, h2 = _resid_rms_mod(x, t1, g1, n2g, sc2, sh2, ts, "resid_rms_mod2")
    u0 = _mm(h2, w_up, NN, 512, 1408, 1024, F32, "mm_up")
    tc = 1408 if F % 1408 == 0 else F
    a = _conv_swiglu(u0, conv_w8, conv_b, min(256, S), tc)
    t2 = _mm(a, w_down, NN, 512, 1024, F, F32, "mm_down")
    dx3, dt2, sums3 = _loss_resid(x2, t2, g2, target, ts)
    loss_row, dg2 = sums3[0:1], sums3[1:2]

    g_w_down = _mm(a, dt2, TN, 1408, 1024, 512, F32, "mm_gw_down")
    da = _mm(dt2, w_down, NT, 512, 1408, 1024, F32, "mm_da")
    du, conv_sums = _ffn_du(da, u0, conv_w8, conv_b, min(256, S), tc)
    g_conv_w, g_conv_b = conv_sums[0:3], conv_sums[3:4]
    du0 = _ffn_du0(du, conv_w8, min(256, S), tc)
    g_w_up = _mm(h2, du0, TN, 1024, 1408, 512, F32, "mm_gw_up")
    dh2 = _mm(du0, w_up, NT, 512, 1024, 1408, F32, "mm_dh2")
    dx2, sums2, dt1 = _rms_mod_bwd(dh2, x2, dx3, n2g, sc2, ts, "rms_mod_bwd2", t_prev=t1, g_prev=g1)
    dsh2, dsc2, g_n2g, dg1 = sums2[0:1], sums2[1:2], sums2[2:3], sums2[3:4]
    g_w_out = _mm(mixed, dt1, TN, 1024, 1024, 512, F32, "mm_gw_out")
    dmixed = _mm(dt1, w_out, NT, 512, 1024, 1024, F32, "mm_dmixed")
    do_gla, dgr, gng_sums = _gla_out_bwd(dmixed, o_gla, proj, gng, ts)
    dqh, dkh, dlah, dvh = _gla_bwd(qh, kh, lah, vh, do_gla, states, 512)
    dglr, g_wg_p, gb_sums = _gate_bwd(_from_heads(dlah), la, proj, wg_p, ts)
    delta = _attn_delta(dmixed, y_att, ts)
    dqs = [_attn_dq(qn, kn, vb, dmixed, lse, delta, d) for d in DILATIONS]
    dkvs = [_attn_dkv(qn, kn, vb, dmixed, lse, delta, d) for d in DILATIONS]
    daq, dak, dav, qk_sums = _attn_post(dqs, [t[0] for t in dkvs], [t[1] for t in dkvs], proj, qg_t, kg_t, ts)
    dproj = jnp.concatenate([_from_heads(dqh).astype(BF16), _from_heads(dkh).astype(BF16), _from_heads(dvh).astype(BF16),
                             dgr, daq, dak, dav, dglr], axis=1)
    g_w_in_p = _mm(h1, dproj, TN, 1024, 640, 512, F32, "mm_gw_in")
    dh1 = _mm(dproj, w_in_p, NT, 512, 1024, 640, F32, "mm_dh1")
    dx, sums1 = _rms_mod_bwd(dh1, x, dx2, n1g, sc1, ts, "rms_mod_bwd1")
    dsh1, dsc1, g_n1g = sums1[0:1], sums1[1:2], sums1[2:3]

    dmod = jnp.concatenate([dsh1, dsc1, dg1, dsh2, dsc2, dg2], axis=1)
    grads = dict(n1g=g_n1g, w_in_p=g_w_in_p, wg=g_wg_p[:GLA_RANK], bg=gb_sums[0:1], gng=gng_sums[0:1],
                 qng_lanes=qk_sums[0:1], kng_lanes=qk_sums[1:2], w_out=g_w_out, n2g=g_n2g, w_up=g_w_up,
                 conv_w=g_conv_w, conv_b=g_conv_b, w_down=g_w_down)
    return loss_row, dx, dmod, grads


def _dense(a):
    *lead, R, C = a.shape
    return a.reshape(*lead, R * C // LANE, LANE)


def _col_blocks(a):
    R, W = a.shape
    return a.reshape(R, N_DEV, W // N_DEV).transpose(1, 0, 2)


def _cols_from_blocks(a):
    n, R, C = a.shape
    return a.transpose(1, 0, 2).reshape(R, n * C)


def kernel(x, c, w_ada, b_ada, norm1_g, w_in, gla_w_gate, gla_b_gate, gla_norm_g, q_norm_g, k_norm_g, w_out, norm2_g, w_up, conv_w, conv_b, w_down, loss_target, m_w_ada, m_b_ada, m_norm1_g, m_w_in, m_gla_w_gate, m_gla_b_gate, m_gla_norm_g, m_q_norm_g, m_k_norm_g, m_w_out, m_norm2_g, m_w_up, m_conv_w, m_conv_b, m_w_down, v_w_ada, v_b_ada, v_norm1_g, v_w_in, v_gla_w_gate, v_gla_b_gate, v_gla_norm_g, v_q_norm_g, v_k_norm_g, v_w_out, v_norm2_g, v_w_up, v_conv_w, v_conv_b, v_w_down):
    axes = ("x", "y", "c")
    me = 4 * lax.axis_index("x") + 2 * lax.axis_index("y") + lax.axis_index("c")
    S, D = x.shape[1], x.shape[2]
    x2d, tgt2d = x[0], loss_target[0]
    w_in_s, w_out_s, w_up_s, w_down_s, w_ada_s = w_in[0], w_out[0], w_up[0], w_down[0], w_ada[0]
    conv_w_s, wg_s = conv_w[0], gla_w_gate[0]
    in_c, up_c, ada_c, wg_c, cw_c = w_in_s.shape[1], w_up_s.shape[1], w_ada_s.shape[1], wg_s.shape[1], conv_w_s.shape[1]
    F = w_down_s.shape[0] * N_DEV

    small = jnp.concatenate([conv_w_s.reshape(1, -1), wg_s.reshape(1, -1)], axis=1)
    n_small = small.shape[1]
    small = jnp.pad(small, ((0, 0), (0, -n_small % LANE)))
    g_c, g_in, g_out, g_up, g_down, g_small = _exchange(
        [c, _dense(w_in_s.astype(BF16)), w_out_s.astype(BF16), _dense(w_up_s.astype(BF16)), w_down_s.astype(BF16), small],
        [True] * 6, "gather_weights")
    c_all = g_c.reshape(N_DEV, D)
    w_in_full = _cols_from_blocks(g_in.reshape(N_DEV, D, in_c))
    w_in_p = jnp.concatenate([w_in_full[:, :GLR_SRC], w_in_full[:, GLR_SRC + GLA_RANK:],
                              w_in_full[:, GLR_SRC:GLR_SRC + GLA_RANK], jnp.zeros((D, LANE - GLA_RANK), BF16)], axis=1)
    w_out_full = g_out.reshape(N_DEV * w_out_s.shape[0], D)
    w_up_full = _cols_from_blocks(g_up.reshape(N_DEV, D, up_c))
    w_down_full = g_down.reshape(F, D)
    g_small = g_small.reshape(N_DEV, -1)
    conv_w_full = _cols_from_blocks(g_small[:, :3 * cw_c].reshape(N_DEV, 3, cw_c))
    wg_full = _cols_from_blocks(g_small[:, 3 * cw_c:n_small].reshape(N_DEV, GLA_RANK, wg_c))
    wg_p = jnp.concatenate([wg_full, jnp.zeros((LANE - GLA_RANK, wg_full.shape[1]), F32)], axis=0)

    b_shard = lax.dynamic_slice(b_ada, (0, me * ada_c), (1, ada_c))
    mod_part = _ada_fwd(c_all, w_ada_s, b_shard)
    mod_recv, = _exchange([mod_part.reshape(N_DEV, 1, ada_c)], [False], "exchange_mod")
    mod = mod_recv.reshape(6, D)

    loss_row, dx, dmod, gr = _local_step(
        x2d, tgt2d, mod, norm1_g, w_in_p, wg_p, gla_b_gate, gla_norm_g, q_norm_g, k_norm_g,
        w_out_full, norm2_g, w_up_full, conv_w_full, conv_b, w_down_full)
    loss = lax.psum(0.5 / D * jnp.sum(loss_row), axes)

    parts = [dmod, gr["n1g"], gr["bg"], gr["gng"], gr["qng_lanes"], gr["kng_lanes"], gr["n2g"], gr["conv_b"],
             gr["wg"].reshape(1, -1), gr["conv_w"].reshape(1, -1)]
    sizes = [p.shape[1] for p in parts]
    packed = jnp.concatenate(parts, axis=1)
    packed = jnp.pad(packed, ((0, 0), (0, -packed.shape[1] % (8 * LANE))))
    gathered, = _exchange([packed.reshape(8, -1)], [True], "gather_small_grads")
    gathered = gathered.reshape(N_DEV, -1)
    total = _sum_slots(gathered.reshape(N_DEV, 8, -1), "sum_small_grads").reshape(1, -1)
    offs = [0]
    for s_ in sizes:
        offs.append(offs[-1] + s_)
    t_dmod, t_n1g, t_bg, t_gng, t_qng, t_kng, t_n2g, t_conv_b, t_wg, t_conv_w = [
        total[:, offs[i]:offs[i + 1]] for i in range(len(sizes))]
    g_b_ada = t_dmod
    g_qng = t_qng.reshape(ATTN_HEADS, ATTN_HD).sum(axis=0, keepdims=True)
    g_kng = t_kng.reshape(ATTN_HEADS, ATTN_HD).sum(axis=0, keepdims=True)
    g_wg = lax.dynamic_slice(t_wg.reshape(GLA_RANK, -1), (0, me * wg_c), (GLA_RANK, wg_c))
    g_conv_w = lax.dynamic_slice(t_conv_w.reshape(3, -1), (0, me * cw_c), (3, cw_c))
    dmod_shard = lax.dynamic_slice(gathered[:, :6 * D], (0, me * ada_c), (N_DEV, ada_c))
    g_w_ada = _ada_bwd(c_all, dmod_shard)

    gw_in_p = gr["w_in_p"]
    gw_in = jnp.concatenate([gw_in_p[:, :GLR_SRC], gw_in_p[:, O_GLR:O_GLR + GLA_RANK], gw_in_p[:, GLR_SRC:O_GLR]], axis=1)
    r_in, r_out, r_up, r_down = _exchange(
        [_dense(_col_blocks(gw_in)), gr["w_out"].reshape(N_DEV, -1, D), _dense(_col_blocks(gr["w_up"])),
         gr["w_down"].reshape(N_DEV, -1, D)], [False] * 4, "exchange_big_grads")
    g_w_in = _sum_slots(r_in, "sum_gw_in").reshape(D, in_c)
    g_w_out = _sum_slots(r_out, "sum_gw_out")
    g_w_up = _sum_slots(r_up, "sum_gw_up").reshape(D, up_c)
    g_w_down = _sum_slots(r_down, "sum_gw_down")

    names = ["w_ada", "b_ada", "norm1_g", "w_in", "gla_w_gate", "gla_b_gate", "gla_norm_g", "q_norm_g", "k_norm_g",
             "w_out", "norm2_g", "w_up", "conv_w", "conv_b", "w_down"]
    ws = [w_ada, b_ada, norm1_g, w_in, gla_w_gate, gla_b_gate, gla_norm_g, q_norm_g, k_norm_g, w_out, norm2_g, w_up, conv_w, conv_b, w_down]
    ms = [m_w_ada, m_b_ada, m_norm1_g, m_w_in, m_gla_w_gate, m_gla_b_gate, m_gla_norm_g, m_q_norm_g, m_k_norm_g, m_w_out, m_norm2_g, m_w_up, m_conv_w, m_conv_b, m_w_down]
    vs = [v_w_ada, v_b_ada, v_norm1_g, v_w_in, v_gla_w_gate, v_gla_b_gate, v_gla_norm_g, v_q_norm_g, v_k_norm_g, v_w_out, v_norm2_g, v_w_up, v_conv_w, v_conv_b, v_w_down]
    gs = [g_w_ada, g_b_ada, t_n1g, g_w_in, g_wg, t_bg, t_gng, g_qng, g_kng, g_w_out, t_n2g, g_w_up, g_conv_w, t_conv_b, g_w_down]
    gs = [g.reshape(w.shape) for g, w in zip(gs, ws)]
    deltas, new_ms, new_vs = [], [], []
    for nm, w, g, m, v in zip(names, ws, gs, ms, vs):
        d_, m_, v_ = _adamw(w, g, m, v, "adamw_" + nm)
        deltas.append(d_)
        new_ms.append(m_)
        new_vs.append(v_)
    return (loss, dx.reshape(x.shape), *gs, *deltas, *new_ms, *new_vs)
```

```python
import functools
import math

import jax
import jax.numpy as jnp
from jax import lax
from jax.experimental import pallas as pl
from jax.experimental.pallas import tpu as pltpu

F32, BF16 = jnp.float32, jnp.bfloat16
HI = lax.Precision.HIGHEST
EPS = 1e-6
NEG = -1e30

N_DEV = 8
GLA_HEADS, GLA_DK, GLA_DV, GLA_RANK, GLA_TAU, GLA_CHUNK = 4, 64, 128, 16, 16.0, 64
ATTN_HEADS, ATTN_HD, ATTN_BLOCK = 8, 64, 128
DILATIONS = (1, 4, 16)
GLA_QK, GLA_V, ATTN_DIM = GLA_HEADS * GLA_DK, GLA_HEADS * GLA_DV, ATTN_HEADS * ATTN_HD
O_GQ, O_GK, O_GV, O_GR, O_AQ, O_AK, O_AV, O_GLR = 0, 256, 512, 1024, 1536, 2048, 2560, 3072
PROJ_W = 3200
LANE = 128
GLR_SRC = 2 * GLA_QK + 2 * GLA_V

ADAM_LR, ADAM_B1, ADAM_B2, ADAM_EPS, ADAM_WD, ADAM_STEP = 0.001, 0.9, 0.999, 1e-08, 0.01, 10

VMEM_LIMIT = 56 * 1024 * 1024
SUM_BLOCK_ELEMS = 256 * 1024


def _cp(*sem):
    return pltpu.CompilerParams(dimension_semantics=sem, vmem_limit_bytes=VMEM_LIMIT)


def _dot(a, b, dims, precision=None):
    return lax.dot_general(a, b, (dims, ((), ())), preferred_element_type=F32, precision=precision)


NN, NT, TN = ((1,), (0,)), ((1,), (1,)), ((0,), (0,))


def _sigmoid(z):
    return 1.0 / (1.0 + jnp.exp(-z))


def _exchange(arrays, gather, name):
    n = len(arrays)
    out_shapes = [jax.ShapeDtypeStruct((N_DEV,) + (a.shape if g else a.shape[1:]), a.dtype) for a, g in zip(arrays, gather)]

    def body(*refs):
        ins, outs = refs[:n], refs[n:2 * n]
        send_sems, recv_sems, local_sems = refs[2 * n:]
        x, y, c = lax.axis_index("x"), lax.axis_index("y"), lax.axis_index("c")
        me = 4 * x + 2 * y + c
        copies = []
        for a in range(n):
            for p in range(1, N_DEV):
                px, py, pc = x ^ (p >> 2), y ^ ((p >> 1) & 1), c ^ (p & 1)
                peer = 4 * px + 2 * py + pc
                k = a * (N_DEV - 1) + p - 1
                cp = pltpu.make_async_remote_copy(
                    src_ref=ins[a] if gather[a] else ins[a].at[peer], dst_ref=outs[a].at[me],
                    send_sem=send_sems.at[k], recv_sem=recv_sems.at[k],
                    device_id=(px, py, pc), device_id_type=pl.DeviceIdType.MESH)
                cp.start()
                copies.append(cp)
            own = pltpu.make_async_copy(ins[a] if gather[a] else ins[a].at[me], outs[a].at[me], local_sems.at[a])
            own.start()
            copies.append(own)
        for cp in copies:
            cp.wait()

    anyspec = pl.BlockSpec(memory_space=pl.ANY)
    return pl.pallas_call(
        body, out_shape=out_shapes, in_specs=[anyspec] * n, out_specs=[anyspec] * n,
        scratch_shapes=[pltpu.SemaphoreType.DMA((n * (N_DEV - 1),)), pltpu.SemaphoreType.DMA((n * (N_DEV - 1),)),
                        pltpu.SemaphoreType.DMA((n,))],
        name=name)(*arrays)


def _sum_slots(x, name):
    _, R, C = x.shape
    tr = max(t for t in range(8, min(SUM_BLOCK_ELEMS // C, R) + 1, 8) if R % t == 0)

    def body(x_ref, o_ref):
        acc = x_ref[0]
        for s in range(1, N_DEV):
            acc = acc + x_ref[s]
        o_ref[...] = acc

    return pl.pallas_call(
        body, grid=(R // tr,), in_specs=[pl.BlockSpec((N_DEV, tr, C), lambda i: (0, i, 0))],
        out_specs=pl.BlockSpec((tr, C), lambda i: (i, 0)), out_shape=jax.ShapeDtypeStruct((R, C), x.dtype),
        compiler_params=_cp("parallel"), name=name)(x)


def _mm(a, b, mode, tm, tn, tk, out_dtype, name):
    if mode == NN:
        (M, K), N = a.shape, b.shape[1]
    elif mode == NT:
        (M, K), N = a.shape, b.shape[0]
    else:
        (K, M), N = a.shape, b.shape[1]
    tm, tn, tk = min(tm, M), min(tn, N), min(tk, K)
    assert M % tm == 0 and N % tn == 0 and K % tk == 0, (name, M, N, K, tm, tn, tk)
    nk = K // tk
    if mode == NN:
        a_spec = pl.BlockSpec((tm, tk), lambda i, j, k: (i, k))
        b_spec = pl.BlockSpec((tk, tn), lambda i, j, k: (k, j))
    elif mode == NT:
        a_spec = pl.BlockSpec((tm, tk), lambda i, j, k: (i, k))
        b_spec = pl.BlockSpec((tn, tk), lambda i, j, k: (j, k))
    else:
        a_spec = pl.BlockSpec((tk, tm), lambda i, j, k: (k, i))
        b_spec = pl.BlockSpec((tk, tn), lambda i, j, k: (k, j))

    def body(a_ref, b_ref, o_ref, *acc):
        p = _dot(a_ref[...].astype(BF16), b_ref[...].astype(BF16), mode)
        if nk == 1:
            o_ref[...] = p.astype(out_dtype)
        else:
            acc_ref, = acc
            k = pl.program_id(2)

            @pl.when(k == 0)
            def _():
                acc_ref[...] = p

            @pl.when(k > 0)
            def _():
                acc_ref[...] += p

            @pl.when(k == nk - 1)
            def _():
                o_ref[...] = acc_ref[...].astype(out_dtype)

    return pl.pallas_call(
        body, grid=(M // tm, N // tn, nk), in_specs=[a_spec, b_spec],
        out_specs=pl.BlockSpec((tm, tn), lambda i, j, k: (i, j)),
        out_shape=jax.ShapeDtypeStruct((M, N), out_dtype),
        scratch_shapes=[] if nk == 1 else [pltpu.VMEM((tm, tn), F32)],
        compiler_params=_cp("parallel", "parallel", "arbitrary"), name=name)(a, b)


def _ada_fwd(c_all, w_shard, b_shard):
    Nc = w_shard.shape[1]

    def body(c_ref, w_ref, b_ref, o_ref):
        cv = c_ref[...]
        o_ref[...] = _dot(cv * _sigmoid(cv), w_ref[...], NN, HI) + b_ref[...]

    return pl.pallas_call(body, out_shape=jax.ShapeDtypeStruct((N_DEV, Nc), F32), name="ada_fwd",
                          compiler_params=pltpu.CompilerParams(vmem_limit_bytes=VMEM_LIMIT))(c_all, w_shard, b_shard)


def _ada_bwd(c_all, dmod_shard):
    D, Nc = c_all.shape[1], dmod_shard.shape[1]

    def body(c_ref, d_ref, o_ref):
        cv = c_ref[...]
        o_ref[...] = _dot(cv * _sigmoid(cv), d_ref[...], TN, HI)

    return pl.pallas_call(body, out_shape=jax.ShapeDtypeStruct((D, Nc), F32), name="ada_bwd",
                          compiler_params=pltpu.CompilerParams(vmem_limit_bytes=VMEM_LIMIT))(c_all, dmod_shard)


def _row_spec(ts, D):
    return pl.BlockSpec((ts, D), lambda i: (i, 0))


def _vec_spec(D):
    return pl.BlockSpec((1, D), lambda i: (0, 0))


def _rms_mod(x, ng, sc, sh, ts, name):
    S, D = x.shape

    def body(x_ref, ng_ref, sc_ref, sh_ref, h_ref):
        xv = x_ref[...]
        r = lax.rsqrt(jnp.mean(xv * xv, axis=-1, keepdims=True) + EPS)
        h_ref[...] = (xv * r * ng_ref[...] * (1.0 + sc_ref[...]) + sh_ref[...]).astype(BF16)

    return pl.pallas_call(
        body, grid=(S // ts,), in_specs=[_row_spec(ts, D)] + [_vec_spec(D)] * 3, out_specs=_row_spec(ts, D),
        out_shape=jax.ShapeDtypeStruct((S, D), BF16), compiler_params=_cp("parallel"), name=name)(x, ng, sc, sh)


def _resid_rms_mod(x, t, g, ng, sc, sh, ts, name):
    S, D = x.shape

    def body(x_ref, t_ref, g_ref, ng_ref, sc_ref, sh_ref, x2_ref, h_ref):
        xv = x_ref[...] + g_ref[...] * t_ref[...]
        x2_ref[...] = xv
        r = lax.rsqrt(jnp.mean(xv * xv, axis=-1, keepdims=True) + EPS)
        h_ref[...] = (xv * r * ng_ref[...] * (1.0 + sc_ref[...]) + sh_ref[...]).astype(BF16)

    return pl.pallas_call(
        body, grid=(S // ts,), in_specs=[_row_spec(ts, D)] * 2 + [_vec_spec(D)] * 4,
        out_specs=[_row_spec(ts, D)] * 2,
        out_shape=[jax.ShapeDtypeStruct((S, D), F32), jax.ShapeDtypeStruct((S, D), BF16)],
        compiler_params=_cp("parallel"), name=name)(x, t, g, ng, sc, sh)


def _rms_mod_bwd(dh, xin, dres, ng, sc, ts, name, t_prev=None, g_prev=None):
    S, D = xin.shape
    chain = t_prev is not None

    def body(*refs):
        if chain:
            dh_ref, x_ref, dr_ref, ng_ref, sc_ref, t_ref, g_ref, dx_ref, sums_ref, dt_ref = refs
        else:
            dh_ref, x_ref, dr_ref, ng_ref, sc_ref, dx_ref, sums_ref = refs
        i = pl.program_id(0)
        xv, dhv = x_ref[...], dh_ref[...]
        r = lax.rsqrt(jnp.mean(xv * xv, axis=-1, keepdims=True) + EPS)
        xh = xv * r
        ngv, scv = ng_ref[...], sc_ref[...]
        dxh = dhv * (ngv * (1.0 + scv))
        dx = dr_ref[...] + r * (dxh - xh * jnp.mean(dxh * xh, axis=-1, keepdims=True))
        dx_ref[...] = dx
        dhx = dhv * xh
        rows = [jnp.sum(dhv, axis=0, keepdims=True), jnp.sum(dhx * ngv, axis=0, keepdims=True),
                jnp.sum(dhx * (1.0 + scv), axis=0, keepdims=True)]
        if chain:
            dt_ref[...] = (dx * g_ref[...]).astype(BF16)
            rows.append(jnp.sum(dx * t_ref[...], axis=0, keepdims=True))
        rows.append(jnp.zeros((8 - len(rows), D), F32))
        part = jnp.concatenate(rows, axis=0)

        @pl.when(i == 0)
        def _():
            sums_ref[...] = part

        @pl.when(i > 0)
        def _():
            sums_ref[...] += part

    row, vec = _row_spec(ts, D), _vec_spec(D)
    sums_spec = pl.BlockSpec((8, D), lambda i: (0, 0))
    ins = [dh, xin, dres, ng, sc] + ([t_prev, g_prev] if chain else [])
    in_specs = [row, row, row, vec, vec] + ([row, vec] if chain else [])
    out_specs = [row, sums_spec] + ([row] if chain else [])
    out_shape = [jax.ShapeDtypeStruct((S, D), F32), jax.ShapeDtypeStruct((8, D), F32)] + (
        [jax.ShapeDtypeStruct((S, D), BF16)] if chain else [])
    return pl.pallas_call(body, grid=(S // ts,), in_specs=in_specs, out_specs=out_specs, out_shape=out_shape,
                          compiler_params=_cp("arbitrary"), name=name)(*ins)


def _gate_fwd(proj, wg_p, bg, ts):
    S = proj.shape[0]

    def body(glr_ref, w_ref, b_ref, la_ref):
        z = _dot(glr_ref[...], w_ref[...], NN, HI) + b_ref[...]
        la_ref[...] = (jnp.minimum(z, 0.0) - jnp.log(1.0 + jnp.exp(-jnp.abs(z)))) * (1.0 / GLA_TAU)

    return pl.pallas_call(
        body, grid=(S // ts,),
        in_specs=[pl.BlockSpec((ts, LANE), lambda i: (i, O_GLR // LANE)), pl.BlockSpec((LANE, GLA_QK), lambda i: (0, 0)),
                  pl.BlockSpec((1, GLA_QK), lambda i: (0, 0))],
        out_specs=pl.BlockSpec((ts, GLA_QK), lambda i: (i, 0)), out_shape=jax.ShapeDtypeStruct((S, GLA_QK), F32),
        compiler_params=_cp("parallel"), name="gla_gate_fwd")(proj, wg_p, bg)


def _gate_bwd(dla, la, proj, wg_p, ts):
    S = proj.shape[0]

    def body(dla_ref, la_ref, glr_ref, w_ref, dglr_ref, gw_ref, gb_ref):
        i = pl.program_id(0)
        dz = dla_ref[...] * (1.0 / GLA_TAU) * (1.0 - jnp.exp(GLA_TAU * la_ref[...]))
        dglr_ref[...] = _dot(dz, w_ref[...], NT, HI).astype(BF16)
        gw = _dot(glr_ref[...], dz, TN, HI)
        gb = jnp.concatenate([jnp.sum(dz, axis=0, keepdims=True), jnp.zeros((7, GLA_QK), F32)], axis=0)

        @pl.when(i == 0)
        def _():
            gw_ref[...] = gw
            gb_ref[...] = gb

        @pl.when(i > 0)
        def _():
            gw_ref[...] += gw
            gb_ref[...] += gb

    return pl.pallas_call(
        body, grid=(S // ts,),
        in_specs=[pl.BlockSpec((ts, GLA_QK), lambda i: (i, 0)), pl.BlockSpec((ts, GLA_QK), lambda i: (i, 0)),
                  pl.BlockSpec((ts, LANE), lambda i: (i, O_GLR // LANE)), pl.BlockSpec((LANE, GLA_QK), lambda i: (0, 0))],
        out_specs=[pl.BlockSpec((ts, LANE), lambda i: (i, 0)), pl.BlockSpec((LANE, GLA_QK), lambda i: (0, 0)),
                   pl.BlockSpec((8, GLA_QK), lambda i: (0, 0))],
        out_shape=[jax.ShapeDtypeStruct((S, LANE), BF16), jax.ShapeDtypeStruct((LANE, GLA_QK), F32),
                   jax.ShapeDtypeStruct((8, GLA_QK), F32)],
        compiler_params=_cp("arbitrary"), name="gla_gate_bwd")(dla, la, proj, wg_p)


def _tri(lower):
    r = lax.broadcasted_iota(jnp.int32, (GLA_CHUNK, GLA_CHUNK), 0)
    c = lax.broadcasted_iota(jnp.int32, (GLA_CHUNK, GLA_CHUNK), 1)
    return jnp.where((r >= c) if lower else (c >= r), 1.0, 0.0).astype(F32)


GLA_SUB = 16
GLA_NSUB = GLA_CHUNK // GLA_SUB
PAIR_QK = 2 * GLA_DK
PAIR_V = 2 * GLA_DV


def _band_selector():
    r = lax.broadcasted_iota(jnp.int32, (GLA_SUB * PAIR_QK, LANE), 0)
    c = lax.broadcasted_iota(jnp.int32, (GLA_SUB * PAIR_QK, LANE), 1)
    dist, head = r // PAIR_QK, (r % PAIR_QK) // GLA_DK
    return jnp.where(c == head * GLA_DK + (GLA_SUB - 1 - dist), 1.0, 0.0).astype(BF16)


def _flip_matrix():
    r = lax.broadcasted_iota(jnp.int32, (GLA_CHUNK, GLA_CHUNK), 0)
    c = lax.broadcasted_iota(jnp.int32, (GLA_CHUNK, GLA_CHUNK), 1)
    return jnp.where(r + c == GLA_CHUNK - 1, 1.0, 0.0).astype(BF16)


def _state_mask():
    r = lax.broadcasted_iota(jnp.int32, (PAIR_V, PAIR_QK), 0)
    c = lax.broadcasted_iota(jnp.int32, (PAIR_V, PAIR_QK), 1)
    return (r < GLA_DV) == (c < GLA_DK)


class _GlaChunk:
    def __init__(self, qs, kc, vc, g, sel):
        C = GLA_CHUNK
        self.qs, self.kc, self.vc = qs, kc, vc
        rows = lax.broadcasted_iota(jnp.int32, (C, 1), 0)
        lane = lax.broadcasted_iota(jnp.int32, (1, PAIR_QK), 1)
        self.rows, self.lane = rows, lane
        b = _dot(_tri(True), g, NN, HI)
        self.bl = b[C - 1:C, :]
        self.eb = jnp.exp(b)
        self.kdec = jnp.exp(self.bl - b)
        edge = lambda J: b[GLA_SUB * (J + 1):GLA_SUB * (J + 1) + 1, :]
        self.e_far = [jnp.exp(jnp.where(rows >= GLA_SUB * (J + 1), b - edge(J), NEG)) for J in range(GLA_NSUB - 1)]
        blk = rows // GLA_SUB
        bnext = edge(0)
        for J in range(1, GLA_NSUB - 1):
            bnext = jnp.where(blk == J, edge(J), bnext)
        self.e_khat = jnp.exp(jnp.where(blk < GLA_NSUB - 1, bnext - b, NEG))
        khat = kc * self.e_khat
        k2 = jnp.concatenate([jnp.where(lane < GLA_DK, khat, 0.0), jnp.where(lane >= GLA_DK, khat, 0.0)], axis=0)
        self.blk2 = jnp.concatenate([blk, blk], axis=0)
        self.m_far = jnp.concatenate([jnp.where(self.blk2 == J, k2, 0.0) for J in range(GLA_NSUB - 1)], axis=1).astype(BF16)
        self.qcat = jnp.concatenate([qs * e for e in self.e_far], axis=1).astype(BF16)
        a_far = _dot(self.qcat, self.m_far, NT)
        self.e_band, self.rk, terms = [], [], []
        for d in range(GLA_SUB):
            rk = pltpu.roll(kc, d, 0) if d else kc
            rb = pltpu.roll(b, d, 0) if d else b
            e = jnp.exp(jnp.where(rows >= d, b - rb, NEG))
            self.e_band.append(e)
            self.rk.append(rk)
            terms.append((qs * rk * e).astype(BF16))
        band = _dot(jnp.concatenate(terms, axis=1), sel, NN)
        a_band = pltpu.roll(band, LANE - (GLA_SUB - 1), 1, stride=1, stride_axis=0)
        dist = rows - lane % GLA_DK
        self.far_mask = dist >= GLA_SUB
        self.band_mask = (dist >= 0) & (dist < GLA_SUB)
        self.a = (a_band + jnp.where(self.far_mask, a_far, 0.0)).astype(BF16)
        self.lane_v = lax.broadcasted_iota(jnp.int32, (1, PAIR_V), 1)
        self.v2 = jnp.concatenate([jnp.where(self.lane_v < GLA_DV, vc, 0.0), jnp.where(self.lane_v >= GLA_DV, vc, 0.0)],
                                  axis=0).astype(BF16)


def _gla_fwd(proj, la, tb):
    S = proj.shape[0]
    C = GLA_CHUNK
    tb = min(tb, S)
    nbc = tb // C
    npair = GLA_HEADS // 2
    scale = GLA_DK ** -0.5

    def body(q_ref, k_ref, v_ref, la_ref, sel_ref, o_ref, st_ref, state):
        @pl.when(pl.program_id(1) == 0)
        def _():
            state[...] = jnp.zeros_like(state)

        def chunk(ci, carry):
            sl = pl.ds(pl.multiple_of(ci * C, C), C)
            ch = _GlaChunk(q_ref[sl, :] * scale, k_ref[sl, :], v_ref[sl, :], la_ref[sl, :], sel_ref[...])
            st = state[...]
            st_ref[0, ci] = st
            o_ref[sl, :] = _dot((ch.qs * ch.eb).astype(BF16), st.astype(BF16), NT) + _dot(ch.a, ch.v2, NN)
            upd = _dot(ch.vc.astype(BF16), (ch.kc * ch.kdec).astype(BF16), TN)
            state[...] = st * jnp.exp(ch.bl) + jnp.where(_state_mask(), upd, 0.0)
            return carry

        lax.fori_loop(0, nbc, chunk, 0)

    qspec = lambda off: pl.BlockSpec((tb, PAIR_QK), lambda p, i: (i, off // PAIR_QK + p))
    return pl.pallas_call(
        body, grid=(npair, S // tb),
        in_specs=[qspec(O_GQ), qspec(O_GK), pl.BlockSpec((tb, PAIR_V), lambda p, i: (i, O_GV // PAIR_V + p)),
                  pl.BlockSpec((tb, PAIR_QK), lambda p, i: (i, p)),
                  pl.BlockSpec((GLA_SUB * PAIR_QK, LANE), lambda p, i: (0, 0))],
        out_specs=[pl.BlockSpec((tb, PAIR_V), lambda p, i: (i, p)),
                   pl.BlockSpec((1, nbc, PAIR_V, PAIR_QK), lambda p, i: (p, i, 0, 0))],
        out_shape=[jax.ShapeDtypeStruct((S, GLA_V), F32), jax.ShapeDtypeStruct((npair, S // C, PAIR_V, PAIR_QK), F32)],
        scratch_shapes=[pltpu.VMEM((PAIR_V, PAIR_QK), F32)],
        compiler_params=_cp("parallel", "arbitrary"), name="gla_fwd")(proj, proj, proj, la, _band_selector())


def _gla_bwd(proj, la, do, states, tb):
    S = proj.shape[0]
    C = GLA_CHUNK
    tb = min(tb, S)
    nbc = tb // C
    nblk = S // tb
    npair = GLA_HEADS // 2
    scale = GLA_DK ** -0.5

    def body(q_ref, k_ref, v_ref, la_ref, do_ref, st_ref, sel_ref, selt_ref, dq_ref, dk_ref, dv_ref, dla_ref, dstate):
        @pl.when(pl.program_id(1) == 0)
        def _():
            dstate[...] = jnp.zeros_like(dstate)

        def chunk(cc, carry):
            ci = nbc - 1 - cc
            sl = pl.ds(pl.multiple_of(ci * C, C), C)
            ch = _GlaChunk(q_ref[sl, :] * scale, k_ref[sl, :], v_ref[sl, :], la_ref[sl, :], sel_ref[...])
            qs, kc, rows = ch.qs, ch.kc, ch.rows
            doc_b = do_ref[sl, :].astype(BF16)
            st = st_ref[0, ci]
            dst = dstate[...]
            dst_b = dst.astype(BF16)
            ebl = jnp.exp(ch.bl)
            dq = _dot(doc_b, st.astype(BF16), NN) * ch.eb
            dk = _dot(ch.vc.astype(BF16), dst_b, NN) * ch.kdec
            dv = _dot((kc * ch.kdec).astype(BF16), dst_b, NT)
            dbl = jnp.sum(dst * st, axis=0, keepdims=True) * ebl + jnp.sum(kc * dk, axis=0, keepdims=True)
            da = _dot(doc_b, ch.v2, NT)
            dv2 = _dot(ch.a, doc_b, TN)
            dv = dv + jnp.where(ch.lane_v < GLA_DV, dv2[:C], dv2[C:])
            da_far = jnp.where(ch.far_mask, da, 0.0).astype(BF16)
            dqcat = _dot(da_far, ch.m_far, NN)
            dm = _dot(da_far, ch.qcat, TN)
            dk2 = jnp.zeros((2 * C, PAIR_QK), F32)
            for J in range(GLA_NSUB - 1):
                dq = dq + dqcat[:, J * PAIR_QK:(J + 1) * PAIR_QK] * ch.e_far[J]
                dk2 = dk2 + jnp.where(ch.blk2 == J, dm[:, J * PAIR_QK:(J + 1) * PAIR_QK], 0.0)
            dk = dk + jnp.where(ch.lane < GLA_DK, dk2[:C], dk2[C:]) * ch.e_khat
            flip = _flip_matrix()
            da_band = _dot(flip, jnp.where(ch.band_mask, da, 0.0).astype(BF16), NN)
            dband = pltpu.roll(da_band, LANE - (C - GLA_SUB), 1, stride=1, stride_axis=0)
            dband = _dot(flip, dband.astype(BF16), NN)
            dterms = _dot(dband.astype(BF16), selt_ref[...], NN)
            for d in range(GLA_SUB):
                dt = dterms[:, d * PAIR_QK:(d + 1) * PAIR_QK]
                dq = dq + dt * (ch.rk[d] * ch.e_band[d])
                dkr = dt * (qs * ch.e_band[d])
                dk = dk + (pltpu.roll(dkr, C - d, 0) if d else dkr)
            db = qs * dq - kc * dk
            db = jnp.where(rows == C - 1, db + dbl, db)
            dq_ref[sl, :] = (dq * scale).astype(BF16)
            dk_ref[sl, :] = dk.astype(BF16)
            dv_ref[sl, :] = dv.astype(BF16)
            dla_ref[sl, :] = _dot(_tri(False), db, NN, HI)
            upd = _dot(doc_b, (qs * ch.eb).astype(BF16), TN)
            dstate[...] = dst * ebl + jnp.where(_state_mask(), upd, 0.0)
            return carry

        lax.fori_loop(0, nbc, chunk, 0)

    rev = lambda i: nblk - 1 - i
    qspec = lambda off: pl.BlockSpec((tb, PAIR_QK), lambda p, i: (rev(i), off // PAIR_QK + p))
    pair_qk = pl.BlockSpec((tb, PAIR_QK), lambda p, i: (rev(i), p))
    pair_v = pl.BlockSpec((tb, PAIR_V), lambda p, i: (rev(i), p))
    sel = _band_selector()
    return pl.pallas_call(
        body, grid=(npair, nblk),
        in_specs=[qspec(O_GQ), qspec(O_GK), pl.BlockSpec((tb, PAIR_V), lambda p, i: (rev(i), O_GV // PAIR_V + p)),
                  pair_qk, pair_v, pl.BlockSpec((1, nbc, PAIR_V, PAIR_QK), lambda p, i: (p, rev(i), 0, 0)),
                  pl.BlockSpec((GLA_SUB * PAIR_QK, LANE), lambda p, i: (0, 0)),
                  pl.BlockSpec((LANE, GLA_SUB * PAIR_QK), lambda p, i: (0, 0))],
        out_specs=[pair_qk, pair_qk, pair_v, pair_qk],
        out_shape=[jax.ShapeDtypeStruct((S, GLA_QK), BF16), jax.ShapeDtypeStruct((S, GLA_QK), BF16),
                   jax.ShapeDtypeStruct((S, GLA_V), BF16), jax.ShapeDtypeStruct((S, GLA_QK), F32)],
        scratch_shapes=[pltpu.VMEM((PAIR_V, PAIR_QK), F32)],
        compiler_params=_cp("parallel", "arbitrary"), name="gla_bwd")(proj, proj, proj, la, do, states, sel, sel.T)


def _gla_out(o, proj, gng, ts):
    S = o.shape[0]

    def body(o_ref, gr_ref, g_ref, y_ref):
        for h in range(GLA_HEADS):
            cols = slice(h * GLA_DV, (h + 1) * GLA_DV)
            ov, grv = o_ref[:, cols], gr_ref[:, cols]
            r = lax.rsqrt(jnp.mean(ov * ov, axis=-1, keepdims=True) + EPS)
            y_ref[:, cols] = (ov * r * g_ref[...] * (grv * _sigmoid(grv))).astype(BF16)

    return pl.pallas_call(
        body, grid=(S // ts,),
        in_specs=[pl.BlockSpec((ts, GLA_V), lambda i: (i, 0)), pl.BlockSpec((ts, GLA_V), lambda i: (i, O_GR // GLA_V)),
                  pl.BlockSpec((1, GLA_DV), lambda i: (0, 0))],
        out_specs=pl.BlockSpec((ts, GLA_V), lambda i: (i, 0)), out_shape=jax.ShapeDtypeStruct((S, GLA_V), BF16),
        compiler_params=_cp("parallel"), name="gla_out_fwd")(o, proj, gng)


def _gla_out_bwd(dmixed, o, proj, gng, ts):
    S = o.shape[0]

    def body(dy_ref, o_ref, gr_ref, g_ref, do_ref, dgr_ref, gg_ref):
        i = pl.program_id(0)
        gsum = jnp.zeros((1, GLA_DV), F32)
        for h in range(GLA_HEADS):
            cols = slice(h * GLA_DV, (h + 1) * GLA_DV)
            ov, grv, dy = o_ref[:, cols], gr_ref[:, cols], dy_ref[:, cols]
            r = lax.rsqrt(jnp.mean(ov * ov, axis=-1, keepdims=True) + EPS)
            oh = ov * r
            sg = _sigmoid(grv)
            silu = grv * sg
            don = dy * silu
            dgr_ref[:, cols] = (dy * (oh * g_ref[...]) * (sg * (1.0 + grv * (1.0 - sg)))).astype(BF16)
            gsum = gsum + jnp.sum(don * oh, axis=0, keepdims=True)
            doh = don * g_ref[...]
            do_ref[:, cols] = r * (doh - oh * jnp.mean(doh * oh, axis=-1, keepdims=True))
        part = jnp.concatenate([gsum, jnp.zeros((7, GLA_DV), F32)], axis=0)

        @pl.when(i == 0)
        def _():
            gg_ref[...] = part

        @pl.when(i > 0)
        def _():
            gg_ref[...] += part

    return pl.pallas_call(
        body, grid=(S // ts,),
        in_specs=[pl.BlockSpec((ts, GLA_V), lambda i: (i, 0)), pl.BlockSpec((ts, GLA_V), lambda i: (i, 0)),
                  pl.BlockSpec((ts, GLA_V), lambda i: (i, O_GR // GLA_V)), pl.BlockSpec((1, GLA_DV), lambda i: (0, 0))],
        out_specs=[pl.BlockSpec((ts, GLA_V), lambda i: (i, 0)), pl.BlockSpec((ts, GLA_V), lambda i: (i, 0)),
                   pl.BlockSpec((8, GLA_DV), lambda i: (0, 0))],
        out_shape=[jax.ShapeDtypeStruct((S, GLA_V), F32), jax.ShapeDtypeStruct((S, GLA_V), BF16),
                   jax.ShapeDtypeStruct((8, GLA_DV), F32)],
        compiler_params=_cp("arbitrary"), name="gla_out_bwd")(dmixed, o, proj, gng)


def _seg_matrix(width, seg, value):
    r = lax.broadcasted_iota(jnp.int32, (width, width), 0) // seg
    c = lax.broadcasted_iota(jnp.int32, (width, width), 1) // seg
    return jnp.where(r == c, value, 0.0).astype(F32)


def _head_norm(proj, qg, kg, ts):
    S = proj.shape[0]
    W = ATTN_DIM

    def body(q_ref, k_ref, qg_ref, kg_ref, qn_ref, kn_ref):
        seg = _seg_matrix(W, ATTN_HD, 1.0 / ATTN_HD)
        for x_ref, g_ref, o_ref in ((q_ref, qg_ref, qn_ref), (k_ref, kg_ref, kn_ref)):
            xv = x_ref[...]
            ms = _dot(xv * xv, seg, NN, HI)
            o_ref[...] = xv * lax.rsqrt(ms + EPS) * g_ref[...]

    blk = lambda off: pl.BlockSpec((ts, W), lambda i: (i, off // W))
    out = pl.BlockSpec((ts, W), lambda i: (i, 0))
    vec = pl.BlockSpec((1, W), lambda i: (0, 0))
    return pl.pallas_call(
        body, grid=(S // ts,), in_specs=[blk(O_AQ), blk(O_AK), vec, vec], out_specs=[out] * 2,
        out_shape=[jax.ShapeDtypeStruct((S, W), F32)] * 2, compiler_params=_cp("parallel"), name="attn_head_norm")(
            proj, proj, qg, kg)


def _slope(head):
    one = jnp.ones((1, 1), jnp.int32)
    return 1.0 / jnp.left_shift(one, one * (head + 1)).astype(F32)


def _for_subsequences(d, fn):
    if d == 1:
        fn(pl.ds(0, ATTN_BLOCK))
    else:
        def step(r, carry):
            fn(pl.ds(r, ATTN_BLOCK, stride=d))
            return carry
        lax.fori_loop(0, d, step, 0)


def _attn_specs(d, S):
    tq = d * ATTN_BLOCK
    nb = S // tq

    def specs(off=0):
        return [pl.BlockSpec((tq, LANE), lambda hp, n: (n, off + hp)),
                pl.BlockSpec((tq, LANE), lambda hp, n: (jnp.maximum(n - 1, 0), off + hp)),
                pl.BlockSpec((tq, LANE), lambda hp, n: (jnp.minimum(n + 1, nb - 1), off + hp))]

    return tq, nb, specs


def _attn_fwd(qn, kn, proj, d):
    S, W = qn.shape
    B = ATTN_BLOCK
    tq, nb, specs = _attn_specs(d, S)

    def body(q_ref, kp_ref, kc_ref, vp_ref, vc_ref, o_ref, l_ref):
        hp, n = pl.program_id(0), pl.program_id(1)
        lo = lax.broadcasted_iota(jnp.int32, (1, LANE), 1) < ATTN_HD
        iq = lax.broadcasted_iota(jnp.int32, (B, 2 * B), 0)
        ik = lax.broadcasted_iota(jnp.int32, (B, 2 * B), 1)
        rel = iq + B - ik
        valid = (rel >= 0) & (rel <= B) & ((ik >= B) | (n > 0))
        relf = (d * rel).astype(F32)

        def sub(rows):
            qv = q_ref[rows, :].astype(BF16)
            kv = jnp.concatenate([kp_ref[rows, :], kc_ref[rows, :]], axis=0).astype(BF16)
            vv = jnp.concatenate([vp_ref[rows, :], vc_ref[rows, :]], axis=0).astype(BF16)
            outs, lses = [], []
            for h in range(2):
                qm = jnp.where(lo == (h == 0), qv, jnp.zeros_like(qv))
                s = _dot(qm, kv, NT) * (ATTN_HD ** -0.5) - _slope(hp * 2 + h) * relf
                s = jnp.where(valid, s, NEG)
                m = jnp.max(s, axis=-1, keepdims=True)
                p = jnp.exp(s - m)
                den = jnp.sum(p, axis=-1, keepdims=True)
                outs.append(_dot(p.astype(BF16), vv, NN) / den)
                lses.append(m + jnp.log(den))
            o_ref[rows, :] = jnp.where(lo, outs[0], outs[1])
            l_ref[rows, :] = jnp.where(lo, lses[0], lses[1])

        _for_subsequences(d, sub)

    cur, prev, _ = specs()
    vcur, vprev, _ = specs(O_AV // LANE)
    return pl.pallas_call(
        body, grid=(W // LANE, nb), in_specs=[cur, prev, cur, vprev, vcur], out_specs=[cur, cur],
        out_shape=[jax.ShapeDtypeStruct((S, W), F32)] * 2,
        compiler_params=_cp("parallel", "arbitrary"), name=f"attn_fwd_d{d}")(qn, kn, kn, proj, proj)


def _attn_merge(y_gla, os_, ls_, ts):
    S, W = os_[0].shape

    def body(yg, o1, o2, o3, l1, l2, l3, mixed_ref, y_ref, lse_ref):
        a, b, c = l1[...], l2[...], l3[...]
        m = jnp.maximum(jnp.maximum(a, b), c)
        ea, eb, ec = jnp.exp(a - m), jnp.exp(b - m), jnp.exp(c - m)
        tot = ea + eb + ec
        y = (ea * o1[...] + eb * o2[...] + ec * o3[...]) / tot
        y_ref[...] = y
        mixed_ref[:, :W] = yg[...]
        mixed_ref[:, W:] = y.astype(BF16)
        lse_ref[...] = m + jnp.log(tot)

    spec = pl.BlockSpec((ts, W), lambda i: (i, 0))
    return pl.pallas_call(
        body, grid=(S // ts,), in_specs=[spec] * 7, out_specs=[pl.BlockSpec((ts, 2 * W), lambda i: (i, 0)), spec, spec],
        out_shape=[jax.ShapeDtypeStruct((S, 2 * W), BF16), jax.ShapeDtypeStruct((S, W), F32), jax.ShapeDtypeStruct((S, W), F32)],
        compiler_params=_cp("parallel"), name="attn_merge")(y_gla, *os_, *ls_)


def _attn_delta(dmixed, y, ts):
    S, W = y.shape

    def body(dy_ref, y_ref, d_ref):
        d_ref[...] = _dot(dy_ref[...] * y_ref[...], _seg_matrix(W, ATTN_HD, 1.0), NN, HI)

    return pl.pallas_call(
        body, grid=(S // ts,), in_specs=[pl.BlockSpec((ts, W), lambda i: (i, 1)), pl.BlockSpec((ts, W), lambda i: (i, 0))],
        out_specs=pl.BlockSpec((ts, W), lambda i: (i, 0)), out_shape=jax.ShapeDtypeStruct((S, W), F32),
        compiler_params=_cp("parallel"), name="attn_delta")(dmixed, y)


def _attn_dq(qn, kn, proj, dmixed, lse, delta, d):
    S, W = qn.shape
    B = ATTN_BLOCK
    tq, nb, specs = _attn_specs(d, S)

    def body(q_ref, kp_ref, kc_ref, vp_ref, vc_ref, dy_ref, l_ref, de_ref, dq_ref):
        hp, n = pl.program_id(0), pl.program_id(1)
        lo = lax.broadcasted_iota(jnp.int32, (1, LANE), 1) < ATTN_HD
        iq = lax.broadcasted_iota(jnp.int32, (B, 2 * B), 0)
        ik = lax.broadcasted_iota(jnp.int32, (B, 2 * B), 1)
        rel = iq + B - ik
        valid = (rel >= 0) & (rel <= B) & ((ik >= B) | (n > 0))
        relf = (d * rel).astype(F32)

        def sub(rows):
            qv, dyv = q_ref[rows, :].astype(BF16), dy_ref[rows, :]
            lv, dev = l_ref[rows, :], de_ref[rows, :]
            kv = jnp.concatenate([kp_ref[rows, :], kc_ref[rows, :]], axis=0).astype(BF16)
            vv = jnp.concatenate([vp_ref[rows, :], vc_ref[rows, :]], axis=0).astype(BF16)
            outs = []
            for h in range(2):
                sel = lo == (h == 0)
                qm = jnp.where(sel, qv, jnp.zeros_like(qv))
                dym = jnp.where(sel, dyv, 0.0).astype(BF16)
                lse_h = lv[:, h * ATTN_HD:h * ATTN_HD + 1]
                del_h = dev[:, h * ATTN_HD:h * ATTN_HD + 1]
                s = _dot(qm, kv, NT) * (ATTN_HD ** -0.5) - _slope(hp * 2 + h) * relf
                p = jnp.exp(jnp.where(valid, s, NEG) - lse_h)
                ds = p * (_dot(dym, vv, NT) - del_h)
                outs.append(_dot(ds.astype(BF16), kv, NN) * (ATTN_HD ** -0.5))
            dq_ref[rows, :] = jnp.where(lo, outs[0], outs[1])

        _for_subsequences(d, sub)

    cur, prev, _ = specs()
    vcur, vprev, _ = specs(O_AV // LANE)
    dycur, _, _ = specs(W // LANE)
    return pl.pallas_call(
        body, grid=(W // LANE, nb), in_specs=[cur, prev, cur, vprev, vcur, dycur, cur, cur], out_specs=cur,
        out_shape=jax.ShapeDtypeStruct((S, W), F32),
        compiler_params=_cp("parallel", "arbitrary"), name=f"attn_dq_d{d}")(qn, kn, kn, proj, proj, dmixed, lse, delta)


def _attn_dkv(qn, kn, proj, dmixed, lse, delta, d):
    S, W = qn.shape
    B = ATTN_BLOCK
    tq, nb, specs = _attn_specs(d, S)

    def body(k_ref, v_ref, qc_ref, qn_ref, dyc_ref, dyn_ref, lc_ref, ln_ref, dec_ref, den_ref, dk_ref, dv_ref):
        hp, n = pl.program_id(0), pl.program_id(1)
        lo = lax.broadcasted_iota(jnp.int32, (1, LANE), 1) < ATTN_HD
        iq = lax.broadcasted_iota(jnp.int32, (B, B), 0)
        ik = lax.broadcasted_iota(jnp.int32, (B, B), 1)

        def sub(rows):
            kv, vv = k_ref[rows, :].astype(BF16), v_ref[rows, :].astype(BF16)
            dk = jnp.zeros((B, LANE), F32)
            dv = jnp.zeros((B, LANE), F32)
            for nxt, q_ref, dy_ref, l_ref, de_ref in ((0, qc_ref, dyc_ref, lc_ref, dec_ref), (1, qn_ref, dyn_ref, ln_ref, den_ref)):
                rel = iq - ik + nxt * B
                valid = (rel >= 0) & (rel <= B)
                if nxt:
                    valid = valid & (n + 1 < nb)
                relf = (d * rel).astype(F32)
                qv, dyv = q_ref[rows, :].astype(BF16), dy_ref[rows, :]
                lv, dev = l_ref[rows, :], de_ref[rows, :]
                for h in range(2):
                    sel = lo == (h == 0)
                    qm = jnp.where(sel, qv, jnp.zeros_like(qv))
                    dym = jnp.where(sel, dyv, 0.0).astype(BF16)
                    lse_h = lv[:, h * ATTN_HD:h * ATTN_HD + 1]
                    del_h = dev[:, h * ATTN_HD:h * ATTN_HD + 1]
                    s = _dot(qm, kv, NT) * (ATTN_HD ** -0.5) - _slope(hp * 2 + h) * relf
                    p = jnp.exp(jnp.where(valid, s, NEG) - lse_h)
                    dv = dv + _dot(p.astype(BF16), dym, TN)
                    ds = p * (_dot(dym, vv, NT) - del_h)
                    dk = dk + _dot(ds.astype(BF16), qm, TN) * (ATTN_HD ** -0.5)
            dk_ref[rows, :] = dk
            dv_ref[rows, :] = dv

        _for_subsequences(d, sub)

    cur, _, nxt = specs()
    vcur, _, _ = specs(O_AV // LANE)
    dycur, _, dynxt = specs(W // LANE)
    return pl.pallas_call(
        body, grid=(W // LANE, nb), in_specs=[cur, vcur, cur, nxt, dycur, dynxt, cur, nxt, cur, nxt], out_specs=[cur, cur],
        out_shape=[jax.ShapeDtypeStruct((S, W), F32)] * 2,
        compiler_params=_cp("parallel", "arbitrary"), name=f"attn_dkv_d{d}")(
            kn, proj, qn, qn, dmixed, dmixed, lse, lse, delta, delta)


def _attn_post(dqs, dks, dvs, proj, qg, kg, ts):
    S = proj.shape[0]
    W = ATTN_DIM

    def body(dq1, dq2, dq3, dk1, dk2, dk3, dv1, dv2, dv3, aq_ref, ak_ref, qg_ref, kg_ref, daq_ref, dak_ref, dav_ref, gg_ref):
        i = pl.program_id(0)
        seg = _seg_matrix(W, ATTN_HD, 1.0 / ATTN_HD)
        gsums = []
        for (d1, d2, d3), x_ref, g_ref, o_ref in (((dq1, dq2, dq3), aq_ref, qg_ref, daq_ref), ((dk1, dk2, dk3), ak_ref, kg_ref, dak_ref)):
            dy = d1[...] + d2[...] + d3[...]
            xv = x_ref[...]
            r = lax.rsqrt(_dot(xv * xv, seg, NN, HI) + EPS)
            xh = xv * r
            dxh = dy * g_ref[...]
            o_ref[...] = (r * (dxh - xh * _dot(dxh * xh, seg, NN, HI))).astype(BF16)
            gsums.append(jnp.sum(dy * xh, axis=0, keepdims=True))
        dav_ref[...] = (dv1[...] + dv2[...] + dv3[...]).astype(BF16)
        part = jnp.concatenate(gsums + [jnp.zeros((6, W), F32)], axis=0)

        @pl.when(i == 0)
        def _():
            gg_ref[...] = part

        @pl.when(i > 0)
        def _():
            gg_ref[...] += part

    row = pl.BlockSpec((ts, W), lambda i: (i, 0))
    blk = lambda off: pl.BlockSpec((ts, W), lambda i: (i, off // W))
    vec = pl.BlockSpec((1, W), lambda i: (0, 0))
    return pl.pallas_call(
        body, grid=(S // ts,), in_specs=[row] * 9 + [blk(O_AQ), blk(O_AK), vec, vec],
        out_specs=[row, row, row, pl.BlockSpec((8, W), lambda i: (0, 0))],
        out_shape=[jax.ShapeDtypeStruct((S, W), BF16)] * 3 + [jax.ShapeDtypeStruct((8, W), F32)],
        compiler_params=_cp("arbitrary"), name="attn_post")(*dqs, *dks, *dvs, proj, proj, qg, kg)


def _shift_down(cur, halo, n):
    ts = cur.shape[0]
    rows = lax.broadcasted_iota(jnp.int32, (ts, 1), 0)
    out = pltpu.roll(cur, n, 0)
    for t in range(n):
        out = jnp.where(rows == t, halo[8 - n + t:8 - n + t + 1, :], out)
    return out


def _shift_up(cur, halo, n):
    ts = cur.shape[0]
    rows = lax.broadcasted_iota(jnp.int32, (ts, 1), 0)
    out = pltpu.roll(cur, ts - n, 0)
    for t in range(n):
        out = jnp.where(rows == ts - n + t, halo[t:t + 1, :], out)
    return out


def _conv(cur, halo, w, b):
    return b + w[0:1, :] * _shift_down(cur, halo, 2) + w[1:2, :] * _shift_down(cur, halo, 1) + w[2:3, :] * cur


def _conv_swiglu(u0, conv_w8, conv_b, ts, tc):
    S, F2 = u0.shape
    F = F2 // 2
    nc = F // tc
    hb = ts // 8

    def body(ug_ref, ugh_ref, uv_ref, uvh_ref, wg_ref, wv_ref, bg_ref, bv_ref, a_ref):
        first = pl.program_id(0) == 0
        ugh = jnp.where(first, 0.0, ugh_ref[...])
        uvh = jnp.where(first, 0.0, uvh_ref[...])
        g = _conv(ug_ref[...], ugh, wg_ref[...], bg_ref[...])
        v = _conv(uv_ref[...], uvh, wv_ref[...], bv_ref[...])
        a_ref[...] = (g * _sigmoid(g) * v).astype(BF16)

    main = lambda off: pl.BlockSpec((ts, tc), lambda i, j: (i, j + off))
    halo = lambda off: pl.BlockSpec((8, tc), lambda i, j: (jnp.maximum(i * hb - 1, 0), j + off))
    wspec = lambda off: pl.BlockSpec((8, tc), lambda i, j: (0, j + off))
    bspec = lambda off: pl.BlockSpec((1, tc), lambda i, j: (0, j + off))
    return pl.pallas_call(
        body, grid=(S // ts, nc),
        in_specs=[main(0), halo(0), main(nc), halo(nc), wspec(0), wspec(nc), bspec(0), bspec(nc)],
        out_specs=pl.BlockSpec((ts, tc), lambda i, j: (i, j)), out_shape=jax.ShapeDtypeStruct((S, F), BF16),
        compiler_params=_cp("parallel", "parallel"), name="conv_swiglu")(u0, u0, u0, u0, conv_w8, conv_w8, conv_b, conv_b)


def _ffn_du(da, u0, conv_w8, conv_b, ts, tc):
    S, F2 = u0.shape
    F = F2 // 2
    nc = F // tc
    hb = ts // 8

    def body(da_ref, uo_ref, uoh_ref, up_ref, uph_ref, wo_ref, wp_ref, bo_ref, bp_ref, du_ref, sums_ref):
        j, i = pl.program_id(0), pl.program_id(1)
        first = i == 0
        uo, uoh = uo_ref[...], jnp.where(first, 0.0, uoh_ref[...])
        s2, s1 = _shift_down(uo, uoh, 2), _shift_down(uo, uoh, 1)
        wo = wo_ref[...]
        own = bo_ref[...] + wo[0:1, :] * s2 + wo[1:2, :] * s1 + wo[2:3, :] * uo
        partner = _conv(up_ref[...], jnp.where(first, 0.0, uph_ref[...]), wp_ref[...], bp_ref[...])
        dav = da_ref[...]
        sg_own, sg_par = _sigmoid(own), _sigmoid(partner)
        du_gate = dav * partner * (sg_own * (1.0 + own * (1.0 - sg_own)))
        du_val = dav * (partner * sg_par)
        du = jnp.where(j < nc, du_gate, du_val)
        du_ref[...] = du
        part = jnp.concatenate([jnp.sum(du * s2, axis=0, keepdims=True), jnp.sum(du * s1, axis=0, keepdims=True),
                                jnp.sum(du * uo, axis=0, keepdims=True), jnp.sum(du, axis=0, keepdims=True),
                                jnp.zeros((4, tc), F32)], axis=0)

        @pl.when(first)
        def _():
            sums_ref[...] = part

        @pl.when(i > 0)
        def _():
            sums_ref[...] += part

    partner = lambda j: (j + nc) % (2 * nc)
    main_o = pl.BlockSpec((ts, tc), lambda j, i: (i, j))
    halo_o = pl.BlockSpec((8, tc), lambda j, i: (jnp.maximum(i * hb - 1, 0), j))
    main_p = pl.BlockSpec((ts, tc), lambda j, i: (i, partner(j)))
    halo_p = pl.BlockSpec((8, tc), lambda j, i: (jnp.maximum(i * hb - 1, 0), partner(j)))
    return pl.pallas_call(
        body, grid=(2 * nc, S // ts),
        in_specs=[pl.BlockSpec((ts, tc), lambda j, i: (i, j % nc)), main_o, halo_o, main_p, halo_p,
                  pl.BlockSpec((8, tc), lambda j, i: (0, j)), pl.BlockSpec((8, tc), lambda j, i: (0, partner(j))),
                  pl.BlockSpec((1, tc), lambda j, i: (0, j)), pl.BlockSpec((1, tc), lambda j, i: (0, partner(j)))],
        out_specs=[main_o, pl.BlockSpec((8, tc), lambda j, i: (0, j))],
        out_shape=[jax.ShapeDtypeStruct((S, F2), F32), jax.ShapeDtypeStruct((8, F2), F32)],
        compiler_params=_cp("parallel", "arbitrary"), name="ffn_du")(da, u0, u0, u0, u0, conv_w8, conv_w8, conv_b, conv_b)


def _ffn_du0(du, conv_w8, ts, tc):
    S, F2 = du.shape
    hb = ts // 8
    nrow = S // ts

    def body(du_ref, duh_ref, w_ref, o_ref):
        last = pl.program_id(0) == nrow - 1
        cur, halo, w = du_ref[...], jnp.where(last, 0.0, duh_ref[...]), w_ref[...]
        o_ref[...] = (w[2:3, :] * cur + w[1:2, :] * _shift_up(cur, halo, 1) + w[0:1, :] * _shift_up(cur, halo, 2)).astype(BF16)

    return pl.pallas_call(
        body, grid=(nrow, F2 // tc),
        in_specs=[pl.BlockSpec((ts, tc), lambda i, j: (i, j)),
                  pl.BlockSpec((8, tc), lambda i, j: (jnp.minimum((i + 1) * hb, S // 8 - 1), j)),
                  pl.BlockSpec((8, tc), lambda i, j: (0, j))],
        out_specs=pl.BlockSpec((ts, tc), lambda i, j: (i, j)), out_shape=jax.ShapeDtypeStruct((S, F2), BF16),
        compiler_params=_cp("parallel", "parallel"), name="ffn_du0")(du, du, conv_w8)


def _loss_resid(x2, t2, g2, target, ts):
    S, D = x2.shape

    def body(x_ref, t_ref, g_ref, y_ref, dx_ref, dt_ref, sums_ref):
        i = pl.program_id(0)
        tv, gv = t_ref[...], g_ref[...]
        e = x_ref[...] + gv * tv - y_ref[...]
        dx = e * (1.0 / D)
        dx_ref[...] = dx
        dt_ref[...] = (dx * gv).astype(BF16)
        part = jnp.concatenate([jnp.sum(e * e, axis=0, keepdims=True), jnp.sum(dx * tv, axis=0, keepdims=True),
                                jnp.zeros((6, D), F32)], axis=0)

        @pl.when(i == 0)
        def _():
            sums_ref[...] = part

        @pl.when(i > 0)
        def _():
            sums_ref[...] += part

    row, vec = _row_spec(ts, D), _vec_spec(D)
    return pl.pallas_call(
        body, grid=(S // ts,), in_specs=[row, row, vec, row], out_specs=[row, row, pl.BlockSpec((8, D), lambda i: (0, 0))],
        out_shape=[jax.ShapeDtypeStruct((S, D), F32), jax.ShapeDtypeStruct((S, D), BF16), jax.ShapeDtypeStruct((8, D), F32)],
        compiler_params=_cp("arbitrary"), name="loss_resid")(x2, t2, g2, target)


def _adamw(w, g, m, v, name):
    shape = w.shape
    n = math.prod(shape)
    view = (n // LANE, LANE) if n % LANE == 0 else (math.prod(shape[:-1]), shape[-1])
    R, C = view
    tr = R
    for cand in (1024, 512, 256):
        if R > cand and R % cand == 0:
            tr = cand
            break

    def body(w_ref, g_ref, m_ref, v_ref, d_ref, nm_ref, nv_ref):
        gv = g_ref[...]
        nm = ADAM_B1 * m_ref[...] + (1.0 - ADAM_B1) * gv
        nv = ADAM_B2 * v_ref[...] + (1.0 - ADAM_B2) * (gv * gv)
        m_hat = nm / (1.0 - ADAM_B1 ** ADAM_STEP)
        v_hat = nv / (1.0 - ADAM_B2 ** ADAM_STEP)
        d_ref[...] = -ADAM_LR * (m_hat / (jnp.sqrt(v_hat) + ADAM_EPS) + ADAM_WD * w_ref[...])
        nm_ref[...] = nm
        nv_ref[...] = nv

    spec = pl.BlockSpec((tr, C), lambda i: (i, 0))
    outs = pl.pallas_call(
        body, grid=(R // tr,), in_specs=[spec] * 4, out_specs=[spec] * 3, out_shape=[jax.ShapeDtypeStruct(view, F32)] * 3,
        compiler_params=_cp("parallel"), name=name)(*[a.reshape(view) for a in (w, g, m, v)])
    return [o.reshape(shape) for o in outs]


def _pad_rows8(a):
    return jnp.concatenate([a, jnp.zeros((8 - a.shape[0], a.shape[1]), a.dtype)], axis=0)


def _local_step(x, target, mod, n1g, w_in_p, wg_p, bg, gng, qng, kng, w_out, n2g, w_up, conv_w, conv_b, w_down):
    S, D = x.shape
    F = w_down.shape[0]
    ts = min(512, S)
    sh1, sc1, g1, sh2, sc2, g2 = [mod[i:i + 1] for i in range(6)]
    conv_w8 = _pad_rows8(conv_w)
    qg_t, kg_t = jnp.tile(qng, (1, ATTN_HEADS)), jnp.tile(kng, (1, ATTN_HEADS))

    h1 = _rms_mod(x, n1g, sc1, sh1, ts, "rms_mod1")
    proj = _mm(h1, w_in_p, NN, 512, 640, 1024, F32, "mm_in")
    la = _gate_fwd(proj, wg_p, bg, ts)
    o_gla, states = _gla_fwd(proj, la, 512)
    y_gla = _gla_out(o_gla, proj, gng, ts)
    qn, kn = _head_norm(proj, qg_t, kg_t, ts)
    branches = [_attn_fwd(qn, kn, proj, d) for d in DILATIONS]
    mixed, y_att, lse = _attn_merge(y_gla, [b[0] for b in branches], [b[1] for b in branches], ts)
    t1 = _mm(mixed, w_out, NN, 512, 1024, 1024, F32, "mm_out")
    x2, h2 = _resid_rms_mod(x, t1, g1, n2g, sc2, sh2, ts, "resid_rms_mod2")
    u0 = _mm(h2, w_up, NN, 512, 1408, 1024, F32, "mm_up")
    tc = 1408 if F % 1408 == 0 else F
    a = _conv_swiglu(u0, conv_w8, conv_b, min(256, S), tc)
    t2 = _mm(a, w_down, NN, 512, 1024, F, F32, "mm_down")
    dx3, dt2, sums3 = _loss_resid(x2, t2, g2, target, ts)
    loss_row, dg2 = sums3[0:1], sums3[1:2]

    g_w_down = _mm(a, dt2, TN, 1408, 1024, 512, F32, "mm_gw_down")
    da = _mm(dt2, w_down, NT, 512, 1408, 1024, F32, "mm_da")
    du, conv_sums = _ffn_du(da, u0, conv_w8, conv_b, min(256, S), tc)
    g_conv_w, g_conv_b = conv_sums[0:3], conv_sums[3:4]
    du0 = _ffn_du0(du, conv_w8, min(256, S), tc)
    g_w_up = _mm(h2, du0, TN, 1024, 1408, 512, F32, "mm_gw_up")
    dh2 = _mm(du0, w_up, NT, 512, 1024, 1408, F32, "mm_dh2")
    dx2, sums2, dt1 = _rms_mod_bwd(dh2, x2, dx3, n2g, sc2, ts, "rms_mod_bwd2", t_prev=t1, g_prev=g1)
    dsh2, dsc2, g_n2g, dg1 = sums2[0:1], sums2[1:2], sums2[2:3], sums2[3:4]
    g_w_out = _mm(mixed, dt1, TN, 1024, 1024, 512, F32, "mm_gw_out")
    dmixed = _mm(dt1, w_out, NT, 512, 1024, 1024, F32, "mm_dmixed")
    do_gla, dgr, gng_sums = _gla_out_bwd(dmixed, o_gla, proj, gng, ts)
    dgq, dgk, dgv, dla = _gla_bwd(proj, la, do_gla, states, 512)
    dglr, g_wg_p, gb_sums = _gate_bwd(dla, la, proj, wg_p, ts)
    delta = _attn_delta(dmixed, y_att, ts)
    dqs = [_attn_dq(qn, kn, proj, dmixed, lse, delta, d) for d in DILATIONS]
    dkvs = [_attn_dkv(qn, kn, proj, dmixed, lse, delta, d) for d in DILATIONS]
    daq, dak, dav, qk_sums = _attn_post(dqs, [t[0] for t in dkvs], [t[1] for t in dkvs], proj, qg_t, kg_t, ts)
    dproj = jnp.concatenate([dgq, dgk, dgv, dgr, daq, dak, dav, dglr], axis=1)
    g_w_in_p = _mm(h1, dproj, TN, 1024, 640, 512, F32, "mm_gw_in")
    dh1 = _mm(dproj, w_in_p, NT, 512, 1024, 640, F32, "mm_dh1")
    dx, sums1 = _rms_mod_bwd(dh1, x, dx2, n1g, sc1, ts, "rms_mod_bwd1")
    dsh1, dsc1, g_n1g = sums1[0:1], sums1[1:2], sums1[2:3]

    dmod = jnp.concatenate([dsh1, dsc1, dg1, dsh2, dsc2, dg2], axis=1)
    grads = dict(n1g=g_n1g, w_in_p=g_w_in_p, wg=g_wg_p[:GLA_RANK], bg=gb_sums[0:1], gng=gng_sums[0:1],
                 qng_lanes=qk_sums[0:1], kng_lanes=qk_sums[1:2], w_out=g_w_out, n2g=g_n2g, w_up=g_w_up,
                 conv_w=g_conv_w, conv_b=g_conv_b, w_down=g_w_down)
    return loss_row, dx, dmod, grads


def _dense(a):
    *lead, R, C = a.shape
    return a.reshape(*lead, R * C // LANE, LANE)


def _col_blocks(a):
    R, W = a.shape
    return a.reshape(R, N_DEV, W // N_DEV).transpose(1, 0, 2)


def _cols_from_blocks(a):
    n, R, C = a.shape
    return a.transpose(1, 0, 2).reshape(R, n * C)


def kernel(x, c, w_ada, b_ada, norm1_g, w_in, gla_w_gate, gla_b_gate, gla_norm_g, q_norm_g, k_norm_g, w_out, norm2_g, w_up, conv_w, conv_b, w_down, loss_target, m_w_ada, m_b_ada, m_norm1_g, m_w_in, m_gla_w_gate, m_gla_b_gate, m_gla_norm_g, m_q_norm_g, m_k_norm_g, m_w_out, m_norm2_g, m_w_up, m_conv_w, m_conv_b, m_w_down, v_w_ada, v_b_ada, v_norm1_g, v_w_in, v_gla_w_gate, v_gla_b_gate, v_gla_norm_g, v_q_norm_g, v_k_norm_g, v_w_out, v_norm2_g, v_w_up, v_conv_w, v_conv_b, v_w_down):
    axes = ("x", "y", "c")
    me = 4 * lax.axis_index("x") + 2 * lax.axis_index("y") + lax.axis_index("c")
    S, D = x.shape[1], x.shape[2]
    x2d, tgt2d = x[0], loss_target[0]
    w_in_s, w_out_s, w_up_s, w_down_s, w_ada_s = w_in[0], w_out[0], w_up[0], w_down[0], w_ada[0]
    conv_w_s, wg_s = conv_w[0], gla_w_gate[0]
    in_c, up_c, ada_c, wg_c, cw_c = w_in_s.shape[1], w_up_s.shape[1], w_ada_s.shape[1], wg_s.shape[1], conv_w_s.shape[1]
    F = w_down_s.shape[0] * N_DEV

    small = jnp.concatenate([conv_w_s.reshape(1, -1), wg_s.reshape(1, -1)], axis=1)
    n_small = small.shape[1]
    small = jnp.pad(small, ((0, 0), (0, -n_small % LANE)))
    g_c, g_in, g_out, g_up, g_down, g_small = _exchange(
        [c, _dense(w_in_s.astype(BF16)), w_out_s.astype(BF16), _dense(w_up_s.astype(BF16)), w_down_s.astype(BF16), small],
        [True] * 6, "gather_weights")
    c_all = g_c.reshape(N_DEV, D)
    w_in_full = _cols_from_blocks(g_in.reshape(N_DEV, D, in_c))
    w_in_p = jnp.concatenate([w_in_full[:, :GLR_SRC], w_in_full[:, GLR_SRC + GLA_RANK:],
                              w_in_full[:, GLR_SRC:GLR_SRC + GLA_RANK], jnp.zeros((D, LANE - GLA_RANK), BF16)], axis=1)
    w_out_full = g_out.reshape(N_DEV * w_out_s.shape[0], D)
    w_up_full = _cols_from_blocks(g_up.reshape(N_DEV, D, up_c))
    w_down_full = g_down.reshape(F, D)
    g_small = g_small.reshape(N_DEV, -1)
    conv_w_full = _cols_from_blocks(g_small[:, :3 * cw_c].reshape(N_DEV, 3, cw_c))
    wg_full = _cols_from_blocks(g_small[:, 3 * cw_c:n_small].reshape(N_DEV, GLA_RANK, wg_c))
    wg_p = jnp.concatenate([wg_full, jnp.zeros((LANE - GLA_RANK, wg_full.shape[1]), F32)], axis=0)

    b_shard = lax.dynamic_slice(b_ada, (0, me * ada_c), (1, ada_c))
    mod_part = _ada_fwd(c_all, w_ada_s, b_shard)
    mod_recv, = _exchange([mod_part.reshape(N_DEV, 1, ada_c)], [False], "exchange_mod")
    mod = mod_recv.reshape(6, D)

    loss_row, dx, dmod, gr = _local_step(
        x2d, tgt2d, mod, norm1_g, w_in_p, wg_p, gla_b_gate, gla_norm_g, q_norm_g, k_norm_g,
        w_out_full, norm2_g, w_up_full, conv_w_full, conv_b, w_down_full)
    loss = lax.psum(0.5 / D * jnp.sum(loss_row), axes)

    parts = [dmod, gr["n1g"], gr["bg"], gr["gng"], gr["qng_lanes"], gr["kng_lanes"], gr["n2g"], gr["conv_b"],
             gr["wg"].reshape(1, -1), gr["conv_w"].reshape(1, -1)]
    sizes = [p.shape[1] for p in parts]
    packed = jnp.concatenate(parts, axis=1)
    packed = jnp.pad(packed, ((0, 0), (0, -packed.shape[1] % (8 * LANE))))
    gathered, = _exchange([packed.reshape(8, -1)], [True], "gather_small_grads")
    gathered = gathered.reshape(N_DEV, -1)
    total = _sum_slots(gathered.reshape(N_DEV, 8, -1), "sum_small_grads").reshape(1, -1)
    offs = [0]
    for s_ in sizes:
        offs.append(offs[-1] + s_)
    t_dmod, t_n1g, t_bg, t_gng, t_qng, t_kng, t_n2g, t_conv_b, t_wg, t_conv_w = [
        total[:, offs[i]:offs[i + 1]] for i in range(len(sizes))]
    g_b_ada = t_dmod
    g_qng = t_qng.reshape(ATTN_HEADS, ATTN_HD).sum(axis=0, keepdims=True)
    g_kng = t_kng.reshape(ATTN_HEADS, ATTN_HD).sum(axis=0, keepdims=True)
    g_wg = lax.dynamic_slice(t_wg.reshape(GLA_RANK, -1), (0, me * wg_c), (GLA_RANK, wg_c))
    g_conv_w = lax.dynamic_slice(t_conv_w.reshape(3, -1), (0, me * cw_c), (3, cw_c))
    dmod_shard = lax.dynamic_slice(gathered[:, :6 * D], (0, me * ada_c), (N_DEV, ada_c))
    g_w_ada = _ada_bwd(c_all, dmod_shard)

    gw_in_p = gr["w_in_p"]
    gw_in = jnp.concatenate([gw_in_p[:, :GLR_SRC], gw_in_p[:, O_GLR:O_GLR + GLA_RANK], gw_in_p[:, GLR_SRC:O_GLR]], axis=1)
    r_in, r_out, r_up, r_down = _exchange(
        [_dense(_col_blocks(gw_in)), gr["w_out"].reshape(N_DEV, -1, D), _dense(_col_blocks(gr["w_up"])),
         gr["w_down"].reshape(N_DEV, -1, D)], [False] * 4, "exchange_big_grads")
    g_w_in = _sum_slots(r_in, "sum_gw_in").reshape(D, in_c)
    g_w_out = _sum_slots(r_out, "sum_gw_out")
    g_w_up = _sum_slots(r_up, "sum_gw_up").reshape(D, up_c)
    g_w_down = _sum_slots(r_down, "sum_gw_down")

    names = ["w_ada", "b_ada", "norm1_g", "w_in", "gla_w_gate", "gla_b_gate", "gla_norm_g", "q_norm_g", "k_norm_g",
             "w_out", "norm2_g", "w_up", "conv_w", "conv_b", "w_down"]
    ws = [w_ada, b_ada, norm1_g, w_in, gla_w_gate, gla_b_gate, gla_norm_g, q_norm_g, k_norm_g, w_out, norm2_g, w_up, conv_w, conv_b, w_down]
    ms = [m_w_ada, m_b_ada, m_norm1_g, m_w_in, m_gla_w_gate, m_gla_b_gate, m_gla_norm_g, m_q_norm_g, m_k_norm_g, m_w_out, m_norm2_g, m_w_up, m_conv_w, m_conv_b, m_w_down]
    vs = [v_w_ada, v_b_ada, v_norm1_g, v_w_in, v_gla_w_gate, v_gla_b_gate, v_gla_norm_g, v_q_norm_g, v_k_norm_g, v_w_out, v_norm2_g, v_w_up, v_conv_w, v_conv_b, v_w_down]
    gs = [g_w_ada, g_b_ada, t_n1g, g_w_in, g_wg, t_bg, t_gng, g_qng, g_kng, g_w_out, t_n2g, g_w_up, g_conv_w, t_conv_b, g_w_down]
    gs = [g.reshape(w.shape) for g, w in zip(gs, ws)]
    deltas, new_ms, new_vs = [], [], []
    for nm, w, g, m, v in zip(names, ws, gs, ms, vs):
        d_, m_, v_ = _adamw(w, g, m, v, "adamw_" + nm)
        deltas.append(d_)
        new_ms.append(m_)
        new_vs.append(v_)
    return (loss, dx.reshape(x.shape), *gs, *deltas, *new_ms, *new_vs)
```

```python
import functools
import math

import jax
import jax.numpy as jnp
from jax import lax
from jax.experimental import pallas as pl
from jax.experimental.pallas import tpu as pltpu

F32, BF16 = jnp.float32, jnp.bfloat16
HI = lax.Precision.HIGHEST
EPS = 1e-6
NEG = -1e30

N_DEV = 8
GLA_HEADS, GLA_DK, GLA_DV, GLA_RANK, GLA_TAU, GLA_CHUNK = 4, 64, 128, 16, 16.0, 64
ATTN_HEADS, ATTN_HD, ATTN_BLOCK = 8, 64, 128
DILATIONS = (1, 4, 16)
GLA_QK, GLA_V, ATTN_DIM = GLA_HEADS * GLA_DK, GLA_HEADS * GLA_DV, ATTN_HEADS * ATTN_HD
O_GQ, O_GK, O_GV, O_GR, O_AQ, O_AK, O_AV, O_GLR = 0, 256, 512, 1024, 1536, 2048, 2560, 3072
PROJ_W = 3200
LANE = 128
GLR_SRC = 2 * GLA_QK + 2 * GLA_V

ADAM_LR, ADAM_B1, ADAM_B2, ADAM_EPS, ADAM_WD, ADAM_STEP = 0.001, 0.9, 0.999, 1e-08, 0.01, 10

VMEM_LIMIT = 56 * 1024 * 1024
SUM_BLOCK_ELEMS = 256 * 1024


def _cp(*sem):
    return pltpu.CompilerParams(dimension_semantics=sem, vmem_limit_bytes=VMEM_LIMIT)


def _dot(a, b, dims, precision=None):
    return lax.dot_general(a, b, (dims, ((), ())), preferred_element_type=F32, precision=precision)


NN, NT, TN = ((1,), (0,)), ((1,), (1,)), ((0,), (0,))


def _sigmoid(z):
    return 1.0 / (1.0 + jnp.exp(-z))


ANY_SPEC = pl.BlockSpec(memory_space=pl.ANY)


def _exchange_shapes(arrays, gather):
    return [jax.ShapeDtypeStruct((N_DEV,) + (a.shape if g else a.shape[1:]), a.dtype) for a, g in zip(arrays, gather)]


def _exchange_sems(n):
    return [pltpu.SemaphoreType.DMA((n * (N_DEV - 1),)), pltpu.SemaphoreType.DMA((n * (N_DEV - 1),)), pltpu.SemaphoreType.DMA((n,))]


def _exchange_copies(ins, outs, gather, send_sems, recv_sems, local_sems):
    x, y, c = lax.axis_index("x"), lax.axis_index("y"), lax.axis_index("c")
    me = 4 * x + 2 * y + c
    copies = []
    for a in range(len(ins)):
        for p in range(1, N_DEV):
            px, py, pc = x ^ (p >> 2), y ^ ((p >> 1) & 1), c ^ (p & 1)
            peer = 4 * px + 2 * py + pc
            k = a * (N_DEV - 1) + p - 1
            copies.append(pltpu.make_async_remote_copy(
                src_ref=ins[a] if gather[a] else ins[a].at[peer], dst_ref=outs[a].at[me],
                send_sem=send_sems.at[k], recv_sem=recv_sems.at[k],
                device_id=(px, py, pc), device_id_type=pl.DeviceIdType.MESH))
        copies.append(pltpu.make_async_copy(ins[a] if gather[a] else ins[a].at[me], outs[a].at[me], local_sems.at[a]))
    return copies


def _exchange(arrays, gather, name):
    n = len(arrays)

    def body(*refs):
        copies = _exchange_copies(refs[:n], refs[n:2 * n], gather, *refs[2 * n:])
        for cp in copies:
            cp.start()
        for cp in copies:
            cp.wait()

    return pl.pallas_call(
        body, out_shape=_exchange_shapes(arrays, gather), in_specs=[ANY_SPEC] * n, out_specs=[ANY_SPEC] * n,
        scratch_shapes=_exchange_sems(n), name=name)(*arrays)


def _sum_slots(x, name):
    _, R, C = x.shape
    tr = max(t for t in range(8, min(SUM_BLOCK_ELEMS // C, R) + 1, 8) if R % t == 0)

    def body(x_ref, o_ref):
        acc = x_ref[0]
        for s in range(1, N_DEV):
            acc = acc + x_ref[s]
        o_ref[...] = acc

    return pl.pallas_call(
        body, grid=(R // tr,), in_specs=[pl.BlockSpec((N_DEV, tr, C), lambda i: (0, i, 0))],
        out_specs=pl.BlockSpec((tr, C), lambda i: (i, 0)), out_shape=jax.ShapeDtypeStruct((R, C), x.dtype),
        compiler_params=_cp("parallel"), name=name)(x)


def _mm(a, b, mode, tm, tn, tk, out_dtype, name, ride=None):
    if mode == NN:
        (M, K), N = a.shape, b.shape[1]
    elif mode == NT:
        (M, K), N = a.shape, b.shape[0]
    else:
        (K, M), N = a.shape, b.shape[1]
    tm, tn, tk = min(tm, M), min(tn, N), min(tk, K)
    assert M % tm == 0 and N % tn == 0 and K % tk == 0, (name, M, N, K, tm, tn, tk)
    nk = K // tk
    if mode == NN:
        a_spec = pl.BlockSpec((tm, tk), lambda i, j, k: (i, k))
        b_spec = pl.BlockSpec((tk, tn), lambda i, j, k: (k, j))
    elif mode == NT:
        a_spec = pl.BlockSpec((tm, tk), lambda i, j, k: (i, k))
        b_spec = pl.BlockSpec((tn, tk), lambda i, j, k: (j, k))
    else:
        a_spec = pl.BlockSpec((tk, tm), lambda i, j, k: (k, i))
        b_spec = pl.BlockSpec((tk, tn), lambda i, j, k: (k, j))

    ride_arrays, ride_gather = ride if ride else ([], [])
    nr = len(ride_arrays)
    grid = (M // tm, N // tn, nk)

    def body(*refs):
        a_ref, b_ref, o_ref = refs[0], refs[1], refs[2 + nr]
        scratch = refs[3 + 2 * nr:]
        if nr:
            ids = [pl.program_id(t) for t in range(3)]
            first = (ids[0] == 0) & (ids[1] == 0) & (ids[2] == 0)
            last = (ids[0] == grid[0] - 1) & (ids[1] == grid[1] - 1) & (ids[2] == grid[2] - 1)
            copies = _exchange_copies(refs[2:2 + nr], refs[3 + nr:3 + 2 * nr], ride_gather, *scratch[-3:])

            @pl.when(first)
            def _():
                for cp in copies:
                    cp.start()

        p = _dot(a_ref[...].astype(BF16), b_ref[...].astype(BF16), mode)
        if nk == 1:
            o_ref[...] = p.astype(out_dtype)
        else:
            acc_ref = scratch[0]
            k = pl.program_id(2)

            @pl.when(k == 0)
            def _():
                acc_ref[...] = p

            @pl.when(k > 0)
            def _():
                acc_ref[...] += p

            @pl.when(k == nk - 1)
            def _():
                o_ref[...] = acc_ref[...].astype(out_dtype)

        if nr:
            @pl.when(last)
            def _():
                for cp in copies:
                    cp.wait()

    outs = pl.pallas_call(
        body, grid=grid, in_specs=[a_spec, b_spec] + [ANY_SPEC] * nr,
        out_specs=[pl.BlockSpec((tm, tn), lambda i, j, k: (i, j))] + [ANY_SPEC] * nr,
        out_shape=[jax.ShapeDtypeStruct((M, N), out_dtype)] + _exchange_shapes(ride_arrays, ride_gather),
        scratch_shapes=([] if nk == 1 else [pltpu.VMEM((tm, tn), F32)]) + (_exchange_sems(nr) if nr else []),
        compiler_params=_cp(*(("arbitrary",) * 3 if nr else ("parallel", "parallel", "arbitrary"))), name=name)(a, b, *ride_arrays)
    return (outs[0], outs[1:]) if nr else outs[0]


def _ada_fwd(c_all, w_shard, b_shard):
    Nc = w_shard.shape[1]

    def body(c_ref, w_ref, b_ref, o_ref):
        cv = c_ref[...]
        o_ref[...] = _dot(cv * _sigmoid(cv), w_ref[...], NN, HI) + b_ref[...]

    return pl.pallas_call(body, out_shape=jax.ShapeDtypeStruct((N_DEV, Nc), F32), name="ada_fwd",
                          compiler_params=pltpu.CompilerParams(vmem_limit_bytes=VMEM_LIMIT))(c_all, w_shard, b_shard)


def _ada_bwd(c_all, dmod_shard):
    D, Nc = c_all.shape[1], dmod_shard.shape[1]

    def body(c_ref, d_ref, o_ref):
        cv = c_ref[...]
        o_ref[...] = _dot(cv * _sigmoid(cv), d_ref[...], TN, HI)

    return pl.pallas_call(body, out_shape=jax.ShapeDtypeStruct((D, Nc), F32), name="ada_bwd",
                          compiler_params=pltpu.CompilerParams(vmem_limit_bytes=VMEM_LIMIT))(c_all, dmod_shard)


def _row_spec(ts, D):
    return pl.BlockSpec((ts, D), lambda i: (i, 0))


def _vec_spec(D):
    return pl.BlockSpec((1, D), lambda i: (0, 0))


def _rms_mod(x, ng, sc, sh, ts, name):
    S, D = x.shape

    def body(x_ref, ng_ref, sc_ref, sh_ref, h_ref):
        xv = x_ref[...]
        r = lax.rsqrt(jnp.mean(xv * xv, axis=-1, keepdims=True) + EPS)
        h_ref[...] = (xv * r * ng_ref[...] * (1.0 + sc_ref[...]) + sh_ref[...]).astype(BF16)

    return pl.pallas_call(
        body, grid=(S // ts,), in_specs=[_row_spec(ts, D)] + [_vec_spec(D)] * 3, out_specs=_row_spec(ts, D),
        out_shape=jax.ShapeDtypeStruct((S, D), BF16), compiler_params=_cp("parallel"), name=name)(x, ng, sc, sh)


def _resid_rms_mod(x, t, g, ng, sc, sh, ts, name):
    S, D = x.shape

    def body(x_ref, t_ref, g_ref, ng_ref, sc_ref, sh_ref, x2_ref, h_ref):
        xv = x_ref[...] + g_ref[...] * t_ref[...]
        x2_ref[...] = xv
        r = lax.rsqrt(jnp.mean(xv * xv, axis=-1, keepdims=True) + EPS)
        h_ref[...] = (xv * r * ng_ref[...] * (1.0 + sc_ref[...]) + sh_ref[...]).astype(BF16)

    return pl.pallas_call(
        body, grid=(S // ts,), in_specs=[_row_spec(ts, D)] * 2 + [_vec_spec(D)] * 4,
        out_specs=[_row_spec(ts, D)] * 2,
        out_shape=[jax.ShapeDtypeStruct((S, D), F32), jax.ShapeDtypeStruct((S, D), BF16)],
        compiler_params=_cp("parallel"), name=name)(x, t, g, ng, sc, sh)


def _rms_mod_bwd(dh, xin, dres, ng, sc, ts, name, t_prev=None, g_prev=None):
    S, D = xin.shape
    chain = t_prev is not None

    def body(*refs):
        if chain:
            dh_ref, x_ref, dr_ref, ng_ref, sc_ref, t_ref, g_ref, dx_ref, sums_ref, dt_ref = refs
        else:
            dh_ref, x_ref, dr_ref, ng_ref, sc_ref, dx_ref, sums_ref = refs
        i = pl.program_id(0)
        xv, dhv = x_ref[...], dh_ref[...]
        r = lax.rsqrt(jnp.mean(xv * xv, axis=-1, keepdims=True) + EPS)
        xh = xv * r
        ngv, scv = ng_ref[...], sc_ref[...]
        dxh = dhv * (ngv * (1.0 + scv))
        dx = dr_ref[...] + r * (dxh - xh * jnp.mean(dxh * xh, axis=-1, keepdims=True))
        dx_ref[...] = dx
        dhx = dhv * xh
        rows = [jnp.sum(dhv, axis=0, keepdims=True), jnp.sum(dhx * ngv, axis=0, keepdims=True),
                jnp.sum(dhx * (1.0 + scv), axis=0, keepdims=True)]
        if chain:
            dt_ref[...] = (dx * g_ref[...]).astype(BF16)
            rows.append(jnp.sum(dx * t_ref[...], axis=0, keepdims=True))
        rows.append(jnp.zeros((8 - len(rows), D), F32))
        part = jnp.concatenate(rows, axis=0)

        @pl.when(i == 0)
        def _():
            sums_ref[...] = part

        @pl.when(i > 0)
        def _():
            sums_ref[...] += part

    row, vec = _row_spec(ts, D), _vec_spec(D)
    sums_spec = pl.BlockSpec((8, D), lambda i: (0, 0))
    ins = [dh, xin, dres, ng, sc] + ([t_prev, g_prev] if chain else [])
    in_specs = [row, row, row, vec, vec] + ([row, vec] if chain else [])
    out_specs = [row, sums_spec] + ([row] if chain else [])
    out_shape = [jax.ShapeDtypeStruct((S, D), F32), jax.ShapeDtypeStruct((8, D), F32)] + (
        [jax.ShapeDtypeStruct((S, D), BF16)] if chain else [])
    return pl.pallas_call(body, grid=(S // ts,), in_specs=in_specs, out_specs=out_specs, out_shape=out_shape,
                          compiler_params=_cp("arbitrary"), name=name)(*ins)


def _gate_fwd(proj, wg_p, bg, ts):
    S = proj.shape[0]

    def body(glr_ref, w_ref, b_ref, la_ref):
        z = _dot(glr_ref[...], w_ref[...], NN, HI) + b_ref[...]
        la_ref[...] = (jnp.minimum(z, 0.0) - jnp.log(1.0 + jnp.exp(-jnp.abs(z)))) * (1.0 / GLA_TAU)

    return pl.pallas_call(
        body, grid=(S // ts,),
        in_specs=[pl.BlockSpec((ts, LANE), lambda i: (i, O_GLR // LANE)), pl.BlockSpec((LANE, GLA_QK), lambda i: (0, 0)),
                  pl.BlockSpec((1, GLA_QK), lambda i: (0, 0))],
        out_specs=pl.BlockSpec((ts, GLA_QK), lambda i: (i, 0)), out_shape=jax.ShapeDtypeStruct((S, GLA_QK), F32),
        compiler_params=_cp("parallel"), name="gla_gate_fwd")(proj, wg_p, bg)


def _gate_bwd(dla, la, proj, wg_p, ts):
    S = proj.shape[0]

    def body(dla_ref, la_ref, glr_ref, w_ref, dglr_ref, gw_ref, gb_ref):
        i = pl.program_id(0)
        dz = dla_ref[...] * (1.0 / GLA_TAU) * (1.0 - jnp.exp(GLA_TAU * la_ref[...]))
        dglr_ref[...] = _dot(dz, w_ref[...], NT, HI).astype(BF16)
        gw = _dot(glr_ref[...], dz, TN, HI)
        gb = jnp.concatenate([jnp.sum(dz, axis=0, keepdims=True), jnp.zeros((7, GLA_QK), F32)], axis=0)

        @pl.when(i == 0)
        def _():
            gw_ref[...] = gw
            gb_ref[...] = gb

        @pl.when(i > 0)
        def _():
            gw_ref[...] += gw
            gb_ref[...] += gb

    return pl.pallas_call(
        body, grid=(S // ts,),
        in_specs=[pl.BlockSpec((ts, GLA_QK), lambda i: (i, 0)), pl.BlockSpec((ts, GLA_QK), lambda i: (i, 0)),
                  pl.BlockSpec((ts, LANE), lambda i: (i, O_GLR // LANE)), pl.BlockSpec((LANE, GLA_QK), lambda i: (0, 0))],
        out_specs=[pl.BlockSpec((ts, LANE), lambda i: (i, 0)), pl.BlockSpec((LANE, GLA_QK), lambda i: (0, 0)),
                   pl.BlockSpec((8, GLA_QK), lambda i: (0, 0))],
        out_shape=[jax.ShapeDtypeStruct((S, LANE), BF16), jax.ShapeDtypeStruct((LANE, GLA_QK), F32),
                   jax.ShapeDtypeStruct((8, GLA_QK), F32)],
        compiler_params=_cp("arbitrary"), name="gla_gate_bwd")(dla, la, proj, wg_p)


def _tri(lower):
    r = lax.broadcasted_iota(jnp.int32, (GLA_CHUNK, GLA_CHUNK), 0)
    c = lax.broadcasted_iota(jnp.int32, (GLA_CHUNK, GLA_CHUNK), 1)
    return jnp.where((r >= c) if lower else (c >= r), 1.0, 0.0).astype(F32)


GLA_SUB = 16
GLA_NSUB = GLA_CHUNK // GLA_SUB
PAIR_QK = 2 * GLA_DK
PAIR_V = 2 * GLA_DV


def _band_selector():
    r = lax.broadcasted_iota(jnp.int32, (GLA_SUB * PAIR_QK, LANE), 0)
    c = lax.broadcasted_iota(jnp.int32, (GLA_SUB * PAIR_QK, LANE), 1)
    dist, head = r // PAIR_QK, (r % PAIR_QK) // GLA_DK
    return jnp.where(c == head * GLA_DK + (GLA_SUB - 1 - dist), 1.0, 0.0).astype(BF16)


def _flip_matrix():
    r = lax.broadcasted_iota(jnp.int32, (GLA_CHUNK, GLA_CHUNK), 0)
    c = lax.broadcasted_iota(jnp.int32, (GLA_CHUNK, GLA_CHUNK), 1)
    return jnp.where(r + c == GLA_CHUNK - 1, 1.0, 0.0).astype(BF16)


def _state_mask():
    r = lax.broadcasted_iota(jnp.int32, (PAIR_V, PAIR_QK), 0)
    c = lax.broadcasted_iota(jnp.int32, (PAIR_V, PAIR_QK), 1)
    return (r < GLA_DV) == (c < GLA_DK)


class _GlaChunk:
    def __init__(self, qs, kc, vc, g, sel):
        C = GLA_CHUNK
        self.qs, self.kc, self.vc = qs, kc, vc
        rows = lax.broadcasted_iota(jnp.int32, (C, 1), 0)
        lane = lax.broadcasted_iota(jnp.int32, (1, PAIR_QK), 1)
        self.rows, self.lane = rows, lane
        b = _dot(_tri(True), g, NN, HI)
        self.bl = b[C - 1:C, :]
        self.eb = jnp.exp(b)
        self.kdec = jnp.exp(self.bl - b)
        edge = lambda J: b[GLA_SUB * (J + 1):GLA_SUB * (J + 1) + 1, :]
        self.e_far = [jnp.exp(jnp.where(rows >= GLA_SUB * (J + 1), b - edge(J), NEG)) for J in range(GLA_NSUB - 1)]
        blk = rows // GLA_SUB
        bnext = edge(0)
        for J in range(1, GLA_NSUB - 1):
            bnext = jnp.where(blk == J, edge(J), bnext)
        self.e_khat = jnp.exp(jnp.where(blk < GLA_NSUB - 1, bnext - b, NEG))
        khat = kc * self.e_khat
        k2 = jnp.concatenate([jnp.where(lane < GLA_DK, khat, 0.0), jnp.where(lane >= GLA_DK, khat, 0.0)], axis=0)
        self.blk2 = jnp.concatenate([blk, blk], axis=0)
        self.m_far = jnp.concatenate([jnp.where(self.blk2 == J, k2, 0.0) for J in range(GLA_NSUB - 1)], axis=1).astype(BF16)
        self.qcat = jnp.concatenate([qs * e for e in self.e_far], axis=1).astype(BF16)
        a_far = _dot(self.qcat, self.m_far, NT)
        self.e_band, self.rk, terms = [], [], []
        for d in range(GLA_SUB):
            rk = pltpu.roll(kc, d, 0) if d else kc
            rb = pltpu.roll(b, d, 0) if d else b
            e = jnp.exp(jnp.where(rows >= d, b - rb, NEG))
            self.e_band.append(e)
            self.rk.append(rk)
            terms.append((qs * rk * e).astype(BF16))
        band = _dot(jnp.concatenate(terms, axis=1), sel, NN)
        a_band = pltpu.roll(band, LANE - (GLA_SUB - 1), 1, stride=1, stride_axis=0)
        dist = rows - lane % GLA_DK
        self.far_mask = dist >= GLA_SUB
        self.band_mask = (dist >= 0) & (dist < GLA_SUB)
        self.a = (a_band + jnp.where(self.far_mask, a_far, 0.0)).astype(BF16)
        self.lane_v = lax.broadcasted_iota(jnp.int32, (1, PAIR_V), 1)
        self.v2 = jnp.concatenate([jnp.where(self.lane_v < GLA_DV, vc, 0.0), jnp.where(self.lane_v >= GLA_DV, vc, 0.0)],
                                  axis=0).astype(BF16)


def _gla_fwd(proj, la, tb):
    S = proj.shape[0]
    C = GLA_CHUNK
    tb = min(tb, S)
    nbc = tb // C
    npair = GLA_HEADS // 2
    scale = GLA_DK ** -0.5

    def body(q_ref, k_ref, v_ref, la_ref, sel_ref, o_ref, st_ref, state):
        @pl.when(pl.program_id(1) == 0)
        def _():
            state[...] = jnp.zeros_like(state)

        def chunk(ci, carry):
            sl = pl.ds(pl.multiple_of(ci * C, C), C)
            ch = _GlaChunk(q_ref[sl, :] * scale, k_ref[sl, :], v_ref[sl, :], la_ref[sl, :], sel_ref[...])
            st = state[...]
            st_ref[0, ci] = st
            o_ref[sl, :] = _dot((ch.qs * ch.eb).astype(BF16), st.astype(BF16), NT) + _dot(ch.a, ch.v2, NN)
            upd = _dot(ch.vc.astype(BF16), (ch.kc * ch.kdec).astype(BF16), TN)
            state[...] = st * jnp.exp(ch.bl) + jnp.where(_state_mask(), upd, 0.0)
            return carry

        lax.fori_loop(0, nbc, chunk, 0)

    qspec = lambda off: pl.BlockSpec((tb, PAIR_QK), lambda p, i: (i, off // PAIR_QK + p))
    return pl.pallas_call(
        body, grid=(npair, S // tb),
        in_specs=[qspec(O_GQ), qspec(O_GK), pl.BlockSpec((tb, PAIR_V), lambda p, i: (i, O_GV // PAIR_V + p)),
                  pl.BlockSpec((tb, PAIR_QK), lambda p, i: (i, p)),
                  pl.BlockSpec((GLA_SUB * PAIR_QK, LANE), lambda p, i: (0, 0))],
        out_specs=[pl.BlockSpec((tb, PAIR_V), lambda p, i: (i, p)),
                   pl.BlockSpec((1, nbc, PAIR_V, PAIR_QK), lambda p, i: (p, i, 0, 0))],
        out_shape=[jax.ShapeDtypeStruct((S, GLA_V), F32), jax.ShapeDtypeStruct((npair, S // C, PAIR_V, PAIR_QK), F32)],
        scratch_shapes=[pltpu.VMEM((PAIR_V, PAIR_QK), F32)],
        compiler_params=_cp("parallel", "arbitrary"), name="gla_fwd")(proj, proj, proj, la, _band_selector())


def _gla_bwd(proj, la, do, states, tb):
    S = proj.shape[0]
    C = GLA_CHUNK
    tb = min(tb, S)
    nbc = tb // C
    nblk = S // tb
    npair = GLA_HEADS // 2
    scale = GLA_DK ** -0.5

    def body(q_ref, k_ref, v_ref, la_ref, do_ref, st_ref, sel_ref, selt_ref, dq_ref, dk_ref, dv_ref, dla_ref, dstate):
        @pl.when(pl.program_id(1) == 0)
        def _():
            dstate[...] = jnp.zeros_like(dstate)

        def chunk(cc, carry):
            ci = nbc - 1 - cc
            sl = pl.ds(pl.multiple_of(ci * C, C), C)
            ch = _GlaChunk(q_ref[sl, :] * scale, k_ref[sl, :], v_ref[sl, :], la_ref[sl, :], sel_ref[...])
            qs, kc, rows = ch.qs, ch.kc, ch.rows
            doc_b = do_ref[sl, :].astype(BF16)
            st = st_ref[0, ci]
            dst = dstate[...]
            dst_b = dst.astype(BF16)
            ebl = jnp.exp(ch.bl)
            dq = _dot(doc_b, st.astype(BF16), NN) * ch.eb
            dk = _dot(ch.vc.astype(BF16), dst_b, NN) * ch.kdec
            dv = _dot((kc * ch.kdec).astype(BF16), dst_b, NT)
            dbl = jnp.sum(dst * st, axis=0, keepdims=True) * ebl + jnp.sum(kc * dk, axis=0, keepdims=True)
            da = _dot(doc_b, ch.v2, NT)
            dv2 = _dot(ch.a, doc_b, TN)
            dv = dv + jnp.where(ch.lane_v < GLA_DV, dv2[:C], dv2[C:])
            da_far = jnp.where(ch.far_mask, da, 0.0).astype(BF16)
            dqcat = _dot(da_far, ch.m_far, NN)
            dm = _dot(da_far, ch.qcat, TN)
            dk2 = jnp.zeros((2 * C, PAIR_QK), F32)
            for J in range(GLA_NSUB - 1):
                dq = dq + dqcat[:, J * PAIR_QK:(J + 1) * PAIR_QK] * ch.e_far[J]
                dk2 = dk2 + jnp.where(ch.blk2 == J, dm[:, J * PAIR_QK:(J + 1) * PAIR_QK], 0.0)
            dk = dk + jnp.where(ch.lane < GLA_DK, dk2[:C], dk2[C:]) * ch.e_khat
            flip = _flip_matrix()
            da_band = _dot(flip, jnp.where(ch.band_mask, da, 0.0).astype(BF16), NN)
            dband = pltpu.roll(da_band, LANE - (C - GLA_SUB), 1, stride=1, stride_axis=0)
            dband = _dot(flip, dband.astype(BF16), NN)
            dterms = _dot(dband.astype(BF16), selt_ref[...], NN)
            for d in range(GLA_SUB):
                dt = dterms[:, d * PAIR_QK:(d + 1) * PAIR_QK]
                dq = dq + dt * (ch.rk[d] * ch.e_band[d])
                dkr = dt * (qs * ch.e_band[d])
                dk = dk + (pltpu.roll(dkr, C - d, 0) if d else dkr)
            db = qs * dq - kc * dk
            db = jnp.where(rows == C - 1, db + dbl, db)
            dq_ref[sl, :] = (dq * scale).astype(BF16)
            dk_ref[sl, :] = dk.astype(BF16)
            dv_ref[sl, :] = dv.astype(BF16)
            dla_ref[sl, :] = _dot(_tri(False), db, NN, HI)
            upd = _dot(doc_b, (qs * ch.eb).astype(BF16), TN)
            dstate[...] = dst * ebl + jnp.where(_state_mask(), upd, 0.0)
            return carry

        lax.fori_loop(0, nbc, chunk, 0)

    rev = lambda i: nblk - 1 - i
    qspec = lambda off: pl.BlockSpec((tb, PAIR_QK), lambda p, i: (rev(i), off // PAIR_QK + p))
    pair_qk = pl.BlockSpec((tb, PAIR_QK), lambda p, i: (rev(i), p))
    pair_v = pl.BlockSpec((tb, PAIR_V), lambda p, i: (rev(i), p))
    sel = _band_selector()
    return pl.pallas_call(
        body, grid=(npair, nblk),
        in_specs=[qspec(O_GQ), qspec(O_GK), pl.BlockSpec((tb, PAIR_V), lambda p, i: (rev(i), O_GV // PAIR_V + p)),
                  pair_qk, pair_v, pl.BlockSpec((1, nbc, PAIR_V, PAIR_QK), lambda p, i: (p, rev(i), 0, 0)),
                  pl.BlockSpec((GLA_SUB * PAIR_QK, LANE), lambda p, i: (0, 0)),
                  pl.BlockSpec((LANE, GLA_SUB * PAIR_QK), lambda p, i: (0, 0))],
        out_specs=[pair_qk, pair_qk, pair_v, pair_qk],
        out_shape=[jax.ShapeDtypeStruct((S, GLA_QK), BF16), jax.ShapeDtypeStruct((S, GLA_QK), BF16),
                   jax.ShapeDtypeStruct((S, GLA_V), BF16), jax.ShapeDtypeStruct((S, GLA_QK), F32)],
        scratch_shapes=[pltpu.VMEM((PAIR_V, PAIR_QK), F32)],
        compiler_params=_cp("parallel", "arbitrary"), name="gla_bwd")(proj, proj, proj, la, do, states, sel, sel.T)


def _gla_out(o, proj, gng, ts):
    S = o.shape[0]

    def body(o_ref, gr_ref, g_ref, y_ref):
        for h in range(GLA_HEADS):
            cols = slice(h * GLA_DV, (h + 1) * GLA_DV)
            ov, grv = o_ref[:, cols], gr_ref[:, cols]
            r = lax.rsqrt(jnp.mean(ov * ov, axis=-1, keepdims=True) + EPS)
            y_ref[:, cols] = (ov * r * g_ref[...] * (grv * _sigmoid(grv))).astype(BF16)

    return pl.pallas_call(
        body, grid=(S // ts,),
        in_specs=[pl.BlockSpec((ts, GLA_V), lambda i: (i, 0)), pl.BlockSpec((ts, GLA_V), lambda i: (i, O_GR // GLA_V)),
                  pl.BlockSpec((1, GLA_DV), lambda i: (0, 0))],
        out_specs=pl.BlockSpec((ts, GLA_V), lambda i: (i, 0)), out_shape=jax.ShapeDtypeStruct((S, GLA_V), BF16),
        compiler_params=_cp("parallel"), name="gla_out_fwd")(o, proj, gng)


def _gla_out_bwd(dmixed, o, proj, gng, ts):
    S = o.shape[0]

    def body(dy_ref, o_ref, gr_ref, g_ref, do_ref, dgr_ref, gg_ref):
        i = pl.program_id(0)
        gsum = jnp.zeros((1, GLA_DV), F32)
        for h in range(GLA_HEADS):
            cols = slice(h * GLA_DV, (h + 1) * GLA_DV)
            ov, grv, dy = o_ref[:, cols], gr_ref[:, cols], dy_ref[:, cols]
            r = lax.rsqrt(jnp.mean(ov * ov, axis=-1, keepdims=True) + EPS)
            oh = ov * r
            sg = _sigmoid(grv)
            silu = grv * sg
            don = dy * silu
            dgr_ref[:, cols] = (dy * (oh * g_ref[...]) * (sg * (1.0 + grv * (1.0 - sg)))).astype(BF16)
            gsum = gsum + jnp.sum(don * oh, axis=0, keepdims=True)
            doh = don * g_ref[...]
            do_ref[:, cols] = r * (doh - oh * jnp.mean(doh * oh, axis=-1, keepdims=True))
        part = jnp.concatenate([gsum, jnp.zeros((7, GLA_DV), F32)], axis=0)

        @pl.when(i == 0)
        def _():
            gg_ref[...] = part

        @pl.when(i > 0)
        def _():
            gg_ref[...] += part

    return pl.pallas_call(
        body, grid=(S // ts,),
        in_specs=[pl.BlockSpec((ts, GLA_V), lambda i: (i, 0)), pl.BlockSpec((ts, GLA_V), lambda i: (i, 0)),
                  pl.BlockSpec((ts, GLA_V), lambda i: (i, O_GR // GLA_V)), pl.BlockSpec((1, GLA_DV), lambda i: (0, 0))],
        out_specs=[pl.BlockSpec((ts, GLA_V), lambda i: (i, 0)), pl.BlockSpec((ts, GLA_V), lambda i: (i, 0)),
                   pl.BlockSpec((8, GLA_DV), lambda i: (0, 0))],
        out_shape=[jax.ShapeDtypeStruct((S, GLA_V), F32), jax.ShapeDtypeStruct((S, GLA_V), BF16),
                   jax.ShapeDtypeStruct((8, GLA_DV), F32)],
        compiler_params=_cp("arbitrary"), name="gla_out_bwd")(dmixed, o, proj, gng)


def _seg_matrix(width, seg, value):
    r = lax.broadcasted_iota(jnp.int32, (width, width), 0) // seg
    c = lax.broadcasted_iota(jnp.int32, (width, width), 1) // seg
    return jnp.where(r == c, value, 0.0).astype(F32)


def _head_norm(proj, qg, kg, ts):
    S = proj.shape[0]
    W = ATTN_DIM

    def body(q_ref, k_ref, qg_ref, kg_ref, qn_ref, kn_ref):
        seg = _seg_matrix(W, ATTN_HD, 1.0 / ATTN_HD)
        for x_ref, g_ref, o_ref in ((q_ref, qg_ref, qn_ref), (k_ref, kg_ref, kn_ref)):
            xv = x_ref[...]
            ms = _dot(xv * xv, seg, NN, HI)
            o_ref[...] = xv * lax.rsqrt(ms + EPS) * g_ref[...]

    blk = lambda off: pl.BlockSpec((ts, W), lambda i: (i, off // W))
    out = pl.BlockSpec((ts, W), lambda i: (i, 0))
    vec = pl.BlockSpec((1, W), lambda i: (0, 0))
    return pl.pallas_call(
        body, grid=(S // ts,), in_specs=[blk(O_AQ), blk(O_AK), vec, vec], out_specs=[out] * 2,
        out_shape=[jax.ShapeDtypeStruct((S, W), F32)] * 2, compiler_params=_cp("parallel"), name="attn_head_norm")(
            proj, proj, qg, kg)


def _slope(head):
    one = jnp.ones((1, 1), jnp.int32)
    return 1.0 / jnp.left_shift(one, one * (head + 1)).astype(F32)


ATTN_GROUP = 4


def _attn_rows(d, g, r):
    start = g * d * ATTN_BLOCK + r
    return pl.ds(start, ATTN_BLOCK) if d == 1 else pl.ds(start, ATTN_BLOCK, stride=d)


def _for_blocks(d, G, fn):
    for g in range(G):
        if d <= ATTN_GROUP:
            for r in range(d):
                fn(g, r)
        else:
            def step(r, carry, g=g):
                fn(g, r)
                return carry
            lax.fori_loop(0, d, step, 0, unroll=ATTN_GROUP)


def _attn_specs(d, S):
    G = max(1, ATTN_GROUP // d)
    edge = d * ATTN_BLOCK
    tq = G * edge
    nb, n_edge = S // tq, S // edge

    def specs(off=0):
        return [pl.BlockSpec((tq, LANE), lambda hp, n: (n, off + hp)),
                pl.BlockSpec((edge, LANE), lambda hp, n: (jnp.maximum(n * G - 1, 0), off + hp)),
                pl.BlockSpec((edge, LANE), lambda hp, n: (jnp.minimum((n + 1) * G, n_edge - 1), off + hp))]

    return G, nb, specs


def _attn_fwd(qn, kn, proj, d):
    S, W = qn.shape
    B = ATTN_BLOCK
    G, nb, specs = _attn_specs(d, S)

    def body(q_ref, kp_ref, kc_ref, vp_ref, vc_ref, o_ref, l_ref):
        hp, n = pl.program_id(0), pl.program_id(1)
        lo = lax.broadcasted_iota(jnp.int32, (1, LANE), 1) < ATTN_HD
        iq = lax.broadcasted_iota(jnp.int32, (B, 2 * B), 0)
        ik = lax.broadcasted_iota(jnp.int32, (B, 2 * B), 1)
        rel = iq + B - ik
        window = (rel >= 0) & (rel <= B)
        relf = (d * rel).astype(F32)

        def sub(g, r):
            rows = _attn_rows(d, g, r)
            before = _attn_rows(d, max(g - 1, 0), r)
            kb_ref, vb_ref = (kp_ref, vp_ref) if g == 0 else (kc_ref, vc_ref)
            valid = window & ((ik >= B) | (n > 0)) if g == 0 else window
            qv = q_ref[rows, :].astype(BF16)
            kv = jnp.concatenate([kb_ref[before, :], kc_ref[rows, :]], axis=0).astype(BF16)
            vv = jnp.concatenate([vb_ref[before, :], vc_ref[rows, :]], axis=0).astype(BF16)
            outs, lses = [], []
            for h in range(2):
                qm = jnp.where(lo == (h == 0), qv, jnp.zeros_like(qv))
                s = _dot(qm, kv, NT) * (ATTN_HD ** -0.5) - _slope(hp * 2 + h) * relf
                s = jnp.where(valid, s, NEG)
                m = jnp.max(s, axis=-1, keepdims=True)
                p = jnp.exp(s - m)
                den = jnp.sum(p, axis=-1, keepdims=True)
                outs.append(_dot(p.astype(BF16), vv, NN) / den)
                lses.append(m + jnp.log(den))
            o_ref[rows, :] = jnp.where(lo, outs[0], outs[1])
            l_ref[rows, :] = jnp.where(lo, lses[0], lses[1])

        _for_blocks(d, G, sub)

    cur, prev, _ = specs()
    vcur, vprev, _ = specs(O_AV // LANE)
    return pl.pallas_call(
        body, grid=(W // LANE, nb), in_specs=[cur, prev, cur, vprev, vcur], out_specs=[cur, cur],
        out_shape=[jax.ShapeDtypeStruct((S, W), F32)] * 2,
        compiler_params=_cp("parallel", "arbitrary"), name=f"attn_fwd_d{d}")(qn, kn, kn, proj, proj)


def _attn_merge(y_gla, os_, ls_, ts):
    S, W = os_[0].shape

    def body(yg, o1, o2, o3, l1, l2, l3, mixed_ref, y_ref, lse_ref):
        a, b, c = l1[...], l2[...], l3[...]
        m = jnp.maximum(jnp.maximum(a, b), c)
        ea, eb, ec = jnp.exp(a - m), jnp.exp(b - m), jnp.exp(c - m)
        tot = ea + eb + ec
        y = (ea * o1[...] + eb * o2[...] + ec * o3[...]) / tot
        y_ref[...] = y
        mixed_ref[:, :W] = yg[...]
        mixed_ref[:, W:] = y.astype(BF16)
        lse_ref[...] = m + jnp.log(tot)

    spec = pl.BlockSpec((ts, W), lambda i: (i, 0))
    return pl.pallas_call(
        body, grid=(S // ts,), in_specs=[spec] * 7, out_specs=[pl.BlockSpec((ts, 2 * W), lambda i: (i, 0)), spec, spec],
        out_shape=[jax.ShapeDtypeStruct((S, 2 * W), BF16), jax.ShapeDtypeStruct((S, W), F32), jax.ShapeDtypeStruct((S, W), F32)],
        compiler_params=_cp("parallel"), name="attn_merge")(y_gla, *os_, *ls_)


def _attn_delta(dmixed, y, ts):
    S, W = y.shape

    def body(dy_ref, y_ref, d_ref):
        d_ref[...] = _dot(dy_ref[...] * y_ref[...], _seg_matrix(W, ATTN_HD, 1.0), NN, HI)

    return pl.pallas_call(
        body, grid=(S // ts,), in_specs=[pl.BlockSpec((ts, W), lambda i: (i, 1)), pl.BlockSpec((ts, W), lambda i: (i, 0))],
        out_specs=pl.BlockSpec((ts, W), lambda i: (i, 0)), out_shape=jax.ShapeDtypeStruct((S, W), F32),
        compiler_params=_cp("parallel"), name="attn_delta")(dmixed, y)


def _attn_dq(qn, kn, proj, dmixed, lse, delta, d):
    S, W = qn.shape
    B = ATTN_BLOCK
    G, nb, specs = _attn_specs(d, S)

    def body(q_ref, kp_ref, kc_ref, vp_ref, vc_ref, dy_ref, l_ref, de_ref, dq_ref):
        hp, n = pl.program_id(0), pl.program_id(1)
        lo = lax.broadcasted_iota(jnp.int32, (1, LANE), 1) < ATTN_HD
        iq = lax.broadcasted_iota(jnp.int32, (B, 2 * B), 0)
        ik = lax.broadcasted_iota(jnp.int32, (B, 2 * B), 1)
        rel = iq + B - ik
        window = (rel >= 0) & (rel <= B)
        relf = (d * rel).astype(F32)

        def sub(g, r):
            rows = _attn_rows(d, g, r)
            before = _attn_rows(d, max(g - 1, 0), r)
            kb_ref, vb_ref = (kp_ref, vp_ref) if g == 0 else (kc_ref, vc_ref)
            valid = window & ((ik >= B) | (n > 0)) if g == 0 else window
            qv, dyv = q_ref[rows, :].astype(BF16), dy_ref[rows, :]
            lv, dev = l_ref[rows, :], de_ref[rows, :]
            kv = jnp.concatenate([kb_ref[before, :], kc_ref[rows, :]], axis=0).astype(BF16)
            vv = jnp.concatenate([vb_ref[before, :], vc_ref[rows, :]], axis=0).astype(BF16)
            outs = []
            for h in range(2):
                sel = lo == (h == 0)
                qm = jnp.where(sel, qv, jnp.zeros_like(qv))
                dym = jnp.where(sel, dyv, 0.0).astype(BF16)
                lse_h = lv[:, h * ATTN_HD:h * ATTN_HD + 1]
                del_h = dev[:, h * ATTN_HD:h * ATTN_HD + 1]
                s = _dot(qm, kv, NT) * (ATTN_HD ** -0.5) - _slope(hp * 2 + h) * relf
                p = jnp.exp(jnp.where(valid, s, NEG) - lse_h)
                ds = p * (_dot(dym, vv, NT) - del_h)
                outs.append(_dot(ds.astype(BF16), kv, NN) * (ATTN_HD ** -0.5))
            dq_ref[rows, :] = jnp.where(lo, outs[0], outs[1])

        _for_blocks(d, G, sub)

    cur, prev, _ = specs()
    vcur, vprev, _ = specs(O_AV // LANE)
    dycur, _, _ = specs(W // LANE)
    return pl.pallas_call(
        body, grid=(W // LANE, nb), in_specs=[cur, prev, cur, vprev, vcur, dycur, cur, cur], out_specs=cur,
        out_shape=jax.ShapeDtypeStruct((S, W), F32),
        compiler_params=_cp("parallel", "arbitrary"), name=f"attn_dq_d{d}")(qn, kn, kn, proj, proj, dmixed, lse, delta)


def _attn_dkv(qn, kn, proj, dmixed, lse, delta, d):
    S, W = qn.shape
    B = ATTN_BLOCK
    G, nb, specs = _attn_specs(d, S)

    def body(k_ref, v_ref, qc_ref, qn_ref, dyc_ref, dyn_ref, lc_ref, ln_ref, dec_ref, den_ref, dk_ref, dv_ref):
        hp, n = pl.program_id(0), pl.program_id(1)
        lo = lax.broadcasted_iota(jnp.int32, (1, LANE), 1) < ATTN_HD
        iq = lax.broadcasted_iota(jnp.int32, (B, B), 0)
        ik = lax.broadcasted_iota(jnp.int32, (B, B), 1)

        def sub(g, r):
            rows = _attn_rows(d, g, r)
            kv, vv = k_ref[rows, :].astype(BF16), v_ref[rows, :].astype(BF16)
            dk = jnp.zeros((B, LANE), F32)
            dv = jnp.zeros((B, LANE), F32)
            inside = g + 1 < G
            after = _attn_rows(d, g + 1 if inside else 0, r)
            following = (qc_ref, dyc_ref, lc_ref, dec_ref) if inside else (qn_ref, dyn_ref, ln_ref, den_ref)
            for nxt, qrows, (q_ref, dy_ref, l_ref, de_ref) in ((0, rows, (qc_ref, dyc_ref, lc_ref, dec_ref)), (1, after, following)):
                rel = iq - ik + nxt * B
                valid = (rel >= 0) & (rel <= B)
                if nxt and not inside:
                    valid = valid & (n + 1 < nb)
                relf = (d * rel).astype(F32)
                qv, dyv = q_ref[qrows, :].astype(BF16), dy_ref[qrows, :]
                lv, dev = l_ref[qrows, :], de_ref[qrows, :]
                for h in range(2):
                    sel = lo == (h == 0)
                    qm = jnp.where(sel, qv, jnp.zeros_like(qv))
                    dym = jnp.where(sel, dyv, 0.0).astype(BF16)
                    lse_h = lv[:, h * ATTN_HD:h * ATTN_HD + 1]
                    del_h = dev[:, h * ATTN_HD:h * ATTN_HD + 1]
                    s = _dot(qm, kv, NT) * (ATTN_HD ** -0.5) - _slope(hp * 2 + h) * relf
                    p = jnp.exp(jnp.where(valid, s, NEG) - lse_h)
                    dv = dv + _dot(p.astype(BF16), dym, TN)
                    ds = p * (_dot(dym, vv, NT) - del_h)
                    dk = dk + _dot(ds.astype(BF16), qm, TN) * (ATTN_HD ** -0.5)
            dk_ref[rows, :] = dk
            dv_ref[rows, :] = dv

        _for_blocks(d, G, sub)

    cur, _, nxt = specs()
    vcur, _, _ = specs(O_AV // LANE)
    dycur, _, dynxt = specs(W // LANE)
    return pl.pallas_call(
        body, grid=(W // LANE, nb), in_specs=[cur, vcur, cur, nxt, dycur, dynxt, cur, nxt, cur, nxt], out_specs=[cur, cur],
        out_shape=[jax.ShapeDtypeStruct((S, W), F32)] * 2,
        compiler_params=_cp("parallel", "arbitrary"), name=f"attn_dkv_d{d}")(
            kn, proj, qn, qn, dmixed, dmixed, lse, lse, delta, delta)


def _attn_post(dqs, dks, dvs, proj, qg, kg, ts):
    S = proj.shape[0]
    W = ATTN_DIM

    def body(dq1, dq2, dq3, dk1, dk2, dk3, dv1, dv2, dv3, aq_ref, ak_ref, qg_ref, kg_ref, daq_ref, dak_ref, dav_ref, gg_ref):
        i = pl.program_id(0)
        seg = _seg_matrix(W, ATTN_HD, 1.0 / ATTN_HD)
        gsums = []
        for (d1, d2, d3), x_ref, g_ref, o_ref in (((dq1, dq2, dq3), aq_ref, qg_ref, daq_ref), ((dk1, dk2, dk3), ak_ref, kg_ref, dak_ref)):
            dy = d1[...] + d2[...] + d3[...]
            xv = x_ref[...]
            r = lax.rsqrt(_dot(xv * xv, seg, NN, HI) + EPS)
            xh = xv * r
            dxh = dy * g_ref[...]
            o_ref[...] = (r * (dxh - xh * _dot(dxh * xh, seg, NN, HI))).astype(BF16)
            gsums.append(jnp.sum(dy * xh, axis=0, keepdims=True))
        dav_ref[...] = (dv1[...] + dv2[...] + dv3[...]).astype(BF16)
        part = jnp.concatenate(gsums + [jnp.zeros((6, W), F32)], axis=0)

        @pl.when(i == 0)
        def _():
            gg_ref[...] = part

        @pl.when(i > 0)
        def _():
            gg_ref[...] += part

    row = pl.BlockSpec((ts, W), lambda i: (i, 0))
    blk = lambda off: pl.BlockSpec((ts, W), lambda i: (i, off // W))
    vec = pl.BlockSpec((1, W), lambda i: (0, 0))
    return pl.pallas_call(
        body, grid=(S // ts,), in_specs=[row] * 9 + [blk(O_AQ), blk(O_AK), vec, vec],
        out_specs=[row, row, row, pl.BlockSpec((8, W), lambda i: (0, 0))],
        out_shape=[jax.ShapeDtypeStruct((S, W), BF16)] * 3 + [jax.ShapeDtypeStruct((8, W), F32)],
        compiler_params=_cp("arbitrary"), name="attn_post")(*dqs, *dks, *dvs, proj, proj, qg, kg)


def _shift_down(cur, halo, n):
    ts = cur.shape[0]
    rows = lax.broadcasted_iota(jnp.int32, (ts, 1), 0)
    out = pltpu.roll(cur, n, 0)
    for t in range(n):
        out = jnp.where(rows == t, halo[8 - n + t:8 - n + t + 1, :], out)
    return out


def _shift_up(cur, halo, n):
    ts = cur.shape[0]
    rows = lax.broadcasted_iota(jnp.int32, (ts, 1), 0)
    out = pltpu.roll(cur, ts - n, 0)
    for t in range(n):
        out = jnp.where(rows == ts - n + t, halo[t:t + 1, :], out)
    return out


def _conv(cur, halo, w, b):
    return b + w[0:1, :] * _shift_down(cur, halo, 2) + w[1:2, :] * _shift_down(cur, halo, 1) + w[2:3, :] * cur


def _conv_swiglu(u0, conv_w8, conv_b, ts, tc):
    S, F2 = u0.shape
    F = F2 // 2
    nc = F // tc
    hb = ts // 8

    def body(ug_ref, ugh_ref, uv_ref, uvh_ref, wg_ref, wv_ref, bg_ref, bv_ref, a_ref):
        first = pl.program_id(0) == 0
        ugh = jnp.where(first, 0.0, ugh_ref[...])
        uvh = jnp.where(first, 0.0, uvh_ref[...])
        g = _conv(ug_ref[...], ugh, wg_ref[...], bg_ref[...])
        v = _conv(uv_ref[...], uvh, wv_ref[...], bv_ref[...])
        a_ref[...] = (g * _sigmoid(g) * v).astype(BF16)

    main = lambda off: pl.BlockSpec((ts, tc), lambda i, j: (i, j + off))
    halo = lambda off: pl.BlockSpec((8, tc), lambda i, j: (jnp.maximum(i * hb - 1, 0), j + off))
    wspec = lambda off: pl.BlockSpec((8, tc), lambda i, j: (0, j + off))
    bspec = lambda off: pl.BlockSpec((1, tc), lambda i, j: (0, j + off))
    return pl.pallas_call(
        body, grid=(S // ts, nc),
        in_specs=[main(0), halo(0), main(nc), halo(nc), wspec(0), wspec(nc), bspec(0), bspec(nc)],
        out_specs=pl.BlockSpec((ts, tc), lambda i, j: (i, j)), out_shape=jax.ShapeDtypeStruct((S, F), BF16),
        compiler_params=_cp("parallel", "parallel"), name="conv_swiglu")(u0, u0, u0, u0, conv_w8, conv_w8, conv_b, conv_b)


def _ffn_du(da, u0, conv_w8, conv_b, ts, tc):
    S, F2 = u0.shape
    F = F2 // 2
    nc = F // tc
    hb = ts // 8

    def body(da_ref, uo_ref, uoh_ref, up_ref, uph_ref, wo_ref, wp_ref, bo_ref, bp_ref, du_ref, sums_ref):
        j, i = pl.program_id(0), pl.program_id(1)
        first = i == 0
        uo, uoh = uo_ref[...], jnp.where(first, 0.0, uoh_ref[...])
        s2, s1 = _shift_down(uo, uoh, 2), _shift_down(uo, uoh, 1)
        wo = wo_ref[...]
        own = bo_ref[...] + wo[0:1, :] * s2 + wo[1:2, :] * s1 + wo[2:3, :] * uo
        partner = _conv(up_ref[...], jnp.where(first, 0.0, uph_ref[...]), wp_ref[...], bp_ref[...])
        dav = da_ref[...]
        sg_own, sg_par = _sigmoid(own), _sigmoid(partner)
        du_gate = dav * partner * (sg_own * (1.0 + own * (1.0 - sg_own)))
        du_val = dav * (partner * sg_par)
        du = jnp.where(j < nc, du_gate, du_val)
        du_ref[...] = du
        part = jnp.concatenate([jnp.sum(du * s2, axis=0, keepdims=True), jnp.sum(du * s1, axis=0, keepdims=True),
                                jnp.sum(du * uo, axis=0, keepdims=True), jnp.sum(du, axis=0, keepdims=True),
                                jnp.zeros((4, tc), F32)], axis=0)

        @pl.when(first)
        def _():
            sums_ref[...] = part

        @pl.when(i > 0)
        def _():
            sums_ref[...] += part

    partner = lambda j: (j + nc) % (2 * nc)
    main_o = pl.BlockSpec((ts, tc), lambda j, i: (i, j))
    halo_o = pl.BlockSpec((8, tc), lambda j, i: (jnp.maximum(i * hb - 1, 0), j))
    main_p = pl.BlockSpec((ts, tc), lambda j, i: (i, partner(j)))
    halo_p = pl.BlockSpec((8, tc), lambda j, i: (jnp.maximum(i * hb - 1, 0), partner(j)))
    return pl.pallas_call(
        body, grid=(2 * nc, S // ts),
        in_specs=[pl.BlockSpec((ts, tc), lambda j, i: (i, j % nc)), main_o, halo_o, main_p, halo_p,
                  pl.BlockSpec((8, tc), lambda j, i: (0, j)), pl.BlockSpec((8, tc), lambda j, i: (0, partner(j))),
                  pl.BlockSpec((1, tc), lambda j, i: (0, j)), pl.BlockSpec((1, tc), lambda j, i: (0, partner(j)))],
        out_specs=[main_o, pl.BlockSpec((8, tc), lambda j, i: (0, j))],
        out_shape=[jax.ShapeDtypeStruct((S, F2), F32), jax.ShapeDtypeStruct((8, F2), F32)],
        compiler_params=_cp("parallel", "arbitrary"), name="ffn_du")(da, u0, u0, u0, u0, conv_w8, conv_w8, conv_b, conv_b)


def _ffn_du0(du, conv_w8, ts, tc):
    S, F2 = du.shape
    hb = ts // 8
    nrow = S // ts

    def body(du_ref, duh_ref, w_ref, o_ref):
        last = pl.program_id(0) == nrow - 1
        cur, halo, w = du_ref[...], jnp.where(last, 0.0, duh_ref[...]), w_ref[...]
        o_ref[...] = (w[2:3, :] * cur + w[1:2, :] * _shift_up(cur, halo, 1) + w[0:1, :] * _shift_up(cur, halo, 2)).astype(BF16)

    return pl.pallas_call(
        body, grid=(nrow, F2 // tc),
        in_specs=[pl.BlockSpec((ts, tc), lambda i, j: (i, j)),
                  pl.BlockSpec((8, tc), lambda i, j: (jnp.minimum((i + 1) * hb, S // 8 - 1), j)),
                  pl.BlockSpec((8, tc), lambda i, j: (0, j))],
        out_specs=pl.BlockSpec((ts, tc), lambda i, j: (i, j)), out_shape=jax.ShapeDtypeStruct((S, F2), BF16),
        compiler_params=_cp("parallel", "parallel"), name="ffn_du0")(du, du, conv_w8)


def _loss_resid(x2, t2, g2, target, ts):
    S, D = x2.shape

    def body(x_ref, t_ref, g_ref, y_ref, dx_ref, dt_ref, sums_ref):
        i = pl.program_id(0)
        tv, gv = t_ref[...], g_ref[...]
        e = x_ref[...] + gv * tv - y_ref[...]
        dx = e * (1.0 / D)
        dx_ref[...] = dx
        dt_ref[...] = (dx * gv).astype(BF16)
        part = jnp.concatenate([jnp.sum(e * e, axis=0, keepdims=True), jnp.sum(dx * tv, axis=0, keepdims=True),
                                jnp.zeros((6, D), F32)], axis=0)

        @pl.when(i == 0)
        def _():
            sums_ref[...] = part

        @pl.when(i > 0)
        def _():
            sums_ref[...] += part

    row, vec = _row_spec(ts, D), _vec_spec(D)
    return pl.pallas_call(
        body, grid=(S // ts,), in_specs=[row, row, vec, row], out_specs=[row, row, pl.BlockSpec((8, D), lambda i: (0, 0))],
        out_shape=[jax.ShapeDtypeStruct((S, D), F32), jax.ShapeDtypeStruct((S, D), BF16), jax.ShapeDtypeStruct((8, D), F32)],
        compiler_params=_cp("arbitrary"), name="loss_resid")(x2, t2, g2, target)


def _adamw(w, g, m, v, name):
    shape = w.shape
    n = math.prod(shape)
    view = (n // LANE, LANE) if n % LANE == 0 else (math.prod(shape[:-1]), shape[-1])
    R, C = view
    tr = R
    for cand in (1024, 512, 256):
        if R > cand and R % cand == 0:
            tr = cand
            break

    def body(w_ref, g_ref, m_ref, v_ref, d_ref, nm_ref, nv_ref):
        gv = g_ref[...]
        nm = ADAM_B1 * m_ref[...] + (1.0 - ADAM_B1) * gv
        nv = ADAM_B2 * v_ref[...] + (1.0 - ADAM_B2) * (gv * gv)
        m_hat = nm / (1.0 - ADAM_B1 ** ADAM_STEP)
        v_hat = nv / (1.0 - ADAM_B2 ** ADAM_STEP)
        d_ref[...] = -ADAM_LR * (m_hat / (jnp.sqrt(v_hat) + ADAM_EPS) + ADAM_WD * w_ref[...])
        nm_ref[...] = nm
        nv_ref[...] = nv

    spec = pl.BlockSpec((tr, C), lambda i: (i, 0))
    outs = pl.pallas_call(
        body, grid=(R // tr,), in_specs=[spec] * 4, out_specs=[spec] * 3, out_shape=[jax.ShapeDtypeStruct(view, F32)] * 3,
        compiler_params=_cp("parallel"), name=name)(*[a.reshape(view) for a in (w, g, m, v)])
    return [o.reshape(shape) for o in outs]


def _pad_rows8(a):
    return jnp.concatenate([a, jnp.zeros((8 - a.shape[0], a.shape[1]), a.dtype)], axis=0)


def _local_step(x, target, mod, n1g, w_in_p, wg_p, bg, gng, qng, kng, w_out_s, n2g, w_up_s, conv_w, conv_b, w_down_s):
    S, D = x.shape
    F = w_down_s.shape[0] * N_DEV
    ts = min(512, S)
    sh1, sc1, g1, sh2, sc2, g2 = [mod[i:i + 1] for i in range(6)]
    conv_w8 = _pad_rows8(conv_w)
    qg_t, kg_t = jnp.tile(qng, (1, ATTN_HEADS)), jnp.tile(kng, (1, ATTN_HEADS))

    h1 = _rms_mod(x, n1g, sc1, sh1, ts, "rms_mod1")
    proj, (g_out, g_up) = _mm(h1, w_in_p, NN, 512, 640, 1024, F32, "mm_in", ride=([w_out_s, _dense(w_up_s)], [True, True]))
    w_out = g_out.reshape(-1, D)
    w_up = _cols_from_blocks(g_up.reshape(N_DEV, D, -1))
    la = _gate_fwd(proj, wg_p, bg, ts)
    o_gla, states = _gla_fwd(proj, la, 512)
    y_gla = _gla_out(o_gla, proj, gng, ts)
    qn, kn = _head_norm(proj, qg_t, kg_t, ts)
    branches = [_attn_fwd(qn, kn, proj, d) for d in DILATIONS]
    mixed, y_att, lse = _attn_merge(y_gla, [b[0] for b in branches], [b[1] for b in branches], ts)
    t1 = _mm(mixed, w_out, NN, 512, 1024, 1024, F32, "mm_out")
    x2, h2 = _resid_rms_mod(x, t1, g1, n2g, sc2, sh2, ts, "resid_rms_mod2")
    u0, (g_down,) = _mm(h2, w_up, NN, 512, 1408, 1024, F32, "mm_up", ride=([w_down_s], [True]))
    w_down = g_down.reshape(F, D)
    tc = 1408 if F % 1408 == 0 else F
    a = _conv_swiglu(u0, conv_w8, conv_b, min(256, S), tc)
    t2 = _mm(a, w_down, NN, 512, 1024, F, F32, "mm_down")
    dx3, dt2, sums3 = _loss_resid(x2, t2, g2, target, ts)
    loss_row, dg2 = sums3[0:1], sums3[1:2]

    g_w_down = _mm(a, dt2, TN, 1408, 1024, 512, F32, "mm_gw_down")
    da, (r_down,) = _mm(dt2, w_down, NT, 512, 1408, 1024, F32, "mm_da", ride=([g_w_down.reshape(N_DEV, -1, D)], [False]))
    du, conv_sums = _ffn_du(da, u0, conv_w8, conv_b, min(256, S), tc)
    g_conv_w, g_conv_b = conv_sums[0:3], conv_sums[3:4]
    du0 = _ffn_du0(du, conv_w8, min(256, S), tc)
    g_w_up = _mm(h2, du0, TN, 1024, 1408, 512, F32, "mm_gw_up")
    dh2, (r_up,) = _mm(du0, w_up, NT, 512, 1024, 1408, F32, "mm_dh2", ride=([_dense(_col_blocks(g_w_up))], [False]))
    dx2, sums2, dt1 = _rms_mod_bwd(dh2, x2, dx3, n2g, sc2, ts, "rms_mod_bwd2", t_prev=t1, g_prev=g1)
    dsh2, dsc2, g_n2g, dg1 = sums2[0:1], sums2[1:2], sums2[2:3], sums2[3:4]
    g_w_out = _mm(mixed, dt1, TN, 1024, 1024, 512, F32, "mm_gw_out")
    dmixed, (r_out,) = _mm(dt1, w_out, NT, 512, 1024, 1024, F32, "mm_dmixed", ride=([g_w_out.reshape(N_DEV, -1, D)], [False]))
    do_gla, dgr, gng_sums = _gla_out_bwd(dmixed, o_gla, proj, gng, ts)
    dgq, dgk, dgv, dla = _gla_bwd(proj, la, do_gla, states, 512)
    dglr, g_wg_p, gb_sums = _gate_bwd(dla, la, proj, wg_p, ts)
    delta = _attn_delta(dmixed, y_att, ts)
    dqs = [_attn_dq(qn, kn, proj, dmixed, lse, delta, d) for d in DILATIONS]
    dkvs = [_attn_dkv(qn, kn, proj, dmixed, lse, delta, d) for d in DILATIONS]
    daq, dak, dav, qk_sums = _attn_post(dqs, [t[0] for t in dkvs], [t[1] for t in dkvs], proj, qg_t, kg_t, ts)
    dproj = jnp.concatenate([dgq, dgk, dgv, dgr, daq, dak, dav, dglr], axis=1)
    g_w_in_p = _mm(h1, dproj, TN, 1024, 640, 512, F32, "mm_gw_in")
    g_w_in = jnp.concatenate([g_w_in_p[:, :GLR_SRC], g_w_in_p[:, O_GLR:O_GLR + GLA_RANK], g_w_in_p[:, GLR_SRC:O_GLR]], axis=1)
    dh1, (r_in,) = _mm(dproj, w_in_p, NT, 512, 1024, 640, F32, "mm_dh1", ride=([_dense(_col_blocks(g_w_in))], [False]))
    dx, sums1 = _rms_mod_bwd(dh1, x, dx2, n1g, sc1, ts, "rms_mod_bwd1")
    dsh1, dsc1, g_n1g = sums1[0:1], sums1[1:2], sums1[2:3]

    dmod = jnp.concatenate([dsh1, dsc1, dg1, dsh2, dsc2, dg2], axis=1)
    grads = dict(n1g=g_n1g, w_in=r_in, wg=g_wg_p[:GLA_RANK], bg=gb_sums[0:1], gng=gng_sums[0:1],
                 qng_lanes=qk_sums[0:1], kng_lanes=qk_sums[1:2], w_out=r_out, n2g=g_n2g, w_up=r_up,
                 conv_w=g_conv_w, conv_b=g_conv_b, w_down=r_down)
    return loss_row, dx, dmod, grads


def _dense(a):
    *lead, R, C = a.shape
    return a.reshape(*lead, R * C // LANE, LANE)


def _col_blocks(a):
    R, W = a.shape
    return a.reshape(R, N_DEV, W // N_DEV).transpose(1, 0, 2)


def _cols_from_blocks(a):
    n, R, C = a.shape
    return a.transpose(1, 0, 2).reshape(R, n * C)


def kernel(x, c, w_ada, b_ada, norm1_g, w_in, gla_w_gate, gla_b_gate, gla_norm_g, q_norm_g, k_norm_g, w_out, norm2_g, w_up, conv_w, conv_b, w_down, loss_target, m_w_ada, m_b_ada, m_norm1_g, m_w_in, m_gla_w_gate, m_gla_b_gate, m_gla_norm_g, m_q_norm_g, m_k_norm_g, m_w_out, m_norm2_g, m_w_up, m_conv_w, m_conv_b, m_w_down, v_w_ada, v_b_ada, v_norm1_g, v_w_in, v_gla_w_gate, v_gla_b_gate, v_gla_norm_g, v_q_norm_g, v_k_norm_g, v_w_out, v_norm2_g, v_w_up, v_conv_w, v_conv_b, v_w_down):
    axes = ("x", "y", "c")
    me = 4 * lax.axis_index("x") + 2 * lax.axis_index("y") + lax.axis_index("c")
    S, D = x.shape[1], x.shape[2]
    x2d, tgt2d = x[0], loss_target[0]
    w_in_s, w_out_s, w_up_s, w_down_s, w_ada_s = w_in[0], w_out[0], w_up[0], w_down[0], w_ada[0]
    conv_w_s, wg_s = conv_w[0], gla_w_gate[0]
    in_c, up_c, ada_c, wg_c, cw_c = w_in_s.shape[1], w_up_s.shape[1], w_ada_s.shape[1], wg_s.shape[1], conv_w_s.shape[1]
    F = w_down_s.shape[0] * N_DEV

    small = jnp.concatenate([conv_w_s.reshape(1, -1), wg_s.reshape(1, -1)], axis=1)
    n_small = small.shape[1]
    small = jnp.pad(small, ((0, 0), (0, -n_small % LANE)))
    g_c, g_in, g_small = _exchange([c, _dense(w_in_s.astype(BF16)), small], [True] * 3, "gather_w_in")
    c_all = g_c.reshape(N_DEV, D)
    w_in_full = _cols_from_blocks(g_in.reshape(N_DEV, D, in_c))
    w_in_p = jnp.concatenate([w_in_full[:, :GLR_SRC], w_in_full[:, GLR_SRC + GLA_RANK:],
                              w_in_full[:, GLR_SRC:GLR_SRC + GLA_RANK], jnp.zeros((D, LANE - GLA_RANK), BF16)], axis=1)
    g_small = g_small.reshape(N_DEV, -1)
    conv_w_full = _cols_from_blocks(g_small[:, :3 * cw_c].reshape(N_DEV, 3, cw_c))
    wg_full = _cols_from_blocks(g_small[:, 3 * cw_c:n_small].reshape(N_DEV, GLA_RANK, wg_c))
    wg_p = jnp.concatenate([wg_full, jnp.zeros((LANE - GLA_RANK, wg_full.shape[1]), F32)], axis=0)

    b_shard = lax.dynamic_slice(b_ada, (0, me * ada_c), (1, ada_c))
    mod_part = _ada_fwd(c_all, w_ada_s, b_shard)
    mod_recv, = _exchange([mod_part.reshape(N_DEV, 1, ada_c)], [False], "exchange_mod")
    mod = mod_recv.reshape(6, D)

    loss_row, dx, dmod, gr = _local_step(
        x2d, tgt2d, mod, norm1_g, w_in_p, wg_p, gla_b_gate, gla_norm_g, q_norm_g, k_norm_g,
        w_out_s.astype(BF16), norm2_g, w_up_s.astype(BF16), conv_w_full, conv_b, w_down_s.astype(BF16))
    loss = lax.psum(0.5 / D * jnp.sum(loss_row), axes)

    parts = [dmod, gr["n1g"], gr["bg"], gr["gng"], gr["qng_lanes"], gr["kng_lanes"], gr["n2g"], gr["conv_b"],
             gr["wg"].reshape(1, -1), gr["conv_w"].reshape(1, -1)]
    sizes = [p.shape[1] for p in parts]
    packed = jnp.concatenate(parts, axis=1)
    packed = jnp.pad(packed, ((0, 0), (0, -packed.shape[1] % (8 * LANE))))
    gathered, = _exchange([packed.reshape(8, -1)], [True], "gather_small_grads")
    gathered = gathered.reshape(N_DEV, -1)
    total = _sum_slots(gathered.reshape(N_DEV, 8, -1), "sum_small_grads").reshape(1, -1)
    offs = [0]
    for s_ in sizes:
        offs.append(offs[-1] + s_)
    t_dmod, t_n1g, t_bg, t_gng, t_qng, t_kng, t_n2g, t_conv_b, t_wg, t_conv_w = [
        total[:, offs[i]:offs[i + 1]] for i in range(len(sizes))]
    g_b_ada = t_dmod
    g_qng = t_qng.reshape(ATTN_HEADS, ATTN_HD).sum(axis=0, keepdims=True)
    g_kng = t_kng.reshape(ATTN_HEADS, ATTN_HD).sum(axis=0, keepdims=True)
    g_wg = lax.dynamic_slice(t_wg.reshape(GLA_RANK, -1), (0, me * wg_c), (GLA_RANK, wg_c))
    g_conv_w = lax.dynamic_slice(t_conv_w.reshape(3, -1), (0, me * cw_c), (3, cw_c))
    dmod_shard = lax.dynamic_slice(gathered[:, :6 * D], (0, me * ada_c), (N_DEV, ada_c))
    g_w_ada = _ada_bwd(c_all, dmod_shard)

    g_w_in = _sum_slots(gr["w_in"], "sum_gw_in").reshape(D, in_c)
    g_w_out = _sum_slots(gr["w_out"], "sum_gw_out")
    g_w_up = _sum_slots(gr["w_up"], "sum_gw_up").reshape(D, up_c)
    g_w_down = _sum_slots(gr["w_down"], "sum_gw_down")

    names = ["w_ada", "b_ada", "norm1_g", "w_in", "gla_w_gate", "gla_b_gate", "gla_norm_g", "q_norm_g", "k_norm_g",
             "w_out", "norm2_g", "w_up", "conv_w", "conv_b", "w_down"]
    ws = [w_ada, b_ada, norm1_g, w_in, gla_w_gate, gla_b_gate, gla_norm_g, q_norm_g, k_norm_g, w_out, norm2_g, w_up, conv_w, conv_b, w_down]
    ms = [m_w_ada, m_b_ada, m_norm1_g, m_w_in, m_gla_w_gate, m_gla_b_gate, m_gla_norm_g, m_q_norm_g, m_k_norm_g, m_w_out, m_norm2_g, m_w_up, m_conv_w, m_conv_b, m_w_down]
    vs = [v_w_ada, v_b_ada, v_norm1_g, v_w_in, v_gla_w_gate, v_gla_b_gate, v_gla_norm_g, v_q_norm_g, v_k_norm_g, v_w_out, v_norm2_g, v_w_up, v_conv_w, v_conv_b, v_w_down]
    gs = [g_w_ada, g_b_ada, t_n1g, g_w_in, g_wg, t_bg, t_gng, g_qng, g_kng, g_w_out, t_n2g, g_w_up, g_conv_w, t_conv_b, g_w_down]
    gs = [g.reshape(w.shape) for g, w in zip(gs, ws)]
    deltas, new_ms, new_vs = [], [], []
    for nm, w, g, m, v in zip(names, ws, gs, ms, vs):
        d_, m_, v_ = _adamw(w, g, m, v, "adamw_" + nm)
        deltas.append(d_)
        new_ms.append(m_)
        new_vs.append(v_)
    return (loss, dx.reshape(x.shape), *gs, *deltas, *new_ms, *new_vs)
```

```python
import functools
import math

import jax
import jax.numpy as jnp
from jax import lax
from jax.experimental import pallas as pl
from jax.experimental.pallas import tpu as pltpu

F32, BF16 = jnp.float32, jnp.bfloat16
HI = lax.Precision.HIGHEST
EPS = 1e-6
NEG = -1e30

N_DEV = 8
GLA_HEADS, GLA_DK, GLA_DV, GLA_RANK, GLA_TAU, GLA_CHUNK = 4, 64, 128, 16, 16.0, 64
ATTN_HEADS, ATTN_HD, ATTN_BLOCK = 8, 64, 128
DILATIONS = (1, 4, 16)
GLA_QK, GLA_V, ATTN_DIM = GLA_HEADS * GLA_DK, GLA_HEADS * GLA_DV, ATTN_HEADS * ATTN_HD
O_GQ, O_GK, O_GV, O_GR, O_AQ, O_AK, O_AV, O_GLR = 0, 256, 512, 1024, 1536, 2048, 2560, 3072
PROJ_W = 3328
LANE = 128
GLR_SRC = 2 * GLA_QK + 2 * GLA_V

ADAM_LR, ADAM_B1, ADAM_B2, ADAM_EPS, ADAM_WD, ADAM_STEP = 0.001, 0.9, 0.999, 1e-08, 0.01, 10

VMEM_LIMIT = 56 * 1024 * 1024
SUM_BLOCK_ELEMS = 256 * 1024


def _cp(*sem):
    return pltpu.CompilerParams(dimension_semantics=sem, vmem_limit_bytes=VMEM_LIMIT)


def _dot(a, b, dims, precision=None):
    return lax.dot_general(a, b, (dims, ((), ())), preferred_element_type=F32, precision=precision)


NN, NT, TN = ((1,), (0,)), ((1,), (1,)), ((0,), (0,))


def _sigmoid(z):
    return 1.0 / (1.0 + jnp.exp(-z))


ANY_SPEC = pl.BlockSpec(memory_space=pl.ANY)


def _exchange_shapes(arrays, gather):
    return [jax.ShapeDtypeStruct((N_DEV,) + (a.shape if g else a.shape[1:]), a.dtype) for a, g in zip(arrays, gather)]


def _exchange_sems(n):
    return [pltpu.SemaphoreType.DMA((n * (N_DEV - 1),)), pltpu.SemaphoreType.DMA((n * (N_DEV - 1),)), pltpu.SemaphoreType.DMA((n,))]


def _exchange_copies(ins, outs, gather, send_sems, recv_sems, local_sems):
    x, y, c = lax.axis_index("x"), lax.axis_index("y"), lax.axis_index("c")
    me = 4 * x + 2 * y + c
    copies = []
    for a in range(len(ins)):
        for p in range(1, N_DEV):
            px, py, pc = x ^ (p >> 2), y ^ ((p >> 1) & 1), c ^ (p & 1)
            peer = 4 * px + 2 * py + pc
            k = a * (N_DEV - 1) + p - 1
            copies.append(pltpu.make_async_remote_copy(
                src_ref=ins[a] if gather[a] else ins[a].at[peer], dst_ref=outs[a].at[me],
                send_sem=send_sems.at[k], recv_sem=recv_sems.at[k],
                device_id=(px, py, pc), device_id_type=pl.DeviceIdType.MESH))
        copies.append(pltpu.make_async_copy(ins[a] if gather[a] else ins[a].at[me], outs[a].at[me], local_sems.at[a]))
    return copies


def _riding(body, n_in, n_out, gather, grid):
    nr = len(gather)
    if not nr:
        return body

    def wrapped(*refs):
        ins, r_ins = refs[:n_in], refs[n_in:n_in + nr]
        outs, r_outs = refs[n_in + nr:n_in + nr + n_out], refs[n_in + nr + n_out:n_in + 2 * nr + n_out]
        scratch = refs[n_in + 2 * nr + n_out:]
        first = last = None
        for t, steps in enumerate(grid):
            pid = pl.program_id(t)
            first = (pid == 0) if first is None else first & (pid == 0)
            last = (pid == steps - 1) if last is None else last & (pid == steps - 1)
        copies = _exchange_copies(r_ins, r_outs, gather, *scratch[-3:])

        @pl.when(first)
        def _():
            for cp in copies:
                cp.start()

        body(*ins, *outs, *scratch[:-3])

        @pl.when(last)
        def _():
            for cp in copies:
                cp.wait()

    return wrapped


def _exchange(arrays, gather, name):
    n = len(arrays)

    def body(*refs):
        copies = _exchange_copies(refs[:n], refs[n:2 * n], gather, *refs[2 * n:])
        for cp in copies:
            cp.start()
        for cp in copies:
            cp.wait()

    return pl.pallas_call(
        body, out_shape=_exchange_shapes(arrays, gather), in_specs=[ANY_SPEC] * n, out_specs=[ANY_SPEC] * n,
        scratch_shapes=_exchange_sems(n), name=name)(*arrays)


def _sum_slots(x, name):
    _, R, C = x.shape
    tr = max(t for t in range(8, min(SUM_BLOCK_ELEMS // C, R) + 1, 8) if R % t == 0)

    def body(x_ref, o_ref):
        acc = x_ref[0]
        for s in range(1, N_DEV):
            acc = acc + x_ref[s]
        o_ref[...] = acc

    return pl.pallas_call(
        body, grid=(R // tr,), in_specs=[pl.BlockSpec((N_DEV, tr, C), lambda i: (0, i, 0))],
        out_specs=pl.BlockSpec((tr, C), lambda i: (i, 0)), out_shape=jax.ShapeDtypeStruct((R, C), x.dtype),
        compiler_params=_cp("parallel"), name=name)(x)


def _mm(a, b, mode, tm, tn, tk, out_dtype, name, ride=None):
    if mode == NN:
        (M, K), N = a.shape, b.shape[1]
    elif mode == NT:
        (M, K), N = a.shape, b.shape[0]
    else:
        (K, M), N = a.shape, b.shape[1]
    tm, tn, tk = min(tm, M), min(tn, N), min(tk, K)
    assert M % tm == 0 and N % tn == 0 and K % tk == 0, (name, M, N, K, tm, tn, tk)
    nk = K // tk
    if mode == NN:
        a_spec = pl.BlockSpec((tm, tk), lambda i, j, k: (i, k))
        b_spec = pl.BlockSpec((tk, tn), lambda i, j, k: (k, j))
    elif mode == NT:
        a_spec = pl.BlockSpec((tm, tk), lambda i, j, k: (i, k))
        b_spec = pl.BlockSpec((tn, tk), lambda i, j, k: (j, k))
    else:
        a_spec = pl.BlockSpec((tk, tm), lambda i, j, k: (k, i))
        b_spec = pl.BlockSpec((tk, tn), lambda i, j, k: (k, j))

    ride_arrays, ride_gather = ride if ride else ([], [])
    nr = len(ride_arrays)
    grid = (M // tm, N // tn, nk)

    own_acc = nk > 1 and out_dtype != F32

    def body(a_ref, b_ref, o_ref, *acc):
        p = _dot(a_ref[...].astype(BF16), b_ref[...].astype(BF16), mode)
        if nk == 1:
            o_ref[...] = p.astype(out_dtype)
        else:
            acc_ref = acc[0] if own_acc else o_ref
            k = pl.program_id(2)

            @pl.when(k == 0)
            def _():
                acc_ref[...] = p

            @pl.when(k > 0)
            def _():
                acc_ref[...] += p

            if own_acc:
                @pl.when(k == nk - 1)
                def _():
                    o_ref[...] = acc_ref[...].astype(out_dtype)

    outs = pl.pallas_call(
        _riding(body, 2, 1, ride_gather, grid), grid=grid, in_specs=[a_spec, b_spec] + [ANY_SPEC] * nr,
        out_specs=[pl.BlockSpec((tm, tn), lambda i, j, k: (i, j))] + [ANY_SPEC] * nr,
        out_shape=[jax.ShapeDtypeStruct((M, N), out_dtype)] + _exchange_shapes(ride_arrays, ride_gather),
        scratch_shapes=([pltpu.VMEM((tm, tn), F32)] if own_acc else []) + (_exchange_sems(nr) if nr else []),
        compiler_params=_cp(*(("arbitrary",) * 3 if nr else ("parallel", "parallel", "arbitrary"))), name=name)(a, b, *ride_arrays)
    return (outs[0], outs[1:]) if nr else outs[0]


def _ada_fwd(c_all, w_shard, b_shard):
    Nc = w_shard.shape[1]

    def body(c_ref, w_ref, b_ref, o_ref):
        cv = c_ref[...]
        o_ref[...] = _dot(cv * _sigmoid(cv), w_ref[...], NN, HI) + b_ref[...]

    return pl.pallas_call(body, out_shape=jax.ShapeDtypeStruct((N_DEV, Nc), F32), name="ada_fwd",
                          compiler_params=pltpu.CompilerParams(vmem_limit_bytes=VMEM_LIMIT))(c_all, w_shard, b_shard)


def _ada_bwd(c_all, dmod_shard):
    D, Nc = c_all.shape[1], dmod_shard.shape[1]

    def body(c_ref, d_ref, o_ref):
        cv = c_ref[...]
        o_ref[...] = _dot(cv * _sigmoid(cv), d_ref[...], TN, HI)

    return pl.pallas_call(body, out_shape=jax.ShapeDtypeStruct((D, Nc), F32), name="ada_bwd",
                          compiler_params=pltpu.CompilerParams(vmem_limit_bytes=VMEM_LIMIT))(c_all, dmod_shard)


def _row_spec(ts, D):
    return pl.BlockSpec((ts, D), lambda i: (i, 0))


def _vec_spec(D):
    return pl.BlockSpec((1, D), lambda i: (0, 0))


def _rms_mod(x, ng, sc, sh, ts, name):
    S, D = x.shape

    def body(x_ref, ng_ref, sc_ref, sh_ref, h_ref):
        xv = x_ref[...]
        r = lax.rsqrt(jnp.mean(xv * xv, axis=-1, keepdims=True) + EPS)
        h_ref[...] = (xv * r * ng_ref[...] * (1.0 + sc_ref[...]) + sh_ref[...]).astype(BF16)

    return pl.pallas_call(
        body, grid=(S // ts,), in_specs=[_row_spec(ts, D)] + [_vec_spec(D)] * 3, out_specs=_row_spec(ts, D),
        out_shape=jax.ShapeDtypeStruct((S, D), BF16), compiler_params=_cp("parallel"), name=name)(x, ng, sc, sh)


def _resid_rms_mod(x, t, g, ng, sc, sh, ts, name):
    S, D = x.shape

    def body(x_ref, t_ref, g_ref, ng_ref, sc_ref, sh_ref, x2_ref, h_ref):
        xv = x_ref[...] + g_ref[...] * t_ref[...]
        x2_ref[...] = xv
        r = lax.rsqrt(jnp.mean(xv * xv, axis=-1, keepdims=True) + EPS)
        h_ref[...] = (xv * r * ng_ref[...] * (1.0 + sc_ref[...]) + sh_ref[...]).astype(BF16)

    return pl.pallas_call(
        body, grid=(S // ts,), in_specs=[_row_spec(ts, D)] * 2 + [_vec_spec(D)] * 4,
        out_specs=[_row_spec(ts, D)] * 2,
        out_shape=[jax.ShapeDtypeStruct((S, D), F32), jax.ShapeDtypeStruct((S, D), BF16)],
        compiler_params=_cp("parallel"), name=name)(x, t, g, ng, sc, sh)


def _rms_mod_bwd(dh, xin, dres, ng, sc, ts, name, t_prev=None, g_prev=None):
    S, D = xin.shape
    chain = t_prev is not None

    def body(*refs):
        if chain:
            dh_ref, x_ref, dr_ref, ng_ref, sc_ref, t_ref, g_ref, dx_ref, sums_ref, dt_ref = refs
        else:
            dh_ref, x_ref, dr_ref, ng_ref, sc_ref, dx_ref, sums_ref = refs
        i = pl.program_id(0)
        xv, dhv = x_ref[...], dh_ref[...]
        r = lax.rsqrt(jnp.mean(xv * xv, axis=-1, keepdims=True) + EPS)
        xh = xv * r
        ngv, scv = ng_ref[...], sc_ref[...]
        dxh = dhv * (ngv * (1.0 + scv))
        dx = dr_ref[...] + r * (dxh - xh * jnp.mean(dxh * xh, axis=-1, keepdims=True))
        dx_ref[...] = dx
        dhx = dhv * xh
        rows = [jnp.sum(dhv, axis=0, keepdims=True), jnp.sum(dhx * ngv, axis=0, keepdims=True),
                jnp.sum(dhx * (1.0 + scv), axis=0, keepdims=True)]
        if chain:
            dt_ref[...] = (dx * g_ref[...]).astype(BF16)
            rows.append(jnp.sum(dx * t_ref[...], axis=0, keepdims=True))
        rows.append(jnp.zeros((8 - len(rows), D), F32))
        part = jnp.concatenate(rows, axis=0)

        @pl.when(i == 0)
        def _():
            sums_ref[...] = part

        @pl.when(i > 0)
        def _():
            sums_ref[...] += part

    row, vec = _row_spec(ts, D), _vec_spec(D)
    sums_spec = pl.BlockSpec((8, D), lambda i: (0, 0))
    ins = [dh, xin, dres, ng, sc] + ([t_prev, g_prev] if chain else [])
    in_specs = [row, row, row, vec, vec] + ([row, vec] if chain else [])
    out_specs = [row, sums_spec] + ([row] if chain else [])
    out_shape = [jax.ShapeDtypeStruct((S, D), F32), jax.ShapeDtypeStruct((8, D), F32)] + (
        [jax.ShapeDtypeStruct((S, D), BF16)] if chain else [])
    return pl.pallas_call(body, grid=(S // ts,), in_specs=in_specs, out_specs=out_specs, out_shape=out_shape,
                          compiler_params=_cp("arbitrary"), name=name)(*ins)


def _gate_fwd(proj, wg_p, bg, ts):
    S = proj.shape[0]

    def body(glr_ref, w_ref, b_ref, la_ref):
        z = _dot(glr_ref[...], w_ref[...], NN, HI) + b_ref[...]
        la_ref[...] = (jnp.minimum(z, 0.0) - jnp.log(1.0 + jnp.exp(-jnp.abs(z)))) * (1.0 / GLA_TAU)

    return pl.pallas_call(
        body, grid=(S // ts,),
        in_specs=[pl.BlockSpec((ts, LANE), lambda i: (i, O_GLR // LANE)), pl.BlockSpec((LANE, GLA_QK), lambda i: (0, 0)),
                  pl.BlockSpec((1, GLA_QK), lambda i: (0, 0))],
        out_specs=pl.BlockSpec((ts, GLA_QK), lambda i: (i, 0)), out_shape=jax.ShapeDtypeStruct((S, GLA_QK), F32),
        compiler_params=_cp("parallel"), name="gla_gate_fwd")(proj, wg_p, bg)


def _gate_bwd(dla, la, proj, wg_p, ts):
    S = proj.shape[0]

    def body(dla_ref, la_ref, glr_ref, w_ref, dglr_ref, gw_ref, gb_ref):
        i = pl.program_id(0)
        dz = dla_ref[...] * (1.0 / GLA_TAU) * (1.0 - jnp.exp(GLA_TAU * la_ref[...]))
        dglr_ref[...] = _dot(dz, w_ref[...], NT, HI).astype(BF16)
        gw = _dot(glr_ref[...], dz, TN, HI)
        gb = jnp.concatenate([jnp.sum(dz, axis=0, keepdims=True), jnp.zeros((7, GLA_QK), F32)], axis=0)

        @pl.when(i == 0)
        def _():
            gw_ref[...] = gw
            gb_ref[...] = gb

        @pl.when(i > 0)
        def _():
            gw_ref[...] += gw
            gb_ref[...] += gb

    return pl.pallas_call(
        body, grid=(S // ts,),
        in_specs=[pl.BlockSpec((ts, GLA_QK), lambda i: (i, 0)), pl.BlockSpec((ts, GLA_QK), lambda i: (i, 0)),
                  pl.BlockSpec((ts, LANE), lambda i: (i, O_GLR // LANE)), pl.BlockSpec((LANE, GLA_QK), lambda i: (0, 0))],
        out_specs=[pl.BlockSpec((ts, LANE), lambda i: (i, 0)), pl.BlockSpec((LANE, GLA_QK), lambda i: (0, 0)),
                   pl.BlockSpec((8, GLA_QK), lambda i: (0, 0))],
        out_shape=[jax.ShapeDtypeStruct((S, LANE), BF16), jax.ShapeDtypeStruct((LANE, GLA_QK), F32),
                   jax.ShapeDtypeStruct((8, GLA_QK), F32)],
        compiler_params=_cp("arbitrary"), name="gla_gate_bwd")(dla, la, proj, wg_p)


def _tri(lower):
    r = lax.broadcasted_iota(jnp.int32, (GLA_CHUNK, GLA_CHUNK), 0)
    c = lax.broadcasted_iota(jnp.int32, (GLA_CHUNK, GLA_CHUNK), 1)
    return jnp.where((r >= c) if lower else (c >= r), 1.0, 0.0).astype(F32)


GLA_SUB = 16
GLA_NSUB = GLA_CHUNK // GLA_SUB
PAIR_QK = 2 * GLA_DK
PAIR_V = 2 * GLA_DV


def _band_selector():
    r = lax.broadcasted_iota(jnp.int32, (GLA_SUB * PAIR_QK, LANE), 0)
    c = lax.broadcasted_iota(jnp.int32, (GLA_SUB * PAIR_QK, LANE), 1)
    dist, head = r // PAIR_QK, (r % PAIR_QK) // GLA_DK
    return jnp.where(c == head * GLA_DK + (GLA_SUB - 1 - dist), 1.0, 0.0).astype(BF16)


def _flip_matrix():
    r = lax.broadcasted_iota(jnp.int32, (GLA_CHUNK, GLA_CHUNK), 0)
    c = lax.broadcasted_iota(jnp.int32, (GLA_CHUNK, GLA_CHUNK), 1)
    return jnp.where(r + c == GLA_CHUNK - 1, 1.0, 0.0).astype(BF16)


def _state_mask():
    r = lax.broadcasted_iota(jnp.int32, (PAIR_V, PAIR_QK), 0)
    c = lax.broadcasted_iota(jnp.int32, (PAIR_V, PAIR_QK), 1)
    return (r < GLA_DV) == (c < GLA_DK)


class _GlaChunk:
    def __init__(self, qs, kc, vc, g, sel):
        C = GLA_CHUNK
        self.qs, self.kc, self.vc = qs, kc, vc
        rows = lax.broadcasted_iota(jnp.int32, (C, 1), 0)
        lane = lax.broadcasted_iota(jnp.int32, (1, PAIR_QK), 1)
        self.rows, self.lane = rows, lane
        b = _dot(_tri(True), g, NN, HI)
        self.bl = b[C - 1:C, :]
        self.eb = jnp.exp(b)
        self.kdec = jnp.exp(self.bl - b)
        edge = lambda J: b[GLA_SUB * (J + 1):GLA_SUB * (J + 1) + 1, :]
        self.e_far = [jnp.exp(jnp.where(rows >= GLA_SUB * (J + 1), b - edge(J), NEG)) for J in range(GLA_NSUB - 1)]
        blk = rows // GLA_SUB
        bnext = edge(0)
        for J in range(1, GLA_NSUB - 1):
            bnext = jnp.where(blk == J, edge(J), bnext)
        self.e_khat = jnp.exp(jnp.where(blk < GLA_NSUB - 1, bnext - b, NEG))
        khat = kc * self.e_khat
        k2 = jnp.concatenate([jnp.where(lane < GLA_DK, khat, 0.0), jnp.where(lane >= GLA_DK, khat, 0.0)], axis=0)
        self.blk2 = jnp.concatenate([blk, blk], axis=0)
        self.m_far = jnp.concatenate([jnp.where(self.blk2 == J, k2, 0.0) for J in range(GLA_NSUB - 1)], axis=1).astype(BF16)
        self.qcat = jnp.concatenate([qs * e for e in self.e_far], axis=1).astype(BF16)
        a_far = _dot(self.qcat, self.m_far, NT)
        self.e_band, self.rk, terms = [], [], []
        for d in range(GLA_SUB):
            rk = pltpu.roll(kc, d, 0) if d else kc
            rb = pltpu.roll(b, d, 0) if d else b
            e = jnp.exp(jnp.where(rows >= d, b - rb, NEG))
            self.e_band.append(e)
            self.rk.append(rk)
            terms.append((qs * rk * e).astype(BF16))
        band = _dot(jnp.concatenate(terms, axis=1), sel, NN)
        a_band = pltpu.roll(band, LANE - (GLA_SUB - 1), 1, stride=1, stride_axis=0)
        dist = rows - lane % GLA_DK
        self.far_mask = dist >= GLA_SUB
        self.band_mask = (dist >= 0) & (dist < GLA_SUB)
        self.a = (a_band + jnp.where(self.far_mask, a_far, 0.0)).astype(BF16)
        self.lane_v = lax.broadcasted_iota(jnp.int32, (1, PAIR_V), 1)
        self.v2 = jnp.concatenate([jnp.where(self.lane_v < GLA_DV, vc, 0.0), jnp.where(self.lane_v >= GLA_DV, vc, 0.0)],
                                  axis=0).astype(BF16)


def _gla_fwd(proj, la, tb):
    S = proj.shape[0]
    C = GLA_CHUNK
    tb = min(tb, S)
    nbc = tb // C
    npair = GLA_HEADS // 2
    scale = GLA_DK ** -0.5

    def body(q_ref, k_ref, v_ref, la_ref, sel_ref, o_ref, st_ref, state):
        @pl.when(pl.program_id(1) == 0)
        def _():
            state[...] = jnp.zeros_like(state)

        def chunk(ci, carry):
            sl = pl.ds(pl.multiple_of(ci * C, C), C)
            ch = _GlaChunk(q_ref[sl, :] * scale, k_ref[sl, :], v_ref[sl, :], la_ref[sl, :], sel_ref[...])
            st = state[...]
            st_ref[0, ci] = st
            o_ref[sl, :] = _dot((ch.qs * ch.eb).astype(BF16), st.astype(BF16), NT) + _dot(ch.a, ch.v2, NN)
            upd = _dot(ch.vc.astype(BF16), (ch.kc * ch.kdec).astype(BF16), TN)
            state[...] = st * jnp.exp(ch.bl) + jnp.where(_state_mask(), upd, 0.0)
            return carry

        lax.fori_loop(0, nbc, chunk, 0, unroll=2)

    qspec = lambda off: pl.BlockSpec((tb, PAIR_QK), lambda p, i: (i, off // PAIR_QK + p))
    return pl.pallas_call(
        body, grid=(npair, S // tb),
        in_specs=[qspec(O_GQ), qspec(O_GK), pl.BlockSpec((tb, PAIR_V), lambda p, i: (i, O_GV // PAIR_V + p)),
                  pl.BlockSpec((tb, PAIR_QK), lambda p, i: (i, p)),
                  pl.BlockSpec((GLA_SUB * PAIR_QK, LANE), lambda p, i: (0, 0))],
        out_specs=[pl.BlockSpec((tb, PAIR_V), lambda p, i: (i, p)),
                   pl.BlockSpec((1, nbc, PAIR_V, PAIR_QK), lambda p, i: (p, i, 0, 0))],
        out_shape=[jax.ShapeDtypeStruct((S, GLA_V), F32), jax.ShapeDtypeStruct((npair, S // C, PAIR_V, PAIR_QK), F32)],
        scratch_shapes=[pltpu.VMEM((PAIR_V, PAIR_QK), F32)],
        compiler_params=_cp("parallel", "arbitrary"), name="gla_fwd")(proj, proj, proj, la, _band_selector())


def _gla_bwd(proj, la, do, states, tb, ride=None):
    S = proj.shape[0]
    C = GLA_CHUNK
    tb = min(tb, S)
    nbc = tb // C
    nblk = S // tb
    npair = GLA_HEADS // 2
    scale = GLA_DK ** -0.5

    def body(q_ref, k_ref, v_ref, la_ref, do_ref, st_ref, sel_ref, selt_ref, dq_ref, dk_ref, dv_ref, dla_ref, dstate):
        @pl.when(pl.program_id(1) == 0)
        def _():
            dstate[...] = jnp.zeros_like(dstate)

        def chunk(cc, carry):
            ci = nbc - 1 - cc
            sl = pl.ds(pl.multiple_of(ci * C, C), C)
            ch = _GlaChunk(q_ref[sl, :] * scale, k_ref[sl, :], v_ref[sl, :], la_ref[sl, :], sel_ref[...])
            qs, kc, rows = ch.qs, ch.kc, ch.rows
            doc_b = do_ref[sl, :].astype(BF16)
            st = st_ref[0, ci]
            dst = dstate[...]
            dst_b = dst.astype(BF16)
            ebl = jnp.exp(ch.bl)
            dq = _dot(doc_b, st.astype(BF16), NN) * ch.eb
            dk = _dot(ch.vc.astype(BF16), dst_b, NN) * ch.kdec
            dv = _dot((kc * ch.kdec).astype(BF16), dst_b, NT)
            dbl = jnp.sum(dst * st, axis=0, keepdims=True) * ebl + jnp.sum(kc * dk, axis=0, keepdims=True)
            da = _dot(doc_b, ch.v2, NT)
            dv2 = _dot(ch.a, doc_b, TN)
            dv = dv + jnp.where(ch.lane_v < GLA_DV, dv2[:C], dv2[C:])
            da_far = jnp.where(ch.far_mask, da, 0.0).astype(BF16)
            dqcat = _dot(da_far, ch.m_far, NN)
            dm = _dot(da_far, ch.qcat, TN)
            dk2 = jnp.zeros((2 * C, PAIR_QK), F32)
            for J in range(GLA_NSUB - 1):
                dq = dq + dqcat[:, J * PAIR_QK:(J + 1) * PAIR_QK] * ch.e_far[J]
                dk2 = dk2 + jnp.where(ch.blk2 == J, dm[:, J * PAIR_QK:(J + 1) * PAIR_QK], 0.0)
            dk = dk + jnp.where(ch.lane < GLA_DK, dk2[:C], dk2[C:]) * ch.e_khat
            flip = _flip_matrix()
            da_band = _dot(flip, jnp.where(ch.band_mask, da, 0.0).astype(BF16), NN)
            dband = pltpu.roll(da_band, LANE - (C - GLA_SUB), 1, stride=1, stride_axis=0)
            dband = _dot(flip, dband.astype(BF16), NN)
            dterms = _dot(dband.astype(BF16), selt_ref[...], NN)
            for d in range(GLA_SUB):
                dt = dterms[:, d * PAIR_QK:(d + 1) * PAIR_QK]
                dq = dq + dt * (ch.rk[d] * ch.e_band[d])
                dkr = dt * (qs * ch.e_band[d])
                dk = dk + (pltpu.roll(dkr, C - d, 0) if d else dkr)
            db = qs * dq - kc * dk
            db = jnp.where(rows == C - 1, db + dbl, db)
            dq_ref[sl, :] = (dq * scale).astype(BF16)
            dk_ref[sl, :] = dk.astype(BF16)
            dv_ref[sl, :] = dv.astype(BF16)
            dla_ref[sl, :] = _dot(_tri(False), db, NN, HI)
            upd = _dot(doc_b, (qs * ch.eb).astype(BF16), TN)
            dstate[...] = dst * ebl + jnp.where(_state_mask(), upd, 0.0)
            return carry

        lax.fori_loop(0, nbc, chunk, 0, unroll=2)

    rev = lambda i: nblk - 1 - i
    qspec = lambda off: pl.BlockSpec((tb, PAIR_QK), lambda p, i: (rev(i), off // PAIR_QK + p))
    pair_qk = pl.BlockSpec((tb, PAIR_QK), lambda p, i: (rev(i), p))
    pair_v = pl.BlockSpec((tb, PAIR_V), lambda p, i: (rev(i), p))
    sel = _band_selector()
    ride_arrays, ride_gather = ride if ride else ([], [])
    nr = len(ride_arrays)
    grid = (npair, nblk)
    outs = pl.pallas_call(
        _riding(body, 8, 4, ride_gather, grid), grid=grid,
        in_specs=[qspec(O_GQ), qspec(O_GK), pl.BlockSpec((tb, PAIR_V), lambda p, i: (rev(i), O_GV // PAIR_V + p)),
                  pair_qk, pair_v, pl.BlockSpec((1, nbc, PAIR_V, PAIR_QK), lambda p, i: (p, rev(i), 0, 0)),
                  pl.BlockSpec((GLA_SUB * PAIR_QK, LANE), lambda p, i: (0, 0)),
                  pl.BlockSpec((LANE, GLA_SUB * PAIR_QK), lambda p, i: (0, 0))] + [ANY_SPEC] * nr,
        out_specs=[pair_qk, pair_qk, pair_v, pair_qk] + [ANY_SPEC] * nr,
        out_shape=[jax.ShapeDtypeStruct((S, GLA_QK), BF16), jax.ShapeDtypeStruct((S, GLA_QK), BF16),
                   jax.ShapeDtypeStruct((S, GLA_V), BF16), jax.ShapeDtypeStruct((S, GLA_QK), F32)]
        + _exchange_shapes(ride_arrays, ride_gather),
        scratch_shapes=[pltpu.VMEM((PAIR_V, PAIR_QK), F32)] + (_exchange_sems(nr) if nr else []),
        compiler_params=_cp("arbitrary", "arbitrary"), name="gla_bwd")(proj, proj, proj, la, do, states, sel, sel.T, *ride_arrays)
    return outs[0], outs[1], outs[2], outs[3], outs[4:]


def _gla_out(o, proj, gng, ts):
    S = o.shape[0]

    def body(o_ref, gr_ref, g_ref, y_ref):
        for h in range(GLA_HEADS):
            cols = slice(h * GLA_DV, (h + 1) * GLA_DV)
            ov, grv = o_ref[:, cols], gr_ref[:, cols]
            r = lax.rsqrt(jnp.mean(ov * ov, axis=-1, keepdims=True) + EPS)
            y_ref[:, cols] = (ov * r * g_ref[...] * (grv * _sigmoid(grv))).astype(BF16)

    return pl.pallas_call(
        body, grid=(S // ts,),
        in_specs=[pl.BlockSpec((ts, GLA_V), lambda i: (i, 0)), pl.BlockSpec((ts, GLA_V), lambda i: (i, O_GR // GLA_V)),
                  pl.BlockSpec((1, GLA_DV), lambda i: (0, 0))],
        out_specs=pl.BlockSpec((ts, GLA_V), lambda i: (i, 0)), out_shape=jax.ShapeDtypeStruct((S, GLA_V), BF16),
        compiler_params=_cp("parallel"), name="gla_out_fwd")(o, proj, gng)


def _gla_out_bwd(dmixed, o, proj, gng, ts):
    S = o.shape[0]

    def body(dy_ref, o_ref, gr_ref, g_ref, do_ref, dgr_ref, gg_ref):
        i = pl.program_id(0)
        gsum = jnp.zeros((1, GLA_DV), F32)
        for h in range(GLA_HEADS):
            cols = slice(h * GLA_DV, (h + 1) * GLA_DV)
            ov, grv, dy = o_ref[:, cols], gr_ref[:, cols], dy_ref[:, cols]
            r = lax.rsqrt(jnp.mean(ov * ov, axis=-1, keepdims=True) + EPS)
            oh = ov * r
            sg = _sigmoid(grv)
            silu = grv * sg
            don = dy * silu
            dgr_ref[:, cols] = (dy * (oh * g_ref[...]) * (sg * (1.0 + grv * (1.0 - sg)))).astype(BF16)
            gsum = gsum + jnp.sum(don * oh, axis=0, keepdims=True)
            doh = don * g_ref[...]
            do_ref[:, cols] = r * (doh - oh * jnp.mean(doh * oh, axis=-1, keepdims=True))
        part = jnp.concatenate([gsum, jnp.zeros((7, GLA_DV), F32)], axis=0)

        @pl.when(i == 0)
        def _():
            gg_ref[...] = part

        @pl.when(i > 0)
        def _():
            gg_ref[...] += part

    return pl.pallas_call(
        body, grid=(S // ts,),
        in_specs=[pl.BlockSpec((ts, GLA_V), lambda i: (i, 0)), pl.BlockSpec((ts, GLA_V), lambda i: (i, 0)),
                  pl.BlockSpec((ts, GLA_V), lambda i: (i, O_GR // GLA_V)), pl.BlockSpec((1, GLA_DV), lambda i: (0, 0))],
        out_specs=[pl.BlockSpec((ts, GLA_V), lambda i: (i, 0)), pl.BlockSpec((ts, GLA_V), lambda i: (i, 0)),
                   pl.BlockSpec((8, GLA_DV), lambda i: (0, 0))],
        out_shape=[jax.ShapeDtypeStruct((S, GLA_V), F32), jax.ShapeDtypeStruct((S, GLA_V), BF16),
                   jax.ShapeDtypeStruct((8, GLA_DV), F32)],
        compiler_params=_cp("arbitrary"), name="gla_out_bwd")(dmixed, o, proj, gng)


def _seg_matrix(width, seg, value):
    r = lax.broadcasted_iota(jnp.int32, (width, width), 0) // seg
    c = lax.broadcasted_iota(jnp.int32, (width, width), 1) // seg
    return jnp.where(r == c, value, 0.0).astype(F32)


def _head_norm(proj, qg, kg, ts):
    S = proj.shape[0]
    W = ATTN_DIM

    def body(q_ref, k_ref, qg_ref, kg_ref, qn_ref, kn_ref):
        seg = _seg_matrix(W, ATTN_HD, 1.0 / ATTN_HD)
        for x_ref, g_ref, o_ref in ((q_ref, qg_ref, qn_ref), (k_ref, kg_ref, kn_ref)):
            xv = x_ref[...]
            ms = _dot(xv * xv, seg, NN, HI)
            o_ref[...] = xv * lax.rsqrt(ms + EPS) * g_ref[...]

    blk = lambda off: pl.BlockSpec((ts, W), lambda i: (i, off // W))
    out = pl.BlockSpec((ts, W), lambda i: (i, 0))
    vec = pl.BlockSpec((1, W), lambda i: (0, 0))
    return pl.pallas_call(
        body, grid=(S // ts,), in_specs=[blk(O_AQ), blk(O_AK), vec, vec], out_specs=[out] * 2,
        out_shape=[jax.ShapeDtypeStruct((S, W), F32)] * 2, compiler_params=_cp("parallel"), name="attn_head_norm")(
            proj, proj, qg, kg)


def _slope(head):
    one = jnp.ones((1, 1), jnp.int32)
    return 1.0 / jnp.left_shift(one, one * (head + 1)).astype(F32)


ATTN_GROUP = 4


def _attn_rows(d, g, r):
    start = g * d * ATTN_BLOCK + r
    return pl.ds(start, ATTN_BLOCK) if d == 1 else pl.ds(start, ATTN_BLOCK, stride=d)


def _for_blocks(d, G, fn):
    for g in range(G):
        if d <= ATTN_GROUP:
            for r in range(d):
                fn(g, r)
        else:
            def step(r, carry, g=g):
                fn(g, r)
                return carry
            lax.fori_loop(0, d, step, 0, unroll=ATTN_GROUP)


def _attn_specs(d, S):
    G = max(1, ATTN_GROUP // d)
    edge = d * ATTN_BLOCK
    tq = G * edge
    nb, n_edge = S // tq, S // edge

    def specs(off=0):
        return [pl.BlockSpec((tq, LANE), lambda hp, n: (n, off + hp)),
                pl.BlockSpec((edge, LANE), lambda hp, n: (jnp.maximum(n * G - 1, 0), off + hp)),
                pl.BlockSpec((edge, LANE), lambda hp, n: (jnp.minimum((n + 1) * G, n_edge - 1), off + hp))]

    return G, nb, specs


def _attn_fwd(qn, kn, proj, d):
    S, W = qn.shape
    B = ATTN_BLOCK
    G, nb, specs = _attn_specs(d, S)

    def body(q_ref, kp_ref, kc_ref, vp_ref, vc_ref, o_ref, l_ref):
        hp, n = pl.program_id(0), pl.program_id(1)
        lo = lax.broadcasted_iota(jnp.int32, (1, LANE), 1) < ATTN_HD
        iq = lax.broadcasted_iota(jnp.int32, (B, 2 * B), 0)
        ik = lax.broadcasted_iota(jnp.int32, (B, 2 * B), 1)
        rel = iq + B - ik
        window = (rel >= 0) & (rel <= B)
        relf = (d * rel).astype(F32)

        def sub(g, r):
            rows = _attn_rows(d, g, r)
            before = _attn_rows(d, max(g - 1, 0), r)
            kb_ref, vb_ref = (kp_ref, vp_ref) if g == 0 else (kc_ref, vc_ref)
            valid = window & ((ik >= B) | (n > 0)) if g == 0 else window
            qv = q_ref[rows, :].astype(BF16)
            kv = jnp.concatenate([kb_ref[before, :], kc_ref[rows, :]], axis=0).astype(BF16)
            vv = jnp.concatenate([vb_ref[before, :], vc_ref[rows, :]], axis=0).astype(BF16)
            outs, lses = [], []
            for h in range(2):
                qm = jnp.where(lo == (h == 0), qv, jnp.zeros_like(qv))
                s = _dot(qm, kv, NT) * (ATTN_HD ** -0.5) - _slope(hp * 2 + h) * relf
                s = jnp.where(valid, s, NEG)
                m = jnp.max(s, axis=-1, keepdims=True)
                p = jnp.exp(s - m)
                den = jnp.sum(p, axis=-1, keepdims=True)
                outs.append(_dot(p.astype(BF16), vv, NN) / den)
                lses.append(m + jnp.log(den))
            o_ref[rows, :] = jnp.where(lo, outs[0], outs[1])
            l_ref[rows, :] = jnp.where(lo, lses[0], lses[1])

        _for_blocks(d, G, sub)

    cur, prev, _ = specs()
    vcur, vprev, _ = specs(O_AV // LANE)
    return pl.pallas_call(
        body, grid=(W // LANE, nb), in_specs=[cur, prev, cur, vprev, vcur], out_specs=[cur, cur],
        out_shape=[jax.ShapeDtypeStruct((S, W), F32)] * 2,
        compiler_params=_cp("parallel", "arbitrary"), name=f"attn_fwd_d{d}")(qn, kn, kn, proj, proj)


def _attn_merge(y_gla, os_, ls_, ts):
    S, W = os_[0].shape

    def body(yg, o1, o2, o3, l1, l2, l3, mixed_ref, y_ref, lse_ref):
        a, b, c = l1[...], l2[...], l3[...]
        m = jnp.maximum(jnp.maximum(a, b), c)
        ea, eb, ec = jnp.exp(a - m), jnp.exp(b - m), jnp.exp(c - m)
        tot = ea + eb + ec
        y = (ea * o1[...] + eb * o2[...] + ec * o3[...]) / tot
        y_ref[...] = y
        mixed_ref[:, :W] = yg[...]
        mixed_ref[:, W:] = y.astype(BF16)
        lse_ref[...] = m + jnp.log(tot)

    spec = pl.BlockSpec((ts, W), lambda i: (i, 0))
    return pl.pallas_call(
        body, grid=(S // ts,), in_specs=[spec] * 7, out_specs=[pl.BlockSpec((ts, 2 * W), lambda i: (i, 0)), spec, spec],
        out_shape=[jax.ShapeDtypeStruct((S, 2 * W), BF16), jax.ShapeDtypeStruct((S, W), F32), jax.ShapeDtypeStruct((S, W), F32)],
        compiler_params=_cp("parallel"), name="attn_merge")(y_gla, *os_, *ls_)


def _attn_delta(dmixed, y, ts):
    S, W = y.shape

    def body(dy_ref, y_ref, d_ref):
        d_ref[...] = _dot(dy_ref[...] * y_ref[...], _seg_matrix(W, ATTN_HD, 1.0), NN, HI)

    return pl.pallas_call(
        body, grid=(S // ts,), in_specs=[pl.BlockSpec((ts, W), lambda i: (i, 1)), pl.BlockSpec((ts, W), lambda i: (i, 0))],
        out_specs=pl.BlockSpec((ts, W), lambda i: (i, 0)), out_shape=jax.ShapeDtypeStruct((S, W), F32),
        compiler_params=_cp("parallel"), name="attn_delta")(dmixed, y)


def _attn_dq(qn, kn, proj, dmixed, lse, delta, d):
    S, W = qn.shape
    B = ATTN_BLOCK
    G, nb, specs = _attn_specs(d, S)

    def body(q_ref, kp_ref, kc_ref, vp_ref, vc_ref, dy_ref, l_ref, de_ref, dq_ref):
        hp, n = pl.program_id(0), pl.program_id(1)
        lo = lax.broadcasted_iota(jnp.int32, (1, LANE), 1) < ATTN_HD
        iq = lax.broadcasted_iota(jnp.int32, (B, 2 * B), 0)
        ik = lax.broadcasted_iota(jnp.int32, (B, 2 * B), 1)
        rel = iq + B - ik
        window = (rel >= 0) & (rel <= B)
        relf = (d * rel).astype(F32)

        def sub(g, r):
            rows = _attn_rows(d, g, r)
            before = _attn_rows(d, max(g - 1, 0), r)
            kb_ref, vb_ref = (kp_ref, vp_ref) if g == 0 else (kc_ref, vc_ref)
            valid = window & ((ik >= B) | (n > 0)) if g == 0 else window
            qv, dyv = q_ref[rows, :].astype(BF16), dy_ref[rows, :]
            lv, dev = l_ref[rows, :], de_ref[rows, :]
            kv = jnp.concatenate([kb_ref[before, :], kc_ref[rows, :]], axis=0).astype(BF16)
            vv = jnp.concatenate([vb_ref[before, :], vc_ref[rows, :]], axis=0).astype(BF16)
            outs = []
            for h in range(2):
                sel = lo == (h == 0)
                qm = jnp.where(sel, qv, jnp.zeros_like(qv))
                dym = jnp.where(sel, dyv, 0.0).astype(BF16)
                lse_h = lv[:, h * ATTN_HD:h * ATTN_HD + 1]
                del_h = dev[:, h * ATTN_HD:h * ATTN_HD + 1]
                s = _dot(qm, kv, NT) * (ATTN_HD ** -0.5) - _slope(hp * 2 + h) * relf
                p = jnp.exp(jnp.where(valid, s, NEG) - lse_h)
                ds = p * (_dot(dym, vv, NT) - del_h)
                outs.append(_dot(ds.astype(BF16), kv, NN) * (ATTN_HD ** -0.5))
            dq_ref[rows, :] = jnp.where(lo, outs[0], outs[1])

        _for_blocks(d, G, sub)

    cur, prev, _ = specs()
    vcur, vprev, _ = specs(O_AV // LANE)
    dycur, _, _ = specs(W // LANE)
    return pl.pallas_call(
        body, grid=(W // LANE, nb), in_specs=[cur, prev, cur, vprev, vcur, dycur, cur, cur], out_specs=cur,
        out_shape=jax.ShapeDtypeStruct((S, W), F32),
        compiler_params=_cp("parallel", "arbitrary"), name=f"attn_dq_d{d}")(qn, kn, kn, proj, proj, dmixed, lse, delta)


def _attn_dkv(qn, kn, proj, dmixed, lse, delta, d):
    S, W = qn.shape
    B = ATTN_BLOCK
    G, nb, specs = _attn_specs(d, S)

    def body(k_ref, v_ref, qc_ref, qn_ref, dyc_ref, dyn_ref, lc_ref, ln_ref, dec_ref, den_ref, dk_ref, dv_ref):
        hp, n = pl.program_id(0), pl.program_id(1)
        lo = lax.broadcasted_iota(jnp.int32, (1, LANE), 1) < ATTN_HD
        iq = lax.broadcasted_iota(jnp.int32, (B, B), 0)
        ik = lax.broadcasted_iota(jnp.int32, (B, B), 1)

        def sub(g, r):
            rows = _attn_rows(d, g, r)
            kv, vv = k_ref[rows, :].astype(BF16), v_ref[rows, :].astype(BF16)
            dk = jnp.zeros((B, LANE), F32)
            dv = jnp.zeros((B, LANE), F32)
            inside = g + 1 < G
            after = _attn_rows(d, g + 1 if inside else 0, r)
            following = (qc_ref, dyc_ref, lc_ref, dec_ref) if inside else (qn_ref, dyn_ref, ln_ref, den_ref)
            for nxt, qrows, (q_ref, dy_ref, l_ref, de_ref) in ((0, rows, (qc_ref, dyc_ref, lc_ref, dec_ref)), (1, after, following)):
                rel = iq - ik + nxt * B
                valid = (rel >= 0) & (rel <= B)
                if nxt and not inside:
                    valid = valid & (n + 1 < nb)
                relf = (d * rel).astype(F32)
                qv, dyv = q_ref[qrows, :].astype(BF16), dy_ref[qrows, :]
                lv, dev = l_ref[qrows, :], de_ref[qrows, :]
                for h in range(2):
                    sel = lo == (h == 0)
                    qm = jnp.where(sel, qv, jnp.zeros_like(qv))
                    dym = jnp.where(sel, dyv, 0.0).astype(BF16)
                    lse_h = lv[:, h * ATTN_HD:h * ATTN_HD + 1]
                    del_h = dev[:, h * ATTN_HD:h * ATTN_HD + 1]
                    s = _dot(qm, kv, NT) * (ATTN_HD ** -0.5) - _slope(hp * 2 + h) * relf
                    p = jnp.exp(jnp.where(valid, s, NEG) - lse_h)
                    dv = dv + _dot(p.astype(BF16), dym, TN)
                    ds = p * (_dot(dym, vv, NT) - del_h)
                    dk = dk + _dot(ds.astype(BF16), qm, TN) * (ATTN_HD ** -0.5)
            dk_ref[rows, :] = dk
            dv_ref[rows, :] = dv

        _for_blocks(d, G, sub)

    cur, _, nxt = specs()
    vcur, _, _ = specs(O_AV // LANE)
    dycur, _, dynxt = specs(W // LANE)
    return pl.pallas_call(
        body, grid=(W // LANE, nb), in_specs=[cur, vcur, cur, nxt, dycur, dynxt, cur, nxt, cur, nxt], out_specs=[cur, cur],
        out_shape=[jax.ShapeDtypeStruct((S, W), F32)] * 2,
        compiler_params=_cp("parallel", "arbitrary"), name=f"attn_dkv_d{d}")(
            kn, proj, qn, qn, dmixed, dmixed, lse, lse, delta, delta)


def _attn_post(dqs, dks, dvs, proj, qg, kg, ts):
    S = proj.shape[0]
    W = ATTN_DIM

    def body(dq1, dq2, dq3, dk1, dk2, dk3, dv1, dv2, dv3, aq_ref, ak_ref, qg_ref, kg_ref, daq_ref, dak_ref, dav_ref, gg_ref):
        i = pl.program_id(0)
        seg = _seg_matrix(W, ATTN_HD, 1.0 / ATTN_HD)
        gsums = []
        for (d1, d2, d3), x_ref, g_ref, o_ref in (((dq1, dq2, dq3), aq_ref, qg_ref, daq_ref), ((dk1, dk2, dk3), ak_ref, kg_ref, dak_ref)):
            dy = d1[...] + d2[...] + d3[...]
            xv = x_ref[...]
            r = lax.rsqrt(_dot(xv * xv, seg, NN, HI) + EPS)
            xh = xv * r
            dxh = dy * g_ref[...]
            o_ref[...] = (r * (dxh - xh * _dot(dxh * xh, seg, NN, HI))).astype(BF16)
            gsums.append(jnp.sum(dy * xh, axis=0, keepdims=True))
        dav_ref[...] = (dv1[...] + dv2[...] + dv3[...]).astype(BF16)
        part = jnp.concatenate(gsums + [jnp.zeros((6, W), F32)], axis=0)

        @pl.when(i == 0)
        def _():
            gg_ref[...] = part

        @pl.when(i > 0)
        def _():
            gg_ref[...] += part

    row = pl.BlockSpec((ts, W), lambda i: (i, 0))
    blk = lambda off: pl.BlockSpec((ts, W), lambda i: (i, off // W))
    vec = pl.BlockSpec((1, W), lambda i: (0, 0))
    return pl.pallas_call(
        body, grid=(S // ts,), in_specs=[row] * 9 + [blk(O_AQ), blk(O_AK), vec, vec],
        out_specs=[row, row, row, pl.BlockSpec((8, W), lambda i: (0, 0))],
        out_shape=[jax.ShapeDtypeStruct((S, W), BF16)] * 3 + [jax.ShapeDtypeStruct((8, W), F32)],
        compiler_params=_cp("arbitrary"), name="attn_post")(*dqs, *dks, *dvs, proj, proj, qg, kg)


def _shift_down(cur, halo, n):
    ts = cur.shape[0]
    rows = lax.broadcasted_iota(jnp.int32, (ts, 1), 0)
    out = pltpu.roll(cur, n, 0)
    for t in range(n):
        out = jnp.where(rows == t, halo[8 - n + t:8 - n + t + 1, :], out)
    return out


def _shift_up(cur, halo, n):
    ts = cur.shape[0]
    rows = lax.broadcasted_iota(jnp.int32, (ts, 1), 0)
    out = pltpu.roll(cur, ts - n, 0)
    for t in range(n):
        out = jnp.where(rows == ts - n + t, halo[t:t + 1, :], out)
    return out


def _conv(cur, halo, w, b):
    return b + w[0:1, :] * _shift_down(cur, halo, 2) + w[1:2, :] * _shift_down(cur, halo, 1) + w[2:3, :] * cur


def _conv_swiglu(u0, conv_w8, conv_b, ts, tc):
    S, F2 = u0.shape
    F = F2 // 2
    nc = F // tc
    hb = ts // 8

    def body(ug_ref, ugh_ref, uv_ref, uvh_ref, wg_ref, wv_ref, bg_ref, bv_ref, a_ref):
        first = pl.program_id(0) == 0
        ugh = jnp.where(first, 0.0, ugh_ref[...])
        uvh = jnp.where(first, 0.0, uvh_ref[...])
        g = _conv(ug_ref[...], ugh, wg_ref[...], bg_ref[...])
        v = _conv(uv_ref[...], uvh, wv_ref[...], bv_ref[...])
        a_ref[...] = (g * _sigmoid(g) * v).astype(BF16)

    main = lambda off: pl.BlockSpec((ts, tc), lambda i, j: (i, j + off))
    halo = lambda off: pl.BlockSpec((8, tc), lambda i, j: (jnp.maximum(i * hb - 1, 0), j + off))
    wspec = lambda off: pl.BlockSpec((8, tc), lambda i, j: (0, j + off))
    bspec = lambda off: pl.BlockSpec((1, tc), lambda i, j: (0, j + off))
    return pl.pallas_call(
        body, grid=(S // ts, nc),
        in_specs=[main(0), halo(0), main(nc), halo(nc), wspec(0), wspec(nc), bspec(0), bspec(nc)],
        out_specs=pl.BlockSpec((ts, tc), lambda i, j: (i, j)), out_shape=jax.ShapeDtypeStruct((S, F), BF16),
        compiler_params=_cp("parallel", "parallel"), name="conv_swiglu")(u0, u0, u0, u0, conv_w8, conv_w8, conv_b, conv_b)


def _ffn_du(da, u0, conv_w8, conv_b, ts, tc, ride=None):
    S, F2 = u0.shape
    F = F2 // 2
    nc = F // tc
    hb = ts // 8
    grid = (nc, S // ts)
    ride_arrays, ride_gather = ride if ride else ([], [])
    nr = len(ride_arrays)

    def body(da_ref, ug_ref, ugh_ref, uv_ref, uvh_ref, wg_ref, wv_ref, bg_ref, bv_ref, du_ref, sg_ref, sv_ref):
        i = pl.program_id(1)
        first = i == 0
        halves = []
        for u_ref, h_ref, w_ref, b_ref in ((ug_ref, ugh_ref, wg_ref, bg_ref), (uv_ref, uvh_ref, wv_ref, bv_ref)):
            u, halo, w = u_ref[...], jnp.where(first, 0.0, h_ref[...]), w_ref[...]
            s2, s1 = _shift_down(u, halo, 2), _shift_down(u, halo, 1)
            halves.append((b_ref[...] + w[0:1, :] * s2 + w[1:2, :] * s1 + w[2:3, :] * u, s2, s1, u))
        g, v = halves[0][0], halves[1][0]
        dav = da_ref[...]
        sig = _sigmoid(g)
        dus = (dav * v * (sig * (1.0 + g * (1.0 - sig))), dav * (g * sig))
        for h, (du, sums_ref) in enumerate(zip(dus, (sg_ref, sv_ref))):
            du_ref[h] = du
            _, s2, s1, u = halves[h]
            part = jnp.concatenate([jnp.sum(du * s2, axis=0, keepdims=True), jnp.sum(du * s1, axis=0, keepdims=True),
                                    jnp.sum(du * u, axis=0, keepdims=True), jnp.sum(du, axis=0, keepdims=True),
                                    jnp.zeros((4, tc), F32)], axis=0)

            @pl.when(first)
            def _(sums_ref=sums_ref, part=part):
                sums_ref[...] = part

            @pl.when(i > 0)
            def _(sums_ref=sums_ref, part=part):
                sums_ref[...] += part

    main = lambda off: pl.BlockSpec((ts, tc), lambda j, i: (i, j + off))
    halo = lambda off: pl.BlockSpec((8, tc), lambda j, i: (jnp.maximum(i * hb - 1, 0), j + off))
    wspec = lambda off: pl.BlockSpec((8, tc), lambda j, i: (0, j + off))
    bspec = lambda off: pl.BlockSpec((1, tc), lambda j, i: (0, j + off))
    sums_spec = pl.BlockSpec((8, tc), lambda j, i: (0, j))
    outs = pl.pallas_call(
        _riding(body, 9, 3, ride_gather, grid), grid=grid,
        in_specs=[main(0), main(0), halo(0), main(nc), halo(nc), wspec(0), wspec(nc), bspec(0), bspec(nc)] + [ANY_SPEC] * nr,
        out_specs=[pl.BlockSpec((2, ts, tc), lambda j, i: (0, i, j)), sums_spec, sums_spec] + [ANY_SPEC] * nr,
        out_shape=[jax.ShapeDtypeStruct((2, S, F), F32), jax.ShapeDtypeStruct((8, F), F32), jax.ShapeDtypeStruct((8, F), F32)]
        + _exchange_shapes(ride_arrays, ride_gather),
        scratch_shapes=_exchange_sems(nr) if nr else [],
        compiler_params=_cp("arbitrary", "arbitrary"), name="ffn_du")(
            da, u0, u0, u0, u0, conv_w8, conv_w8, conv_b, conv_b, *ride_arrays)
    return outs[0], outs[1], outs[2], outs[3:]


def _ffn_du0(du, conv_w8, ts, tc):
    _, S, F = du.shape
    nc = F // tc
    hb = ts // 8
    nrow = S // ts

    def body(du_ref, duh_ref, w_ref, o_ref):
        last = pl.program_id(0) == nrow - 1
        cur, halo, w = du_ref[...], jnp.where(last, 0.0, duh_ref[...]), w_ref[...]
        o_ref[...] = (w[2:3, :] * cur + w[1:2, :] * _shift_up(cur, halo, 1) + w[0:1, :] * _shift_up(cur, halo, 2)).astype(BF16)

    return pl.pallas_call(
        body, grid=(nrow, 2, nc),
        in_specs=[pl.BlockSpec((None, ts, tc), lambda i, h, j: (h, i, j)),
                  pl.BlockSpec((None, 8, tc), lambda i, h, j: (h, jnp.minimum((i + 1) * hb, S // 8 - 1), j)),
                  pl.BlockSpec((8, tc), lambda i, h, j: (0, h * nc + j))],
        out_specs=pl.BlockSpec((ts, tc), lambda i, h, j: (i, h * nc + j)), out_shape=jax.ShapeDtypeStruct((S, 2 * F), BF16),
        compiler_params=_cp("parallel", "parallel", "parallel"), name="ffn_du0")(du, du, conv_w8)


def _loss_resid(x2, t2, g2, target, ts):
    S, D = x2.shape

    def body(x_ref, t_ref, g_ref, y_ref, dx_ref, dt_ref, sums_ref):
        i = pl.program_id(0)
        tv, gv = t_ref[...], g_ref[...]
        e = x_ref[...] + gv * tv - y_ref[...]
        dx = e * (1.0 / D)
        dx_ref[...] = dx
        dt_ref[...] = (dx * gv).astype(BF16)
        part = jnp.concatenate([jnp.sum(e * e, axis=0, keepdims=True), jnp.sum(dx * tv, axis=0, keepdims=True),
                                jnp.zeros((6, D), F32)], axis=0)

        @pl.when(i == 0)
        def _():
            sums_ref[...] = part

        @pl.when(i > 0)
        def _():
            sums_ref[...] += part

    row, vec = _row_spec(ts, D), _vec_spec(D)
    return pl.pallas_call(
        body, grid=(S // ts,), in_specs=[row, row, vec, row], out_specs=[row, row, pl.BlockSpec((8, D), lambda i: (0, 0))],
        out_shape=[jax.ShapeDtypeStruct((S, D), F32), jax.ShapeDtypeStruct((S, D), BF16), jax.ShapeDtypeStruct((8, D), F32)],
        compiler_params=_cp("arbitrary"), name="loss_resid")(x2, t2, g2, target)


def _adamw(w, g, m, v, name):
    shape = w.shape
    n = math.prod(shape)
    view = (n // LANE, LANE) if n % LANE == 0 else (math.prod(shape[:-1]), shape[-1])
    R, C = view
    tr = R
    for cand in (1024, 512, 256):
        if R > cand and R % cand == 0:
            tr = cand
            break

    def body(w_ref, g_ref, m_ref, v_ref, d_ref, nm_ref, nv_ref):
        gv = g_ref[...]
        nm = ADAM_B1 * m_ref[...] + (1.0 - ADAM_B1) * gv
        nv = ADAM_B2 * v_ref[...] + (1.0 - ADAM_B2) * (gv * gv)
        m_hat = nm / (1.0 - ADAM_B1 ** ADAM_STEP)
        v_hat = nv / (1.0 - ADAM_B2 ** ADAM_STEP)
        d_ref[...] = -ADAM_LR * (m_hat / (jnp.sqrt(v_hat) + ADAM_EPS) + ADAM_WD * w_ref[...])
        nm_ref[...] = nm
        nv_ref[...] = nv

    spec = pl.BlockSpec((tr, C), lambda i: (i, 0))
    outs = pl.pallas_call(
        body, grid=(R // tr,), in_specs=[spec] * 4, out_specs=[spec] * 3, out_shape=[jax.ShapeDtypeStruct(view, F32)] * 3,
        compiler_params=_cp("parallel"), name=name)(*[a.reshape(view) for a in (w, g, m, v)])
    return [o.reshape(shape) for o in outs]


def _pad_rows8(a):
    return jnp.concatenate([a, jnp.zeros((8 - a.shape[0], a.shape[1]), a.dtype)], axis=0)


def _local_step(x, target, mod, n1g, w_in_p, wg_p, bg, gng, qng, kng, w_out_s, n2g, w_up_s, conv_w, conv_b, w_down_s):
    S, D = x.shape
    F = w_down_s.shape[0] * N_DEV
    ts = min(512, S)
    sh1, sc1, g1, sh2, sc2, g2 = [mod[i:i + 1] for i in range(6)]
    conv_w8 = _pad_rows8(conv_w)
    qg_t, kg_t = jnp.tile(qng, (1, ATTN_HEADS)), jnp.tile(kng, (1, ATTN_HEADS))

    h1 = _rms_mod(x, n1g, sc1, sh1, ts, "rms_mod1")
    proj, (g_out, g_up) = _mm(h1, w_in_p, NN, 512, PROJ_W, 1024, F32, "mm_in", ride=([w_out_s, _dense(w_up_s)], [True, True]))
    w_out = g_out.reshape(-1, D)
    w_up = _cols_from_blocks(g_up.reshape(N_DEV, D, -1))
    la = _gate_fwd(proj, wg_p, bg, ts)
    o_gla, states = _gla_fwd(proj, la, 512)
    y_gla = _gla_out(o_gla, proj, gng, ts)
    qn, kn = _head_norm(proj, qg_t, kg_t, ts)
    branches = [_attn_fwd(qn, kn, proj, d) for d in DILATIONS]
    mixed, y_att, lse = _attn_merge(y_gla, [b[0] for b in branches], [b[1] for b in branches], ts)
    t1 = _mm(mixed, w_out, NN, 512, 1024, 1024, F32, "mm_out")
    x2, h2 = _resid_rms_mod(x, t1, g1, n2g, sc2, sh2, ts, "resid_rms_mod2")
    u0, (g_down,) = _mm(h2, w_up, NN, 512, 2816, 1024, F32, "mm_up", ride=([w_down_s], [True]))
    w_down = g_down.reshape(F, D)
    tc = 1408 if F % 1408 == 0 else F
    a = _conv_swiglu(u0, conv_w8, conv_b, min(256, S), tc)
    t2 = _mm(a, w_down, NN, 512, 1024, F, F32, "mm_down")
    dx3, dt2, sums3 = _loss_resid(x2, t2, g2, target, ts)
    loss_row, dg2 = sums3[0:1], sums3[1:2]

    g_w_down = _mm(a, dt2, TN, 1408, 1024, 512, F32, "mm_gw_down")
    da = _mm(dt2, w_down, NT, 512, 2816, 1024, F32, "mm_da")
    du, sums_g, sums_v, (r_down,) = _ffn_du(da, u0, conv_w8, conv_b, min(256, S), tc,
                                            ride=([g_w_down.reshape(N_DEV, -1, D)], [False]))
    g_conv_w = jnp.concatenate([sums_g[0:3], sums_v[0:3]], axis=1)
    g_conv_b = jnp.concatenate([sums_g[3:4], sums_v[3:4]], axis=1)
    du0 = _ffn_du0(du, conv_w8, min(256, S), tc)
    g_w_up = _mm(h2, du0, TN, 512, 2816, 512, F32, "mm_gw_up")
    dh2 = _mm(du0, w_up, NT, 512, 1024, 2816, F32, "mm_dh2")
    dx2, sums2, dt1 = _rms_mod_bwd(dh2, x2, dx3, n2g, sc2, ts, "rms_mod_bwd2", t_prev=t1, g_prev=g1)
    dsh2, dsc2, g_n2g, dg1 = sums2[0:1], sums2[1:2], sums2[2:3], sums2[3:4]
    g_w_out = _mm(mixed, dt1, TN, 1024, 1024, 512, F32, "mm_gw_out")
    dmixed = _mm(dt1, w_out, NT, 512, 1024, 1024, F32, "mm_dmixed")
    do_gla, dgr, gng_sums = _gla_out_bwd(dmixed, o_gla, proj, gng, ts)
    dgq, dgk, dgv, dla, (r_up, r_out) = _gla_bwd(
        proj, la, do_gla, states, 512, ride=([_dense(_col_blocks(g_w_up)), g_w_out.reshape(N_DEV, -1, D)], [False, False]))
    dglr, g_wg_p, gb_sums = _gate_bwd(dla, la, proj, wg_p, ts)
    delta = _attn_delta(dmixed, y_att, ts)
    dqs = [_attn_dq(qn, kn, proj, dmixed, lse, delta, d) for d in DILATIONS]
    dkvs = [_attn_dkv(qn, kn, proj, dmixed, lse, delta, d) for d in DILATIONS]
    daq, dak, dav, qk_sums = _attn_post(dqs, [t[0] for t in dkvs], [t[1] for t in dkvs], proj, qg_t, kg_t, ts)
    dproj = jnp.concatenate([dgq, dgk, dgv, dgr, daq, dak, dav, dglr, jnp.zeros((S, PROJ_W - O_GLR - LANE), BF16)], axis=1)
    g_w_in_p = _mm(h1, dproj, TN, 512, PROJ_W, 512, F32, "mm_gw_in")
    g_w_in = jnp.concatenate([g_w_in_p[:, :GLR_SRC], g_w_in_p[:, O_GLR:O_GLR + GLA_RANK], g_w_in_p[:, GLR_SRC:O_GLR]], axis=1)
    dh1, (r_in,) = _mm(dproj, w_in_p, NT, 512, 1024, PROJ_W, F32, "mm_dh1", ride=([_dense(_col_blocks(g_w_in))], [False]))
    dx, sums1 = _rms_mod_bwd(dh1, x, dx2, n1g, sc1, ts, "rms_mod_bwd1")
    dsh1, dsc1, g_n1g = sums1[0:1], sums1[1:2], sums1[2:3]

    dmod = jnp.concatenate([dsh1, dsc1, dg1, dsh2, dsc2, dg2], axis=1)
    grads = dict(n1g=g_n1g, w_in=r_in, wg=g_wg_p[:GLA_RANK], bg=gb_sums[0:1], gng=gng_sums[0:1],
                 qng_lanes=qk_sums[0:1], kng_lanes=qk_sums[1:2], w_out=r_out, n2g=g_n2g, w_up=r_up,
                 conv_w=g_conv_w, conv_b=g_conv_b, w_down=r_down)
    return loss_row, dx, dmod, grads


def _dense(a):
    *lead, R, C = a.shape
    return a.reshape(*lead, R * C // LANE, LANE)


def _col_blocks(a):
    R, W = a.shape
    return a.reshape(R, N_DEV, W // N_DEV).transpose(1, 0, 2)


def _cols_from_blocks(a):
    n, R, C = a.shape
    return a.transpose(1, 0, 2).reshape(R, n * C)


def kernel(x, c, w_ada, b_ada, norm1_g, w_in, gla_w_gate, gla_b_gate, gla_norm_g, q_norm_g, k_norm_g, w_out, norm2_g, w_up, conv_w, conv_b, w_down, loss_target, m_w_ada, m_b_ada, m_norm1_g, m_w_in, m_gla_w_gate, m_gla_b_gate, m_gla_norm_g, m_q_norm_g, m_k_norm_g, m_w_out, m_norm2_g, m_w_up, m_conv_w, m_conv_b, m_w_down, v_w_ada, v_b_ada, v_norm1_g, v_w_in, v_gla_w_gate, v_gla_b_gate, v_gla_norm_g, v_q_norm_g, v_k_norm_g, v_w_out, v_norm2_g, v_w_up, v_conv_w, v_conv_b, v_w_down):
    axes = ("x", "y", "c")
    me = 4 * lax.axis_index("x") + 2 * lax.axis_index("y") + lax.axis_index("c")
    S, D = x.shape[1], x.shape[2]
    x2d, tgt2d = x[0], loss_target[0]
    w_in_s, w_out_s, w_up_s, w_down_s, w_ada_s = w_in[0], w_out[0], w_up[0], w_down[0], w_ada[0]
    conv_w_s, wg_s = conv_w[0], gla_w_gate[0]
    in_c, up_c, ada_c, wg_c, cw_c = w_in_s.shape[1], w_up_s.shape[1], w_ada_s.shape[1], wg_s.shape[1], conv_w_s.shape[1]
    F = w_down_s.shape[0] * N_DEV

    small = jnp.concatenate([conv_w_s.reshape(1, -1), wg_s.reshape(1, -1)], axis=1)
    n_small = small.shape[1]
    small = jnp.pad(small, ((0, 0), (0, -n_small % LANE)))
    g_c, g_in, g_small = _exchange([c, _dense(w_in_s.astype(BF16)), small], [True] * 3, "gather_w_in")
    c_all = g_c.reshape(N_DEV, D)
    w_in_full = _cols_from_blocks(g_in.reshape(N_DEV, D, in_c))
    w_in_p = jnp.concatenate([w_in_full[:, :GLR_SRC], w_in_full[:, GLR_SRC + GLA_RANK:],
                              w_in_full[:, GLR_SRC:GLR_SRC + GLA_RANK], jnp.zeros((D, PROJ_W - O_GLR - GLA_RANK), BF16)], axis=1)
    g_small = g_small.reshape(N_DEV, -1)
    conv_w_full = _cols_from_blocks(g_small[:, :3 * cw_c].reshape(N_DEV, 3, cw_c))
    wg_full = _cols_from_blocks(g_small[:, 3 * cw_c:n_small].reshape(N_DEV, GLA_RANK, wg_c))
    wg_p = jnp.concatenate([wg_full, jnp.zeros((LANE - GLA_RANK, wg_full.shape[1]), F32)], axis=0)

    b_shard = lax.dynamic_slice(b_ada, (0, me * ada_c), (1, ada_c))
    mod_part = _ada_fwd(c_all, w_ada_s, b_shard)
    mod_recv, = _exchange([mod_part.reshape(N_DEV, 1, ada_c)], [False], "exchange_mod")
    mod = mod_recv.reshape(6, D)

    loss_row, dx, dmod, gr = _local_step(
        x2d, tgt2d, mod, norm1_g, w_in_p, wg_p, gla_b_gate, gla_norm_g, q_norm_g, k_norm_g,
        w_out_s.astype(BF16), norm2_g, w_up_s.astype(BF16), conv_w_full, conv_b, w_down_s.astype(BF16))
    loss = lax.psum(0.5 / D * jnp.sum(loss_row), axes)

    parts = [dmod, gr["n1g"], gr["bg"], gr["gng"], gr["qng_lanes"], gr["kng_lanes"], gr["n2g"], gr["conv_b"],
             gr["wg"].reshape(1, -1), gr["conv_w"].reshape(1, -1)]
    sizes = [p.shape[1] for p in parts]
    packed = jnp.concatenate(parts, axis=1)
    packed = jnp.pad(packed, ((0, 0), (0, -packed.shape[1] % (8 * LANE))))
    gathered, = _exchange([packed.reshape(8, -1)], [True], "gather_small_grads")
    gathered = gathered.reshape(N_DEV, -1)
    total = _sum_slots(gathered.reshape(N_DEV, 8, -1), "sum_small_grads").reshape(1, -1)
    offs = [0]
    for s_ in sizes:
        offs.append(offs[-1] + s_)
    t_dmod, t_n1g, t_bg, t_gng, t_qng, t_kng, t_n2g, t_conv_b, t_wg, t_conv_w = [
        total[:, offs[i]:offs[i + 1]] for i in range(len(sizes))]
    g_b_ada = t_dmod
    g_qng = t_qng.reshape(ATTN_HEADS, ATTN_HD).sum(axis=0, keepdims=True)
    g_kng = t_kng.reshape(ATTN_HEADS, ATTN_HD).sum(axis=0, keepdims=True)
    g_wg = lax.dynamic_slice(t_wg.reshape(GLA_RANK, -1), (0, me * wg_c), (GLA_RANK, wg_c))
    g_conv_w = lax.dynamic_slice(t_conv_w.reshape(3, -1), (0, me * cw_c), (3, cw_c))
    dmod_shard = lax.dynamic_slice(gathered[:, :6 * D], (0, me * ada_c), (N_DEV, ada_c))
    g_w_ada = _ada_bwd(c_all, dmod_shard)

    g_w_in = _sum_slots(gr["w_in"], "sum_gw_in").reshape(D, in_c)
    g_w_out = _sum_slots(gr["w_out"], "sum_gw_out")
    g_w_up = _sum_slots(gr["w_up"], "sum_gw_up").reshape(D, up_c)
    g_w_down = _sum_slots(gr["w_down"], "sum_gw_down")

    names = ["w_ada", "b_ada", "norm1_g", "w_in", "gla_w_gate", "gla_b_gate", "gla_norm_g", "q_norm_g", "k_norm_g",
             "w_out", "norm2_g", "w_up", "conv_w", "conv_b", "w_down"]
    ws = [w_ada, b_ada, norm1_g, w_in, gla_w_gate, gla_b_gate, gla_norm_g, q_norm_g, k_norm_g, w_out, norm2_g, w_up, conv_w, conv_b, w_down]
    ms = [m_w_ada, m_b_ada, m_norm1_g, m_w_in, m_gla_w_gate, m_gla_b_gate, m_gla_norm_g, m_q_norm_g, m_k_norm_g, m_w_out, m_norm2_g, m_w_up, m_conv_w, m_conv_b, m_w_down]
    vs = [v_w_ada, v_b_ada, v_norm1_g, v_w_in, v_gla_w_gate, v_gla_b_gate, v_gla_norm_g, v_q_norm_g, v_k_norm_g, v_w_out, v_norm2_g, v_w_up, v_conv_w, v_conv_b, v_w_down]
    gs = [g_w_ada, g_b_ada, t_n1g, g_w_in, g_wg, t_bg, t_gng, g_qng, g_kng, g_w_out, t_n2g, g_w_up, g_conv_w, t_conv_b, g_w_down]
    gs = [g.reshape(w.shape) for g, w in zip(gs, ws)]
    deltas, new_ms, new_vs = [], [], []
    for nm, w, g, m, v in zip(names, ws, gs, ms, vs):
        d_, m_, v_ = _adamw(w, g, m, v, "adamw_" + nm)
        deltas.append(d_)
        new_ms.append(m_)
        new_vs.append(v_)
    return (loss, dx.reshape(x.shape), *gs, *deltas, *new_ms, *new_vs)
```

```python
import functools
import math

import jax
import jax.numpy as jnp
from jax import lax
from jax.experimental import pallas as pl
from jax.experimental.pallas import tpu as pltpu

F32, BF16 = jnp.float32, jnp.bfloat16
HI = lax.Precision.HIGHEST
EPS = 1e-6
NEG = -1e30

N_DEV = 8
GLA_HEADS, GLA_DK, GLA_DV, GLA_RANK, GLA_TAU, GLA_CHUNK = 4, 64, 128, 16, 16.0, 64
ATTN_HEADS, ATTN_HD, ATTN_BLOCK = 8, 64, 128
DILATIONS = (1, 4, 16)
GLA_QK, GLA_V, ATTN_DIM = GLA_HEADS * GLA_DK, GLA_HEADS * GLA_DV, ATTN_HEADS * ATTN_HD
O_GQ, O_GK, O_GV, O_GR, O_AQ, O_AK, O_AV, O_GLR = 0, 256, 512, 1024, 1536, 2048, 2560, 3072
PROJ_W = 3328
LANE = 128
GLR_SRC = 2 * GLA_QK + 2 * GLA_V

ADAM_LR, ADAM_B1, ADAM_B2, ADAM_EPS, ADAM_WD, ADAM_STEP = 0.001, 0.9, 0.999, 1e-08, 0.01, 10

VMEM_LIMIT = 56 * 1024 * 1024
SUM_BLOCK_ELEMS = 256 * 1024


def _cp(*sem):
    return pltpu.CompilerParams(dimension_semantics=sem, vmem_limit_bytes=VMEM_LIMIT)


def _dot(a, b, dims, precision=None):
    return lax.dot_general(a, b, (dims, ((), ())), preferred_element_type=F32, precision=precision)


NN, NT, TN = ((1,), (0,)), ((1,), (1,)), ((0,), (0,))


def _sigmoid(z):
    return 1.0 / (1.0 + jnp.exp(-z))


ANY_SPEC = pl.BlockSpec(memory_space=pl.ANY)


def _exchange_shapes(arrays, gather):
    return [jax.ShapeDtypeStruct((N_DEV,) + (a.shape if g else a.shape[1:]), a.dtype) for a, g in zip(arrays, gather)]


def _exchange_sems(n):
    return [pltpu.SemaphoreType.DMA((n * (N_DEV - 1),)), pltpu.SemaphoreType.DMA((n * (N_DEV - 1),)), pltpu.SemaphoreType.DMA((n,))]


def _exchange_copies(ins, outs, gather, send_sems, recv_sems, local_sems):
    x, y, c = lax.axis_index("x"), lax.axis_index("y"), lax.axis_index("c")
    me = 4 * x + 2 * y + c
    copies = []
    for a in range(len(ins)):
        for p in range(1, N_DEV):
            px, py, pc = x ^ (p >> 2), y ^ ((p >> 1) & 1), c ^ (p & 1)
            peer = 4 * px + 2 * py + pc
            k = a * (N_DEV - 1) + p - 1
            copies.append(pltpu.make_async_remote_copy(
                src_ref=ins[a] if gather[a] else ins[a].at[peer], dst_ref=outs[a].at[me],
                send_sem=send_sems.at[k], recv_sem=recv_sems.at[k],
                device_id=(px, py, pc), device_id_type=pl.DeviceIdType.MESH))
        copies.append(pltpu.make_async_copy(ins[a] if gather[a] else ins[a].at[me], outs[a].at[me], local_sems.at[a]))
    return copies


def _riding(body, n_in, n_out, gather, grid):
    nr = len(gather)
    if not nr:
        return body

    def wrapped(*refs):
        ins, r_ins = refs[:n_in], refs[n_in:n_in + nr]
        outs, r_outs = refs[n_in + nr:n_in + nr + n_out], refs[n_in + nr + n_out:n_in + 2 * nr + n_out]
        scratch = refs[n_in + 2 * nr + n_out:]
        first = last = None
        for t, steps in enumerate(grid):
            pid = pl.program_id(t)
            first = (pid == 0) if first is None else first & (pid == 0)
            last = (pid == steps - 1) if last is None else last & (pid == steps - 1)
        copies = _exchange_copies(r_ins, r_outs, gather, *scratch[-3:])

        @pl.when(first)
        def _():
            for cp in copies:
                cp.start()

        body(*ins, *outs, *scratch[:-3])

        @pl.when(last)
        def _():
            for cp in copies:
                cp.wait()

    return wrapped


def _exchange(arrays, gather, name):
    n = len(arrays)

    def body(*refs):
        copies = _exchange_copies(refs[:n], refs[n:2 * n], gather, *refs[2 * n:])
        for cp in copies:
            cp.start()
        for cp in copies:
            cp.wait()

    return pl.pallas_call(
        body, out_shape=_exchange_shapes(arrays, gather), in_specs=[ANY_SPEC] * n, out_specs=[ANY_SPEC] * n,
        scratch_shapes=_exchange_sems(n), name=name)(*arrays)


def _sum_slots(x, name):
    _, R, C = x.shape
    tr = max(t for t in range(8, min(SUM_BLOCK_ELEMS // C, R) + 1, 8) if R % t == 0)

    def body(x_ref, o_ref):
        acc = x_ref[0]
        for s in range(1, N_DEV):
            acc = acc + x_ref[s]
        o_ref[...] = acc

    return pl.pallas_call(
        body, grid=(R // tr,), in_specs=[pl.BlockSpec((N_DEV, tr, C), lambda i: (0, i, 0))],
        out_specs=pl.BlockSpec((tr, C), lambda i: (i, 0)), out_shape=jax.ShapeDtypeStruct((R, C), x.dtype),
        compiler_params=_cp("parallel"), name=name)(x)


def _mm(a, b, mode, tm, tn, tk, out_dtype, name, ride=None):
    if mode == NN:
        (M, K), N = a.shape, b.shape[1]
    elif mode == NT:
        (M, K), N = a.shape, b.shape[0]
    else:
        (K, M), N = a.shape, b.shape[1]
    tm, tn, tk = min(tm, M), min(tn, N), min(tk, K)
    assert M % tm == 0 and N % tn == 0 and K % tk == 0, (name, M, N, K, tm, tn, tk)
    nk = K // tk
    if mode == NN:
        a_spec = pl.BlockSpec((tm, tk), lambda i, j, k: (i, k))
        b_spec = pl.BlockSpec((tk, tn), lambda i, j, k: (k, j))
    elif mode == NT:
        a_spec = pl.BlockSpec((tm, tk), lambda i, j, k: (i, k))
        b_spec = pl.BlockSpec((tn, tk), lambda i, j, k: (j, k))
    else:
        a_spec = pl.BlockSpec((tk, tm), lambda i, j, k: (k, i))
        b_spec = pl.BlockSpec((tk, tn), lambda i, j, k: (k, j))

    ride_arrays, ride_gather = ride if ride else ([], [])
    nr = len(ride_arrays)
    grid = (M // tm, N // tn, nk)

    own_acc = nk > 1 and out_dtype != F32

    def body(a_ref, b_ref, o_ref, *acc):
        p = _dot(a_ref[...].astype(BF16), b_ref[...].astype(BF16), mode)
        if nk == 1:
            o_ref[...] = p.astype(out_dtype)
        else:
            acc_ref = acc[0] if own_acc else o_ref
            k = pl.program_id(2)

            @pl.when(k == 0)
            def _():
                acc_ref[...] = p

            @pl.when(k > 0)
            def _():
                acc_ref[...] += p

            if own_acc:
                @pl.when(k == nk - 1)
                def _():
                    o_ref[...] = acc_ref[...].astype(out_dtype)

    outs = pl.pallas_call(
        _riding(body, 2, 1, ride_gather, grid), grid=grid, in_specs=[a_spec, b_spec] + [ANY_SPEC] * nr,
        out_specs=[pl.BlockSpec((tm, tn), lambda i, j, k: (i, j))] + [ANY_SPEC] * nr,
        out_shape=[jax.ShapeDtypeStruct((M, N), out_dtype)] + _exchange_shapes(ride_arrays, ride_gather),
        scratch_shapes=([pltpu.VMEM((tm, tn), F32)] if own_acc else []) + (_exchange_sems(nr) if nr else []),
        compiler_params=_cp(*(("arbitrary",) * 3 if nr else ("parallel", "parallel", "arbitrary"))), name=name)(a, b, *ride_arrays)
    return (outs[0], outs[1:]) if nr else outs[0]


def _ada_fwd(c_all, w_shard, b_shard):
    Nc = w_shard.shape[1]

    def body(c_ref, w_ref, b_ref, o_ref):
        cv = c_ref[...]
        o_ref[...] = _dot(cv * _sigmoid(cv), w_ref[...], NN, HI) + b_ref[...]

    return pl.pallas_call(body, out_shape=jax.ShapeDtypeStruct((N_DEV, Nc), F32), name="ada_fwd",
                          compiler_params=pltpu.CompilerParams(vmem_limit_bytes=VMEM_LIMIT))(c_all, w_shard, b_shard)


def _ada_bwd(c_all, dmod_shard):
    D, Nc = c_all.shape[1], dmod_shard.shape[1]

    def body(c_ref, d_ref, o_ref):
        cv = c_ref[...]
        o_ref[...] = _dot(cv * _sigmoid(cv), d_ref[...], TN, HI)

    return pl.pallas_call(body, out_shape=jax.ShapeDtypeStruct((D, Nc), F32), name="ada_bwd",
                          compiler_params=pltpu.CompilerParams(vmem_limit_bytes=VMEM_LIMIT))(c_all, dmod_shard)


def _row_spec(ts, D):
    return pl.BlockSpec((ts, D), lambda i: (i, 0))


def _vec_spec(D):
    return pl.BlockSpec((1, D), lambda i: (0, 0))


def _rms_mod(x, ng, sc, sh, ts, name):
    S, D = x.shape

    def body(x_ref, ng_ref, sc_ref, sh_ref, h_ref):
        xv = x_ref[...]
        r = lax.rsqrt(jnp.mean(xv * xv, axis=-1, keepdims=True) + EPS)
        h_ref[...] = (xv * r * ng_ref[...] * (1.0 + sc_ref[...]) + sh_ref[...]).astype(BF16)

    return pl.pallas_call(
        body, grid=(S // ts,), in_specs=[_row_spec(ts, D)] + [_vec_spec(D)] * 3, out_specs=_row_spec(ts, D),
        out_shape=jax.ShapeDtypeStruct((S, D), BF16), compiler_params=_cp("parallel"), name=name)(x, ng, sc, sh)


def _resid_rms_mod(x, t, g, ng, sc, sh, ts, name):
    S, D = x.shape

    def body(x_ref, t_ref, g_ref, ng_ref, sc_ref, sh_ref, x2_ref, h_ref):
        xv = x_ref[...] + g_ref[...] * t_ref[...]
        x2_ref[...] = xv
        r = lax.rsqrt(jnp.mean(xv * xv, axis=-1, keepdims=True) + EPS)
        h_ref[...] = (xv * r * ng_ref[...] * (1.0 + sc_ref[...]) + sh_ref[...]).astype(BF16)

    return pl.pallas_call(
        body, grid=(S // ts,), in_specs=[_row_spec(ts, D)] * 2 + [_vec_spec(D)] * 4,
        out_specs=[_row_spec(ts, D)] * 2,
        out_shape=[jax.ShapeDtypeStruct((S, D), F32), jax.ShapeDtypeStruct((S, D), BF16)],
        compiler_params=_cp("parallel"), name=name)(x, t, g, ng, sc, sh)


def _rms_mod_bwd(dh, xin, dres, ng, sc, ts, name, t_prev=None, g_prev=None):
    S, D = xin.shape
    chain = t_prev is not None

    def body(*refs):
        if chain:
            dh_ref, x_ref, dr_ref, ng_ref, sc_ref, t_ref, g_ref, dx_ref, sums_ref, dt_ref = refs
        else:
            dh_ref, x_ref, dr_ref, ng_ref, sc_ref, dx_ref, sums_ref = refs
        i = pl.program_id(0)
        xv, dhv = x_ref[...], dh_ref[...]
        r = lax.rsqrt(jnp.mean(xv * xv, axis=-1, keepdims=True) + EPS)
        xh = xv * r
        ngv, scv = ng_ref[...], sc_ref[...]
        dxh = dhv * (ngv * (1.0 + scv))
        dx = dr_ref[...] + r * (dxh - xh * jnp.mean(dxh * xh, axis=-1, keepdims=True))
        dx_ref[...] = dx
        dhx = dhv * xh
        rows = [jnp.sum(dhv, axis=0, keepdims=True), jnp.sum(dhx * ngv, axis=0, keepdims=True),
                jnp.sum(dhx * (1.0 + scv), axis=0, keepdims=True)]
        if chain:
            dt_ref[...] = (dx * g_ref[...]).astype(BF16)
            rows.append(jnp.sum(dx * t_ref[...], axis=0, keepdims=True))
        rows.append(jnp.zeros((8 - len(rows), D), F32))
        part = jnp.concatenate(rows, axis=0)

        @pl.when(i == 0)
        def _():
            sums_ref[...] = part

        @pl.when(i > 0)
        def _():
            sums_ref[...] += part

    row, vec = _row_spec(ts, D), _vec_spec(D)
    sums_spec = pl.BlockSpec((8, D), lambda i: (0, 0))
    ins = [dh, xin, dres, ng, sc] + ([t_prev, g_prev] if chain else [])
    in_specs = [row, row, row, vec, vec] + ([row, vec] if chain else [])
    out_specs = [row, sums_spec] + ([row] if chain else [])
    out_shape = [jax.ShapeDtypeStruct((S, D), F32), jax.ShapeDtypeStruct((8, D), F32)] + (
        [jax.ShapeDtypeStruct((S, D), BF16)] if chain else [])
    return pl.pallas_call(body, grid=(S // ts,), in_specs=in_specs, out_specs=out_specs, out_shape=out_shape,
                          compiler_params=_cp("arbitrary"), name=name)(*ins)


def _gate_fwd(proj, wg_p, bg, ts):
    S = proj.shape[0]

    def body(glr_ref, w_ref, b_ref, la_ref):
        z = _dot(glr_ref[...], w_ref[...], NN, HI) + b_ref[...]
        la_ref[...] = (jnp.minimum(z, 0.0) - jnp.log(1.0 + jnp.exp(-jnp.abs(z)))) * (1.0 / GLA_TAU)

    return pl.pallas_call(
        body, grid=(S // ts,),
        in_specs=[pl.BlockSpec((ts, LANE), lambda i: (i, O_GLR // LANE)), pl.BlockSpec((LANE, GLA_QK), lambda i: (0, 0)),
                  pl.BlockSpec((1, GLA_QK), lambda i: (0, 0))],
        out_specs=pl.BlockSpec((ts, GLA_QK), lambda i: (i, 0)), out_shape=jax.ShapeDtypeStruct((S, GLA_QK), F32),
        compiler_params=_cp("parallel"), name="gla_gate_fwd")(proj, wg_p, bg)


def _gate_bwd(dla, la, proj, wg_p, ts):
    S = proj.shape[0]

    def body(dla_ref, la_ref, glr_ref, w_ref, dglr_ref, gw_ref, gb_ref):
        i = pl.program_id(0)
        dz = dla_ref[...] * (1.0 / GLA_TAU) * (1.0 - jnp.exp(GLA_TAU * la_ref[...]))
        dglr_ref[...] = _dot(dz, w_ref[...], NT, HI).astype(BF16)
        gw = _dot(glr_ref[...], dz, TN, HI)
        gb = jnp.concatenate([jnp.sum(dz, axis=0, keepdims=True), jnp.zeros((7, GLA_QK), F32)], axis=0)

        @pl.when(i == 0)
        def _():
            gw_ref[...] = gw
            gb_ref[...] = gb

        @pl.when(i > 0)
        def _():
            gw_ref[...] += gw
            gb_ref[...] += gb

    return pl.pallas_call(
        body, grid=(S // ts,),
        in_specs=[pl.BlockSpec((ts, GLA_QK), lambda i: (i, 0)), pl.BlockSpec((ts, GLA_QK), lambda i: (i, 0)),
                  pl.BlockSpec((ts, LANE), lambda i: (i, O_GLR // LANE)), pl.BlockSpec((LANE, GLA_QK), lambda i: (0, 0))],
        out_specs=[pl.BlockSpec((ts, LANE), lambda i: (i, 0)), pl.BlockSpec((LANE, GLA_QK), lambda i: (0, 0)),
                   pl.BlockSpec((8, GLA_QK), lambda i: (0, 0))],
        out_shape=[jax.ShapeDtypeStruct((S, LANE), BF16), jax.ShapeDtypeStruct((LANE, GLA_QK), F32),
                   jax.ShapeDtypeStruct((8, GLA_QK), F32)],
        compiler_params=_cp("arbitrary"), name="gla_gate_bwd")(dla, la, proj, wg_p)


def _tri(lower):
    r = lax.broadcasted_iota(jnp.int32, (GLA_CHUNK, GLA_CHUNK), 0)
    c = lax.broadcasted_iota(jnp.int32, (GLA_CHUNK, GLA_CHUNK), 1)
    return jnp.where((r >= c) if lower else (c >= r), 1.0, 0.0).astype(F32)


GLA_SUB = 16
GLA_NSUB = GLA_CHUNK // GLA_SUB
PAIR_QK = 2 * GLA_DK
PAIR_V = 2 * GLA_DV


def _band_selector():
    r = lax.broadcasted_iota(jnp.int32, (GLA_SUB * PAIR_QK, LANE), 0)
    c = lax.broadcasted_iota(jnp.int32, (GLA_SUB * PAIR_QK, LANE), 1)
    dist, head = r // PAIR_QK, (r % PAIR_QK) // GLA_DK
    return jnp.where(c == head * GLA_DK + (GLA_SUB - 1 - dist), 1.0, 0.0).astype(BF16)


def _flip_matrix():
    r = lax.broadcasted_iota(jnp.int32, (GLA_CHUNK, GLA_CHUNK), 0)
    c = lax.broadcasted_iota(jnp.int32, (GLA_CHUNK, GLA_CHUNK), 1)
    return jnp.where(r + c == GLA_CHUNK - 1, 1.0, 0.0).astype(BF16)


def _state_mask():
    r = lax.broadcasted_iota(jnp.int32, (PAIR_V, PAIR_QK), 0)
    c = lax.broadcasted_iota(jnp.int32, (PAIR_V, PAIR_QK), 1)
    return (r < GLA_DV) == (c < GLA_DK)


class _GlaChunk:
    def __init__(self, qs, kc, vc, g, sel):
        C = GLA_CHUNK
        self.qs, self.kc, self.vc = qs, kc, vc
        rows = lax.broadcasted_iota(jnp.int32, (C, 1), 0)
        lane = lax.broadcasted_iota(jnp.int32, (1, PAIR_QK), 1)
        self.rows, self.lane = rows, lane
        b = _dot(_tri(True), g, NN, HI)
        self.bl = b[C - 1:C, :]
        self.eb = jnp.exp(b)
        self.kdec = jnp.exp(self.bl - b)
        edge = lambda J: b[GLA_SUB * (J + 1):GLA_SUB * (J + 1) + 1, :]
        self.e_far = [jnp.exp(jnp.where(rows >= GLA_SUB * (J + 1), b - edge(J), NEG)) for J in range(GLA_NSUB - 1)]
        blk = rows // GLA_SUB
        bnext = edge(0)
        for J in range(1, GLA_NSUB - 1):
            bnext = jnp.where(blk == J, edge(J), bnext)
        self.e_khat = jnp.exp(jnp.where(blk < GLA_NSUB - 1, bnext - b, NEG))
        khat = kc * self.e_khat
        k2 = jnp.concatenate([jnp.where(lane < GLA_DK, khat, 0.0), jnp.where(lane >= GLA_DK, khat, 0.0)], axis=0)
        self.blk2 = jnp.concatenate([blk, blk], axis=0)
        self.m_far = jnp.concatenate([jnp.where(self.blk2 == J, k2, 0.0) for J in range(GLA_NSUB - 1)], axis=1).astype(BF16)
        self.qcat = jnp.concatenate([qs * e for e in self.e_far], axis=1).astype(BF16)
        a_far = _dot(self.qcat, self.m_far, NT)
        self.e_band, self.rk, terms = [], [], []
        for d in range(GLA_SUB):
            rk = pltpu.roll(kc, d, 0) if d else kc
            rb = pltpu.roll(b, d, 0) if d else b
            e = jnp.exp(jnp.where(rows >= d, b - rb, NEG))
            self.e_band.append(e)
            self.rk.append(rk)
            terms.append((qs * rk * e).astype(BF16))
        band = _dot(jnp.concatenate(terms, axis=1), sel, NN)
        a_band = pltpu.roll(band, LANE - (GLA_SUB - 1), 1, stride=1, stride_axis=0)
        dist = rows - lane % GLA_DK
        self.far_mask = dist >= GLA_SUB
        self.band_mask = (dist >= 0) & (dist < GLA_SUB)
        self.a = (a_band + jnp.where(self.far_mask, a_far, 0.0)).astype(BF16)
        self.lane_v = lax.broadcasted_iota(jnp.int32, (1, PAIR_V), 1)
        self.v2 = jnp.concatenate([jnp.where(self.lane_v < GLA_DV, vc, 0.0), jnp.where(self.lane_v >= GLA_DV, vc, 0.0)],
                                  axis=0).astype(BF16)


def _gla_fwd(proj, la, tb):
    S = proj.shape[0]
    C = GLA_CHUNK
    tb = min(tb, S)
    nbc = tb // C
    npair = GLA_HEADS // 2
    scale = GLA_DK ** -0.5

    def body(q_ref, k_ref, v_ref, la_ref, sel_ref, o_ref, st_ref, state):
        @pl.when(pl.program_id(1) == 0)
        def _():
            state[...] = jnp.zeros_like(state)

        def chunk(ci, carry):
            sl = pl.ds(pl.multiple_of(ci * C, C), C)
            ch = _GlaChunk(q_ref[sl, :] * scale, k_ref[sl, :], v_ref[sl, :], la_ref[sl, :], sel_ref[...])
            st = state[...]
            st_ref[0, ci] = st
            o_ref[sl, :] = _dot((ch.qs * ch.eb).astype(BF16), st.astype(BF16), NT) + _dot(ch.a, ch.v2, NN)
            upd = _dot(ch.vc.astype(BF16), (ch.kc * ch.kdec).astype(BF16), TN)
            state[...] = st * jnp.exp(ch.bl) + jnp.where(_state_mask(), upd, 0.0)
            return carry

        lax.fori_loop(0, nbc, chunk, 0, unroll=2)

    qspec = lambda off: pl.BlockSpec((tb, PAIR_QK), lambda p, i: (i, off // PAIR_QK + p))
    return pl.pallas_call(
        body, grid=(npair, S // tb),
        in_specs=[qspec(O_GQ), qspec(O_GK), pl.BlockSpec((tb, PAIR_V), lambda p, i: (i, O_GV // PAIR_V + p)),
                  pl.BlockSpec((tb, PAIR_QK), lambda p, i: (i, p)),
                  pl.BlockSpec((GLA_SUB * PAIR_QK, LANE), lambda p, i: (0, 0))],
        out_specs=[pl.BlockSpec((tb, PAIR_V), lambda p, i: (i, p)),
                   pl.BlockSpec((1, nbc, PAIR_V, PAIR_QK), lambda p, i: (p, i, 0, 0))],
        out_shape=[jax.ShapeDtypeStruct((S, GLA_V), F32), jax.ShapeDtypeStruct((npair, S // C, PAIR_V, PAIR_QK), F32)],
        scratch_shapes=[pltpu.VMEM((PAIR_V, PAIR_QK), F32)],
        compiler_params=_cp("parallel", "arbitrary"), name="gla_fwd")(proj, proj, proj, la, _band_selector())


def _gla_bwd(proj, la, do, states, tb, ride=None):
    S = proj.shape[0]
    C = GLA_CHUNK
    tb = min(tb, S)
    nbc = tb // C
    nblk = S // tb
    npair = GLA_HEADS // 2
    scale = GLA_DK ** -0.5

    def body(q_ref, k_ref, v_ref, la_ref, do_ref, st_ref, sel_ref, selt_ref, dq_ref, dk_ref, dv_ref, dla_ref, dstate):
        @pl.when(pl.program_id(1) == 0)
        def _():
            dstate[...] = jnp.zeros_like(dstate)

        def chunk(cc, carry):
            ci = nbc - 1 - cc
            sl = pl.ds(pl.multiple_of(ci * C, C), C)
            ch = _GlaChunk(q_ref[sl, :] * scale, k_ref[sl, :], v_ref[sl, :], la_ref[sl, :], sel_ref[...])
            qs, kc, rows = ch.qs, ch.kc, ch.rows
            doc_b = do_ref[sl, :].astype(BF16)
            st = st_ref[0, ci]
            dst = dstate[...]
            dst_b = dst.astype(BF16)
            ebl = jnp.exp(ch.bl)
            dq = _dot(doc_b, st.astype(BF16), NN) * ch.eb
            dk = _dot(ch.vc.astype(BF16), dst_b, NN) * ch.kdec
            dv = _dot((kc * ch.kdec).astype(BF16), dst_b, NT)
            dbl = jnp.sum(dst * st, axis=0, keepdims=True) * ebl + jnp.sum(kc * dk, axis=0, keepdims=True)
            da = _dot(doc_b, ch.v2, NT)
            dv2 = _dot(ch.a, doc_b, TN)
            dv = dv + jnp.where(ch.lane_v < GLA_DV, dv2[:C], dv2[C:])
            da_far = jnp.where(ch.far_mask, da, 0.0).astype(BF16)
            dqcat = _dot(da_far, ch.m_far, NN)
            dm = _dot(da_far, ch.qcat, TN)
            dk2 = jnp.zeros((2 * C, PAIR_QK), F32)
            for J in range(GLA_NSUB - 1):
                dq = dq + dqcat[:, J * PAIR_QK:(J + 1) * PAIR_QK] * ch.e_far[J]
                dk2 = dk2 + jnp.where(ch.blk2 == J, dm[:, J * PAIR_QK:(J + 1) * PAIR_QK], 0.0)
            dk = dk + jnp.where(ch.lane < GLA_DK, dk2[:C], dk2[C:]) * ch.e_khat
            flip = _flip_matrix()
            da_band = _dot(flip, jnp.where(ch.band_mask, da, 0.0).astype(BF16), NN)
            dband = pltpu.roll(da_band, LANE - (C - GLA_SUB), 1, stride=1, stride_axis=0)
            dband = _dot(flip, dband.astype(BF16), NN)
            dterms = _dot(dband.astype(BF16), selt_ref[...], NN)
            for d in range(GLA_SUB):
                dt = dterms[:, d * PAIR_QK:(d + 1) * PAIR_QK]
                dq = dq + dt * (ch.rk[d] * ch.e_band[d])
                dkr = dt * (qs * ch.e_band[d])
                dk = dk + (pltpu.roll(dkr, C - d, 0) if d else dkr)
            db = qs * dq - kc * dk
            db = jnp.where(rows == C - 1, db + dbl, db)
            dq_ref[sl, :] = (dq * scale).astype(BF16)
            dk_ref[sl, :] = dk.astype(BF16)
            dv_ref[sl, :] = dv.astype(BF16)
            dla_ref[sl, :] = _dot(_tri(False), db, NN, HI)
            upd = _dot(doc_b, (qs * ch.eb).astype(BF16), TN)
            dstate[...] = dst * ebl + jnp.where(_state_mask(), upd, 0.0)
            return carry

        lax.fori_loop(0, nbc, chunk, 0, unroll=2)

    rev = lambda i: nblk - 1 - i
    qspec = lambda off: pl.BlockSpec((tb, PAIR_QK), lambda p, i: (rev(i), off // PAIR_QK + p))
    pair_qk = pl.BlockSpec((tb, PAIR_QK), lambda p, i: (rev(i), p))
    pair_v = pl.BlockSpec((tb, PAIR_V), lambda p, i: (rev(i), p))
    sel = _band_selector()
    ride_arrays, ride_gather = ride if ride else ([], [])
    nr = len(ride_arrays)
    grid = (npair, nblk)
    outs = pl.pallas_call(
        _riding(body, 8, 4, ride_gather, grid), grid=grid,
        in_specs=[qspec(O_GQ), qspec(O_GK), pl.BlockSpec((tb, PAIR_V), lambda p, i: (rev(i), O_GV // PAIR_V + p)),
                  pair_qk, pair_v, pl.BlockSpec((1, nbc, PAIR_V, PAIR_QK), lambda p, i: (p, rev(i), 0, 0)),
                  pl.BlockSpec((GLA_SUB * PAIR_QK, LANE), lambda p, i: (0, 0)),
                  pl.BlockSpec((LANE, GLA_SUB * PAIR_QK), lambda p, i: (0, 0))] + [ANY_SPEC] * nr,
        out_specs=[pair_qk, pair_qk, pair_v, pair_qk] + [ANY_SPEC] * nr,
        out_shape=[jax.ShapeDtypeStruct((S, GLA_QK), BF16), jax.ShapeDtypeStruct((S, GLA_QK), BF16),
                   jax.ShapeDtypeStruct((S, GLA_V), BF16), jax.ShapeDtypeStruct((S, GLA_QK), F32)]
        + _exchange_shapes(ride_arrays, ride_gather),
        scratch_shapes=[pltpu.VMEM((PAIR_V, PAIR_QK), F32)] + (_exchange_sems(nr) if nr else []),
        compiler_params=_cp("arbitrary", "arbitrary"), name="gla_bwd")(proj, proj, proj, la, do, states, sel, sel.T, *ride_arrays)
    return outs[0], outs[1], outs[2], outs[3], outs[4:]


def _gla_out(o, proj, gng, ts):
    S = o.shape[0]

    def body(o_ref, gr_ref, g_ref, y_ref):
        for h in range(GLA_HEADS):
            cols = slice(h * GLA_DV, (h + 1) * GLA_DV)
            ov, grv = o_ref[:, cols], gr_ref[:, cols]
            r = lax.rsqrt(jnp.mean(ov * ov, axis=-1, keepdims=True) + EPS)
            y_ref[:, cols] = (ov * r * g_ref[...] * (grv * _sigmoid(grv))).astype(BF16)

    return pl.pallas_call(
        body, grid=(S // ts,),
        in_specs=[pl.BlockSpec((ts, GLA_V), lambda i: (i, 0)), pl.BlockSpec((ts, GLA_V), lambda i: (i, O_GR // GLA_V)),
                  pl.BlockSpec((1, GLA_DV), lambda i: (0, 0))],
        out_specs=pl.BlockSpec((ts, GLA_V), lambda i: (i, 0)), out_shape=jax.ShapeDtypeStruct((S, GLA_V), BF16),
        compiler_params=_cp("parallel"), name="gla_out_fwd")(o, proj, gng)


def _gla_out_bwd(dmixed, o, proj, gng, ts):
    S = o.shape[0]

    def body(dy_ref, o_ref, gr_ref, g_ref, do_ref, dgr_ref, gg_ref):
        i = pl.program_id(0)
        gsum = jnp.zeros((1, GLA_DV), F32)
        for h in range(GLA_HEADS):
            cols = slice(h * GLA_DV, (h + 1) * GLA_DV)
            ov, grv, dy = o_ref[:, cols], gr_ref[:, cols], dy_ref[:, cols]
            r = lax.rsqrt(jnp.mean(ov * ov, axis=-1, keepdims=True) + EPS)
            oh = ov * r
            sg = _sigmoid(grv)
            silu = grv * sg
            don = dy * silu
            dgr_ref[:, cols] = (dy * (oh * g_ref[...]) * (sg * (1.0 + grv * (1.0 - sg)))).astype(BF16)
            gsum = gsum + jnp.sum(don * oh, axis=0, keepdims=True)
            doh = don * g_ref[...]
            do_ref[:, cols] = r * (doh - oh * jnp.mean(doh * oh, axis=-1, keepdims=True))
        part = jnp.concatenate([gsum, jnp.zeros((7, GLA_DV), F32)], axis=0)

        @pl.when(i == 0)
        def _():
            gg_ref[...] = part

        @pl.when(i > 0)
        def _():
            gg_ref[...] += part

    return pl.pallas_call(
        body, grid=(S // ts,),
        in_specs=[pl.BlockSpec((ts, GLA_V), lambda i: (i, 0)), pl.BlockSpec((ts, GLA_V), lambda i: (i, 0)),
                  pl.BlockSpec((ts, GLA_V), lambda i: (i, O_GR // GLA_V)), pl.BlockSpec((1, GLA_DV), lambda i: (0, 0))],
        out_specs=[pl.BlockSpec((ts, GLA_V), lambda i: (i, 0)), pl.BlockSpec((ts, GLA_V), lambda i: (i, 0)),
                   pl.BlockSpec((8, GLA_DV), lambda i: (0, 0))],
        out_shape=[jax.ShapeDtypeStruct((S, GLA_V), F32), jax.ShapeDtypeStruct((S, GLA_V), BF16),
                   jax.ShapeDtypeStruct((8, GLA_DV), F32)],
        compiler_params=_cp("arbitrary"), name="gla_out_bwd")(dmixed, o, proj, gng)


def _seg_matrix(width, seg, value):
    r = lax.broadcasted_iota(jnp.int32, (width, width), 0) // seg
    c = lax.broadcasted_iota(jnp.int32, (width, width), 1) // seg
    return jnp.where(r == c, value, 0.0).astype(BF16)


def _seg_sum(x, seg_matrix):
    hi = x.astype(BF16)
    lo = (x - hi.astype(F32)).astype(BF16)
    return _dot(hi, seg_matrix, NN) + _dot(lo, seg_matrix, NN)


def _head_norm(proj, qg, kg, ts):
    S = proj.shape[0]
    W = ATTN_DIM

    def body(q_ref, k_ref, qg_ref, kg_ref, qn_ref, kn_ref):
        seg = _seg_matrix(W, ATTN_HD, 1.0 / ATTN_HD)
        for x_ref, g_ref, o_ref, scale in ((q_ref, qg_ref, qn_ref, ATTN_HD ** -0.5), (k_ref, kg_ref, kn_ref, 1.0)):
            xv = x_ref[...]
            ms = _seg_sum(xv * xv, seg)
            o_ref[...] = xv * lax.rsqrt(ms + EPS) * (g_ref[...] * scale)

    blk = lambda off: pl.BlockSpec((ts, W), lambda i: (i, off // W))
    out = pl.BlockSpec((ts, W), lambda i: (i, 0))
    vec = pl.BlockSpec((1, W), lambda i: (0, 0))
    return pl.pallas_call(
        body, grid=(S // ts,), in_specs=[blk(O_AQ), blk(O_AK), vec, vec], out_specs=[out] * 2,
        out_shape=[jax.ShapeDtypeStruct((S, W), F32)] * 2, compiler_params=_cp("parallel"), name="attn_head_norm")(
            proj, proj, qg, kg)


def _slope(head):
    one = jnp.ones((1, 1), jnp.int32)
    return 1.0 / jnp.left_shift(one, one * (head + 1)).astype(F32)


ATTN_GROUP = 4


def _attn_rows(d, g, r):
    start = g * d * ATTN_BLOCK + r
    return pl.ds(start, ATTN_BLOCK) if d == 1 else pl.ds(start, ATTN_BLOCK, stride=d)


def _for_blocks(d, G, fn):
    for g in range(G):
        if d <= ATTN_GROUP:
            for r in range(d):
                fn(g, r)
        else:
            def step(r, carry, g=g):
                fn(g, r)
                return carry
            lax.fori_loop(0, d, step, 0, unroll=ATTN_GROUP)


def _attn_specs(d, S):
    G = max(1, ATTN_GROUP // d)
    edge = d * ATTN_BLOCK
    tq = G * edge
    nb, n_edge = S // tq, S // edge

    def specs(off=0):
        return [pl.BlockSpec((tq, LANE), lambda hp, n: (n, off + hp)),
                pl.BlockSpec((edge, LANE), lambda hp, n: (jnp.maximum(n * G - 1, 0), off + hp)),
                pl.BlockSpec((edge, LANE), lambda hp, n: (jnp.minimum((n + 1) * G, n_edge - 1), off + hp))]

    return G, nb, specs


def _attn_bias(d, hp, first_tile):
    B = ATTN_BLOCK
    iq = lax.broadcasted_iota(jnp.int32, (B, 2 * B), 0)
    ik = lax.broadcasted_iota(jnp.int32, (B, 2 * B), 1)
    rel = iq + B - ik
    window = (rel >= 0) & (rel <= B)
    relf = (d * rel).astype(F32)
    full = [jnp.where(window, -_slope(hp * 2 + h) * relf, NEG) for h in range(2)]
    edge = [jnp.where((ik >= B) | jnp.logical_not(first_tile), b, NEG) for b in full]
    return full, edge


def _attn_bias_t(d, hp, has_next):
    B = ATTN_BLOCK
    ik = lax.broadcasted_iota(jnp.int32, (B, B), 0)
    iq = lax.broadcasted_iota(jnp.int32, (B, B), 1)
    tiles = []
    for nxt in range(2):
        rel = iq - ik + nxt * B
        window = (rel >= 0) & (rel <= B)
        relf = (d * rel).astype(F32)
        tiles.append([jnp.where(window, -_slope(hp * 2 + h) * relf, NEG) for h in range(2)])
    tiles.append([jnp.where(has_next, b, NEG) for b in tiles[1]])
    return tiles


def _attn_fwd(qn, kn, proj, d):
    S, W = qn.shape
    G, nb, specs = _attn_specs(d, S)

    def body(q_ref, kp_ref, kc_ref, vp_ref, vc_ref, o_ref, l_ref):
        hp, n = pl.program_id(0), pl.program_id(1)
        lo = lax.broadcasted_iota(jnp.int32, (1, LANE), 1) < ATTN_HD
        full, edge = _attn_bias(d, hp, n == 0)

        def sub(g, r):
            rows = _attn_rows(d, g, r)
            before = _attn_rows(d, max(g - 1, 0), r)
            kb_ref, vb_ref = (kp_ref, vp_ref) if g == 0 else (kc_ref, vc_ref)
            bias = edge if g == 0 else full
            qv = q_ref[rows, :].astype(BF16)
            kv = jnp.concatenate([kb_ref[before, :], kc_ref[rows, :]], axis=0).astype(BF16)
            vv = jnp.concatenate([vb_ref[before, :], vc_ref[rows, :]], axis=0).astype(BF16)
            outs, lses = [], []
            for h in range(2):
                qm = jnp.where(lo == (h == 0), qv, jnp.zeros_like(qv))
                s = _dot(qm, kv, NT) + bias[h]
                m = jnp.max(s, axis=-1, keepdims=True)
                p = jnp.exp(s - m)
                den = jnp.sum(p, axis=-1, keepdims=True)
                outs.append(_dot(p.astype(BF16), vv, NN) / den)
                lses.append(m + jnp.log(den))
            o_ref[rows, :] = jnp.where(lo, outs[0], outs[1])
            l_ref[rows, :] = jnp.where(lo, lses[0], lses[1])

        _for_blocks(d, G, sub)

    cur, prev, _ = specs()
    vcur, vprev, _ = specs(O_AV // LANE)
    return pl.pallas_call(
        body, grid=(W // LANE, nb), in_specs=[cur, prev, cur, vprev, vcur], out_specs=[cur, cur],
        out_shape=[jax.ShapeDtypeStruct((S, W), F32)] * 2,
        compiler_params=_cp("parallel", "arbitrary"), name=f"attn_fwd_d{d}")(qn, kn, kn, proj, proj)


def _attn_merge(y_gla, os_, ls_, ts):
    S, W = os_[0].shape

    def body(yg, o1, o2, o3, l1, l2, l3, mixed_ref, y_ref, lse_ref):
        a, b, c = l1[...], l2[...], l3[...]
        m = jnp.maximum(jnp.maximum(a, b), c)
        ea, eb, ec = jnp.exp(a - m), jnp.exp(b - m), jnp.exp(c - m)
        tot = ea + eb + ec
        y = (ea * o1[...] + eb * o2[...] + ec * o3[...]) / tot
        y_ref[...] = y
        mixed_ref[:, :W] = yg[...]
        mixed_ref[:, W:] = y.astype(BF16)
        lse_ref[...] = m + jnp.log(tot)

    spec = pl.BlockSpec((ts, W), lambda i: (i, 0))
    return pl.pallas_call(
        body, grid=(S // ts,), in_specs=[spec] * 7, out_specs=[pl.BlockSpec((ts, 2 * W), lambda i: (i, 0)), spec, spec],
        out_shape=[jax.ShapeDtypeStruct((S, 2 * W), BF16), jax.ShapeDtypeStruct((S, W), F32), jax.ShapeDtypeStruct((S, W), F32)],
        compiler_params=_cp("parallel"), name="attn_merge")(y_gla, *os_, *ls_)


def _attn_delta(dmixed, y, ts):
    S, W = y.shape

    def body(dy_ref, y_ref, d_ref):
        d_ref[...] = _seg_sum(dy_ref[...] * y_ref[...], _seg_matrix(W, ATTN_HD, 1.0))

    return pl.pallas_call(
        body, grid=(S // ts,), in_specs=[pl.BlockSpec((ts, W), lambda i: (i, 1)), pl.BlockSpec((ts, W), lambda i: (i, 0))],
        out_specs=pl.BlockSpec((ts, W), lambda i: (i, 0)), out_shape=jax.ShapeDtypeStruct((S, W), F32),
        compiler_params=_cp("parallel"), name="attn_delta")(dmixed, y)


def _attn_dq(qn, kn, proj, dmixed, lse, delta, d):
    S, W = qn.shape
    B = ATTN_BLOCK
    G, nb, specs = _attn_specs(d, S)

    def body(q_ref, kp_ref, kc_ref, vp_ref, vc_ref, dy_ref, l_ref, de_ref, dq_ref):
        hp, n = pl.program_id(0), pl.program_id(1)
        lo = lax.broadcasted_iota(jnp.int32, (1, LANE), 1) < ATTN_HD
        full, edge = _attn_bias(d, hp, n == 0)

        def sub(g, r):
            rows = _attn_rows(d, g, r)
            before = _attn_rows(d, max(g - 1, 0), r)
            kb_ref, vb_ref = (kp_ref, vp_ref) if g == 0 else (kc_ref, vc_ref)
            bias = edge if g == 0 else full
            qv, dyv = q_ref[rows, :].astype(BF16), dy_ref[rows, :]
            lv, dev = l_ref[rows, :], de_ref[rows, :]
            kv = jnp.concatenate([kb_ref[before, :], kc_ref[rows, :]], axis=0).astype(BF16)
            vv = jnp.concatenate([vb_ref[before, :], vc_ref[rows, :]], axis=0).astype(BF16)
            outs = []
            for h in range(2):
                sel = lo == (h == 0)
                qm = jnp.where(sel, qv, jnp.zeros_like(qv))
                dym = jnp.where(sel, dyv, 0.0).astype(BF16)
                lse_h = lv[:, h * ATTN_HD:h * ATTN_HD + 1]
                del_h = dev[:, h * ATTN_HD:h * ATTN_HD + 1]
                p = jnp.exp(_dot(qm, kv, NT) + bias[h] - lse_h)
                ds = p * (_dot(dym, vv, NT) - del_h)
                outs.append(_dot(ds.astype(BF16), kv, NN) * (ATTN_HD ** -0.5))
            dq_ref[rows, :] = jnp.where(lo, outs[0], outs[1])

        _for_blocks(d, G, sub)

    cur, prev, _ = specs()
    vcur, vprev, _ = specs(O_AV // LANE)
    dycur, _, _ = specs(W // LANE)
    return pl.pallas_call(
        body, grid=(W // LANE, nb), in_specs=[cur, prev, cur, vprev, vcur, dycur, cur, cur], out_specs=cur,
        out_shape=jax.ShapeDtypeStruct((S, W), F32),
        compiler_params=_cp("parallel", "arbitrary"), name=f"attn_dq_d{d}")(qn, kn, kn, proj, proj, dmixed, lse, delta)


def _attn_dkv(qn, kn, proj, dmixed, lse, delta, d):
    S, W = qn.shape
    B = ATTN_BLOCK
    G, nb, specs = _attn_specs(d, S)

    def body(k_ref, v_ref, qc_ref, qn_ref, dyc_ref, dyn_ref, lc_ref, ln_ref, dec_ref, den_ref, dk_ref, dv_ref):
        hp, n = pl.program_id(0), pl.program_id(1)
        lo = lax.broadcasted_iota(jnp.int32, (1, LANE), 1) < ATTN_HD
        own, inner, outer = _attn_bias_t(d, hp, n + 1 < nb)

        def sub(g, r):
            rows = _attn_rows(d, g, r)
            kv, vv = k_ref[rows, :].astype(BF16), v_ref[rows, :].astype(BF16)
            dk = jnp.zeros((B, LANE), F32)
            dv = jnp.zeros((B, LANE), F32)
            inside = g + 1 < G
            after = _attn_rows(d, g + 1 if inside else 0, r)
            following = (qc_ref, dyc_ref, lc_ref, dec_ref) if inside else (qn_ref, dyn_ref, ln_ref, den_ref)
            for bias, qrows, (q_ref, dy_ref, l_ref, de_ref) in (
                    (own, rows, (qc_ref, dyc_ref, lc_ref, dec_ref)), (inner if inside else outer, after, following)):
                qv, dyv = q_ref[qrows, :].astype(BF16), dy_ref[qrows, :]
                lt, det = l_ref[qrows, :].T, de_ref[qrows, :].T
                for h in range(2):
                    sel = lo == (h == 0)
                    qm = jnp.where(sel, qv, jnp.zeros_like(qv))
                    dym = jnp.where(sel, dyv, 0.0).astype(BF16)
                    lse_h = lt[h * ATTN_HD:h * ATTN_HD + 1, :]
                    del_h = det[h * ATTN_HD:h * ATTN_HD + 1, :]
                    pt = jnp.exp(_dot(kv, qm, NT) + bias[h] - lse_h)
                    dv = dv + _dot(pt.astype(BF16), dym, NN)
                    dst = pt * (_dot(vv, dym, NT) - del_h)
                    dk = dk + _dot(dst.astype(BF16), qm, NN)
            dk_ref[rows, :] = dk
            dv_ref[rows, :] = dv

        _for_blocks(d, G, sub)

    cur, _, nxt = specs()
    vcur, _, _ = specs(O_AV // LANE)
    dycur, _, dynxt = specs(W // LANE)
    return pl.pallas_call(
        body, grid=(W // LANE, nb), in_specs=[cur, vcur, cur, nxt, dycur, dynxt, cur, nxt, cur, nxt], out_specs=[cur, cur],
        out_shape=[jax.ShapeDtypeStruct((S, W), F32)] * 2,
        compiler_params=_cp("parallel", "arbitrary"), name=f"attn_dkv_d{d}")(
            kn, proj, qn, qn, dmixed, dmixed, lse, lse, delta, delta)


def _attn_post(dqs, dks, dvs, proj, qg, kg, ts):
    S = proj.shape[0]
    W = ATTN_DIM

    def body(dq1, dq2, dq3, dk1, dk2, dk3, dv1, dv2, dv3, aq_ref, ak_ref, qg_ref, kg_ref, daq_ref, dak_ref, dav_ref, gg_ref):
        i = pl.program_id(0)
        seg = _seg_matrix(W, ATTN_HD, 1.0 / ATTN_HD)
        gsums = []
        for (d1, d2, d3), x_ref, g_ref, o_ref in (((dq1, dq2, dq3), aq_ref, qg_ref, daq_ref), ((dk1, dk2, dk3), ak_ref, kg_ref, dak_ref)):
            dy = d1[...] + d2[...] + d3[...]
            xv = x_ref[...]
            r = lax.rsqrt(_seg_sum(xv * xv, seg) + EPS)
            xh = xv * r
            dxh = dy * g_ref[...]
            o_ref[...] = (r * (dxh - xh * _seg_sum(dxh * xh, seg))).astype(BF16)
            gsums.append(jnp.sum(dy * xh, axis=0, keepdims=True))
        dav_ref[...] = (dv1[...] + dv2[...] + dv3[...]).astype(BF16)
        part = jnp.concatenate(gsums + [jnp.zeros((6, W), F32)], axis=0)

        @pl.when(i == 0)
        def _():
            gg_ref[...] = part

        @pl.when(i > 0)
        def _():
            gg_ref[...] += part

    row = pl.BlockSpec((ts, W), lambda i: (i, 0))
    blk = lambda off: pl.BlockSpec((ts, W), lambda i: (i, off // W))
    vec = pl.BlockSpec((1, W), lambda i: (0, 0))
    return pl.pallas_call(
        body, grid=(S // ts,), in_specs=[row] * 9 + [blk(O_AQ), blk(O_AK), vec, vec],
        out_specs=[row, row, row, pl.BlockSpec((8, W), lambda i: (0, 0))],
        out_shape=[jax.ShapeDtypeStruct((S, W), BF16)] * 3 + [jax.ShapeDtypeStruct((8, W), F32)],
        compiler_params=_cp("arbitrary"), name="attn_post")(*dqs, *dks, *dvs, proj, proj, qg, kg)


def _shift_down(cur, halo, n):
    ts = cur.shape[0]
    rows = lax.broadcasted_iota(jnp.int32, (ts, 1), 0)
    out = pltpu.roll(cur, n, 0)
    for t in range(n):
        out = jnp.where(rows == t, halo[8 - n + t:8 - n + t + 1, :], out)
    return out


def _shift_up(cur, halo, n):
    ts = cur.shape[0]
    rows = lax.broadcasted_iota(jnp.int32, (ts, 1), 0)
    out = pltpu.roll(cur, ts - n, 0)
    for t in range(n):
        out = jnp.where(rows == ts - n + t, halo[t:t + 1, :], out)
    return out


def _conv(cur, halo, w, b):
    return b + w[0:1, :] * _shift_down(cur, halo, 2) + w[1:2, :] * _shift_down(cur, halo, 1) + w[2:3, :] * cur


def _conv_swiglu(u0, conv_w8, conv_b, ts, tc):
    S, F2 = u0.shape
    F = F2 // 2
    nc = F // tc
    hb = ts // 8

    def body(ug_ref, ugh_ref, uv_ref, uvh_ref, wg_ref, wv_ref, bg_ref, bv_ref, a_ref):
        first = pl.program_id(0) == 0
        ugh = jnp.where(first, 0.0, ugh_ref[...])
        uvh = jnp.where(first, 0.0, uvh_ref[...])
        g = _conv(ug_ref[...], ugh, wg_ref[...], bg_ref[...])
        v = _conv(uv_ref[...], uvh, wv_ref[...], bv_ref[...])
        a_ref[...] = (g * _sigmoid(g) * v).astype(BF16)

    main = lambda off: pl.BlockSpec((ts, tc), lambda i, j: (i, j + off))
    halo = lambda off: pl.BlockSpec((8, tc), lambda i, j: (jnp.maximum(i * hb - 1, 0), j + off))
    wspec = lambda off: pl.BlockSpec((8, tc), lambda i, j: (0, j + off))
    bspec = lambda off: pl.BlockSpec((1, tc), lambda i, j: (0, j + off))
    return pl.pallas_call(
        body, grid=(S // ts, nc),
        in_specs=[main(0), halo(0), main(nc), halo(nc), wspec(0), wspec(nc), bspec(0), bspec(nc)],
        out_specs=pl.BlockSpec((ts, tc), lambda i, j: (i, j)), out_shape=jax.ShapeDtypeStruct((S, F), BF16),
        compiler_params=_cp("parallel", "parallel"), name="conv_swiglu")(u0, u0, u0, u0, conv_w8, conv_w8, conv_b, conv_b)


def _ffn_du(da, u0, conv_w8, conv_b, ts, tc, ride=None):
    S, F2 = u0.shape
    F = F2 // 2
    nc = F // tc
    hb = ts // 8
    grid = (nc, S // ts)
    ride_arrays, ride_gather = ride if ride else ([], [])
    nr = len(ride_arrays)

    def body(da_ref, ug_ref, ugh_ref, uv_ref, uvh_ref, wg_ref, wv_ref, bg_ref, bv_ref, du_ref, sg_ref, sv_ref):
        i = pl.program_id(1)
        first = i == 0
        halves = []
        for u_ref, h_ref, w_ref, b_ref in ((ug_ref, ugh_ref, wg_ref, bg_ref), (uv_ref, uvh_ref, wv_ref, bv_ref)):
            u, halo, w = u_ref[...], jnp.where(first, 0.0, h_ref[...]), w_ref[...]
            s2, s1 = _shift_down(u, halo, 2), _shift_down(u, halo, 1)
            halves.append((b_ref[...] + w[0:1, :] * s2 + w[1:2, :] * s1 + w[2:3, :] * u, s2, s1, u))
        g, v = halves[0][0], halves[1][0]
        dav = da_ref[...]
        sig = _sigmoid(g)
        dus = (dav * v * (sig * (1.0 + g * (1.0 - sig))), dav * (g * sig))
        for h, (du, sums_ref) in enumerate(zip(dus, (sg_ref, sv_ref))):
            du_ref[h] = du
            _, s2, s1, u = halves[h]
            part = jnp.concatenate([jnp.sum(du * s2, axis=0, keepdims=True), jnp.sum(du * s1, axis=0, keepdims=True),
                                    jnp.sum(du * u, axis=0, keepdims=True), jnp.sum(du, axis=0, keepdims=True),
                                    jnp.zeros((4, tc), F32)], axis=0)

            @pl.when(first)
            def _(sums_ref=sums_ref, part=part):
                sums_ref[...] = part

            @pl.when(i > 0)
            def _(sums_ref=sums_ref, part=part):
                sums_ref[...] += part

    main = lambda off: pl.BlockSpec((ts, tc), lambda j, i: (i, j + off))
    halo = lambda off: pl.BlockSpec((8, tc), lambda j, i: (jnp.maximum(i * hb - 1, 0), j + off))
    wspec = lambda off: pl.BlockSpec((8, tc), lambda j, i: (0, j + off))
    bspec = lambda off: pl.BlockSpec((1, tc), lambda j, i: (0, j + off))
    sums_spec = pl.BlockSpec((8, tc), lambda j, i: (0, j))
    outs = pl.pallas_call(
        _riding(body, 9, 3, ride_gather, grid), grid=grid,
        in_specs=[main(0), main(0), halo(0), main(nc), halo(nc), wspec(0), wspec(nc), bspec(0), bspec(nc)] + [ANY_SPEC] * nr,
        out_specs=[pl.BlockSpec((2, ts, tc), lambda j, i: (0, i, j)), sums_spec, sums_spec] + [ANY_SPEC] * nr,
        out_shape=[jax.ShapeDtypeStruct((2, S, F), F32), jax.ShapeDtypeStruct((8, F), F32), jax.ShapeDtypeStruct((8, F), F32)]
        + _exchange_shapes(ride_arrays, ride_gather),
        scratch_shapes=_exchange_sems(nr) if nr else [],
        compiler_params=_cp("arbitrary", "arbitrary"), name="ffn_du")(
            da, u0, u0, u0, u0, conv_w8, conv_w8, conv_b, conv_b, *ride_arrays)
    return outs[0], outs[1], outs[2], outs[3:]


def _ffn_du0(du, conv_w8, ts, tc):
    _, S, F = du.shape
    nc = F // tc
    hb = ts // 8
    nrow = S // ts

    def body(du_ref, duh_ref, w_ref, o_ref):
        last = pl.program_id(0) == nrow - 1
        cur, halo, w = du_ref[...], jnp.where(last, 0.0, duh_ref[...]), w_ref[...]
        o_ref[...] = (w[2:3, :] * cur + w[1:2, :] * _shift_up(cur, halo, 1) + w[0:1, :] * _shift_up(cur, halo, 2)).astype(BF16)

    return pl.pallas_call(
        body, grid=(nrow, 2, nc),
        in_specs=[pl.BlockSpec((None, ts, tc), lambda i, h, j: (h, i, j)),
                  pl.BlockSpec((None, 8, tc), lambda i, h, j: (h, jnp.minimum((i + 1) * hb, S // 8 - 1), j)),
                  pl.BlockSpec((8, tc), lambda i, h, j: (0, h * nc + j))],
        out_specs=pl.BlockSpec((ts, tc), lambda i, h, j: (i, h * nc + j)), out_shape=jax.ShapeDtypeStruct((S, 2 * F), BF16),
        compiler_params=_cp("parallel", "parallel", "parallel"), name="ffn_du0")(du, du, conv_w8)


def _loss_resid(x2, t2, g2, target, ts):
    S, D = x2.shape

    def body(x_ref, t_ref, g_ref, y_ref, dx_ref, dt_ref, sums_ref):
        i = pl.program_id(0)
        tv, gv = t_ref[...], g_ref[...]
        e = x_ref[...] + gv * tv - y_ref[...]
        dx = e * (1.0 / D)
        dx_ref[...] = dx
        dt_ref[...] = (dx * gv).astype(BF16)
        part = jnp.concatenate([jnp.sum(e * e, axis=0, keepdims=True), jnp.sum(dx * tv, axis=0, keepdims=True),
                                jnp.zeros((6, D), F32)], axis=0)

        @pl.when(i == 0)
        def _():
            sums_ref[...] = part

        @pl.when(i > 0)
        def _():
            sums_ref[...] += part

    row, vec = _row_spec(ts, D), _vec_spec(D)
    return pl.pallas_call(
        body, grid=(S // ts,), in_specs=[row, row, vec, row], out_specs=[row, row, pl.BlockSpec((8, D), lambda i: (0, 0))],
        out_shape=[jax.ShapeDtypeStruct((S, D), F32), jax.ShapeDtypeStruct((S, D), BF16), jax.ShapeDtypeStruct((8, D), F32)],
        compiler_params=_cp("arbitrary"), name="loss_resid")(x2, t2, g2, target)


def _adamw(w, g, m, v, name):
    shape = w.shape
    n = math.prod(shape)
    view = (n // LANE, LANE) if n % LANE == 0 else (math.prod(shape[:-1]), shape[-1])
    R, C = view
    tr = R
    for cand in (1024, 512, 256):
        if R > cand and R % cand == 0:
            tr = cand
            break

    def body(w_ref, g_ref, m_ref, v_ref, d_ref, nm_ref, nv_ref):
        gv = g_ref[...]
        nm = ADAM_B1 * m_ref[...] + (1.0 - ADAM_B1) * gv
        nv = ADAM_B2 * v_ref[...] + (1.0 - ADAM_B2) * (gv * gv)
        m_hat = nm / (1.0 - ADAM_B1 ** ADAM_STEP)
        v_hat = nv / (1.0 - ADAM_B2 ** ADAM_STEP)
        d_ref[...] = -ADAM_LR * (m_hat / (jnp.sqrt(v_hat) + ADAM_EPS) + ADAM_WD * w_ref[...])
        nm_ref[...] = nm
        nv_ref[...] = nv

    spec = pl.BlockSpec((tr, C), lambda i: (i, 0))
    outs = pl.pallas_call(
        body, grid=(R // tr,), in_specs=[spec] * 4, out_specs=[spec] * 3, out_shape=[jax.ShapeDtypeStruct(view, F32)] * 3,
        compiler_params=_cp("parallel"), name=name)(*[a.reshape(view) for a in (w, g, m, v)])
    return [o.reshape(shape) for o in outs]


def _pad_rows8(a):
    return jnp.concatenate([a, jnp.zeros((8 - a.shape[0], a.shape[1]), a.dtype)], axis=0)


def _local_step(x, target, mod, n1g, w_in_p, wg_p, bg, gng, qng, kng, w_out_s, n2g, w_up_s, conv_w, conv_b, w_down_s):
    S, D = x.shape
    F = w_down_s.shape[0] * N_DEV
    ts = min(512, S)
    sh1, sc1, g1, sh2, sc2, g2 = [mod[i:i + 1] for i in range(6)]
    conv_w8 = _pad_rows8(conv_w)
    qg_t, kg_t = jnp.tile(qng, (1, ATTN_HEADS)), jnp.tile(kng, (1, ATTN_HEADS))

    h1 = _rms_mod(x, n1g, sc1, sh1, ts, "rms_mod1")
    proj, (g_out, g_up) = _mm(h1, w_in_p, NN, 512, PROJ_W, 1024, F32, "mm_in", ride=([w_out_s, _dense(w_up_s)], [True, True]))
    w_out = g_out.reshape(-1, D)
    w_up = _cols_from_blocks(g_up.reshape(N_DEV, D, -1))
    la = _gate_fwd(proj, wg_p, bg, ts)
    o_gla, states = _gla_fwd(proj, la, 512)
    y_gla = _gla_out(o_gla, proj, gng, ts)
    qn, kn = _head_norm(proj, qg_t, kg_t, ts)
    branches = [_attn_fwd(qn, kn, proj, d) for d in DILATIONS]
    mixed, y_att, lse = _attn_merge(y_gla, [b[0] for b in branches], [b[1] for b in branches], ts)
    t1 = _mm(mixed, w_out, NN, 512, 1024, 1024, F32, "mm_out")
    x2, h2 = _resid_rms_mod(x, t1, g1, n2g, sc2, sh2, ts, "resid_rms_mod2")
    u0, (g_down,) = _mm(h2, w_up, NN, 512, 2816, 1024, F32, "mm_up", ride=([w_down_s], [True]))
    w_down = g_down.reshape(F, D)
    tc = 1408 if F % 1408 == 0 else F
    a = _conv_swiglu(u0, conv_w8, conv_b, min(256, S), tc)
    t2 = _mm(a, w_down, NN, 512, 1024, F, F32, "mm_down")
    dx3, dt2, sums3 = _loss_resid(x2, t2, g2, target, ts)
    loss_row, dg2 = sums3[0:1], sums3[1:2]

    g_w_down = _mm(a, dt2, TN, 1408, 1024, 512, F32, "mm_gw_down")
    da = _mm(dt2, w_down, NT, 512, 2816, 1024, F32, "mm_da")
    du, sums_g, sums_v, (r_down,) = _ffn_du(da, u0, conv_w8, conv_b, min(256, S), tc,
                                            ride=([g_w_down.reshape(N_DEV, -1, D)], [False]))
    g_conv_w = jnp.concatenate([sums_g[0:3], sums_v[0:3]], axis=1)
    g_conv_b = jnp.concatenate([sums_g[3:4], sums_v[3:4]], axis=1)
    du0 = _ffn_du0(du, conv_w8, min(256, S), tc)
    g_w_up = _mm(h2, du0, TN, 512, 2816, 512, F32, "mm_gw_up")
    dh2 = _mm(du0, w_up, NT, 512, 1024, 2816, F32, "mm_dh2")
    dx2, sums2, dt1 = _rms_mod_bwd(dh2, x2, dx3, n2g, sc2, ts, "rms_mod_bwd2", t_prev=t1, g_prev=g1)
    dsh2, dsc2, g_n2g, dg1 = sums2[0:1], sums2[1:2], sums2[2:3], sums2[3:4]
    g_w_out = _mm(mixed, dt1, TN, 1024, 1024, 512, F32, "mm_gw_out")
    dmixed = _mm(dt1, w_out, NT, 512, 1024, 1024, F32, "mm_dmixed")
    do_gla, dgr, gng_sums = _gla_out_bwd(dmixed, o_gla, proj, gng, ts)
    dgq, dgk, dgv, dla, (r_up, r_out) = _gla_bwd(
        proj, la, do_gla, states, 512, ride=([_dense(_col_blocks(g_w_up)), g_w_out.reshape(N_DEV, -1, D)], [False, False]))
    dglr, g_wg_p, gb_sums = _gate_bwd(dla, la, proj, wg_p, ts)
    delta = _attn_delta(dmixed, y_att, ts)
    dqs = [_attn_dq(qn, kn, proj, dmixed, lse, delta, d) for d in DILATIONS]
    dkvs = [_attn_dkv(qn, kn, proj, dmixed, lse, delta, d) for d in DILATIONS]
    daq, dak, dav, qk_sums = _attn_post(dqs, [t[0] for t in dkvs], [t[1] for t in dkvs], proj, qg_t, kg_t, ts)
    dproj = jnp.concatenate([dgq, dgk, dgv, dgr, daq, dak, dav, dglr, jnp.zeros((S, PROJ_W - O_GLR - LANE), BF16)], axis=1)
    g_w_in_p = _mm(h1, dproj, TN, 512, PROJ_W, 512, F32, "mm_gw_in")
    g_w_in = jnp.concatenate([g_w_in_p[:, :GLR_SRC], g_w_in_p[:, O_GLR:O_GLR + GLA_RANK], g_w_in_p[:, GLR_SRC:O_GLR]], axis=1)
    dh1, (r_in,) = _mm(dproj, w_in_p, NT, 512, 1024, PROJ_W, F32, "mm_dh1", ride=([_dense(_col_blocks(g_w_in))], [False]))
    dx, sums1 = _rms_mod_bwd(dh1, x, dx2, n1g, sc1, ts, "rms_mod_bwd1")
    dsh1, dsc1, g_n1g = sums1[0:1], sums1[1:2], sums1[2:3]

    dmod = jnp.concatenate([dsh1, dsc1, dg1, dsh2, dsc2, dg2], axis=1)
    grads = dict(n1g=g_n1g, w_in=r_in, wg=g_wg_p[:GLA_RANK], bg=gb_sums[0:1], gng=gng_sums[0:1],
                 qng_lanes=qk_sums[0:1], kng_lanes=qk_sums[1:2], w_out=r_out, n2g=g_n2g, w_up=r_up,
                 conv_w=g_conv_w, conv_b=g_conv_b, w_down=r_down)
    return loss_row, dx, dmod, grads


def _dense(a):
    *lead, R, C = a.shape
    return a.reshape(*lead, R * C // LANE, LANE)


def _col_blocks(a):
    R, W = a.shape
    return a.reshape(R, N_DEV, W // N_DEV).transpose(1, 0, 2)


def _cols_from_blocks(a):
    n, R, C = a.shape
    return a.transpose(1, 0, 2).reshape(R, n * C)


def kernel(x, c, w_ada, b_ada, norm1_g, w_in, gla_w_gate, gla_b_gate, gla_norm_g, q_norm_g, k_norm_g, w_out, norm2_g, w_up, conv_w, conv_b, w_down, loss_target, m_w_ada, m_b_ada, m_norm1_g, m_w_in, m_gla_w_gate, m_gla_b_gate, m_gla_norm_g, m_q_norm_g, m_k_norm_g, m_w_out, m_norm2_g, m_w_up, m_conv_w, m_conv_b, m_w_down, v_w_ada, v_b_ada, v_norm1_g, v_w_in, v_gla_w_gate, v_gla_b_gate, v_gla_norm_g, v_q_norm_g, v_k_norm_g, v_w_out, v_norm2_g, v_w_up, v_conv_w, v_conv_b, v_w_down):
    axes = ("x", "y", "c")
    me = 4 * lax.axis_index("x") + 2 * lax.axis_index("y") + lax.axis_index("c")
    S, D = x.shape[1], x.shape[2]
    x2d, tgt2d = x[0], loss_target[0]
    w_in_s, w_out_s, w_up_s, w_down_s, w_ada_s = w_in[0], w_out[0], w_up[0], w_down[0], w_ada[0]
    conv_w_s, wg_s = conv_w[0], gla_w_gate[0]
    in_c, up_c, ada_c, wg_c, cw_c = w_in_s.shape[1], w_up_s.shape[1], w_ada_s.shape[1], wg_s.shape[1], conv_w_s.shape[1]
    F = w_down_s.shape[0] * N_DEV

    small = jnp.concatenate([conv_w_s.reshape(1, -1), wg_s.reshape(1, -1)], axis=1)
    n_small = small.shape[1]
    small = jnp.pad(small, ((0, 0), (0, -n_small % LANE)))
    g_c, g_in, g_small = _exchange([c, _dense(w_in_s.astype(BF16)), small], [True] * 3, "gather_w_in")
    c_all = g_c.reshape(N_DEV, D)
    w_in_full = _cols_from_blocks(g_in.reshape(N_DEV, D, in_c))
    w_in_p = jnp.concatenate([w_in_full[:, :GLR_SRC], w_in_full[:, GLR_SRC + GLA_RANK:],
                              w_in_full[:, GLR_SRC:GLR_SRC + GLA_RANK], jnp.zeros((D, PROJ_W - O_GLR - GLA_RANK), BF16)], axis=1)
    g_small = g_small.reshape(N_DEV, -1)
    conv_w_full = _cols_from_blocks(g_small[:, :3 * cw_c].reshape(N_DEV, 3, cw_c))
    wg_full = _cols_from_blocks(g_small[:, 3 * cw_c:n_small].reshape(N_DEV, GLA_RANK, wg_c))
    wg_p = jnp.concatenate([wg_full, jnp.zeros((LANE - GLA_RANK, wg_full.shape[1]), F32)], axis=0)

    b_shard = lax.dynamic_slice(b_ada, (0, me * ada_c), (1, ada_c))
    mod_part = _ada_fwd(c_all, w_ada_s, b_shard)
    mod_recv, = _exchange([mod_part.reshape(N_DEV, 1, ada_c)], [False], "exchange_mod")
    mod = mod_recv.reshape(6, D)

    loss_row, dx, dmod, gr = _local_step(
        x2d, tgt2d, mod, norm1_g, w_in_p, wg_p, gla_b_gate, gla_norm_g, q_norm_g, k_norm_g,
        w_out_s.astype(BF16), norm2_g, w_up_s.astype(BF16), conv_w_full, conv_b, w_down_s.astype(BF16))
    loss = lax.psum(0.5 / D * jnp.sum(loss_row), axes)

    parts = [dmod, gr["n1g"], gr["bg"], gr["gng"], gr["qng_lanes"], gr["kng_lanes"], gr["n2g"], gr["conv_b"],
             gr["wg"].reshape(1, -1), gr["conv_w"].reshape(1, -1)]
    sizes = [p.shape[1] for p in parts]
    packed = jnp.concatenate(parts, axis=1)
    packed = jnp.pad(packed, ((0, 0), (0, -packed.shape[1] % (8 * LANE))))
    gathered, = _exchange([packed.reshape(8, -1)], [True], "gather_small_grads")
    gathered = gathered.reshape(N_DEV, -1)
    total = _sum_slots(gathered.reshape(N_DEV, 8, -1), "sum_small_grads").reshape(1, -1)
    offs = [0]
    for s_ in sizes:
        offs.append(offs[-1] + s_)
    t_dmod, t_n1g, t_bg, t_gng, t_qng, t_kng, t_n2g, t_conv_b, t_wg, t_conv_w = [
        total[:, offs[i]:offs[i + 1]] for i in range(len(sizes))]
    g_b_ada = t_dmod
    g_qng = t_qng.reshape(ATTN_HEADS, ATTN_HD).sum(axis=0, keepdims=True)
    g_kng = t_kng.reshape(ATTN_HEADS, ATTN_HD).sum(axis=0, keepdims=True)
    g_wg = lax.dynamic_slice(t_wg.reshape(GLA_RANK, -1), (0, me * wg_c), (GLA_RANK, wg_c))
    g_conv_w = lax.dynamic_slice(t_conv_w.reshape(3, -1), (0, me * cw_c), (3, cw_c))
    dmod_shard = lax.dynamic_slice(gathered[:, :6 * D], (0, me * ada_c), (N_DEV, ada_c))
    g_w_ada = _ada_bwd(c_all, dmod_shard)

    g_w_in = _sum_slots(gr["w_in"], "sum_gw_in").reshape(D, in_c)
    g_w_out = _sum_slots(gr["w_out"], "sum_gw_out")
    g_w_up = _sum_slots(gr["w_up"], "sum_gw_up").reshape(D, up_c)
    g_w_down = _sum_slots(gr["w_down"], "sum_gw_down")

    names = ["w_ada", "b_ada", "norm1_g", "w_in", "gla_w_gate", "gla_b_gate", "gla_norm_g", "q_norm_g", "k_norm_g",
             "w_out", "norm2_g", "w_up", "conv_w", "conv_b", "w_down"]
    ws = [w_ada, b_ada, norm1_g, w_in, gla_w_gate, gla_b_gate, gla_norm_g, q_norm_g, k_norm_g, w_out, norm2_g, w_up, conv_w, conv_b, w_down]
    ms = [m_w_ada, m_b_ada, m_norm1_g, m_w_in, m_gla_w_gate, m_gla_b_gate, m_gla_norm_g, m_q_norm_g, m_k_norm_g, m_w_out, m_norm2_g, m_w_up, m_conv_w, m_conv_b, m_w_down]
    vs = [v_w_ada, v_b_ada, v_norm1_g, v_w_in, v_gla_w_gate, v_gla_b_gate, v_gla_norm_g, v_q_norm_g, v_k_norm_g, v_w_out, v_norm2_g, v_w_up, v_conv_w, v_conv_b, v_w_down]
    gs = [g_w_ada, g_b_ada, t_n1g, g_w_in, g_wg, t_bg, t_gng, g_qng, g_kng, g_w_out, t_n2g, g_w_up, g_conv_w, t_conv_b, g_w_down]
    gs = [g.reshape(w.shape) for g, w in zip(gs, ws)]
    deltas, new_ms, new_vs = [], [], []
    for nm, w, g, m, v in zip(names, ws, gs, ms, vs):
        d_, m_, v_ = _adamw(w, g, m, v, "adamw_" + nm)
        deltas.append(d_)
        new_ms.append(m_)
        new_vs.append(v_)
    return (loss, dx.reshape(x.shape), *gs, *deltas, *new_ms, *new_vs)
```

```python
import functools
import math

import jax
import jax.numpy as jnp
from jax import lax
from jax.experimental import pallas as pl
from jax.experimental.pallas import tpu as pltpu

F32, BF16 = jnp.float32, jnp.bfloat16
HI = lax.Precision.HIGHEST
EPS = 1e-6
NEG = -1e30

N_DEV = 8
GLA_HEADS, GLA_DK, GLA_DV, GLA_RANK, GLA_TAU, GLA_CHUNK = 4, 64, 128, 16, 16.0, 64
ATTN_HEADS, ATTN_HD, ATTN_BLOCK = 8, 64, 128
DILATIONS = (1, 4, 16)
GLA_QK, GLA_V, ATTN_DIM = GLA_HEADS * GLA_DK, GLA_HEADS * GLA_DV, ATTN_HEADS * ATTN_HD
O_GQ, O_GK, O_GV, O_GR, O_AQ, O_AK, O_AV, O_GLR = 0, 256, 512, 1024, 1536, 2048, 2560, 3072
PROJ_W = 3328
LANE = 128
GLR_SRC = 2 * GLA_QK + 2 * GLA_V

ADAM_LR, ADAM_B1, ADAM_B2, ADAM_EPS, ADAM_WD, ADAM_STEP = 0.001, 0.9, 0.999, 1e-08, 0.01, 10

VMEM_LIMIT = 56 * 1024 * 1024
SUM_BLOCK_ELEMS = 256 * 1024


def _cp(*sem):
    return pltpu.CompilerParams(dimension_semantics=sem, vmem_limit_bytes=VMEM_LIMIT)


def _dot(a, b, dims, precision=None):
    return lax.dot_general(a, b, (dims, ((), ())), preferred_element_type=F32, precision=precision)


NN, NT, TN = ((1,), (0,)), ((1,), (1,)), ((0,), (0,))


def _sigmoid(z):
    return 1.0 / (1.0 + jnp.exp(-z))


ANY_SPEC = pl.BlockSpec(memory_space=pl.ANY)


def _exchange_shapes(arrays, gather):
    return [jax.ShapeDtypeStruct((N_DEV,) + (a.shape if g else a.shape[1:]), a.dtype) for a, g in zip(arrays, gather)]


def _exchange_sems(n):
    return [pltpu.SemaphoreType.DMA((n * (N_DEV - 1),)), pltpu.SemaphoreType.DMA((n * (N_DEV - 1),)), pltpu.SemaphoreType.DMA((n,))]


def _exchange_copies(ins, outs, gather, send_sems, recv_sems, local_sems):
    x, y, c = lax.axis_index("x"), lax.axis_index("y"), lax.axis_index("c")
    me = 4 * x + 2 * y + c
    copies = []
    for a in range(len(ins)):
        for p in range(1, N_DEV):
            px, py, pc = x ^ (p >> 2), y ^ ((p >> 1) & 1), c ^ (p & 1)
            peer = 4 * px + 2 * py + pc
            k = a * (N_DEV - 1) + p - 1
            copies.append(pltpu.make_async_remote_copy(
                src_ref=ins[a] if gather[a] else ins[a].at[peer], dst_ref=outs[a].at[me],
                send_sem=send_sems.at[k], recv_sem=recv_sems.at[k],
                device_id=(px, py, pc), device_id_type=pl.DeviceIdType.MESH))
        copies.append(pltpu.make_async_copy(ins[a] if gather[a] else ins[a].at[me], outs[a].at[me], local_sems.at[a]))
    return copies


def _riding(body, n_in, n_out, gather, grid):
    nr = len(gather)
    if not nr:
        return body

    def wrapped(*refs):
        ins, r_ins = refs[:n_in], refs[n_in:n_in + nr]
        outs, r_outs = refs[n_in + nr:n_in + nr + n_out], refs[n_in + nr + n_out:n_in + 2 * nr + n_out]
        scratch = refs[n_in + 2 * nr + n_out:]
        first = last = None
        for t, steps in enumerate(grid):
            pid = pl.program_id(t)
            first = (pid == 0) if first is None else first & (pid == 0)
            last = (pid == steps - 1) if last is None else last & (pid == steps - 1)
        copies = _exchange_copies(r_ins, r_outs, gather, *scratch[-3:])

        @pl.when(first)
        def _():
            for cp in copies:
                cp.start()

        body(*ins, *outs, *scratch[:-3])

        @pl.when(last)
        def _():
            for cp in copies:
                cp.wait()

    return wrapped


def _exchange(arrays, gather, name):
    n = len(arrays)

    def body(*refs):
        copies = _exchange_copies(refs[:n], refs[n:2 * n], gather, *refs[2 * n:])
        for cp in copies:
            cp.start()
        for cp in copies:
            cp.wait()

    return pl.pallas_call(
        body, out_shape=_exchange_shapes(arrays, gather), in_specs=[ANY_SPEC] * n, out_specs=[ANY_SPEC] * n,
        scratch_shapes=_exchange_sems(n), name=name)(*arrays)


def _sum_slots(x, name):
    _, R, C = x.shape
    tr = max(t for t in range(8, min(SUM_BLOCK_ELEMS // C, R) + 1, 8) if R % t == 0)

    def body(x_ref, o_ref):
        acc = x_ref[0]
        for s in range(1, N_DEV):
            acc = acc + x_ref[s]
        o_ref[...] = acc

    return pl.pallas_call(
        body, grid=(R // tr,), in_specs=[pl.BlockSpec((N_DEV, tr, C), lambda i: (0, i, 0))],
        out_specs=pl.BlockSpec((tr, C), lambda i: (i, 0)), out_shape=jax.ShapeDtypeStruct((R, C), x.dtype),
        compiler_params=_cp("parallel"), name=name)(x)


def _mm(a, b, mode, tm, tn, tk, out_dtype, name, ride=None):
    if mode == NN:
        (M, K), N = a.shape, b.shape[1]
    elif mode == NT:
        (M, K), N = a.shape, b.shape[0]
    else:
        (K, M), N = a.shape, b.shape[1]
    tm, tn, tk = min(tm, M), min(tn, N), min(tk, K)
    assert M % tm == 0 and N % tn == 0 and K % tk == 0, (name, M, N, K, tm, tn, tk)
    nk = K // tk
    if mode == NN:
        a_spec = pl.BlockSpec((tm, tk), lambda i, j, k: (i, k))
        b_spec = pl.BlockSpec((tk, tn), lambda i, j, k: (k, j))
    elif mode == NT:
        a_spec = pl.BlockSpec((tm, tk), lambda i, j, k: (i, k))
        b_spec = pl.BlockSpec((tn, tk), lambda i, j, k: (j, k))
    else:
        a_spec = pl.BlockSpec((tk, tm), lambda i, j, k: (k, i))
        b_spec = pl.BlockSpec((tk, tn), lambda i, j, k: (k, j))

    ride_arrays, ride_gather = ride if ride else ([], [])
    nr = len(ride_arrays)
    grid = (M // tm, N // tn, nk)

    own_acc = nk > 1 and out_dtype != F32

    def body(a_ref, b_ref, o_ref, *acc):
        p = _dot(a_ref[...].astype(BF16), b_ref[...].astype(BF16), mode)
        if nk == 1:
            o_ref[...] = p.astype(out_dtype)
        else:
            acc_ref = acc[0] if own_acc else o_ref
            k = pl.program_id(2)

            @pl.when(k == 0)
            def _():
                acc_ref[...] = p

            @pl.when(k > 0)
            def _():
                acc_ref[...] += p

            if own_acc:
                @pl.when(k == nk - 1)
                def _():
                    o_ref[...] = acc_ref[...].astype(out_dtype)

    outs = pl.pallas_call(
        _riding(body, 2, 1, ride_gather, grid), grid=grid, in_specs=[a_spec, b_spec] + [ANY_SPEC] * nr,
        out_specs=[pl.BlockSpec((tm, tn), lambda i, j, k: (i, j))] + [ANY_SPEC] * nr,
        out_shape=[jax.ShapeDtypeStruct((M, N), out_dtype)] + _exchange_shapes(ride_arrays, ride_gather),
        scratch_shapes=([pltpu.VMEM((tm, tn), F32)] if own_acc else []) + (_exchange_sems(nr) if nr else []),
        compiler_params=_cp(*(("arbitrary",) * 3 if nr else ("parallel", "parallel", "arbitrary"))), name=name)(a, b, *ride_arrays)
    return (outs[0], outs[1:]) if nr else outs[0]


def _ada_fwd(c_all, w_shard, b_shard):
    Nc = w_shard.shape[1]

    def body(c_ref, w_ref, b_ref, o_ref):
        cv = c_ref[...]
        o_ref[...] = _dot(cv * _sigmoid(cv), w_ref[...], NN, HI) + b_ref[...]

    return pl.pallas_call(body, out_shape=jax.ShapeDtypeStruct((N_DEV, Nc), F32), name="ada_fwd",
                          compiler_params=pltpu.CompilerParams(vmem_limit_bytes=VMEM_LIMIT))(c_all, w_shard, b_shard)


def _ada_bwd(c_all, dmod_shard):
    D, Nc = c_all.shape[1], dmod_shard.shape[1]

    def body(c_ref, d_ref, o_ref):
        cv = c_ref[...]
        o_ref[...] = _dot(cv * _sigmoid(cv), d_ref[...], TN, HI)

    return pl.pallas_call(body, out_shape=jax.ShapeDtypeStruct((D, Nc), F32), name="ada_bwd",
                          compiler_params=pltpu.CompilerParams(vmem_limit_bytes=VMEM_LIMIT))(c_all, dmod_shard)


def _row_spec(ts, D):
    return pl.BlockSpec((ts, D), lambda i: (i, 0))


def _vec_spec(D):
    return pl.BlockSpec((1, D), lambda i: (0, 0))


def _col_spec(D, ts):
    return pl.BlockSpec((D, ts), lambda i: (0, i))


def _rms_mod(x, ng, sc, sh, ts, name):
    S, D = x.shape

    def body(x_ref, ng_ref, sc_ref, sh_ref, h_ref, ht_ref):
        xv = x_ref[...]
        r = lax.rsqrt(jnp.mean(xv * xv, axis=-1, keepdims=True) + EPS)
        h = xv * r * ng_ref[...] * (1.0 + sc_ref[...]) + sh_ref[...]
        h_ref[...] = h.astype(BF16)
        ht_ref[...] = h.T.astype(BF16)

    return pl.pallas_call(
        body, grid=(S // ts,), in_specs=[_row_spec(ts, D)] + [_vec_spec(D)] * 3, out_specs=[_row_spec(ts, D), _col_spec(D, ts)],
        out_shape=[jax.ShapeDtypeStruct((S, D), BF16), jax.ShapeDtypeStruct((D, S), BF16)],
        compiler_params=_cp("parallel"), name=name)(x, ng, sc, sh)


def _resid_rms_mod(x, t, g, ng, sc, sh, ts, name):
    S, D = x.shape

    def body(x_ref, t_ref, g_ref, ng_ref, sc_ref, sh_ref, x2_ref, h_ref, ht_ref):
        xv = x_ref[...] + g_ref[...] * t_ref[...]
        x2_ref[...] = xv
        r = lax.rsqrt(jnp.mean(xv * xv, axis=-1, keepdims=True) + EPS)
        h = xv * r * ng_ref[...] * (1.0 + sc_ref[...]) + sh_ref[...]
        h_ref[...] = h.astype(BF16)
        ht_ref[...] = h.T.astype(BF16)

    return pl.pallas_call(
        body, grid=(S // ts,), in_specs=[_row_spec(ts, D)] * 2 + [_vec_spec(D)] * 4,
        out_specs=[_row_spec(ts, D)] * 2 + [_col_spec(D, ts)],
        out_shape=[jax.ShapeDtypeStruct((S, D), F32), jax.ShapeDtypeStruct((S, D), BF16), jax.ShapeDtypeStruct((D, S), BF16)],
        compiler_params=_cp("parallel"), name=name)(x, t, g, ng, sc, sh)


def _rms_mod_bwd(dh, xin, dres, ng, sc, ts, name, t_prev=None, g_prev=None):
    S, D = xin.shape
    chain = t_prev is not None

    def body(*refs):
        if chain:
            dh_ref, x_ref, dr_ref, ng_ref, sc_ref, t_ref, g_ref, dx_ref, sums_ref, dt_ref = refs
        else:
            dh_ref, x_ref, dr_ref, ng_ref, sc_ref, dx_ref, sums_ref = refs
        i = pl.program_id(0)
        xv, dhv = x_ref[...], dh_ref[...]
        r = lax.rsqrt(jnp.mean(xv * xv, axis=-1, keepdims=True) + EPS)
        xh = xv * r
        ngv, scv = ng_ref[...], sc_ref[...]
        dxh = dhv * (ngv * (1.0 + scv))
        dx = dr_ref[...] + r * (dxh - xh * jnp.mean(dxh * xh, axis=-1, keepdims=True))
        dx_ref[...] = dx
        dhx = dhv * xh
        rows = [jnp.sum(dhv, axis=0, keepdims=True), jnp.sum(dhx * ngv, axis=0, keepdims=True),
                jnp.sum(dhx * (1.0 + scv), axis=0, keepdims=True)]
        if chain:
            dt_ref[...] = (dx * g_ref[...]).astype(BF16)
            rows.append(jnp.sum(dx * t_ref[...], axis=0, keepdims=True))
        rows.append(jnp.zeros((8 - len(rows), D), F32))
        part = jnp.concatenate(rows, axis=0)

        @pl.when(i == 0)
        def _():
            sums_ref[...] = part

        @pl.when(i > 0)
        def _():
            sums_ref[...] += part

    row, vec = _row_spec(ts, D), _vec_spec(D)
    sums_spec = pl.BlockSpec((8, D), lambda i: (0, 0))
    ins = [dh, xin, dres, ng, sc] + ([t_prev, g_prev] if chain else [])
    in_specs = [row, row, row, vec, vec] + ([row, vec] if chain else [])
    out_specs = [row, sums_spec] + ([row] if chain else [])
    out_shape = [jax.ShapeDtypeStruct((S, D), F32), jax.ShapeDtypeStruct((8, D), F32)] + (
        [jax.ShapeDtypeStruct((S, D), BF16)] if chain else [])
    return pl.pallas_call(body, grid=(S // ts,), in_specs=in_specs, out_specs=out_specs, out_shape=out_shape,
                          compiler_params=_cp("arbitrary"), name=name)(*ins)


def _gate_fwd(proj, wg_p, bg, ts):
    S = proj.shape[0]

    def body(glr_ref, w_ref, b_ref, la_ref):
        z = _dot(glr_ref[...], w_ref[...], NN, HI) + b_ref[...]
        la_ref[...] = (jnp.minimum(z, 0.0) - jnp.log(1.0 + jnp.exp(-jnp.abs(z)))) * (1.0 / GLA_TAU)

    return pl.pallas_call(
        body, grid=(S // ts,),
        in_specs=[pl.BlockSpec((ts, LANE), lambda i: (i, O_GLR // LANE)), pl.BlockSpec((LANE, GLA_QK), lambda i: (0, 0)),
                  pl.BlockSpec((1, GLA_QK), lambda i: (0, 0))],
        out_specs=pl.BlockSpec((ts, GLA_QK), lambda i: (i, 0)), out_shape=jax.ShapeDtypeStruct((S, GLA_QK), F32),
        compiler_params=_cp("parallel"), name="gla_gate_fwd")(proj, wg_p, bg)


def _gate_bwd(dla, la, proj, wg_p, ts):
    S = proj.shape[0]

    def body(dla_ref, la_ref, glr_ref, w_ref, dglr_ref, gw_ref, gb_ref):
        i = pl.program_id(0)
        dz = dla_ref[...] * (1.0 / GLA_TAU) * (1.0 - jnp.exp(GLA_TAU * la_ref[...]))
        dglr_ref[...] = _dot(dz, w_ref[...], NT, HI).astype(BF16)
        gw = _dot(glr_ref[...], dz, TN, HI)
        gb = jnp.concatenate([jnp.sum(dz, axis=0, keepdims=True), jnp.zeros((7, GLA_QK), F32)], axis=0)

        @pl.when(i == 0)
        def _():
            gw_ref[...] = gw
            gb_ref[...] = gb

        @pl.when(i > 0)
        def _():
            gw_ref[...] += gw
            gb_ref[...] += gb

    return pl.pallas_call(
        body, grid=(S // ts,),
        in_specs=[pl.BlockSpec((ts, GLA_QK), lambda i: (i, 0)), pl.BlockSpec((ts, GLA_QK), lambda i: (i, 0)),
                  pl.BlockSpec((ts, LANE), lambda i: (i, O_GLR // LANE)), pl.BlockSpec((LANE, GLA_QK), lambda i: (0, 0))],
        out_specs=[pl.BlockSpec((ts, LANE), lambda i: (i, 0)), pl.BlockSpec((LANE, GLA_QK), lambda i: (0, 0)),
                   pl.BlockSpec((8, GLA_QK), lambda i: (0, 0))],
        out_shape=[jax.ShapeDtypeStruct((S, LANE), BF16), jax.ShapeDtypeStruct((LANE, GLA_QK), F32),
                   jax.ShapeDtypeStruct((8, GLA_QK), F32)],
        compiler_params=_cp("arbitrary"), name="gla_gate_bwd")(dla, la, proj, wg_p)


def _tri(lower):
    r = lax.broadcasted_iota(jnp.int32, (GLA_CHUNK, GLA_CHUNK), 0)
    c = lax.broadcasted_iota(jnp.int32, (GLA_CHUNK, GLA_CHUNK), 1)
    return jnp.where((r >= c) if lower else (c >= r), 1.0, 0.0).astype(F32)


GLA_SUB = 16
GLA_NSUB = GLA_CHUNK // GLA_SUB
PAIR_QK = 2 * GLA_DK
PAIR_V = 2 * GLA_DV


def _band_selector():
    r = lax.broadcasted_iota(jnp.int32, (GLA_SUB * PAIR_QK, LANE), 0)
    c = lax.broadcasted_iota(jnp.int32, (GLA_SUB * PAIR_QK, LANE), 1)
    dist, head = r // PAIR_QK, (r % PAIR_QK) // GLA_DK
    return jnp.where(c == head * GLA_DK + (GLA_SUB - 1 - dist), 1.0, 0.0).astype(BF16)


def _flip_matrix():
    r = lax.broadcasted_iota(jnp.int32, (GLA_CHUNK, GLA_CHUNK), 0)
    c = lax.broadcasted_iota(jnp.int32, (GLA_CHUNK, GLA_CHUNK), 1)
    return jnp.where(r + c == GLA_CHUNK - 1, 1.0, 0.0).astype(BF16)


def _state_mask():
    r = lax.broadcasted_iota(jnp.int32, (PAIR_V, PAIR_QK), 0)
    c = lax.broadcasted_iota(jnp.int32, (PAIR_V, PAIR_QK), 1)
    return (r < GLA_DV) == (c < GLA_DK)


class _GlaChunk:
    def __init__(self, qs, kc, vc, g, sel):
        C = GLA_CHUNK
        self.qs, self.kc, self.vc = qs, kc, vc
        rows = lax.broadcasted_iota(jnp.int32, (C, 1), 0)
        lane = lax.broadcasted_iota(jnp.int32, (1, PAIR_QK), 1)
        self.rows, self.lane = rows, lane
        b = _dot(_tri(True), g, NN, HI)
        self.bl = b[C - 1:C, :]
        self.eb = jnp.exp(b)
        self.kdec = jnp.exp(self.bl - b)
        edge = lambda J: b[GLA_SUB * (J + 1):GLA_SUB * (J + 1) + 1, :]
        self.e_far = [jnp.exp(jnp.where(rows >= GLA_SUB * (J + 1), b - edge(J), NEG)) for J in range(GLA_NSUB - 1)]
        blk = rows // GLA_SUB
        bnext = edge(0)
        for J in range(1, GLA_NSUB - 1):
            bnext = jnp.where(blk == J, edge(J), bnext)
        self.e_khat = jnp.exp(jnp.where(blk < GLA_NSUB - 1, bnext - b, NEG))
        khat = kc * self.e_khat
        k2 = jnp.concatenate([jnp.where(lane < GLA_DK, khat, 0.0), jnp.where(lane >= GLA_DK, khat, 0.0)], axis=0)
        self.blk2 = jnp.concatenate([blk, blk], axis=0)
        self.m_far = jnp.concatenate([jnp.where(self.blk2 == J, k2, 0.0) for J in range(GLA_NSUB - 1)], axis=1).astype(BF16)
        self.qcat = jnp.concatenate([qs * e for e in self.e_far], axis=1).astype(BF16)
        a_far = _dot(self.qcat, self.m_far, NT)
        self.e_band, self.rk, terms = [], [], []
        for d in range(GLA_SUB):
            rk = pltpu.roll(kc, d, 0) if d else kc
            rb = pltpu.roll(b, d, 0) if d else b
            e = jnp.exp(jnp.where(rows >= d, b - rb, NEG))
            self.e_band.append(e)
            self.rk.append(rk)
            terms.append((qs * rk * e).astype(BF16))
        band = _dot(jnp.concatenate(terms, axis=1), sel, NN)
        a_band = pltpu.roll(band, LANE - (GLA_SUB - 1), 1, stride=1, stride_axis=0)
        dist = rows - lane % GLA_DK
        self.far_mask = dist >= GLA_SUB
        self.band_mask = (dist >= 0) & (dist < GLA_SUB)
        self.a = (a_band + jnp.where(self.far_mask, a_far, 0.0)).astype(BF16)
        self.lane_v = lax.broadcasted_iota(jnp.int32, (1, PAIR_V), 1)
        self.v2 = jnp.concatenate([jnp.where(self.lane_v < GLA_DV, vc, 0.0), jnp.where(self.lane_v >= GLA_DV, vc, 0.0)],
                                  axis=0).astype(BF16)


def _gla_fwd(proj, la, tb, ride=None):
    S = proj.shape[0]
    C = GLA_CHUNK
    tb = min(tb, S)
    nbc = tb // C
    npair = GLA_HEADS // 2
    scale = GLA_DK ** -0.5

    def body(q_ref, k_ref, v_ref, la_ref, sel_ref, o_ref, st_ref, state):
        @pl.when(pl.program_id(1) == 0)
        def _():
            state[...] = jnp.zeros_like(state)

        def chunk(ci, carry):
            sl = pl.ds(pl.multiple_of(ci * C, C), C)
            ch = _GlaChunk(q_ref[sl, :] * scale, k_ref[sl, :], v_ref[sl, :], la_ref[sl, :], sel_ref[...])
            st = state[...]
            st_ref[0, ci] = st
            o_ref[sl, :] = _dot((ch.qs * ch.eb).astype(BF16), st.astype(BF16), NT) + _dot(ch.a, ch.v2, NN)
            upd = _dot(ch.vc.astype(BF16), (ch.kc * ch.kdec).astype(BF16), TN)
            state[...] = st * jnp.exp(ch.bl) + jnp.where(_state_mask(), upd, 0.0)
            return carry

        lax.fori_loop(0, nbc, chunk, 0, unroll=8)

    qspec = lambda off: pl.BlockSpec((tb, PAIR_QK), lambda p, i: (i, off // PAIR_QK + p))
    ride_arrays, ride_gather = ride if ride else ([], [])
    nr = len(ride_arrays)
    grid = (npair, S // tb)
    outs = pl.pallas_call(
        _riding(body, 5, 2, ride_gather, grid), grid=grid,
        in_specs=[qspec(O_GQ), qspec(O_GK), pl.BlockSpec((tb, PAIR_V), lambda p, i: (i, O_GV // PAIR_V + p)),
                  pl.BlockSpec((tb, PAIR_QK), lambda p, i: (i, p)),
                  pl.BlockSpec((GLA_SUB * PAIR_QK, LANE), lambda p, i: (0, 0))] + [ANY_SPEC] * nr,
        out_specs=[pl.BlockSpec((tb, PAIR_V), lambda p, i: (i, p)),
                   pl.BlockSpec((1, nbc, PAIR_V, PAIR_QK), lambda p, i: (p, i, 0, 0))] + [ANY_SPEC] * nr,
        out_shape=[jax.ShapeDtypeStruct((S, GLA_V), F32), jax.ShapeDtypeStruct((npair, S // C, PAIR_V, PAIR_QK), F32)]
        + _exchange_shapes(ride_arrays, ride_gather),
        scratch_shapes=[pltpu.VMEM((PAIR_V, PAIR_QK), F32)] + (_exchange_sems(nr) if nr else []),
        compiler_params=_cp("arbitrary", "arbitrary"), name="gla_fwd")(proj, proj, proj, la, _band_selector(), *ride_arrays)
    return outs[0], outs[1], outs[2:]


def _gla_bwd(proj, la, do, states, tb, ride=None):
    S = proj.shape[0]
    C = GLA_CHUNK
    tb = min(tb, S)
    nbc = tb // C
    nblk = S // tb
    npair = GLA_HEADS // 2
    scale = GLA_DK ** -0.5

    def body(q_ref, k_ref, v_ref, la_ref, do_ref, st_ref, sel_ref, selt_ref, dq_ref, dk_ref, dv_ref, dla_ref, dstate):
        @pl.when(pl.program_id(1) == 0)
        def _():
            dstate[...] = jnp.zeros_like(dstate)

        def chunk(cc, carry):
            ci = nbc - 1 - cc
            sl = pl.ds(pl.multiple_of(ci * C, C), C)
            ch = _GlaChunk(q_ref[sl, :] * scale, k_ref[sl, :], v_ref[sl, :], la_ref[sl, :], sel_ref[...])
            qs, kc, rows = ch.qs, ch.kc, ch.rows
            doc_b = do_ref[sl, :].astype(BF16)
            st = st_ref[0, ci]
            dst = dstate[...]
            dst_b = dst.astype(BF16)
            ebl = jnp.exp(ch.bl)
            dq = _dot(doc_b, st.astype(BF16), NN) * ch.eb
            dk = _dot(ch.vc.astype(BF16), dst_b, NN) * ch.kdec
            dv = _dot((kc * ch.kdec).astype(BF16), dst_b, NT)
            dbl = jnp.sum(dst * st, axis=0, keepdims=True) * ebl + jnp.sum(kc * dk, axis=0, keepdims=True)
            da = _dot(doc_b, ch.v2, NT)
            dv2 = _dot(ch.a, doc_b, TN)
            dv = dv + jnp.where(ch.lane_v < GLA_DV, dv2[:C], dv2[C:])
            da_far = jnp.where(ch.far_mask, da, 0.0).astype(BF16)
            dqcat = _dot(da_far, ch.m_far, NN)
            dm = _dot(da_far, ch.qcat, TN)
            dk2 = jnp.zeros((2 * C, PAIR_QK), F32)
            for J in range(GLA_NSUB - 1):
                dq = dq + dqcat[:, J * PAIR_QK:(J + 1) * PAIR_QK] * ch.e_far[J]
                dk2 = dk2 + jnp.where(ch.blk2 == J, dm[:, J * PAIR_QK:(J + 1) * PAIR_QK], 0.0)
            dk = dk + jnp.where(ch.lane < GLA_DK, dk2[:C], dk2[C:]) * ch.e_khat
            flip = _flip_matrix()
            da_band = _dot(flip, jnp.where(ch.band_mask, da, 0.0).astype(BF16), NN)
            dband = pltpu.roll(da_band, LANE - (C - GLA_SUB), 1, stride=1, stride_axis=0)
            dband = _dot(flip, dband.astype(BF16), NN)
            dterms = _dot(dband.astype(BF16), selt_ref[...], NN)
            for d in range(GLA_SUB):
                dt = dterms[:, d * PAIR_QK:(d + 1) * PAIR_QK]
                dq = dq + dt * (ch.rk[d] * ch.e_band[d])
                dkr = dt * (qs * ch.e_band[d])
                dk = dk + (pltpu.roll(dkr, C - d, 0) if d else dkr)
            db = qs * dq - kc * dk
            db = jnp.where(rows == C - 1, db + dbl, db)
            dq_ref[sl, :] = (dq * scale).astype(BF16)
            dk_ref[sl, :] = dk.astype(BF16)
            dv_ref[sl, :] = dv.astype(BF16)
            dla_ref[sl, :] = _dot(_tri(False), db, NN, HI)
            upd = _dot(doc_b, (qs * ch.eb).astype(BF16), TN)
            dstate[...] = dst * ebl + jnp.where(_state_mask(), upd, 0.0)
            return carry

        lax.fori_loop(0, nbc, chunk, 0, unroll=8)

    rev = lambda i: nblk - 1 - i
    qspec = lambda off: pl.BlockSpec((tb, PAIR_QK), lambda p, i: (rev(i), off // PAIR_QK + p))
    pair_qk = pl.BlockSpec((tb, PAIR_QK), lambda p, i: (rev(i), p))
    pair_v = pl.BlockSpec((tb, PAIR_V), lambda p, i: (rev(i), p))
    sel = _band_selector()
    ride_arrays, ride_gather = ride if ride else ([], [])
    nr = len(ride_arrays)
    grid = (npair, nblk)
    outs = pl.pallas_call(
        _riding(body, 8, 4, ride_gather, grid), grid=grid,
        in_specs=[qspec(O_GQ), qspec(O_GK), pl.BlockSpec((tb, PAIR_V), lambda p, i: (rev(i), O_GV // PAIR_V + p)),
                  pair_qk, pair_v, pl.BlockSpec((1, nbc, PAIR_V, PAIR_QK), lambda p, i: (p, rev(i), 0, 0)),
                  pl.BlockSpec((GLA_SUB * PAIR_QK, LANE), lambda p, i: (0, 0)),
                  pl.BlockSpec((LANE, GLA_SUB * PAIR_QK), lambda p, i: (0, 0))] + [ANY_SPEC] * nr,
        out_specs=[pair_qk, pair_qk, pair_v, pair_qk] + [ANY_SPEC] * nr,
        out_shape=[jax.ShapeDtypeStruct((S, GLA_QK), BF16), jax.ShapeDtypeStruct((S, GLA_QK), BF16),
                   jax.ShapeDtypeStruct((S, GLA_V), BF16), jax.ShapeDtypeStruct((S, GLA_QK), F32)]
        + _exchange_shapes(ride_arrays, ride_gather),
        scratch_shapes=[pltpu.VMEM((PAIR_V, PAIR_QK), F32)] + (_exchange_sems(nr) if nr else []),
        compiler_params=_cp("arbitrary", "arbitrary"), name="gla_bwd")(proj, proj, proj, la, do, states, sel, sel.T, *ride_arrays)
    return outs[0], outs[1], outs[2], outs[3], outs[4:]


def _gla_out(o, proj, gng, ts):
    S = o.shape[0]

    def body(o_ref, gr_ref, g_ref, y_ref):
        for h in range(GLA_HEADS):
            cols = slice(h * GLA_DV, (h + 1) * GLA_DV)
            ov, grv = o_ref[:, cols], gr_ref[:, cols]
            r = lax.rsqrt(jnp.mean(ov * ov, axis=-1, keepdims=True) + EPS)
            y_ref[:, cols] = (ov * r * g_ref[...] * (grv * _sigmoid(grv))).astype(BF16)

    return pl.pallas_call(
        body, grid=(S // ts,),
        in_specs=[pl.BlockSpec((ts, GLA_V), lambda i: (i, 0)), pl.BlockSpec((ts, GLA_V), lambda i: (i, O_GR // GLA_V)),
                  pl.BlockSpec((1, GLA_DV), lambda i: (0, 0))],
        out_specs=pl.BlockSpec((ts, GLA_V), lambda i: (i, 0)), out_shape=jax.ShapeDtypeStruct((S, GLA_V), BF16),
        compiler_params=_cp("parallel"), name="gla_out_fwd")(o, proj, gng)


def _gla_out_bwd(dmixed, o, proj, gng, ts):
    S = o.shape[0]

    def body(dy_ref, o_ref, gr_ref, g_ref, do_ref, dgr_ref, gg_ref):
        i = pl.program_id(0)
        gsum = jnp.zeros((1, GLA_DV), F32)
        for h in range(GLA_HEADS):
            cols = slice(h * GLA_DV, (h + 1) * GLA_DV)
            ov, grv, dy = o_ref[:, cols], gr_ref[:, cols], dy_ref[:, cols]
            r = lax.rsqrt(jnp.mean(ov * ov, axis=-1, keepdims=True) + EPS)
            oh = ov * r
            sg = _sigmoid(grv)
            silu = grv * sg
            don = dy * silu
            dgr_ref[:, cols] = (dy * (oh * g_ref[...]) * (sg * (1.0 + grv * (1.0 - sg)))).astype(BF16)
            gsum = gsum + jnp.sum(don * oh, axis=0, keepdims=True)
            doh = don * g_ref[...]
            do_ref[:, cols] = r * (doh - oh * jnp.mean(doh * oh, axis=-1, keepdims=True))
        part = jnp.concatenate([gsum, jnp.zeros((7, GLA_DV), F32)], axis=0)

        @pl.when(i == 0)
        def _():
            gg_ref[...] = part

        @pl.when(i > 0)
        def _():
            gg_ref[...] += part

    return pl.pallas_call(
        body, grid=(S // ts,),
        in_specs=[pl.BlockSpec((ts, GLA_V), lambda i: (i, 0)), pl.BlockSpec((ts, GLA_V), lambda i: (i, 0)),
                  pl.BlockSpec((ts, GLA_V), lambda i: (i, O_GR // GLA_V)), pl.BlockSpec((1, GLA_DV), lambda i: (0, 0))],
        out_specs=[pl.BlockSpec((ts, GLA_V), lambda i: (i, 0)), pl.BlockSpec((ts, GLA_V), lambda i: (i, 0)),
                   pl.BlockSpec((8, GLA_DV), lambda i: (0, 0))],
        out_shape=[jax.ShapeDtypeStruct((S, GLA_V), F32), jax.ShapeDtypeStruct((S, GLA_V), BF16),
                   jax.ShapeDtypeStruct((8, GLA_DV), F32)],
        compiler_params=_cp("arbitrary"), name="gla_out_bwd")(dmixed, o, proj, gng)


def _seg_matrix(width, seg, value):
    r = lax.broadcasted_iota(jnp.int32, (width, width), 0) // seg
    c = lax.broadcasted_iota(jnp.int32, (width, width), 1) // seg
    return jnp.where(r == c, value, 0.0).astype(BF16)


def _seg_sum(x, seg_matrix):
    hi = x.astype(BF16)
    lo = (x - hi.astype(F32)).astype(BF16)
    return _dot(hi, seg_matrix, NN) + _dot(lo, seg_matrix, NN)


def _head_norm(proj, qg, kg, ts):
    S = proj.shape[0]
    W = ATTN_DIM

    def body(q_ref, k_ref, qg_ref, kg_ref, qn_ref, kn_ref):
        seg = _seg_matrix(W, ATTN_HD, 1.0 / ATTN_HD)
        for x_ref, g_ref, o_ref, scale in ((q_ref, qg_ref, qn_ref, ATTN_HD ** -0.5), (k_ref, kg_ref, kn_ref, 1.0)):
            xv = x_ref[...]
            ms = _seg_sum(xv * xv, seg)
            o_ref[...] = xv * lax.rsqrt(ms + EPS) * (g_ref[...] * scale)

    blk = lambda off: pl.BlockSpec((ts, W), lambda i: (i, off // W))
    out = pl.BlockSpec((ts, W), lambda i: (i, 0))
    vec = pl.BlockSpec((1, W), lambda i: (0, 0))
    return pl.pallas_call(
        body, grid=(S // ts,), in_specs=[blk(O_AQ), blk(O_AK), vec, vec], out_specs=[out] * 2,
        out_shape=[jax.ShapeDtypeStruct((S, W), F32)] * 2, compiler_params=_cp("parallel"), name="attn_head_norm")(
            proj, proj, qg, kg)


def _slope(head):
    one = jnp.ones((1, 1), jnp.int32)
    return 1.0 / jnp.left_shift(one, one * (head + 1)).astype(F32)


ATTN_GROUP = 4


def _attn_rows(d, g, r):
    start = g * d * ATTN_BLOCK + r
    return pl.ds(start, ATTN_BLOCK) if d == 1 else pl.ds(start, ATTN_BLOCK, stride=d)


def _for_blocks(d, G, fn):
    for g in range(G):
        if d <= ATTN_GROUP:
            for r in range(d):
                fn(g, r)
        else:
            def step(r, carry, g=g):
                fn(g, r)
                return carry
            lax.fori_loop(0, d, step, 0, unroll=ATTN_GROUP)


def _attn_specs(d, S):
    G = max(1, ATTN_GROUP // d)
    edge = d * ATTN_BLOCK
    tq = G * edge
    nb, n_edge = S // tq, S // edge

    def specs(off=0):
        return [pl.BlockSpec((tq, LANE), lambda hp, n: (n, off + hp)),
                pl.BlockSpec((edge, LANE), lambda hp, n: (jnp.maximum(n * G - 1, 0), off + hp)),
                pl.BlockSpec((edge, LANE), lambda hp, n: (jnp.minimum((n + 1) * G, n_edge - 1), off + hp))]

    return G, nb, specs


def _attn_bias(d, hp, first_tile):
    B = ATTN_BLOCK
    iq = lax.broadcasted_iota(jnp.int32, (B, 2 * B), 0)
    ik = lax.broadcasted_iota(jnp.int32, (B, 2 * B), 1)
    rel = iq + B - ik
    window = (rel >= 0) & (rel <= B)
    relf = (d * rel).astype(F32)
    full = [jnp.where(window, -_slope(hp * 2 + h) * relf, NEG) for h in range(2)]
    edge = [jnp.where((ik >= B) | jnp.logical_not(first_tile), b, NEG) for b in full]
    return full, edge


def _attn_bias_t(d, hp, has_next):
    B = ATTN_BLOCK
    ik = lax.broadcasted_iota(jnp.int32, (B, B), 0)
    iq = lax.broadcasted_iota(jnp.int32, (B, B), 1)
    tiles = []
    for nxt in range(2):
        rel = iq - ik + nxt * B
        window = (rel >= 0) & (rel <= B)
        relf = (d * rel).astype(F32)
        tiles.append([jnp.where(window, -_slope(hp * 2 + h) * relf, NEG) for h in range(2)])
    tiles.append([jnp.where(has_next, b, NEG) for b in tiles[1]])
    return tiles


def _attn_fwd(qn, kn, proj, d):
    S, W = qn.shape
    G, nb, specs = _attn_specs(d, S)

    def body(q_ref, kp_ref, kc_ref, vp_ref, vc_ref, o_ref, l_ref):
        hp, n = pl.program_id(0), pl.program_id(1)
        lo = lax.broadcasted_iota(jnp.int32, (1, LANE), 1) < ATTN_HD
        full, edge = _attn_bias(d, hp, n == 0)

        def sub(g, r):
            rows = _attn_rows(d, g, r)
            before = _attn_rows(d, max(g - 1, 0), r)
            kb_ref, vb_ref = (kp_ref, vp_ref) if g == 0 else (kc_ref, vc_ref)
            bias = edge if g == 0 else full
            qv = q_ref[rows, :].astype(BF16)
            kv = jnp.concatenate([kb_ref[before, :], kc_ref[rows, :]], axis=0).astype(BF16)
            vv = jnp.concatenate([vb_ref[before, :], vc_ref[rows, :]], axis=0).astype(BF16)
            outs, lses = [], []
            for h in range(2):
                qm = jnp.where(lo == (h == 0), qv, jnp.zeros_like(qv))
                s = _dot(qm, kv, NT) + bias[h]
                m = jnp.max(s, axis=-1, keepdims=True)
                p = jnp.exp(s - m)
                den = jnp.sum(p, axis=-1, keepdims=True)
                outs.append(_dot(p.astype(BF16), vv, NN) / den)
                lses.append(m + jnp.log(den))
            o_ref[rows, :] = jnp.where(lo, outs[0], outs[1])
            l_ref[rows, :] = jnp.where(lo, lses[0], lses[1])

        _for_blocks(d, G, sub)

    cur, prev, _ = specs()
    vcur, vprev, _ = specs(O_AV // LANE)
    return pl.pallas_call(
        body, grid=(W // LANE, nb), in_specs=[cur, prev, cur, vprev, vcur], out_specs=[cur, cur],
        out_shape=[jax.ShapeDtypeStruct((S, W), F32)] * 2,
        compiler_params=_cp("parallel", "arbitrary"), name=f"attn_fwd_d{d}")(qn, kn, kn, proj, proj)


def _attn_merge(y_gla, os_, ls_, ts):
    S, W = os_[0].shape

    def body(yg, o1, o2, o3, l1, l2, l3, mixed_ref, mixed_t_ref, y_ref, lse_ref):
        a, b, c = l1[...], l2[...], l3[...]
        m = jnp.maximum(jnp.maximum(a, b), c)
        ea, eb, ec = jnp.exp(a - m), jnp.exp(b - m), jnp.exp(c - m)
        tot = ea + eb + ec
        y = (ea * o1[...] + eb * o2[...] + ec * o3[...]) / tot
        y_ref[...] = y
        mixed_ref[:, :W] = yg[...]
        mixed_ref[:, W:] = y.astype(BF16)
        mixed_t_ref[:W, :] = yg[...].astype(F32).T.astype(BF16)
        mixed_t_ref[W:, :] = y.T.astype(BF16)
        lse_ref[...] = m + jnp.log(tot)

    spec = pl.BlockSpec((ts, W), lambda i: (i, 0))
    return pl.pallas_call(
        body, grid=(S // ts,), in_specs=[spec] * 7,
        out_specs=[pl.BlockSpec((ts, 2 * W), lambda i: (i, 0)), _col_spec(2 * W, ts), spec, spec],
        out_shape=[jax.ShapeDtypeStruct((S, 2 * W), BF16), jax.ShapeDtypeStruct((2 * W, S), BF16),
                   jax.ShapeDtypeStruct((S, W), F32), jax.ShapeDtypeStruct((S, W), F32)],
        compiler_params=_cp("parallel"), name="attn_merge")(y_gla, *os_, *ls_)


def _attn_delta(dmixed, y, ts):
    S, W = y.shape

    def body(dy_ref, y_ref, d_ref):
        d_ref[...] = _seg_sum(dy_ref[...] * y_ref[...], _seg_matrix(W, ATTN_HD, 1.0))

    return pl.pallas_call(
        body, grid=(S // ts,), in_specs=[pl.BlockSpec((ts, W), lambda i: (i, 1)), pl.BlockSpec((ts, W), lambda i: (i, 0))],
        out_specs=pl.BlockSpec((ts, W), lambda i: (i, 0)), out_shape=jax.ShapeDtypeStruct((S, W), F32),
        compiler_params=_cp("parallel"), name="attn_delta")(dmixed, y)


def _attn_dq(qn, kn, proj, dmixed, lse, delta, d):
    S, W = qn.shape
    B = ATTN_BLOCK
    G, nb, specs = _attn_specs(d, S)

    def body(q_ref, kp_ref, kc_ref, vp_ref, vc_ref, dy_ref, l_ref, de_ref, dq_ref):
        hp, n = pl.program_id(0), pl.program_id(1)
        lo = lax.broadcasted_iota(jnp.int32, (1, LANE), 1) < ATTN_HD
        full, edge = _attn_bias(d, hp, n == 0)

        def sub(g, r):
            rows = _attn_rows(d, g, r)
            before = _attn_rows(d, max(g - 1, 0), r)
            kb_ref, vb_ref = (kp_ref, vp_ref) if g == 0 else (kc_ref, vc_ref)
            bias = edge if g == 0 else full
            qv, dyv = q_ref[rows, :].astype(BF16), dy_ref[rows, :]
            lv, dev = l_ref[rows, :], de_ref[rows, :]
            kv = jnp.concatenate([kb_ref[before, :], kc_ref[rows, :]], axis=0).astype(BF16)
            vv = jnp.concatenate([vb_ref[before, :], vc_ref[rows, :]], axis=0).astype(BF16)
            outs = []
            for h in range(2):
                sel = lo == (h == 0)
                qm = jnp.where(sel, qv, jnp.zeros_like(qv))
                dym = jnp.where(sel, dyv, 0.0).astype(BF16)
                lse_h = lv[:, h * ATTN_HD:h * ATTN_HD + 1]
                del_h = dev[:, h * ATTN_HD:h * ATTN_HD + 1]
                p = jnp.exp(_dot(qm, kv, NT) + bias[h] - lse_h)
                ds = p * (_dot(dym, vv, NT) - del_h)
                outs.append(_dot(ds.astype(BF16), kv, NN) * (ATTN_HD ** -0.5))
            dq_ref[rows, :] = jnp.where(lo, outs[0], outs[1])

        _for_blocks(d, G, sub)

    cur, prev, _ = specs()
    vcur, vprev, _ = specs(O_AV // LANE)
    dycur, _, _ = specs(W // LANE)
    return pl.pallas_call(
        body, grid=(W // LANE, nb), in_specs=[cur, prev, cur, vprev, vcur, dycur, cur, cur], out_specs=cur,
        out_shape=jax.ShapeDtypeStruct((S, W), F32),
        compiler_params=_cp("parallel", "arbitrary"), name=f"attn_dq_d{d}")(qn, kn, kn, proj, proj, dmixed, lse, delta)


def _attn_dkv(qn, kn, proj, dmixed, lse, delta, d):
    S, W = qn.shape
    B = ATTN_BLOCK
    G, nb, specs = _attn_specs(d, S)

    def body(k_ref, v_ref, qc_ref, qn_ref, dyc_ref, dyn_ref, lc_ref, ln_ref, dec_ref, den_ref, dk_ref, dv_ref):
        hp, n = pl.program_id(0), pl.program_id(1)
        lo = lax.broadcasted_iota(jnp.int32, (1, LANE), 1) < ATTN_HD
        own, inner, outer = _attn_bias_t(d, hp, n + 1 < nb)

        def sub(g, r):
            rows = _attn_rows(d, g, r)
            kv, vv = k_ref[rows, :].astype(BF16), v_ref[rows, :].astype(BF16)
            dk = jnp.zeros((B, LANE), F32)
            dv = jnp.zeros((B, LANE), F32)
            inside = g + 1 < G
            after = _attn_rows(d, g + 1 if inside else 0, r)
            following = (qc_ref, dyc_ref, lc_ref, dec_ref) if inside else (qn_ref, dyn_ref, ln_ref, den_ref)
            for bias, qrows, (q_ref, dy_ref, l_ref, de_ref) in (
                    (own, rows, (qc_ref, dyc_ref, lc_ref, dec_ref)), (inner if inside else outer, after, following)):
                qv, dyv = q_ref[qrows, :].astype(BF16), dy_ref[qrows, :]
                lt, det = l_ref[qrows, :].T, de_ref[qrows, :].T
                for h in range(2):
                    sel = lo == (h == 0)
                    qm = jnp.where(sel, qv, jnp.zeros_like(qv))
                    dym = jnp.where(sel, dyv, 0.0).astype(BF16)
                    lse_h = lt[h * ATTN_HD:h * ATTN_HD + 1, :]
                    del_h = det[h * ATTN_HD:h * ATTN_HD + 1, :]
                    pt = jnp.exp(_dot(kv, qm, NT) + bias[h] - lse_h)
                    dv = dv + _dot(pt.astype(BF16), dym, NN)
                    dst = pt * (_dot(vv, dym, NT) - del_h)
                    dk = dk + _dot(dst.astype(BF16), qm, NN)
            dk_ref[rows, :] = dk
            dv_ref[rows, :] = dv

        _for_blocks(d, G, sub)

    cur, _, nxt = specs()
    vcur, _, _ = specs(O_AV // LANE)
    dycur, _, dynxt = specs(W // LANE)
    return pl.pallas_call(
        body, grid=(W // LANE, nb), in_specs=[cur, vcur, cur, nxt, dycur, dynxt, cur, nxt, cur, nxt], out_specs=[cur, cur],
        out_shape=[jax.ShapeDtypeStruct((S, W), F32)] * 2,
        compiler_params=_cp("parallel", "arbitrary"), name=f"attn_dkv_d{d}")(
            kn, proj, qn, qn, dmixed, dmixed, lse, lse, delta, delta)


def _attn_post(dqs, dks, dvs, proj, qg, kg, ts):
    S = proj.shape[0]
    W = ATTN_DIM

    def body(dq1, dq2, dq3, dk1, dk2, dk3, dv1, dv2, dv3, aq_ref, ak_ref, qg_ref, kg_ref, daq_ref, dak_ref, dav_ref, gg_ref):
        i = pl.program_id(0)
        seg = _seg_matrix(W, ATTN_HD, 1.0 / ATTN_HD)
        gsums = []
        for (d1, d2, d3), x_ref, g_ref, o_ref in (((dq1, dq2, dq3), aq_ref, qg_ref, daq_ref), ((dk1, dk2, dk3), ak_ref, kg_ref, dak_ref)):
            dy = d1[...] + d2[...] + d3[...]
            xv = x_ref[...]
            r = lax.rsqrt(_seg_sum(xv * xv, seg) + EPS)
            xh = xv * r
            dxh = dy * g_ref[...]
            o_ref[...] = (r * (dxh - xh * _seg_sum(dxh * xh, seg))).astype(BF16)
            gsums.append(jnp.sum(dy * xh, axis=0, keepdims=True))
        dav_ref[...] = (dv1[...] + dv2[...] + dv3[...]).astype(BF16)
        part = jnp.concatenate(gsums + [jnp.zeros((6, W), F32)], axis=0)

        @pl.when(i == 0)
        def _():
            gg_ref[...] = part

        @pl.when(i > 0)
        def _():
            gg_ref[...] += part

    row = pl.BlockSpec((ts, W), lambda i: (i, 0))
    blk = lambda off: pl.BlockSpec((ts, W), lambda i: (i, off // W))
    vec = pl.BlockSpec((1, W), lambda i: (0, 0))
    return pl.pallas_call(
        body, grid=(S // ts,), in_specs=[row] * 9 + [blk(O_AQ), blk(O_AK), vec, vec],
        out_specs=[row, row, row, pl.BlockSpec((8, W), lambda i: (0, 0))],
        out_shape=[jax.ShapeDtypeStruct((S, W), BF16)] * 3 + [jax.ShapeDtypeStruct((8, W), F32)],
        compiler_params=_cp("arbitrary"), name="attn_post")(*dqs, *dks, *dvs, proj, proj, qg, kg)


def _shift_down(cur, halo, n):
    ts = cur.shape[0]
    rows = lax.broadcasted_iota(jnp.int32, (ts, 1), 0)
    out = pltpu.roll(cur, n, 0)
    for t in range(n):
        out = jnp.where(rows == t, halo[8 - n + t:8 - n + t + 1, :], out)
    return out


def _shift_up(cur, halo, n):
    ts = cur.shape[0]
    rows = lax.broadcasted_iota(jnp.int32, (ts, 1), 0)
    out = pltpu.roll(cur, ts - n, 0)
    for t in range(n):
        out = jnp.where(rows == ts - n + t, halo[t:t + 1, :], out)
    return out


def _conv(cur, halo, w, b):
    return b + w[0:1, :] * _shift_down(cur, halo, 2) + w[1:2, :] * _shift_down(cur, halo, 1) + w[2:3, :] * cur


def _conv_swiglu(u0, conv_w8, conv_b, ts, tc):
    S, F2 = u0.shape
    F = F2 // 2
    nc = F // tc
    hb = ts // 8

    def body(ug_ref, ugh_ref, uv_ref, uvh_ref, wg_ref, wv_ref, bg_ref, bv_ref, a_ref, at_ref):
        first = pl.program_id(0) == 0
        ugh = jnp.where(first, 0.0, ugh_ref[...])
        uvh = jnp.where(first, 0.0, uvh_ref[...])
        g = _conv(ug_ref[...], ugh, wg_ref[...], bg_ref[...])
        v = _conv(uv_ref[...], uvh, wv_ref[...], bv_ref[...])
        a = g * _sigmoid(g) * v
        a_ref[...] = a.astype(BF16)
        at_ref[...] = a.T.astype(BF16)

    main = lambda off: pl.BlockSpec((ts, tc), lambda i, j: (i, j + off))
    halo = lambda off: pl.BlockSpec((8, tc), lambda i, j: (jnp.maximum(i * hb - 1, 0), j + off))
    wspec = lambda off: pl.BlockSpec((8, tc), lambda i, j: (0, j + off))
    bspec = lambda off: pl.BlockSpec((1, tc), lambda i, j: (0, j + off))
    return pl.pallas_call(
        body, grid=(S // ts, nc),
        in_specs=[main(0), halo(0), main(nc), halo(nc), wspec(0), wspec(nc), bspec(0), bspec(nc)],
        out_specs=[pl.BlockSpec((ts, tc), lambda i, j: (i, j)), pl.BlockSpec((tc, ts), lambda i, j: (j, i))],
        out_shape=[jax.ShapeDtypeStruct((S, F), BF16), jax.ShapeDtypeStruct((F, S), BF16)],
        compiler_params=_cp("parallel", "parallel"), name="conv_swiglu")(u0, u0, u0, u0, conv_w8, conv_w8, conv_b, conv_b)


def _ffn_du(da, u0, conv_w8, conv_b, ts, tc, ride=None):
    S, F2 = u0.shape
    F = F2 // 2
    nc = F // tc
    hb = ts // 8
    grid = (nc, S // ts)
    ride_arrays, ride_gather = ride if ride else ([], [])
    nr = len(ride_arrays)

    def body(da_ref, ug_ref, ugh_ref, uv_ref, uvh_ref, wg_ref, wv_ref, bg_ref, bv_ref, du_ref, sg_ref, sv_ref):
        i = pl.program_id(1)
        first = i == 0
        halves = []
        for u_ref, h_ref, w_ref, b_ref in ((ug_ref, ugh_ref, wg_ref, bg_ref), (uv_ref, uvh_ref, wv_ref, bv_ref)):
            u, halo, w = u_ref[...], jnp.where(first, 0.0, h_ref[...]), w_ref[...]
            s2, s1 = _shift_down(u, halo, 2), _shift_down(u, halo, 1)
            halves.append((b_ref[...] + w[0:1, :] * s2 + w[1:2, :] * s1 + w[2:3, :] * u, s2, s1, u))
        g, v = halves[0][0], halves[1][0]
        dav = da_ref[...]
        sig = _sigmoid(g)
        dus = (dav * v * (sig * (1.0 + g * (1.0 - sig))), dav * (g * sig))
        for h, (du, sums_ref) in enumerate(zip(dus, (sg_ref, sv_ref))):
            du_ref[h] = du
            _, s2, s1, u = halves[h]
            part = jnp.concatenate([jnp.sum(du * s2, axis=0, keepdims=True), jnp.sum(du * s1, axis=0, keepdims=True),
                                    jnp.sum(du * u, axis=0, keepdims=True), jnp.sum(du, axis=0, keepdims=True),
                                    jnp.zeros((4, tc), F32)], axis=0)

            @pl.when(first)
            def _(sums_ref=sums_ref, part=part):
                sums_ref[...] = part

            @pl.when(i > 0)
            def _(sums_ref=sums_ref, part=part):
                sums_ref[...] += part

    main = lambda off: pl.BlockSpec((ts, tc), lambda j, i: (i, j + off))
    halo = lambda off: pl.BlockSpec((8, tc), lambda j, i: (jnp.maximum(i * hb - 1, 0), j + off))
    wspec = lambda off: pl.BlockSpec((8, tc), lambda j, i: (0, j + off))
    bspec = lambda off: pl.BlockSpec((1, tc), lambda j, i: (0, j + off))
    sums_spec = pl.BlockSpec((8, tc), lambda j, i: (0, j))
    outs = pl.pallas_call(
        _riding(body, 9, 3, ride_gather, grid), grid=grid,
        in_specs=[main(0), main(0), halo(0), main(nc), halo(nc), wspec(0), wspec(nc), bspec(0), bspec(nc)] + [ANY_SPEC] * nr,
        out_specs=[pl.BlockSpec((2, ts, tc), lambda j, i: (0, i, j)), sums_spec, sums_spec] + [ANY_SPEC] * nr,
        out_shape=[jax.ShapeDtypeStruct((2, S, F), F32), jax.ShapeDtypeStruct((8, F), F32), jax.ShapeDtypeStruct((8, F), F32)]
        + _exchange_shapes(ride_arrays, ride_gather),
        scratch_shapes=_exchange_sems(nr) if nr else [],
        compiler_params=_cp("arbitrary", "arbitrary"), name="ffn_du")(
            da, u0, u0, u0, u0, conv_w8, conv_w8, conv_b, conv_b, *ride_arrays)
    return outs[0], outs[1], outs[2], outs[3:]


def _ffn_du0(du, conv_w8, ts, tc):
    _, S, F = du.shape
    nc = F // tc
    hb = ts // 8
    nrow = S // ts

    def body(du_ref, duh_ref, w_ref, o_ref):
        last = pl.program_id(0) == nrow - 1
        cur, halo, w = du_ref[...], jnp.where(last, 0.0, duh_ref[...]), w_ref[...]
        o_ref[...] = (w[2:3, :] * cur + w[1:2, :] * _shift_up(cur, halo, 1) + w[0:1, :] * _shift_up(cur, halo, 2)).astype(BF16)

    return pl.pallas_call(
        body, grid=(nrow, 2, nc),
        in_specs=[pl.BlockSpec((None, ts, tc), lambda i, h, j: (h, i, j)),
                  pl.BlockSpec((None, 8, tc), lambda i, h, j: (h, jnp.minimum((i + 1) * hb, S // 8 - 1), j)),
                  pl.BlockSpec((8, tc), lambda i, h, j: (0, h * nc + j))],
        out_specs=pl.BlockSpec((ts, tc), lambda i, h, j: (i, h * nc + j)), out_shape=jax.ShapeDtypeStruct((S, 2 * F), BF16),
        compiler_params=_cp("parallel", "parallel", "parallel"), name="ffn_du0")(du, du, conv_w8)


def _loss_resid(x2, t2, g2, target, ts):
    S, D = x2.shape

    def body(x_ref, t_ref, g_ref, y_ref, dx_ref, dt_ref, sums_ref):
        i = pl.program_id(0)
        tv, gv = t_ref[...], g_ref[...]
        e = x_ref[...] + gv * tv - y_ref[...]
        dx = e * (1.0 / D)
        dx_ref[...] = dx
        dt_ref[...] = (dx * gv).astype(BF16)
        part = jnp.concatenate([jnp.sum(e * e, axis=0, keepdims=True), jnp.sum(dx * tv, axis=0, keepdims=True),
                                jnp.zeros((6, D), F32)], axis=0)

        @pl.when(i == 0)
        def _():
            sums_ref[...] = part

        @pl.when(i > 0)
        def _():
            sums_ref[...] += part

    row, vec = _row_spec(ts, D), _vec_spec(D)
    return pl.pallas_call(
        body, grid=(S // ts,), in_specs=[row, row, vec, row], out_specs=[row, row, pl.BlockSpec((8, D), lambda i: (0, 0))],
        out_shape=[jax.ShapeDtypeStruct((S, D), F32), jax.ShapeDtypeStruct((S, D), BF16), jax.ShapeDtypeStruct((8, D), F32)],
        compiler_params=_cp("arbitrary"), name="loss_resid")(x2, t2, g2, target)


def _adamw(w, g, m, v, name):
    shape = w.shape
    n = math.prod(shape)
    view = (n // LANE, LANE) if n % LANE == 0 else (math.prod(shape[:-1]), shape[-1])
    R, C = view
    tr = R
    for cand in (1024, 512, 256):
        if R > cand and R % cand == 0:
            tr = cand
            break

    def body(w_ref, g_ref, m_ref, v_ref, d_ref, nm_ref, nv_ref):
        gv = g_ref[...]
        nm = ADAM_B1 * m_ref[...] + (1.0 - ADAM_B1) * gv
        nv = ADAM_B2 * v_ref[...] + (1.0 - ADAM_B2) * (gv * gv)
        m_hat = nm / (1.0 - ADAM_B1 ** ADAM_STEP)
        v_hat = nv / (1.0 - ADAM_B2 ** ADAM_STEP)
        d_ref[...] = -ADAM_LR * (m_hat / (jnp.sqrt(v_hat) + ADAM_EPS) + ADAM_WD * w_ref[...])
        nm_ref[...] = nm
        nv_ref[...] = nv

    spec = pl.BlockSpec((tr, C), lambda i: (i, 0))
    outs = pl.pallas_call(
        body, grid=(R // tr,), in_specs=[spec] * 4, out_specs=[spec] * 3, out_shape=[jax.ShapeDtypeStruct(view, F32)] * 3,
        compiler_params=_cp("parallel"), name=name)(*[a.reshape(view) for a in (w, g, m, v)])
    return [o.reshape(shape) for o in outs]


def _pad_rows8(a):
    return jnp.concatenate([a, jnp.zeros((8 - a.shape[0], a.shape[1]), a.dtype)], axis=0)


def _local_step(x, target, mod, n1g, w_in_p, wg_p, bg, gng, qng, kng, w_out_s, n2g, w_up_s, conv_w, conv_b, w_down_s):
    S, D = x.shape
    F = w_down_s.shape[0] * N_DEV
    ts = min(512, S)
    sh1, sc1, g1, sh2, sc2, g2 = [mod[i:i + 1] for i in range(6)]
    conv_w8 = _pad_rows8(conv_w)
    qg_t, kg_t = jnp.tile(qng, (1, ATTN_HEADS)), jnp.tile(kng, (1, ATTN_HEADS))

    h1, h1_t = _rms_mod(x, n1g, sc1, sh1, ts, "rms_mod1")
    proj, (g_out,) = _mm(h1, w_in_p, NN, 512, PROJ_W, 1024, F32, "mm_in", ride=([w_out_s], [True]))
    w_out = g_out.reshape(-1, D)
    la = _gate_fwd(proj, wg_p, bg, ts)
    o_gla, states, (g_up,) = _gla_fwd(proj, la, 512, ride=([w_up_s], [True]))
    w_up = _cols_from_blocks(g_up)
    y_gla = _gla_out(o_gla, proj, gng, ts)
    qn, kn = _head_norm(proj, qg_t, kg_t, ts)
    branches = [_attn_fwd(qn, kn, proj, d) for d in DILATIONS]
    mixed, mixed_t, y_att, lse = _attn_merge(y_gla, [b[0] for b in branches], [b[1] for b in branches], ts)
    t1 = _mm(mixed, w_out, NN, 512, 1024, 1024, F32, "mm_out")
    x2, h2, h2_t = _resid_rms_mod(x, t1, g1, n2g, sc2, sh2, ts, "resid_rms_mod2")
    u0, (g_down,) = _mm(h2, w_up, NN, 512, 2816, 1024, F32, "mm_up", ride=([w_down_s], [True]))
    w_down = g_down.reshape(F, D)
    tc = 1408 if F % 1408 == 0 else F
    a, a_t = _conv_swiglu(u0, conv_w8, conv_b, min(256, S), tc)
    t2 = _mm(a, w_down, NN, 512, 1024, F, F32, "mm_down")
    dx3, dt2, sums3 = _loss_resid(x2, t2, g2, target, ts)
    loss_row, dg2 = sums3[0:1], sums3[1:2]

    g_w_down = _mm(a_t, dt2, NN, 1408, 1024, 512, F32, "mm_gw_down")
    da = _mm(dt2, w_down, NT, 512, 2816, 1024, F32, "mm_da")
    du, sums_g, sums_v, (r_down,) = _ffn_du(da, u0, conv_w8, conv_b, min(256, S), tc,
                                            ride=([g_w_down.reshape(N_DEV, -1, D)], [False]))
    g_conv_w = jnp.concatenate([sums_g[0:3], sums_v[0:3]], axis=1)
    g_conv_b = jnp.concatenate([sums_g[3:4], sums_v[3:4]], axis=1)
    du0 = _ffn_du0(du, conv_w8, min(256, S), tc)
    g_w_up = _mm(h2_t, du0, NN, 512, 2816, 512, F32, "mm_gw_up")
    dh2 = _mm(du0, w_up, NT, 512, 1024, 2816, F32, "mm_dh2")
    dx2, sums2, dt1 = _rms_mod_bwd(dh2, x2, dx3, n2g, sc2, ts, "rms_mod_bwd2", t_prev=t1, g_prev=g1)
    dsh2, dsc2, g_n2g, dg1 = sums2[0:1], sums2[1:2], sums2[2:3], sums2[3:4]
    g_w_out = _mm(mixed_t, dt1, NN, 1024, 1024, 512, F32, "mm_gw_out")
    dmixed = _mm(dt1, w_out, NT, 512, 1024, 1024, F32, "mm_dmixed")
    do_gla, dgr, gng_sums = _gla_out_bwd(dmixed, o_gla, proj, gng, ts)
    dgq, dgk, dgv, dla, (r_up, r_out) = _gla_bwd(
        proj, la, do_gla, states, 512, ride=([_col_blocks(g_w_up), g_w_out.reshape(N_DEV, -1, D)], [False, False]))
    dglr, g_wg_p, gb_sums = _gate_bwd(dla, la, proj, wg_p, ts)
    delta = _attn_delta(dmixed, y_att, ts)
    dqs = [_attn_dq(qn, kn, proj, dmixed, lse, delta, d) for d in DILATIONS]
    dkvs = [_attn_dkv(qn, kn, proj, dmixed, lse, delta, d) for d in DILATIONS]
    daq, dak, dav, qk_sums = _attn_post(dqs, [t[0] for t in dkvs], [t[1] for t in dkvs], proj, qg_t, kg_t, ts)
    dproj = jnp.concatenate([dgq, dgk, dgv, dgr, daq, dak, dav, dglr, jnp.zeros((S, PROJ_W - O_GLR - LANE), BF16)], axis=1)
    g_w_in_p = _mm(h1_t, dproj, NN, 512, PROJ_W, 512, F32, "mm_gw_in")
    g_w_in = jnp.concatenate([g_w_in_p[:, :GLR_SRC], g_w_in_p[:, O_GLR:O_GLR + GLA_RANK], g_w_in_p[:, GLR_SRC:O_GLR]], axis=1)
    dh1, (r_in,) = _mm(dproj, w_in_p, NT, 512, 1024, PROJ_W, F32, "mm_dh1", ride=([_col_blocks(g_w_in)], [False]))
    dx, sums1 = _rms_mod_bwd(dh1, x, dx2, n1g, sc1, ts, "rms_mod_bwd1")
    dsh1, dsc1, g_n1g = sums1[0:1], sums1[1:2], sums1[2:3]

    dmod = jnp.concatenate([dsh1, dsc1, dg1, dsh2, dsc2, dg2], axis=1)
    grads = dict(n1g=g_n1g, w_in=r_in, wg=g_wg_p[:GLA_RANK], bg=gb_sums[0:1], gng=gng_sums[0:1],
                 qng_lanes=qk_sums[0:1], kng_lanes=qk_sums[1:2], w_out=r_out, n2g=g_n2g, w_up=r_up,
                 conv_w=g_conv_w, conv_b=g_conv_b, w_down=r_down)
    return loss_row, dx, dmod, grads


def _col_blocks(a):
    R, W = a.shape
    return a.reshape(R, N_DEV, W // N_DEV).transpose(1, 0, 2)


def _cols_from_blocks(a):
    n, R, C = a.shape
    return a.transpose(1, 0, 2).reshape(R, n * C)


def kernel(x, c, w_ada, b_ada, norm1_g, w_in, gla_w_gate, gla_b_gate, gla_norm_g, q_norm_g, k_norm_g, w_out, norm2_g, w_up, conv_w, conv_b, w_down, loss_target, m_w_ada, m_b_ada, m_norm1_g, m_w_in, m_gla_w_gate, m_gla_b_gate, m_gla_norm_g, m_q_norm_g, m_k_norm_g, m_w_out, m_norm2_g, m_w_up, m_conv_w, m_conv_b, m_w_down, v_w_ada, v_b_ada, v_norm1_g, v_w_in, v_gla_w_gate, v_gla_b_gate, v_gla_norm_g, v_q_norm_g, v_k_norm_g, v_w_out, v_norm2_g, v_w_up, v_conv_w, v_conv_b, v_w_down):
    axes = ("x", "y", "c")
    me = 4 * lax.axis_index("x") + 2 * lax.axis_index("y") + lax.axis_index("c")
    S, D = x.shape[1], x.shape[2]
    x2d, tgt2d = x[0], loss_target[0]
    w_in_s, w_out_s, w_up_s, w_down_s, w_ada_s = w_in[0], w_out[0], w_up[0], w_down[0], w_ada[0]
    conv_w_s, wg_s = conv_w[0], gla_w_gate[0]
    in_c, up_c, ada_c, wg_c, cw_c = w_in_s.shape[1], w_up_s.shape[1], w_ada_s.shape[1], wg_s.shape[1], conv_w_s.shape[1]
    F = w_down_s.shape[0] * N_DEV

    small = jnp.concatenate([conv_w_s.reshape(1, -1), wg_s.reshape(1, -1)], axis=1)
    n_small = small.shape[1]
    small = jnp.pad(small, ((0, 0), (0, -n_small % LANE)))
    g_c, g_in, g_small = _exchange([c, w_in_s.astype(BF16), small], [True] * 3, "gather_w_in")
    c_all = g_c.reshape(N_DEV, D)
    w_in_full = _cols_from_blocks(g_in)
    w_in_p = jnp.concatenate([w_in_full[:, :GLR_SRC], w_in_full[:, GLR_SRC + GLA_RANK:],
                              w_in_full[:, GLR_SRC:GLR_SRC + GLA_RANK], jnp.zeros((D, PROJ_W - O_GLR - GLA_RANK), BF16)], axis=1)
    g_small = g_small.reshape(N_DEV, -1)
    conv_w_full = _cols_from_blocks(g_small[:, :3 * cw_c].reshape(N_DEV, 3, cw_c))
    wg_full = _cols_from_blocks(g_small[:, 3 * cw_c:n_small].reshape(N_DEV, GLA_RANK, wg_c))
    wg_p = jnp.concatenate([wg_full, jnp.zeros((LANE - GLA_RANK, wg_full.shape[1]), F32)], axis=0)

    b_shard = lax.dynamic_slice(b_ada, (0, me * ada_c), (1, ada_c))
    mod_part = _ada_fwd(c_all, w_ada_s, b_shard)
    mod_recv, = _exchange([mod_part.reshape(N_DEV, 1, ada_c)], [False], "exchange_mod")
    mod = mod_recv.reshape(6, D)

    loss_row, dx, dmod, gr = _local_step(
        x2d, tgt2d, mod, norm1_g, w_in_p, wg_p, gla_b_gate, gla_norm_g, q_norm_g, k_norm_g,
        w_out_s.astype(BF16), norm2_g, w_up_s.astype(BF16), conv_w_full, conv_b, w_down_s.astype(BF16))
    loss = lax.psum(0.5 / D * jnp.sum(loss_row), axes)

    parts = [dmod, gr["n1g"], gr["bg"], gr["gng"], gr["qng_lanes"], gr["kng_lanes"], gr["n2g"], gr["conv_b"],
             gr["wg"].reshape(1, -1), gr["conv_w"].reshape(1, -1)]
    sizes = [p.shape[1] for p in parts]
    packed = jnp.concatenate(parts, axis=1)
    packed = jnp.pad(packed, ((0, 0), (0, -packed.shape[1] % (8 * LANE))))
    gathered, = _exchange([packed.reshape(8, -1)], [True], "gather_small_grads")
    gathered = gathered.reshape(N_DEV, -1)
    total = _sum_slots(gathered.reshape(N_DEV, 8, -1), "sum_small_grads").reshape(1, -1)
    offs = [0]
    for s_ in sizes:
        offs.append(offs[-1] + s_)
    t_dmod, t_n1g, t_bg, t_gng, t_qng, t_kng, t_n2g, t_conv_b, t_wg, t_conv_w = [
        total[:, offs[i]:offs[i + 1]] for i in range(len(sizes))]
    g_b_ada = t_dmod
    g_qng = t_qng.reshape(ATTN_HEADS, ATTN_HD).sum(axis=0, keepdims=True)
    g_kng = t_kng.reshape(ATTN_HEADS, ATTN_HD).sum(axis=0, keepdims=True)
    g_wg = lax.dynamic_slice(t_wg.reshape(GLA_RANK, -1), (0, me * wg_c), (GLA_RANK, wg_c))
    g_conv_w = lax.dynamic_slice(t_conv_w.reshape(3, -1), (0, me * cw_c), (3, cw_c))
    dmod_shard = lax.dynamic_slice(gathered[:, :6 * D], (0, me * ada_c), (N_DEV, ada_c))
    g_w_ada = _ada_bwd(c_all, dmod_shard)

    g_w_in = _sum_slots(gr["w_in"], "sum_gw_in")
    g_w_out = _sum_slots(gr["w_out"], "sum_gw_out")
    g_w_up = _sum_slots(gr["w_up"], "sum_gw_up")
    g_w_down = _sum_slots(gr["w_down"], "sum_gw_down")

    names = ["w_ada", "b_ada", "norm1_g", "w_in", "gla_w_gate", "gla_b_gate", "gla_norm_g", "q_norm_g", "k_norm_g",
             "w_out", "norm2_g", "w_up", "conv_w", "conv_b", "w_down"]
    ws = [w_ada, b_ada, norm1_g, w_in, gla_w_gate, gla_b_gate, gla_norm_g, q_norm_g, k_norm_g, w_out, norm2_g, w_up, conv_w, conv_b, w_down]
    ms = [m_w_ada, m_b_ada, m_norm1_g, m_w_in, m_gla_w_gate, m_gla_b_gate, m_gla_norm_g, m_q_norm_g, m_k_norm_g, m_w_out, m_norm2_g, m_w_up, m_conv_w, m_conv_b, m_w_down]
    vs = [v_w_ada, v_b_ada, v_norm1_g, v_w_in, v_gla_w_gate, v_gla_b_gate, v_gla_norm_g, v_q_norm_g, v_k_norm_g, v_w_out, v_norm2_g, v_w_up, v_conv_w, v_conv_b, v_w_down]
    gs = [g_w_ada, g_b_ada, t_n1g, g_w_in, g_wg, t_bg, t_gng, g_qng, g_kng, g_w_out, t_n2g, g_w_up, g_conv_w, t_conv_b, g_w_down]
    gs = [g.reshape(w.shape) for g, w in zip(gs, ws)]
    deltas, new_ms, new_vs = [], [], []
    for nm, w, g, m, v in zip(names, ws, gs, ms, vs):
        d_, m_, v_ = _adamw(w, g, m, v, "adamw_" + nm)
        deltas.append(d_)
        new_ms.append(m_)
        new_vs.append(v_)
    return (loss, dx.reshape(x.shape), *gs, *deltas, *new_ms, *new_vs)
```

```python
import functools
import math

import jax
import jax.numpy as jnp
from jax import lax
from jax.experimental import pallas as pl
from jax.experimental.pallas import tpu as pltpu

F32, BF16 = jnp.float32, jnp.bfloat16
HI = lax.Precision.HIGHEST
EPS = 1e-6
NEG = -1e30

N_DEV = 8
GLA_HEADS, GLA_DK, GLA_DV, GLA_RANK, GLA_TAU, GLA_CHUNK = 4, 64, 128, 16, 16.0, 64
ATTN_HEADS, ATTN_HD, ATTN_BLOCK = 8, 64, 128
DILATIONS = (1, 4, 16)
GLA_QK, GLA_V, ATTN_DIM = GLA_HEADS * GLA_DK, GLA_HEADS * GLA_DV, ATTN_HEADS * ATTN_HD
O_GQ, O_GK, O_GV, O_GR, O_AQ, O_AK, O_AV, O_GLR = 0, 256, 512, 1024, 1536, 2048, 2560, 3072
PROJ_W = 3328
LANE = 128
GLR_SRC = 2 * GLA_QK + 2 * GLA_V

ADAM_LR, ADAM_B1, ADAM_B2, ADAM_EPS, ADAM_WD, ADAM_STEP = 0.001, 0.9, 0.999, 1e-08, 0.01, 10

VMEM_LIMIT = 56 * 1024 * 1024
SUM_BLOCK_ELEMS = 256 * 1024


def _cp(*sem):
    return pltpu.CompilerParams(dimension_semantics=sem, vmem_limit_bytes=VMEM_LIMIT)


def _dot(a, b, dims, precision=None):
    return lax.dot_general(a, b, (dims, ((), ())), preferred_element_type=F32, precision=precision)


NN, NT, TN = ((1,), (0,)), ((1,), (1,)), ((0,), (0,))


def _sigmoid(z):
    return 1.0 / (1.0 + jnp.exp(-z))


ANY_SPEC = pl.BlockSpec(memory_space=pl.ANY)


def _exchange_shapes(arrays, gather):
    return [jax.ShapeDtypeStruct((N_DEV,) + (a.shape if g else a.shape[1:]), a.dtype) for a, g in zip(arrays, gather)]


def _exchange_sems(n):
    return [pltpu.SemaphoreType.DMA((n * (N_DEV - 1),)), pltpu.SemaphoreType.DMA((n * (N_DEV - 1),)), pltpu.SemaphoreType.DMA((n,))]


def _exchange_copies(ins, outs, gather, send_sems, recv_sems, local_sems):
    x, y, c = lax.axis_index("x"), lax.axis_index("y"), lax.axis_index("c")
    me = 4 * x + 2 * y + c
    copies = []
    for a in range(len(ins)):
        for p in range(1, N_DEV):
            px, py, pc = x ^ (p >> 2), y ^ ((p >> 1) & 1), c ^ (p & 1)
            peer = 4 * px + 2 * py + pc
            k = a * (N_DEV - 1) + p - 1
            copies.append(pltpu.make_async_remote_copy(
                src_ref=ins[a] if gather[a] else ins[a].at[peer], dst_ref=outs[a].at[me],
                send_sem=send_sems.at[k], recv_sem=recv_sems.at[k],
                device_id=(px, py, pc), device_id_type=pl.DeviceIdType.MESH))
        copies.append(pltpu.make_async_copy(ins[a] if gather[a] else ins[a].at[me], outs[a].at[me], local_sems.at[a]))
    return copies


def _riding(body, n_in, n_out, gather, grid):
    nr = len(gather)
    if not nr:
        return body

    def wrapped(*refs):
        ins, r_ins = refs[:n_in], refs[n_in:n_in + nr]
        outs, r_outs = refs[n_in + nr:n_in + nr + n_out], refs[n_in + nr + n_out:n_in + 2 * nr + n_out]
        scratch = refs[n_in + 2 * nr + n_out:]
        first = last = None
        for t, steps in enumerate(grid):
            pid = pl.program_id(t)
            first = (pid == 0) if first is None else first & (pid == 0)
            last = (pid == steps - 1) if last is None else last & (pid == steps - 1)
        copies = _exchange_copies(r_ins, r_outs, gather, *scratch[-3:])

        @pl.when(first)
        def _():
            for cp in copies:
                cp.start()

        body(*ins, *outs, *scratch[:-3])

        @pl.when(last)
        def _():
            for cp in copies:
                cp.wait()

    return wrapped


def _exchange(arrays, gather, name):
    n = len(arrays)

    def body(*refs):
        copies = _exchange_copies(refs[:n], refs[n:2 * n], gather, *refs[2 * n:])
        for cp in copies:
            cp.start()
        for cp in copies:
            cp.wait()

    return pl.pallas_call(
        body, out_shape=_exchange_shapes(arrays, gather), in_specs=[ANY_SPEC] * n, out_specs=[ANY_SPEC] * n,
        scratch_shapes=_exchange_sems(n), name=name)(*arrays)


def _sum_slots(x, name):
    _, R, C = x.shape
    tr = max(t for t in range(8, min(SUM_BLOCK_ELEMS // C, R) + 1, 8) if R % t == 0)

    def body(x_ref, o_ref):
        acc = x_ref[0].astype(F32)
        for s in range(1, N_DEV):
            acc = acc + x_ref[s].astype(F32)
        o_ref[...] = acc

    return pl.pallas_call(
        body, grid=(R // tr,), in_specs=[pl.BlockSpec((N_DEV, tr, C), lambda i: (0, i, 0))],
        out_specs=pl.BlockSpec((tr, C), lambda i: (i, 0)), out_shape=jax.ShapeDtypeStruct((R, C), F32),
        compiler_params=_cp("parallel"), name=name)(x)


def _mm(a, b, mode, tm, tn, tk, out_dtype, name, ride=None):
    if mode == NN:
        (M, K), N = a.shape, b.shape[1]
    elif mode == NT:
        (M, K), N = a.shape, b.shape[0]
    else:
        (K, M), N = a.shape, b.shape[1]
    tm, tn, tk = min(tm, M), min(tn, N), min(tk, K)
    assert M % tm == 0 and N % tn == 0 and K % tk == 0, (name, M, N, K, tm, tn, tk)
    nk = K // tk
    if mode == NN:
        a_spec = pl.BlockSpec((tm, tk), lambda i, j, k: (i, k))
        b_spec = pl.BlockSpec((tk, tn), lambda i, j, k: (k, j))
    elif mode == NT:
        a_spec = pl.BlockSpec((tm, tk), lambda i, j, k: (i, k))
        b_spec = pl.BlockSpec((tn, tk), lambda i, j, k: (j, k))
    else:
        a_spec = pl.BlockSpec((tk, tm), lambda i, j, k: (k, i))
        b_spec = pl.BlockSpec((tk, tn), lambda i, j, k: (k, j))

    ride_arrays, ride_gather = ride if ride else ([], [])
    nr = len(ride_arrays)
    grid = (M // tm, N // tn, nk)

    own_acc = nk > 1 and out_dtype != F32

    def body(a_ref, b_ref, o_ref, *acc):
        p = _dot(a_ref[...].astype(BF16), b_ref[...].astype(BF16), mode)
        if nk == 1:
            o_ref[...] = p.astype(out_dtype)
        else:
            acc_ref = acc[0] if own_acc else o_ref
            k = pl.program_id(2)

            @pl.when(k == 0)
            def _():
                acc_ref[...] = p

            @pl.when(k > 0)
            def _():
                acc_ref[...] += p

            if own_acc:
                @pl.when(k == nk - 1)
                def _():
                    o_ref[...] = acc_ref[...].astype(out_dtype)

    outs = pl.pallas_call(
        _riding(body, 2, 1, ride_gather, grid), grid=grid, in_specs=[a_spec, b_spec] + [ANY_SPEC] * nr,
        out_specs=[pl.BlockSpec((tm, tn), lambda i, j, k: (i, j))] + [ANY_SPEC] * nr,
        out_shape=[jax.ShapeDtypeStruct((M, N), out_dtype)] + _exchange_shapes(ride_arrays, ride_gather),
        scratch_shapes=([pltpu.VMEM((tm, tn), F32)] if own_acc else []) + (_exchange_sems(nr) if nr else []),
        compiler_params=_cp(*(("arbitrary",) * 3 if nr else ("parallel", "parallel", "arbitrary"))), name=name)(a, b, *ride_arrays)
    return (outs[0], outs[1:]) if nr else outs[0]


def _ada_fwd(c_all, w_shard, b_shard):
    Nc = w_shard.shape[1]

    def body(c_ref, w_ref, b_ref, o_ref):
        cv = c_ref[...]
        o_ref[...] = _dot(cv * _sigmoid(cv), w_ref[...], NN, HI) + b_ref[...]

    return pl.pallas_call(body, out_shape=jax.ShapeDtypeStruct((N_DEV, Nc), F32), name="ada_fwd",
                          compiler_params=pltpu.CompilerParams(vmem_limit_bytes=VMEM_LIMIT))(c_all, w_shard, b_shard)


def _ada_bwd(c_all, dmod_shard):
    D, Nc = c_all.shape[1], dmod_shard.shape[1]

    def body(c_ref, d_ref, o_ref):
        cv = c_ref[...]
        o_ref[...] = _dot(cv * _sigmoid(cv), d_ref[...], TN, HI)

    return pl.pallas_call(body, out_shape=jax.ShapeDtypeStruct((D, Nc), F32), name="ada_bwd",
                          compiler_params=pltpu.CompilerParams(vmem_limit_bytes=VMEM_LIMIT))(c_all, dmod_shard)


def _row_spec(ts, D):
    return pl.BlockSpec((ts, D), lambda i: (i, 0))


def _vec_spec(D):
    return pl.BlockSpec((1, D), lambda i: (0, 0))


def _col_spec(D, ts):
    return pl.BlockSpec((D, ts), lambda i: (0, i))


def _rms_mod(x, ng, sc, sh, ts, name):
    S, D = x.shape

    def body(x_ref, ng_ref, sc_ref, sh_ref, h_ref, ht_ref):
        xv = x_ref[...]
        r = lax.rsqrt(jnp.mean(xv * xv, axis=-1, keepdims=True) + EPS)
        h = xv * r * ng_ref[...] * (1.0 + sc_ref[...]) + sh_ref[...]
        h_ref[...] = h.astype(BF16)
        ht_ref[...] = h.T.astype(BF16)

    return pl.pallas_call(
        body, grid=(S // ts,), in_specs=[_row_spec(ts, D)] + [_vec_spec(D)] * 3, out_specs=[_row_spec(ts, D), _col_spec(D, ts)],
        out_shape=[jax.ShapeDtypeStruct((S, D), BF16), jax.ShapeDtypeStruct((D, S), BF16)],
        compiler_params=_cp("parallel"), name=name)(x, ng, sc, sh)


def _resid_rms_mod(x, t, g, ng, sc, sh, ts, name):
    S, D = x.shape

    def body(x_ref, t_ref, g_ref, ng_ref, sc_ref, sh_ref, x2_ref, h_ref, ht_ref):
        xv = x_ref[...] + g_ref[...] * t_ref[...]
        x2_ref[...] = xv
        r = lax.rsqrt(jnp.mean(xv * xv, axis=-1, keepdims=True) + EPS)
        h = xv * r * ng_ref[...] * (1.0 + sc_ref[...]) + sh_ref[...]
        h_ref[...] = h.astype(BF16)
        ht_ref[...] = h.T.astype(BF16)

    return pl.pallas_call(
        body, grid=(S // ts,), in_specs=[_row_spec(ts, D)] * 2 + [_vec_spec(D)] * 4,
        out_specs=[_row_spec(ts, D)] * 2 + [_col_spec(D, ts)],
        out_shape=[jax.ShapeDtypeStruct((S, D), F32), jax.ShapeDtypeStruct((S, D), BF16), jax.ShapeDtypeStruct((D, S), BF16)],
        compiler_params=_cp("parallel"), name=name)(x, t, g, ng, sc, sh)


def _rms_mod_bwd(dh, xin, dres, ng, sc, ts, name, t_prev=None, g_prev=None):
    S, D = xin.shape
    chain = t_prev is not None

    def body(*refs):
        if chain:
            dh_ref, x_ref, dr_ref, ng_ref, sc_ref, t_ref, g_ref, dx_ref, sums_ref, dt_ref = refs
        else:
            dh_ref, x_ref, dr_ref, ng_ref, sc_ref, dx_ref, sums_ref = refs
        i = pl.program_id(0)
        xv, dhv = x_ref[...], dh_ref[...]
        r = lax.rsqrt(jnp.mean(xv * xv, axis=-1, keepdims=True) + EPS)
        xh = xv * r
        ngv, scv = ng_ref[...], sc_ref[...]
        dxh = dhv * (ngv * (1.0 + scv))
        dx = dr_ref[...] + r * (dxh - xh * jnp.mean(dxh * xh, axis=-1, keepdims=True))
        dx_ref[...] = dx
        dhx = dhv * xh
        rows = [jnp.sum(dhv, axis=0, keepdims=True), jnp.sum(dhx * ngv, axis=0, keepdims=True),
                jnp.sum(dhx * (1.0 + scv), axis=0, keepdims=True)]
        if chain:
            dt_ref[...] = (dx * g_ref[...]).astype(BF16)
            rows.append(jnp.sum(dx * t_ref[...], axis=0, keepdims=True))
        rows.append(jnp.zeros((8 - len(rows), D), F32))
        part = jnp.concatenate(rows, axis=0)

        @pl.when(i == 0)
        def _():
            sums_ref[...] = part

        @pl.when(i > 0)
        def _():
            sums_ref[...] += part

    row, vec = _row_spec(ts, D), _vec_spec(D)
    sums_spec = pl.BlockSpec((8, D), lambda i: (0, 0))
    ins = [dh, xin, dres, ng, sc] + ([t_prev, g_prev] if chain else [])
    in_specs = [row, row, row, vec, vec] + ([row, vec] if chain else [])
    out_specs = [row, sums_spec] + ([row] if chain else [])
    out_shape = [jax.ShapeDtypeStruct((S, D), F32), jax.ShapeDtypeStruct((8, D), F32)] + (
        [jax.ShapeDtypeStruct((S, D), BF16)] if chain else [])
    return pl.pallas_call(body, grid=(S // ts,), in_specs=in_specs, out_specs=out_specs, out_shape=out_shape,
                          compiler_params=_cp("arbitrary"), name=name)(*ins)


def _gate_fwd(proj, wg_p, bg, ts):
    S = proj.shape[0]

    def body(glr_ref, w_ref, b_ref, la_ref):
        z = _dot(glr_ref[...], w_ref[...], NN, HI) + b_ref[...]
        la_ref[...] = (jnp.minimum(z, 0.0) - jnp.log(1.0 + jnp.exp(-jnp.abs(z)))) * (1.0 / GLA_TAU)

    return pl.pallas_call(
        body, grid=(S // ts,),
        in_specs=[pl.BlockSpec((ts, LANE), lambda i: (i, O_GLR // LANE)), pl.BlockSpec((LANE, GLA_QK), lambda i: (0, 0)),
                  pl.BlockSpec((1, GLA_QK), lambda i: (0, 0))],
        out_specs=pl.BlockSpec((ts, GLA_QK), lambda i: (i, 0)), out_shape=jax.ShapeDtypeStruct((S, GLA_QK), F32),
        compiler_params=_cp("parallel"), name="gla_gate_fwd")(proj, wg_p, bg)


def _gate_bwd(dla, la, proj, wg_p, ts):
    S = proj.shape[0]

    def body(dla_ref, la_ref, glr_ref, w_ref, dglr_ref, gw_ref, gb_ref):
        i = pl.program_id(0)
        dz = dla_ref[...] * (1.0 / GLA_TAU) * (1.0 - jnp.exp(GLA_TAU * la_ref[...]))
        dglr_ref[...] = _dot(dz, w_ref[...], NT, HI).astype(BF16)
        gw = _dot(glr_ref[...], dz, TN, HI)
        gb = jnp.concatenate([jnp.sum(dz, axis=0, keepdims=True), jnp.zeros((7, GLA_QK), F32)], axis=0)

        @pl.when(i == 0)
        def _():
            gw_ref[...] = gw
            gb_ref[...] = gb

        @pl.when(i > 0)
        def _():
            gw_ref[...] += gw
            gb_ref[...] += gb

    return pl.pallas_call(
        body, grid=(S // ts,),
        in_specs=[pl.BlockSpec((ts, GLA_QK), lambda i: (i, 0)), pl.BlockSpec((ts, GLA_QK), lambda i: (i, 0)),
                  pl.BlockSpec((ts, LANE), lambda i: (i, O_GLR // LANE)), pl.BlockSpec((LANE, GLA_QK), lambda i: (0, 0))],
        out_specs=[pl.BlockSpec((ts, LANE), lambda i: (i, 0)), pl.BlockSpec((LANE, GLA_QK), lambda i: (0, 0)),
                   pl.BlockSpec((8, GLA_QK), lambda i: (0, 0))],
        out_shape=[jax.ShapeDtypeStruct((S, LANE), BF16), jax.ShapeDtypeStruct((LANE, GLA_QK), F32),
                   jax.ShapeDtypeStruct((8, GLA_QK), F32)],
        compiler_params=_cp("arbitrary"), name="gla_gate_bwd")(dla, la, proj, wg_p)


def _tri(lower):
    r = lax.broadcasted_iota(jnp.int32, (GLA_CHUNK, GLA_CHUNK), 0)
    c = lax.broadcasted_iota(jnp.int32, (GLA_CHUNK, GLA_CHUNK), 1)
    return jnp.where((r >= c) if lower else (c >= r), 1.0, 0.0).astype(F32)


GLA_SUB = 16
GLA_NSUB = GLA_CHUNK // GLA_SUB
PAIR_QK = 2 * GLA_DK
PAIR_V = 2 * GLA_DV


def _band_selector():
    r = lax.broadcasted_iota(jnp.int32, (GLA_SUB * PAIR_QK, LANE), 0)
    c = lax.broadcasted_iota(jnp.int32, (GLA_SUB * PAIR_QK, LANE), 1)
    dist, head = r // PAIR_QK, (r % PAIR_QK) // GLA_DK
    return jnp.where(c == head * GLA_DK + (GLA_SUB - 1 - dist), 1.0, 0.0).astype(BF16)


def _flip_matrix():
    r = lax.broadcasted_iota(jnp.int32, (GLA_CHUNK, GLA_CHUNK), 0)
    c = lax.broadcasted_iota(jnp.int32, (GLA_CHUNK, GLA_CHUNK), 1)
    return jnp.where(r + c == GLA_CHUNK - 1, 1.0, 0.0).astype(BF16)


def _state_mask():
    r = lax.broadcasted_iota(jnp.int32, (PAIR_V, PAIR_QK), 0)
    c = lax.broadcasted_iota(jnp.int32, (PAIR_V, PAIR_QK), 1)
    return (r < GLA_DV) == (c < GLA_DK)


class _GlaChunk:
    def __init__(self, qs, kc, vc, g, sel):
        C = GLA_CHUNK
        self.qs, self.kc, self.vc = qs, kc, vc
        rows = lax.broadcasted_iota(jnp.int32, (C, 1), 0)
        lane = lax.broadcasted_iota(jnp.int32, (1, PAIR_QK), 1)
        self.rows, self.lane = rows, lane
        b = _dot(_tri(True), g, NN, HI)
        self.bl = b[C - 1:C, :]
        self.eb = jnp.exp(b)
        self.kdec = jnp.exp(self.bl - b)
        edge = lambda J: b[GLA_SUB * (J + 1):GLA_SUB * (J + 1) + 1, :]
        self.e_far = [jnp.exp(jnp.where(rows >= GLA_SUB * (J + 1), b - edge(J), NEG)) for J in range(GLA_NSUB - 1)]
        blk = rows // GLA_SUB
        bnext = edge(0)
        for J in range(1, GLA_NSUB - 1):
            bnext = jnp.where(blk == J, edge(J), bnext)
        self.e_khat = jnp.exp(jnp.where(blk < GLA_NSUB - 1, bnext - b, NEG))
        khat = kc * self.e_khat
        k2 = jnp.concatenate([jnp.where(lane < GLA_DK, khat, 0.0), jnp.where(lane >= GLA_DK, khat, 0.0)], axis=0)
        self.blk2 = jnp.concatenate([blk, blk], axis=0)
        self.m_far = jnp.concatenate([jnp.where(self.blk2 == J, k2, 0.0) for J in range(GLA_NSUB - 1)], axis=1).astype(BF16)
        self.qcat = jnp.concatenate([qs * e for e in self.e_far], axis=1).astype(BF16)
        a_far = _dot(self.qcat, self.m_far, NT)
        self.e_band, self.rk, terms = [], [], []
        for d in range(GLA_SUB):
            rk = pltpu.roll(kc, d, 0) if d else kc
            rb = pltpu.roll(b, d, 0) if d else b
            e = jnp.exp(jnp.where(rows >= d, b - rb, NEG))
            self.e_band.append(e)
            self.rk.append(rk)
            terms.append((qs * rk * e).astype(BF16))
        band = _dot(jnp.concatenate(terms, axis=1), sel, NN)
        a_band = pltpu.roll(band, LANE - (GLA_SUB - 1), 1, stride=1, stride_axis=0)
        dist = rows - lane % GLA_DK
        self.far_mask = dist >= GLA_SUB
        self.band_mask = (dist >= 0) & (dist < GLA_SUB)
        self.a = (a_band + jnp.where(self.far_mask, a_far, 0.0)).astype(BF16)
        self.lane_v = lax.broadcasted_iota(jnp.int32, (1, PAIR_V), 1)
        self.v2 = jnp.concatenate([jnp.where(self.lane_v < GLA_DV, vc, 0.0), jnp.where(self.lane_v >= GLA_DV, vc, 0.0)],
                                  axis=0).astype(BF16)


def _gla_fwd(proj, la, tb, ride=None):
    S = proj.shape[0]
    C = GLA_CHUNK
    tb = min(tb, S)
    nbc = tb // C
    npair = GLA_HEADS // 2
    scale = GLA_DK ** -0.5

    def body(q_ref, k_ref, v_ref, la_ref, sel_ref, o_ref, st_ref, state):
        @pl.when(pl.program_id(1) == 0)
        def _():
            state[...] = jnp.zeros_like(state)

        def chunk(ci, carry):
            sl = pl.ds(pl.multiple_of(ci * C, C), C)
            ch = _GlaChunk(q_ref[sl, :] * scale, k_ref[sl, :], v_ref[sl, :], la_ref[sl, :], sel_ref[...])
            st = state[...]
            st_ref[0, ci] = st
            o_ref[sl, :] = _dot((ch.qs * ch.eb).astype(BF16), st.astype(BF16), NT) + _dot(ch.a, ch.v2, NN)
            upd = _dot(ch.vc.astype(BF16), (ch.kc * ch.kdec).astype(BF16), TN)
            state[...] = st * jnp.exp(ch.bl) + jnp.where(_state_mask(), upd, 0.0)
            return carry

        lax.fori_loop(0, nbc, chunk, 0, unroll=8)

    qspec = lambda off: pl.BlockSpec((tb, PAIR_QK), lambda p, i: (i, off // PAIR_QK + p))
    ride_arrays, ride_gather = ride if ride else ([], [])
    nr = len(ride_arrays)
    grid = (npair, S // tb)
    outs = pl.pallas_call(
        _riding(body, 5, 2, ride_gather, grid), grid=grid,
        in_specs=[qspec(O_GQ), qspec(O_GK), pl.BlockSpec((tb, PAIR_V), lambda p, i: (i, O_GV // PAIR_V + p)),
                  pl.BlockSpec((tb, PAIR_QK), lambda p, i: (i, p)),
                  pl.BlockSpec((GLA_SUB * PAIR_QK, LANE), lambda p, i: (0, 0))] + [ANY_SPEC] * nr,
        out_specs=[pl.BlockSpec((tb, PAIR_V), lambda p, i: (i, p)),
                   pl.BlockSpec((1, nbc, PAIR_V, PAIR_QK), lambda p, i: (p, i, 0, 0))] + [ANY_SPEC] * nr,
        out_shape=[jax.ShapeDtypeStruct((S, GLA_V), F32), jax.ShapeDtypeStruct((npair, S // C, PAIR_V, PAIR_QK), F32)]
        + _exchange_shapes(ride_arrays, ride_gather),
        scratch_shapes=[pltpu.VMEM((PAIR_V, PAIR_QK), F32)] + (_exchange_sems(nr) if nr else []),
        compiler_params=_cp("arbitrary", "arbitrary"), name="gla_fwd")(proj, proj, proj, la, _band_selector(), *ride_arrays)
    return outs[0], outs[1], outs[2:]


def _gla_bwd(proj, la, do, states, tb, ride=None):
    S = proj.shape[0]
    C = GLA_CHUNK
    tb = min(tb, S)
    nbc = tb // C
    nblk = S // tb
    npair = GLA_HEADS // 2
    scale = GLA_DK ** -0.5

    def body(q_ref, k_ref, v_ref, la_ref, do_ref, st_ref, sel_ref, selt_ref, dq_ref, dk_ref, dv_ref, dla_ref, dstate):
        @pl.when(pl.program_id(1) == 0)
        def _():
            dstate[...] = jnp.zeros_like(dstate)

        def chunk(cc, carry):
            ci = nbc - 1 - cc
            sl = pl.ds(pl.multiple_of(ci * C, C), C)
            ch = _GlaChunk(q_ref[sl, :] * scale, k_ref[sl, :], v_ref[sl, :], la_ref[sl, :], sel_ref[...])
            qs, kc, rows = ch.qs, ch.kc, ch.rows
            doc_b = do_ref[sl, :].astype(BF16)
            st = st_ref[0, ci]
            dst = dstate[...]
            dst_b = dst.astype(BF16)
            ebl = jnp.exp(ch.bl)
            dq = _dot(doc_b, st.astype(BF16), NN) * ch.eb
            dk = _dot(ch.vc.astype(BF16), dst_b, NN) * ch.kdec
            dv = _dot((kc * ch.kdec).astype(BF16), dst_b, NT)
            dbl = jnp.sum(dst * st, axis=0, keepdims=True) * ebl + jnp.sum(kc * dk, axis=0, keepdims=True)
            da = _dot(doc_b, ch.v2, NT)
            dv2 = _dot(ch.a, doc_b, TN)
            dv = dv + jnp.where(ch.lane_v < GLA_DV, dv2[:C], dv2[C:])
            da_far = jnp.where(ch.far_mask, da, 0.0).astype(BF16)
            dqcat = _dot(da_far, ch.m_far, NN)
            dm = _dot(da_far, ch.qcat, TN)
            dk2 = jnp.zeros((2 * C, PAIR_QK), F32)
            for J in range(GLA_NSUB - 1):
                dq = dq + dqcat[:, J * PAIR_QK:(J + 1) * PAIR_QK] * ch.e_far[J]
                dk2 = dk2 + jnp.where(ch.blk2 == J, dm[:, J * PAIR_QK:(J + 1) * PAIR_QK], 0.0)
            dk = dk + jnp.where(ch.lane < GLA_DK, dk2[:C], dk2[C:]) * ch.e_khat
            flip = _flip_matrix()
            da_band = _dot(flip, jnp.where(ch.band_mask, da, 0.0).astype(BF16), NN)
            dband = pltpu.roll(da_band, LANE - (C - GLA_SUB), 1, stride=1, stride_axis=0)
            dband = _dot(flip, dband.astype(BF16), NN)
            dterms = _dot(dband.astype(BF16), selt_ref[...], NN)
            for d in range(GLA_SUB):
                dt = dterms[:, d * PAIR_QK:(d + 1) * PAIR_QK]
                dq = dq + dt * (ch.rk[d] * ch.e_band[d])
                dkr = dt * (qs * ch.e_band[d])
                dk = dk + (pltpu.roll(dkr, C - d, 0) if d else dkr)
            db = qs * dq - kc * dk
            db = jnp.where(rows == C - 1, db + dbl, db)
            dq_ref[sl, :] = (dq * scale).astype(BF16)
            dk_ref[sl, :] = dk.astype(BF16)
            dv_ref[sl, :] = dv.astype(BF16)
            dla_ref[sl, :] = _dot(_tri(False), db, NN, HI)
            upd = _dot(doc_b, (qs * ch.eb).astype(BF16), TN)
            dstate[...] = dst * ebl + jnp.where(_state_mask(), upd, 0.0)
            return carry

        lax.fori_loop(0, nbc, chunk, 0, unroll=8)

    rev = lambda i: nblk - 1 - i
    qspec = lambda off: pl.BlockSpec((tb, PAIR_QK), lambda p, i: (rev(i), off // PAIR_QK + p))
    pair_qk = pl.BlockSpec((tb, PAIR_QK), lambda p, i: (rev(i), p))
    pair_v = pl.BlockSpec((tb, PAIR_V), lambda p, i: (rev(i), p))
    sel = _band_selector()
    ride_arrays, ride_gather = ride if ride else ([], [])
    nr = len(ride_arrays)
    grid = (npair, nblk)
    outs = pl.pallas_call(
        _riding(body, 8, 4, ride_gather, grid), grid=grid,
        in_specs=[qspec(O_GQ), qspec(O_GK), pl.BlockSpec((tb, PAIR_V), lambda p, i: (rev(i), O_GV // PAIR_V + p)),
                  pair_qk, pair_v, pl.BlockSpec((1, nbc, PAIR_V, PAIR_QK), lambda p, i: (p, rev(i), 0, 0)),
                  pl.BlockSpec((GLA_SUB * PAIR_QK, LANE), lambda p, i: (0, 0)),
                  pl.BlockSpec((LANE, GLA_SUB * PAIR_QK), lambda p, i: (0, 0))] + [ANY_SPEC] * nr,
        out_specs=[pair_qk, pair_qk, pair_v, pair_qk] + [ANY_SPEC] * nr,
        out_shape=[jax.ShapeDtypeStruct((S, GLA_QK), BF16), jax.ShapeDtypeStruct((S, GLA_QK), BF16),
                   jax.ShapeDtypeStruct((S, GLA_V), BF16), jax.ShapeDtypeStruct((S, GLA_QK), F32)]
        + _exchange_shapes(ride_arrays, ride_gather),
        scratch_shapes=[pltpu.VMEM((PAIR_V, PAIR_QK), F32)] + (_exchange_sems(nr) if nr else []),
        compiler_params=_cp("arbitrary", "arbitrary"), name="gla_bwd")(proj, proj, proj, la, do, states, sel, sel.T, *ride_arrays)
    return outs[0], outs[1], outs[2], outs[3], outs[4:]


def _gla_out(o, proj, gng, ts):
    S = o.shape[0]

    def body(o_ref, gr_ref, g_ref, y_ref):
        for h in range(GLA_HEADS):
            cols = slice(h * GLA_DV, (h + 1) * GLA_DV)
            ov, grv = o_ref[:, cols], gr_ref[:, cols]
            r = lax.rsqrt(jnp.mean(ov * ov, axis=-1, keepdims=True) + EPS)
            y_ref[:, cols] = (ov * r * g_ref[...] * (grv * _sigmoid(grv))).astype(BF16)

    return pl.pallas_call(
        body, grid=(S // ts,),
        in_specs=[pl.BlockSpec((ts, GLA_V), lambda i: (i, 0)), pl.BlockSpec((ts, GLA_V), lambda i: (i, O_GR // GLA_V)),
                  pl.BlockSpec((1, GLA_DV), lambda i: (0, 0))],
        out_specs=pl.BlockSpec((ts, GLA_V), lambda i: (i, 0)), out_shape=jax.ShapeDtypeStruct((S, GLA_V), BF16),
        compiler_params=_cp("parallel"), name="gla_out_fwd")(o, proj, gng)


def _gla_out_bwd(dmixed, o, proj, gng, ts):
    S = o.shape[0]

    def body(dy_ref, o_ref, gr_ref, g_ref, do_ref, dgr_ref, gg_ref):
        i = pl.program_id(0)
        gsum = jnp.zeros((1, GLA_DV), F32)
        for h in range(GLA_HEADS):
            cols = slice(h * GLA_DV, (h + 1) * GLA_DV)
            ov, grv, dy = o_ref[:, cols], gr_ref[:, cols], dy_ref[:, cols]
            r = lax.rsqrt(jnp.mean(ov * ov, axis=-1, keepdims=True) + EPS)
            oh = ov * r
            sg = _sigmoid(grv)
            silu = grv * sg
            don = dy * silu
            dgr_ref[:, cols] = (dy * (oh * g_ref[...]) * (sg * (1.0 + grv * (1.0 - sg)))).astype(BF16)
            gsum = gsum + jnp.sum(don * oh, axis=0, keepdims=True)
            doh = don * g_ref[...]
            do_ref[:, cols] = r * (doh - oh * jnp.mean(doh * oh, axis=-1, keepdims=True))
        part = jnp.concatenate([gsum, jnp.zeros((7, GLA_DV), F32)], axis=0)

        @pl.when(i == 0)
        def _():
            gg_ref[...] = part

        @pl.when(i > 0)
        def _():
            gg_ref[...] += part

    return pl.pallas_call(
        body, grid=(S // ts,),
        in_specs=[pl.BlockSpec((ts, GLA_V), lambda i: (i, 0)), pl.BlockSpec((ts, GLA_V), lambda i: (i, 0)),
                  pl.BlockSpec((ts, GLA_V), lambda i: (i, O_GR // GLA_V)), pl.BlockSpec((1, GLA_DV), lambda i: (0, 0))],
        out_specs=[pl.BlockSpec((ts, GLA_V), lambda i: (i, 0)), pl.BlockSpec((ts, GLA_V), lambda i: (i, 0)),
                   pl.BlockSpec((8, GLA_DV), lambda i: (0, 0))],
        out_shape=[jax.ShapeDtypeStruct((S, GLA_V), F32), jax.ShapeDtypeStruct((S, GLA_V), BF16),
                   jax.ShapeDtypeStruct((8, GLA_DV), F32)],
        compiler_params=_cp("arbitrary"), name="gla_out_bwd")(dmixed, o, proj, gng)


def _seg_matrix(width, seg, value):
    r = lax.broadcasted_iota(jnp.int32, (width, width), 0) // seg
    c = lax.broadcasted_iota(jnp.int32, (width, width), 1) // seg
    return jnp.where(r == c, value, 0.0).astype(BF16)


def _seg_sum(x, seg_matrix):
    hi = x.astype(BF16)
    lo = (x - hi.astype(F32)).astype(BF16)
    return _dot(hi, seg_matrix, NN) + _dot(lo, seg_matrix, NN)


def _head_norm(proj, qg, kg, ts):
    S = proj.shape[0]
    W = ATTN_DIM

    def body(q_ref, k_ref, qg_ref, kg_ref, qn_ref, kn_ref):
        seg = _seg_matrix(W, ATTN_HD, 1.0 / ATTN_HD)
        for x_ref, g_ref, o_ref, scale in ((q_ref, qg_ref, qn_ref, ATTN_HD ** -0.5), (k_ref, kg_ref, kn_ref, 1.0)):
            xv = x_ref[...]
            ms = _seg_sum(xv * xv, seg)
            o_ref[...] = xv * lax.rsqrt(ms + EPS) * (g_ref[...] * scale)

    blk = lambda off: pl.BlockSpec((ts, W), lambda i: (i, off // W))
    out = pl.BlockSpec((ts, W), lambda i: (i, 0))
    vec = pl.BlockSpec((1, W), lambda i: (0, 0))
    return pl.pallas_call(
        body, grid=(S // ts,), in_specs=[blk(O_AQ), blk(O_AK), vec, vec], out_specs=[out] * 2,
        out_shape=[jax.ShapeDtypeStruct((S, W), F32)] * 2, compiler_params=_cp("parallel"), name="attn_head_norm")(
            proj, proj, qg, kg)


def _slope(head):
    one = jnp.ones((1, 1), jnp.int32)
    return 1.0 / jnp.left_shift(one, one * (head + 1)).astype(F32)


ATTN_GROUP = 4


def _attn_rows(d, g, r):
    start = g * d * ATTN_BLOCK + r
    return pl.ds(start, ATTN_BLOCK) if d == 1 else pl.ds(start, ATTN_BLOCK, stride=d)


def _for_blocks(d, G, fn):
    for g in range(G):
        if d <= ATTN_GROUP:
            for r in range(d):
                fn(g, r)
        else:
            def step(r, carry, g=g):
                fn(g, r)
                return carry
            lax.fori_loop(0, d, step, 0, unroll=ATTN_GROUP)


def _attn_specs(d, S):
    G = max(1, ATTN_GROUP // d)
    edge = d * ATTN_BLOCK
    tq = G * edge
    nb, n_edge = S // tq, S // edge

    def specs(off=0):
        return [pl.BlockSpec((tq, LANE), lambda hp, n: (n, off + hp)),
                pl.BlockSpec((edge, LANE), lambda hp, n: (jnp.maximum(n * G - 1, 0), off + hp)),
                pl.BlockSpec((edge, LANE), lambda hp, n: (jnp.minimum((n + 1) * G, n_edge - 1), off + hp))]

    return G, nb, specs


def _attn_bias(d, hp, first_tile):
    B = ATTN_BLOCK
    iq = lax.broadcasted_iota(jnp.int32, (B, 2 * B), 0)
    ik = lax.broadcasted_iota(jnp.int32, (B, 2 * B), 1)
    rel = iq + B - ik
    window = (rel >= 0) & (rel <= B)
    relf = (d * rel).astype(F32)
    full = [jnp.where(window, -_slope(hp * 2 + h) * relf, NEG) for h in range(2)]
    edge = [jnp.where((ik >= B) | jnp.logical_not(first_tile), b, NEG) for b in full]
    return full, edge


def _attn_bias_t(d, hp, has_next):
    B = ATTN_BLOCK
    ik = lax.broadcasted_iota(jnp.int32, (B, B), 0)
    iq = lax.broadcasted_iota(jnp.int32, (B, B), 1)
    tiles = []
    for nxt in range(2):
        rel = iq - ik + nxt * B
        window = (rel >= 0) & (rel <= B)
        relf = (d * rel).astype(F32)
        tiles.append([jnp.where(window, -_slope(hp * 2 + h) * relf, NEG) for h in range(2)])
    tiles.append([jnp.where(has_next, b, NEG) for b in tiles[1]])
    return tiles


def _attn_fwd(qn, kn, proj, d):
    S, W = qn.shape
    G, nb, specs = _attn_specs(d, S)

    def body(q_ref, kp_ref, kc_ref, vp_ref, vc_ref, o_ref, l_ref):
        hp, n = pl.program_id(0), pl.program_id(1)
        lo = lax.broadcasted_iota(jnp.int32, (1, LANE), 1) < ATTN_HD
        full, edge = _attn_bias(d, hp, n == 0)

        def sub(g, r):
            rows = _attn_rows(d, g, r)
            before = _attn_rows(d, max(g - 1, 0), r)
            kb_ref, vb_ref = (kp_ref, vp_ref) if g == 0 else (kc_ref, vc_ref)
            bias = edge if g == 0 else full
            qv = q_ref[rows, :].astype(BF16)
            kv = jnp.concatenate([kb_ref[before, :], kc_ref[rows, :]], axis=0).astype(BF16)
            vv = jnp.concatenate([vb_ref[before, :], vc_ref[rows, :]], axis=0).astype(BF16)
            outs, lses = [], []
            for h in range(2):
                qm = jnp.where(lo == (h == 0), qv, jnp.zeros_like(qv))
                s = _dot(qm, kv, NT) + bias[h]
                m = jnp.max(s, axis=-1, keepdims=True)
                p = jnp.exp(s - m)
                den = jnp.sum(p, axis=-1, keepdims=True)
                outs.append(_dot(p.astype(BF16), vv, NN) / den)
                lses.append(m + jnp.log(den))
            o_ref[rows, :] = jnp.where(lo, outs[0], outs[1])
            l_ref[rows, :] = jnp.where(lo, lses[0], lses[1])

        _for_blocks(d, G, sub)

    cur, prev, _ = specs()
    vcur, vprev, _ = specs(O_AV // LANE)
    return pl.pallas_call(
        body, grid=(W // LANE, nb), in_specs=[cur, prev, cur, vprev, vcur], out_specs=[cur, cur],
        out_shape=[jax.ShapeDtypeStruct((S, W), F32)] * 2,
        compiler_params=_cp("parallel", "arbitrary"), name=f"attn_fwd_d{d}")(qn, kn, kn, proj, proj)


def _attn_merge(y_gla, os_, ls_, ts):
    S, W = os_[0].shape

    def body(yg, o1, o2, o3, l1, l2, l3, mixed_ref, mixed_t_ref, y_ref, lse_ref):
        a, b, c = l1[...], l2[...], l3[...]
        m = jnp.maximum(jnp.maximum(a, b), c)
        ea, eb, ec = jnp.exp(a - m), jnp.exp(b - m), jnp.exp(c - m)
        tot = ea + eb + ec
        y = (ea * o1[...] + eb * o2[...] + ec * o3[...]) / tot
        y_ref[...] = y
        mixed_ref[:, :W] = yg[...]
        mixed_ref[:, W:] = y.astype(BF16)
        mixed_t_ref[:W, :] = yg[...].astype(F32).T.astype(BF16)
        mixed_t_ref[W:, :] = y.T.astype(BF16)
        lse_ref[...] = m + jnp.log(tot)

    spec = pl.BlockSpec((ts, W), lambda i: (i, 0))
    return pl.pallas_call(
        body, grid=(S // ts,), in_specs=[spec] * 7,
        out_specs=[pl.BlockSpec((ts, 2 * W), lambda i: (i, 0)), _col_spec(2 * W, ts), spec, spec],
        out_shape=[jax.ShapeDtypeStruct((S, 2 * W), BF16), jax.ShapeDtypeStruct((2 * W, S), BF16),
                   jax.ShapeDtypeStruct((S, W), F32), jax.ShapeDtypeStruct((S, W), F32)],
        compiler_params=_cp("parallel"), name="attn_merge")(y_gla, *os_, *ls_)


def _attn_delta(dmixed, y, ts):
    S, W = y.shape

    def body(dy_ref, y_ref, d_ref):
        d_ref[...] = _seg_sum(dy_ref[...] * y_ref[...], _seg_matrix(W, ATTN_HD, 1.0))

    return pl.pallas_call(
        body, grid=(S // ts,), in_specs=[pl.BlockSpec((ts, W), lambda i: (i, 1)), pl.BlockSpec((ts, W), lambda i: (i, 0))],
        out_specs=pl.BlockSpec((ts, W), lambda i: (i, 0)), out_shape=jax.ShapeDtypeStruct((S, W), F32),
        compiler_params=_cp("parallel"), name="attn_delta")(dmixed, y)


def _attn_dq(qn, kn, proj, dmixed, lse, delta, d):
    S, W = qn.shape
    B = ATTN_BLOCK
    G, nb, specs = _attn_specs(d, S)

    def body(q_ref, kp_ref, kc_ref, vp_ref, vc_ref, dy_ref, l_ref, de_ref, dq_ref):
        hp, n = pl.program_id(0), pl.program_id(1)
        lo = lax.broadcasted_iota(jnp.int32, (1, LANE), 1) < ATTN_HD
        full, edge = _attn_bias(d, hp, n == 0)

        def sub(g, r):
            rows = _attn_rows(d, g, r)
            before = _attn_rows(d, max(g - 1, 0), r)
            kb_ref, vb_ref = (kp_ref, vp_ref) if g == 0 else (kc_ref, vc_ref)
            bias = edge if g == 0 else full
            qv, dyv = q_ref[rows, :].astype(BF16), dy_ref[rows, :]
            lv, dev = l_ref[rows, :], de_ref[rows, :]
            kv = jnp.concatenate([kb_ref[before, :], kc_ref[rows, :]], axis=0).astype(BF16)
            vv = jnp.concatenate([vb_ref[before, :], vc_ref[rows, :]], axis=0).astype(BF16)
            outs = []
            for h in range(2):
                sel = lo == (h == 0)
                qm = jnp.where(sel, qv, jnp.zeros_like(qv))
                dym = jnp.where(sel, dyv, 0.0).astype(BF16)
                lse_h = lv[:, h * ATTN_HD:h * ATTN_HD + 1]
                del_h = dev[:, h * ATTN_HD:h * ATTN_HD + 1]
                p = jnp.exp(_dot(qm, kv, NT) + bias[h] - lse_h)
                ds = p * (_dot(dym, vv, NT) - del_h)
                outs.append(_dot(ds.astype(BF16), kv, NN) * (ATTN_HD ** -0.5))
            dq_ref[rows, :] = jnp.where(lo, outs[0], outs[1])

        _for_blocks(d, G, sub)

    cur, prev, _ = specs()
    vcur, vprev, _ = specs(O_AV // LANE)
    dycur, _, _ = specs(W // LANE)
    return pl.pallas_call(
        body, grid=(W // LANE, nb), in_specs=[cur, prev, cur, vprev, vcur, dycur, cur, cur], out_specs=cur,
        out_shape=jax.ShapeDtypeStruct((S, W), F32),
        compiler_params=_cp("parallel", "arbitrary"), name=f"attn_dq_d{d}")(qn, kn, kn, proj, proj, dmixed, lse, delta)


def _attn_dkv(qn, kn, proj, dmixed, lse, delta, d):
    S, W = qn.shape
    B = ATTN_BLOCK
    G, nb, specs = _attn_specs(d, S)

    def body(k_ref, v_ref, qc_ref, qn_ref, dyc_ref, dyn_ref, lc_ref, ln_ref, dec_ref, den_ref, dk_ref, dv_ref):
        hp, n = pl.program_id(0), pl.program_id(1)
        lo = lax.broadcasted_iota(jnp.int32, (1, LANE), 1) < ATTN_HD
        own, inner, outer = _attn_bias_t(d, hp, n + 1 < nb)

        def sub(g, r):
            rows = _attn_rows(d, g, r)
            kv, vv = k_ref[rows, :].astype(BF16), v_ref[rows, :].astype(BF16)
            dk = jnp.zeros((B, LANE), F32)
            dv = jnp.zeros((B, LANE), F32)
            inside = g + 1 < G
            after = _attn_rows(d, g + 1 if inside else 0, r)
            following = (qc_ref, dyc_ref, lc_ref, dec_ref) if inside else (qn_ref, dyn_ref, ln_ref, den_ref)
            for bias, qrows, (q_ref, dy_ref, l_ref, de_ref) in (
                    (own, rows, (qc_ref, dyc_ref, lc_ref, dec_ref)), (inner if inside else outer, after, following)):
                qv, dyv = q_ref[qrows, :].astype(BF16), dy_ref[qrows, :]
                lt, det = l_ref[qrows, :].T, de_ref[qrows, :].T
                for h in range(2):
                    sel = lo == (h == 0)
                    qm = jnp.where(sel, qv, jnp.zeros_like(qv))
                    dym = jnp.where(sel, dyv, 0.0).astype(BF16)
                    lse_h = lt[h * ATTN_HD:h * ATTN_HD + 1, :]
                    del_h = det[h * ATTN_HD:h * ATTN_HD + 1, :]
                    pt = jnp.exp(_dot(kv, qm, NT) + bias[h] - lse_h)
                    dv = dv + _dot(pt.astype(BF16), dym, NN)
                    dst = pt * (_dot(vv, dym, NT) - del_h)
                    dk = dk + _dot(dst.astype(BF16), qm, NN)
            dk_ref[rows, :] = dk
            dv_ref[rows, :] = dv

        _for_blocks(d, G, sub)

    cur, _, nxt = specs()
    vcur, _, _ = specs(O_AV // LANE)
    dycur, _, dynxt = specs(W // LANE)
    return pl.pallas_call(
        body, grid=(W // LANE, nb), in_specs=[cur, vcur, cur, nxt, dycur, dynxt, cur, nxt, cur, nxt], out_specs=[cur, cur],
        out_shape=[jax.ShapeDtypeStruct((S, W), F32)] * 2,
        compiler_params=_cp("parallel", "arbitrary"), name=f"attn_dkv_d{d}")(
            kn, proj, qn, qn, dmixed, dmixed, lse, lse, delta, delta)


def _attn_post(dqs, dks, dvs, proj, qg, kg, ts):
    S = proj.shape[0]
    W = ATTN_DIM

    def body(dq1, dq2, dq3, dk1, dk2, dk3, dv1, dv2, dv3, aq_ref, ak_ref, qg_ref, kg_ref, daq_ref, dak_ref, dav_ref, gg_ref):
        i = pl.program_id(0)
        seg = _seg_matrix(W, ATTN_HD, 1.0 / ATTN_HD)
        gsums = []
        for (d1, d2, d3), x_ref, g_ref, o_ref in (((dq1, dq2, dq3), aq_ref, qg_ref, daq_ref), ((dk1, dk2, dk3), ak_ref, kg_ref, dak_ref)):
            dy = d1[...] + d2[...] + d3[...]
            xv = x_ref[...]
            r = lax.rsqrt(_seg_sum(xv * xv, seg) + EPS)
            xh = xv * r
            dxh = dy * g_ref[...]
            o_ref[...] = (r * (dxh - xh * _seg_sum(dxh * xh, seg))).astype(BF16)
            gsums.append(jnp.sum(dy * xh, axis=0, keepdims=True))
        dav_ref[...] = (dv1[...] + dv2[...] + dv3[...]).astype(BF16)
        part = jnp.concatenate(gsums + [jnp.zeros((6, W), F32)], axis=0)

        @pl.when(i == 0)
        def _():
            gg_ref[...] = part

        @pl.when(i > 0)
        def _():
            gg_ref[...] += part

    row = pl.BlockSpec((ts, W), lambda i: (i, 0))
    blk = lambda off: pl.BlockSpec((ts, W), lambda i: (i, off // W))
    vec = pl.BlockSpec((1, W), lambda i: (0, 0))
    return pl.pallas_call(
        body, grid=(S // ts,), in_specs=[row] * 9 + [blk(O_AQ), blk(O_AK), vec, vec],
        out_specs=[row, row, row, pl.BlockSpec((8, W), lambda i: (0, 0))],
        out_shape=[jax.ShapeDtypeStruct((S, W), BF16)] * 3 + [jax.ShapeDtypeStruct((8, W), F32)],
        compiler_params=_cp("arbitrary"), name="attn_post")(*dqs, *dks, *dvs, proj, proj, qg, kg)


def _shift_down(cur, halo, n):
    return pltpu.roll(jnp.concatenate([halo, cur], axis=0), n, 0)[8:]


def _shift_up(cur, halo, n):
    ts = cur.shape[0]
    return pltpu.roll(jnp.concatenate([cur, halo], axis=0), ts + 8 - n, 0)[:ts]


def _conv(cur, halo, w, b):
    return b + w[0:1, :] * _shift_down(cur, halo, 2) + w[1:2, :] * _shift_down(cur, halo, 1) + w[2:3, :] * cur


def _conv_swiglu(u0, conv_w8, conv_b, ts, tc):
    S, F2 = u0.shape
    F = F2 // 2
    nc = F // tc
    hb = ts // 8

    def body(ug_ref, ugh_ref, uv_ref, uvh_ref, wg_ref, wv_ref, bg_ref, bv_ref, a_ref, at_ref):
        first = pl.program_id(0) == 0
        ugh = jnp.where(first, 0.0, ugh_ref[...])
        uvh = jnp.where(first, 0.0, uvh_ref[...])
        g = _conv(ug_ref[...], ugh, wg_ref[...], bg_ref[...])
        v = _conv(uv_ref[...], uvh, wv_ref[...], bv_ref[...])
        a = g * _sigmoid(g) * v
        a_ref[...] = a.astype(BF16)
        at_ref[...] = a.T.astype(BF16)

    main = lambda off: pl.BlockSpec((ts, tc), lambda i, j: (i, j + off))
    halo = lambda off: pl.BlockSpec((8, tc), lambda i, j: (jnp.maximum(i * hb - 1, 0), j + off))
    wspec = lambda off: pl.BlockSpec((8, tc), lambda i, j: (0, j + off))
    bspec = lambda off: pl.BlockSpec((1, tc), lambda i, j: (0, j + off))
    return pl.pallas_call(
        body, grid=(S // ts, nc),
        in_specs=[main(0), halo(0), main(nc), halo(nc), wspec(0), wspec(nc), bspec(0), bspec(nc)],
        out_specs=[pl.BlockSpec((ts, tc), lambda i, j: (i, j)), pl.BlockSpec((tc, ts), lambda i, j: (j, i))],
        out_shape=[jax.ShapeDtypeStruct((S, F), BF16), jax.ShapeDtypeStruct((F, S), BF16)],
        compiler_params=_cp("parallel", "parallel"), name="conv_swiglu")(u0, u0, u0, u0, conv_w8, conv_w8, conv_b, conv_b)


def _ffn_du(da, u0, conv_w8, conv_b, ts, tc, ride=None):
    S, F2 = u0.shape
    F = F2 // 2
    nc = F // tc
    hb = ts // 8
    grid = (nc, S // ts)
    ride_arrays, ride_gather = ride if ride else ([], [])
    nr = len(ride_arrays)

    def body(da_ref, ug_ref, ugh_ref, uv_ref, uvh_ref, wg_ref, wv_ref, bg_ref, bv_ref, du_ref, sg_ref, sv_ref):
        i = pl.program_id(1)
        first = i == 0
        halves = []
        for u_ref, h_ref, w_ref, b_ref in ((ug_ref, ugh_ref, wg_ref, bg_ref), (uv_ref, uvh_ref, wv_ref, bv_ref)):
            u, halo, w = u_ref[...], jnp.where(first, 0.0, h_ref[...]), w_ref[...]
            s2, s1 = _shift_down(u, halo, 2), _shift_down(u, halo, 1)
            halves.append((b_ref[...] + w[0:1, :] * s2 + w[1:2, :] * s1 + w[2:3, :] * u, s2, s1, u))
        g, v = halves[0][0], halves[1][0]
        dav = da_ref[...]
        sig = _sigmoid(g)
        dus = (dav * v * (sig * (1.0 + g * (1.0 - sig))), dav * (g * sig))
        for h, (du, sums_ref) in enumerate(zip(dus, (sg_ref, sv_ref))):
            du_ref[h] = du
            _, s2, s1, u = halves[h]
            part = jnp.concatenate([jnp.sum(du * s2, axis=0, keepdims=True), jnp.sum(du * s1, axis=0, keepdims=True),
                                    jnp.sum(du * u, axis=0, keepdims=True), jnp.sum(du, axis=0, keepdims=True),
                                    jnp.zeros((4, tc), F32)], axis=0)

            @pl.when(first)
            def _(sums_ref=sums_ref, part=part):
                sums_ref[...] = part

            @pl.when(i > 0)
            def _(sums_ref=sums_ref, part=part):
                sums_ref[...] += part

    main = lambda off: pl.BlockSpec((ts, tc), lambda j, i: (i, j + off))
    halo = lambda off: pl.BlockSpec((8, tc), lambda j, i: (jnp.maximum(i * hb - 1, 0), j + off))
    wspec = lambda off: pl.BlockSpec((8, tc), lambda j, i: (0, j + off))
    bspec = lambda off: pl.BlockSpec((1, tc), lambda j, i: (0, j + off))
    sums_spec = pl.BlockSpec((8, tc), lambda j, i: (0, j))
    outs = pl.pallas_call(
        _riding(body, 9, 3, ride_gather, grid), grid=grid,
        in_specs=[main(0), main(0), halo(0), main(nc), halo(nc), wspec(0), wspec(nc), bspec(0), bspec(nc)] + [ANY_SPEC] * nr,
        out_specs=[pl.BlockSpec((2, ts, tc), lambda j, i: (0, i, j)), sums_spec, sums_spec] + [ANY_SPEC] * nr,
        out_shape=[jax.ShapeDtypeStruct((2, S, F), F32), jax.ShapeDtypeStruct((8, F), F32), jax.ShapeDtypeStruct((8, F), F32)]
        + _exchange_shapes(ride_arrays, ride_gather),
        scratch_shapes=_exchange_sems(nr) if nr else [],
        compiler_params=_cp("arbitrary", "arbitrary"), name="ffn_du")(
            da, u0, u0, u0, u0, conv_w8, conv_w8, conv_b, conv_b, *ride_arrays)
    return outs[0], outs[1], outs[2], outs[3:]


def _ffn_du0(du, conv_w8, ts, tc):
    _, S, F = du.shape
    nc = F // tc
    hb = ts // 8
    nrow = S // ts

    def body(du_ref, duh_ref, w_ref, o_ref):
        last = pl.program_id(0) == nrow - 1
        cur, halo, w = du_ref[...], jnp.where(last, 0.0, duh_ref[...]), w_ref[...]
        o_ref[...] = (w[2:3, :] * cur + w[1:2, :] * _shift_up(cur, halo, 1) + w[0:1, :] * _shift_up(cur, halo, 2)).astype(BF16)

    return pl.pallas_call(
        body, grid=(nrow, 2, nc),
        in_specs=[pl.BlockSpec((None, ts, tc), lambda i, h, j: (h, i, j)),
                  pl.BlockSpec((None, 8, tc), lambda i, h, j: (h, jnp.minimum((i + 1) * hb, S // 8 - 1), j)),
                  pl.BlockSpec((8, tc), lambda i, h, j: (0, h * nc + j))],
        out_specs=pl.BlockSpec((ts, tc), lambda i, h, j: (i, h * nc + j)), out_shape=jax.ShapeDtypeStruct((S, 2 * F), BF16),
        compiler_params=_cp("parallel", "parallel", "parallel"), name="ffn_du0")(du, du, conv_w8)


def _loss_resid(x2, t2, g2, target, ts):
    S, D = x2.shape

    def body(x_ref, t_ref, g_ref, y_ref, dx_ref, dt_ref, sums_ref):
        i = pl.program_id(0)
        tv, gv = t_ref[...], g_ref[...]
        e = x_ref[...] + gv * tv - y_ref[...]
        dx = e * (1.0 / D)
        dx_ref[...] = dx
        dt_ref[...] = (dx * gv).astype(BF16)
        part = jnp.concatenate([jnp.sum(e * e, axis=0, keepdims=True), jnp.sum(dx * tv, axis=0, keepdims=True),
                                jnp.zeros((6, D), F32)], axis=0)

        @pl.when(i == 0)
        def _():
            sums_ref[...] = part

        @pl.when(i > 0)
        def _():
            sums_ref[...] += part

    row, vec = _row_spec(ts, D), _vec_spec(D)
    return pl.pallas_call(
        body, grid=(S // ts,), in_specs=[row, row, vec, row], out_specs=[row, row, pl.BlockSpec((8, D), lambda i: (0, 0))],
        out_shape=[jax.ShapeDtypeStruct((S, D), F32), jax.ShapeDtypeStruct((S, D), BF16), jax.ShapeDtypeStruct((8, D), F32)],
        compiler_params=_cp("arbitrary"), name="loss_resid")(x2, t2, g2, target)


def _adamw(w, g, m, v, name):
    shape = w.shape
    n = math.prod(shape)
    view = (n // LANE, LANE) if n % LANE == 0 else (math.prod(shape[:-1]), shape[-1])
    R, C = view
    tr = R
    for cand in (1024, 512, 256):
        if R > cand and R % cand == 0:
            tr = cand
            break

    def body(w_ref, g_ref, m_ref, v_ref, d_ref, nm_ref, nv_ref):
        gv = g_ref[...]
        nm = ADAM_B1 * m_ref[...] + (1.0 - ADAM_B1) * gv
        nv = ADAM_B2 * v_ref[...] + (1.0 - ADAM_B2) * (gv * gv)
        m_hat = nm / (1.0 - ADAM_B1 ** ADAM_STEP)
        v_hat = nv / (1.0 - ADAM_B2 ** ADAM_STEP)
        d_ref[...] = -ADAM_LR * (m_hat / (jnp.sqrt(v_hat) + ADAM_EPS) + ADAM_WD * w_ref[...])
        nm_ref[...] = nm
        nv_ref[...] = nv

    spec = pl.BlockSpec((tr, C), lambda i: (i, 0))
    outs = pl.pallas_call(
        body, grid=(R // tr,), in_specs=[spec] * 4, out_specs=[spec] * 3, out_shape=[jax.ShapeDtypeStruct(view, F32)] * 3,
        compiler_params=_cp("parallel"), name=name)(*[a.reshape(view) for a in (w, g, m, v)])
    return [o.reshape(shape) for o in outs]


def _pad_rows8(a):
    return jnp.concatenate([a, jnp.zeros((8 - a.shape[0], a.shape[1]), a.dtype)], axis=0)


def _local_step(x, target, mod, n1g, w_in_p, wg_p, bg, gng, qng, kng, w_out_s, n2g, w_up_s, conv_w, conv_b, w_down_s):
    S, D = x.shape
    F = w_down_s.shape[0] * N_DEV
    ts = min(512, S)
    sh1, sc1, g1, sh2, sc2, g2 = [mod[i:i + 1] for i in range(6)]
    conv_w8 = _pad_rows8(conv_w)
    qg_t, kg_t = jnp.tile(qng, (1, ATTN_HEADS)), jnp.tile(kng, (1, ATTN_HEADS))

    h1, h1_t = _rms_mod(x, n1g, sc1, sh1, ts, "rms_mod1")
    proj, (g_out,) = _mm(h1, w_in_p, NN, 512, PROJ_W, 1024, F32, "mm_in", ride=([w_out_s], [True]))
    w_out = g_out.reshape(-1, D)
    la = _gate_fwd(proj, wg_p, bg, ts)
    o_gla, states, (g_up,) = _gla_fwd(proj, la, 512, ride=([w_up_s], [True]))
    w_up = _cols_from_blocks(g_up)
    y_gla = _gla_out(o_gla, proj, gng, ts)
    qn, kn = _head_norm(proj, qg_t, kg_t, ts)
    branches = [_attn_fwd(qn, kn, proj, d) for d in DILATIONS]
    mixed, mixed_t, y_att, lse = _attn_merge(y_gla, [b[0] for b in branches], [b[1] for b in branches], ts)
    t1 = _mm(mixed, w_out, NN, 512, 1024, 1024, F32, "mm_out")
    x2, h2, h2_t = _resid_rms_mod(x, t1, g1, n2g, sc2, sh2, ts, "resid_rms_mod2")
    u0, (g_down,) = _mm(h2, w_up, NN, 512, 2816, 1024, F32, "mm_up", ride=([w_down_s], [True]))
    w_down = g_down.reshape(F, D)
    tc = 1408 if F % 1408 == 0 else F
    a, a_t = _conv_swiglu(u0, conv_w8, conv_b, min(256, S), tc)
    t2 = _mm(a, w_down, NN, 512, 1024, F, F32, "mm_down")
    dx3, dt2, sums3 = _loss_resid(x2, t2, g2, target, ts)
    loss_row, dg2 = sums3[0:1], sums3[1:2]

    g_w_down = _mm(a_t, dt2, NN, 1408, 1024, 2048, F32, "mm_gw_down")
    da = _mm(dt2, w_down, NT, 512, 2816, 1024, F32, "mm_da")
    du, sums_g, sums_v, (r_down,) = _ffn_du(da, u0, conv_w8, conv_b, min(256, S), tc,
                                            ride=([g_w_down.reshape(N_DEV, -1, D)], [False]))
    g_conv_w = jnp.concatenate([sums_g[0:3], sums_v[0:3]], axis=1)
    g_conv_b = jnp.concatenate([sums_g[3:4], sums_v[3:4]], axis=1)
    du0 = _ffn_du0(du, conv_w8, min(256, S), tc)
    g_w_up = _mm(h2_t, du0, NN, 512, 2816, 2048, F32, "mm_gw_up")
    dh2 = _mm(du0, w_up, NT, 512, 1024, 2816, F32, "mm_dh2")
    dx2, sums2, dt1 = _rms_mod_bwd(dh2, x2, dx3, n2g, sc2, ts, "rms_mod_bwd2", t_prev=t1, g_prev=g1)
    dsh2, dsc2, g_n2g, dg1 = sums2[0:1], sums2[1:2], sums2[2:3], sums2[3:4]
    g_w_out = _mm(mixed_t, dt1, NN, 1024, 1024, 2048, F32, "mm_gw_out")
    dmixed = _mm(dt1, w_out, NT, 512, 1024, 1024, F32, "mm_dmixed")
    do_gla, dgr, gng_sums = _gla_out_bwd(dmixed, o_gla, proj, gng, ts)
    dgq, dgk, dgv, dla, (r_up, r_out) = _gla_bwd(
        proj, la, do_gla, states, 512, ride=([_col_blocks(g_w_up), g_w_out.reshape(N_DEV, -1, D)], [False, False]))
    dglr, g_wg_p, gb_sums = _gate_bwd(dla, la, proj, wg_p, ts)
    delta = _attn_delta(dmixed, y_att, ts)
    dqs = [_attn_dq(qn, kn, proj, dmixed, lse, delta, d) for d in DILATIONS]
    dkvs = [_attn_dkv(qn, kn, proj, dmixed, lse, delta, d) for d in DILATIONS]
    daq, dak, dav, qk_sums = _attn_post(dqs, [t[0] for t in dkvs], [t[1] for t in dkvs], proj, qg_t, kg_t, ts)
    dproj = jnp.concatenate([dgq, dgk, dgv, dgr, daq, dak, dav, dglr, jnp.zeros((S, PROJ_W - O_GLR - LANE), BF16)], axis=1)
    g_w_in_p = _mm(h1_t, dproj, NN, 512, PROJ_W, 1024, F32, "mm_gw_in")
    g_w_in = jnp.concatenate([g_w_in_p[:, :GLR_SRC], g_w_in_p[:, O_GLR:O_GLR + GLA_RANK], g_w_in_p[:, GLR_SRC:O_GLR]], axis=1)
    dh1, (r_in,) = _mm(dproj, w_in_p, NT, 512, 1024, PROJ_W, F32, "mm_dh1", ride=([_col_blocks(g_w_in).astype(BF16)], [False]))
    dx, sums1 = _rms_mod_bwd(dh1, x, dx2, n1g, sc1, ts, "rms_mod_bwd1")
    dsh1, dsc1, g_n1g = sums1[0:1], sums1[1:2], sums1[2:3]

    dmod = jnp.concatenate([dsh1, dsc1, dg1, dsh2, dsc2, dg2], axis=1)
    grads = dict(n1g=g_n1g, w_in=r_in, wg=g_wg_p[:GLA_RANK], bg=gb_sums[0:1], gng=gng_sums[0:1],
                 qng_lanes=qk_sums[0:1], kng_lanes=qk_sums[1:2], w_out=r_out, n2g=g_n2g, w_up=r_up,
                 conv_w=g_conv_w, conv_b=g_conv_b, w_down=r_down)
    return loss_row, dx, dmod, grads


def _col_blocks(a):
    R, W = a.shape
    return a.reshape(R, N_DEV, W // N_DEV).transpose(1, 0, 2)


def _cols_from_blocks(a):
    n, R, C = a.shape
    return a.transpose(1, 0, 2).reshape(R, n * C)


def kernel(x, c, w_ada, b_ada, norm1_g, w_in, gla_w_gate, gla_b_gate, gla_norm_g, q_norm_g, k_norm_g, w_out, norm2_g, w_up, conv_w, conv_b, w_down, loss_target, m_w_ada, m_b_ada, m_norm1_g, m_w_in, m_gla_w_gate, m_gla_b_gate, m_gla_norm_g, m_q_norm_g, m_k_norm_g, m_w_out, m_norm2_g, m_w_up, m_conv_w, m_conv_b, m_w_down, v_w_ada, v_b_ada, v_norm1_g, v_w_in, v_gla_w_gate, v_gla_b_gate, v_gla_norm_g, v_q_norm_g, v_k_norm_g, v_w_out, v_norm2_g, v_w_up, v_conv_w, v_conv_b, v_w_down):
    axes = ("x", "y", "c")
    me = 4 * lax.axis_index("x") + 2 * lax.axis_index("y") + lax.axis_index("c")
    S, D = x.shape[1], x.shape[2]
    x2d, tgt2d = x[0], loss_target[0]
    w_in_s, w_out_s, w_up_s, w_down_s, w_ada_s = w_in[0], w_out[0], w_up[0], w_down[0], w_ada[0]
    conv_w_s, wg_s = conv_w[0], gla_w_gate[0]
    in_c, up_c, ada_c, wg_c, cw_c = w_in_s.shape[1], w_up_s.shape[1], w_ada_s.shape[1], wg_s.shape[1], conv_w_s.shape[1]
    F = w_down_s.shape[0] * N_DEV

    small = jnp.concatenate([conv_w_s.reshape(1, -1), wg_s.reshape(1, -1)], axis=1)
    n_small = small.shape[1]
    small = jnp.pad(small, ((0, 0), (0, -n_small % LANE)))
    g_c, g_in, g_small = _exchange([c, w_in_s.astype(BF16), small], [True] * 3, "gather_w_in")
    c_all = g_c.reshape(N_DEV, D)
    w_in_full = _cols_from_blocks(g_in)
    w_in_p = jnp.concatenate([w_in_full[:, :GLR_SRC], w_in_full[:, GLR_SRC + GLA_RANK:],
                              w_in_full[:, GLR_SRC:GLR_SRC + GLA_RANK], jnp.zeros((D, PROJ_W - O_GLR - GLA_RANK), BF16)], axis=1)
    g_small = g_small.reshape(N_DEV, -1)
    conv_w_full = _cols_from_blocks(g_small[:, :3 * cw_c].reshape(N_DEV, 3, cw_c))
    wg_full = _cols_from_blocks(g_small[:, 3 * cw_c:n_small].reshape(N_DEV, GLA_RANK, wg_c))
    wg_p = jnp.concatenate([wg_full, jnp.zeros((LANE - GLA_RANK, wg_full.shape[1]), F32)], axis=0)

    b_shard = lax.dynamic_slice(b_ada, (0, me * ada_c), (1, ada_c))
    mod_part = _ada_fwd(c_all, w_ada_s, b_shard)
    mod_recv, = _exchange([mod_part.reshape(N_DEV, 1, ada_c)], [False], "exchange_mod")
    mod = mod_recv.reshape(6, D)

    loss_row, dx, dmod, gr = _local_step(
        x2d, tgt2d, mod, norm1_g, w_in_p, wg_p, gla_b_gate, gla_norm_g, q_norm_g, k_norm_g,
        w_out_s.astype(BF16), norm2_g, w_up_s.astype(BF16), conv_w_full, conv_b, w_down_s.astype(BF16))
    loss = lax.psum(0.5 / D * jnp.sum(loss_row), axes)

    parts = [dmod, gr["n1g"], gr["bg"], gr["gng"], gr["qng_lanes"], gr["kng_lanes"], gr["n2g"], gr["conv_b"],
             gr["wg"].reshape(1, -1), gr["conv_w"].reshape(1, -1)]
    sizes = [p.shape[1] for p in parts]
    packed = jnp.concatenate(parts, axis=1)
    packed = jnp.pad(packed, ((0, 0), (0, -packed.shape[1] % (8 * LANE))))
    gathered, = _exchange([packed.reshape(8, -1)], [True], "gather_small_grads")
    gathered = gathered.reshape(N_DEV, -1)
    total = _sum_slots(gathered.reshape(N_DEV, 8, -1), "sum_small_grads").reshape(1, -1)
    offs = [0]
    for s_ in sizes:
        offs.append(offs[-1] + s_)
    t_dmod, t_n1g, t_bg, t_gng, t_qng, t_kng, t_n2g, t_conv_b, t_wg, t_conv_w = [
        total[:, offs[i]:offs[i + 1]] for i in range(len(sizes))]
    g_b_ada = t_dmod
    g_qng = t_qng.reshape(ATTN_HEADS, ATTN_HD).sum(axis=0, keepdims=True)
    g_kng = t_kng.reshape(ATTN_HEADS, ATTN_HD).sum(axis=0, keepdims=True)
    g_wg = lax.dynamic_slice(t_wg.reshape(GLA_RANK, -1), (0, me * wg_c), (GLA_RANK, wg_c))
    g_conv_w = lax.dynamic_slice(t_conv_w.reshape(3, -1), (0, me * cw_c), (3, cw_c))
    dmod_shard = lax.dynamic_slice(gathered[:, :6 * D], (0, me * ada_c), (N_DEV, ada_c))
    g_w_ada = _ada_bwd(c_all, dmod_shard)

    g_w_in = _sum_slots(gr["w_in"], "sum_gw_in")
    g_w_out = _sum_slots(gr["w_out"], "sum_gw_out")
    g_w_up = _sum_slots(gr["w_up"], "sum_gw_up")
    g_w_down = _sum_slots(gr["w_down"], "sum_gw_down")

    names = ["w_ada", "b_ada", "norm1_g", "w_in", "gla_w_gate", "gla_b_gate", "gla_norm_g", "q_norm_g", "k_norm_g",
             "w_out", "norm2_g", "w_up", "conv_w", "conv_b", "w_down"]
    ws = [w_ada, b_ada, norm1_g, w_in, gla_w_gate, gla_b_gate, gla_norm_g, q_norm_g, k_norm_g, w_out, norm2_g, w_up, conv_w, conv_b, w_down]
    ms = [m_w_ada, m_b_ada, m_norm1_g, m_w_in, m_gla_w_gate, m_gla_b_gate, m_gla_norm_g, m_q_norm_g, m_k_norm_g, m_w_out, m_norm2_g, m_w_up, m_conv_w, m_conv_b, m_w_down]
    vs = [v_w_ada, v_b_ada, v_norm1_g, v_w_in, v_gla_w_gate, v_gla_b_gate, v_gla_norm_g, v_q_norm_g, v_k_norm_g, v_w_out, v_norm2_g, v_w_up, v_conv_w, v_conv_b, v_w_down]
    gs = [g_w_ada, g_b_ada, t_n1g, g_w_in, g_wg, t_bg, t_gng, g_qng, g_kng, g_w_out, t_n2g, g_w_up, g_conv_w, t_conv_b, g_w_down]
    gs = [g.reshape(w.shape) for g, w in zip(gs, ws)]
    deltas, new_ms, new_vs = [], [], []
    for nm, w, g, m, v in zip(names, ws, gs, ms, vs):
        d_, m_, v_ = _adamw(w, g, m, v, "adamw_" + nm)
        deltas.append(d_)
        new_ms.append(m_)
        new_vs.append(v_)
    return (loss, dx.reshape(x.shape), *gs, *deltas, *new_ms, *new_vs)
```

```python
import functools
import math

import jax
import jax.numpy as jnp
from jax import lax
from jax.experimental import pallas as pl
from jax.experimental.pallas import tpu as pltpu

F32, BF16 = jnp.float32, jnp.bfloat16
HI = lax.Precision.HIGHEST
EPS = 1e-6
NEG = -1e30

N_DEV = 8
GLA_HEADS, GLA_DK, GLA_DV, GLA_RANK, GLA_TAU, GLA_CHUNK = 4, 64, 128, 16, 16.0, 64
ATTN_HEADS, ATTN_HD, ATTN_BLOCK = 8, 64, 128
DILATIONS = (1, 4, 16)
GLA_QK, GLA_V, ATTN_DIM = GLA_HEADS * GLA_DK, GLA_HEADS * GLA_DV, ATTN_HEADS * ATTN_HD
O_GQ, O_GK, O_GV, O_GR, O_AQ, O_AK, O_AV, O_GLR = 0, 256, 512, 1024, 1536, 2048, 2560, 3072
PROJ_W = 3328
LANE = 128
GLR_SRC = 2 * GLA_QK + 2 * GLA_V

ADAM_LR, ADAM_B1, ADAM_B2, ADAM_EPS, ADAM_WD, ADAM_STEP = 0.001, 0.9, 0.999, 1e-08, 0.01, 10

VMEM_LIMIT = 56 * 1024 * 1024
SUM_BLOCK_ELEMS = 256 * 1024


def _cp(*sem):
    return pltpu.CompilerParams(dimension_semantics=sem, vmem_limit_bytes=VMEM_LIMIT)


def _dot(a, b, dims, precision=None):
    return lax.dot_general(a, b, (dims, ((), ())), preferred_element_type=F32, precision=precision)


NN, NT, TN = ((1,), (0,)), ((1,), (1,)), ((0,), (0,))


def _sigmoid(z):
    return 1.0 / (1.0 + jnp.exp(-z))


ANY_SPEC = pl.BlockSpec(memory_space=pl.ANY)


def _exchange_shapes(arrays, gather):
    return [jax.ShapeDtypeStruct((N_DEV,) + (a.shape if g else a.shape[1:]), a.dtype) for a, g in zip(arrays, gather)]


def _exchange_sems(n):
    return [pltpu.SemaphoreType.DMA((n * (N_DEV - 1),)), pltpu.SemaphoreType.DMA((n * (N_DEV - 1),)), pltpu.SemaphoreType.DMA((n,))]


def _exchange_copies(ins, outs, gather, send_sems, recv_sems, local_sems):
    x, y, c = lax.axis_index("x"), lax.axis_index("y"), lax.axis_index("c")
    me = 4 * x + 2 * y + c
    copies = []
    for a in range(len(ins)):
        for p in range(1, N_DEV):
            px, py, pc = x ^ (p >> 2), y ^ ((p >> 1) & 1), c ^ (p & 1)
            peer = 4 * px + 2 * py + pc
            k = a * (N_DEV - 1) + p - 1
            copies.append(pltpu.make_async_remote_copy(
                src_ref=ins[a] if gather[a] else ins[a].at[peer], dst_ref=outs[a].at[me],
                send_sem=send_sems.at[k], recv_sem=recv_sems.at[k],
                device_id=(px, py, pc), device_id_type=pl.DeviceIdType.MESH))
        copies.append(pltpu.make_async_copy(ins[a] if gather[a] else ins[a].at[me], outs[a].at[me], local_sems.at[a]))
    return copies


def _riding(body, n_in, n_out, gather, grid):
    nr = len(gather)
    if not nr:
        return body

    def wrapped(*refs):
        ins, r_ins = refs[:n_in], refs[n_in:n_in + nr]
        outs, r_outs = refs[n_in + nr:n_in + nr + n_out], refs[n_in + nr + n_out:n_in + 2 * nr + n_out]
        scratch = refs[n_in + 2 * nr + n_out:]
        first = last = None
        for t, steps in enumerate(grid):
            pid = pl.program_id(t)
            first = (pid == 0) if first is None else first & (pid == 0)
            last = (pid == steps - 1) if last is None else last & (pid == steps - 1)
        copies = _exchange_copies(r_ins, r_outs, gather, *scratch[-3:])

        @pl.when(first)
        def _():
            for cp in copies:
                cp.start()

        body(*ins, *outs, *scratch[:-3])

        @pl.when(last)
        def _():
            for cp in copies:
                cp.wait()

    return wrapped


def _exchange(arrays, gather, name):
    n = len(arrays)

    def body(*refs):
        copies = _exchange_copies(refs[:n], refs[n:2 * n], gather, *refs[2 * n:])
        for cp in copies:
            cp.start()
        for cp in copies:
            cp.wait()

    return pl.pallas_call(
        body, out_shape=_exchange_shapes(arrays, gather), in_specs=[ANY_SPEC] * n, out_specs=[ANY_SPEC] * n,
        scratch_shapes=_exchange_sems(n), name=name)(*arrays)


def _sum_slots(x, name):
    _, R, C = x.shape
    tr = max(t for t in range(8, min(SUM_BLOCK_ELEMS // C, R) + 1, 8) if R % t == 0)

    def body(x_ref, o_ref):
        acc = x_ref[0].astype(F32)
        for s in range(1, N_DEV):
            acc = acc + x_ref[s].astype(F32)
        o_ref[...] = acc

    return pl.pallas_call(
        body, grid=(R // tr,), in_specs=[pl.BlockSpec((N_DEV, tr, C), lambda i: (0, i, 0))],
        out_specs=pl.BlockSpec((tr, C), lambda i: (i, 0)), out_shape=jax.ShapeDtypeStruct((R, C), F32),
        compiler_params=_cp("parallel"), name=name)(x)


def _mm(a, b, mode, tm, tn, tk, out_dtype, name, ride=None):
    if mode == NN:
        (M, K), N = a.shape, b.shape[1]
    elif mode == NT:
        (M, K), N = a.shape, b.shape[0]
    else:
        (K, M), N = a.shape, b.shape[1]
    tm, tn, tk = min(tm, M), min(tn, N), min(tk, K)
    assert M % tm == 0 and N % tn == 0 and K % tk == 0, (name, M, N, K, tm, tn, tk)
    nk = K // tk
    if mode == NN:
        a_spec = pl.BlockSpec((tm, tk), lambda i, j, k: (i, k))
        b_spec = pl.BlockSpec((tk, tn), lambda i, j, k: (k, j))
    elif mode == NT:
        a_spec = pl.BlockSpec((tm, tk), lambda i, j, k: (i, k))
        b_spec = pl.BlockSpec((tn, tk), lambda i, j, k: (j, k))
    else:
        a_spec = pl.BlockSpec((tk, tm), lambda i, j, k: (k, i))
        b_spec = pl.BlockSpec((tk, tn), lambda i, j, k: (k, j))

    ride_arrays, ride_gather = ride if ride else ([], [])
    nr = len(ride_arrays)
    grid = (M // tm, N // tn, nk)

    own_acc = nk > 1 and out_dtype != F32

    def body(a_ref, b_ref, o_ref, *acc):
        p = _dot(a_ref[...].astype(BF16), b_ref[...].astype(BF16), mode)
        if nk == 1:
            o_ref[...] = p.astype(out_dtype)
        else:
            acc_ref = acc[0] if own_acc else o_ref
            k = pl.program_id(2)

            @pl.when(k == 0)
            def _():
                acc_ref[...] = p

            @pl.when(k > 0)
            def _():
                acc_ref[...] += p

            if own_acc:
                @pl.when(k == nk - 1)
                def _():
                    o_ref[...] = acc_ref[...].astype(out_dtype)

    outs = pl.pallas_call(
        _riding(body, 2, 1, ride_gather, grid), grid=grid, in_specs=[a_spec, b_spec] + [ANY_SPEC] * nr,
        out_specs=[pl.BlockSpec((tm, tn), lambda i, j, k: (i, j))] + [ANY_SPEC] * nr,
        out_shape=[jax.ShapeDtypeStruct((M, N), out_dtype)] + _exchange_shapes(ride_arrays, ride_gather),
        scratch_shapes=([pltpu.VMEM((tm, tn), F32)] if own_acc else []) + (_exchange_sems(nr) if nr else []),
        compiler_params=_cp(*(("arbitrary",) * 3 if nr else ("parallel", "parallel", "arbitrary"))), name=name)(a, b, *ride_arrays)
    return (outs[0], outs[1:]) if nr else outs[0]


def _ada_fwd(c_all, w_shard, b_shard):
    Nc = w_shard.shape[1]

    def body(c_ref, w_ref, b_ref, o_ref):
        cv = c_ref[...]
        o_ref[...] = _dot(cv * _sigmoid(cv), w_ref[...], NN, HI) + b_ref[...]

    return pl.pallas_call(body, out_shape=jax.ShapeDtypeStruct((N_DEV, Nc), F32), name="ada_fwd",
                          compiler_params=pltpu.CompilerParams(vmem_limit_bytes=VMEM_LIMIT))(c_all, w_shard, b_shard)


def _ada_bwd(c_all, dmod_shard):
    D, Nc = c_all.shape[1], dmod_shard.shape[1]

    def body(c_ref, d_ref, o_ref):
        cv = c_ref[...]
        o_ref[...] = _dot(cv * _sigmoid(cv), d_ref[...], TN, HI)

    return pl.pallas_call(body, out_shape=jax.ShapeDtypeStruct((D, Nc), F32), name="ada_bwd",
                          compiler_params=pltpu.CompilerParams(vmem_limit_bytes=VMEM_LIMIT))(c_all, dmod_shard)


def _row_spec(ts, D):
    return pl.BlockSpec((ts, D), lambda i: (i, 0))


def _vec_spec(D):
    return pl.BlockSpec((1, D), lambda i: (0, 0))


def _col_spec(D, ts):
    return pl.BlockSpec((D, ts), lambda i: (0, i))


def _rms_mod(x, ng, sc, sh, ts, name):
    S, D = x.shape

    def body(x_ref, ng_ref, sc_ref, sh_ref, h_ref, ht_ref):
        xv = x_ref[...]
        r = lax.rsqrt(jnp.mean(xv * xv, axis=-1, keepdims=True) + EPS)
        h = xv * r * ng_ref[...] * (1.0 + sc_ref[...]) + sh_ref[...]
        h_ref[...] = h.astype(BF16)
        ht_ref[...] = h.T.astype(BF16)

    return pl.pallas_call(
        body, grid=(S // ts,), in_specs=[_row_spec(ts, D)] + [_vec_spec(D)] * 3, out_specs=[_row_spec(ts, D), _col_spec(D, ts)],
        out_shape=[jax.ShapeDtypeStruct((S, D), BF16), jax.ShapeDtypeStruct((D, S), BF16)],
        compiler_params=_cp("parallel"), name=name)(x, ng, sc, sh)


def _resid_rms_mod(x, t, g, ng, sc, sh, ts, name):
    S, D = x.shape

    def body(x_ref, t_ref, g_ref, ng_ref, sc_ref, sh_ref, x2_ref, h_ref, ht_ref):
        xv = x_ref[...] + g_ref[...] * t_ref[...]
        x2_ref[...] = xv
        r = lax.rsqrt(jnp.mean(xv * xv, axis=-1, keepdims=True) + EPS)
        h = xv * r * ng_ref[...] * (1.0 + sc_ref[...]) + sh_ref[...]
        h_ref[...] = h.astype(BF16)
        ht_ref[...] = h.T.astype(BF16)

    return pl.pallas_call(
        body, grid=(S // ts,), in_specs=[_row_spec(ts, D)] * 2 + [_vec_spec(D)] * 4,
        out_specs=[_row_spec(ts, D)] * 2 + [_col_spec(D, ts)],
        out_shape=[jax.ShapeDtypeStruct((S, D), F32), jax.ShapeDtypeStruct((S, D), BF16), jax.ShapeDtypeStruct((D, S), BF16)],
        compiler_params=_cp("parallel"), name=name)(x, t, g, ng, sc, sh)


def _rms_mod_bwd(dh, xin, dres, ng, sc, ts, name, t_prev=None, g_prev=None):
    S, D = xin.shape
    chain = t_prev is not None

    def body(*refs):
        if chain:
            dh_ref, x_ref, dr_ref, ng_ref, sc_ref, t_ref, g_ref, dx_ref, sums_ref, dt_ref = refs
        else:
            dh_ref, x_ref, dr_ref, ng_ref, sc_ref, dx_ref, sums_ref = refs
        i = pl.program_id(0)
        xv, dhv = x_ref[...], dh_ref[...]
        r = lax.rsqrt(jnp.mean(xv * xv, axis=-1, keepdims=True) + EPS)
        xh = xv * r
        ngv, scv = ng_ref[...], sc_ref[...]
        dxh = dhv * (ngv * (1.0 + scv))
        dx = dr_ref[...] + r * (dxh - xh * jnp.mean(dxh * xh, axis=-1, keepdims=True))
        dx_ref[...] = dx
        dhx = dhv * xh
        rows = [jnp.sum(dhv, axis=0, keepdims=True), jnp.sum(dhx * ngv, axis=0, keepdims=True),
                jnp.sum(dhx * (1.0 + scv), axis=0, keepdims=True)]
        if chain:
            dt_ref[...] = (dx * g_ref[...]).astype(BF16)
            rows.append(jnp.sum(dx * t_ref[...], axis=0, keepdims=True))
        rows.append(jnp.zeros((8 - len(rows), D), F32))
        part = jnp.concatenate(rows, axis=0)

        @pl.when(i == 0)
        def _():
            sums_ref[...] = part

        @pl.when(i > 0)
        def _():
            sums_ref[...] += part

    row, vec = _row_spec(ts, D), _vec_spec(D)
    sums_spec = pl.BlockSpec((8, D), lambda i: (0, 0))
    ins = [dh, xin, dres, ng, sc] + ([t_prev, g_prev] if chain else [])
    in_specs = [row, row, row, vec, vec] + ([row, vec] if chain else [])
    out_specs = [row, sums_spec] + ([row] if chain else [])
    out_shape = [jax.ShapeDtypeStruct((S, D), F32), jax.ShapeDtypeStruct((8, D), F32)] + (
        [jax.ShapeDtypeStruct((S, D), BF16)] if chain else [])
    return pl.pallas_call(body, grid=(S // ts,), in_specs=in_specs, out_specs=out_specs, out_shape=out_shape,
                          compiler_params=_cp("arbitrary"), name=name)(*ins)


def _gate_fwd(proj, wg_p, bg, ts):
    S = proj.shape[0]

    def body(glr_ref, w_ref, b_ref, la_ref):
        z = _dot(glr_ref[...], w_ref[...], NN, HI) + b_ref[...]
        la_ref[...] = (jnp.minimum(z, 0.0) - jnp.log(1.0 + jnp.exp(-jnp.abs(z)))) * (1.0 / GLA_TAU)

    return pl.pallas_call(
        body, grid=(S // ts,),
        in_specs=[pl.BlockSpec((ts, LANE), lambda i: (i, O_GLR // LANE)), pl.BlockSpec((LANE, GLA_QK), lambda i: (0, 0)),
                  pl.BlockSpec((1, GLA_QK), lambda i: (0, 0))],
        out_specs=pl.BlockSpec((ts, GLA_QK), lambda i: (i, 0)), out_shape=jax.ShapeDtypeStruct((S, GLA_QK), F32),
        compiler_params=_cp("parallel"), name="gla_gate_fwd")(proj, wg_p, bg)


def _gate_bwd(dla, la, proj, wg_p, ts):
    S = proj.shape[0]

    def body(dla_ref, la_ref, glr_ref, w_ref, dglr_ref, gw_ref, gb_ref):
        i = pl.program_id(0)
        dz = dla_ref[...] * (1.0 / GLA_TAU) * (1.0 - jnp.exp(GLA_TAU * la_ref[...]))
        dglr_ref[...] = _dot(dz, w_ref[...], NT, HI).astype(BF16)
        gw = _dot(glr_ref[...], dz, TN, HI)
        gb = jnp.concatenate([jnp.sum(dz, axis=0, keepdims=True), jnp.zeros((7, GLA_QK), F32)], axis=0)

        @pl.when(i == 0)
        def _():
            gw_ref[...] = gw
            gb_ref[...] = gb

        @pl.when(i > 0)
        def _():
            gw_ref[...] += gw
            gb_ref[...] += gb

    return pl.pallas_call(
        body, grid=(S // ts,),
        in_specs=[pl.BlockSpec((ts, GLA_QK), lambda i: (i, 0)), pl.BlockSpec((ts, GLA_QK), lambda i: (i, 0)),
                  pl.BlockSpec((ts, LANE), lambda i: (i, O_GLR // LANE)), pl.BlockSpec((LANE, GLA_QK), lambda i: (0, 0))],
        out_specs=[pl.BlockSpec((ts, LANE), lambda i: (i, 0)), pl.BlockSpec((LANE, GLA_QK), lambda i: (0, 0)),
                   pl.BlockSpec((8, GLA_QK), lambda i: (0, 0))],
        out_shape=[jax.ShapeDtypeStruct((S, LANE), BF16), jax.ShapeDtypeStruct((LANE, GLA_QK), F32),
                   jax.ShapeDtypeStruct((8, GLA_QK), F32)],
        compiler_params=_cp("arbitrary"), name="gla_gate_bwd")(dla, la, proj, wg_p)


def _tri(lower):
    r = lax.broadcasted_iota(jnp.int32, (GLA_CHUNK, GLA_CHUNK), 0)
    c = lax.broadcasted_iota(jnp.int32, (GLA_CHUNK, GLA_CHUNK), 1)
    return jnp.where((r >= c) if lower else (c >= r), 1.0, 0.0).astype(F32)


GLA_SUB = 16
GLA_NSUB = GLA_CHUNK // GLA_SUB
PAIR_QK = 2 * GLA_DK
PAIR_V = 2 * GLA_DV


def _band_selector():
    r = lax.broadcasted_iota(jnp.int32, (GLA_SUB * PAIR_QK, LANE), 0)
    c = lax.broadcasted_iota(jnp.int32, (GLA_SUB * PAIR_QK, LANE), 1)
    dist, head = r // PAIR_QK, (r % PAIR_QK) // GLA_DK
    return jnp.where(c == head * GLA_DK + (GLA_SUB - 1 - dist), 1.0, 0.0).astype(BF16)


def _flip_matrix():
    r = lax.broadcasted_iota(jnp.int32, (GLA_CHUNK, GLA_CHUNK), 0)
    c = lax.broadcasted_iota(jnp.int32, (GLA_CHUNK, GLA_CHUNK), 1)
    return jnp.where(r + c == GLA_CHUNK - 1, 1.0, 0.0).astype(BF16)


def _state_mask():
    r = lax.broadcasted_iota(jnp.int32, (PAIR_V, PAIR_QK), 0)
    c = lax.broadcasted_iota(jnp.int32, (PAIR_V, PAIR_QK), 1)
    return (r < GLA_DV) == (c < GLA_DK)


class _GlaChunk:
    def __init__(self, qs, kc, vc, g, sel):
        C = GLA_CHUNK
        self.qs, self.kc, self.vc = qs, kc, vc
        rows = lax.broadcasted_iota(jnp.int32, (C, 1), 0)
        lane = lax.broadcasted_iota(jnp.int32, (1, PAIR_QK), 1)
        self.rows, self.lane = rows, lane
        b = _dot(_tri(True), g, NN, HI)
        self.bl = b[C - 1:C, :]
        self.eb = jnp.exp(b)
        self.kdec = jnp.exp(self.bl - b)
        edge = lambda J: b[GLA_SUB * (J + 1):GLA_SUB * (J + 1) + 1, :]
        self.e_far = [jnp.exp(jnp.where(rows >= GLA_SUB * (J + 1), b - edge(J), NEG)) for J in range(GLA_NSUB - 1)]
        blk = rows // GLA_SUB
        bnext = edge(0)
        for J in range(1, GLA_NSUB - 1):
            bnext = jnp.where(blk == J, edge(J), bnext)
        self.e_khat = jnp.exp(jnp.where(blk < GLA_NSUB - 1, bnext - b, NEG))
        khat = kc * self.e_khat
        k2 = jnp.concatenate([jnp.where(lane < GLA_DK, khat, 0.0), jnp.where(lane >= GLA_DK, khat, 0.0)], axis=0)
        self.blk2 = jnp.concatenate([blk, blk], axis=0)
        self.m_far = jnp.concatenate([jnp.where(self.blk2 == J, k2, 0.0) for J in range(GLA_NSUB - 1)], axis=1).astype(BF16)
        self.qcat = jnp.concatenate([qs * e for e in self.e_far], axis=1).astype(BF16)
        a_far = _dot(self.qcat, self.m_far, NT)
        self.e_band, self.rk, terms = [], [], []
        for d in range(GLA_SUB):
            rk = pltpu.roll(kc, d, 0) if d else kc
            rb = pltpu.roll(b, d, 0) if d else b
            e = jnp.exp(jnp.where(rows >= d, b - rb, NEG))
            self.e_band.append(e)
            self.rk.append(rk)
            terms.append((qs * rk * e).astype(BF16))
        band = _dot(jnp.concatenate(terms, axis=1), sel, NN)
        a_band = pltpu.roll(band, LANE - (GLA_SUB - 1), 1, stride=1, stride_axis=0)
        dist = rows - lane % GLA_DK
        self.far_mask = dist >= GLA_SUB
        self.band_mask = (dist >= 0) & (dist < GLA_SUB)
        self.a = (a_band + jnp.where(self.far_mask, a_far, 0.0)).astype(BF16)
        self.lane_v = lax.broadcasted_iota(jnp.int32, (1, PAIR_V), 1)
        self.v2 = jnp.concatenate([jnp.where(self.lane_v < GLA_DV, vc, 0.0), jnp.where(self.lane_v >= GLA_DV, vc, 0.0)],
                                  axis=0).astype(BF16)


def _gla_fwd(proj, la, tb, ride=None):
    S = proj.shape[0]
    C = GLA_CHUNK
    tb = min(tb, S)
    nbc = tb // C
    npair = GLA_HEADS // 2
    scale = GLA_DK ** -0.5

    def body(q_ref, k_ref, v_ref, la_ref, sel_ref, o_ref, st_ref, state):
        @pl.when(pl.program_id(1) == 0)
        def _():
            state[...] = jnp.zeros_like(state)

        def chunk(ci, carry):
            sl = pl.ds(pl.multiple_of(ci * C, C), C)
            ch = _GlaChunk(q_ref[sl, :] * scale, k_ref[sl, :], v_ref[sl, :], la_ref[sl, :], sel_ref[...])
            st = state[...]
            st_ref[0, ci] = st
            o_ref[sl, :] = _dot((ch.qs * ch.eb).astype(BF16), st.astype(BF16), NT) + _dot(ch.a, ch.v2, NN)
            upd = _dot(ch.vc.astype(BF16), (ch.kc * ch.kdec).astype(BF16), TN)
            state[...] = st * jnp.exp(ch.bl) + jnp.where(_state_mask(), upd, 0.0)
            return carry

        lax.fori_loop(0, nbc, chunk, 0, unroll=8)

    qspec = lambda off: pl.BlockSpec((tb, PAIR_QK), lambda p, i: (i, off // PAIR_QK + p))
    ride_arrays, ride_gather = ride if ride else ([], [])
    nr = len(ride_arrays)
    grid = (npair, S // tb)
    outs = pl.pallas_call(
        _riding(body, 5, 2, ride_gather, grid), grid=grid,
        in_specs=[qspec(O_GQ), qspec(O_GK), pl.BlockSpec((tb, PAIR_V), lambda p, i: (i, O_GV // PAIR_V + p)),
                  pl.BlockSpec((tb, PAIR_QK), lambda p, i: (i, p)),
                  pl.BlockSpec((GLA_SUB * PAIR_QK, LANE), lambda p, i: (0, 0))] + [ANY_SPEC] * nr,
        out_specs=[pl.BlockSpec((tb, PAIR_V), lambda p, i: (i, p)),
                   pl.BlockSpec((1, nbc, PAIR_V, PAIR_QK), lambda p, i: (p, i, 0, 0))] + [ANY_SPEC] * nr,
        out_shape=[jax.ShapeDtypeStruct((S, GLA_V), F32), jax.ShapeDtypeStruct((npair, S // C, PAIR_V, PAIR_QK), F32)]
        + _exchange_shapes(ride_arrays, ride_gather),
        scratch_shapes=[pltpu.VMEM((PAIR_V, PAIR_QK), F32)] + (_exchange_sems(nr) if nr else []),
        compiler_params=_cp("arbitrary", "arbitrary"), name="gla_fwd")(proj, proj, proj, la, _band_selector(), *ride_arrays)
    return outs[0], outs[1], outs[2:]


def _gla_bwd(proj, la, do, states, tb, ride=None):
    S = proj.shape[0]
    C = GLA_CHUNK
    tb = min(tb, S)
    nbc = tb // C
    nblk = S // tb
    npair = GLA_HEADS // 2
    scale = GLA_DK ** -0.5

    def body(q_ref, k_ref, v_ref, la_ref, do_ref, st_ref, sel_ref, selt_ref, dq_ref, dk_ref, dv_ref, dla_ref, dstate):
        @pl.when(pl.program_id(1) == 0)
        def _():
            dstate[...] = jnp.zeros_like(dstate)

        def chunk(cc, carry):
            ci = nbc - 1 - cc
            sl = pl.ds(pl.multiple_of(ci * C, C), C)
            ch = _GlaChunk(q_ref[sl, :] * scale, k_ref[sl, :], v_ref[sl, :], la_ref[sl, :], sel_ref[...])
            qs, kc, rows = ch.qs, ch.kc, ch.rows
            doc_b = do_ref[sl, :].astype(BF16)
            st = st_ref[0, ci]
            dst = dstate[...]
            dst_b = dst.astype(BF16)
            ebl = jnp.exp(ch.bl)
            dq = _dot(doc_b, st.astype(BF16), NN) * ch.eb
            dk = _dot(ch.vc.astype(BF16), dst_b, NN) * ch.kdec
            dv = _dot((kc * ch.kdec).astype(BF16), dst_b, NT)
            dbl = jnp.sum(dst * st, axis=0, keepdims=True) * ebl + jnp.sum(kc * dk, axis=0, keepdims=True)
            da = _dot(doc_b, ch.v2, NT)
            dv2 = _dot(ch.a, doc_b, TN)
            dv = dv + jnp.where(ch.lane_v < GLA_DV, dv2[:C], dv2[C:])
            da_far = jnp.where(ch.far_mask, da, 0.0).astype(BF16)
            dqcat = _dot(da_far, ch.m_far, NN)
            dm = _dot(da_far, ch.qcat, TN)
            dk2 = jnp.zeros((2 * C, PAIR_QK), F32)
            for J in range(GLA_NSUB - 1):
                dq = dq + dqcat[:, J * PAIR_QK:(J + 1) * PAIR_QK] * ch.e_far[J]
                dk2 = dk2 + jnp.where(ch.blk2 == J, dm[:, J * PAIR_QK:(J + 1) * PAIR_QK], 0.0)
            dk = dk + jnp.where(ch.lane < GLA_DK, dk2[:C], dk2[C:]) * ch.e_khat
            flip = _flip_matrix()
            da_band = _dot(flip, jnp.where(ch.band_mask, da, 0.0).astype(BF16), NN)
            dband = pltpu.roll(da_band, LANE - (C - GLA_SUB), 1, stride=1, stride_axis=0)
            dband = _dot(flip, dband.astype(BF16), NN)
            dterms = _dot(dband.astype(BF16), selt_ref[...], NN)
            for d in range(GLA_SUB):
                dt = dterms[:, d * PAIR_QK:(d + 1) * PAIR_QK]
                dq = dq + dt * (ch.rk[d] * ch.e_band[d])
                dkr = dt * (qs * ch.e_band[d])
                dk = dk + (pltpu.roll(dkr, C - d, 0) if d else dkr)
            db = qs * dq - kc * dk
            db = jnp.where(rows == C - 1, db + dbl, db)
            dq_ref[sl, :] = (dq * scale).astype(BF16)
            dk_ref[sl, :] = dk.astype(BF16)
            dv_ref[sl, :] = dv.astype(BF16)
            dla_ref[sl, :] = _dot(_tri(False), db, NN, HI)
            upd = _dot(doc_b, (qs * ch.eb).astype(BF16), TN)
            dstate[...] = dst * ebl + jnp.where(_state_mask(), upd, 0.0)
            return carry

        lax.fori_loop(0, nbc, chunk, 0, unroll=8)

    rev = lambda i: nblk - 1 - i
    qspec = lambda off: pl.BlockSpec((tb, PAIR_QK), lambda p, i: (rev(i), off // PAIR_QK + p))
    pair_qk = pl.BlockSpec((tb, PAIR_QK), lambda p, i: (rev(i), p))
    pair_v = pl.BlockSpec((tb, PAIR_V), lambda p, i: (rev(i), p))
    sel = _band_selector()
    ride_arrays, ride_gather = ride if ride else ([], [])
    nr = len(ride_arrays)
    grid = (npair, nblk)
    outs = pl.pallas_call(
        _riding(body, 8, 4, ride_gather, grid), grid=grid,
        in_specs=[qspec(O_GQ), qspec(O_GK), pl.BlockSpec((tb, PAIR_V), lambda p, i: (rev(i), O_GV // PAIR_V + p)),
                  pair_qk, pair_v, pl.BlockSpec((1, nbc, PAIR_V, PAIR_QK), lambda p, i: (p, rev(i), 0, 0)),
                  pl.BlockSpec((GLA_SUB * PAIR_QK, LANE), lambda p, i: (0, 0)),
                  pl.BlockSpec((LANE, GLA_SUB * PAIR_QK), lambda p, i: (0, 0))] + [ANY_SPEC] * nr,
        out_specs=[pair_qk, pair_qk, pair_v, pair_qk] + [ANY_SPEC] * nr,
        out_shape=[jax.ShapeDtypeStruct((S, GLA_QK), BF16), jax.ShapeDtypeStruct((S, GLA_QK), BF16),
                   jax.ShapeDtypeStruct((S, GLA_V), BF16), jax.ShapeDtypeStruct((S, GLA_QK), F32)]
        + _exchange_shapes(ride_arrays, ride_gather),
        scratch_shapes=[pltpu.VMEM((PAIR_V, PAIR_QK), F32)] + (_exchange_sems(nr) if nr else []),
        compiler_params=_cp("arbitrary", "arbitrary"), name="gla_bwd")(proj, proj, proj, la, do, states, sel, sel.T, *ride_arrays)
    return outs[0], outs[1], outs[2], outs[3], outs[4:]


def _gla_out(o, proj, gng, ts):
    S = o.shape[0]

    def body(o_ref, gr_ref, g_ref, y_ref):
        for h in range(GLA_HEADS):
            cols = slice(h * GLA_DV, (h + 1) * GLA_DV)
            ov, grv = o_ref[:, cols], gr_ref[:, cols]
            r = lax.rsqrt(jnp.mean(ov * ov, axis=-1, keepdims=True) + EPS)
            y_ref[:, cols] = (ov * r * g_ref[...] * (grv * _sigmoid(grv))).astype(BF16)

    return pl.pallas_call(
        body, grid=(S // ts,),
        in_specs=[pl.BlockSpec((ts, GLA_V), lambda i: (i, 0)), pl.BlockSpec((ts, GLA_V), lambda i: (i, O_GR // GLA_V)),
                  pl.BlockSpec((1, GLA_DV), lambda i: (0, 0))],
        out_specs=pl.BlockSpec((ts, GLA_V), lambda i: (i, 0)), out_shape=jax.ShapeDtypeStruct((S, GLA_V), BF16),
        compiler_params=_cp("parallel"), name="gla_out_fwd")(o, proj, gng)


def _gla_out_bwd(dmixed, o, proj, gng, ts):
    S = o.shape[0]

    def body(dy_ref, o_ref, gr_ref, g_ref, do_ref, dgr_ref, gg_ref):
        i = pl.program_id(0)
        gsum = jnp.zeros((1, GLA_DV), F32)
        for h in range(GLA_HEADS):
            cols = slice(h * GLA_DV, (h + 1) * GLA_DV)
            ov, grv, dy = o_ref[:, cols], gr_ref[:, cols], dy_ref[:, cols]
            r = lax.rsqrt(jnp.mean(ov * ov, axis=-1, keepdims=True) + EPS)
            oh = ov * r
            sg = _sigmoid(grv)
            silu = grv * sg
            don = dy * silu
            dgr_ref[:, cols] = (dy * (oh * g_ref[...]) * (sg * (1.0 + grv * (1.0 - sg)))).astype(BF16)
            gsum = gsum + jnp.sum(don * oh, axis=0, keepdims=True)
            doh = don * g_ref[...]
            do_ref[:, cols] = r * (doh - oh * jnp.mean(doh * oh, axis=-1, keepdims=True))
        part = jnp.concatenate([gsum, jnp.zeros((7, GLA_DV), F32)], axis=0)

        @pl.when(i == 0)
        def _():
            gg_ref[...] = part

        @pl.when(i > 0)
        def _():
            gg_ref[...] += part

    return pl.pallas_call(
        body, grid=(S // ts,),
        in_specs=[pl.BlockSpec((ts, GLA_V), lambda i: (i, 0)), pl.BlockSpec((ts, GLA_V), lambda i: (i, 0)),
                  pl.BlockSpec((ts, GLA_V), lambda i: (i, O_GR // GLA_V)), pl.BlockSpec((1, GLA_DV), lambda i: (0, 0))],
        out_specs=[pl.BlockSpec((ts, GLA_V), lambda i: (i, 0)), pl.BlockSpec((ts, GLA_V), lambda i: (i, 0)),
                   pl.BlockSpec((8, GLA_DV), lambda i: (0, 0))],
        out_shape=[jax.ShapeDtypeStruct((S, GLA_V), F32), jax.ShapeDtypeStruct((S, GLA_V), BF16),
                   jax.ShapeDtypeStruct((8, GLA_DV), F32)],
        compiler_params=_cp("arbitrary"), name="gla_out_bwd")(dmixed, o, proj, gng)


def _seg_matrix(width, seg, value):
    r = lax.broadcasted_iota(jnp.int32, (width, width), 0) // seg
    c = lax.broadcasted_iota(jnp.int32, (width, width), 1) // seg
    return jnp.where(r == c, value, 0.0).astype(BF16)


def _seg_sum(x, seg_matrix):
    hi = x.astype(BF16)
    lo = (x - hi.astype(F32)).astype(BF16)
    return _dot(hi, seg_matrix, NN) + _dot(lo, seg_matrix, NN)


def _head_norm(proj, qg, kg, ts):
    S = proj.shape[0]
    W = ATTN_DIM

    def body(q_ref, k_ref, qg_ref, kg_ref, qn_ref, kn_ref):
        seg = _seg_matrix(W, ATTN_HD, 1.0 / ATTN_HD)
        for x_ref, g_ref, o_ref, scale in ((q_ref, qg_ref, qn_ref, ATTN_HD ** -0.5), (k_ref, kg_ref, kn_ref, 1.0)):
            xv = x_ref[...]
            ms = _seg_sum(xv * xv, seg)
            o_ref[...] = xv * lax.rsqrt(ms + EPS) * (g_ref[...] * scale)

    blk = lambda off: pl.BlockSpec((ts, W), lambda i: (i, off // W))
    out = pl.BlockSpec((ts, W), lambda i: (i, 0))
    vec = pl.BlockSpec((1, W), lambda i: (0, 0))
    return pl.pallas_call(
        body, grid=(S // ts,), in_specs=[blk(O_AQ), blk(O_AK), vec, vec], out_specs=[out] * 2,
        out_shape=[jax.ShapeDtypeStruct((S, W), F32)] * 2, compiler_params=_cp("parallel"), name="attn_head_norm")(
            proj, proj, qg, kg)


def _slope(head):
    one = jnp.ones((1, 1), jnp.int32)
    return 1.0 / jnp.left_shift(one, one * (head + 1)).astype(F32)


ATTN_GROUP = 4


def _attn_rows(d, g, r):
    start = g * d * ATTN_BLOCK + r
    return pl.ds(start, ATTN_BLOCK) if d == 1 else pl.ds(start, ATTN_BLOCK, stride=d)


def _for_blocks(d, G, fn):
    for g in range(G):
        if d <= ATTN_GROUP:
            for r in range(d):
                fn(g, r)
        else:
            def step(r, carry, g=g):
                fn(g, r)
                return carry
            lax.fori_loop(0, d, step, 0, unroll=ATTN_GROUP)


def _attn_specs(d, S):
    G = max(1, ATTN_GROUP // d)
    edge = d * ATTN_BLOCK
    tq = G * edge
    nb, n_edge = S // tq, S // edge

    def specs(off=0):
        return [pl.BlockSpec((tq, LANE), lambda hp, n: (n, off + hp)),
                pl.BlockSpec((edge, LANE), lambda hp, n: (jnp.maximum(n * G - 1, 0), off + hp)),
                pl.BlockSpec((edge, LANE), lambda hp, n: (jnp.minimum((n + 1) * G, n_edge - 1), off + hp))]

    return G, nb, specs


def _attn_bias(d, hp, first_tile):
    B = ATTN_BLOCK
    iq = lax.broadcasted_iota(jnp.int32, (B, 2 * B), 0)
    ik = lax.broadcasted_iota(jnp.int32, (B, 2 * B), 1)
    rel = iq + B - ik
    window = (rel >= 0) & (rel <= B)
    relf = (d * rel).astype(F32)
    full = [jnp.where(window, -_slope(hp * 2 + h) * relf, NEG) for h in range(2)]
    edge = [jnp.where((ik >= B) | jnp.logical_not(first_tile), b, NEG) for b in full]
    return full, edge


def _attn_bias_t(d, hp, has_next):
    B = ATTN_BLOCK
    ik = lax.broadcasted_iota(jnp.int32, (B, B), 0)
    iq = lax.broadcasted_iota(jnp.int32, (B, B), 1)
    tiles = []
    for nxt in range(2):
        rel = iq - ik + nxt * B
        window = (rel >= 0) & (rel <= B)
        relf = (d * rel).astype(F32)
        tiles.append([jnp.where(window, -_slope(hp * 2 + h) * relf, NEG) for h in range(2)])
    tiles.append([jnp.where(has_next, b, NEG) for b in tiles[1]])
    return tiles


def _attn_fwd(qn, kn, proj, d):
    S, W = qn.shape
    G, nb, specs = _attn_specs(d, S)

    def body(q_ref, kp_ref, kc_ref, vp_ref, vc_ref, o_ref, l_ref):
        hp, n = pl.program_id(0), pl.program_id(1)
        lo = lax.broadcasted_iota(jnp.int32, (1, LANE), 1) < ATTN_HD
        full, edge = _attn_bias(d, hp, n == 0)

        def sub(g, r):
            rows = _attn_rows(d, g, r)
            before = _attn_rows(d, max(g - 1, 0), r)
            kb_ref, vb_ref = (kp_ref, vp_ref) if g == 0 else (kc_ref, vc_ref)
            bias = edge if g == 0 else full
            qv = q_ref[rows, :].astype(BF16)
            kv = jnp.concatenate([kb_ref[before, :], kc_ref[rows, :]], axis=0).astype(BF16)
            vv = jnp.concatenate([vb_ref[before, :], vc_ref[rows, :]], axis=0).astype(BF16)
            outs, lses = [], []
            for h in range(2):
                qm = jnp.where(lo == (h == 0), qv, jnp.zeros_like(qv))
                s = _dot(qm, kv, NT) + bias[h]
                m = jnp.max(s, axis=-1, keepdims=True)
                p = jnp.exp(s - m)
                den = jnp.sum(p, axis=-1, keepdims=True)
                outs.append(_dot(p.astype(BF16), vv, NN) / den)
                lses.append(m + jnp.log(den))
            o_ref[rows, :] = jnp.where(lo, outs[0], outs[1])
            l_ref[rows, :] = jnp.where(lo, lses[0], lses[1])

        _for_blocks(d, G, sub)

    cur, prev, _ = specs()
    vcur, vprev, _ = specs(O_AV // LANE)
    return pl.pallas_call(
        body, grid=(W // LANE, nb), in_specs=[cur, prev, cur, vprev, vcur], out_specs=[cur, cur],
        out_shape=[jax.ShapeDtypeStruct((S, W), F32)] * 2,
        compiler_params=_cp("parallel", "arbitrary"), name=f"attn_fwd_d{d}")(qn, kn, kn, proj, proj)


def _attn_merge(y_gla, os_, ls_, ts):
    S, W = os_[0].shape

    def body(yg, o1, o2, o3, l1, l2, l3, mixed_ref, mixed_t_ref, y_ref, lse_ref):
        a, b, c = l1[...], l2[...], l3[...]
        m = jnp.maximum(jnp.maximum(a, b), c)
        ea, eb, ec = jnp.exp(a - m), jnp.exp(b - m), jnp.exp(c - m)
        tot = ea + eb + ec
        y = (ea * o1[...] + eb * o2[...] + ec * o3[...]) / tot
        y_ref[...] = y
        mixed_ref[:, :W] = yg[...]
        mixed_ref[:, W:] = y.astype(BF16)
        mixed_t_ref[:W, :] = yg[...].astype(F32).T.astype(BF16)
        mixed_t_ref[W:, :] = y.T.astype(BF16)
        lse_ref[...] = m + jnp.log(tot)

    spec = pl.BlockSpec((ts, W), lambda i: (i, 0))
    return pl.pallas_call(
        body, grid=(S // ts,), in_specs=[spec] * 7,
        out_specs=[pl.BlockSpec((ts, 2 * W), lambda i: (i, 0)), _col_spec(2 * W, ts), spec, spec],
        out_shape=[jax.ShapeDtypeStruct((S, 2 * W), BF16), jax.ShapeDtypeStruct((2 * W, S), BF16),
                   jax.ShapeDtypeStruct((S, W), F32), jax.ShapeDtypeStruct((S, W), F32)],
        compiler_params=_cp("parallel"), name="attn_merge")(y_gla, *os_, *ls_)


def _attn_delta(dmixed, y, ts):
    S, W = y.shape

    def body(dy_ref, y_ref, d_ref):
        d_ref[...] = _seg_sum(dy_ref[...] * y_ref[...], _seg_matrix(W, ATTN_HD, 1.0))

    return pl.pallas_call(
        body, grid=(S // ts,), in_specs=[pl.BlockSpec((ts, W), lambda i: (i, 1)), pl.BlockSpec((ts, W), lambda i: (i, 0))],
        out_specs=pl.BlockSpec((ts, W), lambda i: (i, 0)), out_shape=jax.ShapeDtypeStruct((S, W), F32),
        compiler_params=_cp("parallel"), name="attn_delta")(dmixed, y)


def _attn_dq(qn, kn, proj, dmixed, lse, delta, d):
    S, W = qn.shape
    B = ATTN_BLOCK
    G, nb, specs = _attn_specs(d, S)

    def body(q_ref, kp_ref, kc_ref, vp_ref, vc_ref, dy_ref, l_ref, de_ref, dq_ref):
        hp, n = pl.program_id(0), pl.program_id(1)
        lo = lax.broadcasted_iota(jnp.int32, (1, LANE), 1) < ATTN_HD
        full, edge = _attn_bias(d, hp, n == 0)

        def sub(g, r):
            rows = _attn_rows(d, g, r)
            before = _attn_rows(d, max(g - 1, 0), r)
            kb_ref, vb_ref = (kp_ref, vp_ref) if g == 0 else (kc_ref, vc_ref)
            bias = edge if g == 0 else full
            qv, dyv = q_ref[rows, :].astype(BF16), dy_ref[rows, :]
            lv, dev = l_ref[rows, :], de_ref[rows, :]
            kv = jnp.concatenate([kb_ref[before, :], kc_ref[rows, :]], axis=0).astype(BF16)
            vv = jnp.concatenate([vb_ref[before, :], vc_ref[rows, :]], axis=0).astype(BF16)
            outs = []
            for h in range(2):
                sel = lo == (h == 0)
                qm = jnp.where(sel, qv, jnp.zeros_like(qv))
                dym = jnp.where(sel, dyv, 0.0).astype(BF16)
                lse_h = lv[:, h * ATTN_HD:h * ATTN_HD + 1]
                del_h = dev[:, h * ATTN_HD:h * ATTN_HD + 1]
                p = jnp.exp(_dot(qm, kv, NT) + bias[h] - lse_h)
                ds = p * (_dot(dym, vv, NT) - del_h)
                outs.append(_dot(ds.astype(BF16), kv, NN) * (ATTN_HD ** -0.5))
            dq_ref[rows, :] = jnp.where(lo, outs[0], outs[1])

        _for_blocks(d, G, sub)

    cur, prev, _ = specs()
    vcur, vprev, _ = specs(O_AV // LANE)
    dycur, _, _ = specs(W // LANE)
    return pl.pallas_call(
        body, grid=(W // LANE, nb), in_specs=[cur, prev, cur, vprev, vcur, dycur, cur, cur], out_specs=cur,
        out_shape=jax.ShapeDtypeStruct((S, W), F32),
        compiler_params=_cp("parallel", "arbitrary"), name=f"attn_dq_d{d}")(qn, kn, kn, proj, proj, dmixed, lse, delta)


def _attn_dkv(qn, kn, proj, dmixed, lse, delta, d):
    S, W = qn.shape
    B = ATTN_BLOCK
    G, nb, specs = _attn_specs(d, S)

    def body(k_ref, v_ref, qc_ref, qn_ref, dyc_ref, dyn_ref, lc_ref, ln_ref, dec_ref, den_ref, dk_ref, dv_ref):
        hp, n = pl.program_id(0), pl.program_id(1)
        lo = lax.broadcasted_iota(jnp.int32, (1, LANE), 1) < ATTN_HD
        own, inner, outer = _attn_bias_t(d, hp, n + 1 < nb)

        def sub(g, r):
            rows = _attn_rows(d, g, r)
            kv, vv = k_ref[rows, :].astype(BF16), v_ref[rows, :].astype(BF16)
            dk = jnp.zeros((B, LANE), F32)
            dv = jnp.zeros((B, LANE), F32)
            inside = g + 1 < G
            after = _attn_rows(d, g + 1 if inside else 0, r)
            following = (qc_ref, dyc_ref, lc_ref, dec_ref) if inside else (qn_ref, dyn_ref, ln_ref, den_ref)
            for bias, qrows, (q_ref, dy_ref, l_ref, de_ref) in (
                    (own, rows, (qc_ref, dyc_ref, lc_ref, dec_ref)), (inner if inside else outer, after, following)):
                qv, dyv = q_ref[qrows, :].astype(BF16), dy_ref[qrows, :]
                lt, det = l_ref[qrows, :].T, de_ref[qrows, :].T
                for h in range(2):
                    sel = lo == (h == 0)
                    qm = jnp.where(sel, qv, jnp.zeros_like(qv))
                    dym = jnp.where(sel, dyv, 0.0).astype(BF16)
                    lse_h = lt[h * ATTN_HD:h * ATTN_HD + 1, :]
                    del_h = det[h * ATTN_HD:h * ATTN_HD + 1, :]
                    pt = jnp.exp(_dot(kv, qm, NT) + bias[h] - lse_h)
                    dv = dv + _dot(pt.astype(BF16), dym, NN)
                    dst = pt * (_dot(vv, dym, NT) - del_h)
                    dk = dk + _dot(dst.astype(BF16), qm, NN)
            dk_ref[rows, :] = dk
            dv_ref[rows, :] = dv

        _for_blocks(d, G, sub)

    cur, _, nxt = specs()
    vcur, _, _ = specs(O_AV // LANE)
    dycur, _, dynxt = specs(W // LANE)
    return pl.pallas_call(
        body, grid=(W // LANE, nb), in_specs=[cur, vcur, cur, nxt, dycur, dynxt, cur, nxt, cur, nxt], out_specs=[cur, cur],
        out_shape=[jax.ShapeDtypeStruct((S, W), F32)] * 2,
        compiler_params=_cp("parallel", "arbitrary"), name=f"attn_dkv_d{d}")(
            kn, proj, qn, qn, dmixed, dmixed, lse, lse, delta, delta)


def _attn_post(dqs, dks, dvs, proj, qg, kg, ts):
    S = proj.shape[0]
    W = ATTN_DIM

    def body(dq1, dq2, dq3, dk1, dk2, dk3, dv1, dv2, dv3, aq_ref, ak_ref, qg_ref, kg_ref, daq_ref, dak_ref, dav_ref, gg_ref):
        i = pl.program_id(0)
        seg = _seg_matrix(W, ATTN_HD, 1.0 / ATTN_HD)
        gsums = []
        for (d1, d2, d3), x_ref, g_ref, o_ref in (((dq1, dq2, dq3), aq_ref, qg_ref, daq_ref), ((dk1, dk2, dk3), ak_ref, kg_ref, dak_ref)):
            dy = d1[...] + d2[...] + d3[...]
            xv = x_ref[...]
            r = lax.rsqrt(_seg_sum(xv * xv, seg) + EPS)
            xh = xv * r
            dxh = dy * g_ref[...]
            o_ref[...] = (r * (dxh - xh * _seg_sum(dxh * xh, seg))).astype(BF16)
            gsums.append(jnp.sum(dy * xh, axis=0, keepdims=True))
        dav_ref[...] = (dv1[...] + dv2[...] + dv3[...]).astype(BF16)
        part = jnp.concatenate(gsums + [jnp.zeros((6, W), F32)], axis=0)

        @pl.when(i == 0)
        def _():
            gg_ref[...] = part

        @pl.when(i > 0)
        def _():
            gg_ref[...] += part

    row = pl.BlockSpec((ts, W), lambda i: (i, 0))
    blk = lambda off: pl.BlockSpec((ts, W), lambda i: (i, off // W))
    vec = pl.BlockSpec((1, W), lambda i: (0, 0))
    return pl.pallas_call(
        body, grid=(S // ts,), in_specs=[row] * 9 + [blk(O_AQ), blk(O_AK), vec, vec],
        out_specs=[row, row, row, pl.BlockSpec((8, W), lambda i: (0, 0))],
        out_shape=[jax.ShapeDtypeStruct((S, W), BF16)] * 3 + [jax.ShapeDtypeStruct((8, W), F32)],
        compiler_params=_cp("arbitrary"), name="attn_post")(*dqs, *dks, *dvs, proj, proj, qg, kg)


def _shift_down(cur, halo, n):
    return pltpu.roll(jnp.concatenate([halo, cur], axis=0), n, 0)[8:]


def _shift_up(cur, halo, n):
    ts = cur.shape[0]
    return pltpu.roll(jnp.concatenate([cur, halo], axis=0), ts + 8 - n, 0)[:ts]


def _conv(cur, halo, w, b):
    return b + w[0:1, :] * _shift_down(cur, halo, 2) + w[1:2, :] * _shift_down(cur, halo, 1) + w[2:3, :] * cur


def _conv_swiglu(u0, conv_w8, conv_b, ts, tc):
    S, F2 = u0.shape
    F = F2 // 2
    nc = F // tc
    hb = ts // 8

    def body(ug_ref, ugh_ref, uv_ref, uvh_ref, wg_ref, wv_ref, bg_ref, bv_ref, a_ref, at_ref):
        first = pl.program_id(0) == 0
        ugh = jnp.where(first, 0.0, ugh_ref[...])
        uvh = jnp.where(first, 0.0, uvh_ref[...])
        g = _conv(ug_ref[...], ugh, wg_ref[...], bg_ref[...])
        v = _conv(uv_ref[...], uvh, wv_ref[...], bv_ref[...])
        a = g * _sigmoid(g) * v
        a_ref[...] = a.astype(BF16)
        at_ref[...] = a.T.astype(BF16)

    main = lambda off: pl.BlockSpec((ts, tc), lambda i, j: (i, j + off))
    halo = lambda off: pl.BlockSpec((8, tc), lambda i, j: (jnp.maximum(i * hb - 1, 0), j + off))
    wspec = lambda off: pl.BlockSpec((8, tc), lambda i, j: (0, j + off))
    bspec = lambda off: pl.BlockSpec((1, tc), lambda i, j: (0, j + off))
    return pl.pallas_call(
        body, grid=(S // ts, nc),
        in_specs=[main(0), halo(0), main(nc), halo(nc), wspec(0), wspec(nc), bspec(0), bspec(nc)],
        out_specs=[pl.BlockSpec((ts, tc), lambda i, j: (i, j)), pl.BlockSpec((tc, ts), lambda i, j: (j, i))],
        out_shape=[jax.ShapeDtypeStruct((S, F), BF16), jax.ShapeDtypeStruct((F, S), BF16)],
        compiler_params=_cp("parallel", "parallel"), name="conv_swiglu")(u0, u0, u0, u0, conv_w8, conv_w8, conv_b, conv_b)


def _ffn_du(da, u0, conv_w8, conv_b, ts, tc, ride=None):
    S, F2 = u0.shape
    F = F2 // 2
    nc = F // tc
    hb = ts // 8
    grid = (nc, S // ts)
    ride_arrays, ride_gather = ride if ride else ([], [])
    nr = len(ride_arrays)

    def body(da_ref, ug_ref, ugh_ref, uv_ref, uvh_ref, wg_ref, wv_ref, bg_ref, bv_ref, du_ref, sg_ref, sv_ref):
        i = pl.program_id(1)
        first = i == 0
        halves = []
        for u_ref, h_ref, w_ref, b_ref in ((ug_ref, ugh_ref, wg_ref, bg_ref), (uv_ref, uvh_ref, wv_ref, bv_ref)):
            u, halo, w = u_ref[...], jnp.where(first, 0.0, h_ref[...]), w_ref[...]
            s2, s1 = _shift_down(u, halo, 2), _shift_down(u, halo, 1)
            halves.append((b_ref[...] + w[0:1, :] * s2 + w[1:2, :] * s1 + w[2:3, :] * u, s2, s1, u))
        g, v = halves[0][0], halves[1][0]
        dav = da_ref[...]
        sig = _sigmoid(g)
        dus = (dav * v * (sig * (1.0 + g * (1.0 - sig))), dav * (g * sig))
        for h, (du, sums_ref) in enumerate(zip(dus, (sg_ref, sv_ref))):
            du_ref[h] = du
            _, s2, s1, u = halves[h]
            part = jnp.concatenate([jnp.sum(du * s2, axis=0, keepdims=True), jnp.sum(du * s1, axis=0, keepdims=True),
                                    jnp.sum(du * u, axis=0, keepdims=True), jnp.sum(du, axis=0, keepdims=True),
                                    jnp.zeros((4, tc), F32)], axis=0)

            @pl.when(first)
            def _(sums_ref=sums_ref, part=part):
                sums_ref[...] = part

            @pl.when(i > 0)
            def _(sums_ref=sums_ref, part=part):
                sums_ref[...] += part

    main = lambda off: pl.BlockSpec((ts, tc), lambda j, i: (i, j + off))
    halo = lambda off: pl.BlockSpec((8, tc), lambda j, i: (jnp.maximum(i * hb - 1, 0), j + off))
    wspec = lambda off: pl.BlockSpec((8, tc), lambda j, i: (0, j + off))
    bspec = lambda off: pl.BlockSpec((1, tc), lambda j, i: (0, j + off))
    sums_spec = pl.BlockSpec((8, tc), lambda j, i: (0, j))
    outs = pl.pallas_call(
        _riding(body, 9, 3, ride_gather, grid), grid=grid,
        in_specs=[main(0), main(0), halo(0), main(nc), halo(nc), wspec(0), wspec(nc), bspec(0), bspec(nc)] + [ANY_SPEC] * nr,
        out_specs=[pl.BlockSpec((2, ts, tc), lambda j, i: (0, i, j)), sums_spec, sums_spec] + [ANY_SPEC] * nr,
        out_shape=[jax.ShapeDtypeStruct((2, S, F), F32), jax.ShapeDtypeStruct((8, F), F32), jax.ShapeDtypeStruct((8, F), F32)]
        + _exchange_shapes(ride_arrays, ride_gather),
        scratch_shapes=_exchange_sems(nr) if nr else [],
        compiler_params=_cp("arbitrary", "arbitrary"), name="ffn_du")(
            da, u0, u0, u0, u0, conv_w8, conv_w8, conv_b, conv_b, *ride_arrays)
    return outs[0], outs[1], outs[2], outs[3:]


def _ffn_du0(du, conv_w8, ts, tc):
    _, S, F = du.shape
    nc = F // tc
    hb = ts // 8
    nrow = S // ts

    def body(du_ref, duh_ref, w_ref, o_ref):
        last = pl.program_id(0) == nrow - 1
        cur, halo, w = du_ref[...], jnp.where(last, 0.0, duh_ref[...]), w_ref[...]
        o_ref[...] = (w[2:3, :] * cur + w[1:2, :] * _shift_up(cur, halo, 1) + w[0:1, :] * _shift_up(cur, halo, 2)).astype(BF16)

    return pl.pallas_call(
        body, grid=(nrow, 2, nc),
        in_specs=[pl.BlockSpec((None, ts, tc), lambda i, h, j: (h, i, j)),
                  pl.BlockSpec((None, 8, tc), lambda i, h, j: (h, jnp.minimum((i + 1) * hb, S // 8 - 1), j)),
                  pl.BlockSpec((8, tc), lambda i, h, j: (0, h * nc + j))],
        out_specs=pl.BlockSpec((ts, tc), lambda i, h, j: (i, h * nc + j)), out_shape=jax.ShapeDtypeStruct((S, 2 * F), BF16),
        compiler_params=_cp("parallel", "parallel", "parallel"), name="ffn_du0")(du, du, conv_w8)


def _loss_resid(x2, t2, g2, target, ts):
    S, D = x2.shape

    def body(x_ref, t_ref, g_ref, y_ref, dx_ref, dt_ref, sums_ref):
        i = pl.program_id(0)
        tv, gv = t_ref[...], g_ref[...]
        e = x_ref[...] + gv * tv - y_ref[...]
        dx = e * (1.0 / D)
        dx_ref[...] = dx
        dt_ref[...] = (dx * gv).astype(BF16)
        part = jnp.concatenate([jnp.sum(e * e, axis=0, keepdims=True), jnp.sum(dx * tv, axis=0, keepdims=True),
                                jnp.zeros((6, D), F32)], axis=0)

        @pl.when(i == 0)
        def _():
            sums_ref[...] = part

        @pl.when(i > 0)
        def _():
            sums_ref[...] += part

    row, vec = _row_spec(ts, D), _vec_spec(D)
    return pl.pallas_call(
        body, grid=(S // ts,), in_specs=[row, row, vec, row], out_specs=[row, row, pl.BlockSpec((8, D), lambda i: (0, 0))],
        out_shape=[jax.ShapeDtypeStruct((S, D), F32), jax.ShapeDtypeStruct((S, D), BF16), jax.ShapeDtypeStruct((8, D), F32)],
        compiler_params=_cp("arbitrary"), name="loss_resid")(x2, t2, g2, target)


def _adamw(w, g, m, v, name):
    shape = w.shape
    view = (math.prod(shape[:-1]), shape[-1])
    R, C = view
    fits = [t for t in range(8, R + 1, 8) if R % t == 0 and t * C <= SUM_BLOCK_ELEMS]
    tr = max(fits) if fits else R

    def body(w_ref, g_ref, m_ref, v_ref, d_ref, nm_ref, nv_ref):
        gv = g_ref[...]
        nm = ADAM_B1 * m_ref[...] + (1.0 - ADAM_B1) * gv
        nv = ADAM_B2 * v_ref[...] + (1.0 - ADAM_B2) * (gv * gv)
        m_hat = nm / (1.0 - ADAM_B1 ** ADAM_STEP)
        v_hat = nv / (1.0 - ADAM_B2 ** ADAM_STEP)
        d_ref[...] = -ADAM_LR * (m_hat / (jnp.sqrt(v_hat) + ADAM_EPS) + ADAM_WD * w_ref[...])
        nm_ref[...] = nm
        nv_ref[...] = nv

    spec = pl.BlockSpec((tr, C), lambda i: (i, 0))
    outs = pl.pallas_call(
        body, grid=(R // tr,), in_specs=[spec] * 4, out_specs=[spec] * 3, out_shape=[jax.ShapeDtypeStruct(view, F32)] * 3,
        compiler_params=_cp("parallel"), name=name)(*[a.reshape(view) for a in (w, g, m, v)])
    return [o.reshape(shape) for o in outs]


def _pad_rows8(a):
    return jnp.concatenate([a, jnp.zeros((8 - a.shape[0], a.shape[1]), a.dtype)], axis=0)


def _local_step(x, target, mod, n1g, w_in_p, wg_p, bg, gng, qng, kng, w_out_s, n2g, w_up_s, conv_w, conv_b, w_down_s):
    S, D = x.shape
    F = w_down_s.shape[0] * N_DEV
    ts = min(512, S)
    sh1, sc1, g1, sh2, sc2, g2 = [mod[i:i + 1] for i in range(6)]
    conv_w8 = _pad_rows8(conv_w)
    qg_t, kg_t = jnp.tile(qng, (1, ATTN_HEADS)), jnp.tile(kng, (1, ATTN_HEADS))

    h1, h1_t = _rms_mod(x, n1g, sc1, sh1, ts, "rms_mod1")
    proj, (g_out,) = _mm(h1, w_in_p, NN, 512, PROJ_W, 1024, F32, "mm_in", ride=([w_out_s], [True]))
    w_out = g_out.reshape(-1, D)
    la = _gate_fwd(proj, wg_p, bg, ts)
    o_gla, states, (g_up,) = _gla_fwd(proj, la, 512, ride=([w_up_s], [True]))
    w_up = _cols_from_blocks(g_up)
    y_gla = _gla_out(o_gla, proj, gng, ts)
    qn, kn = _head_norm(proj, qg_t, kg_t, ts)
    branches = [_attn_fwd(qn, kn, proj, d) for d in DILATIONS]
    mixed, mixed_t, y_att, lse = _attn_merge(y_gla, [b[0] for b in branches], [b[1] for b in branches], ts)
    t1 = _mm(mixed, w_out, NN, 512, 1024, 1024, F32, "mm_out")
    x2, h2, h2_t = _resid_rms_mod(x, t1, g1, n2g, sc2, sh2, ts, "resid_rms_mod2")
    u0, (g_down,) = _mm(h2, w_up, NN, 512, 2816, 1024, F32, "mm_up", ride=([w_down_s], [True]))
    w_down = g_down.reshape(F, D)
    tc = 1408 if F % 1408 == 0 else F
    a, a_t = _conv_swiglu(u0, conv_w8, conv_b, min(256, S), tc)
    t2 = _mm(a, w_down, NN, 512, 1024, F, F32, "mm_down")
    dx3, dt2, sums3 = _loss_resid(x2, t2, g2, target, ts)
    loss_row, dg2 = sums3[0:1], sums3[1:2]

    g_w_down = _mm(a_t, dt2, NN, 1408, 1024, 2048, F32, "mm_gw_down")
    da = _mm(dt2, w_down, NT, 512, 2816, 1024, F32, "mm_da")
    du, sums_g, sums_v, (r_down,) = _ffn_du(da, u0, conv_w8, conv_b, min(256, S), tc,
                                            ride=([g_w_down.reshape(N_DEV, -1, D)], [False]))
    g_conv_w = jnp.concatenate([sums_g[0:3], sums_v[0:3]], axis=1)
    g_conv_b = jnp.concatenate([sums_g[3:4], sums_v[3:4]], axis=1)
    du0 = _ffn_du0(du, conv_w8, min(256, S), tc)
    g_w_up = _mm(h2_t, du0, NN, 512, 2816, 2048, F32, "mm_gw_up")
    dh2 = _mm(du0, w_up, NT, 512, 1024, 2816, F32, "mm_dh2")
    dx2, sums2, dt1 = _rms_mod_bwd(dh2, x2, dx3, n2g, sc2, ts, "rms_mod_bwd2", t_prev=t1, g_prev=g1)
    dsh2, dsc2, g_n2g, dg1 = sums2[0:1], sums2[1:2], sums2[2:3], sums2[3:4]
    g_w_out = _mm(mixed_t, dt1, NN, 1024, 1024, 2048, F32, "mm_gw_out")
    dmixed = _mm(dt1, w_out, NT, 512, 1024, 1024, F32, "mm_dmixed")
    do_gla, dgr, gng_sums = _gla_out_bwd(dmixed, o_gla, proj, gng, ts)
    dgq, dgk, dgv, dla, (r_up, r_out) = _gla_bwd(
        proj, la, do_gla, states, 512, ride=([_col_blocks(g_w_up), g_w_out.reshape(N_DEV, -1, D)], [False, False]))
    dglr, g_wg_p, gb_sums = _gate_bwd(dla, la, proj, wg_p, ts)
    delta = _attn_delta(dmixed, y_att, ts)
    dqs = [_attn_dq(qn, kn, proj, dmixed, lse, delta, d) for d in DILATIONS]
    dkvs = [_attn_dkv(qn, kn, proj, dmixed, lse, delta, d) for d in DILATIONS]
    daq, dak, dav, qk_sums = _attn_post(dqs, [t[0] for t in dkvs], [t[1] for t in dkvs], proj, qg_t, kg_t, ts)
    dproj = jnp.concatenate([dgq, dgk, dgv, dgr, daq, dak, dav, dglr, jnp.zeros((S, PROJ_W - O_GLR - LANE), BF16)], axis=1)
    g_w_in_p = _mm(h1_t, dproj, NN, 512, PROJ_W, 1024, F32, "mm_gw_in")
    g_w_in = jnp.concatenate([g_w_in_p[:, :GLR_SRC], g_w_in_p[:, O_GLR:O_GLR + GLA_RANK], g_w_in_p[:, GLR_SRC:O_GLR]], axis=1)
    dh1, (r_in,) = _mm(dproj, w_in_p, NT, 512, 1024, PROJ_W, F32, "mm_dh1", ride=([_col_blocks(g_w_in).astype(BF16)], [False]))
    dx, sums1 = _rms_mod_bwd(dh1, x, dx2, n1g, sc1, ts, "rms_mod_bwd1")
    dsh1, dsc1, g_n1g = sums1[0:1], sums1[1:2], sums1[2:3]

    dmod = jnp.concatenate([dsh1, dsc1, dg1, dsh2, dsc2, dg2], axis=1)
    grads = dict(n1g=g_n1g, w_in=r_in, wg=g_wg_p[:GLA_RANK], bg=gb_sums[0:1], gng=gng_sums[0:1],
                 qng_lanes=qk_sums[0:1], kng_lanes=qk_sums[1:2], w_out=r_out, n2g=g_n2g, w_up=r_up,
                 conv_w=g_conv_w, conv_b=g_conv_b, w_down=r_down)
    return loss_row, dx, dmod, grads


def _col_blocks(a):
    R, W = a.shape
    return a.reshape(R, N_DEV, W // N_DEV).transpose(1, 0, 2)


def _cols_from_blocks(a):
    n, R, C = a.shape
    return a.transpose(1, 0, 2).reshape(R, n * C)


def kernel(x, c, w_ada, b_ada, norm1_g, w_in, gla_w_gate, gla_b_gate, gla_norm_g, q_norm_g, k_norm_g, w_out, norm2_g, w_up, conv_w, conv_b, w_down, loss_target, m_w_ada, m_b_ada, m_norm1_g, m_w_in, m_gla_w_gate, m_gla_b_gate, m_gla_norm_g, m_q_norm_g, m_k_norm_g, m_w_out, m_norm2_g, m_w_up, m_conv_w, m_conv_b, m_w_down, v_w_ada, v_b_ada, v_norm1_g, v_w_in, v_gla_w_gate, v_gla_b_gate, v_gla_norm_g, v_q_norm_g, v_k_norm_g, v_w_out, v_norm2_g, v_w_up, v_conv_w, v_conv_b, v_w_down):
    axes = ("x", "y", "c")
    me = 4 * lax.axis_index("x") + 2 * lax.axis_index("y") + lax.axis_index("c")
    S, D = x.shape[1], x.shape[2]
    x2d, tgt2d = x[0], loss_target[0]
    w_in_s, w_out_s, w_up_s, w_down_s, w_ada_s = w_in[0], w_out[0], w_up[0], w_down[0], w_ada[0]
    conv_w_s, wg_s = conv_w[0], gla_w_gate[0]
    in_c, up_c, ada_c, wg_c, cw_c = w_in_s.shape[1], w_up_s.shape[1], w_ada_s.shape[1], wg_s.shape[1], conv_w_s.shape[1]
    F = w_down_s.shape[0] * N_DEV

    small = jnp.concatenate([conv_w_s.reshape(1, -1), wg_s.reshape(1, -1)], axis=1)
    n_small = small.shape[1]
    small = jnp.pad(small, ((0, 0), (0, -n_small % LANE)))
    g_c, g_in, g_small = _exchange([c, w_in_s.astype(BF16), small], [True] * 3, "gather_w_in")
    c_all = g_c.reshape(N_DEV, D)
    w_in_full = _cols_from_blocks(g_in)
    w_in_p = jnp.concatenate([w_in_full[:, :GLR_SRC], w_in_full[:, GLR_SRC + GLA_RANK:],
                              w_in_full[:, GLR_SRC:GLR_SRC + GLA_RANK], jnp.zeros((D, PROJ_W - O_GLR - GLA_RANK), BF16)], axis=1)
    g_small = g_small.reshape(N_DEV, -1)
    conv_w_full = _cols_from_blocks(g_small[:, :3 * cw_c].reshape(N_DEV, 3, cw_c))
    wg_full = _cols_from_blocks(g_small[:, 3 * cw_c:n_small].reshape(N_DEV, GLA_RANK, wg_c))
    wg_p = jnp.concatenate([wg_full, jnp.zeros((LANE - GLA_RANK, wg_full.shape[1]), F32)], axis=0)

    b_shard = lax.dynamic_slice(b_ada, (0, me * ada_c), (1, ada_c))
    mod_part = _ada_fwd(c_all, w_ada_s, b_shard)
    mod_recv, = _exchange([mod_part.reshape(N_DEV, 1, ada_c)], [False], "exchange_mod")
    mod = mod_recv.reshape(6, D)

    loss_row, dx, dmod, gr = _local_step(
        x2d, tgt2d, mod, norm1_g, w_in_p, wg_p, gla_b_gate, gla_norm_g, q_norm_g, k_norm_g,
        w_out_s.astype(BF16), norm2_g, w_up_s.astype(BF16), conv_w_full, conv_b, w_down_s.astype(BF16))
    loss = lax.psum(0.5 / D * jnp.sum(loss_row), axes)

    parts = [dmod, gr["n1g"], gr["bg"], gr["gng"], gr["qng_lanes"], gr["kng_lanes"], gr["n2g"], gr["conv_b"],
             gr["wg"].reshape(1, -1), gr["conv_w"].reshape(1, -1)]
    sizes = [p.shape[1] for p in parts]
    packed = jnp.concatenate(parts, axis=1)
    packed = jnp.pad(packed, ((0, 0), (0, -packed.shape[1] % (8 * LANE))))
    gathered, = _exchange([packed.reshape(8, -1)], [True], "gather_small_grads")
    gathered = gathered.reshape(N_DEV, -1)
    total = _sum_slots(gathered.reshape(N_DEV, 8, -1), "sum_small_grads").reshape(1, -1)
    offs = [0]
    for s_ in sizes:
        offs.append(offs[-1] + s_)
    t_dmod, t_n1g, t_bg, t_gng, t_qng, t_kng, t_n2g, t_conv_b, t_wg, t_conv_w = [
        total[:, offs[i]:offs[i + 1]] for i in range(len(sizes))]
    g_b_ada = t_dmod
    g_qng = t_qng.reshape(ATTN_HEADS, ATTN_HD).sum(axis=0, keepdims=True)
    g_kng = t_kng.reshape(ATTN_HEADS, ATTN_HD).sum(axis=0, keepdims=True)
    g_wg = lax.dynamic_slice(t_wg.reshape(GLA_RANK, -1), (0, me * wg_c), (GLA_RANK, wg_c))
    g_conv_w = lax.dynamic_slice(t_conv_w.reshape(3, -1), (0, me * cw_c), (3, cw_c))
    dmod_shard = lax.dynamic_slice(gathered[:, :6 * D], (0, me * ada_c), (N_DEV, ada_c))
    g_w_ada = _ada_bwd(c_all, dmod_shard)

    g_w_in = _sum_slots(gr["w_in"], "sum_gw_in")
    g_w_out = _sum_slots(gr["w_out"], "sum_gw_out")
    g_w_up = _sum_slots(gr["w_up"], "sum_gw_up")
    g_w_down = _sum_slots(gr["w_down"], "sum_gw_down")

    names = ["w_ada", "b_ada", "norm1_g", "w_in", "gla_w_gate", "gla_b_gate", "gla_norm_g", "q_norm_g", "k_norm_g",
             "w_out", "norm2_g", "w_up", "conv_w", "conv_b", "w_down"]
    ws = [w_ada, b_ada, norm1_g, w_in, gla_w_gate, gla_b_gate, gla_norm_g, q_norm_g, k_norm_g, w_out, norm2_g, w_up, conv_w, conv_b, w_down]
    ms = [m_w_ada, m_b_ada, m_norm1_g, m_w_in, m_gla_w_gate, m_gla_b_gate, m_gla_norm_g, m_q_norm_g, m_k_norm_g, m_w_out, m_norm2_g, m_w_up, m_conv_w, m_conv_b, m_w_down]
    vs = [v_w_ada, v_b_ada, v_norm1_g, v_w_in, v_gla_w_gate, v_gla_b_gate, v_gla_norm_g, v_q_norm_g, v_k_norm_g, v_w_out, v_norm2_g, v_w_up, v_conv_w, v_conv_b, v_w_down]
    gs = [g_w_ada, g_b_ada, t_n1g, g_w_in, g_wg, t_bg, t_gng, g_qng, g_kng, g_w_out, t_n2g, g_w_up, g_conv_w, t_conv_b, g_w_down]
    gs = [g.reshape(w.shape) for g, w in zip(gs, ws)]
    deltas, new_ms, new_vs = [], [], []
    for nm, w, g, m, v in zip(names, ws, gs, ms, vs):
        d_, m_, v_ = _adamw(w, g, m, v, "adamw_" + nm)
        deltas.append(d_)
        new_ms.append(m_)
        new_vs.append(v_)
    return (loss, dx.reshape(x.shape), *gs, *deltas, *new_ms, *new_vs)
```

```python
import functools
import math

import jax
import jax.numpy as jnp
from jax import lax
from jax.experimental import pallas as pl
from jax.experimental.pallas import tpu as pltpu

F32, BF16 = jnp.float32, jnp.bfloat16
HI = lax.Precision.HIGHEST
EPS = 1e-6
NEG = -1e30

N_DEV = 8
GLA_HEADS, GLA_DK, GLA_DV, GLA_RANK, GLA_TAU, GLA_CHUNK = 4, 64, 128, 16, 16.0, 64
ATTN_HEADS, ATTN_HD, ATTN_BLOCK = 8, 64, 128
DILATIONS = (1, 4, 16)
GLA_QK, GLA_V, ATTN_DIM = GLA_HEADS * GLA_DK, GLA_HEADS * GLA_DV, ATTN_HEADS * ATTN_HD
O_GQ, O_GK, O_GV, O_GR, O_AQ, O_AK, O_AV, O_GLR = 0, 256, 512, 1024, 1536, 2048, 2560, 3072
PROJ_W = 3328
LANE = 128
GLR_SRC = 2 * GLA_QK + 2 * GLA_V

ADAM_LR, ADAM_B1, ADAM_B2, ADAM_EPS, ADAM_WD, ADAM_STEP = 0.001, 0.9, 0.999, 1e-08, 0.01, 10

VMEM_LIMIT = 56 * 1024 * 1024
SUM_BLOCK_ELEMS = 256 * 1024


def _cp(*sem):
    return pltpu.CompilerParams(dimension_semantics=sem, vmem_limit_bytes=VMEM_LIMIT)


def _dot(a, b, dims, precision=None):
    return lax.dot_general(a, b, (dims, ((), ())), preferred_element_type=F32, precision=precision)


NN, NT, TN = ((1,), (0,)), ((1,), (1,)), ((0,), (0,))


def _sigmoid(z):
    return 1.0 / (1.0 + jnp.exp(-z))


HBM_SPEC = pl.BlockSpec(memory_space=pltpu.HBM)


def _exchange_shapes(arrays, gather):
    return [jax.ShapeDtypeStruct((N_DEV,) + (a.shape if g else a.shape[1:]), a.dtype) for a, g in zip(arrays, gather)]


def _exchange_sems(n):
    return [pltpu.SemaphoreType.DMA((n * (N_DEV - 1),)), pltpu.SemaphoreType.DMA((n * (N_DEV - 1),)), pltpu.SemaphoreType.DMA((n,))]


def _exchange_copies(ins, outs, gather, send_sems, recv_sems, local_sems):
    x, y, c = lax.axis_index("x"), lax.axis_index("y"), lax.axis_index("c")
    me = 4 * x + 2 * y + c
    copies = []
    for a in range(len(ins)):
        for p in range(1, N_DEV):
            px, py, pc = x ^ (p >> 2), y ^ ((p >> 1) & 1), c ^ (p & 1)
            peer = 4 * px + 2 * py + pc
            k = a * (N_DEV - 1) + p - 1
            copies.append(pltpu.make_async_remote_copy(
                src_ref=ins[a] if gather[a] else ins[a].at[peer], dst_ref=outs[a].at[me],
                send_sem=send_sems.at[k], recv_sem=recv_sems.at[k],
                device_id=(px, py, pc), device_id_type=pl.DeviceIdType.MESH))
        copies.append(pltpu.make_async_copy(ins[a] if gather[a] else ins[a].at[me], outs[a].at[me], local_sems.at[a]))
    return copies


def _riding(body, n_in, n_out, gather, grid):
    nr = len(gather)
    if not nr:
        return body

    def wrapped(*refs):
        ins, r_ins = refs[:n_in], refs[n_in:n_in + nr]
        outs, r_outs = refs[n_in + nr:n_in + nr + n_out], refs[n_in + nr + n_out:n_in + 2 * nr + n_out]
        scratch = refs[n_in + 2 * nr + n_out:]
        first = last = None
        for t, steps in enumerate(grid):
            pid = pl.program_id(t)
            first = (pid == 0) if first is None else first & (pid == 0)
            last = (pid == steps - 1) if last is None else last & (pid == steps - 1)
        copies = _exchange_copies(r_ins, r_outs, gather, *scratch[-3:])

        @pl.when(first)
        def _():
            for cp in copies:
                cp.start()

        body(*ins, *outs, *scratch[:-3])

        @pl.when(last)
        def _():
            for cp in copies:
                cp.wait()

    return wrapped


def _exchange(arrays, gather, name):
    n = len(arrays)

    def body(*refs):
        copies = _exchange_copies(refs[:n], refs[n:2 * n], gather, *refs[2 * n:])
        for cp in copies:
            cp.start()
        for cp in copies:
            cp.wait()

    return pl.pallas_call(
        body, out_shape=_exchange_shapes(arrays, gather), in_specs=[HBM_SPEC] * n, out_specs=[HBM_SPEC] * n,
        scratch_shapes=_exchange_sems(n), name=name)(*arrays)


def _sum_slots(x, name):
    _, R, C = x.shape
    tr = max(t for t in range(8, min(SUM_BLOCK_ELEMS // C, R) + 1, 8) if R % t == 0)

    def body(x_ref, o_ref):
        acc = x_ref[0].astype(F32)
        for s in range(1, N_DEV):
            acc = acc + x_ref[s].astype(F32)
        o_ref[...] = acc

    return pl.pallas_call(
        body, grid=(R // tr,), in_specs=[pl.BlockSpec((N_DEV, tr, C), lambda i: (0, i, 0))],
        out_specs=pl.BlockSpec((tr, C), lambda i: (i, 0)), out_shape=jax.ShapeDtypeStruct((R, C), F32),
        compiler_params=_cp("parallel"), name=name)(x)


def _mm(a, b, mode, tm, tn, tk, out_dtype, name, ride=None):
    if mode == NN:
        (M, K), N = a.shape, b.shape[1]
    elif mode == NT:
        (M, K), N = a.shape, b.shape[0]
    else:
        (K, M), N = a.shape, b.shape[1]
    tm, tn, tk = min(tm, M), min(tn, N), min(tk, K)
    assert M % tm == 0 and N % tn == 0 and K % tk == 0, (name, M, N, K, tm, tn, tk)
    nk = K // tk
    if mode == NN:
        a_spec = pl.BlockSpec((tm, tk), lambda i, j, k: (i, k))
        b_spec = pl.BlockSpec((tk, tn), lambda i, j, k: (k, j))
    elif mode == NT:
        a_spec = pl.BlockSpec((tm, tk), lambda i, j, k: (i, k))
        b_spec = pl.BlockSpec((tn, tk), lambda i, j, k: (j, k))
    else:
        a_spec = pl.BlockSpec((tk, tm), lambda i, j, k: (k, i))
        b_spec = pl.BlockSpec((tk, tn), lambda i, j, k: (k, j))

    ride_arrays, ride_gather = ride if ride else ([], [])
    nr = len(ride_arrays)
    grid = (M // tm, N // tn, nk)

    own_acc = nk > 1 and out_dtype != F32

    def body(a_ref, b_ref, o_ref, *acc):
        p = _dot(a_ref[...].astype(BF16), b_ref[...].astype(BF16), mode)
        if nk == 1:
            o_ref[...] = p.astype(out_dtype)
        else:
            acc_ref = acc[0] if own_acc else o_ref
            k = pl.program_id(2)

            @pl.when(k == 0)
            def _():
                acc_ref[...] = p

            @pl.when(k > 0)
            def _():
                acc_ref[...] += p

            if own_acc:
                @pl.when(k == nk - 1)
                def _():
                    o_ref[...] = acc_ref[...].astype(out_dtype)

    outs = pl.pallas_call(
        _riding(body, 2, 1, ride_gather, grid), grid=grid, in_specs=[a_spec, b_spec] + [HBM_SPEC] * nr,
        out_specs=[pl.BlockSpec((tm, tn), lambda i, j, k: (i, j))] + [HBM_SPEC] * nr,
        out_shape=[jax.ShapeDtypeStruct((M, N), out_dtype)] + _exchange_shapes(ride_arrays, ride_gather),
        scratch_shapes=([pltpu.VMEM((tm, tn), F32)] if own_acc else []) + (_exchange_sems(nr) if nr else []),
        compiler_params=_cp(*(("arbitrary",) * 3 if nr else ("parallel", "parallel", "arbitrary"))), name=name)(a, b, *ride_arrays)
    return (outs[0], outs[1:]) if nr else outs[0]


def _ada_fwd(c_all, w_shard, b_shard):
    Nc = w_shard.shape[1]

    def body(c_ref, w_ref, b_ref, o_ref):
        cv = c_ref[...]
        o_ref[...] = _dot(cv * _sigmoid(cv), w_ref[...], NN, HI) + b_ref[...]

    return pl.pallas_call(body, out_shape=jax.ShapeDtypeStruct((N_DEV, Nc), F32), name="ada_fwd",
                          compiler_params=pltpu.CompilerParams(vmem_limit_bytes=VMEM_LIMIT))(c_all, w_shard, b_shard)


def _ada_bwd(c_all, dmod_shard):
    D, Nc = c_all.shape[1], dmod_shard.shape[1]

    def body(c_ref, d_ref, o_ref):
        cv = c_ref[...]
        o_ref[...] = _dot(cv * _sigmoid(cv), d_ref[...], TN, HI)

    return pl.pallas_call(body, out_shape=jax.ShapeDtypeStruct((D, Nc), F32), name="ada_bwd",
                          compiler_params=pltpu.CompilerParams(vmem_limit_bytes=VMEM_LIMIT))(c_all, dmod_shard)


def _row_spec(ts, D):
    return pl.BlockSpec((ts, D), lambda i: (i, 0))


def _vec_spec(D):
    return pl.BlockSpec((1, D), lambda i: (0, 0))


def _col_spec(D, ts):
    return pl.BlockSpec((D, ts), lambda i: (0, i))


def _rms_mod(x, ng, sc, sh, ts, name):
    S, D = x.shape

    def body(x_ref, ng_ref, sc_ref, sh_ref, h_ref, ht_ref):
        xv = x_ref[...]
        r = lax.rsqrt(jnp.mean(xv * xv, axis=-1, keepdims=True) + EPS)
        h = xv * r * ng_ref[...] * (1.0 + sc_ref[...]) + sh_ref[...]
        h_ref[...] = h.astype(BF16)
        ht_ref[...] = h.T.astype(BF16)

    return pl.pallas_call(
        body, grid=(S // ts,), in_specs=[_row_spec(ts, D)] + [_vec_spec(D)] * 3, out_specs=[_row_spec(ts, D), _col_spec(D, ts)],
        out_shape=[jax.ShapeDtypeStruct((S, D), BF16), jax.ShapeDtypeStruct((D, S), BF16)],
        compiler_params=_cp("parallel"), name=name)(x, ng, sc, sh)


def _mm_rows(a, b, mode, tm, extras, extra_specs, out_shapes, out_specs, epilogue, name, ride=None):
    M, K = a.shape
    grid = (M // tm,)
    ride_arrays, ride_gather = ride if ride else ([], [])
    nr = len(ride_arrays)

    def body(a_ref, b_ref, *refs):
        epilogue(_dot(a_ref[...].astype(BF16), b_ref[...].astype(BF16), mode), pl.program_id(0), *refs)

    outs = pl.pallas_call(
        _riding(body, 2 + len(extras), len(out_shapes), ride_gather, grid), grid=grid,
        in_specs=[pl.BlockSpec((tm, K), lambda i: (i, 0)), pl.BlockSpec(b.shape, lambda i: (0, 0), pipeline_mode=pl.Buffered(1))]
        + list(extra_specs) + [HBM_SPEC] * nr,
        out_specs=list(out_specs) + [HBM_SPEC] * nr,
        out_shape=list(out_shapes) + _exchange_shapes(ride_arrays, ride_gather),
        scratch_shapes=_exchange_sems(nr) if nr else [],
        compiler_params=_cp("arbitrary"), name=name)(a, b, *extras, *ride_arrays)
    return outs[:len(out_shapes)], outs[len(out_shapes):]


def _accumulate(ref, part, step):
    @pl.when(step == 0)
    def _():
        ref[...] = part

    @pl.when(step > 0)
    def _():
        ref[...] += part


def _rows8(rows, width):
    return jnp.concatenate(rows + [jnp.zeros((8 - len(rows), width), F32)], axis=0)


def _mm_resid_rms_mod(a, w, x, g, ng, sc, sh, tm, name):
    S, D = x.shape

    def epilogue(t, step, x_ref, g_ref, ng_ref, sc_ref, sh_ref, t_ref, x2_ref, h_ref, ht_ref):
        t_ref[...] = t
        xv = x_ref[...] + g_ref[...] * t
        x2_ref[...] = xv
        r = lax.rsqrt(jnp.mean(xv * xv, axis=-1, keepdims=True) + EPS)
        h = xv * r * ng_ref[...] * (1.0 + sc_ref[...]) + sh_ref[...]
        h_ref[...] = h.astype(BF16)
        ht_ref[...] = h.T.astype(BF16)

    row, vec = _row_spec(tm, D), _vec_spec(D)
    full, half = jax.ShapeDtypeStruct((S, D), F32), jax.ShapeDtypeStruct((S, D), BF16)
    outs, _ = _mm_rows(a, w, NN, tm, [x, g, ng, sc, sh], [row] + [vec] * 4,
                       [full, full, half, jax.ShapeDtypeStruct((D, S), BF16)], [row, row, row, _col_spec(D, tm)], epilogue, name)
    return outs


def _mm_loss_resid(a, w, x2, g2, target, tm, name):
    S, D = x2.shape

    def epilogue(t, step, x_ref, y_ref, g_ref, dx_ref, dt_ref, sums_ref):
        gv = g_ref[...]
        e = x_ref[...] + gv * t - y_ref[...]
        dx = e * (1.0 / D)
        dx_ref[...] = dx
        dt_ref[...] = (dx * gv).astype(BF16)
        _accumulate(sums_ref, _rows8([jnp.sum(e * e, axis=0, keepdims=True), jnp.sum(dx * t, axis=0, keepdims=True)], D), step)

    row, vec = _row_spec(tm, D), _vec_spec(D)
    outs, _ = _mm_rows(a, w, NN, tm, [x2, target, g2], [row, row, vec],
                       [jax.ShapeDtypeStruct((S, D), F32), jax.ShapeDtypeStruct((S, D), BF16), jax.ShapeDtypeStruct((8, D), F32)],
                       [row, row, pl.BlockSpec((8, D), lambda i: (0, 0))], epilogue, name)
    return outs


def _mm_rms_mod_bwd(a, w, xin, dres, ng, sc, tm, name, t_prev=None, g_prev=None, ride=None):
    S, D = xin.shape
    chain = t_prev is not None

    def epilogue(dhv, step, *refs):
        if chain:
            x_ref, dr_ref, ng_ref, sc_ref, t_ref, g_ref, dx_ref, sums_ref, dt_ref = refs
        else:
            x_ref, dr_ref, ng_ref, sc_ref, dx_ref, sums_ref = refs
        xv = x_ref[...]
        r = lax.rsqrt(jnp.mean(xv * xv, axis=-1, keepdims=True) + EPS)
        xh = xv * r
        ngv, scv = ng_ref[...], sc_ref[...]
        dxh = dhv * (ngv * (1.0 + scv))
        dx = dr_ref[...] + r * (dxh - xh * jnp.mean(dxh * xh, axis=-1, keepdims=True))
        dx_ref[...] = dx
        dhx = dhv * xh
        rows = [jnp.sum(dhv, axis=0, keepdims=True), jnp.sum(dhx * ngv, axis=0, keepdims=True),
                jnp.sum(dhx * (1.0 + scv), axis=0, keepdims=True)]
        if chain:
            dt_ref[...] = (dx * g_ref[...]).astype(BF16)
            rows.append(jnp.sum(dx * t_ref[...], axis=0, keepdims=True))
        _accumulate(sums_ref, _rows8(rows, D), step)

    row, vec = _row_spec(tm, D), _vec_spec(D)
    extras = [xin, dres, ng, sc] + ([t_prev, g_prev] if chain else [])
    extra_specs = [row, row, vec, vec] + ([row, vec] if chain else [])
    out_shapes = [jax.ShapeDtypeStruct((S, D), F32), jax.ShapeDtypeStruct((8, D), F32)] + (
        [jax.ShapeDtypeStruct((S, D), BF16)] if chain else [])
    out_specs = [row, pl.BlockSpec((8, D), lambda i: (0, 0))] + ([row] if chain else [])
    return _mm_rows(a, w, NT, tm, extras, extra_specs, out_shapes, out_specs, epilogue, name, ride=ride)


def _gate_fwd(proj, wg_p, bg, ts):
    S = proj.shape[0]

    def body(glr_ref, w_ref, b_ref, la_ref):
        z = _dot(glr_ref[...], w_ref[...], NN, HI) + b_ref[...]
        la_ref[...] = (jnp.minimum(z, 0.0) - jnp.log(1.0 + jnp.exp(-jnp.abs(z)))) * (1.0 / GLA_TAU)

    return pl.pallas_call(
        body, grid=(S // ts,),
        in_specs=[pl.BlockSpec((ts, LANE), lambda i: (i, O_GLR // LANE)), pl.BlockSpec((LANE, GLA_QK), lambda i: (0, 0)),
                  pl.BlockSpec((1, GLA_QK), lambda i: (0, 0))],
        out_specs=pl.BlockSpec((ts, GLA_QK), lambda i: (i, 0)), out_shape=jax.ShapeDtypeStruct((S, GLA_QK), F32),
        compiler_params=_cp("parallel"), name="gla_gate_fwd")(proj, wg_p, bg)


def _gate_bwd(dla, la, proj, wg_p, ts):
    S = proj.shape[0]

    def body(dla_ref, la_ref, glr_ref, w_ref, dglr_ref, gw_ref, gb_ref):
        i = pl.program_id(0)
        dz = dla_ref[...] * (1.0 / GLA_TAU) * (1.0 - jnp.exp(GLA_TAU * la_ref[...]))
        dglr_ref[...] = _dot(dz, w_ref[...], NT, HI).astype(BF16)
        gw = _dot(glr_ref[...], dz, TN, HI)
        gb = jnp.concatenate([jnp.sum(dz, axis=0, keepdims=True), jnp.zeros((7, GLA_QK), F32)], axis=0)

        @pl.when(i == 0)
        def _():
            gw_ref[...] = gw
            gb_ref[...] = gb

        @pl.when(i > 0)
        def _():
            gw_ref[...] += gw
            gb_ref[...] += gb

    return pl.pallas_call(
        body, grid=(S // ts,),
        in_specs=[pl.BlockSpec((ts, GLA_QK), lambda i: (i, 0)), pl.BlockSpec((ts, GLA_QK), lambda i: (i, 0)),
                  pl.BlockSpec((ts, LANE), lambda i: (i, O_GLR // LANE)), pl.BlockSpec((LANE, GLA_QK), lambda i: (0, 0))],
        out_specs=[pl.BlockSpec((ts, LANE), lambda i: (i, 0)), pl.BlockSpec((LANE, GLA_QK), lambda i: (0, 0)),
                   pl.BlockSpec((8, GLA_QK), lambda i: (0, 0))],
        out_shape=[jax.ShapeDtypeStruct((S, LANE), BF16), jax.ShapeDtypeStruct((LANE, GLA_QK), F32),
                   jax.ShapeDtypeStruct((8, GLA_QK), F32)],
        compiler_params=_cp("arbitrary"), name="gla_gate_bwd")(dla, la, proj, wg_p)


def _tri(lower):
    r = lax.broadcasted_iota(jnp.int32, (GLA_CHUNK, GLA_CHUNK), 0)
    c = lax.broadcasted_iota(jnp.int32, (GLA_CHUNK, GLA_CHUNK), 1)
    return jnp.where((r >= c) if lower else (c >= r), 1.0, 0.0).astype(F32)


GLA_SUB = 16
GLA_NSUB = GLA_CHUNK // GLA_SUB
PAIR_QK = 2 * GLA_DK
PAIR_V = 2 * GLA_DV


def _band_selector():
    r = lax.broadcasted_iota(jnp.int32, (GLA_SUB * PAIR_QK, LANE), 0)
    c = lax.broadcasted_iota(jnp.int32, (GLA_SUB * PAIR_QK, LANE), 1)
    dist, head = r // PAIR_QK, (r % PAIR_QK) // GLA_DK
    return jnp.where(c == head * GLA_DK + (GLA_SUB - 1 - dist), 1.0, 0.0).astype(BF16)


def _flip_matrix():
    r = lax.broadcasted_iota(jnp.int32, (GLA_CHUNK, GLA_CHUNK), 0)
    c = lax.broadcasted_iota(jnp.int32, (GLA_CHUNK, GLA_CHUNK), 1)
    return jnp.where(r + c == GLA_CHUNK - 1, 1.0, 0.0).astype(BF16)


def _state_mask():
    r = lax.broadcasted_iota(jnp.int32, (PAIR_V, PAIR_QK), 0)
    c = lax.broadcasted_iota(jnp.int32, (PAIR_V, PAIR_QK), 1)
    return (r < GLA_DV) == (c < GLA_DK)


class _GlaChunk:
    def __init__(self, qs, kc, vc, g, sel):
        C = GLA_CHUNK
        self.qs, self.kc, self.vc = qs, kc, vc
        rows = lax.broadcasted_iota(jnp.int32, (C, 1), 0)
        lane = lax.broadcasted_iota(jnp.int32, (1, PAIR_QK), 1)
        self.rows, self.lane = rows, lane
        b = _dot(_tri(True), g, NN, HI)
        self.bl = b[C - 1:C, :]
        self.eb = jnp.exp(b)
        self.kdec = jnp.exp(self.bl - b)
        edge = lambda J: b[GLA_SUB * (J + 1):GLA_SUB * (J + 1) + 1, :]
        self.e_far = [jnp.exp(jnp.where(rows >= GLA_SUB * (J + 1), b - edge(J), NEG)) for J in range(GLA_NSUB - 1)]
        blk = rows // GLA_SUB
        bnext = edge(0)
        for J in range(1, GLA_NSUB - 1):
            bnext = jnp.where(blk == J, edge(J), bnext)
        self.e_khat = jnp.exp(jnp.where(blk < GLA_NSUB - 1, bnext - b, NEG))
        khat = kc * self.e_khat
        k2 = jnp.concatenate([jnp.where(lane < GLA_DK, khat, 0.0), jnp.where(lane >= GLA_DK, khat, 0.0)], axis=0)
        self.blk2 = jnp.concatenate([blk, blk], axis=0)
        self.m_far = jnp.concatenate([jnp.where(self.blk2 == J, k2, 0.0) for J in range(GLA_NSUB - 1)], axis=1).astype(BF16)
        self.qcat = jnp.concatenate([qs * e for e in self.e_far], axis=1).astype(BF16)
        a_far = _dot(self.qcat, self.m_far, NT)
        self.e_band, self.rk, terms = [], [], []
        for d in range(GLA_SUB):
            rk = pltpu.roll(kc, d, 0) if d else kc
            rb = pltpu.roll(b, d, 0) if d else b
            e = jnp.exp(jnp.where(rows >= d, b - rb, NEG))
            self.e_band.append(e)
            self.rk.append(rk)
            terms.append((qs * rk * e).astype(BF16))
        band = _dot(jnp.concatenate(terms, axis=1), sel, NN)
        a_band = pltpu.roll(band, LANE - (GLA_SUB - 1), 1, stride=1, stride_axis=0)
        dist = rows - lane % GLA_DK
        self.far_mask = dist >= GLA_SUB
        self.band_mask = (dist >= 0) & (dist < GLA_SUB)
        self.a = (a_band + jnp.where(self.far_mask, a_far, 0.0)).astype(BF16)
        self.lane_v = lax.broadcasted_iota(jnp.int32, (1, PAIR_V), 1)
        self.v2 = jnp.concatenate([jnp.where(self.lane_v < GLA_DV, vc, 0.0), jnp.where(self.lane_v >= GLA_DV, vc, 0.0)],
                                  axis=0).astype(BF16)


def _gla_fwd(proj, la, tb, ride=None):
    S = proj.shape[0]
    C = GLA_CHUNK
    tb = min(tb, S)
    nbc = tb // C
    npair = GLA_HEADS // 2
    scale = GLA_DK ** -0.5

    def body(q_ref, k_ref, v_ref, la_ref, sel_ref, o_ref, st_ref, state):
        @pl.when(pl.program_id(1) == 0)
        def _():
            state[...] = jnp.zeros_like(state)

        def chunk(ci, carry):
            sl = pl.ds(pl.multiple_of(ci * C, C), C)
            ch = _GlaChunk(q_ref[sl, :] * scale, k_ref[sl, :], v_ref[sl, :], la_ref[sl, :], sel_ref[...])
            st = state[...]
            st_ref[0, ci] = st
            o_ref[sl, :] = _dot((ch.qs * ch.eb).astype(BF16), st.astype(BF16), NT) + _dot(ch.a, ch.v2, NN)
            upd = _dot(ch.vc.astype(BF16), (ch.kc * ch.kdec).astype(BF16), TN)
            state[...] = st * jnp.exp(ch.bl) + jnp.where(_state_mask(), upd, 0.0)
            return carry

        lax.fori_loop(0, nbc, chunk, 0, unroll=8)

    qspec = lambda off: pl.BlockSpec((tb, PAIR_QK), lambda p, i: (i, off // PAIR_QK + p))
    ride_arrays, ride_gather = ride if ride else ([], [])
    nr = len(ride_arrays)
    grid = (npair, S // tb)
    outs = pl.pallas_call(
        _riding(body, 5, 2, ride_gather, grid), grid=grid,
        in_specs=[qspec(O_GQ), qspec(O_GK), pl.BlockSpec((tb, PAIR_V), lambda p, i: (i, O_GV // PAIR_V + p)),
                  pl.BlockSpec((tb, PAIR_QK), lambda p, i: (i, p)),
                  pl.BlockSpec((GLA_SUB * PAIR_QK, LANE), lambda p, i: (0, 0))] + [HBM_SPEC] * nr,
        out_specs=[pl.BlockSpec((tb, PAIR_V), lambda p, i: (i, p)),
                   pl.BlockSpec((1, nbc, PAIR_V, PAIR_QK), lambda p, i: (p, i, 0, 0))] + [HBM_SPEC] * nr,
        out_shape=[jax.ShapeDtypeStruct((S, GLA_V), F32), jax.ShapeDtypeStruct((npair, S // C, PAIR_V, PAIR_QK), F32)]
        + _exchange_shapes(ride_arrays, ride_gather),
        scratch_shapes=[pltpu.VMEM((PAIR_V, PAIR_QK), F32)] + (_exchange_sems(nr) if nr else []),
        compiler_params=_cp("arbitrary", "arbitrary"), name="gla_fwd")(proj, proj, proj, la, _band_selector(), *ride_arrays)
    return outs[0], outs[1], outs[2:]


def _gla_bwd(proj, la, do, states, tb, ride=None):
    S = proj.shape[0]
    C = GLA_CHUNK
    tb = min(tb, S)
    nbc = tb // C
    nblk = S // tb
    npair = GLA_HEADS // 2
    scale = GLA_DK ** -0.5

    def body(q_ref, k_ref, v_ref, la_ref, do_ref, st_ref, sel_ref, selt_ref, dq_ref, dk_ref, dv_ref, dla_ref, dstate):
        @pl.when(pl.program_id(1) == 0)
        def _():
            dstate[...] = jnp.zeros_like(dstate)

        def chunk(cc, carry):
            ci = nbc - 1 - cc
            sl = pl.ds(pl.multiple_of(ci * C, C), C)
            ch = _GlaChunk(q_ref[sl, :] * scale, k_ref[sl, :], v_ref[sl, :], la_ref[sl, :], sel_ref[...])
            qs, kc, rows = ch.qs, ch.kc, ch.rows
            doc_b = do_ref[sl, :].astype(BF16)
            st = st_ref[0, ci]
            dst = dstate[...]
            dst_b = dst.astype(BF16)
            ebl = jnp.exp(ch.bl)
            dq = _dot(doc_b, st.astype(BF16), NN) * ch.eb
            dk = _dot(ch.vc.astype(BF16), dst_b, NN) * ch.kdec
            dv = _dot((kc * ch.kdec).astype(BF16), dst_b, NT)
            dbl = jnp.sum(dst * st, axis=0, keepdims=True) * ebl + jnp.sum(kc * dk, axis=0, keepdims=True)
            da = _dot(doc_b, ch.v2, NT)
            dv2 = _dot(ch.a, doc_b, TN)
            dv = dv + jnp.where(ch.lane_v < GLA_DV, dv2[:C], dv2[C:])
            da_far = jnp.where(ch.far_mask, da, 0.0).astype(BF16)
            dqcat = _dot(da_far, ch.m_far, NN)
            dm = _dot(da_far, ch.qcat, TN)
            dk2 = jnp.zeros((2 * C, PAIR_QK), F32)
            for J in range(GLA_NSUB - 1):
                dq = dq + dqcat[:, J * PAIR_QK:(J + 1) * PAIR_QK] * ch.e_far[J]
                dk2 = dk2 + jnp.where(ch.blk2 == J, dm[:, J * PAIR_QK:(J + 1) * PAIR_QK], 0.0)
            dk = dk + jnp.where(ch.lane < GLA_DK, dk2[:C], dk2[C:]) * ch.e_khat
            flip = _flip_matrix()
            da_band = _dot(flip, jnp.where(ch.band_mask, da, 0.0).astype(BF16), NN)
            dband = pltpu.roll(da_band, LANE - (C - GLA_SUB), 1, stride=1, stride_axis=0)
            dband = _dot(flip, dband.astype(BF16), NN)
            dterms = _dot(dband.astype(BF16), selt_ref[...], NN)
            for d in range(GLA_SUB):
                dt = dterms[:, d * PAIR_QK:(d + 1) * PAIR_QK]
                dq = dq + dt * (ch.rk[d] * ch.e_band[d])
                dkr = dt * (qs * ch.e_band[d])
                dk = dk + (pltpu.roll(dkr, C - d, 0) if d else dkr)
            db = qs * dq - kc * dk
            db = jnp.where(rows == C - 1, db + dbl, db)
            dq_ref[sl, :] = (dq * scale).astype(BF16)
            dk_ref[sl, :] = dk.astype(BF16)
            dv_ref[sl, :] = dv.astype(BF16)
            dla_ref[sl, :] = _dot(_tri(False), db, NN, HI)
            upd = _dot(doc_b, (qs * ch.eb).astype(BF16), TN)
            dstate[...] = dst * ebl + jnp.where(_state_mask(), upd, 0.0)
            return carry

        lax.fori_loop(0, nbc, chunk, 0, unroll=8)

    rev = lambda i: nblk - 1 - i
    qspec = lambda off: pl.BlockSpec((tb, PAIR_QK), lambda p, i: (rev(i), off // PAIR_QK + p))
    pair_qk = pl.BlockSpec((tb, PAIR_QK), lambda p, i: (rev(i), p))
    pair_v = pl.BlockSpec((tb, PAIR_V), lambda p, i: (rev(i), p))
    sel = _band_selector()
    ride_arrays, ride_gather = ride if ride else ([], [])
    nr = len(ride_arrays)
    grid = (npair, nblk)
    outs = pl.pallas_call(
        _riding(body, 8, 4, ride_gather, grid), grid=grid,
        in_specs=[qspec(O_GQ), qspec(O_GK), pl.BlockSpec((tb, PAIR_V), lambda p, i: (rev(i), O_GV // PAIR_V + p)),
                  pair_qk, pair_v, pl.BlockSpec((1, nbc, PAIR_V, PAIR_QK), lambda p, i: (p, rev(i), 0, 0)),
                  pl.BlockSpec((GLA_SUB * PAIR_QK, LANE), lambda p, i: (0, 0)),
                  pl.BlockSpec((LANE, GLA_SUB * PAIR_QK), lambda p, i: (0, 0))] + [HBM_SPEC] * nr,
        out_specs=[pair_qk, pair_qk, pair_v, pair_qk] + [HBM_SPEC] * nr,
        out_shape=[jax.ShapeDtypeStruct((S, GLA_QK), BF16), jax.ShapeDtypeStruct((S, GLA_QK), BF16),
                   jax.ShapeDtypeStruct((S, GLA_V), BF16), jax.ShapeDtypeStruct((S, GLA_QK), F32)]
        + _exchange_shapes(ride_arrays, ride_gather),
        scratch_shapes=[pltpu.VMEM((PAIR_V, PAIR_QK), F32)] + (_exchange_sems(nr) if nr else []),
        compiler_params=_cp("arbitrary", "arbitrary"), name="gla_bwd")(proj, proj, proj, la, do, states, sel, sel.T, *ride_arrays)
    return outs[0], outs[1], outs[2], outs[3], outs[4:]


def _gla_out(o, proj, gng, ts):
    S = o.shape[0]

    def body(o_ref, gr_ref, g_ref, y_ref):
        for h in range(GLA_HEADS):
            cols = slice(h * GLA_DV, (h + 1) * GLA_DV)
            ov, grv = o_ref[:, cols], gr_ref[:, cols]
            r = lax.rsqrt(jnp.mean(ov * ov, axis=-1, keepdims=True) + EPS)
            y_ref[:, cols] = (ov * r * g_ref[...] * (grv * _sigmoid(grv))).astype(BF16)

    return pl.pallas_call(
        body, grid=(S // ts,),
        in_specs=[pl.BlockSpec((ts, GLA_V), lambda i: (i, 0)), pl.BlockSpec((ts, GLA_V), lambda i: (i, O_GR // GLA_V)),
                  pl.BlockSpec((1, GLA_DV), lambda i: (0, 0))],
        out_specs=pl.BlockSpec((ts, GLA_V), lambda i: (i, 0)), out_shape=jax.ShapeDtypeStruct((S, GLA_V), BF16),
        compiler_params=_cp("parallel"), name="gla_out_fwd")(o, proj, gng)


def _gla_out_bwd(dmixed, o, proj, gng, ts):
    S = o.shape[0]

    def body(dy_ref, o_ref, gr_ref, g_ref, do_ref, dgr_ref, gg_ref):
        i = pl.program_id(0)
        gsum = jnp.zeros((1, GLA_DV), F32)
        for h in range(GLA_HEADS):
            cols = slice(h * GLA_DV, (h + 1) * GLA_DV)
            ov, grv, dy = o_ref[:, cols], gr_ref[:, cols], dy_ref[:, cols]
            r = lax.rsqrt(jnp.mean(ov * ov, axis=-1, keepdims=True) + EPS)
            oh = ov * r
            sg = _sigmoid(grv)
            silu = grv * sg
            don = dy * silu
            dgr_ref[:, cols] = (dy * (oh * g_ref[...]) * (sg * (1.0 + grv * (1.0 - sg)))).astype(BF16)
            gsum = gsum + jnp.sum(don * oh, axis=0, keepdims=True)
            doh = don * g_ref[...]
            do_ref[:, cols] = r * (doh - oh * jnp.mean(doh * oh, axis=-1, keepdims=True))
        part = jnp.concatenate([gsum, jnp.zeros((7, GLA_DV), F32)], axis=0)

        @pl.when(i == 0)
        def _():
            gg_ref[...] = part

        @pl.when(i > 0)
        def _():
            gg_ref[...] += part

    return pl.pallas_call(
        body, grid=(S // ts,),
        in_specs=[pl.BlockSpec((ts, GLA_V), lambda i: (i, 0)), pl.BlockSpec((ts, GLA_V), lambda i: (i, 0)),
                  pl.BlockSpec((ts, GLA_V), lambda i: (i, O_GR // GLA_V)), pl.BlockSpec((1, GLA_DV), lambda i: (0, 0))],
        out_specs=[pl.BlockSpec((ts, GLA_V), lambda i: (i, 0)), pl.BlockSpec((ts, GLA_V), lambda i: (i, 0)),
                   pl.BlockSpec((8, GLA_DV), lambda i: (0, 0))],
        out_shape=[jax.ShapeDtypeStruct((S, GLA_V), F32), jax.ShapeDtypeStruct((S, GLA_V), BF16),
                   jax.ShapeDtypeStruct((8, GLA_DV), F32)],
        compiler_params=_cp("arbitrary"), name="gla_out_bwd")(dmixed, o, proj, gng)


def _seg_matrix(width, seg, value):
    r = lax.broadcasted_iota(jnp.int32, (width, width), 0) // seg
    c = lax.broadcasted_iota(jnp.int32, (width, width), 1) // seg
    return jnp.where(r == c, value, 0.0).astype(BF16)


def _seg_sum(x, seg_matrix):
    hi = x.astype(BF16)
    lo = (x - hi.astype(F32)).astype(BF16)
    return _dot(hi, seg_matrix, NN) + _dot(lo, seg_matrix, NN)


def _head_norm(proj, qg, kg, ts):
    S = proj.shape[0]
    W = ATTN_DIM

    def body(q_ref, k_ref, qg_ref, kg_ref, qn_ref, kn_ref):
        seg = _seg_matrix(W, ATTN_HD, 1.0 / ATTN_HD)
        for x_ref, g_ref, o_ref, scale in ((q_ref, qg_ref, qn_ref, ATTN_HD ** -0.5), (k_ref, kg_ref, kn_ref, 1.0)):
            xv = x_ref[...]
            ms = _seg_sum(xv * xv, seg)
            o_ref[...] = xv * lax.rsqrt(ms + EPS) * (g_ref[...] * scale)

    blk = lambda off: pl.BlockSpec((ts, W), lambda i: (i, off // W))
    out = pl.BlockSpec((ts, W), lambda i: (i, 0))
    vec = pl.BlockSpec((1, W), lambda i: (0, 0))
    return pl.pallas_call(
        body, grid=(S // ts,), in_specs=[blk(O_AQ), blk(O_AK), vec, vec], out_specs=[out] * 2,
        out_shape=[jax.ShapeDtypeStruct((S, W), F32)] * 2, compiler_params=_cp("parallel"), name="attn_head_norm")(
            proj, proj, qg, kg)


def _slope(head):
    one = jnp.ones((1, 1), jnp.int32)
    return 1.0 / jnp.left_shift(one, one * (head + 1)).astype(F32)


ATTN_GROUP = 4


def _attn_rows(d, g, r):
    start = g * d * ATTN_BLOCK + r
    return pl.ds(start, ATTN_BLOCK) if d == 1 else pl.ds(start, ATTN_BLOCK, stride=d)


def _for_blocks(d, G, fn):
    for g in range(G):
        if d <= ATTN_GROUP:
            for r in range(d):
                fn(g, r)
        else:
            def step(r, carry, g=g):
                fn(g, r)
                return carry
            lax.fori_loop(0, d, step, 0, unroll=ATTN_GROUP)


def _attn_specs(d, S):
    G = max(1, ATTN_GROUP // d)
    edge = d * ATTN_BLOCK
    tq = G * edge
    nb, n_edge = S // tq, S // edge

    def specs(off=0):
        return [pl.BlockSpec((tq, LANE), lambda hp, n: (n, off + hp)),
                pl.BlockSpec((edge, LANE), lambda hp, n: (jnp.maximum(n * G - 1, 0), off + hp)),
                pl.BlockSpec((edge, LANE), lambda hp, n: (jnp.minimum((n + 1) * G, n_edge - 1), off + hp))]

    return G, nb, specs


def _attn_bias(d, hp, first_tile):
    B = ATTN_BLOCK
    iq = lax.broadcasted_iota(jnp.int32, (B, 2 * B), 0)
    ik = lax.broadcasted_iota(jnp.int32, (B, 2 * B), 1)
    rel = iq + B - ik
    window = (rel >= 0) & (rel <= B)
    relf = (d * rel).astype(F32)
    full = [jnp.where(window, -_slope(hp * 2 + h) * relf, NEG) for h in range(2)]
    edge = [jnp.where((ik >= B) | jnp.logical_not(first_tile), b, NEG) for b in full]
    return full, edge


def _attn_bias_t(d, hp, has_next):
    B = ATTN_BLOCK
    ik = lax.broadcasted_iota(jnp.int32, (B, B), 0)
    iq = lax.broadcasted_iota(jnp.int32, (B, B), 1)
    tiles = []
    for nxt in range(2):
        rel = iq - ik + nxt * B
        window = (rel >= 0) & (rel <= B)
        relf = (d * rel).astype(F32)
        tiles.append([jnp.where(window, -_slope(hp * 2 + h) * relf, NEG) for h in range(2)])
    tiles.append([jnp.where(has_next, b, NEG) for b in tiles[1]])
    return tiles


def _attn_fwd(qn, kn, proj, d):
    S, W = qn.shape
    G, nb, specs = _attn_specs(d, S)

    def body(q_ref, kp_ref, kc_ref, vp_ref, vc_ref, o_ref, l_ref):
        hp, n = pl.program_id(0), pl.program_id(1)
        lo = lax.broadcasted_iota(jnp.int32, (1, LANE), 1) < ATTN_HD
        full, edge = _attn_bias(d, hp, n == 0)

        def sub(g, r):
            rows = _attn_rows(d, g, r)
            before = _attn_rows(d, max(g - 1, 0), r)
            kb_ref, vb_ref = (kp_ref, vp_ref) if g == 0 else (kc_ref, vc_ref)
            bias = edge if g == 0 else full
            qv = q_ref[rows, :].astype(BF16)
            kv = jnp.concatenate([kb_ref[before, :], kc_ref[rows, :]], axis=0).astype(BF16)
            vv = jnp.concatenate([vb_ref[before, :], vc_ref[rows, :]], axis=0).astype(BF16)
            outs, lses = [], []
            for h in range(2):
                qm = jnp.where(lo == (h == 0), qv, jnp.zeros_like(qv))
                s = _dot(qm, kv, NT) + bias[h]
                m = jnp.max(s, axis=-1, keepdims=True)
                p = jnp.exp(s - m)
                den = jnp.sum(p, axis=-1, keepdims=True)
                outs.append(_dot(p.astype(BF16), vv, NN) / den)
                lses.append(m + jnp.log(den))
            o_ref[rows, :] = jnp.where(lo, outs[0], outs[1])
            l_ref[rows, :] = jnp.where(lo, lses[0], lses[1])

        _for_blocks(d, G, sub)

    cur, prev, _ = specs()
    vcur, vprev, _ = specs(O_AV // LANE)
    return pl.pallas_call(
        body, grid=(W // LANE, nb), in_specs=[cur, prev, cur, vprev, vcur], out_specs=[cur, cur],
        out_shape=[jax.ShapeDtypeStruct((S, W), F32)] * 2,
        compiler_params=_cp("parallel", "arbitrary"), name=f"attn_fwd_d{d}")(qn, kn, kn, proj, proj)


def _attn_merge(y_gla, os_, ls_, ts):
    S, W = os_[0].shape

    def body(yg, o1, o2, o3, l1, l2, l3, mixed_ref, mixed_t_ref, y_ref, lse_ref):
        a, b, c = l1[...], l2[...], l3[...]
        m = jnp.maximum(jnp.maximum(a, b), c)
        ea, eb, ec = jnp.exp(a - m), jnp.exp(b - m), jnp.exp(c - m)
        tot = ea + eb + ec
        y = (ea * o1[...] + eb * o2[...] + ec * o3[...]) / tot
        y_ref[...] = y
        mixed_ref[:, :W] = yg[...]
        mixed_ref[:, W:] = y.astype(BF16)
        mixed_t_ref[:W, :] = yg[...].astype(F32).T.astype(BF16)
        mixed_t_ref[W:, :] = y.T.astype(BF16)
        lse_ref[...] = m + jnp.log(tot)

    spec = pl.BlockSpec((ts, W), lambda i: (i, 0))
    return pl.pallas_call(
        body, grid=(S // ts,), in_specs=[spec] * 7,
        out_specs=[pl.BlockSpec((ts, 2 * W), lambda i: (i, 0)), _col_spec(2 * W, ts), spec, spec],
        out_shape=[jax.ShapeDtypeStruct((S, 2 * W), BF16), jax.ShapeDtypeStruct((2 * W, S), BF16),
                   jax.ShapeDtypeStruct((S, W), F32), jax.ShapeDtypeStruct((S, W), F32)],
        compiler_params=_cp("parallel"), name="attn_merge")(y_gla, *os_, *ls_)


def _attn_delta(dmixed, y, ts):
    S, W = y.shape

    def body(dy_ref, y_ref, d_ref):
        d_ref[...] = _seg_sum(dy_ref[...] * y_ref[...], _seg_matrix(W, ATTN_HD, 1.0))

    return pl.pallas_call(
        body, grid=(S // ts,), in_specs=[pl.BlockSpec((ts, W), lambda i: (i, 1)), pl.BlockSpec((ts, W), lambda i: (i, 0))],
        out_specs=pl.BlockSpec((ts, W), lambda i: (i, 0)), out_shape=jax.ShapeDtypeStruct((S, W), F32),
        compiler_params=_cp("parallel"), name="attn_delta")(dmixed, y)


def _attn_dq(qn, kn, proj, dmixed, lse, delta, d):
    S, W = qn.shape
    B = ATTN_BLOCK
    G, nb, specs = _attn_specs(d, S)

    def body(q_ref, kp_ref, kc_ref, vp_ref, vc_ref, dy_ref, l_ref, de_ref, dq_ref):
        hp, n = pl.program_id(0), pl.program_id(1)
        lo = lax.broadcasted_iota(jnp.int32, (1, LANE), 1) < ATTN_HD
        full, edge = _attn_bias(d, hp, n == 0)

        def sub(g, r):
            rows = _attn_rows(d, g, r)
            before = _attn_rows(d, max(g - 1, 0), r)
            kb_ref, vb_ref = (kp_ref, vp_ref) if g == 0 else (kc_ref, vc_ref)
            bias = edge if g == 0 else full
            qv, dyv = q_ref[rows, :].astype(BF16), dy_ref[rows, :]
            lv, dev = l_ref[rows, :], de_ref[rows, :]
            kv = jnp.concatenate([kb_ref[before, :], kc_ref[rows, :]], axis=0).astype(BF16)
            vv = jnp.concatenate([vb_ref[before, :], vc_ref[rows, :]], axis=0).astype(BF16)
            outs = []
            for h in range(2):
                sel = lo == (h == 0)
                qm = jnp.where(sel, qv, jnp.zeros_like(qv))
                dym = jnp.where(sel, dyv, 0.0).astype(BF16)
                lse_h = lv[:, h * ATTN_HD:h * ATTN_HD + 1]
                del_h = dev[:, h * ATTN_HD:h * ATTN_HD + 1]
                p = jnp.exp(_dot(qm, kv, NT) + bias[h] - lse_h)
                ds = p * (_dot(dym, vv, NT) - del_h)
                outs.append(_dot(ds.astype(BF16), kv, NN) * (ATTN_HD ** -0.5))
            dq_ref[rows, :] = jnp.where(lo, outs[0], outs[1])

        _for_blocks(d, G, sub)

    cur, prev, _ = specs()
    vcur, vprev, _ = specs(O_AV // LANE)
    dycur, _, _ = specs(W // LANE)
    return pl.pallas_call(
        body, grid=(W // LANE, nb), in_specs=[cur, prev, cur, vprev, vcur, dycur, cur, cur], out_specs=cur,
        out_shape=jax.ShapeDtypeStruct((S, W), F32),
        compiler_params=_cp("parallel", "arbitrary"), name=f"attn_dq_d{d}")(qn, kn, kn, proj, proj, dmixed, lse, delta)


def _attn_dkv(qn, kn, proj, dmixed, lse, delta, d):
    S, W = qn.shape
    B = ATTN_BLOCK
    G, nb, specs = _attn_specs(d, S)

    def body(k_ref, v_ref, qc_ref, qn_ref, dyc_ref, dyn_ref, lc_ref, ln_ref, dec_ref, den_ref, dk_ref, dv_ref):
        hp, n = pl.program_id(0), pl.program_id(1)
        lo = lax.broadcasted_iota(jnp.int32, (1, LANE), 1) < ATTN_HD
        own, inner, outer = _attn_bias_t(d, hp, n + 1 < nb)

        def sub(g, r):
            rows = _attn_rows(d, g, r)
            kv, vv = k_ref[rows, :].astype(BF16), v_ref[rows, :].astype(BF16)
            dk = jnp.zeros((B, LANE), F32)
            dv = jnp.zeros((B, LANE), F32)
            inside = g + 1 < G
            after = _attn_rows(d, g + 1 if inside else 0, r)
            following = (qc_ref, dyc_ref, lc_ref, dec_ref) if inside else (qn_ref, dyn_ref, ln_ref, den_ref)
            for bias, qrows, (q_ref, dy_ref, l_ref, de_ref) in (
                    (own, rows, (qc_ref, dyc_ref, lc_ref, dec_ref)), (inner if inside else outer, after, following)):
                qv, dyv = q_ref[qrows, :].astype(BF16), dy_ref[qrows, :]
                lt, det = l_ref[qrows, :].T, de_ref[qrows, :].T
                for h in range(2):
                    sel = lo == (h == 0)
                    qm = jnp.where(sel, qv, jnp.zeros_like(qv))
                    dym = jnp.where(sel, dyv, 0.0).astype(BF16)
                    lse_h = lt[h * ATTN_HD:h * ATTN_HD + 1, :]
                    del_h = det[h * ATTN_HD:h * ATTN_HD + 1, :]
                    pt = jnp.exp(_dot(kv, qm, NT) + bias[h] - lse_h)
                    dv = dv + _dot(pt.astype(BF16), dym, NN)
                    dst = pt * (_dot(vv, dym, NT) - del_h)
                    dk = dk + _dot(dst.astype(BF16), qm, NN)
            dk_ref[rows, :] = dk
            dv_ref[rows, :] = dv

        _for_blocks(d, G, sub)

    cur, _, nxt = specs()
    vcur, _, _ = specs(O_AV // LANE)
    dycur, _, dynxt = specs(W // LANE)
    return pl.pallas_call(
        body, grid=(W // LANE, nb), in_specs=[cur, vcur, cur, nxt, dycur, dynxt, cur, nxt, cur, nxt], out_specs=[cur, cur],
        out_shape=[jax.ShapeDtypeStruct((S, W), F32)] * 2,
        compiler_params=_cp("parallel", "arbitrary"), name=f"attn_dkv_d{d}")(
            kn, proj, qn, qn, dmixed, dmixed, lse, lse, delta, delta)


def _attn_post(dqs, dks, dvs, proj, qg, kg, ts):
    S = proj.shape[0]
    W = ATTN_DIM

    def body(dq1, dq2, dq3, dk1, dk2, dk3, dv1, dv2, dv3, aq_ref, ak_ref, qg_ref, kg_ref, daq_ref, dak_ref, dav_ref, gg_ref):
        i = pl.program_id(0)
        seg = _seg_matrix(W, ATTN_HD, 1.0 / ATTN_HD)
        gsums = []
        for (d1, d2, d3), x_ref, g_ref, o_ref in (((dq1, dq2, dq3), aq_ref, qg_ref, daq_ref), ((dk1, dk2, dk3), ak_ref, kg_ref, dak_ref)):
            dy = d1[...] + d2[...] + d3[...]
            xv = x_ref[...]
            r = lax.rsqrt(_seg_sum(xv * xv, seg) + EPS)
            xh = xv * r
            dxh = dy * g_ref[...]
            o_ref[...] = (r * (dxh - xh * _seg_sum(dxh * xh, seg))).astype(BF16)
            gsums.append(jnp.sum(dy * xh, axis=0, keepdims=True))
        dav_ref[...] = (dv1[...] + dv2[...] + dv3[...]).astype(BF16)
        part = jnp.concatenate(gsums + [jnp.zeros((6, W), F32)], axis=0)

        @pl.when(i == 0)
        def _():
            gg_ref[...] = part

        @pl.when(i > 0)
        def _():
            gg_ref[...] += part

    row = pl.BlockSpec((ts, W), lambda i: (i, 0))
    blk = lambda off: pl.BlockSpec((ts, W), lambda i: (i, off // W))
    vec = pl.BlockSpec((1, W), lambda i: (0, 0))
    return pl.pallas_call(
        body, grid=(S // ts,), in_specs=[row] * 9 + [blk(O_AQ), blk(O_AK), vec, vec],
        out_specs=[row, row, row, pl.BlockSpec((8, W), lambda i: (0, 0))],
        out_shape=[jax.ShapeDtypeStruct((S, W), BF16)] * 3 + [jax.ShapeDtypeStruct((8, W), F32)],
        compiler_params=_cp("arbitrary"), name="attn_post")(*dqs, *dks, *dvs, proj, proj, qg, kg)


def _shift_down(cur, halo, n):
    return pltpu.roll(jnp.concatenate([halo, cur], axis=0), n, 0)[8:]


def _shift_up(cur, halo, n):
    ts = cur.shape[0]
    return pltpu.roll(jnp.concatenate([cur, halo], axis=0), ts + 8 - n, 0)[:ts]


def _conv(cur, halo, w, b):
    return b + w[0:1, :] * _shift_down(cur, halo, 2) + w[1:2, :] * _shift_down(cur, halo, 1) + w[2:3, :] * cur


def _conv_swiglu(u0, conv_w8, conv_b, ts, tc):
    S, F2 = u0.shape
    F = F2 // 2
    nc = F // tc
    hb = ts // 8

    def body(ug_ref, ugh_ref, uv_ref, uvh_ref, wg_ref, wv_ref, bg_ref, bv_ref, a_ref, at_ref):
        first = pl.program_id(0) == 0
        ugh = jnp.where(first, 0.0, ugh_ref[...])
        uvh = jnp.where(first, 0.0, uvh_ref[...])
        g = _conv(ug_ref[...], ugh, wg_ref[...], bg_ref[...])
        v = _conv(uv_ref[...], uvh, wv_ref[...], bv_ref[...])
        a = g * _sigmoid(g) * v
        a_ref[...] = a.astype(BF16)
        at_ref[...] = a.T.astype(BF16)

    main = lambda off: pl.BlockSpec((ts, tc), lambda i, j: (i, j + off))
    halo = lambda off: pl.BlockSpec((8, tc), lambda i, j: (jnp.maximum(i * hb - 1, 0), j + off))
    wspec = lambda off: pl.BlockSpec((8, tc), lambda i, j: (0, j + off))
    bspec = lambda off: pl.BlockSpec((1, tc), lambda i, j: (0, j + off))
    return pl.pallas_call(
        body, grid=(S // ts, nc),
        in_specs=[main(0), halo(0), main(nc), halo(nc), wspec(0), wspec(nc), bspec(0), bspec(nc)],
        out_specs=[pl.BlockSpec((ts, tc), lambda i, j: (i, j)), pl.BlockSpec((tc, ts), lambda i, j: (j, i))],
        out_shape=[jax.ShapeDtypeStruct((S, F), BF16), jax.ShapeDtypeStruct((F, S), BF16)],
        compiler_params=_cp("parallel", "parallel"), name="conv_swiglu")(u0, u0, u0, u0, conv_w8, conv_w8, conv_b, conv_b)


def _ffn_du(da, u0, conv_w8, conv_b, ts, tc, ride=None):
    S, F2 = u0.shape
    F = F2 // 2
    nc = F // tc
    hb = ts // 8
    grid = (nc, S // ts)
    ride_arrays, ride_gather = ride if ride else ([], [])
    nr = len(ride_arrays)

    def body(da_ref, ug_ref, ugh_ref, uv_ref, uvh_ref, wg_ref, wv_ref, bg_ref, bv_ref, du_ref, sg_ref, sv_ref):
        i = pl.program_id(1)
        first = i == 0
        halves = []
        for u_ref, h_ref, w_ref, b_ref in ((ug_ref, ugh_ref, wg_ref, bg_ref), (uv_ref, uvh_ref, wv_ref, bv_ref)):
            u, halo, w = u_ref[...], jnp.where(first, 0.0, h_ref[...]), w_ref[...]
            s2, s1 = _shift_down(u, halo, 2), _shift_down(u, halo, 1)
            halves.append((b_ref[...] + w[0:1, :] * s2 + w[1:2, :] * s1 + w[2:3, :] * u, s2, s1, u))
        g, v = halves[0][0], halves[1][0]
        dav = da_ref[...]
        sig = _sigmoid(g)
        dus = (dav * v * (sig * (1.0 + g * (1.0 - sig))), dav * (g * sig))
        for h, (du, sums_ref) in enumerate(zip(dus, (sg_ref, sv_ref))):
            du_ref[h] = du
            _, s2, s1, u = halves[h]
            part = jnp.concatenate([jnp.sum(du * s2, axis=0, keepdims=True), jnp.sum(du * s1, axis=0, keepdims=True),
                                    jnp.sum(du * u, axis=0, keepdims=True), jnp.sum(du, axis=0, keepdims=True),
                                    jnp.zeros((4, tc), F32)], axis=0)

            @pl.when(first)
            def _(sums_ref=sums_ref, part=part):
                sums_ref[...] = part

            @pl.when(i > 0)
            def _(sums_ref=sums_ref, part=part):
                sums_ref[...] += part

    main = lambda off: pl.BlockSpec((ts, tc), lambda j, i: (i, j + off))
    halo = lambda off: pl.BlockSpec((8, tc), lambda j, i: (jnp.maximum(i * hb - 1, 0), j + off))
    wspec = lambda off: pl.BlockSpec((8, tc), lambda j, i: (0, j + off))
    bspec = lambda off: pl.BlockSpec((1, tc), lambda j, i: (0, j + off))
    sums_spec = pl.BlockSpec((8, tc), lambda j, i: (0, j))
    outs = pl.pallas_call(
        _riding(body, 9, 3, ride_gather, grid), grid=grid,
        in_specs=[main(0), main(0), halo(0), main(nc), halo(nc), wspec(0), wspec(nc), bspec(0), bspec(nc)] + [HBM_SPEC] * nr,
        out_specs=[pl.BlockSpec((2, ts, tc), lambda j, i: (0, i, j)), sums_spec, sums_spec] + [HBM_SPEC] * nr,
        out_shape=[jax.ShapeDtypeStruct((2, S, F), F32), jax.ShapeDtypeStruct((8, F), F32), jax.ShapeDtypeStruct((8, F), F32)]
        + _exchange_shapes(ride_arrays, ride_gather),
        scratch_shapes=_exchange_sems(nr) if nr else [],
        compiler_params=_cp("arbitrary", "arbitrary"), name="ffn_du")(
            da, u0, u0, u0, u0, conv_w8, conv_w8, conv_b, conv_b, *ride_arrays)
    return outs[0], outs[1], outs[2], outs[3:]


def _ffn_du0(du, conv_w8, ts, tc):
    _, S, F = du.shape
    nc = F // tc
    hb = ts // 8
    nrow = S // ts

    def body(du_ref, duh_ref, w_ref, o_ref):
        last = pl.program_id(0) == nrow - 1
        cur, halo, w = du_ref[...], jnp.where(last, 0.0, duh_ref[...]), w_ref[...]
        o_ref[...] = (w[2:3, :] * cur + w[1:2, :] * _shift_up(cur, halo, 1) + w[0:1, :] * _shift_up(cur, halo, 2)).astype(BF16)

    return pl.pallas_call(
        body, grid=(nrow, 2, nc),
        in_specs=[pl.BlockSpec((None, ts, tc), lambda i, h, j: (h, i, j)),
                  pl.BlockSpec((None, 8, tc), lambda i, h, j: (h, jnp.minimum((i + 1) * hb, S // 8 - 1), j)),
                  pl.BlockSpec((8, tc), lambda i, h, j: (0, h * nc + j))],
        out_specs=pl.BlockSpec((ts, tc), lambda i, h, j: (i, h * nc + j)), out_shape=jax.ShapeDtypeStruct((S, 2 * F), BF16),
        compiler_params=_cp("parallel", "parallel", "parallel"), name="ffn_du0")(du, du, conv_w8)


def _adamw(w, g, m, v, name):
    shape = w.shape
    view = (math.prod(shape[:-1]), shape[-1])
    R, C = view
    fits = [t for t in range(8, R + 1, 8) if R % t == 0 and t * C <= SUM_BLOCK_ELEMS]
    tr = max(fits) if fits else R

    def body(w_ref, g_ref, m_ref, v_ref, d_ref, nm_ref, nv_ref):
        gv = g_ref[...]
        nm = ADAM_B1 * m_ref[...] + (1.0 - ADAM_B1) * gv
        nv = ADAM_B2 * v_ref[...] + (1.0 - ADAM_B2) * (gv * gv)
        m_hat = nm / (1.0 - ADAM_B1 ** ADAM_STEP)
        v_hat = nv / (1.0 - ADAM_B2 ** ADAM_STEP)
        d_ref[...] = -ADAM_LR * (m_hat / (jnp.sqrt(v_hat) + ADAM_EPS) + ADAM_WD * w_ref[...])
        nm_ref[...] = nm
        nv_ref[...] = nv

    spec = pl.BlockSpec((tr, C), lambda i: (i, 0))
    outs = pl.pallas_call(
        body, grid=(R // tr,), in_specs=[spec] * 4, out_specs=[spec] * 3, out_shape=[jax.ShapeDtypeStruct(view, F32)] * 3,
        compiler_params=_cp("parallel"), name=name)(*[a.reshape(view) for a in (w, g, m, v)])
    return [o.reshape(shape) for o in outs]


def _pad_rows8(a):
    return jnp.concatenate([a, jnp.zeros((8 - a.shape[0], a.shape[1]), a.dtype)], axis=0)


def _local_step(x, target, mod, n1g, w_in_p, wg_p, bg, gng, qng, kng, w_out_s, n2g, w_up_s, conv_w, conv_b, w_down_s):
    S, D = x.shape
    F = w_down_s.shape[0] * N_DEV
    ts = min(512, S)
    sh1, sc1, g1, sh2, sc2, g2 = [mod[i:i + 1] for i in range(6)]
    conv_w8 = _pad_rows8(conv_w)
    qg_t, kg_t = jnp.tile(qng, (1, ATTN_HEADS)), jnp.tile(kng, (1, ATTN_HEADS))

    h1, h1_t = _rms_mod(x, n1g, sc1, sh1, ts, "rms_mod1")
    proj, (g_out,) = _mm(h1, w_in_p, NN, 512, PROJ_W, 1024, F32, "mm_in", ride=([w_out_s], [True]))
    w_out = g_out.reshape(-1, D)
    la = _gate_fwd(proj, wg_p, bg, ts)
    o_gla, states, (g_up,) = _gla_fwd(proj, la, 512, ride=([w_up_s], [True]))
    w_up = _cols_from_blocks(g_up)
    y_gla = _gla_out(o_gla, proj, gng, ts)
    qn, kn = _head_norm(proj, qg_t, kg_t, ts)
    branches = [_attn_fwd(qn, kn, proj, d) for d in DILATIONS]
    mixed, mixed_t, y_att, lse = _attn_merge(y_gla, [b[0] for b in branches], [b[1] for b in branches], ts)
    t1, x2, h2, h2_t = _mm_resid_rms_mod(mixed, w_out, x, g1, n2g, sc2, sh2, ts, "mm_out")
    u0, (g_down,) = _mm(h2, w_up, NN, 512, 2816, 1024, F32, "mm_up", ride=([w_down_s], [True]))
    w_down = g_down.reshape(F, D)
    tc = 1408 if F % 1408 == 0 else F
    a, a_t = _conv_swiglu(u0, conv_w8, conv_b, min(256, S), tc)
    dx3, dt2, sums3 = _mm_loss_resid(a, w_down, x2, g2, target, ts, "mm_down")
    loss_row, dg2 = sums3[0:1], sums3[1:2]

    g_w_down = _mm(a_t, dt2, NN, 1408, 1024, 2048, F32, "mm_gw_down")
    da = _mm(dt2, w_down, NT, 512, 2816, 1024, F32, "mm_da")
    du, sums_g, sums_v, (r_down,) = _ffn_du(da, u0, conv_w8, conv_b, min(256, S), tc,
                                            ride=([g_w_down.reshape(N_DEV, -1, D)], [False]))
    g_conv_w = jnp.concatenate([sums_g[0:3], sums_v[0:3]], axis=1)
    g_conv_b = jnp.concatenate([sums_g[3:4], sums_v[3:4]], axis=1)
    du0 = _ffn_du0(du, conv_w8, min(256, S), tc)
    g_w_up = _mm(h2_t, du0, NN, 512, 2816, 2048, F32, "mm_gw_up")
    (dx2, sums2, dt1), _ = _mm_rms_mod_bwd(du0, w_up, x2, dx3, n2g, sc2, ts, "mm_dh2", t_prev=t1, g_prev=g1)
    dsh2, dsc2, g_n2g, dg1 = sums2[0:1], sums2[1:2], sums2[2:3], sums2[3:4]
    g_w_out = _mm(mixed_t, dt1, NN, 1024, 1024, 2048, F32, "mm_gw_out")
    dmixed = _mm(dt1, w_out, NT, 512, 1024, 1024, F32, "mm_dmixed")
    do_gla, dgr, gng_sums = _gla_out_bwd(dmixed, o_gla, proj, gng, ts)
    dgq, dgk, dgv, dla, (r_up, r_out) = _gla_bwd(
        proj, la, do_gla, states, 512, ride=([_col_blocks(g_w_up), g_w_out.reshape(N_DEV, -1, D)], [False, False]))
    dglr, g_wg_p, gb_sums = _gate_bwd(dla, la, proj, wg_p, ts)
    delta = _attn_delta(dmixed, y_att, ts)
    dqs = [_attn_dq(qn, kn, proj, dmixed, lse, delta, d) for d in DILATIONS]
    dkvs = [_attn_dkv(qn, kn, proj, dmixed, lse, delta, d) for d in DILATIONS]
    daq, dak, dav, qk_sums = _attn_post(dqs, [t[0] for t in dkvs], [t[1] for t in dkvs], proj, qg_t, kg_t, ts)
    dproj = jnp.concatenate([dgq, dgk, dgv, dgr, daq, dak, dav, dglr, jnp.zeros((S, PROJ_W - O_GLR - LANE), BF16)], axis=1)
    g_w_in_p = _mm(h1_t, dproj, NN, 512, PROJ_W, 1024, F32, "mm_gw_in")
    g_w_in = jnp.concatenate([g_w_in_p[:, :GLR_SRC], g_w_in_p[:, O_GLR:O_GLR + GLA_RANK], g_w_in_p[:, GLR_SRC:O_GLR]], axis=1)
    (dx, sums1), (r_in,) = _mm_rms_mod_bwd(dproj, w_in_p, x, dx2, n1g, sc1, ts, "mm_dh1",
                                           ride=([_col_blocks(g_w_in).astype(BF16)], [False]))
    dsh1, dsc1, g_n1g = sums1[0:1], sums1[1:2], sums1[2:3]

    dmod = jnp.concatenate([dsh1, dsc1, dg1, dsh2, dsc2, dg2], axis=1)
    grads = dict(n1g=g_n1g, w_in=r_in, wg=g_wg_p[:GLA_RANK], bg=gb_sums[0:1], gng=gng_sums[0:1],
                 qng_lanes=qk_sums[0:1], kng_lanes=qk_sums[1:2], w_out=r_out, n2g=g_n2g, w_up=r_up,
                 conv_w=g_conv_w, conv_b=g_conv_b, w_down=r_down)
    return loss_row, dx, dmod, grads


def _col_blocks(a):
    R, W = a.shape
    return a.reshape(R, N_DEV, W // N_DEV).transpose(1, 0, 2)


def _cols_from_blocks(a):
    n, R, C = a.shape
    return a.transpose(1, 0, 2).reshape(R, n * C)


def kernel(x, c, w_ada, b_ada, norm1_g, w_in, gla_w_gate, gla_b_gate, gla_norm_g, q_norm_g, k_norm_g, w_out, norm2_g, w_up, conv_w, conv_b, w_down, loss_target, m_w_ada, m_b_ada, m_norm1_g, m_w_in, m_gla_w_gate, m_gla_b_gate, m_gla_norm_g, m_q_norm_g, m_k_norm_g, m_w_out, m_norm2_g, m_w_up, m_conv_w, m_conv_b, m_w_down, v_w_ada, v_b_ada, v_norm1_g, v_w_in, v_gla_w_gate, v_gla_b_gate, v_gla_norm_g, v_q_norm_g, v_k_norm_g, v_w_out, v_norm2_g, v_w_up, v_conv_w, v_conv_b, v_w_down):
    axes = ("x", "y", "c")
    me = 4 * lax.axis_index("x") + 2 * lax.axis_index("y") + lax.axis_index("c")
    S, D = x.shape[1], x.shape[2]
    x2d, tgt2d = x[0], loss_target[0]
    w_in_s, w_out_s, w_up_s, w_down_s, w_ada_s = w_in[0], w_out[0], w_up[0], w_down[0], w_ada[0]
    conv_w_s, wg_s = conv_w[0], gla_w_gate[0]
    in_c, up_c, ada_c, wg_c, cw_c = w_in_s.shape[1], w_up_s.shape[1], w_ada_s.shape[1], wg_s.shape[1], conv_w_s.shape[1]
    F = w_down_s.shape[0] * N_DEV

    small = jnp.concatenate([conv_w_s.reshape(1, -1), wg_s.reshape(1, -1)], axis=1)
    n_small = small.shape[1]
    small = jnp.pad(small, ((0, 0), (0, -n_small % LANE)))
    g_c, g_in, g_small = _exchange([c, w_in_s.astype(BF16), small], [True] * 3, "gather_w_in")
    c_all = g_c.reshape(N_DEV, D)
    w_in_full = _cols_from_blocks(g_in)
    w_in_p = jnp.concatenate([w_in_full[:, :GLR_SRC], w_in_full[:, GLR_SRC + GLA_RANK:],
                              w_in_full[:, GLR_SRC:GLR_SRC + GLA_RANK], jnp.zeros((D, PROJ_W - O_GLR - GLA_RANK), BF16)], axis=1)
    g_small = g_small.reshape(N_DEV, -1)
    conv_w_full = jnp.stack([g_small[:, t * cw_c:(t + 1) * cw_c].reshape(-1) for t in range(3)])
    wg_full = _cols_from_blocks(g_small[:, 3 * cw_c:n_small].reshape(N_DEV, GLA_RANK, wg_c))
    wg_p = jnp.concatenate([wg_full, jnp.zeros((LANE - GLA_RANK, wg_full.shape[1]), F32)], axis=0)

    b_shard = lax.dynamic_slice(b_ada, (0, me * ada_c), (1, ada_c))
    mod_part = _ada_fwd(c_all, w_ada_s, b_shard)
    mod_recv, = _exchange([mod_part.reshape(N_DEV, 1, ada_c)], [False], "exchange_mod")
    mod = mod_recv.reshape(6, D)

    loss_row, dx, dmod, gr = _local_step(
        x2d, tgt2d, mod, norm1_g, w_in_p, wg_p, gla_b_gate, gla_norm_g, q_norm_g, k_norm_g,
        w_out_s.astype(BF16), norm2_g, w_up_s.astype(BF16), conv_w_full, conv_b, w_down_s.astype(BF16))
    loss = lax.psum(0.5 / D * jnp.sum(loss_row), axes)

    parts = [dmod, gr["n1g"], gr["bg"], gr["gng"], gr["qng_lanes"], gr["kng_lanes"], gr["n2g"], gr["conv_b"],
             gr["wg"].reshape(1, -1), gr["conv_w"].reshape(1, -1)]
    sizes = [p.shape[1] for p in parts]
    packed = jnp.concatenate(parts, axis=1)
    packed = jnp.pad(packed, ((0, 0), (0, -packed.shape[1] % (8 * LANE))))
    gathered, = _exchange([packed.reshape(8, -1)], [True], "gather_small_grads")
    gathered = gathered.reshape(N_DEV, -1)
    total = _sum_slots(gathered.reshape(N_DEV, 8, -1), "sum_small_grads").reshape(1, -1)
    offs = [0]
    for s_ in sizes:
        offs.append(offs[-1] + s_)
    t_dmod, t_n1g, t_bg, t_gng, t_qng, t_kng, t_n2g, t_conv_b, t_wg, t_conv_w = [
        total[:, offs[i]:offs[i + 1]] for i in range(len(sizes))]
    g_b_ada = t_dmod
    g_qng = t_qng.reshape(ATTN_HEADS, ATTN_HD).sum(axis=0, keepdims=True)
    g_kng = t_kng.reshape(ATTN_HEADS, ATTN_HD).sum(axis=0, keepdims=True)
    g_wg = lax.dynamic_slice(t_wg.reshape(GLA_RANK, -1), (0, me * wg_c), (GLA_RANK, wg_c))
    g_conv_w = lax.dynamic_slice(t_conv_w.reshape(3, -1), (0, me * cw_c), (3, cw_c))
    dmod_shard = lax.dynamic_slice(gathered[:, :6 * D], (0, me * ada_c), (N_DEV, ada_c))
    g_w_ada = _ada_bwd(c_all, dmod_shard)

    g_w_in = _sum_slots(gr["w_in"], "sum_gw_in")
    g_w_out = _sum_slots(gr["w_out"], "sum_gw_out")
    g_w_up = _sum_slots(gr["w_up"], "sum_gw_up")
    g_w_down = _sum_slots(gr["w_down"], "sum_gw_down")

    names = ["w_ada", "b_ada", "norm1_g", "w_in", "gla_w_gate", "gla_b_gate", "gla_norm_g", "q_norm_g", "k_norm_g",
             "w_out", "norm2_g", "w_up", "conv_w", "conv_b", "w_down"]
    ws = [w_ada, b_ada, norm1_g, w_in, gla_w_gate, gla_b_gate, gla_norm_g, q_norm_g, k_norm_g, w_out, norm2_g, w_up, conv_w, conv_b, w_down]
    ms = [m_w_ada, m_b_ada, m_norm1_g, m_w_in, m_gla_w_gate, m_gla_b_gate, m_gla_norm_g, m_q_norm_g, m_k_norm_g, m_w_out, m_norm2_g, m_w_up, m_conv_w, m_conv_b, m_w_down]
    vs = [v_w_ada, v_b_ada, v_norm1_g, v_w_in, v_gla_w_gate, v_gla_b_gate, v_gla_norm_g, v_q_norm_g, v_k_norm_g, v_w_out, v_norm2_g, v_w_up, v_conv_w, v_conv_b, v_w_down]
    gs = [g_w_ada, g_b_ada, t_n1g, g_w_in, g_wg, t_bg, t_gng, g_qng, g_kng, g_w_out, t_n2g, g_w_up, g_conv_w, t_conv_b, g_w_down]
    gs = [g.reshape(w.shape) for g, w in zip(gs, ws)]
    deltas, new_ms, new_vs = [], [], []
    for nm, w, g, m, v in zip(names, ws, gs, ms, vs):
        d_, m_, v_ = _adamw(w, g, m, v, "adamw_" + nm)
        deltas.append(d_)
        new_ms.append(m_)
        new_vs.append(v_)
    return (loss, dx.reshape(x.shape), *gs, *deltas, *new_ms, *new_vs)
```

```python
import functools
import math

import jax
import jax.numpy as jnp
from jax import lax
from jax.experimental import pallas as pl
from jax.experimental.pallas import tpu as pltpu

F32, BF16 = jnp.float32, jnp.bfloat16
HI = lax.Precision.HIGHEST
EPS = 1e-6
NEG = -1e30

N_DEV = 8
GLA_HEADS, GLA_DK, GLA_DV, GLA_RANK, GLA_TAU, GLA_CHUNK = 4, 64, 128, 16, 16.0, 64
ATTN_HEADS, ATTN_HD, ATTN_BLOCK = 8, 64, 128
DILATIONS = (1, 4, 16)
GLA_QK, GLA_V, ATTN_DIM = GLA_HEADS * GLA_DK, GLA_HEADS * GLA_DV, ATTN_HEADS * ATTN_HD
O_GQ, O_GK, O_GV, O_GR, O_AQ, O_AK, O_AV, O_GLR = 0, 256, 512, 1024, 1536, 2048, 2560, 3072
PROJ_W = 3328
LANE = 128
GLR_SRC = 2 * GLA_QK + 2 * GLA_V

ADAM_LR, ADAM_B1, ADAM_B2, ADAM_EPS, ADAM_WD, ADAM_STEP = 0.001, 0.9, 0.999, 1e-08, 0.01, 10

VMEM_LIMIT = 56 * 1024 * 1024
SUM_BLOCK_ELEMS = 256 * 1024


def _cp(*sem):
    return pltpu.CompilerParams(dimension_semantics=sem, vmem_limit_bytes=VMEM_LIMIT)


def _dot(a, b, dims, precision=None):
    return lax.dot_general(a, b, (dims, ((), ())), preferred_element_type=F32, precision=precision)


NN, NT, TN = ((1,), (0,)), ((1,), (1,)), ((0,), (0,))


def _sigmoid(z):
    return 1.0 / (1.0 + jnp.exp(-z))


HBM_SPEC = pl.BlockSpec(memory_space=pltpu.HBM)


def _exchange_shapes(arrays, gather):
    return [jax.ShapeDtypeStruct((N_DEV,) + (a.shape if g else a.shape[1:]), a.dtype) for a, g in zip(arrays, gather)]


def _exchange_sems(n):
    return [pltpu.SemaphoreType.DMA((n * (N_DEV - 1),)), pltpu.SemaphoreType.DMA((n * (N_DEV - 1),)), pltpu.SemaphoreType.DMA((n,))]


def _exchange_copies(ins, outs, gather, send_sems, recv_sems, local_sems):
    x, y, c = lax.axis_index("x"), lax.axis_index("y"), lax.axis_index("c")
    me = 4 * x + 2 * y + c
    copies = []
    for a in range(len(ins)):
        for p in range(1, N_DEV):
            px, py, pc = x ^ (p >> 2), y ^ ((p >> 1) & 1), c ^ (p & 1)
            peer = 4 * px + 2 * py + pc
            k = a * (N_DEV - 1) + p - 1
            copies.append(pltpu.make_async_remote_copy(
                src_ref=ins[a] if gather[a] else ins[a].at[peer], dst_ref=outs[a].at[me],
                send_sem=send_sems.at[k], recv_sem=recv_sems.at[k],
                device_id=(px, py, pc), device_id_type=pl.DeviceIdType.MESH))
        copies.append(pltpu.make_async_copy(ins[a] if gather[a] else ins[a].at[me], outs[a].at[me], local_sems.at[a]))
    return copies


def _riding(body, n_in, n_out, gather, grid):
    nr = len(gather)
    if not nr:
        return body

    def wrapped(*refs):
        ins, r_ins = refs[:n_in], refs[n_in:n_in + nr]
        outs, r_outs = refs[n_in + nr:n_in + nr + n_out], refs[n_in + nr + n_out:n_in + 2 * nr + n_out]
        scratch = refs[n_in + 2 * nr + n_out:]
        first = last = None
        for t, steps in enumerate(grid):
            pid = pl.program_id(t)
            first = (pid == 0) if first is None else first & (pid == 0)
            last = (pid == steps - 1) if last is None else last & (pid == steps - 1)
        copies = _exchange_copies(r_ins, r_outs, gather, *scratch[-3:])

        @pl.when(first)
        def _():
            for cp in copies:
                cp.start()

        body(*ins, *outs, *scratch[:-3])

        @pl.when(last)
        def _():
            for cp in copies:
                cp.wait()

    return wrapped


def _exchange(arrays, gather, name):
    n = len(arrays)

    def body(*refs):
        copies = _exchange_copies(refs[:n], refs[n:2 * n], gather, *refs[2 * n:])
        for cp in copies:
            cp.start()
        for cp in copies:
            cp.wait()

    return pl.pallas_call(
        body, out_shape=_exchange_shapes(arrays, gather), in_specs=[HBM_SPEC] * n, out_specs=[HBM_SPEC] * n,
        scratch_shapes=_exchange_sems(n), name=name)(*arrays)


def _sum_slots(x, name):
    _, R, C = x.shape
    tr = max(t for t in range(8, min(SUM_BLOCK_ELEMS // C, R) + 1, 8) if R % t == 0)

    def body(x_ref, o_ref):
        acc = x_ref[0].astype(F32)
        for s in range(1, N_DEV):
            acc = acc + x_ref[s].astype(F32)
        o_ref[...] = acc

    return pl.pallas_call(
        body, grid=(R // tr,), in_specs=[pl.BlockSpec((N_DEV, tr, C), lambda i: (0, i, 0))],
        out_specs=pl.BlockSpec((tr, C), lambda i: (i, 0)), out_shape=jax.ShapeDtypeStruct((R, C), F32),
        compiler_params=_cp("parallel"), name=name)(x)


def _mm(a, b, mode, tm, tn, tk, out_dtype, name, ride=None):
    if mode == NN:
        (M, K), N = a.shape, b.shape[1]
    elif mode == NT:
        (M, K), N = a.shape, b.shape[0]
    else:
        (K, M), N = a.shape, b.shape[1]
    tm, tn, tk = min(tm, M), min(tn, N), min(tk, K)
    assert M % tm == 0 and N % tn == 0 and K % tk == 0, (name, M, N, K, tm, tn, tk)
    nk = K // tk
    if mode == NN:
        a_spec = pl.BlockSpec((tm, tk), lambda i, j, k: (i, k))
        b_spec = pl.BlockSpec((tk, tn), lambda i, j, k: (k, j))
    elif mode == NT:
        a_spec = pl.BlockSpec((tm, tk), lambda i, j, k: (i, k))
        b_spec = pl.BlockSpec((tn, tk), lambda i, j, k: (j, k))
    else:
        a_spec = pl.BlockSpec((tk, tm), lambda i, j, k: (k, i))
        b_spec = pl.BlockSpec((tk, tn), lambda i, j, k: (k, j))

    ride_arrays, ride_gather = ride if ride else ([], [])
    nr = len(ride_arrays)
    grid = (M // tm, N // tn, nk)

    own_acc = nk > 1 and out_dtype != F32

    def body(a_ref, b_ref, o_ref, *acc):
        p = _dot(a_ref[...].astype(BF16), b_ref[...].astype(BF16), mode)
        if nk == 1:
            o_ref[...] = p.astype(out_dtype)
        else:
            acc_ref = acc[0] if own_acc else o_ref
            k = pl.program_id(2)

            @pl.when(k == 0)
            def _():
                acc_ref[...] = p

            @pl.when(k > 0)
            def _():
                acc_ref[...] += p

            if own_acc:
                @pl.when(k == nk - 1)
                def _():
                    o_ref[...] = acc_ref[...].astype(out_dtype)

    outs = pl.pallas_call(
        _riding(body, 2, 1, ride_gather, grid), grid=grid, in_specs=[a_spec, b_spec] + [HBM_SPEC] * nr,
        out_specs=[pl.BlockSpec((tm, tn), lambda i, j, k: (i, j))] + [HBM_SPEC] * nr,
        out_shape=[jax.ShapeDtypeStruct((M, N), out_dtype)] + _exchange_shapes(ride_arrays, ride_gather),
        scratch_shapes=([pltpu.VMEM((tm, tn), F32)] if own_acc else []) + (_exchange_sems(nr) if nr else []),
        compiler_params=_cp(*(("arbitrary",) * 3 if nr else ("parallel", "parallel", "arbitrary"))), name=name)(a, b, *ride_arrays)
    return (outs[0], outs[1:]) if nr else outs[0]


def _ada_fwd(c_all, w_shard, b_shard):
    Nc = w_shard.shape[1]

    def body(c_ref, w_ref, b_ref, o_ref):
        cv = c_ref[...]
        o_ref[...] = _dot(cv * _sigmoid(cv), w_ref[...], NN, HI) + b_ref[...]

    return pl.pallas_call(body, out_shape=jax.ShapeDtypeStruct((N_DEV, Nc), F32), name="ada_fwd",
                          compiler_params=pltpu.CompilerParams(vmem_limit_bytes=VMEM_LIMIT))(c_all, w_shard, b_shard)


def _ada_bwd(c_all, dmod_shard):
    D, Nc = c_all.shape[1], dmod_shard.shape[1]

    def body(c_ref, d_ref, o_ref):
        cv = c_ref[...]
        o_ref[...] = _dot(cv * _sigmoid(cv), d_ref[...], TN, HI)

    return pl.pallas_call(body, out_shape=jax.ShapeDtypeStruct((D, Nc), F32), name="ada_bwd",
                          compiler_params=pltpu.CompilerParams(vmem_limit_bytes=VMEM_LIMIT))(c_all, dmod_shard)


def _row_spec(ts, D):
    return pl.BlockSpec((ts, D), lambda i: (i, 0))


def _vec_spec(D):
    return pl.BlockSpec((1, D), lambda i: (0, 0))


def _col_spec(D, ts):
    return pl.BlockSpec((D, ts), lambda i: (0, i))


def _rms_mod(x, ng, sc, sh, ts, name):
    S, D = x.shape

    def body(x_ref, ng_ref, sc_ref, sh_ref, h_ref, ht_ref):
        xv = x_ref[...]
        r = lax.rsqrt(jnp.mean(xv * xv, axis=-1, keepdims=True) + EPS)
        h = xv * r * ng_ref[...] * (1.0 + sc_ref[...]) + sh_ref[...]
        h_ref[...] = h.astype(BF16)
        ht_ref[...] = h.T.astype(BF16)

    return pl.pallas_call(
        body, grid=(S // ts,), in_specs=[_row_spec(ts, D)] + [_vec_spec(D)] * 3, out_specs=[_row_spec(ts, D), _col_spec(D, ts)],
        out_shape=[jax.ShapeDtypeStruct((S, D), BF16), jax.ShapeDtypeStruct((D, S), BF16)],
        compiler_params=_cp("parallel"), name=name)(x, ng, sc, sh)


def _mm_rows(a, b, mode, tm, extras, extra_specs, out_shapes, out_specs, epilogue, name, ride=None):
    M, K = a.shape
    grid = (M // tm,)
    ride_arrays, ride_gather = ride if ride else ([], [])
    nr = len(ride_arrays)

    def body(a_ref, b_ref, *refs):
        epilogue(_dot(a_ref[...].astype(BF16), b_ref[...].astype(BF16), mode), pl.program_id(0), *refs)

    outs = pl.pallas_call(
        _riding(body, 2 + len(extras), len(out_shapes), ride_gather, grid), grid=grid,
        in_specs=[pl.BlockSpec((tm, K), lambda i: (i, 0)), pl.BlockSpec(b.shape, lambda i: (0, 0), pipeline_mode=pl.Buffered(1))]
        + list(extra_specs) + [HBM_SPEC] * nr,
        out_specs=list(out_specs) + [HBM_SPEC] * nr,
        out_shape=list(out_shapes) + _exchange_shapes(ride_arrays, ride_gather),
        scratch_shapes=_exchange_sems(nr) if nr else [],
        compiler_params=_cp("arbitrary"), name=name)(a, b, *extras, *ride_arrays)
    return outs[:len(out_shapes)], outs[len(out_shapes):]


def _accumulate(ref, part, step):
    @pl.when(step == 0)
    def _():
        ref[...] = part

    @pl.when(step > 0)
    def _():
        ref[...] += part


def _rows8(rows, width):
    return jnp.concatenate(rows + [jnp.zeros((8 - len(rows), width), F32)], axis=0)


def _mm_resid_rms_mod(a, w, x, g, ng, sc, sh, tm, name):
    S, D = x.shape

    def epilogue(t, step, x_ref, g_ref, ng_ref, sc_ref, sh_ref, t_ref, x2_ref, h_ref, ht_ref):
        t_ref[...] = t
        xv = x_ref[...] + g_ref[...] * t
        x2_ref[...] = xv
        r = lax.rsqrt(jnp.mean(xv * xv, axis=-1, keepdims=True) + EPS)
        h = xv * r * ng_ref[...] * (1.0 + sc_ref[...]) + sh_ref[...]
        h_ref[...] = h.astype(BF16)
        ht_ref[...] = h.T.astype(BF16)

    row, vec = _row_spec(tm, D), _vec_spec(D)
    full, half = jax.ShapeDtypeStruct((S, D), F32), jax.ShapeDtypeStruct((S, D), BF16)
    outs, _ = _mm_rows(a, w, NN, tm, [x, g, ng, sc, sh], [row] + [vec] * 4,
                       [full, full, half, jax.ShapeDtypeStruct((D, S), BF16)], [row, row, row, _col_spec(D, tm)], epilogue, name)
    return outs


def _mm_loss_resid(a, w, x2, g2, target, tm, name):
    S, D = x2.shape

    def epilogue(t, step, x_ref, y_ref, g_ref, dx_ref, dt_ref, sums_ref):
        gv = g_ref[...]
        e = x_ref[...] + gv * t - y_ref[...]
        dx = e * (1.0 / D)
        dx_ref[...] = dx
        dt_ref[...] = (dx * gv).astype(BF16)
        _accumulate(sums_ref, _rows8([jnp.sum(e * e, axis=0, keepdims=True), jnp.sum(dx * t, axis=0, keepdims=True)], D), step)

    row, vec = _row_spec(tm, D), _vec_spec(D)
    outs, _ = _mm_rows(a, w, NN, tm, [x2, target, g2], [row, row, vec],
                       [jax.ShapeDtypeStruct((S, D), F32), jax.ShapeDtypeStruct((S, D), BF16), jax.ShapeDtypeStruct((8, D), F32)],
                       [row, row, pl.BlockSpec((8, D), lambda i: (0, 0))], epilogue, name)
    return outs


def _mm_rms_mod_bwd(a, w, xin, dres, ng, sc, tm, name, t_prev=None, g_prev=None, ride=None):
    S, D = xin.shape
    chain = t_prev is not None

    def epilogue(dhv, step, *refs):
        if chain:
            x_ref, dr_ref, ng_ref, sc_ref, t_ref, g_ref, dx_ref, sums_ref, dt_ref = refs
        else:
            x_ref, dr_ref, ng_ref, sc_ref, dx_ref, sums_ref = refs
        xv = x_ref[...]
        r = lax.rsqrt(jnp.mean(xv * xv, axis=-1, keepdims=True) + EPS)
        xh = xv * r
        ngv, scv = ng_ref[...], sc_ref[...]
        dxh = dhv * (ngv * (1.0 + scv))
        dx = dr_ref[...] + r * (dxh - xh * jnp.mean(dxh * xh, axis=-1, keepdims=True))
        dx_ref[...] = dx
        dhx = dhv * xh
        rows = [jnp.sum(dhv, axis=0, keepdims=True), jnp.sum(dhx * ngv, axis=0, keepdims=True),
                jnp.sum(dhx * (1.0 + scv), axis=0, keepdims=True)]
        if chain:
            dt_ref[...] = (dx * g_ref[...]).astype(BF16)
            rows.append(jnp.sum(dx * t_ref[...], axis=0, keepdims=True))
        _accumulate(sums_ref, _rows8(rows, D), step)

    row, vec = _row_spec(tm, D), _vec_spec(D)
    extras = [xin, dres, ng, sc] + ([t_prev, g_prev] if chain else [])
    extra_specs = [row, row, vec, vec] + ([row, vec] if chain else [])
    out_shapes = [jax.ShapeDtypeStruct((S, D), F32), jax.ShapeDtypeStruct((8, D), F32)] + (
        [jax.ShapeDtypeStruct((S, D), BF16)] if chain else [])
    out_specs = [row, pl.BlockSpec((8, D), lambda i: (0, 0))] + ([row] if chain else [])
    return _mm_rows(a, w, NT, tm, extras, extra_specs, out_shapes, out_specs, epilogue, name, ride=ride)


def _gate_fwd(proj, wg_p, bg, ts):
    S = proj.shape[0]

    def body(glr_ref, w_ref, b_ref, la_ref):
        z = _dot(glr_ref[...], w_ref[...], NN, HI) + b_ref[...]
        la_ref[...] = (jnp.minimum(z, 0.0) - jnp.log(1.0 + jnp.exp(-jnp.abs(z)))) * (1.0 / GLA_TAU)

    return pl.pallas_call(
        body, grid=(S // ts,),
        in_specs=[pl.BlockSpec((ts, LANE), lambda i: (i, O_GLR // LANE)), pl.BlockSpec((LANE, GLA_QK), lambda i: (0, 0)),
                  pl.BlockSpec((1, GLA_QK), lambda i: (0, 0))],
        out_specs=pl.BlockSpec((ts, GLA_QK), lambda i: (i, 0)), out_shape=jax.ShapeDtypeStruct((S, GLA_QK), F32),
        compiler_params=_cp("parallel"), name="gla_gate_fwd")(proj, wg_p, bg)


def _gate_bwd(dla, la, proj, wg_p, ts):
    S = proj.shape[0]

    def body(dla_ref, la_ref, glr_ref, w_ref, dglr_ref, gw_ref, gb_ref):
        i = pl.program_id(0)
        dz = dla_ref[...] * (1.0 / GLA_TAU) * (1.0 - jnp.exp(GLA_TAU * la_ref[...]))
        dglr_ref[...] = _dot(dz, w_ref[...], NT, HI).astype(BF16)
        gw = _dot(glr_ref[...], dz, TN, HI)
        gb = jnp.concatenate([jnp.sum(dz, axis=0, keepdims=True), jnp.zeros((7, GLA_QK), F32)], axis=0)

        @pl.when(i == 0)
        def _():
            gw_ref[...] = gw
            gb_ref[...] = gb

        @pl.when(i > 0)
        def _():
            gw_ref[...] += gw
            gb_ref[...] += gb

    return pl.pallas_call(
        body, grid=(S // ts,),
        in_specs=[pl.BlockSpec((ts, GLA_QK), lambda i: (i, 0)), pl.BlockSpec((ts, GLA_QK), lambda i: (i, 0)),
                  pl.BlockSpec((ts, LANE), lambda i: (i, O_GLR // LANE)), pl.BlockSpec((LANE, GLA_QK), lambda i: (0, 0))],
        out_specs=[pl.BlockSpec((ts, LANE), lambda i: (i, 0)), pl.BlockSpec((LANE, GLA_QK), lambda i: (0, 0)),
                   pl.BlockSpec((8, GLA_QK), lambda i: (0, 0))],
        out_shape=[jax.ShapeDtypeStruct((S, LANE), BF16), jax.ShapeDtypeStruct((LANE, GLA_QK), F32),
                   jax.ShapeDtypeStruct((8, GLA_QK), F32)],
        compiler_params=_cp("arbitrary"), name="gla_gate_bwd")(dla, la, proj, wg_p)


def _tri(lower):
    r = lax.broadcasted_iota(jnp.int32, (GLA_CHUNK, GLA_CHUNK), 0)
    c = lax.broadcasted_iota(jnp.int32, (GLA_CHUNK, GLA_CHUNK), 1)
    return jnp.where((r >= c) if lower else (c >= r), 1.0, 0.0).astype(F32)


GLA_SUB = 16
GLA_NSUB = GLA_CHUNK // GLA_SUB
PAIR_QK = 2 * GLA_DK
PAIR_V = 2 * GLA_DV


def _band_selector():
    r = lax.broadcasted_iota(jnp.int32, (GLA_SUB * PAIR_QK, LANE), 0)
    c = lax.broadcasted_iota(jnp.int32, (GLA_SUB * PAIR_QK, LANE), 1)
    dist, head = r // PAIR_QK, (r % PAIR_QK) // GLA_DK
    return jnp.where(c == head * GLA_DK + (GLA_SUB - 1 - dist), 1.0, 0.0).astype(BF16)


def _flip_matrix():
    r = lax.broadcasted_iota(jnp.int32, (GLA_CHUNK, GLA_CHUNK), 0)
    c = lax.broadcasted_iota(jnp.int32, (GLA_CHUNK, GLA_CHUNK), 1)
    return jnp.where(r + c == GLA_CHUNK - 1, 1.0, 0.0).astype(BF16)


def _state_mask():
    r = lax.broadcasted_iota(jnp.int32, (PAIR_V, PAIR_QK), 0)
    c = lax.broadcasted_iota(jnp.int32, (PAIR_V, PAIR_QK), 1)
    return (r < GLA_DV) == (c < GLA_DK)


class _GlaChunk:
    def __init__(self, qs, kc, vc, g, sel):
        C = GLA_CHUNK
        self.qs, self.kc, self.vc = qs, kc, vc
        rows = lax.broadcasted_iota(jnp.int32, (C, 1), 0)
        lane = lax.broadcasted_iota(jnp.int32, (1, PAIR_QK), 1)
        self.rows, self.lane = rows, lane
        b = _dot(_tri(True), g, NN, HI)
        self.bl = b[C - 1:C, :]
        self.eb = jnp.exp(b)
        self.kdec = jnp.exp(self.bl - b)
        edge = lambda J: b[GLA_SUB * (J + 1):GLA_SUB * (J + 1) + 1, :]
        self.e_far = [jnp.exp(jnp.where(rows >= GLA_SUB * (J + 1), b - edge(J), NEG)) for J in range(GLA_NSUB - 1)]
        blk = rows // GLA_SUB
        bnext = edge(0)
        for J in range(1, GLA_NSUB - 1):
            bnext = jnp.where(blk == J, edge(J), bnext)
        self.e_khat = jnp.exp(jnp.where(blk < GLA_NSUB - 1, bnext - b, NEG))
        khat = kc * self.e_khat
        k2 = jnp.concatenate([jnp.where(lane < GLA_DK, khat, 0.0), jnp.where(lane >= GLA_DK, khat, 0.0)], axis=0)
        self.blk2 = jnp.concatenate([blk, blk], axis=0)
        self.m_far = jnp.concatenate([jnp.where(self.blk2 == J, k2, 0.0) for J in range(GLA_NSUB - 1)], axis=1).astype(BF16)
        self.qcat = jnp.concatenate([qs * e for e in self.e_far], axis=1).astype(BF16)
        a_far = _dot(self.qcat, self.m_far, NT)
        self.e_band, self.rk, terms = [], [], []
        for d in range(GLA_SUB):
            rk = pltpu.roll(kc, d, 0) if d else kc
            rb = pltpu.roll(b, d, 0) if d else b
            e = jnp.exp(jnp.where(rows >= d, b - rb, NEG))
            self.e_band.append(e)
            self.rk.append(rk)
            terms.append((qs * rk * e).astype(BF16))
        band = _dot(jnp.concatenate(terms, axis=1), sel, NN)
        a_band = pltpu.roll(band, LANE - (GLA_SUB - 1), 1, stride=1, stride_axis=0)
        dist = rows - lane % GLA_DK
        self.far_mask = dist >= GLA_SUB
        self.band_mask = (dist >= 0) & (dist < GLA_SUB)
        self.a = (a_band + jnp.where(self.far_mask, a_far, 0.0)).astype(BF16)
        self.lane_v = lax.broadcasted_iota(jnp.int32, (1, PAIR_V), 1)
        self.v2 = jnp.concatenate([jnp.where(self.lane_v < GLA_DV, vc, 0.0), jnp.where(self.lane_v >= GLA_DV, vc, 0.0)],
                                  axis=0).astype(BF16)


def _gla_fwd(proj, la, tb, ride=None):
    S = proj.shape[0]
    C = GLA_CHUNK
    tb = min(tb, S)
    nbc = tb // C
    npair = GLA_HEADS // 2
    scale = GLA_DK ** -0.5

    def body(q_ref, k_ref, v_ref, la_ref, sel_ref, o_ref, st_ref, state):
        @pl.when(pl.program_id(1) == 0)
        def _():
            state[...] = jnp.zeros_like(state)

        def chunk(ci, carry):
            sl = pl.ds(pl.multiple_of(ci * C, C), C)
            ch = _GlaChunk(q_ref[sl, :] * scale, k_ref[sl, :], v_ref[sl, :], la_ref[sl, :], sel_ref[...])
            st = state[...]
            st_ref[0, ci] = st
            o_ref[sl, :] = _dot((ch.qs * ch.eb).astype(BF16), st.astype(BF16), NT) + _dot(ch.a, ch.v2, NN)
            upd = _dot(ch.vc.astype(BF16), (ch.kc * ch.kdec).astype(BF16), TN)
            state[...] = st * jnp.exp(ch.bl) + jnp.where(_state_mask(), upd, 0.0)
            return carry

        lax.fori_loop(0, nbc, chunk, 0, unroll=8)

    qspec = lambda off: pl.BlockSpec((tb, PAIR_QK), lambda p, i: (i, off // PAIR_QK + p))
    ride_arrays, ride_gather = ride if ride else ([], [])
    nr = len(ride_arrays)
    grid = (npair, S // tb)
    outs = pl.pallas_call(
        _riding(body, 5, 2, ride_gather, grid), grid=grid,
        in_specs=[qspec(O_GQ), qspec(O_GK), pl.BlockSpec((tb, PAIR_V), lambda p, i: (i, O_GV // PAIR_V + p)),
                  pl.BlockSpec((tb, PAIR_QK), lambda p, i: (i, p)),
                  pl.BlockSpec((GLA_SUB * PAIR_QK, LANE), lambda p, i: (0, 0))] + [HBM_SPEC] * nr,
        out_specs=[pl.BlockSpec((tb, PAIR_V), lambda p, i: (i, p)),
                   pl.BlockSpec((1, nbc, PAIR_V, PAIR_QK), lambda p, i: (p, i, 0, 0))] + [HBM_SPEC] * nr,
        out_shape=[jax.ShapeDtypeStruct((S, GLA_V), F32), jax.ShapeDtypeStruct((npair, S // C, PAIR_V, PAIR_QK), F32)]
        + _exchange_shapes(ride_arrays, ride_gather),
        scratch_shapes=[pltpu.VMEM((PAIR_V, PAIR_QK), F32)] + (_exchange_sems(nr) if nr else []),
        compiler_params=_cp("arbitrary", "arbitrary"), name="gla_fwd")(proj, proj, proj, la, _band_selector(), *ride_arrays)
    return outs[0], outs[1], outs[2:]


def _gla_bwd(proj, la, do, states, tb, ride=None):
    S = proj.shape[0]
    C = GLA_CHUNK
    tb = min(tb, S)
    nbc = tb // C
    nblk = S // tb
    npair = GLA_HEADS // 2
    scale = GLA_DK ** -0.5

    def body(q_ref, k_ref, v_ref, la_ref, do_ref, st_ref, sel_ref, selt_ref, dq_ref, dk_ref, dv_ref, dla_ref, dstate):
        @pl.when(pl.program_id(1) == 0)
        def _():
            dstate[...] = jnp.zeros_like(dstate)

        def chunk(cc, carry):
            ci = nbc - 1 - cc
            sl = pl.ds(pl.multiple_of(ci * C, C), C)
            ch = _GlaChunk(q_ref[sl, :] * scale, k_ref[sl, :], v_ref[sl, :], la_ref[sl, :], sel_ref[...])
            qs, kc, rows = ch.qs, ch.kc, ch.rows
            doc_b = do_ref[sl, :].astype(BF16)
            st = st_ref[0, ci]
            dst = dstate[...]
            dst_b = dst.astype(BF16)
            ebl = jnp.exp(ch.bl)
            dq = _dot(doc_b, st.astype(BF16), NN) * ch.eb
            dk = _dot(ch.vc.astype(BF16), dst_b, NN) * ch.kdec
            dv = _dot((kc * ch.kdec).astype(BF16), dst_b, NT)
            dbl = jnp.sum(dst * st, axis=0, keepdims=True) * ebl + jnp.sum(kc * dk, axis=0, keepdims=True)
            da = _dot(doc_b, ch.v2, NT)
            dv2 = _dot(ch.a, doc_b, TN)
            dv = dv + jnp.where(ch.lane_v < GLA_DV, dv2[:C], dv2[C:])
            da_far = jnp.where(ch.far_mask, da, 0.0).astype(BF16)
            dqcat = _dot(da_far, ch.m_far, NN)
            dm = _dot(da_far, ch.qcat, TN)
            dk2 = jnp.zeros((2 * C, PAIR_QK), F32)
            for J in range(GLA_NSUB - 1):
                dq = dq + dqcat[:, J * PAIR_QK:(J + 1) * PAIR_QK] * ch.e_far[J]
                dk2 = dk2 + jnp.where(ch.blk2 == J, dm[:, J * PAIR_QK:(J + 1) * PAIR_QK], 0.0)
            dk = dk + jnp.where(ch.lane < GLA_DK, dk2[:C], dk2[C:]) * ch.e_khat
            flip = _flip_matrix()
            da_band = _dot(flip, jnp.where(ch.band_mask, da, 0.0).astype(BF16), NN)
            dband = pltpu.roll(da_band, LANE - (C - GLA_SUB), 1, stride=1, stride_axis=0)
            dband = _dot(flip, dband.astype(BF16), NN)
            dterms = _dot(dband.astype(BF16), selt_ref[...], NN)
            for d in range(GLA_SUB):
                dt = dterms[:, d * PAIR_QK:(d + 1) * PAIR_QK]
                dq = dq + dt * (ch.rk[d] * ch.e_band[d])
                dkr = dt * (qs * ch.e_band[d])
                dk = dk + (pltpu.roll(dkr, C - d, 0) if d else dkr)
            db = qs * dq - kc * dk
            db = jnp.where(rows == C - 1, db + dbl, db)
            dq_ref[sl, :] = (dq * scale).astype(BF16)
            dk_ref[sl, :] = dk.astype(BF16)
            dv_ref[sl, :] = dv.astype(BF16)
            dla_ref[sl, :] = _dot(_tri(False), db, NN, HI)
            upd = _dot(doc_b, (qs * ch.eb).astype(BF16), TN)
            dstate[...] = dst * ebl + jnp.where(_state_mask(), upd, 0.0)
            return carry

        lax.fori_loop(0, nbc, chunk, 0, unroll=8)

    rev = lambda i: nblk - 1 - i
    qspec = lambda off: pl.BlockSpec((tb, PAIR_QK), lambda p, i: (rev(i), off // PAIR_QK + p))
    pair_qk = pl.BlockSpec((tb, PAIR_QK), lambda p, i: (rev(i), p))
    pair_v = pl.BlockSpec((tb, PAIR_V), lambda p, i: (rev(i), p))
    sel = _band_selector()
    ride_arrays, ride_gather = ride if ride else ([], [])
    nr = len(ride_arrays)
    grid = (npair, nblk)
    outs = pl.pallas_call(
        _riding(body, 8, 4, ride_gather, grid), grid=grid,
        in_specs=[qspec(O_GQ), qspec(O_GK), pl.BlockSpec((tb, PAIR_V), lambda p, i: (rev(i), O_GV // PAIR_V + p)),
                  pair_qk, pair_v, pl.BlockSpec((1, nbc, PAIR_V, PAIR_QK), lambda p, i: (p, rev(i), 0, 0)),
                  pl.BlockSpec((GLA_SUB * PAIR_QK, LANE), lambda p, i: (0, 0)),
                  pl.BlockSpec((LANE, GLA_SUB * PAIR_QK), lambda p, i: (0, 0))] + [HBM_SPEC] * nr,
        out_specs=[pair_qk, pair_qk, pair_v, pair_qk] + [HBM_SPEC] * nr,
        out_shape=[jax.ShapeDtypeStruct((S, GLA_QK), BF16), jax.ShapeDtypeStruct((S, GLA_QK), BF16),
                   jax.ShapeDtypeStruct((S, GLA_V), BF16), jax.ShapeDtypeStruct((S, GLA_QK), F32)]
        + _exchange_shapes(ride_arrays, ride_gather),
        scratch_shapes=[pltpu.VMEM((PAIR_V, PAIR_QK), F32)] + (_exchange_sems(nr) if nr else []),
        compiler_params=_cp("arbitrary", "arbitrary"), name="gla_bwd")(proj, proj, proj, la, do, states, sel, sel.T, *ride_arrays)
    return outs[0], outs[1], outs[2], outs[3], outs[4:]


def _gla_out(o, proj, gng, ts):
    S = o.shape[0]

    def body(o_ref, gr_ref, g_ref, y_ref):
        for h in range(GLA_HEADS):
            cols = slice(h * GLA_DV, (h + 1) * GLA_DV)
            ov, grv = o_ref[:, cols], gr_ref[:, cols]
            r = lax.rsqrt(jnp.mean(ov * ov, axis=-1, keepdims=True) + EPS)
            y_ref[:, cols] = (ov * r * g_ref[...] * (grv * _sigmoid(grv))).astype(BF16)

    return pl.pallas_call(
        body, grid=(S // ts,),
        in_specs=[pl.BlockSpec((ts, GLA_V), lambda i: (i, 0)), pl.BlockSpec((ts, GLA_V), lambda i: (i, O_GR // GLA_V)),
                  pl.BlockSpec((1, GLA_DV), lambda i: (0, 0))],
        out_specs=pl.BlockSpec((ts, GLA_V), lambda i: (i, 0)), out_shape=jax.ShapeDtypeStruct((S, GLA_V), BF16),
        compiler_params=_cp("parallel"), name="gla_out_fwd")(o, proj, gng)


def _gla_out_bwd(dmixed, o, proj, gng, ts):
    S = o.shape[0]

    def body(dy_ref, o_ref, gr_ref, g_ref, do_ref, dgr_ref, gg_ref):
        i = pl.program_id(0)
        gsum = jnp.zeros((1, GLA_DV), F32)
        for h in range(GLA_HEADS):
            cols = slice(h * GLA_DV, (h + 1) * GLA_DV)
            ov, grv, dy = o_ref[:, cols], gr_ref[:, cols], dy_ref[:, cols]
            r = lax.rsqrt(jnp.mean(ov * ov, axis=-1, keepdims=True) + EPS)
            oh = ov * r
            sg = _sigmoid(grv)
            silu = grv * sg
            don = dy * silu
            dgr_ref[:, cols] = (dy * (oh * g_ref[...]) * (sg * (1.0 + grv * (1.0 - sg)))).astype(BF16)
            gsum = gsum + jnp.sum(don * oh, axis=0, keepdims=True)
            doh = don * g_ref[...]
            do_ref[:, cols] = r * (doh - oh * jnp.mean(doh * oh, axis=-1, keepdims=True))
        part = jnp.concatenate([gsum, jnp.zeros((7, GLA_DV), F32)], axis=0)

        @pl.when(i == 0)
        def _():
            gg_ref[...] = part

        @pl.when(i > 0)
        def _():
            gg_ref[...] += part

    return pl.pallas_call(
        body, grid=(S // ts,),
        in_specs=[pl.BlockSpec((ts, GLA_V), lambda i: (i, 0)), pl.BlockSpec((ts, GLA_V), lambda i: (i, 0)),
                  pl.BlockSpec((ts, GLA_V), lambda i: (i, O_GR // GLA_V)), pl.BlockSpec((1, GLA_DV), lambda i: (0, 0))],
        out_specs=[pl.BlockSpec((ts, GLA_V), lambda i: (i, 0)), pl.BlockSpec((ts, GLA_V), lambda i: (i, 0)),
                   pl.BlockSpec((8, GLA_DV), lambda i: (0, 0))],
        out_shape=[jax.ShapeDtypeStruct((S, GLA_V), F32), jax.ShapeDtypeStruct((S, GLA_V), BF16),
                   jax.ShapeDtypeStruct((8, GLA_DV), F32)],
        compiler_params=_cp("arbitrary"), name="gla_out_bwd")(dmixed, o, proj, gng)


def _seg_matrix(width, seg, value):
    r = lax.broadcasted_iota(jnp.int32, (width, width), 0) // seg
    c = lax.broadcasted_iota(jnp.int32, (width, width), 1) // seg
    return jnp.where(r == c, value, 0.0).astype(BF16)


def _seg_sum(x, seg_matrix):
    hi = x.astype(BF16)
    lo = (x - hi.astype(F32)).astype(BF16)
    return _dot(hi, seg_matrix, NN) + _dot(lo, seg_matrix, NN)


def _head_norm(proj, qg, kg, ts):
    S = proj.shape[0]
    W = ATTN_DIM

    def body(q_ref, k_ref, qg_ref, kg_ref, qn_ref, kn_ref):
        seg = _seg_matrix(W, ATTN_HD, 1.0 / ATTN_HD)
        for x_ref, g_ref, o_ref, scale in ((q_ref, qg_ref, qn_ref, ATTN_HD ** -0.5), (k_ref, kg_ref, kn_ref, 1.0)):
            xv = x_ref[...]
            ms = _seg_sum(xv * xv, seg)
            o_ref[...] = xv * lax.rsqrt(ms + EPS) * (g_ref[...] * scale)

    blk = lambda off: pl.BlockSpec((ts, W), lambda i: (i, off // W))
    out = pl.BlockSpec((ts, W), lambda i: (i, 0))
    vec = pl.BlockSpec((1, W), lambda i: (0, 0))
    return pl.pallas_call(
        body, grid=(S // ts,), in_specs=[blk(O_AQ), blk(O_AK), vec, vec], out_specs=[out] * 2,
        out_shape=[jax.ShapeDtypeStruct((S, W), F32)] * 2, compiler_params=_cp("parallel"), name="attn_head_norm")(
            proj, proj, qg, kg)


def _slope(head):
    one = jnp.ones((1, 1), jnp.int32)
    return 1.0 / jnp.left_shift(one, one * (head + 1)).astype(F32)


ATTN_GROUP = 4


def _attn_rows(d, g, r):
    start = g * d * ATTN_BLOCK + r
    return pl.ds(start, ATTN_BLOCK) if d == 1 else pl.ds(start, ATTN_BLOCK, stride=d)


def _for_blocks(d, G, fn):
    for g in range(G):
        if d <= ATTN_GROUP:
            for r in range(d):
                fn(g, r)
        else:
            def step(r, carry, g=g):
                fn(g, r)
                return carry
            lax.fori_loop(0, d, step, 0, unroll=ATTN_GROUP)


def _attn_specs(d, S):
    G = max(1, ATTN_GROUP // d)
    edge = d * ATTN_BLOCK
    tq = G * edge
    nb, n_edge = S // tq, S // edge

    def specs(off=0):
        return [pl.BlockSpec((tq, LANE), lambda hp, n: (n, off + hp)),
                pl.BlockSpec((edge, LANE), lambda hp, n: (jnp.maximum(n * G - 1, 0), off + hp)),
                pl.BlockSpec((edge, LANE), lambda hp, n: (jnp.minimum((n + 1) * G, n_edge - 1), off + hp))]

    return G, nb, specs


def _attn_bias(d, hp, first_tile):
    B = ATTN_BLOCK
    iq = lax.broadcasted_iota(jnp.int32, (B, 2 * B), 0)
    ik = lax.broadcasted_iota(jnp.int32, (B, 2 * B), 1)
    rel = iq + B - ik
    window = (rel >= 0) & (rel <= B)
    relf = (d * rel).astype(F32)
    full = [jnp.where(window, -_slope(hp * 2 + h) * relf, NEG) for h in range(2)]
    edge = [jnp.where((ik >= B) | jnp.logical_not(first_tile), b, NEG) for b in full]
    return full, edge


def _attn_bias_t(d, hp, has_next):
    B = ATTN_BLOCK
    ik = lax.broadcasted_iota(jnp.int32, (B, B), 0)
    iq = lax.broadcasted_iota(jnp.int32, (B, B), 1)
    tiles = []
    for nxt in range(2):
        rel = iq - ik + nxt * B
        window = (rel >= 0) & (rel <= B)
        relf = (d * rel).astype(F32)
        tiles.append([jnp.where(window, -_slope(hp * 2 + h) * relf, NEG) for h in range(2)])
    tiles.append([jnp.where(has_next, b, NEG) for b in tiles[1]])
    return tiles


def _attn_fwd(qn, kn, proj, d):
    S, W = qn.shape
    G, nb, specs = _attn_specs(d, S)

    def body(q_ref, kp_ref, kc_ref, vp_ref, vc_ref, o_ref, l_ref):
        hp, n = pl.program_id(0), pl.program_id(1)
        lo = lax.broadcasted_iota(jnp.int32, (1, LANE), 1) < ATTN_HD
        full, edge = _attn_bias(d, hp, n == 0)

        def sub(g, r):
            rows = _attn_rows(d, g, r)
            before = _attn_rows(d, max(g - 1, 0), r)
            kb_ref, vb_ref = (kp_ref, vp_ref) if g == 0 else (kc_ref, vc_ref)
            bias = edge if g == 0 else full
            qv = q_ref[rows, :].astype(BF16)
            kv = jnp.concatenate([kb_ref[before, :], kc_ref[rows, :]], axis=0).astype(BF16)
            vv = jnp.concatenate([vb_ref[before, :], vc_ref[rows, :]], axis=0).astype(BF16)
            outs, lses = [], []
            for h in range(2):
                qm = jnp.where(lo == (h == 0), qv, jnp.zeros_like(qv))
                s = _dot(qm, kv, NT) + bias[h]
                m = jnp.max(s, axis=-1, keepdims=True)
                p = jnp.exp(s - m)
                den = jnp.sum(p, axis=-1, keepdims=True)
                outs.append(_dot(p.astype(BF16), vv, NN) / den)
                lses.append(m + jnp.log(den))
            o_ref[rows, :] = jnp.where(lo, outs[0], outs[1])
            l_ref[rows, :] = jnp.where(lo, lses[0], lses[1])

        _for_blocks(d, G, sub)

    cur, prev, _ = specs()
    vcur, vprev, _ = specs(O_AV // LANE)
    return pl.pallas_call(
        body, grid=(W // LANE, nb), in_specs=[cur, prev, cur, vprev, vcur], out_specs=[cur, cur],
        out_shape=[jax.ShapeDtypeStruct((S, W), F32)] * 2,
        compiler_params=_cp("parallel", "arbitrary"), name=f"attn_fwd_d{d}")(qn, kn, kn, proj, proj)


def _attn_merge(y_gla, os_, ls_, ts):
    S, W = os_[0].shape

    def body(yg, o1, o2, o3, l1, l2, l3, mixed_ref, mixed_t_ref, y_ref, lse_ref):
        a, b, c = l1[...], l2[...], l3[...]
        m = jnp.maximum(jnp.maximum(a, b), c)
        ea, eb, ec = jnp.exp(a - m), jnp.exp(b - m), jnp.exp(c - m)
        tot = ea + eb + ec
        y = (ea * o1[...] + eb * o2[...] + ec * o3[...]) / tot
        y_ref[...] = y
        mixed_ref[:, :W] = yg[...]
        mixed_ref[:, W:] = y.astype(BF16)
        mixed_t_ref[:W, :] = yg[...].astype(F32).T.astype(BF16)
        mixed_t_ref[W:, :] = y.T.astype(BF16)
        lse_ref[...] = m + jnp.log(tot)

    spec = pl.BlockSpec((ts, W), lambda i: (i, 0))
    return pl.pallas_call(
        body, grid=(S // ts,), in_specs=[spec] * 7,
        out_specs=[pl.BlockSpec((ts, 2 * W), lambda i: (i, 0)), _col_spec(2 * W, ts), spec, spec],
        out_shape=[jax.ShapeDtypeStruct((S, 2 * W), BF16), jax.ShapeDtypeStruct((2 * W, S), BF16),
                   jax.ShapeDtypeStruct((S, W), F32), jax.ShapeDtypeStruct((S, W), F32)],
        compiler_params=_cp("parallel"), name="attn_merge")(y_gla, *os_, *ls_)


def _attn_delta(dmixed, y, ts):
    S, W = y.shape

    def body(dy_ref, y_ref, d_ref):
        d_ref[...] = _seg_sum(dy_ref[...] * y_ref[...], _seg_matrix(W, ATTN_HD, 1.0))

    return pl.pallas_call(
        body, grid=(S // ts,), in_specs=[pl.BlockSpec((ts, W), lambda i: (i, 1)), pl.BlockSpec((ts, W), lambda i: (i, 0))],
        out_specs=pl.BlockSpec((ts, W), lambda i: (i, 0)), out_shape=jax.ShapeDtypeStruct((S, W), F32),
        compiler_params=_cp("parallel"), name="attn_delta")(dmixed, y)


def _attn_dq(qn, kn, proj, dmixed, lse, delta, d):
    S, W = qn.shape
    B = ATTN_BLOCK
    G, nb, specs = _attn_specs(d, S)

    def body(q_ref, kp_ref, kc_ref, vp_ref, vc_ref, dy_ref, l_ref, de_ref, dq_ref):
        hp, n = pl.program_id(0), pl.program_id(1)
        lo = lax.broadcasted_iota(jnp.int32, (1, LANE), 1) < ATTN_HD
        full, edge = _attn_bias(d, hp, n == 0)

        def sub(g, r):
            rows = _attn_rows(d, g, r)
            before = _attn_rows(d, max(g - 1, 0), r)
            kb_ref, vb_ref = (kp_ref, vp_ref) if g == 0 else (kc_ref, vc_ref)
            bias = edge if g == 0 else full
            qv, dyv = q_ref[rows, :].astype(BF16), dy_ref[rows, :]
            lv, dev = l_ref[rows, :], de_ref[rows, :]
            kv = jnp.concatenate([kb_ref[before, :], kc_ref[rows, :]], axis=0).astype(BF16)
            vv = jnp.concatenate([vb_ref[before, :], vc_ref[rows, :]], axis=0).astype(BF16)
            outs = []
            for h in range(2):
                sel = lo == (h == 0)
                qm = jnp.where(sel, qv, jnp.zeros_like(qv))
                dym = jnp.where(sel, dyv, 0.0).astype(BF16)
                lse_h = lv[:, h * ATTN_HD:h * ATTN_HD + 1]
                del_h = dev[:, h * ATTN_HD:h * ATTN_HD + 1]
                p = jnp.exp(_dot(qm, kv, NT) + bias[h] - lse_h)
                ds = p * (_dot(dym, vv, NT) - del_h)
                outs.append(_dot(ds.astype(BF16), kv, NN) * (ATTN_HD ** -0.5))
            dq_ref[rows, :] = jnp.where(lo, outs[0], outs[1])

        _for_blocks(d, G, sub)

    cur, prev, _ = specs()
    vcur, vprev, _ = specs(O_AV // LANE)
    dycur, _, _ = specs(W // LANE)
    return pl.pallas_call(
        body, grid=(W // LANE, nb), in_specs=[cur, prev, cur, vprev, vcur, dycur, cur, cur], out_specs=cur,
        out_shape=jax.ShapeDtypeStruct((S, W), F32),
        compiler_params=_cp("parallel", "arbitrary"), name=f"attn_dq_d{d}")(qn, kn, kn, proj, proj, dmixed, lse, delta)


def _attn_dkv(qn, kn, proj, dmixed, lse, delta, d):
    S, W = qn.shape
    B = ATTN_BLOCK
    G, nb, specs = _attn_specs(d, S)

    def body(k_ref, v_ref, qc_ref, qn_ref, dyc_ref, dyn_ref, lc_ref, ln_ref, dec_ref, den_ref, dk_ref, dv_ref):
        hp, n = pl.program_id(0), pl.program_id(1)
        lo = lax.broadcasted_iota(jnp.int32, (1, LANE), 1) < ATTN_HD
        own, inner, outer = _attn_bias_t(d, hp, n + 1 < nb)

        def sub(g, r):
            rows = _attn_rows(d, g, r)
            kv, vv = k_ref[rows, :].astype(BF16), v_ref[rows, :].astype(BF16)
            dk = jnp.zeros((B, LANE), F32)
            dv = jnp.zeros((B, LANE), F32)
            inside = g + 1 < G
            after = _attn_rows(d, g + 1 if inside else 0, r)
            following = (qc_ref, dyc_ref, lc_ref, dec_ref) if inside else (qn_ref, dyn_ref, ln_ref, den_ref)
            for bias, qrows, (q_ref, dy_ref, l_ref, de_ref) in (
                    (own, rows, (qc_ref, dyc_ref, lc_ref, dec_ref)), (inner if inside else outer, after, following)):
                qv, dyv = q_ref[qrows, :].astype(BF16), dy_ref[qrows, :]
                lt, det = l_ref[qrows, :].T, de_ref[qrows, :].T
                for h in range(2):
                    sel = lo == (h == 0)
                    qm = jnp.where(sel, qv, jnp.zeros_like(qv))
                    dym = jnp.where(sel, dyv, 0.0).astype(BF16)
                    lse_h = lt[h * ATTN_HD:h * ATTN_HD + 1, :]
                    del_h = det[h * ATTN_HD:h * ATTN_HD + 1, :]
                    pt = jnp.exp(_dot(kv, qm, NT) + bias[h] - lse_h)
                    dv = dv + _dot(pt.astype(BF16), dym, NN)
                    dst = pt * (_dot(vv, dym, NT) - del_h)
                    dk = dk + _dot(dst.astype(BF16), qm, NN)
            dk_ref[rows, :] = dk
            dv_ref[rows, :] = dv

        _for_blocks(d, G, sub)

    cur, _, nxt = specs()
    vcur, _, _ = specs(O_AV // LANE)
    dycur, _, dynxt = specs(W // LANE)
    return pl.pallas_call(
        body, grid=(W // LANE, nb), in_specs=[cur, vcur, cur, nxt, dycur, dynxt, cur, nxt, cur, nxt], out_specs=[cur, cur],
        out_shape=[jax.ShapeDtypeStruct((S, W), F32)] * 2,
        compiler_params=_cp("parallel", "arbitrary"), name=f"attn_dkv_d{d}")(
            kn, proj, qn, qn, dmixed, dmixed, lse, lse, delta, delta)


def _attn_post(dqs, dks, dvs, proj, qg, kg, ts):
    S = proj.shape[0]
    W = ATTN_DIM

    def body(dq1, dq2, dq3, dk1, dk2, dk3, dv1, dv2, dv3, aq_ref, ak_ref, qg_ref, kg_ref, daq_ref, dak_ref, dav_ref, gg_ref):
        i = pl.program_id(0)
        seg = _seg_matrix(W, ATTN_HD, 1.0 / ATTN_HD)
        gsums = []
        for (d1, d2, d3), x_ref, g_ref, o_ref in (((dq1, dq2, dq3), aq_ref, qg_ref, daq_ref), ((dk1, dk2, dk3), ak_ref, kg_ref, dak_ref)):
            dy = d1[...] + d2[...] + d3[...]
            xv = x_ref[...]
            r = lax.rsqrt(_seg_sum(xv * xv, seg) + EPS)
            xh = xv * r
            dxh = dy * g_ref[...]
            o_ref[...] = (r * (dxh - xh * _seg_sum(dxh * xh, seg))).astype(BF16)
            gsums.append(jnp.sum(dy * xh, axis=0, keepdims=True))
        dav_ref[...] = (dv1[...] + dv2[...] + dv3[...]).astype(BF16)
        part = jnp.concatenate(gsums + [jnp.zeros((6, W), F32)], axis=0)

        @pl.when(i == 0)
        def _():
            gg_ref[...] = part

        @pl.when(i > 0)
        def _():
            gg_ref[...] += part

    row = pl.BlockSpec((ts, W), lambda i: (i, 0))
    blk = lambda off: pl.BlockSpec((ts, W), lambda i: (i, off // W))
    vec = pl.BlockSpec((1, W), lambda i: (0, 0))
    return pl.pallas_call(
        body, grid=(S // ts,), in_specs=[row] * 9 + [blk(O_AQ), blk(O_AK), vec, vec],
        out_specs=[row, row, row, pl.BlockSpec((8, W), lambda i: (0, 0))],
        out_shape=[jax.ShapeDtypeStruct((S, W), BF16)] * 3 + [jax.ShapeDtypeStruct((8, W), F32)],
        compiler_params=_cp("arbitrary"), name="attn_post")(*dqs, *dks, *dvs, proj, proj, qg, kg)


def _shift_down(cur, halo, n):
    return pltpu.roll(jnp.concatenate([halo, cur], axis=0), n, 0)[8:]


def _shift_up(cur, halo, n):
    ts = cur.shape[0]
    return pltpu.roll(jnp.concatenate([cur, halo], axis=0), ts + 8 - n, 0)[:ts]


def _conv(cur, halo, w, b):
    return b + w[0:1, :] * _shift_down(cur, halo, 2) + w[1:2, :] * _shift_down(cur, halo, 1) + w[2:3, :] * cur


def _mm_up_swiglu(h2, w_up, conv_w8, conv_b, tm, tc, ride=None):
    S, D = h2.shape
    F = w_up.shape[1] // 2
    nc = F // tc
    grid = (S // tm, nc)
    ride_arrays, ride_gather = ride if ride else ([], [])
    nr = len(ride_arrays)

    def body(h_ref, bg_ref, bv_ref, wg_ref, wv_ref, cg_ref, cv_ref, u0_ref, a_ref, at_ref, halo):
        i, j = pl.program_id(0), pl.program_id(1)
        hv = h_ref[...]
        acts = []
        for h, (b_ref, w_ref, c_ref) in enumerate(((bg_ref, wg_ref, cg_ref), (bv_ref, wv_ref, cv_ref))):
            u = _dot(hv, b_ref[...], NN)
            u0_ref[h] = u
            acts.append(_conv(u, jnp.where(i == 0, 0.0, halo[j, h]), w_ref[...], c_ref[...]))
            halo[j, h] = u[tm - 8:, :]
        g, v = acts
        a = g * _sigmoid(g) * v
        a_ref[...] = a.astype(BF16)
        at_ref[...] = a.T.astype(BF16)

    wcol = lambda rows, off: pl.BlockSpec((rows, tc), lambda i, j: (0, j + off))
    outs = pl.pallas_call(
        _riding(body, 7, 3, ride_gather, grid), grid=grid,
        in_specs=[pl.BlockSpec((tm, D), lambda i, j: (i, 0)), wcol(D, 0), wcol(D, nc), wcol(8, 0), wcol(8, nc), wcol(1, 0), wcol(1, nc)]
        + [HBM_SPEC] * nr,
        out_specs=[pl.BlockSpec((2, tm, tc), lambda i, j: (0, i, j)), pl.BlockSpec((tm, tc), lambda i, j: (i, j)),
                   pl.BlockSpec((tc, tm), lambda i, j: (j, i))] + [HBM_SPEC] * nr,
        out_shape=[jax.ShapeDtypeStruct((2, S, F), F32), jax.ShapeDtypeStruct((S, F), BF16), jax.ShapeDtypeStruct((F, S), BF16)]
        + _exchange_shapes(ride_arrays, ride_gather),
        scratch_shapes=[pltpu.VMEM((nc, 2, 8, tc), F32)] + (_exchange_sems(nr) if nr else []),
        compiler_params=_cp("arbitrary", "arbitrary"), name="mm_up")(
            h2, w_up, w_up, conv_w8, conv_w8, conv_b, conv_b, *ride_arrays)
    return outs[0], outs[1], outs[2], outs[3:]


def _mm_da_du(dt2, w_down, u0, conv_w8, conv_b, tm, tc, ride=None):
    _, S, F = u0.shape
    D = dt2.shape[1]
    nc = F // tc
    hb = tm // 8
    grid = (nc, S // tm)
    ride_arrays, ride_gather = ride if ride else ([], [])
    nr = len(ride_arrays)

    def body(dt_ref, wd_ref, ug_ref, ugh_ref, uv_ref, uvh_ref, wg_ref, wv_ref, bg_ref, bv_ref, du_ref, sg_ref, sv_ref):
        i = pl.program_id(1)
        first = i == 0
        halves = []
        for u_ref, h_ref, w_ref, b_ref in ((ug_ref, ugh_ref, wg_ref, bg_ref), (uv_ref, uvh_ref, wv_ref, bv_ref)):
            u, halo, w = u_ref[...], jnp.where(first, 0.0, h_ref[...]), w_ref[...]
            s2, s1 = _shift_down(u, halo, 2), _shift_down(u, halo, 1)
            halves.append((b_ref[...] + w[0:1, :] * s2 + w[1:2, :] * s1 + w[2:3, :] * u, s2, s1, u))
        g, v = halves[0][0], halves[1][0]
        dav = _dot(dt_ref[...], wd_ref[...], NT)
        sig = _sigmoid(g)
        dus = (dav * v * (sig * (1.0 + g * (1.0 - sig))), dav * (g * sig))
        for h, (du, sums_ref) in enumerate(zip(dus, (sg_ref, sv_ref))):
            du_ref[h] = du
            _, s2, s1, u = halves[h]
            _accumulate(sums_ref, _rows8([jnp.sum(du * s2, axis=0, keepdims=True), jnp.sum(du * s1, axis=0, keepdims=True),
                                          jnp.sum(du * u, axis=0, keepdims=True), jnp.sum(du, axis=0, keepdims=True)], tc), i)

    main = lambda h: pl.BlockSpec((None, tm, tc), lambda j, i: (h, i, j))
    halo = lambda h: pl.BlockSpec((None, 8, tc), lambda j, i: (h, jnp.maximum(i * hb - 1, 0), j))
    wcol = lambda rows, off: pl.BlockSpec((rows, tc), lambda j, i: (0, j + off))
    sums_spec = pl.BlockSpec((8, tc), lambda j, i: (0, j))
    outs = pl.pallas_call(
        _riding(body, 10, 3, ride_gather, grid), grid=grid,
        in_specs=[pl.BlockSpec((tm, D), lambda j, i: (i, 0)), pl.BlockSpec((tc, D), lambda j, i: (j, 0)),
                  main(0), halo(0), main(1), halo(1), wcol(8, 0), wcol(8, nc), wcol(1, 0), wcol(1, nc)] + [HBM_SPEC] * nr,
        out_specs=[pl.BlockSpec((2, tm, tc), lambda j, i: (0, i, j)), sums_spec, sums_spec] + [HBM_SPEC] * nr,
        out_shape=[jax.ShapeDtypeStruct((2, S, F), F32), jax.ShapeDtypeStruct((8, F), F32), jax.ShapeDtypeStruct((8, F), F32)]
        + _exchange_shapes(ride_arrays, ride_gather),
        scratch_shapes=_exchange_sems(nr) if nr else [],
        compiler_params=_cp("arbitrary", "arbitrary"), name="mm_da")(
            dt2, w_down, u0, u0, u0, u0, conv_w8, conv_w8, conv_b, conv_b, *ride_arrays)
    return outs[0], outs[1], outs[2], outs[3:]


def _ffn_du0(du, conv_w8, ts, tc):
    _, S, F = du.shape
    nc = F // tc
    hb = ts // 8
    nrow = S // ts

    def body(du_ref, duh_ref, w_ref, o_ref):
        last = pl.program_id(0) == nrow - 1
        cur, halo, w = du_ref[...], jnp.where(last, 0.0, duh_ref[...]), w_ref[...]
        o_ref[...] = (w[2:3, :] * cur + w[1:2, :] * _shift_up(cur, halo, 1) + w[0:1, :] * _shift_up(cur, halo, 2)).astype(BF16)

    return pl.pallas_call(
        body, grid=(nrow, 2, nc),
        in_specs=[pl.BlockSpec((None, ts, tc), lambda i, h, j: (h, i, j)),
                  pl.BlockSpec((None, 8, tc), lambda i, h, j: (h, jnp.minimum((i + 1) * hb, S // 8 - 1), j)),
                  pl.BlockSpec((8, tc), lambda i, h, j: (0, h * nc + j))],
        out_specs=pl.BlockSpec((ts, tc), lambda i, h, j: (i, h * nc + j)), out_shape=jax.ShapeDtypeStruct((S, 2 * F), BF16),
        compiler_params=_cp("parallel", "parallel", "parallel"), name="ffn_du0")(du, du, conv_w8)


def _adamw(w, g, m, v, name):
    shape = w.shape
    view = (math.prod(shape[:-1]), shape[-1])
    R, C = view
    fits = [t for t in range(8, R + 1, 8) if R % t == 0 and t * C <= SUM_BLOCK_ELEMS]
    tr = max(fits) if fits else R

    def body(w_ref, g_ref, m_ref, v_ref, d_ref, nm_ref, nv_ref):
        gv = g_ref[...]
        nm = ADAM_B1 * m_ref[...] + (1.0 - ADAM_B1) * gv
        nv = ADAM_B2 * v_ref[...] + (1.0 - ADAM_B2) * (gv * gv)
        m_hat = nm / (1.0 - ADAM_B1 ** ADAM_STEP)
        v_hat = nv / (1.0 - ADAM_B2 ** ADAM_STEP)
        d_ref[...] = -ADAM_LR * (m_hat / (jnp.sqrt(v_hat) + ADAM_EPS) + ADAM_WD * w_ref[...])
        nm_ref[...] = nm
        nv_ref[...] = nv

    spec = pl.BlockSpec((tr, C), lambda i: (i, 0))
    outs = pl.pallas_call(
        body, grid=(R // tr,), in_specs=[spec] * 4, out_specs=[spec] * 3, out_shape=[jax.ShapeDtypeStruct(view, F32)] * 3,
        compiler_params=_cp("parallel"), name=name)(*[a.reshape(view) for a in (w, g, m, v)])
    return [o.reshape(shape) for o in outs]


def _pad_rows8(a):
    return jnp.concatenate([a, jnp.zeros((8 - a.shape[0], a.shape[1]), a.dtype)], axis=0)


def _local_step(x, target, mod, n1g, w_in_p, wg_p, bg, gng, qng, kng, w_out_s, n2g, w_up_s, conv_w, conv_b, w_down_s):
    S, D = x.shape
    F = w_down_s.shape[0] * N_DEV
    ts = min(512, S)
    sh1, sc1, g1, sh2, sc2, g2 = [mod[i:i + 1] for i in range(6)]
    conv_w8 = _pad_rows8(conv_w)
    qg_t, kg_t = jnp.tile(qng, (1, ATTN_HEADS)), jnp.tile(kng, (1, ATTN_HEADS))

    h1, h1_t = _rms_mod(x, n1g, sc1, sh1, ts, "rms_mod1")
    proj, (g_out,) = _mm(h1, w_in_p, NN, 512, PROJ_W, 1024, F32, "mm_in", ride=([w_out_s], [True]))
    w_out = g_out.reshape(-1, D)
    la = _gate_fwd(proj, wg_p, bg, ts)
    o_gla, states, (g_up,) = _gla_fwd(proj, la, 512, ride=([w_up_s], [True]))
    w_up = _cols_from_blocks(g_up)
    y_gla = _gla_out(o_gla, proj, gng, ts)
    qn, kn = _head_norm(proj, qg_t, kg_t, ts)
    branches = [_attn_fwd(qn, kn, proj, d) for d in DILATIONS]
    mixed, mixed_t, y_att, lse = _attn_merge(y_gla, [b[0] for b in branches], [b[1] for b in branches], ts)
    t1, x2, h2, h2_t = _mm_resid_rms_mod(mixed, w_out, x, g1, n2g, sc2, sh2, ts, "mm_out")
    tc = 1408 if F % 1408 == 0 else F
    u0, a, a_t, (g_down,) = _mm_up_swiglu(h2, w_up, conv_w8, conv_b, ts, tc, ride=([w_down_s], [True]))
    w_down = g_down.reshape(F, D)
    dx3, dt2, sums3 = _mm_loss_resid(a, w_down, x2, g2, target, ts, "mm_down")
    loss_row, dg2 = sums3[0:1], sums3[1:2]

    g_w_down = _mm(a_t, dt2, NN, 1408, 1024, 2048, F32, "mm_gw_down")
    du, sums_g, sums_v, (r_down,) = _mm_da_du(dt2, w_down, u0, conv_w8, conv_b, ts, tc,
                                              ride=([g_w_down.reshape(N_DEV, -1, D)], [False]))
    g_conv_w = jnp.concatenate([sums_g[0:3], sums_v[0:3]], axis=1)
    g_conv_b = jnp.concatenate([sums_g[3:4], sums_v[3:4]], axis=1)
    du0 = _ffn_du0(du, conv_w8, min(256, S), tc)
    g_w_up = _mm(h2_t, du0, NN, 512, 2816, 2048, F32, "mm_gw_up")
    (dx2, sums2, dt1), _ = _mm_rms_mod_bwd(du0, w_up, x2, dx3, n2g, sc2, ts, "mm_dh2", t_prev=t1, g_prev=g1)
    dsh2, dsc2, g_n2g, dg1 = sums2[0:1], sums2[1:2], sums2[2:3], sums2[3:4]
    g_w_out = _mm(mixed_t, dt1, NN, 1024, 1024, 2048, F32, "mm_gw_out")
    dmixed = _mm(dt1, w_out, NT, 512, 1024, 1024, F32, "mm_dmixed")
    do_gla, dgr, gng_sums = _gla_out_bwd(dmixed, o_gla, proj, gng, ts)
    dgq, dgk, dgv, dla, (r_up, r_out) = _gla_bwd(
        proj, la, do_gla, states, 512, ride=([_col_blocks(g_w_up), g_w_out.reshape(N_DEV, -1, D)], [False, False]))
    dglr, g_wg_p, gb_sums = _gate_bwd(dla, la, proj, wg_p, ts)
    delta = _attn_delta(dmixed, y_att, ts)
    dqs = [_attn_dq(qn, kn, proj, dmixed, lse, delta, d) for d in DILATIONS]
    dkvs = [_attn_dkv(qn, kn, proj, dmixed, lse, delta, d) for d in DILATIONS]
    daq, dak, dav, qk_sums = _attn_post(dqs, [t[0] for t in dkvs], [t[1] for t in dkvs], proj, qg_t, kg_t, ts)
    dproj = jnp.concatenate([dgq, dgk, dgv, dgr, daq, dak, dav, dglr, jnp.zeros((S, PROJ_W - O_GLR - LANE), BF16)], axis=1)
    g_w_in_p = _mm(h1_t, dproj, NN, 512, PROJ_W, 1024, F32, "mm_gw_in")
    g_w_in = jnp.concatenate([g_w_in_p[:, :GLR_SRC], g_w_in_p[:, O_GLR:O_GLR + GLA_RANK], g_w_in_p[:, GLR_SRC:O_GLR]], axis=1)
    (dx, sums1), (r_in,) = _mm_rms_mod_bwd(dproj, w_in_p, x, dx2, n1g, sc1, ts, "mm_dh1",
                                           ride=([_col_blocks(g_w_in).astype(BF16)], [False]))
    dsh1, dsc1, g_n1g = sums1[0:1], sums1[1:2], sums1[2:3]

    dmod = jnp.concatenate([dsh1, dsc1, dg1, dsh2, dsc2, dg2], axis=1)
    grads = dict(n1g=g_n1g, w_in=r_in, wg=g_wg_p[:GLA_RANK], bg=gb_sums[0:1], gng=gng_sums[0:1],
                 qng_lanes=qk_sums[0:1], kng_lanes=qk_sums[1:2], w_out=r_out, n2g=g_n2g, w_up=r_up,
                 conv_w=g_conv_w, conv_b=g_conv_b, w_down=r_down)
    return loss_row, dx, dmod, grads


def _col_blocks(a):
    R, W = a.shape
    return a.reshape(R, N_DEV, W // N_DEV).transpose(1, 0, 2)


def _cols_from_blocks(a):
    n, R, C = a.shape
    return a.transpose(1, 0, 2).reshape(R, n * C)


def kernel(x, c, w_ada, b_ada, norm1_g, w_in, gla_w_gate, gla_b_gate, gla_norm_g, q_norm_g, k_norm_g, w_out, norm2_g, w_up, conv_w, conv_b, w_down, loss_target, m_w_ada, m_b_ada, m_norm1_g, m_w_in, m_gla_w_gate, m_gla_b_gate, m_gla_norm_g, m_q_norm_g, m_k_norm_g, m_w_out, m_norm2_g, m_w_up, m_conv_w, m_conv_b, m_w_down, v_w_ada, v_b_ada, v_norm1_g, v_w_in, v_gla_w_gate, v_gla_b_gate, v_gla_norm_g, v_q_norm_g, v_k_norm_g, v_w_out, v_norm2_g, v_w_up, v_conv_w, v_conv_b, v_w_down):
    axes = ("x", "y", "c")
    me = 4 * lax.axis_index("x") + 2 * lax.axis_index("y") + lax.axis_index("c")
    S, D = x.shape[1], x.shape[2]
    x2d, tgt2d = x[0], loss_target[0]
    w_in_s, w_out_s, w_up_s, w_down_s, w_ada_s = w_in[0], w_out[0], w_up[0], w_down[0], w_ada[0]
    conv_w_s, wg_s = conv_w[0], gla_w_gate[0]
    in_c, up_c, ada_c, wg_c, cw_c = w_in_s.shape[1], w_up_s.shape[1], w_ada_s.shape[1], wg_s.shape[1], conv_w_s.shape[1]
    F = w_down_s.shape[0] * N_DEV

    small = jnp.concatenate([conv_w_s.reshape(1, -1), wg_s.reshape(1, -1)], axis=1)
    n_small = small.shape[1]
    small = jnp.pad(small, ((0, 0), (0, -n_small % LANE)))
    g_c, g_in, g_small = _exchange([c, w_in_s.astype(BF16), small], [True] * 3, "gather_w_in")
    c_all = g_c.reshape(N_DEV, D)
    w_in_full = _cols_from_blocks(g_in)
    w_in_p = jnp.concatenate([w_in_full[:, :GLR_SRC], w_in_full[:, GLR_SRC + GLA_RANK:],
                              w_in_full[:, GLR_SRC:GLR_SRC + GLA_RANK], jnp.zeros((D, PROJ_W - O_GLR - GLA_RANK), BF16)], axis=1)
    g_small = g_small.reshape(N_DEV, -1)
    conv_w_full = jnp.stack([g_small[:, t * cw_c:(t + 1) * cw_c].reshape(-1) for t in range(3)])
    wg_full = _cols_from_blocks(g_small[:, 3 * cw_c:n_small].reshape(N_DEV, GLA_RANK, wg_c))
    wg_p = jnp.concatenate([wg_full, jnp.zeros((LANE - GLA_RANK, wg_full.shape[1]), F32)], axis=0)

    b_shard = lax.dynamic_slice(b_ada, (0, me * ada_c), (1, ada_c))
    mod_part = _ada_fwd(c_all, w_ada_s, b_shard)
    mod_recv, = _exchange([mod_part.reshape(N_DEV, 1, ada_c)], [False], "exchange_mod")
    mod = mod_recv.reshape(6, D)

    loss_row, dx, dmod, gr = _local_step(
        x2d, tgt2d, mod, norm1_g, w_in_p, wg_p, gla_b_gate, gla_norm_g, q_norm_g, k_norm_g,
        w_out_s.astype(BF16), norm2_g, w_up_s.astype(BF16), conv_w_full, conv_b, w_down_s.astype(BF16))
    loss = lax.psum(0.5 / D * jnp.sum(loss_row), axes)

    parts = [dmod, gr["n1g"], gr["bg"], gr["gng"], gr["qng_lanes"], gr["kng_lanes"], gr["n2g"], gr["conv_b"],
             gr["wg"].reshape(1, -1), gr["conv_w"].reshape(1, -1)]
    sizes = [p.shape[1] for p in parts]
    packed = jnp.concatenate(parts, axis=1)
    packed = jnp.pad(packed, ((0, 0), (0, -packed.shape[1] % (8 * LANE))))
    gathered, = _exchange([packed.reshape(8, -1)], [True], "gather_small_grads")
    gathered = gathered.reshape(N_DEV, -1)
    total = _sum_slots(gathered.reshape(N_DEV, 8, -1), "sum_small_grads").reshape(1, -1)
    offs = [0]
    for s_ in sizes:
        offs.append(offs[-1] + s_)
    t_dmod, t_n1g, t_bg, t_gng, t_qng, t_kng, t_n2g, t_conv_b, t_wg, t_conv_w = [
        total[:, offs[i]:offs[i + 1]] for i in range(len(sizes))]
    g_b_ada = t_dmod
    g_qng = t_qng.reshape(ATTN_HEADS, ATTN_HD).sum(axis=0, keepdims=True)
    g_kng = t_kng.reshape(ATTN_HEADS, ATTN_HD).sum(axis=0, keepdims=True)
    g_wg = lax.dynamic_slice(t_wg.reshape(GLA_RANK, -1), (0, me * wg_c), (GLA_RANK, wg_c))
    g_conv_w = lax.dynamic_slice(t_conv_w.reshape(3, -1), (0, me * cw_c), (3, cw_c))
    dmod_shard = lax.dynamic_slice(gathered[:, :6 * D], (0, me * ada_c), (N_DEV, ada_c))
    g_w_ada = _ada_bwd(c_all, dmod_shard)

    g_w_in = _sum_slots(gr["w_in"], "sum_gw_in")
    g_w_out = _sum_slots(gr["w_out"], "sum_gw_out")
    g_w_up = _sum_slots(gr["w_up"], "sum_gw_up")
    g_w_down = _sum_slots(gr["w_down"], "sum_gw_down")

    names = ["w_ada", "b_ada", "norm1_g", "w_in", "gla_w_gate", "gla_b_gate", "gla_norm_g", "q_norm_g", "k_norm_g",
             "w_out", "norm2_g", "w_up", "conv_w", "conv_b", "w_down"]
    ws = [w_ada, b_ada, norm1_g, w_in, gla_w_gate, gla_b_gate, gla_norm_g, q_norm_g, k_norm_g, w_out, norm2_g, w_up, conv_w, conv_b, w_down]
    ms = [m_w_ada, m_b_ada, m_norm1_g, m_w_in, m_gla_w_gate, m_gla_b_gate, m_gla_norm_g, m_q_norm_g, m_k_norm_g, m_w_out, m_norm2_g, m_w_up, m_conv_w, m_conv_b, m_w_down]
    vs = [v_w_ada, v_b_ada, v_norm1_g, v_w_in, v_gla_w_gate, v_gla_b_gate, v_gla_norm_g, v_q_norm_g, v_k_norm_g, v_w_out, v_norm2_g, v_w_up, v_conv_w, v_conv_b, v_w_down]
    gs = [g_w_ada, g_b_ada, t_n1g, g_w_in, g_wg, t_bg, t_gng, g_qng, g_kng, g_w_out, t_n2g, g_w_up, g_conv_w, t_conv_b, g_w_down]
    gs = [g.reshape(w.shape) for g, w in zip(gs, ws)]
    deltas, new_ms, new_vs = [], [], []
    for nm, w, g, m, v in zip(names, ws, gs, ms, vs):
        d_, m_, v_ = _adamw(w, g, m, v, "adamw_" + nm)
        deltas.append(d_)
        new_ms.append(m_)
        new_vs.append(v_)
    return (loss, dx.reshape(x.shape), *gs, *deltas, *new_ms, *new_vs)
```

```python
import functools
import math

import jax
import jax.numpy as jnp
from jax import lax
from jax.experimental import pallas as pl
from jax.experimental.pallas import tpu as pltpu

F32, BF16 = jnp.float32, jnp.bfloat16
HI = lax.Precision.HIGHEST
EPS = 1e-6
NEG = -1e30

N_DEV = 8
GLA_HEADS, GLA_DK, GLA_DV, GLA_RANK, GLA_TAU, GLA_CHUNK = 4, 64, 128, 16, 16.0, 64
ATTN_HEADS, ATTN_HD, ATTN_BLOCK = 8, 64, 128
DILATIONS = (1, 4, 16)
GLA_QK, GLA_V, ATTN_DIM = GLA_HEADS * GLA_DK, GLA_HEADS * GLA_DV, ATTN_HEADS * ATTN_HD
O_GQ, O_GK, O_GV, O_GR, O_AQ, O_AK, O_AV, O_GLR = 0, 256, 512, 1024, 1536, 2048, 2560, 3072
PROJ_W = 3328
LANE = 128
GLR_SRC = 2 * GLA_QK + 2 * GLA_V

ADAM_LR, ADAM_B1, ADAM_B2, ADAM_EPS, ADAM_WD, ADAM_STEP = 0.001, 0.9, 0.999, 1e-08, 0.01, 10

VMEM_LIMIT = 56 * 1024 * 1024
SUM_BLOCK_ELEMS = 256 * 1024


def _cp(*sem):
    return pltpu.CompilerParams(dimension_semantics=sem, vmem_limit_bytes=VMEM_LIMIT)


def _dot(a, b, dims, precision=None):
    return lax.dot_general(a, b, (dims, ((), ())), preferred_element_type=F32, precision=precision)


NN, NT, TN = ((1,), (0,)), ((1,), (1,)), ((0,), (0,))


def _sigmoid(z):
    return 1.0 / (1.0 + jnp.exp(-z))


HBM_SPEC = pl.BlockSpec(memory_space=pltpu.HBM)


def _exchange_shapes(arrays, gather):
    return [jax.ShapeDtypeStruct((N_DEV,) + (a.shape if g else a.shape[1:]), a.dtype) for a, g in zip(arrays, gather)]


def _exchange_sems(n):
    return [pltpu.SemaphoreType.DMA((n * (N_DEV - 1),)), pltpu.SemaphoreType.DMA((n * (N_DEV - 1),)), pltpu.SemaphoreType.DMA((n,))]


def _exchange_copies(ins, outs, gather, send_sems, recv_sems, local_sems):
    x, y, c = lax.axis_index("x"), lax.axis_index("y"), lax.axis_index("c")
    me = 4 * x + 2 * y + c
    copies = []
    for a in range(len(ins)):
        for p in range(1, N_DEV):
            px, py, pc = x ^ (p >> 2), y ^ ((p >> 1) & 1), c ^ (p & 1)
            peer = 4 * px + 2 * py + pc
            k = a * (N_DEV - 1) + p - 1
            copies.append(pltpu.make_async_remote_copy(
                src_ref=ins[a] if gather[a] else ins[a].at[peer], dst_ref=outs[a].at[me],
                send_sem=send_sems.at[k], recv_sem=recv_sems.at[k],
                device_id=(px, py, pc), device_id_type=pl.DeviceIdType.MESH))
        copies.append(pltpu.make_async_copy(ins[a] if gather[a] else ins[a].at[me], outs[a].at[me], local_sems.at[a]))
    return copies


def _riding(body, n_in, n_out, gather, grid):
    nr = len(gather)
    if not nr:
        return body

    def wrapped(*refs):
        ins, r_ins = refs[:n_in], refs[n_in:n_in + nr]
        outs, r_outs = refs[n_in + nr:n_in + nr + n_out], refs[n_in + nr + n_out:n_in + 2 * nr + n_out]
        scratch = refs[n_in + 2 * nr + n_out:]
        first = last = None
        for t, steps in enumerate(grid):
            pid = pl.program_id(t)
            first = (pid == 0) if first is None else first & (pid == 0)
            last = (pid == steps - 1) if last is None else last & (pid == steps - 1)
        copies = _exchange_copies(r_ins, r_outs, gather, *scratch[-3:])

        @pl.when(first)
        def _():
            for cp in copies:
                cp.start()

        body(*ins, *outs, *scratch[:-3])

        @pl.when(last)
        def _():
            for cp in copies:
                cp.wait()

    return wrapped


def _exchange(arrays, gather, name):
    n = len(arrays)

    def body(*refs):
        copies = _exchange_copies(refs[:n], refs[n:2 * n], gather, *refs[2 * n:])
        for cp in copies:
            cp.start()
        for cp in copies:
            cp.wait()

    return pl.pallas_call(
        body, out_shape=_exchange_shapes(arrays, gather), in_specs=[HBM_SPEC] * n, out_specs=[HBM_SPEC] * n,
        scratch_shapes=_exchange_sems(n), name=name)(*arrays)


def _sum_slots(x, name):
    _, R, C = x.shape
    tr = max(t for t in range(8, min(SUM_BLOCK_ELEMS // C, R) + 1, 8) if R % t == 0)

    def body(x_ref, o_ref):
        acc = x_ref[0].astype(F32)
        for s in range(1, N_DEV):
            acc = acc + x_ref[s].astype(F32)
        o_ref[...] = acc

    return pl.pallas_call(
        body, grid=(R // tr,), in_specs=[pl.BlockSpec((N_DEV, tr, C), lambda i: (0, i, 0))],
        out_specs=pl.BlockSpec((tr, C), lambda i: (i, 0)), out_shape=jax.ShapeDtypeStruct((R, C), F32),
        compiler_params=_cp("parallel"), name=name)(x)


def _mm(a, b, mode, tm, tn, tk, out_dtype, name, ride=None):
    if mode == NN:
        (M, K), N = a.shape, b.shape[1]
    elif mode == NT:
        (M, K), N = a.shape, b.shape[0]
    else:
        (K, M), N = a.shape, b.shape[1]
    tm, tn, tk = min(tm, M), min(tn, N), min(tk, K)
    assert M % tm == 0 and N % tn == 0 and K % tk == 0, (name, M, N, K, tm, tn, tk)
    nk = K // tk
    if mode == NN:
        a_spec = pl.BlockSpec((tm, tk), lambda i, j, k: (i, k))
        b_spec = pl.BlockSpec((tk, tn), lambda i, j, k: (k, j))
    elif mode == NT:
        a_spec = pl.BlockSpec((tm, tk), lambda i, j, k: (i, k))
        b_spec = pl.BlockSpec((tn, tk), lambda i, j, k: (j, k))
    else:
        a_spec = pl.BlockSpec((tk, tm), lambda i, j, k: (k, i))
        b_spec = pl.BlockSpec((tk, tn), lambda i, j, k: (k, j))

    ride_arrays, ride_gather = ride if ride else ([], [])
    nr = len(ride_arrays)
    grid = (M // tm, N // tn, nk)

    own_acc = nk > 1 and out_dtype != F32

    def body(a_ref, b_ref, o_ref, *acc):
        p = _dot(a_ref[...].astype(BF16), b_ref[...].astype(BF16), mode)
        if nk == 1:
            o_ref[...] = p.astype(out_dtype)
        else:
            acc_ref = acc[0] if own_acc else o_ref
            k = pl.program_id(2)

            @pl.when(k == 0)
            def _():
                acc_ref[...] = p

            @pl.when(k > 0)
            def _():
                acc_ref[...] += p

            if own_acc:
                @pl.when(k == nk - 1)
                def _():
                    o_ref[...] = acc_ref[...].astype(out_dtype)

    outs = pl.pallas_call(
        _riding(body, 2, 1, ride_gather, grid), grid=grid, in_specs=[a_spec, b_spec] + [HBM_SPEC] * nr,
        out_specs=[pl.BlockSpec((tm, tn), lambda i, j, k: (i, j))] + [HBM_SPEC] * nr,
        out_shape=[jax.ShapeDtypeStruct((M, N), out_dtype)] + _exchange_shapes(ride_arrays, ride_gather),
        scratch_shapes=([pltpu.VMEM((tm, tn), F32)] if own_acc else []) + (_exchange_sems(nr) if nr else []),
        compiler_params=_cp(*(("arbitrary",) * 3 if nr else ("parallel", "parallel", "arbitrary"))), name=name)(a, b, *ride_arrays)
    return (outs[0], outs[1:]) if nr else outs[0]


def _ada_fwd(c_all, w_shard, b_shard):
    Nc = w_shard.shape[1]

    def body(c_ref, w_ref, b_ref, o_ref):
        cv = c_ref[...]
        o_ref[...] = _dot(cv * _sigmoid(cv), w_ref[...], NN, HI) + b_ref[...]

    return pl.pallas_call(body, out_shape=jax.ShapeDtypeStruct((N_DEV, Nc), F32), name="ada_fwd",
                          compiler_params=pltpu.CompilerParams(vmem_limit_bytes=VMEM_LIMIT))(c_all, w_shard, b_shard)


def _ada_bwd(c_all, dmod_shard):
    D, Nc = c_all.shape[1], dmod_shard.shape[1]

    def body(c_ref, d_ref, o_ref):
        cv = c_ref[...]
        o_ref[...] = _dot(cv * _sigmoid(cv), d_ref[...], TN, HI)

    return pl.pallas_call(body, out_shape=jax.ShapeDtypeStruct((D, Nc), F32), name="ada_bwd",
                          compiler_params=pltpu.CompilerParams(vmem_limit_bytes=VMEM_LIMIT))(c_all, dmod_shard)


def _row_spec(ts, D):
    return pl.BlockSpec((ts, D), lambda i: (i, 0))


def _vec_spec(D):
    return pl.BlockSpec((1, D), lambda i: (0, 0))


def _col_spec(D, ts):
    return pl.BlockSpec((D, ts), lambda i: (0, i))


def _rms_mod(x, ng, sc, sh, ts, name):
    S, D = x.shape

    def body(x_ref, ng_ref, sc_ref, sh_ref, h_ref, ht_ref):
        xv = x_ref[...]
        r = lax.rsqrt(jnp.mean(xv * xv, axis=-1, keepdims=True) + EPS)
        h = xv * r * ng_ref[...] * (1.0 + sc_ref[...]) + sh_ref[...]
        h_ref[...] = h.astype(BF16)
        ht_ref[...] = h.T.astype(BF16)

    return pl.pallas_call(
        body, grid=(S // ts,), in_specs=[_row_spec(ts, D)] + [_vec_spec(D)] * 3, out_specs=[_row_spec(ts, D), _col_spec(D, ts)],
        out_shape=[jax.ShapeDtypeStruct((S, D), BF16), jax.ShapeDtypeStruct((D, S), BF16)],
        compiler_params=_cp("parallel"), name=name)(x, ng, sc, sh)


def _mm_rows(a, b, mode, tm, extras, extra_specs, out_shapes, out_specs, epilogue, name, ride=None):
    M, K = a.shape
    grid = (M // tm,)
    ride_arrays, ride_gather = ride if ride else ([], [])
    nr = len(ride_arrays)

    def body(a_ref, b_ref, *refs):
        epilogue(_dot(a_ref[...].astype(BF16), b_ref[...].astype(BF16), mode), pl.program_id(0), *refs)

    outs = pl.pallas_call(
        _riding(body, 2 + len(extras), len(out_shapes), ride_gather, grid), grid=grid,
        in_specs=[pl.BlockSpec((tm, K), lambda i: (i, 0)), pl.BlockSpec(b.shape, lambda i: (0, 0), pipeline_mode=pl.Buffered(1))]
        + list(extra_specs) + [HBM_SPEC] * nr,
        out_specs=list(out_specs) + [HBM_SPEC] * nr,
        out_shape=list(out_shapes) + _exchange_shapes(ride_arrays, ride_gather),
        scratch_shapes=_exchange_sems(nr) if nr else [],
        compiler_params=_cp("arbitrary"), name=name)(a, b, *extras, *ride_arrays)
    return outs[:len(out_shapes)], outs[len(out_shapes):]


def _accumulate(ref, part, step):
    @pl.when(step == 0)
    def _():
        ref[...] = part

    @pl.when(step > 0)
    def _():
        ref[...] += part


def _rows8(rows, width):
    return jnp.concatenate(rows + [jnp.zeros((8 - len(rows), width), F32)], axis=0)


def _mm_resid_rms_mod(a, w, x, g, ng, sc, sh, tm, name):
    S, D = x.shape

    def epilogue(t, step, x_ref, g_ref, ng_ref, sc_ref, sh_ref, t_ref, x2_ref, h_ref, ht_ref):
        t_ref[...] = t
        xv = x_ref[...] + g_ref[...] * t
        x2_ref[...] = xv
        r = lax.rsqrt(jnp.mean(xv * xv, axis=-1, keepdims=True) + EPS)
        h = xv * r * ng_ref[...] * (1.0 + sc_ref[...]) + sh_ref[...]
        h_ref[...] = h.astype(BF16)
        ht_ref[...] = h.T.astype(BF16)

    row, vec = _row_spec(tm, D), _vec_spec(D)
    full, half = jax.ShapeDtypeStruct((S, D), F32), jax.ShapeDtypeStruct((S, D), BF16)
    outs, _ = _mm_rows(a, w, NN, tm, [x, g, ng, sc, sh], [row] + [vec] * 4,
                       [full, full, half, jax.ShapeDtypeStruct((D, S), BF16)], [row, row, row, _col_spec(D, tm)], epilogue, name)
    return outs


def _mm_loss_resid(a, w, x2, g2, target, tm, name):
    S, D = x2.shape

    def epilogue(t, step, x_ref, y_ref, g_ref, dx_ref, dt_ref, sums_ref):
        gv = g_ref[...]
        e = x_ref[...] + gv * t - y_ref[...]
        dx = e * (1.0 / D)
        dx_ref[...] = dx
        dt_ref[...] = (dx * gv).astype(BF16)
        _accumulate(sums_ref, _rows8([jnp.sum(e * e, axis=0, keepdims=True), jnp.sum(dx * t, axis=0, keepdims=True)], D), step)

    row, vec = _row_spec(tm, D), _vec_spec(D)
    outs, _ = _mm_rows(a, w, NN, tm, [x2, target, g2], [row, row, vec],
                       [jax.ShapeDtypeStruct((S, D), F32), jax.ShapeDtypeStruct((S, D), BF16), jax.ShapeDtypeStruct((8, D), F32)],
                       [row, row, pl.BlockSpec((8, D), lambda i: (0, 0))], epilogue, name)
    return outs


def _mm_rms_mod_bwd(a, w, xin, dres, ng, sc, tm, name, t_prev=None, g_prev=None, ride=None):
    S, D = xin.shape
    chain = t_prev is not None

    def epilogue(dhv, step, *refs):
        if chain:
            x_ref, dr_ref, ng_ref, sc_ref, t_ref, g_ref, dx_ref, sums_ref, dt_ref = refs
        else:
            x_ref, dr_ref, ng_ref, sc_ref, dx_ref, sums_ref = refs
        xv = x_ref[...]
        r = lax.rsqrt(jnp.mean(xv * xv, axis=-1, keepdims=True) + EPS)
        xh = xv * r
        ngv, scv = ng_ref[...], sc_ref[...]
        dxh = dhv * (ngv * (1.0 + scv))
        dx = dr_ref[...] + r * (dxh - xh * jnp.mean(dxh * xh, axis=-1, keepdims=True))
        dx_ref[...] = dx
        dhx = dhv * xh
        rows = [jnp.sum(dhv, axis=0, keepdims=True), jnp.sum(dhx * ngv, axis=0, keepdims=True),
                jnp.sum(dhx * (1.0 + scv), axis=0, keepdims=True)]
        if chain:
            dt_ref[...] = (dx * g_ref[...]).astype(BF16)
            rows.append(jnp.sum(dx * t_ref[...], axis=0, keepdims=True))
        _accumulate(sums_ref, _rows8(rows, D), step)

    row, vec = _row_spec(tm, D), _vec_spec(D)
    extras = [xin, dres, ng, sc] + ([t_prev, g_prev] if chain else [])
    extra_specs = [row, row, vec, vec] + ([row, vec] if chain else [])
    out_shapes = [jax.ShapeDtypeStruct((S, D), F32), jax.ShapeDtypeStruct((8, D), F32)] + (
        [jax.ShapeDtypeStruct((S, D), BF16)] if chain else [])
    out_specs = [row, pl.BlockSpec((8, D), lambda i: (0, 0))] + ([row] if chain else [])
    return _mm_rows(a, w, NT, tm, extras, extra_specs, out_shapes, out_specs, epilogue, name, ride=ride)


def _gate_fwd(proj, wg_p, bg, ts):
    S = proj.shape[0]

    def body(glr_ref, w_ref, b_ref, la_ref):
        z = _dot(glr_ref[...], w_ref[...], NN, HI) + b_ref[...]
        la_ref[...] = (jnp.minimum(z, 0.0) - jnp.log(1.0 + jnp.exp(-jnp.abs(z)))) * (1.0 / GLA_TAU)

    return pl.pallas_call(
        body, grid=(S // ts,),
        in_specs=[pl.BlockSpec((ts, LANE), lambda i: (i, O_GLR // LANE)), pl.BlockSpec((LANE, GLA_QK), lambda i: (0, 0)),
                  pl.BlockSpec((1, GLA_QK), lambda i: (0, 0))],
        out_specs=pl.BlockSpec((ts, GLA_QK), lambda i: (i, 0)), out_shape=jax.ShapeDtypeStruct((S, GLA_QK), F32),
        compiler_params=_cp("parallel"), name="gla_gate_fwd")(proj, wg_p, bg)


def _gate_bwd(dla, la, proj, wg_p, ts):
    S = proj.shape[0]

    def body(dla_ref, la_ref, glr_ref, w_ref, dglr_ref, gw_ref, gb_ref):
        i = pl.program_id(0)
        dz = dla_ref[...] * (1.0 / GLA_TAU) * (1.0 - jnp.exp(GLA_TAU * la_ref[...]))
        dglr_ref[...] = _dot(dz, w_ref[...], NT, HI).astype(BF16)
        gw = _dot(glr_ref[...], dz, TN, HI)
        gb = jnp.concatenate([jnp.sum(dz, axis=0, keepdims=True), jnp.zeros((7, GLA_QK), F32)], axis=0)

        @pl.when(i == 0)
        def _():
            gw_ref[...] = gw
            gb_ref[...] = gb

        @pl.when(i > 0)
        def _():
            gw_ref[...] += gw
            gb_ref[...] += gb

    return pl.pallas_call(
        body, grid=(S // ts,),
        in_specs=[pl.BlockSpec((ts, GLA_QK), lambda i: (i, 0)), pl.BlockSpec((ts, GLA_QK), lambda i: (i, 0)),
                  pl.BlockSpec((ts, LANE), lambda i: (i, O_GLR // LANE)), pl.BlockSpec((LANE, GLA_QK), lambda i: (0, 0))],
        out_specs=[pl.BlockSpec((ts, LANE), lambda i: (i, 0)), pl.BlockSpec((LANE, GLA_QK), lambda i: (0, 0)),
                   pl.BlockSpec((8, GLA_QK), lambda i: (0, 0))],
        out_shape=[jax.ShapeDtypeStruct((S, LANE), BF16), jax.ShapeDtypeStruct((LANE, GLA_QK), F32),
                   jax.ShapeDtypeStruct((8, GLA_QK), F32)],
        compiler_params=_cp("arbitrary"), name="gla_gate_bwd")(dla, la, proj, wg_p)


def _tri(lower):
    r = lax.broadcasted_iota(jnp.int32, (GLA_CHUNK, GLA_CHUNK), 0)
    c = lax.broadcasted_iota(jnp.int32, (GLA_CHUNK, GLA_CHUNK), 1)
    return jnp.where((r >= c) if lower else (c >= r), 1.0, 0.0).astype(F32)


GLA_SUB = 16
GLA_NSUB = GLA_CHUNK // GLA_SUB
PAIR_QK = 2 * GLA_DK
PAIR_V = 2 * GLA_DV


def _band_selector():
    r = lax.broadcasted_iota(jnp.int32, (GLA_SUB * PAIR_QK, LANE), 0)
    c = lax.broadcasted_iota(jnp.int32, (GLA_SUB * PAIR_QK, LANE), 1)
    dist, head = r // PAIR_QK, (r % PAIR_QK) // GLA_DK
    return jnp.where(c == head * GLA_DK + (GLA_SUB - 1 - dist), 1.0, 0.0).astype(BF16)


def _flip_matrix():
    r = lax.broadcasted_iota(jnp.int32, (GLA_CHUNK, GLA_CHUNK), 0)
    c = lax.broadcasted_iota(jnp.int32, (GLA_CHUNK, GLA_CHUNK), 1)
    return jnp.where(r + c == GLA_CHUNK - 1, 1.0, 0.0).astype(BF16)


def _state_mask():
    r = lax.broadcasted_iota(jnp.int32, (PAIR_V, PAIR_QK), 0)
    c = lax.broadcasted_iota(jnp.int32, (PAIR_V, PAIR_QK), 1)
    return (r < GLA_DV) == (c < GLA_DK)


class _GlaChunk:
    def __init__(self, qs, kc, vc, g, sel):
        C = GLA_CHUNK
        self.qs, self.kc, self.vc = qs, kc, vc
        rows = lax.broadcasted_iota(jnp.int32, (C, 1), 0)
        lane = lax.broadcasted_iota(jnp.int32, (1, PAIR_QK), 1)
        self.rows, self.lane = rows, lane
        b = _dot(_tri(True), g, NN, HI)
        self.bl = b[C - 1:C, :]
        self.eb = jnp.exp(b)
        self.kdec = jnp.exp(self.bl - b)
        edge = lambda J: b[GLA_SUB * (J + 1):GLA_SUB * (J + 1) + 1, :]
        self.e_far = [jnp.exp(jnp.where(rows >= GLA_SUB * (J + 1), b - edge(J), NEG)) for J in range(GLA_NSUB - 1)]
        blk = rows // GLA_SUB
        bnext = edge(0)
        for J in range(1, GLA_NSUB - 1):
            bnext = jnp.where(blk == J, edge(J), bnext)
        self.e_khat = jnp.exp(jnp.where(blk < GLA_NSUB - 1, bnext - b, NEG))
        khat = kc * self.e_khat
        k2 = jnp.concatenate([jnp.where(lane < GLA_DK, khat, 0.0), jnp.where(lane >= GLA_DK, khat, 0.0)], axis=0)
        self.blk2 = jnp.concatenate([blk, blk], axis=0)
        self.m_far = jnp.concatenate([jnp.where(self.blk2 == J, k2, 0.0) for J in range(GLA_NSUB - 1)], axis=1).astype(BF16)
        self.qcat = jnp.concatenate([qs * e for e in self.e_far], axis=1).astype(BF16)
        a_far = _dot(self.qcat, self.m_far, NT)
        self.e_band, self.rk, terms = [], [], []
        for d in range(GLA_SUB):
            rk = pltpu.roll(kc, d, 0) if d else kc
            rb = pltpu.roll(b, d, 0) if d else b
            e = jnp.exp(jnp.where(rows >= d, b - rb, NEG))
            self.e_band.append(e)
            self.rk.append(rk)
            terms.append((qs * rk * e).astype(BF16))
        band = _dot(jnp.concatenate(terms, axis=1), sel, NN)
        a_band = pltpu.roll(band, LANE - (GLA_SUB - 1), 1, stride=1, stride_axis=0)
        dist = rows - lane % GLA_DK
        self.far_mask = dist >= GLA_SUB
        self.band_mask = (dist >= 0) & (dist < GLA_SUB)
        self.a = (a_band + jnp.where(self.far_mask, a_far, 0.0)).astype(BF16)
        self.lane_v = lax.broadcasted_iota(jnp.int32, (1, PAIR_V), 1)
        self.v2 = jnp.concatenate([jnp.where(self.lane_v < GLA_DV, vc, 0.0), jnp.where(self.lane_v >= GLA_DV, vc, 0.0)],
                                  axis=0).astype(BF16)


def _gla_fwd(proj, la, tb, ride=None):
    S = proj.shape[0]
    C = GLA_CHUNK
    tb = min(tb, S)
    nbc = tb // C
    npair = GLA_HEADS // 2
    scale = GLA_DK ** -0.5

    def body(q_ref, k_ref, v_ref, la_ref, sel_ref, o_ref, st_ref, state):
        @pl.when(pl.program_id(1) == 0)
        def _():
            state[...] = jnp.zeros_like(state)

        def chunk(ci, carry):
            sl = pl.ds(pl.multiple_of(ci * C, C), C)
            ch = _GlaChunk(q_ref[sl, :] * scale, k_ref[sl, :], v_ref[sl, :], la_ref[sl, :], sel_ref[...])
            st = state[...]
            st_ref[0, ci] = st
            o_ref[sl, :] = _dot((ch.qs * ch.eb).astype(BF16), st.astype(BF16), NT) + _dot(ch.a, ch.v2, NN)
            upd = _dot(ch.vc.astype(BF16), (ch.kc * ch.kdec).astype(BF16), TN)
            state[...] = st * jnp.exp(ch.bl) + jnp.where(_state_mask(), upd, 0.0)
            return carry

        lax.fori_loop(0, nbc, chunk, 0, unroll=8)

    qspec = lambda off: pl.BlockSpec((tb, PAIR_QK), lambda p, i: (i, off // PAIR_QK + p))
    ride_arrays, ride_gather = ride if ride else ([], [])
    nr = len(ride_arrays)
    grid = (npair, S // tb)
    outs = pl.pallas_call(
        _riding(body, 5, 2, ride_gather, grid), grid=grid,
        in_specs=[qspec(O_GQ), qspec(O_GK), pl.BlockSpec((tb, PAIR_V), lambda p, i: (i, O_GV // PAIR_V + p)),
                  pl.BlockSpec((tb, PAIR_QK), lambda p, i: (i, p)),
                  pl.BlockSpec((GLA_SUB * PAIR_QK, LANE), lambda p, i: (0, 0))] + [HBM_SPEC] * nr,
        out_specs=[pl.BlockSpec((tb, PAIR_V), lambda p, i: (i, p)),
                   pl.BlockSpec((1, nbc, PAIR_V, PAIR_QK), lambda p, i: (p, i, 0, 0))] + [HBM_SPEC] * nr,
        out_shape=[jax.ShapeDtypeStruct((S, GLA_V), F32), jax.ShapeDtypeStruct((npair, S // C, PAIR_V, PAIR_QK), F32)]
        + _exchange_shapes(ride_arrays, ride_gather),
        scratch_shapes=[pltpu.VMEM((PAIR_V, PAIR_QK), F32)] + (_exchange_sems(nr) if nr else []),
        compiler_params=_cp("arbitrary", "arbitrary"), name="gla_fwd")(proj, proj, proj, la, _band_selector(), *ride_arrays)
    return outs[0], outs[1], outs[2:]


def _gla_bwd(proj, la, do, states, tb, ride=None):
    S = proj.shape[0]
    C = GLA_CHUNK
    tb = min(tb, S)
    nbc = tb // C
    nblk = S // tb
    npair = GLA_HEADS // 2
    scale = GLA_DK ** -0.5

    def body(q_ref, k_ref, v_ref, la_ref, do_ref, st_ref, sel_ref, selt_ref, dq_ref, dk_ref, dv_ref, dla_ref, dstate):
        @pl.when(pl.program_id(1) == 0)
        def _():
            dstate[...] = jnp.zeros_like(dstate)

        def chunk(cc, carry):
            ci = nbc - 1 - cc
            sl = pl.ds(pl.multiple_of(ci * C, C), C)
            ch = _GlaChunk(q_ref[sl, :] * scale, k_ref[sl, :], v_ref[sl, :], la_ref[sl, :], sel_ref[...])
            qs, kc, rows = ch.qs, ch.kc, ch.rows
            doc_b = do_ref[sl, :].astype(BF16)
            st = st_ref[0, ci]
            dst = dstate[...]
            dst_b = dst.astype(BF16)
            ebl = jnp.exp(ch.bl)
            dq = _dot(doc_b, st.astype(BF16), NN) * ch.eb
            dk = _dot(ch.vc.astype(BF16), dst_b, NN) * ch.kdec
            dv = _dot((kc * ch.kdec).astype(BF16), dst_b, NT)
            dbl = jnp.sum(dst * st, axis=0, keepdims=True) * ebl + jnp.sum(kc * dk, axis=0, keepdims=True)
            da = _dot(doc_b, ch.v2, NT)
            dv2 = _dot(ch.a, doc_b, TN)
            dv = dv + jnp.where(ch.lane_v < GLA_DV, dv2[:C], dv2[C:])
            da_far = jnp.where(ch.far_mask, da, 0.0).astype(BF16)
            dqcat = _dot(da_far, ch.m_far, NN)
            dm = _dot(da_far, ch.qcat, TN)
            dk2 = jnp.zeros((2 * C, PAIR_QK), F32)
            for J in range(GLA_NSUB - 1):
                dq = dq + dqcat[:, J * PAIR_QK:(J + 1) * PAIR_QK] * ch.e_far[J]
                dk2 = dk2 + jnp.where(ch.blk2 == J, dm[:, J * PAIR_QK:(J + 1) * PAIR_QK], 0.0)
            dk = dk + jnp.where(ch.lane < GLA_DK, dk2[:C], dk2[C:]) * ch.e_khat
            flip = _flip_matrix()
            da_band = _dot(flip, jnp.where(ch.band_mask, da, 0.0).astype(BF16), NN)
            dband = pltpu.roll(da_band, LANE - (C - GLA_SUB), 1, stride=1, stride_axis=0)
            dband = _dot(flip, dband.astype(BF16), NN)
            dterms = _dot(dband.astype(BF16), selt_ref[...], NN)
            for d in range(GLA_SUB):
                dt = dterms[:, d * PAIR_QK:(d + 1) * PAIR_QK]
                dq = dq + dt * (ch.rk[d] * ch.e_band[d])
                dkr = dt * (qs * ch.e_band[d])
                dk = dk + (pltpu.roll(dkr, C - d, 0) if d else dkr)
            db = qs * dq - kc * dk
            db = jnp.where(rows == C - 1, db + dbl, db)
            dq_ref[sl, :] = (dq * scale).astype(BF16)
            dk_ref[sl, :] = dk.astype(BF16)
            dv_ref[sl, :] = dv.astype(BF16)
            dla_ref[sl, :] = _dot(_tri(False), db, NN, HI)
            upd = _dot(doc_b, (qs * ch.eb).astype(BF16), TN)
            dstate[...] = dst * ebl + jnp.where(_state_mask(), upd, 0.0)
            return carry

        lax.fori_loop(0, nbc, chunk, 0, unroll=8)

    rev = lambda i: nblk - 1 - i
    qspec = lambda off: pl.BlockSpec((tb, PAIR_QK), lambda p, i: (rev(i), off // PAIR_QK + p))
    pair_qk = pl.BlockSpec((tb, PAIR_QK), lambda p, i: (rev(i), p))
    pair_v = pl.BlockSpec((tb, PAIR_V), lambda p, i: (rev(i), p))
    sel = _band_selector()
    ride_arrays, ride_gather = ride if ride else ([], [])
    nr = len(ride_arrays)
    grid = (npair, nblk)
    outs = pl.pallas_call(
        _riding(body, 8, 4, ride_gather, grid), grid=grid,
        in_specs=[qspec(O_GQ), qspec(O_GK), pl.BlockSpec((tb, PAIR_V), lambda p, i: (rev(i), O_GV // PAIR_V + p)),
                  pair_qk, pair_v, pl.BlockSpec((1, nbc, PAIR_V, PAIR_QK), lambda p, i: (p, rev(i), 0, 0)),
                  pl.BlockSpec((GLA_SUB * PAIR_QK, LANE), lambda p, i: (0, 0)),
                  pl.BlockSpec((LANE, GLA_SUB * PAIR_QK), lambda p, i: (0, 0))] + [HBM_SPEC] * nr,
        out_specs=[pair_qk, pair_qk, pair_v, pair_qk] + [HBM_SPEC] * nr,
        out_shape=[jax.ShapeDtypeStruct((S, GLA_QK), BF16), jax.ShapeDtypeStruct((S, GLA_QK), BF16),
                   jax.ShapeDtypeStruct((S, GLA_V), BF16), jax.ShapeDtypeStruct((S, GLA_QK), F32)]
        + _exchange_shapes(ride_arrays, ride_gather),
        scratch_shapes=[pltpu.VMEM((PAIR_V, PAIR_QK), F32)] + (_exchange_sems(nr) if nr else []),
        compiler_params=_cp("arbitrary", "arbitrary"), name="gla_bwd")(proj, proj, proj, la, do, states, sel, sel.T, *ride_arrays)
    return outs[0], outs[1], outs[2], outs[3], outs[4:]


def _gla_out(o, proj, gng, ts):
    S = o.shape[0]

    def body(o_ref, gr_ref, g_ref, y_ref):
        for h in range(GLA_HEADS):
            cols = slice(h * GLA_DV, (h + 1) * GLA_DV)
            ov, grv = o_ref[:, cols], gr_ref[:, cols]
            r = lax.rsqrt(jnp.mean(ov * ov, axis=-1, keepdims=True) + EPS)
            y_ref[:, cols] = (ov * r * g_ref[...] * (grv * _sigmoid(grv))).astype(BF16)

    return pl.pallas_call(
        body, grid=(S // ts,),
        in_specs=[pl.BlockSpec((ts, GLA_V), lambda i: (i, 0)), pl.BlockSpec((ts, GLA_V), lambda i: (i, O_GR // GLA_V)),
                  pl.BlockSpec((1, GLA_DV), lambda i: (0, 0))],
        out_specs=pl.BlockSpec((ts, GLA_V), lambda i: (i, 0)), out_shape=jax.ShapeDtypeStruct((S, GLA_V), BF16),
        compiler_params=_cp("parallel"), name="gla_out_fwd")(o, proj, gng)


def _gla_out_bwd(dmixed, o, proj, gng, ts):
    S = o.shape[0]

    def body(dy_ref, o_ref, gr_ref, g_ref, do_ref, dgr_ref, gg_ref):
        i = pl.program_id(0)
        gsum = jnp.zeros((1, GLA_DV), F32)
        for h in range(GLA_HEADS):
            cols = slice(h * GLA_DV, (h + 1) * GLA_DV)
            ov, grv, dy = o_ref[:, cols], gr_ref[:, cols], dy_ref[:, cols]
            r = lax.rsqrt(jnp.mean(ov * ov, axis=-1, keepdims=True) + EPS)
            oh = ov * r
            sg = _sigmoid(grv)
            silu = grv * sg
            don = dy * silu
            dgr_ref[:, cols] = (dy * (oh * g_ref[...]) * (sg * (1.0 + grv * (1.0 - sg)))).astype(BF16)
            gsum = gsum + jnp.sum(don * oh, axis=0, keepdims=True)
            doh = don * g_ref[...]
            do_ref[:, cols] = r * (doh - oh * jnp.mean(doh * oh, axis=-1, keepdims=True))
        part = jnp.concatenate([gsum, jnp.zeros((7, GLA_DV), F32)], axis=0)

        @pl.when(i == 0)
        def _():
            gg_ref[...] = part

        @pl.when(i > 0)
        def _():
            gg_ref[...] += part

    return pl.pallas_call(
        body, grid=(S // ts,),
        in_specs=[pl.BlockSpec((ts, GLA_V), lambda i: (i, 0)), pl.BlockSpec((ts, GLA_V), lambda i: (i, 0)),
                  pl.BlockSpec((ts, GLA_V), lambda i: (i, O_GR // GLA_V)), pl.BlockSpec((1, GLA_DV), lambda i: (0, 0))],
        out_specs=[pl.BlockSpec((ts, GLA_V), lambda i: (i, 0)), pl.BlockSpec((ts, GLA_V), lambda i: (i, 0)),
                   pl.BlockSpec((8, GLA_DV), lambda i: (0, 0))],
        out_shape=[jax.ShapeDtypeStruct((S, GLA_V), F32), jax.ShapeDtypeStruct((S, GLA_V), BF16),
                   jax.ShapeDtypeStruct((8, GLA_DV), F32)],
        compiler_params=_cp("arbitrary"), name="gla_out_bwd")(dmixed, o, proj, gng)


def _seg_matrix(width, seg, value):
    r = lax.broadcasted_iota(jnp.int32, (width, width), 0) // seg
    c = lax.broadcasted_iota(jnp.int32, (width, width), 1) // seg
    return jnp.where(r == c, value, 0.0).astype(BF16)


def _seg_sum(x, seg_matrix):
    hi = x.astype(BF16)
    lo = (x - hi.astype(F32)).astype(BF16)
    return _dot(hi, seg_matrix, NN) + _dot(lo, seg_matrix, NN)


def _head_norm(proj, qg, kg, ts):
    S = proj.shape[0]
    W = ATTN_DIM

    def body(q_ref, k_ref, qg_ref, kg_ref, qn_ref, kn_ref):
        seg = _seg_matrix(W, ATTN_HD, 1.0 / ATTN_HD)
        for x_ref, g_ref, o_ref, scale in ((q_ref, qg_ref, qn_ref, ATTN_HD ** -0.5), (k_ref, kg_ref, kn_ref, 1.0)):
            xv = x_ref[...]
            ms = _seg_sum(xv * xv, seg)
            o_ref[...] = xv * lax.rsqrt(ms + EPS) * (g_ref[...] * scale)

    blk = lambda off: pl.BlockSpec((ts, W), lambda i: (i, off // W))
    out = pl.BlockSpec((ts, W), lambda i: (i, 0))
    vec = pl.BlockSpec((1, W), lambda i: (0, 0))
    return pl.pallas_call(
        body, grid=(S // ts,), in_specs=[blk(O_AQ), blk(O_AK), vec, vec], out_specs=[out] * 2,
        out_shape=[jax.ShapeDtypeStruct((S, W), F32)] * 2, compiler_params=_cp("parallel"), name="attn_head_norm")(
            proj, proj, qg, kg)


def _slope(head):
    one = jnp.ones((1, 1), jnp.int32)
    return 1.0 / jnp.left_shift(one, one * (head + 1)).astype(F32)


ATTN_GROUP = 4


ATTN_TILE = max(DILATIONS) * ATTN_BLOCK


def _attn_rows(d, g, r, base=0):
    start = base + (g * d * ATTN_BLOCK if g >= 0 else ATTN_TILE - d * ATTN_BLOCK) + r
    return pl.ds(start, ATTN_BLOCK) if d == 1 else pl.ds(start, ATTN_BLOCK, stride=d)


def _for_blocks(d, G, fn):
    for g in range(G):
        if d <= ATTN_GROUP:
            for r in range(d):
                fn(g, r)
        else:
            def step(r, carry, g=g):
                fn(g, r)
                return carry
            lax.fori_loop(0, d, step, 0, unroll=ATTN_GROUP)


def _attn_specs(S):
    nb = S // ATTN_TILE

    def specs(off=0):
        return [pl.BlockSpec((ATTN_TILE, LANE), lambda hp, n: (n, off + hp)),
                pl.BlockSpec((ATTN_TILE, LANE), lambda hp, n: (jnp.maximum(n - 1, 0), off + hp)),
                pl.BlockSpec((ATTN_TILE, LANE), lambda hp, n: (jnp.minimum(n + 1, nb - 1), off + hp))]

    return nb, specs


def _attn_bias(d, hp, first_tile):
    B = ATTN_BLOCK
    iq = lax.broadcasted_iota(jnp.int32, (B, 2 * B), 0)
    ik = lax.broadcasted_iota(jnp.int32, (B, 2 * B), 1)
    rel = iq + B - ik
    window = (rel >= 0) & (rel <= B)
    relf = (d * rel).astype(F32)
    full = [jnp.where(window, -_slope(hp * 2 + h) * relf, NEG) for h in range(2)]
    edge = [jnp.where((ik >= B) | jnp.logical_not(first_tile), b, NEG) for b in full]
    return full, edge


def _attn_bias_t(d, hp, has_next):
    B = ATTN_BLOCK
    ik = lax.broadcasted_iota(jnp.int32, (B, B), 0)
    iq = lax.broadcasted_iota(jnp.int32, (B, B), 1)
    tiles = []
    for nxt in range(2):
        rel = iq - ik + nxt * B
        window = (rel >= 0) & (rel <= B)
        relf = (d * rel).astype(F32)
        tiles.append([jnp.where(window, -_slope(hp * 2 + h) * relf, NEG) for h in range(2)])
    tiles.append([jnp.where(has_next, b, NEG) for b in tiles[1]])
    return tiles


def _attn_fwd(qn, kn, proj):
    S, W = qn.shape
    T = ATTN_TILE
    nb, specs = _attn_specs(S)

    def body(q_ref, kp_ref, kc_ref, vp_ref, vc_ref, y_ref, l_ref, o_scr, l_scr):
        hp, n = pl.program_id(0), pl.program_id(1)
        lo = lax.broadcasted_iota(jnp.int32, (1, LANE), 1) < ATTN_HD
        for b, d in enumerate(DILATIONS):
            full, edge = _attn_bias(d, hp, n == 0)

            def sub(g, r, b=b, d=d, full=full, edge=edge):
                rows, before = _attn_rows(d, g, r), _attn_rows(d, g - 1, r)
                kb_ref, vb_ref = (kp_ref, vp_ref) if g == 0 else (kc_ref, vc_ref)
                bias = edge if g == 0 else full
                qv = q_ref[rows, :].astype(BF16)
                kv = jnp.concatenate([kb_ref[before, :], kc_ref[rows, :]], axis=0).astype(BF16)
                vv = jnp.concatenate([vb_ref[before, :], vc_ref[rows, :]], axis=0).astype(BF16)
                outs, lses = [], []
                for h in range(2):
                    qm = jnp.where(lo == (h == 0), qv, jnp.zeros_like(qv))
                    s = _dot(qm, kv, NT) + bias[h]
                    m = jnp.max(s, axis=-1, keepdims=True)
                    p = jnp.exp(s - m)
                    den = jnp.sum(p, axis=-1, keepdims=True)
                    outs.append(_dot(p.astype(BF16), vv, NN) / den)
                    lses.append(m + jnp.log(den))
                kept = _attn_rows(d, g, r, base=b * T)
                o_scr[kept, :] = jnp.where(lo, outs[0], outs[1])
                l_scr[kept, :] = jnp.where(lo, lses[0], lses[1])

            _for_blocks(d, T // (d * ATTN_BLOCK), sub)
        l1, l2, l3 = [l_scr[pl.ds(b * T, T), :] for b in range(len(DILATIONS))]
        o1, o2, o3 = [o_scr[pl.ds(b * T, T), :] for b in range(len(DILATIONS))]
        m = jnp.maximum(jnp.maximum(l1, l2), l3)
        e1, e2, e3 = jnp.exp(l1 - m), jnp.exp(l2 - m), jnp.exp(l3 - m)
        tot = e1 + e2 + e3
        y_ref[...] = (e1 * o1 + e2 * o2 + e3 * o3) / tot
        l_ref[...] = m + jnp.log(tot)

    cur, prev, _ = specs()
    vcur, vprev, _ = specs(O_AV // LANE)
    return pl.pallas_call(
        body, grid=(W // LANE, nb), in_specs=[cur, prev, cur, vprev, vcur], out_specs=[cur, cur],
        out_shape=[jax.ShapeDtypeStruct((S, W), F32)] * 2,
        scratch_shapes=[pltpu.VMEM((len(DILATIONS) * T, LANE), F32)] * 2,
        compiler_params=_cp("parallel", "arbitrary"), name="attn_fwd")(qn, kn, kn, proj, proj)


def _attn_mix(y_gla, y_att, ts):
    S, W = y_att.shape

    def body(yg, ya, mixed_ref, mixed_t_ref):
        y = ya[...]
        mixed_ref[:, :W] = yg[...]
        mixed_ref[:, W:] = y.astype(BF16)
        mixed_t_ref[:W, :] = yg[...].astype(F32).T.astype(BF16)
        mixed_t_ref[W:, :] = y.T.astype(BF16)

    spec = pl.BlockSpec((ts, W), lambda i: (i, 0))
    return pl.pallas_call(
        body, grid=(S // ts,), in_specs=[spec] * 2,
        out_specs=[pl.BlockSpec((ts, 2 * W), lambda i: (i, 0)), _col_spec(2 * W, ts)],
        out_shape=[jax.ShapeDtypeStruct((S, 2 * W), BF16), jax.ShapeDtypeStruct((2 * W, S), BF16)],
        compiler_params=_cp("parallel"), name="attn_mix")(y_gla, y_att)


def _attn_delta(dmixed, y, ts):
    S, W = y.shape

    def body(dy_ref, y_ref, d_ref):
        d_ref[...] = _seg_sum(dy_ref[...] * y_ref[...], _seg_matrix(W, ATTN_HD, 1.0))

    return pl.pallas_call(
        body, grid=(S // ts,), in_specs=[pl.BlockSpec((ts, W), lambda i: (i, 1)), pl.BlockSpec((ts, W), lambda i: (i, 0))],
        out_specs=pl.BlockSpec((ts, W), lambda i: (i, 0)), out_shape=jax.ShapeDtypeStruct((S, W), F32),
        compiler_params=_cp("parallel"), name="attn_delta")(dmixed, y)


def _attn_dq(qn, kn, proj, dmixed, lse, delta):
    S, W = qn.shape
    nb, specs = _attn_specs(S)

    def body(q_ref, kp_ref, kc_ref, vp_ref, vc_ref, dy_ref, l_ref, de_ref, dq_ref):
        hp, n = pl.program_id(0), pl.program_id(1)
        lo = lax.broadcasted_iota(jnp.int32, (1, LANE), 1) < ATTN_HD
        for b, d in enumerate(DILATIONS):
            _attn_dq_branch(b, d, _attn_bias(d, hp, n == 0), lo, q_ref, kp_ref, kc_ref, vp_ref, vc_ref, dy_ref, l_ref, de_ref, dq_ref)

    cur, prev, _ = specs()
    vcur, vprev, _ = specs(O_AV // LANE)
    dycur, _, _ = specs(W // LANE)
    return pl.pallas_call(
        body, grid=(W // LANE, nb), in_specs=[cur, prev, cur, vprev, vcur, dycur, cur, cur], out_specs=cur,
        out_shape=jax.ShapeDtypeStruct((S, W), F32),
        compiler_params=_cp("parallel", "arbitrary"), name="attn_dq")(qn, kn, kn, proj, proj, dmixed, lse, delta)


def _attn_dq_branch(b, d, biases, lo, q_ref, kp_ref, kc_ref, vp_ref, vc_ref, dy_ref, l_ref, de_ref, dq_ref):
    full, edge = biases

    def sub(g, r):
        rows, before = _attn_rows(d, g, r), _attn_rows(d, g - 1, r)
        kb_ref, vb_ref = (kp_ref, vp_ref) if g == 0 else (kc_ref, vc_ref)
        bias = edge if g == 0 else full
        qv, dyv = q_ref[rows, :].astype(BF16), dy_ref[rows, :]
        lv, dev = l_ref[rows, :], de_ref[rows, :]
        kv = jnp.concatenate([kb_ref[before, :], kc_ref[rows, :]], axis=0).astype(BF16)
        vv = jnp.concatenate([vb_ref[before, :], vc_ref[rows, :]], axis=0).astype(BF16)
        outs = []
        for h in range(2):
            sel = lo == (h == 0)
            qm = jnp.where(sel, qv, jnp.zeros_like(qv))
            dym = jnp.where(sel, dyv, 0.0).astype(BF16)
            lse_h = lv[:, h * ATTN_HD:h * ATTN_HD + 1]
            del_h = dev[:, h * ATTN_HD:h * ATTN_HD + 1]
            p = jnp.exp(_dot(qm, kv, NT) + bias[h] - lse_h)
            ds = p * (_dot(dym, vv, NT) - del_h)
            outs.append(_dot(ds.astype(BF16), kv, NN) * (ATTN_HD ** -0.5))
        dq = jnp.where(lo, outs[0], outs[1])
        dq_ref[rows, :] = dq if b == 0 else dq_ref[rows, :] + dq

    _for_blocks(d, ATTN_TILE // (d * ATTN_BLOCK), sub)


def _attn_dkv(qn, kn, proj, dmixed, lse, delta):
    S, W = qn.shape
    B = ATTN_BLOCK
    nb, specs = _attn_specs(S)

    def body(k_ref, v_ref, qc_ref, qn_ref, dyc_ref, dyn_ref, lc_ref, ln_ref, dec_ref, den_ref, dk_ref, dv_ref):
        hp, n = pl.program_id(0), pl.program_id(1)
        lo = lax.broadcasted_iota(jnp.int32, (1, LANE), 1) < ATTN_HD
        cur_refs, next_refs = (qc_ref, dyc_ref, lc_ref, dec_ref), (qn_ref, dyn_ref, ln_ref, den_ref)
        for b, d in enumerate(DILATIONS):
            _attn_dkv_branch(b, d, _attn_bias_t(d, hp, n + 1 < nb), lo, k_ref, v_ref, cur_refs, next_refs, dk_ref, dv_ref)

    cur, _, nxt = specs()
    vcur, _, _ = specs(O_AV // LANE)
    dycur, _, dynxt = specs(W // LANE)
    return pl.pallas_call(
        body, grid=(W // LANE, nb), in_specs=[cur, vcur, cur, nxt, dycur, dynxt, cur, nxt, cur, nxt], out_specs=[cur, cur],
        out_shape=[jax.ShapeDtypeStruct((S, W), F32)] * 2,
        compiler_params=_cp("parallel", "arbitrary"), name="attn_dkv")(
            kn, proj, qn, qn, dmixed, dmixed, lse, lse, delta, delta)


def _attn_dkv_branch(b, d, biases, lo, k_ref, v_ref, cur_refs, next_refs, dk_ref, dv_ref):
    B = ATTN_BLOCK
    own, inner, outer = biases
    G = ATTN_TILE // (d * B)

    def sub(g, r):
        rows = _attn_rows(d, g, r)
        kv, vv = k_ref[rows, :].astype(BF16), v_ref[rows, :].astype(BF16)
        dk = jnp.zeros((B, LANE), F32)
        dv = jnp.zeros((B, LANE), F32)
        inside = g + 1 < G
        after = _attn_rows(d, g + 1 if inside else 0, r)
        for bias, qrows, (q_ref, dy_ref, l_ref, de_ref) in (
                (own, rows, cur_refs), (inner if inside else outer, after, cur_refs if inside else next_refs)):
            qv, dyv = q_ref[qrows, :].astype(BF16), dy_ref[qrows, :]
            lt, det = l_ref[qrows, :].T, de_ref[qrows, :].T
            for h in range(2):
                sel = lo == (h == 0)
                qm = jnp.where(sel, qv, jnp.zeros_like(qv))
                dym = jnp.where(sel, dyv, 0.0).astype(BF16)
                lse_h = lt[h * ATTN_HD:h * ATTN_HD + 1, :]
                del_h = det[h * ATTN_HD:h * ATTN_HD + 1, :]
                pt = jnp.exp(_dot(kv, qm, NT) + bias[h] - lse_h)
                dv = dv + _dot(pt.astype(BF16), dym, NN)
                dst = pt * (_dot(vv, dym, NT) - del_h)
                dk = dk + _dot(dst.astype(BF16), qm, NN)
        dk_ref[rows, :] = dk if b == 0 else dk_ref[rows, :] + dk
        dv_ref[rows, :] = dv if b == 0 else dv_ref[rows, :] + dv

    _for_blocks(d, G, sub)


def _attn_post(dq, dk, dv, proj, qg, kg, ts):
    S = proj.shape[0]
    W = ATTN_DIM

    def body(dq_ref, dk_ref, dv_ref, aq_ref, ak_ref, qg_ref, kg_ref, daq_ref, dak_ref, dav_ref, gg_ref):
        i = pl.program_id(0)
        seg = _seg_matrix(W, ATTN_HD, 1.0 / ATTN_HD)
        gsums = []
        for d_ref, x_ref, g_ref, o_ref in ((dq_ref, aq_ref, qg_ref, daq_ref), (dk_ref, ak_ref, kg_ref, dak_ref)):
            dy = d_ref[...]
            xv = x_ref[...]
            r = lax.rsqrt(_seg_sum(xv * xv, seg) + EPS)
            xh = xv * r
            dxh = dy * g_ref[...]
            o_ref[...] = (r * (dxh - xh * _seg_sum(dxh * xh, seg))).astype(BF16)
            gsums.append(jnp.sum(dy * xh, axis=0, keepdims=True))
        dav_ref[...] = dv_ref[...].astype(BF16)
        _accumulate(gg_ref, _rows8(gsums, W), i)

    row = pl.BlockSpec((ts, W), lambda i: (i, 0))
    blk = lambda off: pl.BlockSpec((ts, W), lambda i: (i, off // W))
    vec = pl.BlockSpec((1, W), lambda i: (0, 0))
    return pl.pallas_call(
        body, grid=(S // ts,), in_specs=[row] * 3 + [blk(O_AQ), blk(O_AK), vec, vec],
        out_specs=[row, row, row, pl.BlockSpec((8, W), lambda i: (0, 0))],
        out_shape=[jax.ShapeDtypeStruct((S, W), BF16)] * 3 + [jax.ShapeDtypeStruct((8, W), F32)],
        compiler_params=_cp("arbitrary"), name="attn_post")(dq, dk, dv, proj, proj, qg, kg)


def _shift_down(cur, halo, n):
    return pltpu.roll(jnp.concatenate([halo, cur], axis=0), n, 0)[8:]


def _shift_up(cur, halo, n):
    ts = cur.shape[0]
    return pltpu.roll(jnp.concatenate([cur, halo], axis=0), ts + 8 - n, 0)[:ts]


def _conv(cur, halo, w, b):
    return b + w[0:1, :] * _shift_down(cur, halo, 2) + w[1:2, :] * _shift_down(cur, halo, 1) + w[2:3, :] * cur


def _mm_up_swiglu(h2, w_up, conv_w8, conv_b, tm, tc, ride=None):
    S, D = h2.shape
    F = w_up.shape[1] // 2
    nc = F // tc
    grid = (S // tm, nc)
    ride_arrays, ride_gather = ride if ride else ([], [])
    nr = len(ride_arrays)

    def body(h_ref, bg_ref, bv_ref, wg_ref, wv_ref, cg_ref, cv_ref, u0_ref, a_ref, at_ref, halo):
        i, j = pl.program_id(0), pl.program_id(1)
        hv = h_ref[...]
        acts = []
        for h, (b_ref, w_ref, c_ref) in enumerate(((bg_ref, wg_ref, cg_ref), (bv_ref, wv_ref, cv_ref))):
            u = _dot(hv, b_ref[...], NN)
            u0_ref[h] = u
            acts.append(_conv(u, jnp.where(i == 0, 0.0, halo[j, h]), w_ref[...], c_ref[...]))
            halo[j, h] = u[tm - 8:, :]
        g, v = acts
        a = g * _sigmoid(g) * v
        a_ref[...] = a.astype(BF16)
        at_ref[...] = a.T.astype(BF16)

    wcol = lambda rows, off: pl.BlockSpec((rows, tc), lambda i, j: (0, j + off))
    outs = pl.pallas_call(
        _riding(body, 7, 3, ride_gather, grid), grid=grid,
        in_specs=[pl.BlockSpec((tm, D), lambda i, j: (i, 0)), wcol(D, 0), wcol(D, nc), wcol(8, 0), wcol(8, nc), wcol(1, 0), wcol(1, nc)]
        + [HBM_SPEC] * nr,
        out_specs=[pl.BlockSpec((2, tm, tc), lambda i, j: (0, i, j)), pl.BlockSpec((tm, tc), lambda i, j: (i, j)),
                   pl.BlockSpec((tc, tm), lambda i, j: (j, i))] + [HBM_SPEC] * nr,
        out_shape=[jax.ShapeDtypeStruct((2, S, F), F32), jax.ShapeDtypeStruct((S, F), BF16), jax.ShapeDtypeStruct((F, S), BF16)]
        + _exchange_shapes(ride_arrays, ride_gather),
        scratch_shapes=[pltpu.VMEM((nc, 2, 8, tc), F32)] + (_exchange_sems(nr) if nr else []),
        compiler_params=_cp("arbitrary", "arbitrary"), name="mm_up")(
            h2, w_up, w_up, conv_w8, conv_w8, conv_b, conv_b, *ride_arrays)
    return outs[0], outs[1], outs[2], outs[3:]


def _mm_da_du(dt2, w_down, u0, conv_w8, conv_b, tm, tc, ride=None):
    _, S, F = u0.shape
    D = dt2.shape[1]
    nc = F // tc
    hb = tm // 8
    grid = (nc, S // tm)
    ride_arrays, ride_gather = ride if ride else ([], [])
    nr = len(ride_arrays)

    def body(dt_ref, wd_ref, ug_ref, ugh_ref, uv_ref, uvh_ref, wg_ref, wv_ref, bg_ref, bv_ref, du_ref, sg_ref, sv_ref):
        i = pl.program_id(1)
        first = i == 0
        halves = []
        for u_ref, h_ref, w_ref, b_ref in ((ug_ref, ugh_ref, wg_ref, bg_ref), (uv_ref, uvh_ref, wv_ref, bv_ref)):
            u, halo, w = u_ref[...], jnp.where(first, 0.0, h_ref[...]), w_ref[...]
            s2, s1 = _shift_down(u, halo, 2), _shift_down(u, halo, 1)
            halves.append((b_ref[...] + w[0:1, :] * s2 + w[1:2, :] * s1 + w[2:3, :] * u, s2, s1, u))
        g, v = halves[0][0], halves[1][0]
        dav = _dot(dt_ref[...], wd_ref[...], NT)
        sig = _sigmoid(g)
        dus = (dav * v * (sig * (1.0 + g * (1.0 - sig))), dav * (g * sig))
        for h, (du, sums_ref) in enumerate(zip(dus, (sg_ref, sv_ref))):
            du_ref[h] = du
            _, s2, s1, u = halves[h]
            _accumulate(sums_ref, _rows8([jnp.sum(du * s2, axis=0, keepdims=True), jnp.sum(du * s1, axis=0, keepdims=True),
                                          jnp.sum(du * u, axis=0, keepdims=True), jnp.sum(du, axis=0, keepdims=True)], tc), i)

    main = lambda h: pl.BlockSpec((None, tm, tc), lambda j, i: (h, i, j))
    halo = lambda h: pl.BlockSpec((None, 8, tc), lambda j, i: (h, jnp.maximum(i * hb - 1, 0), j))
    wcol = lambda rows, off: pl.BlockSpec((rows, tc), lambda j, i: (0, j + off))
    sums_spec = pl.BlockSpec((8, tc), lambda j, i: (0, j))
    outs = pl.pallas_call(
        _riding(body, 10, 3, ride_gather, grid), grid=grid,
        in_specs=[pl.BlockSpec((tm, D), lambda j, i: (i, 0)), pl.BlockSpec((tc, D), lambda j, i: (j, 0)),
                  main(0), halo(0), main(1), halo(1), wcol(8, 0), wcol(8, nc), wcol(1, 0), wcol(1, nc)] + [HBM_SPEC] * nr,
        out_specs=[pl.BlockSpec((2, tm, tc), lambda j, i: (0, i, j)), sums_spec, sums_spec] + [HBM_SPEC] * nr,
        out_shape=[jax.ShapeDtypeStruct((2, S, F), F32), jax.ShapeDtypeStruct((8, F), F32), jax.ShapeDtypeStruct((8, F), F32)]
        + _exchange_shapes(ride_arrays, ride_gather),
        scratch_shapes=_exchange_sems(nr) if nr else [],
        compiler_params=_cp("arbitrary", "arbitrary"), name="mm_da")(
            dt2, w_down, u0, u0, u0, u0, conv_w8, conv_w8, conv_b, conv_b, *ride_arrays)
    return outs[0], outs[1], outs[2], outs[3:]


def _ffn_du0(du, conv_w8, ts, tc):
    _, S, F = du.shape
    nc = F // tc
    hb = ts // 8
    nrow = S // ts

    def body(du_ref, duh_ref, w_ref, o_ref):
        last = pl.program_id(0) == nrow - 1
        cur, halo, w = du_ref[...], jnp.where(last, 0.0, duh_ref[...]), w_ref[...]
        o_ref[...] = (w[2:3, :] * cur + w[1:2, :] * _shift_up(cur, halo, 1) + w[0:1, :] * _shift_up(cur, halo, 2)).astype(BF16)

    return pl.pallas_call(
        body, grid=(nrow, 2, nc),
        in_specs=[pl.BlockSpec((None, ts, tc), lambda i, h, j: (h, i, j)),
                  pl.BlockSpec((None, 8, tc), lambda i, h, j: (h, jnp.minimum((i + 1) * hb, S // 8 - 1), j)),
                  pl.BlockSpec((8, tc), lambda i, h, j: (0, h * nc + j))],
        out_specs=pl.BlockSpec((ts, tc), lambda i, h, j: (i, h * nc + j)), out_shape=jax.ShapeDtypeStruct((S, 2 * F), BF16),
        compiler_params=_cp("parallel", "parallel", "parallel"), name="ffn_du0")(du, du, conv_w8)


def _adamw(w, g, m, v, name):
    shape = w.shape
    view = (math.prod(shape[:-1]), shape[-1])
    R, C = view
    fits = [t for t in range(8, R + 1, 8) if R % t == 0 and t * C <= SUM_BLOCK_ELEMS]
    tr = max(fits) if fits else R

    def body(w_ref, g_ref, m_ref, v_ref, d_ref, nm_ref, nv_ref):
        gv = g_ref[...]
        nm = ADAM_B1 * m_ref[...] + (1.0 - ADAM_B1) * gv
        nv = ADAM_B2 * v_ref[...] + (1.0 - ADAM_B2) * (gv * gv)
        m_hat = nm / (1.0 - ADAM_B1 ** ADAM_STEP)
        v_hat = nv / (1.0 - ADAM_B2 ** ADAM_STEP)
        d_ref[...] = -ADAM_LR * (m_hat / (jnp.sqrt(v_hat) + ADAM_EPS) + ADAM_WD * w_ref[...])
        nm_ref[...] = nm
        nv_ref[...] = nv

    spec = pl.BlockSpec((tr, C), lambda i: (i, 0))
    outs = pl.pallas_call(
        body, grid=(R // tr,), in_specs=[spec] * 4, out_specs=[spec] * 3, out_shape=[jax.ShapeDtypeStruct(view, F32)] * 3,
        compiler_params=_cp("parallel"), name=name)(*[a.reshape(view) for a in (w, g, m, v)])
    return [o.reshape(shape) for o in outs]


def _pad_rows8(a):
    return jnp.concatenate([a, jnp.zeros((8 - a.shape[0], a.shape[1]), a.dtype)], axis=0)


def _local_step(x, target, mod, n1g, w_in_p, wg_p, bg, gng, qng, kng, w_out_s, n2g, w_up_s, conv_w, conv_b, w_down_s):
    S, D = x.shape
    F = w_down_s.shape[0] * N_DEV
    ts = min(512, S)
    sh1, sc1, g1, sh2, sc2, g2 = [mod[i:i + 1] for i in range(6)]
    conv_w8 = _pad_rows8(conv_w)
    qg_t, kg_t = jnp.tile(qng, (1, ATTN_HEADS)), jnp.tile(kng, (1, ATTN_HEADS))

    h1, h1_t = _rms_mod(x, n1g, sc1, sh1, ts, "rms_mod1")
    proj, (g_out,) = _mm(h1, w_in_p, NN, 512, PROJ_W, 1024, F32, "mm_in", ride=([w_out_s], [True]))
    w_out = g_out.reshape(-1, D)
    la = _gate_fwd(proj, wg_p, bg, ts)
    o_gla, states, (g_up,) = _gla_fwd(proj, la, 512, ride=([w_up_s], [True]))
    w_up = _cols_from_blocks(g_up)
    y_gla = _gla_out(o_gla, proj, gng, ts)
    qn, kn = _head_norm(proj, qg_t, kg_t, ts)
    y_att, lse = _attn_fwd(qn, kn, proj)
    mixed, mixed_t = _attn_mix(y_gla, y_att, ts)
    t1, x2, h2, h2_t = _mm_resid_rms_mod(mixed, w_out, x, g1, n2g, sc2, sh2, ts, "mm_out")
    tc = 1408 if F % 1408 == 0 else F
    u0, a, a_t, (g_down,) = _mm_up_swiglu(h2, w_up, conv_w8, conv_b, ts, tc, ride=([w_down_s], [True]))
    w_down = g_down.reshape(F, D)
    dx3, dt2, sums3 = _mm_loss_resid(a, w_down, x2, g2, target, ts, "mm_down")
    loss_row, dg2 = sums3[0:1], sums3[1:2]

    g_w_down = _mm(a_t, dt2, NN, 1408, 1024, 2048, F32, "mm_gw_down")
    du, sums_g, sums_v, (r_down,) = _mm_da_du(dt2, w_down, u0, conv_w8, conv_b, ts, tc,
                                              ride=([g_w_down.reshape(N_DEV, -1, D)], [False]))
    g_conv_w = jnp.concatenate([sums_g[0:3], sums_v[0:3]], axis=1)
    g_conv_b = jnp.concatenate([sums_g[3:4], sums_v[3:4]], axis=1)
    du0 = _ffn_du0(du, conv_w8, min(256, S), tc)
    g_w_up = _mm(h2_t, du0, NN, 512, 2816, 2048, F32, "mm_gw_up")
    (dx2, sums2, dt1), _ = _mm_rms_mod_bwd(du0, w_up, x2, dx3, n2g, sc2, ts, "mm_dh2", t_prev=t1, g_prev=g1)
    dsh2, dsc2, g_n2g, dg1 = sums2[0:1], sums2[1:2], sums2[2:3], sums2[3:4]
    g_w_out = _mm(mixed_t, dt1, NN, 1024, 1024, 2048, F32, "mm_gw_out")
    dmixed = _mm(dt1, w_out, NT, 512, 1024, 1024, F32, "mm_dmixed")
    do_gla, dgr, gng_sums = _gla_out_bwd(dmixed, o_gla, proj, gng, ts)
    dgq, dgk, dgv, dla, (r_up, r_out) = _gla_bwd(
        proj, la, do_gla, states, 512, ride=([_col_blocks(g_w_up), g_w_out.reshape(N_DEV, -1, D)], [False, False]))
    dglr, g_wg_p, gb_sums = _gate_bwd(dla, la, proj, wg_p, ts)
    delta = _attn_delta(dmixed, y_att, ts)
    dqn = _attn_dq(qn, kn, proj, dmixed, lse, delta)
    dkn, dvn = _attn_dkv(qn, kn, proj, dmixed, lse, delta)
    daq, dak, dav, qk_sums = _attn_post(dqn, dkn, dvn, proj, qg_t, kg_t, ts)
    dproj = jnp.concatenate([dgq, dgk, dgv, dgr, daq, dak, dav, dglr, jnp.zeros((S, PROJ_W - O_GLR - LANE), BF16)], axis=1)
    g_w_in_p = _mm(h1_t, dproj, NN, 512, PROJ_W, 1024, F32, "mm_gw_in")
    g_w_in = jnp.concatenate([g_w_in_p[:, :GLR_SRC], g_w_in_p[:, O_GLR:O_GLR + GLA_RANK], g_w_in_p[:, GLR_SRC:O_GLR]], axis=1)
    (dx, sums1), (r_in,) = _mm_rms_mod_bwd(dproj, w_in_p, x, dx2, n1g, sc1, ts, "mm_dh1",
                                           ride=([_col_blocks(g_w_in).astype(BF16)], [False]))
    dsh1, dsc1, g_n1g = sums1[0:1], sums1[1:2], sums1[2:3]

    dmod = jnp.concatenate([dsh1, dsc1, dg1, dsh2, dsc2, dg2], axis=1)
    grads = dict(n1g=g_n1g, w_in=r_in, wg=g_wg_p[:GLA_RANK], bg=gb_sums[0:1], gng=gng_sums[0:1],
                 qng_lanes=qk_sums[0:1], kng_lanes=qk_sums[1:2], w_out=r_out, n2g=g_n2g, w_up=r_up,
                 conv_w=g_conv_w, conv_b=g_conv_b, w_down=r_down)
    return loss_row, dx, dmod, grads


def _col_blocks(a):
    R, W = a.shape
    return a.reshape(R, N_DEV, W // N_DEV).transpose(1, 0, 2)


def _cols_from_blocks(a):
    n, R, C = a.shape
    return a.transpose(1, 0, 2).reshape(R, n * C)


def kernel(x, c, w_ada, b_ada, norm1_g, w_in, gla_w_gate, gla_b_gate, gla_norm_g, q_norm_g, k_norm_g, w_out, norm2_g, w_up, conv_w, conv_b, w_down, loss_target, m_w_ada, m_b_ada, m_norm1_g, m_w_in, m_gla_w_gate, m_gla_b_gate, m_gla_norm_g, m_q_norm_g, m_k_norm_g, m_w_out, m_norm2_g, m_w_up, m_conv_w, m_conv_b, m_w_down, v_w_ada, v_b_ada, v_norm1_g, v_w_in, v_gla_w_gate, v_gla_b_gate, v_gla_norm_g, v_q_norm_g, v_k_norm_g, v_w_out, v_norm2_g, v_w_up, v_conv_w, v_conv_b, v_w_down):
    axes = ("x", "y", "c")
    me = 4 * lax.axis_index("x") + 2 * lax.axis_index("y") + lax.axis_index("c")
    S, D = x.shape[1], x.shape[2]
    x2d, tgt2d = x[0], loss_target[0]
    w_in_s, w_out_s, w_up_s, w_down_s, w_ada_s = w_in[0], w_out[0], w_up[0], w_down[0], w_ada[0]
    conv_w_s, wg_s = conv_w[0], gla_w_gate[0]
    in_c, up_c, ada_c, wg_c, cw_c = w_in_s.shape[1], w_up_s.shape[1], w_ada_s.shape[1], wg_s.shape[1], conv_w_s.shape[1]
    F = w_down_s.shape[0] * N_DEV

    small = jnp.concatenate([conv_w_s.reshape(1, -1), wg_s.reshape(1, -1)], axis=1)
    n_small = small.shape[1]
    small = jnp.pad(small, ((0, 0), (0, -n_small % LANE)))
    g_c, g_in, g_small = _exchange([c, w_in_s.astype(BF16), small], [True] * 3, "gather_w_in")
    c_all = g_c.reshape(N_DEV, D)
    w_in_full = _cols_from_blocks(g_in)
    w_in_p = jnp.concatenate([w_in_full[:, :GLR_SRC], w_in_full[:, GLR_SRC + GLA_RANK:],
                              w_in_full[:, GLR_SRC:GLR_SRC + GLA_RANK], jnp.zeros((D, PROJ_W - O_GLR - GLA_RANK), BF16)], axis=1)
    g_small = g_small.reshape(N_DEV, -1)
    conv_w_full = jnp.stack([g_small[:, t * cw_c:(t + 1) * cw_c].reshape(-1) for t in range(3)])
    wg_full = _cols_from_blocks(g_small[:, 3 * cw_c:n_small].reshape(N_DEV, GLA_RANK, wg_c))
    wg_p = jnp.concatenate([wg_full, jnp.zeros((LANE - GLA_RANK, wg_full.shape[1]), F32)], axis=0)

    b_shard = lax.dynamic_slice(b_ada, (0, me * ada_c), (1, ada_c))
    mod_part = _ada_fwd(c_all, w_ada_s, b_shard)
    mod_recv, = _exchange([mod_part.reshape(N_DEV, 1, ada_c)], [False], "exchange_mod")
    mod = mod_recv.reshape(6, D)

    loss_row, dx, dmod, gr = _local_step(
        x2d, tgt2d, mod, norm1_g, w_in_p, wg_p, gla_b_gate, gla_norm_g, q_norm_g, k_norm_g,
        w_out_s.astype(BF16), norm2_g, w_up_s.astype(BF16), conv_w_full, conv_b, w_down_s.astype(BF16))
    loss = lax.psum(0.5 / D * jnp.sum(loss_row), axes)

    parts = [dmod, gr["n1g"], gr["bg"], gr["gng"], gr["qng_lanes"], gr["kng_lanes"], gr["n2g"], gr["conv_b"],
             gr["wg"].reshape(1, -1), gr["conv_w"].reshape(1, -1)]
    sizes = [p.shape[1] for p in parts]
    packed = jnp.concatenate(parts, axis=1)
    packed = jnp.pad(packed, ((0, 0), (0, -packed.shape[1] % (8 * LANE))))
    gathered, = _exchange([packed.reshape(8, -1)], [True], "gather_small_grads")
    gathered = gathered.reshape(N_DEV, -1)
    total = _sum_slots(gathered.reshape(N_DEV, 8, -1), "sum_small_grads").reshape(1, -1)
    offs = [0]
    for s_ in sizes:
        offs.append(offs[-1] + s_)
    t_dmod, t_n1g, t_bg, t_gng, t_qng, t_kng, t_n2g, t_conv_b, t_wg, t_conv_w = [
        total[:, offs[i]:offs[i + 1]] for i in range(len(sizes))]
    g_b_ada = t_dmod
    g_qng = t_qng.reshape(ATTN_HEADS, ATTN_HD).sum(axis=0, keepdims=True)
    g_kng = t_kng.reshape(ATTN_HEADS, ATTN_HD).sum(axis=0, keepdims=True)
    g_wg = lax.dynamic_slice(t_wg.reshape(GLA_RANK, -1), (0, me * wg_c), (GLA_RANK, wg_c))
    g_conv_w = lax.dynamic_slice(t_conv_w.reshape(3, -1), (0, me * cw_c), (3, cw_c))
    dmod_shard = lax.dynamic_slice(gathered[:, :6 * D], (0, me * ada_c), (N_DEV, ada_c))
    g_w_ada = _ada_bwd(c_all, dmod_shard)

    g_w_in = _sum_slots(gr["w_in"], "sum_gw_in")
    g_w_out = _sum_slots(gr["w_out"], "sum_gw_out")
    g_w_up = _sum_slots(gr["w_up"], "sum_gw_up")
    g_w_down = _sum_slots(gr["w_down"], "sum_gw_down")

    names = ["w_ada", "b_ada", "norm1_g", "w_in", "gla_w_gate", "gla_b_gate", "gla_norm_g", "q_norm_g", "k_norm_g",
             "w_out", "norm2_g", "w_up", "conv_w", "conv_b", "w_down"]
    ws = [w_ada, b_ada, norm1_g, w_in, gla_w_gate, gla_b_gate, gla_norm_g, q_norm_g, k_norm_g, w_out, norm2_g, w_up, conv_w, conv_b, w_down]
    ms = [m_w_ada, m_b_ada, m_norm1_g, m_w_in, m_gla_w_gate, m_gla_b_gate, m_gla_norm_g, m_q_norm_g, m_k_norm_g, m_w_out, m_norm2_g, m_w_up, m_conv_w, m_conv_b, m_w_down]
    vs = [v_w_ada, v_b_ada, v_norm1_g, v_w_in, v_gla_w_gate, v_gla_b_gate, v_gla_norm_g, v_q_norm_g, v_k_norm_g, v_w_out, v_norm2_g, v_w_up, v_conv_w, v_conv_b, v_w_down]
    gs = [g_w_ada, g_b_ada, t_n1g, g_w_in, g_wg, t_bg, t_gng, g_qng, g_kng, g_w_out, t_n2g, g_w_up, g_conv_w, t_conv_b, g_w_down]
    gs = [g.reshape(w.shape) for g, w in zip(gs, ws)]
    deltas, new_ms, new_vs = [], [], []
    for nm, w, g, m, v in zip(names, ws, gs, ms, vs):
        d_, m_, v_ = _adamw(w, g, m, v, "adamw_" + nm)
        deltas.append(d_)
        new_ms.append(m_)
        new_vs.append(v_)
    return (loss, dx.reshape(x.shape), *gs, *deltas, *new_ms, *new_vs)
```

```python
import functools
import math

import jax
import jax.numpy as jnp
from jax import lax
from jax.experimental import pallas as pl
from jax.experimental.pallas import tpu as pltpu

F32, BF16 = jnp.float32, jnp.bfloat16
HI = lax.Precision.HIGHEST
EPS = 1e-6
NEG = -1e30

N_DEV = 8
GLA_HEADS, GLA_DK, GLA_DV, GLA_RANK, GLA_TAU, GLA_CHUNK = 4, 64, 128, 16, 16.0, 64
ATTN_HEADS, ATTN_HD, ATTN_BLOCK = 8, 64, 128
DILATIONS = (1, 4, 16)
GLA_QK, GLA_V, ATTN_DIM = GLA_HEADS * GLA_DK, GLA_HEADS * GLA_DV, ATTN_HEADS * ATTN_HD
O_GQ, O_GK, O_GV, O_GR, O_AQ, O_AK, O_AV, O_GLR = 0, 256, 512, 1024, 1536, 2048, 2560, 3072
PROJ_W = 3328
LANE = 128
GLR_SRC = 2 * GLA_QK + 2 * GLA_V

ADAM_LR, ADAM_B1, ADAM_B2, ADAM_EPS, ADAM_WD, ADAM_STEP = 0.001, 0.9, 0.999, 1e-08, 0.01, 10

VMEM_LIMIT = 56 * 1024 * 1024
SUM_BLOCK_ELEMS = 256 * 1024


def _cp(*sem):
    return pltpu.CompilerParams(dimension_semantics=sem, vmem_limit_bytes=VMEM_LIMIT)


def _dot(a, b, dims, precision=None):
    return lax.dot_general(a, b, (dims, ((), ())), preferred_element_type=F32, precision=precision)


NN, NT, TN = ((1,), (0,)), ((1,), (1,)), ((0,), (0,))


def _sigmoid(z):
    return 1.0 / (1.0 + jnp.exp(-z))


HBM_SPEC = pl.BlockSpec(memory_space=pltpu.HBM)


def _exchange_shapes(arrays, gather):
    return [jax.ShapeDtypeStruct((N_DEV,) + (a.shape if g else a.shape[1:]), a.dtype) for a, g in zip(arrays, gather)]


def _exchange_sems(n):
    return [pltpu.SemaphoreType.DMA((n * (N_DEV - 1),)), pltpu.SemaphoreType.DMA((n * (N_DEV - 1),)), pltpu.SemaphoreType.DMA((n,))]


def _exchange_copies(ins, outs, gather, send_sems, recv_sems, local_sems):
    x, y, c = lax.axis_index("x"), lax.axis_index("y"), lax.axis_index("c")
    me = 4 * x + 2 * y + c
    copies = []
    for a in range(len(ins)):
        for p in range(1, N_DEV):
            px, py, pc = x ^ (p >> 2), y ^ ((p >> 1) & 1), c ^ (p & 1)
            peer = 4 * px + 2 * py + pc
            k = a * (N_DEV - 1) + p - 1
            copies.append(pltpu.make_async_remote_copy(
                src_ref=ins[a] if gather[a] else ins[a].at[peer], dst_ref=outs[a].at[me],
                send_sem=send_sems.at[k], recv_sem=recv_sems.at[k],
                device_id=(px, py, pc), device_id_type=pl.DeviceIdType.MESH))
        copies.append(pltpu.make_async_copy(ins[a] if gather[a] else ins[a].at[me], outs[a].at[me], local_sems.at[a]))
    return copies


def _riding(body, n_in, n_out, gather, grid):
    nr = len(gather)
    if not nr:
        return body

    def wrapped(*refs):
        ins, r_ins = refs[:n_in], refs[n_in:n_in + nr]
        outs, r_outs = refs[n_in + nr:n_in + nr + n_out], refs[n_in + nr + n_out:n_in + 2 * nr + n_out]
        scratch = refs[n_in + 2 * nr + n_out:]
        first = last = None
        for t, steps in enumerate(grid):
            pid = pl.program_id(t)
            first = (pid == 0) if first is None else first & (pid == 0)
            last = (pid == steps - 1) if last is None else last & (pid == steps - 1)
        copies = _exchange_copies(r_ins, r_outs, gather, *scratch[-3:])

        @pl.when(first)
        def _():
            for cp in copies:
                cp.start()

        body(*ins, *outs, *scratch[:-3])

        @pl.when(last)
        def _():
            for cp in copies:
                cp.wait()

    return wrapped


def _exchange(arrays, gather, name):
    n = len(arrays)

    def body(*refs):
        copies = _exchange_copies(refs[:n], refs[n:2 * n], gather, *refs[2 * n:])
        for cp in copies:
            cp.start()
        for cp in copies:
            cp.wait()

    return pl.pallas_call(
        body, out_shape=_exchange_shapes(arrays, gather), in_specs=[HBM_SPEC] * n, out_specs=[HBM_SPEC] * n,
        scratch_shapes=_exchange_sems(n), name=name)(*arrays)


def _sum_slots(x, name):
    _, R, C = x.shape
    tr = max(t for t in range(8, min(SUM_BLOCK_ELEMS // C, R) + 1, 8) if R % t == 0)

    def body(x_ref, o_ref):
        acc = x_ref[0].astype(F32)
        for s in range(1, N_DEV):
            acc = acc + x_ref[s].astype(F32)
        o_ref[...] = acc

    return pl.pallas_call(
        body, grid=(R // tr,), in_specs=[pl.BlockSpec((N_DEV, tr, C), lambda i: (0, i, 0))],
        out_specs=pl.BlockSpec((tr, C), lambda i: (i, 0)), out_shape=jax.ShapeDtypeStruct((R, C), F32),
        compiler_params=_cp("parallel"), name=name)(x)


def _mm(a, b, mode, tm, tn, tk, out_dtype, name, ride=None):
    if mode == NN:
        (M, K), N = a.shape, b.shape[1]
    elif mode == NT:
        (M, K), N = a.shape, b.shape[0]
    else:
        (K, M), N = a.shape, b.shape[1]
    tm, tn, tk = min(tm, M), min(tn, N), min(tk, K)
    assert M % tm == 0 and N % tn == 0 and K % tk == 0, (name, M, N, K, tm, tn, tk)
    nk = K // tk
    if mode == NN:
        a_spec = pl.BlockSpec((tm, tk), lambda i, j, k: (i, k))
        b_spec = pl.BlockSpec((tk, tn), lambda i, j, k: (k, j))
    elif mode == NT:
        a_spec = pl.BlockSpec((tm, tk), lambda i, j, k: (i, k))
        b_spec = pl.BlockSpec((tn, tk), lambda i, j, k: (j, k))
    else:
        a_spec = pl.BlockSpec((tk, tm), lambda i, j, k: (k, i))
        b_spec = pl.BlockSpec((tk, tn), lambda i, j, k: (k, j))

    ride_arrays, ride_gather = ride if ride else ([], [])
    nr = len(ride_arrays)
    grid = (M // tm, N // tn, nk)

    own_acc = nk > 1 and out_dtype != F32

    def body(a_ref, b_ref, o_ref, *acc):
        p = _dot(a_ref[...].astype(BF16), b_ref[...].astype(BF16), mode)
        if nk == 1:
            o_ref[...] = p.astype(out_dtype)
        else:
            acc_ref = acc[0] if own_acc else o_ref
            k = pl.program_id(2)

            @pl.when(k == 0)
            def _():
                acc_ref[...] = p

            @pl.when(k > 0)
            def _():
                acc_ref[...] += p

            if own_acc:
                @pl.when(k == nk - 1)
                def _():
                    o_ref[...] = acc_ref[...].astype(out_dtype)

    outs = pl.pallas_call(
        _riding(body, 2, 1, ride_gather, grid), grid=grid, in_specs=[a_spec, b_spec] + [HBM_SPEC] * nr,
        out_specs=[pl.BlockSpec((tm, tn), lambda i, j, k: (i, j))] + [HBM_SPEC] * nr,
        out_shape=[jax.ShapeDtypeStruct((M, N), out_dtype)] + _exchange_shapes(ride_arrays, ride_gather),
        scratch_shapes=([pltpu.VMEM((tm, tn), F32)] if own_acc else []) + (_exchange_sems(nr) if nr else []),
        compiler_params=_cp(*(("arbitrary",) * 3 if nr else ("parallel", "parallel", "arbitrary"))), name=name)(a, b, *ride_arrays)
    return (outs[0], outs[1:]) if nr else outs[0]


def _ada_fwd(c_all, w_shard, b_shard):
    Nc = w_shard.shape[1]

    def body(c_ref, w_ref, b_ref, o_ref):
        cv = c_ref[...]
        o_ref[...] = _dot(cv * _sigmoid(cv), w_ref[...], NN, HI) + b_ref[...]

    return pl.pallas_call(body, out_shape=jax.ShapeDtypeStruct((N_DEV, Nc), F32), name="ada_fwd",
                          compiler_params=pltpu.CompilerParams(vmem_limit_bytes=VMEM_LIMIT))(c_all, w_shard, b_shard)


def _ada_bwd(c_all, dmod_shard):
    D, Nc = c_all.shape[1], dmod_shard.shape[1]

    def body(c_ref, d_ref, o_ref):
        cv = c_ref[...]
        o_ref[...] = _dot(cv * _sigmoid(cv), d_ref[...], TN, HI)

    return pl.pallas_call(body, out_shape=jax.ShapeDtypeStruct((D, Nc), F32), name="ada_bwd",
                          compiler_params=pltpu.CompilerParams(vmem_limit_bytes=VMEM_LIMIT))(c_all, dmod_shard)


def _row_spec(ts, D):
    return pl.BlockSpec((ts, D), lambda i: (i, 0))


def _vec_spec(D):
    return pl.BlockSpec((1, D), lambda i: (0, 0))


def _col_spec(D, ts):
    return pl.BlockSpec((D, ts), lambda i: (0, i))


def _rms_mod(x, ng, sc, sh, ts, name):
    S, D = x.shape

    def body(x_ref, ng_ref, sc_ref, sh_ref, h_ref, ht_ref):
        xv = x_ref[...]
        r = lax.rsqrt(jnp.mean(xv * xv, axis=-1, keepdims=True) + EPS)
        h = xv * r * ng_ref[...] * (1.0 + sc_ref[...]) + sh_ref[...]
        h_ref[...] = h.astype(BF16)
        ht_ref[...] = h.T.astype(BF16)

    return pl.pallas_call(
        body, grid=(S // ts,), in_specs=[_row_spec(ts, D)] + [_vec_spec(D)] * 3, out_specs=[_row_spec(ts, D), _col_spec(D, ts)],
        out_shape=[jax.ShapeDtypeStruct((S, D), BF16), jax.ShapeDtypeStruct((D, S), BF16)],
        compiler_params=_cp("parallel"), name=name)(x, ng, sc, sh)


def _mm_rows(a, b, mode, tm, extras, extra_specs, out_shapes, out_specs, epilogue, name, ride=None):
    M, K = a.shape
    grid = (M // tm,)
    ride_arrays, ride_gather = ride if ride else ([], [])
    nr = len(ride_arrays)

    def body(a_ref, b_ref, *refs):
        epilogue(_dot(a_ref[...].astype(BF16), b_ref[...].astype(BF16), mode), pl.program_id(0), *refs)

    outs = pl.pallas_call(
        _riding(body, 2 + len(extras), len(out_shapes), ride_gather, grid), grid=grid,
        in_specs=[pl.BlockSpec((tm, K), lambda i: (i, 0)), pl.BlockSpec(b.shape, lambda i: (0, 0), pipeline_mode=pl.Buffered(1))]
        + list(extra_specs) + [HBM_SPEC] * nr,
        out_specs=list(out_specs) + [HBM_SPEC] * nr,
        out_shape=list(out_shapes) + _exchange_shapes(ride_arrays, ride_gather),
        scratch_shapes=_exchange_sems(nr) if nr else [],
        compiler_params=_cp("arbitrary"), name=name)(a, b, *extras, *ride_arrays)
    return outs[:len(out_shapes)], outs[len(out_shapes):]


def _accumulate(ref, part, step):
    @pl.when(step == 0)
    def _():
        ref[...] = part

    @pl.when(step > 0)
    def _():
        ref[...] += part


def _rows8(rows, width):
    return jnp.concatenate(rows + [jnp.zeros((8 - len(rows), width), F32)], axis=0)


def _mm_resid_rms_mod(a, w, x, g, ng, sc, sh, tm, name):
    S, D = x.shape

    def epilogue(t, step, x_ref, g_ref, ng_ref, sc_ref, sh_ref, t_ref, x2_ref, h_ref, ht_ref):
        t_ref[...] = t
        xv = x_ref[...] + g_ref[...] * t
        x2_ref[...] = xv
        r = lax.rsqrt(jnp.mean(xv * xv, axis=-1, keepdims=True) + EPS)
        h = xv * r * ng_ref[...] * (1.0 + sc_ref[...]) + sh_ref[...]
        h_ref[...] = h.astype(BF16)
        ht_ref[...] = h.T.astype(BF16)

    row, vec = _row_spec(tm, D), _vec_spec(D)
    full, half = jax.ShapeDtypeStruct((S, D), F32), jax.ShapeDtypeStruct((S, D), BF16)
    outs, _ = _mm_rows(a, w, NN, tm, [x, g, ng, sc, sh], [row] + [vec] * 4,
                       [full, full, half, jax.ShapeDtypeStruct((D, S), BF16)], [row, row, row, _col_spec(D, tm)], epilogue, name)
    return outs


def _mm_loss_resid(a, w, x2, g2, target, tm, name):
    S, D = x2.shape

    def epilogue(t, step, x_ref, y_ref, g_ref, dx_ref, dt_ref, sums_ref):
        gv = g_ref[...]
        e = x_ref[...] + gv * t - y_ref[...]
        dx = e * (1.0 / D)
        dx_ref[...] = dx
        dt_ref[...] = (dx * gv).astype(BF16)
        _accumulate(sums_ref, _rows8([jnp.sum(e * e, axis=0, keepdims=True), jnp.sum(dx * t, axis=0, keepdims=True)], D), step)

    row, vec = _row_spec(tm, D), _vec_spec(D)
    outs, _ = _mm_rows(a, w, NN, tm, [x2, target, g2], [row, row, vec],
                       [jax.ShapeDtypeStruct((S, D), F32), jax.ShapeDtypeStruct((S, D), BF16), jax.ShapeDtypeStruct((8, D), F32)],
                       [row, row, pl.BlockSpec((8, D), lambda i: (0, 0))], epilogue, name)
    return outs


def _mm_rms_mod_bwd(a, w, xin, dres, ng, sc, tm, name, t_prev=None, g_prev=None, ride=None):
    S, D = xin.shape
    chain = t_prev is not None

    def epilogue(dhv, step, *refs):
        if chain:
            x_ref, dr_ref, ng_ref, sc_ref, t_ref, g_ref, dx_ref, sums_ref, dt_ref = refs
        else:
            x_ref, dr_ref, ng_ref, sc_ref, dx_ref, sums_ref = refs
        xv = x_ref[...]
        r = lax.rsqrt(jnp.mean(xv * xv, axis=-1, keepdims=True) + EPS)
        xh = xv * r
        ngv, scv = ng_ref[...], sc_ref[...]
        dxh = dhv * (ngv * (1.0 + scv))
        dx = dr_ref[...] + r * (dxh - xh * jnp.mean(dxh * xh, axis=-1, keepdims=True))
        dx_ref[...] = dx
        dhx = dhv * xh
        rows = [jnp.sum(dhv, axis=0, keepdims=True), jnp.sum(dhx * ngv, axis=0, keepdims=True),
                jnp.sum(dhx * (1.0 + scv), axis=0, keepdims=True)]
        if chain:
            dt_ref[...] = (dx * g_ref[...]).astype(BF16)
            rows.append(jnp.sum(dx * t_ref[...], axis=0, keepdims=True))
        _accumulate(sums_ref, _rows8(rows, D), step)

    row, vec = _row_spec(tm, D), _vec_spec(D)
    extras = [xin, dres, ng, sc] + ([t_prev, g_prev] if chain else [])
    extra_specs = [row, row, vec, vec] + ([row, vec] if chain else [])
    out_shapes = [jax.ShapeDtypeStruct((S, D), F32), jax.ShapeDtypeStruct((8, D), F32)] + (
        [jax.ShapeDtypeStruct((S, D), BF16)] if chain else [])
    out_specs = [row, pl.BlockSpec((8, D), lambda i: (0, 0))] + ([row] if chain else [])
    return _mm_rows(a, w, NT, tm, extras, extra_specs, out_shapes, out_specs, epilogue, name, ride=ride)


def _gate_fwd(proj, wg_p, bg, ts):
    S = proj.shape[0]

    def body(glr_ref, w_ref, b_ref, la_ref):
        z = _dot(glr_ref[...], w_ref[...], NN, HI) + b_ref[...]
        la_ref[...] = (jnp.minimum(z, 0.0) - jnp.log(1.0 + jnp.exp(-jnp.abs(z)))) * (1.0 / GLA_TAU)

    return pl.pallas_call(
        body, grid=(S // ts,),
        in_specs=[pl.BlockSpec((ts, LANE), lambda i: (i, O_GLR // LANE)), pl.BlockSpec((LANE, GLA_QK), lambda i: (0, 0)),
                  pl.BlockSpec((1, GLA_QK), lambda i: (0, 0))],
        out_specs=pl.BlockSpec((ts, GLA_QK), lambda i: (i, 0)), out_shape=jax.ShapeDtypeStruct((S, GLA_QK), F32),
        compiler_params=_cp("parallel"), name="gla_gate_fwd")(proj, wg_p, bg)


def _gate_bwd(dla, la, proj, wg_p, ts):
    S = proj.shape[0]

    def body(dla_ref, la_ref, glr_ref, w_ref, dglr_ref, gw_ref, gb_ref):
        i = pl.program_id(0)
        dz = dla_ref[...] * (1.0 / GLA_TAU) * (1.0 - jnp.exp(GLA_TAU * la_ref[...]))
        dglr_ref[...] = _dot(dz, w_ref[...], NT, HI).astype(BF16)
        gw = _dot(glr_ref[...], dz, TN, HI)
        gb = jnp.concatenate([jnp.sum(dz, axis=0, keepdims=True), jnp.zeros((7, GLA_QK), F32)], axis=0)

        @pl.when(i == 0)
        def _():
            gw_ref[...] = gw
            gb_ref[...] = gb

        @pl.when(i > 0)
        def _():
            gw_ref[...] += gw
            gb_ref[...] += gb

    return pl.pallas_call(
        body, grid=(S // ts,),
        in_specs=[pl.BlockSpec((ts, GLA_QK), lambda i: (i, 0)), pl.BlockSpec((ts, GLA_QK), lambda i: (i, 0)),
                  pl.BlockSpec((ts, LANE), lambda i: (i, O_GLR // LANE)), pl.BlockSpec((LANE, GLA_QK), lambda i: (0, 0))],
        out_specs=[pl.BlockSpec((ts, LANE), lambda i: (i, 0)), pl.BlockSpec((LANE, GLA_QK), lambda i: (0, 0)),
                   pl.BlockSpec((8, GLA_QK), lambda i: (0, 0))],
        out_shape=[jax.ShapeDtypeStruct((S, LANE), BF16), jax.ShapeDtypeStruct((LANE, GLA_QK), F32),
                   jax.ShapeDtypeStruct((8, GLA_QK), F32)],
        compiler_params=_cp("arbitrary"), name="gla_gate_bwd")(dla, la, proj, wg_p)


def _tri(lower):
    r = lax.broadcasted_iota(jnp.int32, (GLA_CHUNK, GLA_CHUNK), 0)
    c = lax.broadcasted_iota(jnp.int32, (GLA_CHUNK, GLA_CHUNK), 1)
    return jnp.where((r >= c) if lower else (c >= r), 1.0, 0.0).astype(F32)


GLA_SUB = 16
GLA_NSUB = GLA_CHUNK // GLA_SUB
PAIR_QK = 2 * GLA_DK
PAIR_V = 2 * GLA_DV


def _band_selector():
    r = lax.broadcasted_iota(jnp.int32, (GLA_SUB * PAIR_QK, LANE), 0)
    c = lax.broadcasted_iota(jnp.int32, (GLA_SUB * PAIR_QK, LANE), 1)
    dist, head = r // PAIR_QK, (r % PAIR_QK) // GLA_DK
    return jnp.where(c == head * GLA_DK + (GLA_SUB - 1 - dist), 1.0, 0.0).astype(BF16)


def _flip_matrix():
    r = lax.broadcasted_iota(jnp.int32, (GLA_CHUNK, GLA_CHUNK), 0)
    c = lax.broadcasted_iota(jnp.int32, (GLA_CHUNK, GLA_CHUNK), 1)
    return jnp.where(r + c == GLA_CHUNK - 1, 1.0, 0.0).astype(BF16)


def _state_mask():
    r = lax.broadcasted_iota(jnp.int32, (PAIR_V, PAIR_QK), 0)
    c = lax.broadcasted_iota(jnp.int32, (PAIR_V, PAIR_QK), 1)
    return (r < GLA_DV) == (c < GLA_DK)


class _GlaChunk:
    def __init__(self, qs, kc, vc, g, sel):
        C = GLA_CHUNK
        self.qs, self.kc, self.vc = qs, kc, vc
        rows = lax.broadcasted_iota(jnp.int32, (C, 1), 0)
        lane = lax.broadcasted_iota(jnp.int32, (1, PAIR_QK), 1)
        self.rows, self.lane = rows, lane
        b = _dot(_tri(True), g, NN, HI)
        self.bl = b[C - 1:C, :]
        self.eb = jnp.exp(b)
        self.kdec = jnp.exp(self.bl - b)
        edge = lambda J: b[GLA_SUB * (J + 1):GLA_SUB * (J + 1) + 1, :]
        self.e_far = [jnp.exp(jnp.where(rows >= GLA_SUB * (J + 1), b - edge(J), NEG)) for J in range(GLA_NSUB - 1)]
        blk = rows // GLA_SUB
        bnext = edge(0)
        for J in range(1, GLA_NSUB - 1):
            bnext = jnp.where(blk == J, edge(J), bnext)
        self.e_khat = jnp.exp(jnp.where(blk < GLA_NSUB - 1, bnext - b, NEG))
        khat = kc * self.e_khat
        k2 = jnp.concatenate([jnp.where(lane < GLA_DK, khat, 0.0), jnp.where(lane >= GLA_DK, khat, 0.0)], axis=0)
        self.blk2 = jnp.concatenate([blk, blk], axis=0)
        self.m_far = jnp.concatenate([jnp.where(self.blk2 == J, k2, 0.0) for J in range(GLA_NSUB - 1)], axis=1).astype(BF16)
        self.qcat = jnp.concatenate([qs * e for e in self.e_far], axis=1).astype(BF16)
        a_far = _dot(self.qcat, self.m_far, NT)
        self.e_band, self.rk, terms = [], [], []
        for d in range(GLA_SUB):
            rk = pltpu.roll(kc, d, 0) if d else kc
            rb = pltpu.roll(b, d, 0) if d else b
            e = jnp.exp(jnp.where(rows >= d, b - rb, NEG))
            self.e_band.append(e)
            self.rk.append(rk)
            terms.append((qs * rk * e).astype(BF16))
        band = _dot(jnp.concatenate(terms, axis=1), sel, NN)
        a_band = pltpu.roll(band, LANE - (GLA_SUB - 1), 1, stride=1, stride_axis=0)
        dist = rows - lane % GLA_DK
        self.far_mask = dist >= GLA_SUB
        self.band_mask = (dist >= 0) & (dist < GLA_SUB)
        self.a = (a_band + jnp.where(self.far_mask, a_far, 0.0)).astype(BF16)
        self.lane_v = lax.broadcasted_iota(jnp.int32, (1, PAIR_V), 1)
        self.v2 = jnp.concatenate([jnp.where(self.lane_v < GLA_DV, vc, 0.0), jnp.where(self.lane_v >= GLA_DV, vc, 0.0)],
                                  axis=0).astype(BF16)


def _gla_fwd(proj, la, tb, ride=None):
    S = proj.shape[0]
    C = GLA_CHUNK
    tb = min(tb, S)
    nbc = tb // C
    npair = GLA_HEADS // 2
    scale = GLA_DK ** -0.5

    def body(q_ref, k_ref, v_ref, la_ref, sel_ref, o_ref, st_ref, state):
        @pl.when(pl.program_id(1) == 0)
        def _():
            state[...] = jnp.zeros_like(state)

        def chunk(ci, carry):
            sl = pl.ds(pl.multiple_of(ci * C, C), C)
            ch = _GlaChunk(q_ref[sl, :] * scale, k_ref[sl, :], v_ref[sl, :], la_ref[sl, :], sel_ref[...])
            st = state[...]
            st_ref[0, ci] = st
            o_ref[sl, :] = _dot((ch.qs * ch.eb).astype(BF16), st.astype(BF16), NT) + _dot(ch.a, ch.v2, NN)
            upd = _dot(ch.vc.astype(BF16), (ch.kc * ch.kdec).astype(BF16), TN)
            state[...] = st * jnp.exp(ch.bl) + jnp.where(_state_mask(), upd, 0.0)
            return carry

        lax.fori_loop(0, nbc, chunk, 0, unroll=8)

    qspec = lambda off: pl.BlockSpec((tb, PAIR_QK), lambda p, i: (i, off // PAIR_QK + p))
    ride_arrays, ride_gather = ride if ride else ([], [])
    nr = len(ride_arrays)
    grid = (npair, S // tb)
    outs = pl.pallas_call(
        _riding(body, 5, 2, ride_gather, grid), grid=grid,
        in_specs=[qspec(O_GQ), qspec(O_GK), pl.BlockSpec((tb, PAIR_V), lambda p, i: (i, O_GV // PAIR_V + p)),
                  pl.BlockSpec((tb, PAIR_QK), lambda p, i: (i, p)),
                  pl.BlockSpec((GLA_SUB * PAIR_QK, LANE), lambda p, i: (0, 0))] + [HBM_SPEC] * nr,
        out_specs=[pl.BlockSpec((tb, PAIR_V), lambda p, i: (i, p)),
                   pl.BlockSpec((1, nbc, PAIR_V, PAIR_QK), lambda p, i: (p, i, 0, 0))] + [HBM_SPEC] * nr,
        out_shape=[jax.ShapeDtypeStruct((S, GLA_V), F32), jax.ShapeDtypeStruct((npair, S // C, PAIR_V, PAIR_QK), F32)]
        + _exchange_shapes(ride_arrays, ride_gather),
        scratch_shapes=[pltpu.VMEM((PAIR_V, PAIR_QK), F32)] + (_exchange_sems(nr) if nr else []),
        compiler_params=_cp("arbitrary", "arbitrary"), name="gla_fwd")(proj, proj, proj, la, _band_selector(), *ride_arrays)
    return outs[0], outs[1], outs[2:]


def _gla_bwd(proj, la, do, states, tb, ride=None):
    S = proj.shape[0]
    C = GLA_CHUNK
    tb = min(tb, S)
    nbc = tb // C
    nblk = S // tb
    npair = GLA_HEADS // 2
    scale = GLA_DK ** -0.5

    def body(q_ref, k_ref, v_ref, la_ref, do_ref, st_ref, sel_ref, selt_ref, dq_ref, dk_ref, dv_ref, dla_ref, dstate):
        @pl.when(pl.program_id(1) == 0)
        def _():
            dstate[...] = jnp.zeros_like(dstate)

        def chunk(cc, carry):
            ci = nbc - 1 - cc
            sl = pl.ds(pl.multiple_of(ci * C, C), C)
            ch = _GlaChunk(q_ref[sl, :] * scale, k_ref[sl, :], v_ref[sl, :], la_ref[sl, :], sel_ref[...])
            qs, kc, rows = ch.qs, ch.kc, ch.rows
            doc_b = do_ref[sl, :].astype(BF16)
            st = st_ref[0, ci]
            dst = dstate[...]
            dst_b = dst.astype(BF16)
            ebl = jnp.exp(ch.bl)
            dq = _dot(doc_b, st.astype(BF16), NN) * ch.eb
            dk = _dot(ch.vc.astype(BF16), dst_b, NN) * ch.kdec
            dv = _dot((kc * ch.kdec).astype(BF16), dst_b, NT)
            dbl = jnp.sum(dst * st, axis=0, keepdims=True) * ebl + jnp.sum(kc * dk, axis=0, keepdims=True)
            da = _dot(doc_b, ch.v2, NT)
            dv2 = _dot(ch.a, doc_b, TN)
            dv = dv + jnp.where(ch.lane_v < GLA_DV, dv2[:C], dv2[C:])
            da_far = jnp.where(ch.far_mask, da, 0.0).astype(BF16)
            dqcat = _dot(da_far, ch.m_far, NN)
            dm = _dot(da_far, ch.qcat, TN)
            dk2 = jnp.zeros((2 * C, PAIR_QK), F32)
            for J in range(GLA_NSUB - 1):
                dq = dq + dqcat[:, J * PAIR_QK:(J + 1) * PAIR_QK] * ch.e_far[J]
                dk2 = dk2 + jnp.where(ch.blk2 == J, dm[:, J * PAIR_QK:(J + 1) * PAIR_QK], 0.0)
            dk = dk + jnp.where(ch.lane < GLA_DK, dk2[:C], dk2[C:]) * ch.e_khat
            flip = _flip_matrix()
            da_band = _dot(flip, jnp.where(ch.band_mask, da, 0.0).astype(BF16), NN)
            dband = pltpu.roll(da_band, LANE - (C - GLA_SUB), 1, stride=1, stride_axis=0)
            dband = _dot(flip, dband.astype(BF16), NN)
            dterms = _dot(dband.astype(BF16), selt_ref[...], NN)
            for d in range(GLA_SUB):
                dt = dterms[:, d * PAIR_QK:(d + 1) * PAIR_QK]
                dq = dq + dt * (ch.rk[d] * ch.e_band[d])
                dkr = dt * (qs * ch.e_band[d])
                dk = dk + (pltpu.roll(dkr, C - d, 0) if d else dkr)
            db = qs * dq - kc * dk
            db = jnp.where(rows == C - 1, db + dbl, db)
            dq_ref[sl, :] = (dq * scale).astype(BF16)
            dk_ref[sl, :] = dk.astype(BF16)
            dv_ref[sl, :] = dv.astype(BF16)
            dla_ref[sl, :] = _dot(_tri(False), db, NN, HI)
            upd = _dot(doc_b, (qs * ch.eb).astype(BF16), TN)
            dstate[...] = dst * ebl + jnp.where(_state_mask(), upd, 0.0)
            return carry

        lax.fori_loop(0, nbc, chunk, 0, unroll=8)

    rev = lambda i: nblk - 1 - i
    qspec = lambda off: pl.BlockSpec((tb, PAIR_QK), lambda p, i: (rev(i), off // PAIR_QK + p))
    pair_qk = pl.BlockSpec((tb, PAIR_QK), lambda p, i: (rev(i), p))
    pair_v = pl.BlockSpec((tb, PAIR_V), lambda p, i: (rev(i), p))
    sel = _band_selector()
    ride_arrays, ride_gather = ride if ride else ([], [])
    nr = len(ride_arrays)
    grid = (npair, nblk)
    outs = pl.pallas_call(
        _riding(body, 8, 4, ride_gather, grid), grid=grid,
        in_specs=[qspec(O_GQ), qspec(O_GK), pl.BlockSpec((tb, PAIR_V), lambda p, i: (rev(i), O_GV // PAIR_V + p)),
                  pair_qk, pair_v, pl.BlockSpec((1, nbc, PAIR_V, PAIR_QK), lambda p, i: (p, rev(i), 0, 0)),
                  pl.BlockSpec((GLA_SUB * PAIR_QK, LANE), lambda p, i: (0, 0)),
                  pl.BlockSpec((LANE, GLA_SUB * PAIR_QK), lambda p, i: (0, 0))] + [HBM_SPEC] * nr,
        out_specs=[pair_qk, pair_qk, pair_v, pair_qk] + [HBM_SPEC] * nr,
        out_shape=[jax.ShapeDtypeStruct((S, GLA_QK), BF16), jax.ShapeDtypeStruct((S, GLA_QK), BF16),
                   jax.ShapeDtypeStruct((S, GLA_V), BF16), jax.ShapeDtypeStruct((S, GLA_QK), F32)]
        + _exchange_shapes(ride_arrays, ride_gather),
        scratch_shapes=[pltpu.VMEM((PAIR_V, PAIR_QK), F32)] + (_exchange_sems(nr) if nr else []),
        compiler_params=_cp("arbitrary", "arbitrary"), name="gla_bwd")(proj, proj, proj, la, do, states, sel, sel.T, *ride_arrays)
    return outs[0], outs[1], outs[2], outs[3], outs[4:]


def _gla_out(o, proj, gng, ts):
    S = o.shape[0]

    def body(o_ref, gr_ref, g_ref, y_ref):
        for h in range(GLA_HEADS):
            cols = slice(h * GLA_DV, (h + 1) * GLA_DV)
            ov, grv = o_ref[:, cols], gr_ref[:, cols]
            r = lax.rsqrt(jnp.mean(ov * ov, axis=-1, keepdims=True) + EPS)
            y_ref[:, cols] = (ov * r * g_ref[...] * (grv * _sigmoid(grv))).astype(BF16)

    return pl.pallas_call(
        body, grid=(S // ts,),
        in_specs=[pl.BlockSpec((ts, GLA_V), lambda i: (i, 0)), pl.BlockSpec((ts, GLA_V), lambda i: (i, O_GR // GLA_V)),
                  pl.BlockSpec((1, GLA_DV), lambda i: (0, 0))],
        out_specs=pl.BlockSpec((ts, GLA_V), lambda i: (i, 0)), out_shape=jax.ShapeDtypeStruct((S, GLA_V), BF16),
        compiler_params=_cp("parallel"), name="gla_out_fwd")(o, proj, gng)


def _gla_out_bwd(dmixed, o, proj, gng, ts):
    S = o.shape[0]

    def body(dy_ref, o_ref, gr_ref, g_ref, do_ref, dgr_ref, gg_ref):
        i = pl.program_id(0)
        gsum = jnp.zeros((1, GLA_DV), F32)
        for h in range(GLA_HEADS):
            cols = slice(h * GLA_DV, (h + 1) * GLA_DV)
            ov, grv, dy = o_ref[:, cols], gr_ref[:, cols], dy_ref[:, cols]
            r = lax.rsqrt(jnp.mean(ov * ov, axis=-1, keepdims=True) + EPS)
            oh = ov * r
            sg = _sigmoid(grv)
            silu = grv * sg
            don = dy * silu
            dgr_ref[:, cols] = (dy * (oh * g_ref[...]) * (sg * (1.0 + grv * (1.0 - sg)))).astype(BF16)
            gsum = gsum + jnp.sum(don * oh, axis=0, keepdims=True)
            doh = don * g_ref[...]
            do_ref[:, cols] = r * (doh - oh * jnp.mean(doh * oh, axis=-1, keepdims=True))
        part = jnp.concatenate([gsum, jnp.zeros((7, GLA_DV), F32)], axis=0)

        @pl.when(i == 0)
        def _():
            gg_ref[...] = part

        @pl.when(i > 0)
        def _():
            gg_ref[...] += part

    return pl.pallas_call(
        body, grid=(S // ts,),
        in_specs=[pl.BlockSpec((ts, GLA_V), lambda i: (i, 0)), pl.BlockSpec((ts, GLA_V), lambda i: (i, 0)),
                  pl.BlockSpec((ts, GLA_V), lambda i: (i, O_GR // GLA_V)), pl.BlockSpec((1, GLA_DV), lambda i: (0, 0))],
        out_specs=[pl.BlockSpec((ts, GLA_V), lambda i: (i, 0)), pl.BlockSpec((ts, GLA_V), lambda i: (i, 0)),
                   pl.BlockSpec((8, GLA_DV), lambda i: (0, 0))],
        out_shape=[jax.ShapeDtypeStruct((S, GLA_V), F32), jax.ShapeDtypeStruct((S, GLA_V), BF16),
                   jax.ShapeDtypeStruct((8, GLA_DV), F32)],
        compiler_params=_cp("arbitrary"), name="gla_out_bwd")(dmixed, o, proj, gng)


def _seg_matrix(width, seg, value):
    r = lax.broadcasted_iota(jnp.int32, (width, width), 0) // seg
    c = lax.broadcasted_iota(jnp.int32, (width, width), 1) // seg
    return jnp.where(r == c, value, 0.0).astype(BF16)


def _seg_sum(x, seg_matrix):
    hi = x.astype(BF16)
    lo = (x - hi.astype(F32)).astype(BF16)
    return _dot(hi, seg_matrix, NN) + _dot(lo, seg_matrix, NN)


def _head_norm(proj, qg, kg, ts):
    S = proj.shape[0]
    W = ATTN_DIM

    def body(q_ref, k_ref, qg_ref, kg_ref, qn_ref, kn_ref):
        seg = _seg_matrix(W, ATTN_HD, 1.0 / ATTN_HD)
        for x_ref, g_ref, o_ref, scale in ((q_ref, qg_ref, qn_ref, ATTN_HD ** -0.5), (k_ref, kg_ref, kn_ref, 1.0)):
            xv = x_ref[...]
            ms = _seg_sum(xv * xv, seg)
            o_ref[...] = xv * lax.rsqrt(ms + EPS) * (g_ref[...] * scale)

    blk = lambda off: pl.BlockSpec((ts, W), lambda i: (i, off // W))
    out = pl.BlockSpec((ts, W), lambda i: (i, 0))
    vec = pl.BlockSpec((1, W), lambda i: (0, 0))
    return pl.pallas_call(
        body, grid=(S // ts,), in_specs=[blk(O_AQ), blk(O_AK), vec, vec], out_specs=[out] * 2,
        out_shape=[jax.ShapeDtypeStruct((S, W), F32)] * 2, compiler_params=_cp("parallel"), name="attn_head_norm")(
            proj, proj, qg, kg)


def _slope(head):
    one = jnp.ones((1, 1), jnp.int32)
    return 1.0 / jnp.left_shift(one, one * (head + 1)).astype(F32)


ATTN_GROUP = 4


ATTN_TILE = max(DILATIONS) * ATTN_BLOCK


def _attn_rows(d, g, r, base=0):
    start = base + (g * d * ATTN_BLOCK if g >= 0 else ATTN_TILE - d * ATTN_BLOCK) + r
    return pl.ds(start, ATTN_BLOCK) if d == 1 else pl.ds(start, ATTN_BLOCK, stride=d)


def _for_blocks(d, G, fn):
    for g in range(G):
        if d <= ATTN_GROUP:
            for r in range(d):
                fn(g, r)
        else:
            def step(r, carry, g=g):
                fn(g, r)
                return carry
            lax.fori_loop(0, d, step, 0, unroll=ATTN_GROUP)


def _attn_specs(S):
    nb = S // ATTN_TILE

    def specs(off=0):
        return [pl.BlockSpec((ATTN_TILE, LANE), lambda hp, n: (n, off + hp)),
                pl.BlockSpec((ATTN_TILE, LANE), lambda hp, n: (jnp.maximum(n - 1, 0), off + hp)),
                pl.BlockSpec((ATTN_TILE, LANE), lambda hp, n: (jnp.minimum(n + 1, nb - 1), off + hp))]

    return nb, specs


def _attn_bias(d, hp, first_tile):
    B = ATTN_BLOCK
    iq = lax.broadcasted_iota(jnp.int32, (B, 2 * B), 0)
    ik = lax.broadcasted_iota(jnp.int32, (B, 2 * B), 1)
    rel = iq + B - ik
    window = (rel >= 0) & (rel <= B)
    relf = (d * rel).astype(F32)
    full = [jnp.where(window, -_slope(hp * 2 + h) * relf, NEG) for h in range(2)]
    edge = [jnp.where((ik >= B) | jnp.logical_not(first_tile), b, NEG) for b in full]
    return full, edge


def _attn_bias_t(d, hp, has_next):
    B = ATTN_BLOCK
    ik = lax.broadcasted_iota(jnp.int32, (B, B), 0)
    iq = lax.broadcasted_iota(jnp.int32, (B, B), 1)
    tiles = []
    for nxt in range(2):
        rel = iq - ik + nxt * B
        window = (rel >= 0) & (rel <= B)
        relf = (d * rel).astype(F32)
        tiles.append([jnp.where(window, -_slope(hp * 2 + h) * relf, NEG) for h in range(2)])
    tiles.append([jnp.where(has_next, b, NEG) for b in tiles[1]])
    return tiles


def _attn_fwd(qn, kn, proj):
    S, W = qn.shape
    T = ATTN_TILE
    nb, specs = _attn_specs(S)

    def body(q_ref, kp_ref, kc_ref, vp_ref, vc_ref, y_ref, l_ref, o_scr, l_scr):
        hp, n = pl.program_id(0), pl.program_id(1)
        lo = lax.broadcasted_iota(jnp.int32, (1, LANE), 1) < ATTN_HD
        for b, d in enumerate(DILATIONS):
            full, edge = _attn_bias(d, hp, n == 0)

            def sub(g, r, b=b, d=d, full=full, edge=edge):
                rows, before = _attn_rows(d, g, r), _attn_rows(d, g - 1, r)
                kb_ref, vb_ref = (kp_ref, vp_ref) if g == 0 else (kc_ref, vc_ref)
                bias = edge if g == 0 else full
                qv = q_ref[rows, :].astype(BF16)
                kv = jnp.concatenate([kb_ref[before, :], kc_ref[rows, :]], axis=0).astype(BF16)
                vv = jnp.concatenate([vb_ref[before, :], vc_ref[rows, :]], axis=0).astype(BF16)
                outs, lses = [], []
                for h in range(2):
                    qm = jnp.where(lo == (h == 0), qv, jnp.zeros_like(qv))
                    s = _dot(qm, kv, NT) + bias[h]
                    m = jnp.max(s, axis=-1, keepdims=True)
                    p = jnp.exp(s - m)
                    den = jnp.sum(p, axis=-1, keepdims=True)
                    outs.append(_dot(p.astype(BF16), vv, NN) / den)
                    lses.append(m + jnp.log(den))
                kept = _attn_rows(d, g, r, base=b * T)
                o_scr[kept, :] = jnp.where(lo, outs[0], outs[1])
                l_scr[kept, :] = jnp.where(lo, lses[0], lses[1])

            _for_blocks(d, T // (d * ATTN_BLOCK), sub)
        l1, l2, l3 = [l_scr[pl.ds(b * T, T), :] for b in range(len(DILATIONS))]
        o1, o2, o3 = [o_scr[pl.ds(b * T, T), :] for b in range(len(DILATIONS))]
        m = jnp.maximum(jnp.maximum(l1, l2), l3)
        e1, e2, e3 = jnp.exp(l1 - m), jnp.exp(l2 - m), jnp.exp(l3 - m)
        tot = e1 + e2 + e3
        y_ref[...] = (e1 * o1 + e2 * o2 + e3 * o3) / tot
        l_ref[...] = m + jnp.log(tot)

    cur, prev, _ = specs()
    vcur, vprev, _ = specs(O_AV // LANE)
    return pl.pallas_call(
        body, grid=(W // LANE, nb), in_specs=[cur, prev, cur, vprev, vcur], out_specs=[cur, cur],
        out_shape=[jax.ShapeDtypeStruct((S, W), F32)] * 2,
        scratch_shapes=[pltpu.VMEM((len(DILATIONS) * T, LANE), F32)] * 2,
        compiler_params=_cp("parallel", "arbitrary"), name="attn_fwd")(qn, kn, kn, proj, proj)


def _attn_mix(y_gla, y_att, ts):
    S, W = y_att.shape

    def body(yg, ya, mixed_ref, mixed_t_ref):
        y = ya[...]
        mixed_ref[:, :W] = yg[...]
        mixed_ref[:, W:] = y.astype(BF16)
        mixed_t_ref[:W, :] = yg[...].astype(F32).T.astype(BF16)
        mixed_t_ref[W:, :] = y.T.astype(BF16)

    spec = pl.BlockSpec((ts, W), lambda i: (i, 0))
    return pl.pallas_call(
        body, grid=(S // ts,), in_specs=[spec] * 2,
        out_specs=[pl.BlockSpec((ts, 2 * W), lambda i: (i, 0)), _col_spec(2 * W, ts)],
        out_shape=[jax.ShapeDtypeStruct((S, 2 * W), BF16), jax.ShapeDtypeStruct((2 * W, S), BF16)],
        compiler_params=_cp("parallel"), name="attn_mix")(y_gla, y_att)


def _attn_delta(dmixed, y, ts):
    S, W = y.shape

    def body(dy_ref, y_ref, d_ref):
        d_ref[...] = _seg_sum(dy_ref[...] * y_ref[...], _seg_matrix(W, ATTN_HD, 1.0))

    return pl.pallas_call(
        body, grid=(S // ts,), in_specs=[pl.BlockSpec((ts, W), lambda i: (i, 1)), pl.BlockSpec((ts, W), lambda i: (i, 0))],
        out_specs=pl.BlockSpec((ts, W), lambda i: (i, 0)), out_shape=jax.ShapeDtypeStruct((S, W), F32),
        compiler_params=_cp("parallel"), name="attn_delta")(dmixed, y)


def _attn_bwd(qn, kn, proj, dmixed, lse, delta):
    S, W = qn.shape
    nb, specs = _attn_specs(S)

    def body(k_ref, v_ref, qc_ref, qn_ref, dyc_ref, dyn_ref, lc_ref, ln_ref, dec_ref, den_ref, dq_ref, dk_ref, dv_ref, carry):
        hp, n = pl.program_id(0), pl.program_id(1)
        lo = lax.broadcasted_iota(jnp.int32, (1, LANE), 1) < ATTN_HD

        @pl.when(n == 0)
        def _():
            carry[...] = jnp.zeros_like(carry)

        dq_ref[...] = carry[...]
        carry[...] = jnp.zeros_like(carry)
        cur_refs, next_refs = (qc_ref, dyc_ref, lc_ref, dec_ref), (qn_ref, dyn_ref, ln_ref, den_ref)
        for b, d in enumerate(DILATIONS):
            _attn_bwd_branch(b, d, _attn_bias_t(d, hp, n + 1 < nb), lo, k_ref, v_ref, cur_refs, next_refs,
                             dq_ref, carry, dk_ref, dv_ref)

    cur, _, nxt = specs()
    vcur, _, _ = specs(O_AV // LANE)
    dycur, _, dynxt = specs(W // LANE)
    return pl.pallas_call(
        body, grid=(W // LANE, nb), in_specs=[cur, vcur, cur, nxt, dycur, dynxt, cur, nxt, cur, nxt], out_specs=[cur, cur, cur],
        out_shape=[jax.ShapeDtypeStruct((S, W), F32)] * 3, scratch_shapes=[pltpu.VMEM((ATTN_TILE, LANE), F32)],
        compiler_params=_cp("parallel", "arbitrary"), name="attn_bwd")(
            kn, proj, qn, qn, dmixed, dmixed, lse, lse, delta, delta)


def _attn_bwd_branch(b, d, biases, lo, k_ref, v_ref, cur_refs, next_refs, dq_ref, carry, dk_ref, dv_ref):
    B = ATTN_BLOCK
    own, inner, outer = biases
    G = ATTN_TILE // (d * B)

    def sub(g, r):
        rows = _attn_rows(d, g, r)
        kv, vv = k_ref[rows, :].astype(BF16), v_ref[rows, :].astype(BF16)
        dk = jnp.zeros((B, LANE), F32)
        dv = jnp.zeros((B, LANE), F32)
        inside = g + 1 < G
        after = _attn_rows(d, g + 1 if inside else 0, r)
        for bias, qrows, (q_ref, dy_ref, l_ref, de_ref), dq_acc in (
                (own, rows, cur_refs, dq_ref),
                (inner if inside else outer, after, cur_refs if inside else next_refs, dq_ref if inside else carry)):
            qv, dyv = q_ref[qrows, :].astype(BF16), dy_ref[qrows, :]
            lt, det = l_ref[qrows, :].T, de_ref[qrows, :].T
            dqs = []
            for h in range(2):
                sel = lo == (h == 0)
                qm = jnp.where(sel, qv, jnp.zeros_like(qv))
                dym = jnp.where(sel, dyv, 0.0).astype(BF16)
                lse_h = lt[h * ATTN_HD:h * ATTN_HD + 1, :]
                del_h = det[h * ATTN_HD:h * ATTN_HD + 1, :]
                pt = jnp.exp(_dot(kv, qm, NT) + bias[h] - lse_h)
                dv = dv + _dot(pt.astype(BF16), dym, NN)
                dst = (pt * (_dot(vv, dym, NT) - del_h)).astype(BF16)
                dk = dk + _dot(dst, qm, NN)
                dqs.append(_dot(dst, kv, TN))
            dq_acc[qrows, :] = dq_acc[qrows, :] + jnp.where(lo, dqs[0], dqs[1]) * (ATTN_HD ** -0.5)
        dk_ref[rows, :] = dk if b == 0 else dk_ref[rows, :] + dk
        dv_ref[rows, :] = dv if b == 0 else dv_ref[rows, :] + dv

    _for_blocks(d, G, sub)


def _attn_post(dq, dk, dv, proj, qg, kg, ts):
    S = proj.shape[0]
    W = ATTN_DIM

    def body(dq_ref, dk_ref, dv_ref, aq_ref, ak_ref, qg_ref, kg_ref, daq_ref, dak_ref, dav_ref, gg_ref):
        i = pl.program_id(0)
        seg = _seg_matrix(W, ATTN_HD, 1.0 / ATTN_HD)
        gsums = []
        for d_ref, x_ref, g_ref, o_ref in ((dq_ref, aq_ref, qg_ref, daq_ref), (dk_ref, ak_ref, kg_ref, dak_ref)):
            dy = d_ref[...]
            xv = x_ref[...]
            r = lax.rsqrt(_seg_sum(xv * xv, seg) + EPS)
            xh = xv * r
            dxh = dy * g_ref[...]
            o_ref[...] = (r * (dxh - xh * _seg_sum(dxh * xh, seg))).astype(BF16)
            gsums.append(jnp.sum(dy * xh, axis=0, keepdims=True))
        dav_ref[...] = dv_ref[...].astype(BF16)
        _accumulate(gg_ref, _rows8(gsums, W), i)

    row = pl.BlockSpec((ts, W), lambda i: (i, 0))
    blk = lambda off: pl.BlockSpec((ts, W), lambda i: (i, off // W))
    vec = pl.BlockSpec((1, W), lambda i: (0, 0))
    return pl.pallas_call(
        body, grid=(S // ts,), in_specs=[row] * 3 + [blk(O_AQ), blk(O_AK), vec, vec],
        out_specs=[row, row, row, pl.BlockSpec((8, W), lambda i: (0, 0))],
        out_shape=[jax.ShapeDtypeStruct((S, W), BF16)] * 3 + [jax.ShapeDtypeStruct((8, W), F32)],
        compiler_params=_cp("arbitrary"), name="attn_post")(dq, dk, dv, proj, proj, qg, kg)


def _shift_down(cur, halo, n):
    return pltpu.roll(jnp.concatenate([halo, cur], axis=0), n, 0)[8:]


def _shift_up(cur, halo, n):
    ts = cur.shape[0]
    return pltpu.roll(jnp.concatenate([cur, halo], axis=0), ts + 8 - n, 0)[:ts]


def _conv(cur, halo, w, b):
    return b + w[0:1, :] * _shift_down(cur, halo, 2) + w[1:2, :] * _shift_down(cur, halo, 1) + w[2:3, :] * cur


def _mm_up_swiglu(h2, w_up, conv_w8, conv_b, tm, tc, ride=None):
    S, D = h2.shape
    F = w_up.shape[1] // 2
    nc = F // tc
    grid = (S // tm, nc)
    ride_arrays, ride_gather = ride if ride else ([], [])
    nr = len(ride_arrays)

    def body(h_ref, bg_ref, bv_ref, wg_ref, wv_ref, cg_ref, cv_ref, u0_ref, a_ref, at_ref, halo):
        i, j = pl.program_id(0), pl.program_id(1)
        hv = h_ref[...]
        acts = []
        for h, (b_ref, w_ref, c_ref) in enumerate(((bg_ref, wg_ref, cg_ref), (bv_ref, wv_ref, cv_ref))):
            u = _dot(hv, b_ref[...], NN)
            u0_ref[h] = u
            acts.append(_conv(u, jnp.where(i == 0, 0.0, halo[j, h]), w_ref[...], c_ref[...]))
            halo[j, h] = u[tm - 8:, :]
        g, v = acts
        a = g * _sigmoid(g) * v
        a_ref[...] = a.astype(BF16)
        at_ref[...] = a.T.astype(BF16)

    wcol = lambda rows, off: pl.BlockSpec((rows, tc), lambda i, j: (0, j + off))
    outs = pl.pallas_call(
        _riding(body, 7, 3, ride_gather, grid), grid=grid,
        in_specs=[pl.BlockSpec((tm, D), lambda i, j: (i, 0)), wcol(D, 0), wcol(D, nc), wcol(8, 0), wcol(8, nc), wcol(1, 0), wcol(1, nc)]
        + [HBM_SPEC] * nr,
        out_specs=[pl.BlockSpec((2, tm, tc), lambda i, j: (0, i, j)), pl.BlockSpec((tm, tc), lambda i, j: (i, j)),
                   pl.BlockSpec((tc, tm), lambda i, j: (j, i))] + [HBM_SPEC] * nr,
        out_shape=[jax.ShapeDtypeStruct((2, S, F), F32), jax.ShapeDtypeStruct((S, F), BF16), jax.ShapeDtypeStruct((F, S), BF16)]
        + _exchange_shapes(ride_arrays, ride_gather),
        scratch_shapes=[pltpu.VMEM((nc, 2, 8, tc), F32)] + (_exchange_sems(nr) if nr else []),
        compiler_params=_cp("arbitrary", "arbitrary"), name="mm_up")(
            h2, w_up, w_up, conv_w8, conv_w8, conv_b, conv_b, *ride_arrays)
    return outs[0], outs[1], outs[2], outs[3:]


def _mm_da_du(dt2, w_down, u0, conv_w8, conv_b, tm, tc, ride=None):
    _, S, F = u0.shape
    D = dt2.shape[1]
    nc = F // tc
    hb = tm // 8
    grid = (nc, S // tm)
    ride_arrays, ride_gather = ride if ride else ([], [])
    nr = len(ride_arrays)

    def body(dt_ref, wd_ref, ug_ref, ugh_ref, uv_ref, uvh_ref, wg_ref, wv_ref, bg_ref, bv_ref, du_ref, sg_ref, sv_ref):
        i = pl.program_id(1)
        first = i == 0
        halves = []
        for u_ref, h_ref, w_ref, b_ref in ((ug_ref, ugh_ref, wg_ref, bg_ref), (uv_ref, uvh_ref, wv_ref, bv_ref)):
            u, halo, w = u_ref[...], jnp.where(first, 0.0, h_ref[...]), w_ref[...]
            s2, s1 = _shift_down(u, halo, 2), _shift_down(u, halo, 1)
            halves.append((b_ref[...] + w[0:1, :] * s2 + w[1:2, :] * s1 + w[2:3, :] * u, s2, s1, u))
        g, v = halves[0][0], halves[1][0]
        dav = _dot(dt_ref[...], wd_ref[...], NT)
        sig = _sigmoid(g)
        dus = (dav * v * (sig * (1.0 + g * (1.0 - sig))), dav * (g * sig))
        for h, (du, sums_ref) in enumerate(zip(dus, (sg_ref, sv_ref))):
            du_ref[h] = du
            _, s2, s1, u = halves[h]
            _accumulate(sums_ref, _rows8([jnp.sum(du * s2, axis=0, keepdims=True), jnp.sum(du * s1, axis=0, keepdims=True),
                                          jnp.sum(du * u, axis=0, keepdims=True), jnp.sum(du, axis=0, keepdims=True)], tc), i)

    main = lambda h: pl.BlockSpec((None, tm, tc), lambda j, i: (h, i, j))
    halo = lambda h: pl.BlockSpec((None, 8, tc), lambda j, i: (h, jnp.maximum(i * hb - 1, 0), j))
    wcol = lambda rows, off: pl.BlockSpec((rows, tc), lambda j, i: (0, j + off))
    sums_spec = pl.BlockSpec((8, tc), lambda j, i: (0, j))
    outs = pl.pallas_call(
        _riding(body, 10, 3, ride_gather, grid), grid=grid,
        in_specs=[pl.BlockSpec((tm, D), lambda j, i: (i, 0)), pl.BlockSpec((tc, D), lambda j, i: (j, 0)),
                  main(0), halo(0), main(1), halo(1), wcol(8, 0), wcol(8, nc), wcol(1, 0), wcol(1, nc)] + [HBM_SPEC] * nr,
        out_specs=[pl.BlockSpec((2, tm, tc), lambda j, i: (0, i, j)), sums_spec, sums_spec] + [HBM_SPEC] * nr,
        out_shape=[jax.ShapeDtypeStruct((2, S, F), F32), jax.ShapeDtypeStruct((8, F), F32), jax.ShapeDtypeStruct((8, F), F32)]
        + _exchange_shapes(ride_arrays, ride_gather),
        scratch_shapes=_exchange_sems(nr) if nr else [],
        compiler_params=_cp("arbitrary", "arbitrary"), name="mm_da")(
            dt2, w_down, u0, u0, u0, u0, conv_w8, conv_w8, conv_b, conv_b, *ride_arrays)
    return outs[0], outs[1], outs[2], outs[3:]


def _ffn_du0(du, conv_w8, ts, tc):
    _, S, F = du.shape
    nc = F // tc
    hb = ts // 8
    nrow = S // ts

    def body(du_ref, duh_ref, w_ref, o_ref):
        last = pl.program_id(0) == nrow - 1
        cur, halo, w = du_ref[...], jnp.where(last, 0.0, duh_ref[...]), w_ref[...]
        o_ref[...] = (w[2:3, :] * cur + w[1:2, :] * _shift_up(cur, halo, 1) + w[0:1, :] * _shift_up(cur, halo, 2)).astype(BF16)

    return pl.pallas_call(
        body, grid=(nrow, 2, nc),
        in_specs=[pl.BlockSpec((None, ts, tc), lambda i, h, j: (h, i, j)),
                  pl.BlockSpec((None, 8, tc), lambda i, h, j: (h, jnp.minimum((i + 1) * hb, S // 8 - 1), j)),
                  pl.BlockSpec((8, tc), lambda i, h, j: (0, h * nc + j))],
        out_specs=pl.BlockSpec((ts, tc), lambda i, h, j: (i, h * nc + j)), out_shape=jax.ShapeDtypeStruct((S, 2 * F), BF16),
        compiler_params=_cp("parallel", "parallel", "parallel"), name="ffn_du0")(du, du, conv_w8)


def _adamw(w, g, m, v, name):
    shape = w.shape
    view = (math.prod(shape[:-1]), shape[-1])
    R, C = view
    fits = [t for t in range(8, R + 1, 8) if R % t == 0 and t * C <= SUM_BLOCK_ELEMS]
    tr = max(fits) if fits else R

    def body(w_ref, g_ref, m_ref, v_ref, d_ref, nm_ref, nv_ref):
        gv = g_ref[...]
        nm = ADAM_B1 * m_ref[...] + (1.0 - ADAM_B1) * gv
        nv = ADAM_B2 * v_ref[...] + (1.0 - ADAM_B2) * (gv * gv)
        m_hat = nm / (1.0 - ADAM_B1 ** ADAM_STEP)
        v_hat = nv / (1.0 - ADAM_B2 ** ADAM_STEP)
        d_ref[...] = -ADAM_LR * (m_hat / (jnp.sqrt(v_hat) + ADAM_EPS) + ADAM_WD * w_ref[...])
        nm_ref[...] = nm
        nv_ref[...] = nv

    spec = pl.BlockSpec((tr, C), lambda i: (i, 0))
    outs = pl.pallas_call(
        body, grid=(R // tr,), in_specs=[spec] * 4, out_specs=[spec] * 3, out_shape=[jax.ShapeDtypeStruct(view, F32)] * 3,
        compiler_params=_cp("parallel"), name=name)(*[a.reshape(view) for a in (w, g, m, v)])
    return [o.reshape(shape) for o in outs]


def _pad_rows8(a):
    return jnp.concatenate([a, jnp.zeros((8 - a.shape[0], a.shape[1]), a.dtype)], axis=0)


def _local_step(x, target, mod, n1g, w_in_p, wg_p, bg, gng, qng, kng, w_out_s, n2g, w_up_s, conv_w, conv_b, w_down_s):
    S, D = x.shape
    F = w_down_s.shape[0] * N_DEV
    ts = min(512, S)
    sh1, sc1, g1, sh2, sc2, g2 = [mod[i:i + 1] for i in range(6)]
    conv_w8 = _pad_rows8(conv_w)
    qg_t, kg_t = jnp.tile(qng, (1, ATTN_HEADS)), jnp.tile(kng, (1, ATTN_HEADS))

    h1, h1_t = _rms_mod(x, n1g, sc1, sh1, ts, "rms_mod1")
    proj, (g_out,) = _mm(h1, w_in_p, NN, 512, PROJ_W, 1024, F32, "mm_in", ride=([w_out_s], [True]))
    w_out = g_out.reshape(-1, D)
    la = _gate_fwd(proj, wg_p, bg, ts)
    o_gla, states, (g_up,) = _gla_fwd(proj, la, 512, ride=([w_up_s], [True]))
    w_up = _cols_from_blocks(g_up)
    y_gla = _gla_out(o_gla, proj, gng, ts)
    qn, kn = _head_norm(proj, qg_t, kg_t, ts)
    y_att, lse = _attn_fwd(qn, kn, proj)
    mixed, mixed_t = _attn_mix(y_gla, y_att, ts)
    t1, x2, h2, h2_t = _mm_resid_rms_mod(mixed, w_out, x, g1, n2g, sc2, sh2, ts, "mm_out")
    tc = 1408 if F % 1408 == 0 else F
    u0, a, a_t, (g_down,) = _mm_up_swiglu(h2, w_up, conv_w8, conv_b, ts, tc, ride=([w_down_s], [True]))
    w_down = g_down.reshape(F, D)
    dx3, dt2, sums3 = _mm_loss_resid(a, w_down, x2, g2, target, ts, "mm_down")
    loss_row, dg2 = sums3[0:1], sums3[1:2]

    g_w_down = _mm(a_t, dt2, NN, 1408, 1024, 2048, F32, "mm_gw_down")
    du, sums_g, sums_v, (r_down,) = _mm_da_du(dt2, w_down, u0, conv_w8, conv_b, ts, tc,
                                              ride=([g_w_down.reshape(N_DEV, -1, D)], [False]))
    g_conv_w = jnp.concatenate([sums_g[0:3], sums_v[0:3]], axis=1)
    g_conv_b = jnp.concatenate([sums_g[3:4], sums_v[3:4]], axis=1)
    du0 = _ffn_du0(du, conv_w8, min(256, S), tc)
    g_w_up = _mm(h2_t, du0, NN, 512, 2816, 2048, F32, "mm_gw_up")
    (dx2, sums2, dt1), _ = _mm_rms_mod_bwd(du0, w_up, x2, dx3, n2g, sc2, ts, "mm_dh2", t_prev=t1, g_prev=g1)
    dsh2, dsc2, g_n2g, dg1 = sums2[0:1], sums2[1:2], sums2[2:3], sums2[3:4]
    g_w_out = _mm(mixed_t, dt1, NN, 1024, 1024, 2048, F32, "mm_gw_out")
    dmixed = _mm(dt1, w_out, NT, 512, 1024, 1024, F32, "mm_dmixed")
    do_gla, dgr, gng_sums = _gla_out_bwd(dmixed, o_gla, proj, gng, ts)
    dgq, dgk, dgv, dla, (r_up, r_out) = _gla_bwd(
        proj, la, do_gla, states, 512, ride=([_col_blocks(g_w_up), g_w_out.reshape(N_DEV, -1, D)], [False, False]))
    dglr, g_wg_p, gb_sums = _gate_bwd(dla, la, proj, wg_p, ts)
    delta = _attn_delta(dmixed, y_att, ts)
    dqn, dkn, dvn = _attn_bwd(qn, kn, proj, dmixed, lse, delta)
    daq, dak, dav, qk_sums = _attn_post(dqn, dkn, dvn, proj, qg_t, kg_t, ts)
    dproj = jnp.concatenate([dgq, dgk, dgv, dgr, daq, dak, dav, dglr, jnp.zeros((S, PROJ_W - O_GLR - LANE), BF16)], axis=1)
    g_w_in_p = _mm(h1_t, dproj, NN, 512, PROJ_W, 1024, F32, "mm_gw_in")
    g_w_in = jnp.concatenate([g_w_in_p[:, :GLR_SRC], g_w_in_p[:, O_GLR:O_GLR + GLA_RANK], g_w_in_p[:, GLR_SRC:O_GLR]], axis=1)
    (dx, sums1), (r_in,) = _mm_rms_mod_bwd(dproj, w_in_p, x, dx2, n1g, sc1, ts, "mm_dh1",
                                           ride=([_col_blocks(g_w_in).astype(BF16)], [False]))
    dsh1, dsc1, g_n1g = sums1[0:1], sums1[1:2], sums1[2:3]

    dmod = jnp.concatenate([dsh1, dsc1, dg1, dsh2, dsc2, dg2], axis=1)
    grads = dict(n1g=g_n1g, w_in=r_in, wg=g_wg_p[:GLA_RANK], bg=gb_sums[0:1], gng=gng_sums[0:1],
                 qng_lanes=qk_sums[0:1], kng_lanes=qk_sums[1:2], w_out=r_out, n2g=g_n2g, w_up=r_up,
                 conv_w=g_conv_w, conv_b=g_conv_b, w_down=r_down)
    return loss_row, dx, dmod, grads


def _col_blocks(a):
    R, W = a.shape
    return a.reshape(R, N_DEV, W // N_DEV).transpose(1, 0, 2)


def _cols_from_blocks(a):
    n, R, C = a.shape
    return a.transpose(1, 0, 2).reshape(R, n * C)


def kernel(x, c, w_ada, b_ada, norm1_g, w_in, gla_w_gate, gla_b_gate, gla_norm_g, q_norm_g, k_norm_g, w_out, norm2_g, w_up, conv_w, conv_b, w_down, loss_target, m_w_ada, m_b_ada, m_norm1_g, m_w_in, m_gla_w_gate, m_gla_b_gate, m_gla_norm_g, m_q_norm_g, m_k_norm_g, m_w_out, m_norm2_g, m_w_up, m_conv_w, m_conv_b, m_w_down, v_w_ada, v_b_ada, v_norm1_g, v_w_in, v_gla_w_gate, v_gla_b_gate, v_gla_norm_g, v_q_norm_g, v_k_norm_g, v_w_out, v_norm2_g, v_w_up, v_conv_w, v_conv_b, v_w_down):
    axes = ("x", "y", "c")
    me = 4 * lax.axis_index("x") + 2 * lax.axis_index("y") + lax.axis_index("c")
    S, D = x.shape[1], x.shape[2]
    x2d, tgt2d = x[0], loss_target[0]
    w_in_s, w_out_s, w_up_s, w_down_s, w_ada_s = w_in[0], w_out[0], w_up[0], w_down[0], w_ada[0]
    conv_w_s, wg_s = conv_w[0], gla_w_gate[0]
    in_c, up_c, ada_c, wg_c, cw_c = w_in_s.shape[1], w_up_s.shape[1], w_ada_s.shape[1], wg_s.shape[1], conv_w_s.shape[1]
    F = w_down_s.shape[0] * N_DEV

    small = jnp.concatenate([conv_w_s.reshape(1, -1), wg_s.reshape(1, -1)], axis=1)
    n_small = small.shape[1]
    small = jnp.pad(small, ((0, 0), (0, -n_small % LANE)))
    g_c, g_in, g_small = _exchange([c, w_in_s.astype(BF16), small], [True] * 3, "gather_w_in")
    c_all = g_c.reshape(N_DEV, D)
    w_in_full = _cols_from_blocks(g_in)
    w_in_p = jnp.concatenate([w_in_full[:, :GLR_SRC], w_in_full[:, GLR_SRC + GLA_RANK:],
                              w_in_full[:, GLR_SRC:GLR_SRC + GLA_RANK], jnp.zeros((D, PROJ_W - O_GLR - GLA_RANK), BF16)], axis=1)
    g_small = g_small.reshape(N_DEV, -1)
    conv_w_full = jnp.stack([g_small[:, t * cw_c:(t + 1) * cw_c].reshape(-1) for t in range(3)])
    wg_full = _cols_from_blocks(g_small[:, 3 * cw_c:n_small].reshape(N_DEV, GLA_RANK, wg_c))
    wg_p = jnp.concatenate([wg_full, jnp.zeros((LANE - GLA_RANK, wg_full.shape[1]), F32)], axis=0)

    b_shard = lax.dynamic_slice(b_ada, (0, me * ada_c), (1, ada_c))
    mod_part = _ada_fwd(c_all, w_ada_s, b_shard)
    mod_recv, = _exchange([mod_part.reshape(N_DEV, 1, ada_c)], [False], "exchange_mod")
    mod = mod_recv.reshape(6, D)

    loss_row, dx, dmod, gr = _local_step(
        x2d, tgt2d, mod, norm1_g, w_in_p, wg_p, gla_b_gate, gla_norm_g, q_norm_g, k_norm_g,
        w_out_s.astype(BF16), norm2_g, w_up_s.astype(BF16), conv_w_full, conv_b, w_down_s.astype(BF16))
    loss = lax.psum(0.5 / D * jnp.sum(loss_row), axes)

    parts = [dmod, gr["n1g"], gr["bg"], gr["gng"], gr["qng_lanes"], gr["kng_lanes"], gr["n2g"], gr["conv_b"],
             gr["wg"].reshape(1, -1), gr["conv_w"].reshape(1, -1)]
    sizes = [p.shape[1] for p in parts]
    packed = jnp.concatenate(parts, axis=1)
    packed = jnp.pad(packed, ((0, 0), (0, -packed.shape[1] % (8 * LANE))))
    gathered, = _exchange([packed.reshape(8, -1)], [True], "gather_small_grads")
    gathered = gathered.reshape(N_DEV, -1)
    total = _sum_slots(gathered.reshape(N_DEV, 8, -1), "sum_small_grads").reshape(1, -1)
    offs = [0]
    for s_ in sizes:
        offs.append(offs[-1] + s_)
    t_dmod, t_n1g, t_bg, t_gng, t_qng, t_kng, t_n2g, t_conv_b, t_wg, t_conv_w = [
        total[:, offs[i]:offs[i + 1]] for i in range(len(sizes))]
    g_b_ada = t_dmod
    g_qng = t_qng.reshape(ATTN_HEADS, ATTN_HD).sum(axis=0, keepdims=True)
    g_kng = t_kng.reshape(ATTN_HEADS, ATTN_HD).sum(axis=0, keepdims=True)
    g_wg = lax.dynamic_slice(t_wg.reshape(GLA_RANK, -1), (0, me * wg_c), (GLA_RANK, wg_c))
    g_conv_w = lax.dynamic_slice(t_conv_w.reshape(3, -1), (0, me * cw_c), (3, cw_c))
    dmod_shard = lax.dynamic_slice(gathered[:, :6 * D], (0, me * ada_c), (N_DEV, ada_c))
    g_w_ada = _ada_bwd(c_all, dmod_shard)

    g_w_in = _sum_slots(gr["w_in"], "sum_gw_in")
    g_w_out = _sum_slots(gr["w_out"], "sum_gw_out")
    g_w_up = _sum_slots(gr["w_up"], "sum_gw_up")
    g_w_down = _sum_slots(gr["w_down"], "sum_gw_down")

    names = ["w_ada", "b_ada", "norm1_g", "w_in", "gla_w_gate", "gla_b_gate", "gla_norm_g", "q_norm_g", "k_norm_g",
             "w_out", "norm2_g", "w_up", "conv_w", "conv_b", "w_down"]
    ws = [w_ada, b_ada, norm1_g, w_in, gla_w_gate, gla_b_gate, gla_norm_g, q_norm_g, k_norm_g, w_out, norm2_g, w_up, conv_w, conv_b, w_down]
    ms = [m_w_ada, m_b_ada, m_norm1_g, m_w_in, m_gla_w_gate, m_gla_b_gate, m_gla_norm_g, m_q_norm_g, m_k_norm_g, m_w_out, m_norm2_g, m_w_up, m_conv_w, m_conv_b, m_w_down]
    vs = [v_w_ada, v_b_ada, v_norm1_g, v_w_in, v_gla_w_gate, v_gla_b_gate, v_gla_norm_g, v_q_norm_g, v_k_norm_g, v_w_out, v_norm2_g, v_w_up, v_conv_w, v_conv_b, v_w_down]
    gs = [g_w_ada, g_b_ada, t_n1g, g_w_in, g_wg, t_bg, t_gng, g_qng, g_kng, g_w_out, t_n2g, g_w_up, g_conv_w, t_conv_b, g_w_down]
    gs = [g.reshape(w.shape) for g, w in zip(gs, ws)]
    deltas, new_ms, new_vs = [], [], []
    for nm, w, g, m, v in zip(names, ws, gs, ms, vs):
        d_, m_, v_ = _adamw(w, g, m, v, "adamw_" + nm)
        deltas.append(d_)
        new_ms.append(m_)
        new_vs.append(v_)
    return (loss, dx.reshape(x.shape), *gs, *deltas, *new_ms, *new_vs)
```

```python
import functools
import math

import jax
import jax.numpy as jnp
from jax import lax
from jax.experimental import pallas as pl
from jax.experimental.pallas import tpu as pltpu

F32, BF16 = jnp.float32, jnp.bfloat16
HI = lax.Precision.HIGHEST
EPS = 1e-6
NEG = -1e30

N_DEV = 8
GLA_HEADS, GLA_DK, GLA_DV, GLA_RANK, GLA_TAU, GLA_CHUNK = 4, 64, 128, 16, 16.0, 64
ATTN_HEADS, ATTN_HD, ATTN_BLOCK = 8, 64, 128
DILATIONS = (1, 4, 16)
GLA_QK, GLA_V, ATTN_DIM = GLA_HEADS * GLA_DK, GLA_HEADS * GLA_DV, ATTN_HEADS * ATTN_HD
O_GQ, O_GK, O_GV, O_GR, O_AQ, O_AK, O_AV, O_GLR = 0, 256, 512, 1024, 1536, 2048, 2560, 3072
PROJ_W = 3328
LANE = 128
GLR_SRC = 2 * GLA_QK + 2 * GLA_V

ADAM_LR, ADAM_B1, ADAM_B2, ADAM_EPS, ADAM_WD, ADAM_STEP = 0.001, 0.9, 0.999, 1e-08, 0.01, 10

VMEM_LIMIT = 56 * 1024 * 1024
SUM_BLOCK_ELEMS = 256 * 1024


def _cp(*sem):
    return pltpu.CompilerParams(dimension_semantics=sem, vmem_limit_bytes=VMEM_LIMIT)


def _dot(a, b, dims, precision=None):
    return lax.dot_general(a, b, (dims, ((), ())), preferred_element_type=F32, precision=precision)


NN, NT, TN = ((1,), (0,)), ((1,), (1,)), ((0,), (0,))


def _sigmoid(z):
    return 1.0 / (1.0 + jnp.exp(-z))


HBM_SPEC = pl.BlockSpec(memory_space=pltpu.HBM)


def _exchange_shapes(arrays, gather):
    return [jax.ShapeDtypeStruct((N_DEV,) + (a.shape if g else a.shape[1:]), a.dtype) for a, g in zip(arrays, gather)]


def _exchange_sems(n):
    return [pltpu.SemaphoreType.DMA((n * (N_DEV - 1),)), pltpu.SemaphoreType.DMA((n * (N_DEV - 1),)), pltpu.SemaphoreType.DMA((n,))]


def _exchange_copies(ins, outs, gather, send_sems, recv_sems, local_sems):
    x, y, c = lax.axis_index("x"), lax.axis_index("y"), lax.axis_index("c")
    me = 4 * x + 2 * y + c
    copies = []
    for a in range(len(ins)):
        for p in range(1, N_DEV):
            px, py, pc = x ^ (p >> 2), y ^ ((p >> 1) & 1), c ^ (p & 1)
            peer = 4 * px + 2 * py + pc
            k = a * (N_DEV - 1) + p - 1
            copies.append(pltpu.make_async_remote_copy(
                src_ref=ins[a] if gather[a] else ins[a].at[peer], dst_ref=outs[a].at[me],
                send_sem=send_sems.at[k], recv_sem=recv_sems.at[k],
                device_id=(px, py, pc), device_id_type=pl.DeviceIdType.MESH))
        copies.append(pltpu.make_async_copy(ins[a] if gather[a] else ins[a].at[me], outs[a].at[me], local_sems.at[a]))
    return copies


def _riding(body, n_in, n_out, gather, grid):
    nr = len(gather)
    if not nr:
        return body

    def wrapped(*refs):
        ins, r_ins = refs[:n_in], refs[n_in:n_in + nr]
        outs, r_outs = refs[n_in + nr:n_in + nr + n_out], refs[n_in + nr + n_out:n_in + 2 * nr + n_out]
        scratch = refs[n_in + 2 * nr + n_out:]
        first = last = None
        for t, steps in enumerate(grid):
            pid = pl.program_id(t)
            first = (pid == 0) if first is None else first & (pid == 0)
            last = (pid == steps - 1) if last is None else last & (pid == steps - 1)
        copies = _exchange_copies(r_ins, r_outs, gather, *scratch[-3:])

        @pl.when(first)
        def _():
            for cp in copies:
                cp.start()

        body(*ins, *outs, *scratch[:-3])

        @pl.when(last)
        def _():
            for cp in copies:
                cp.wait()

    return wrapped


def _exchange(arrays, gather, name):
    n = len(arrays)

    def body(*refs):
        copies = _exchange_copies(refs[:n], refs[n:2 * n], gather, *refs[2 * n:])
        for cp in copies:
            cp.start()
        for cp in copies:
            cp.wait()

    return pl.pallas_call(
        body, out_shape=_exchange_shapes(arrays, gather), in_specs=[HBM_SPEC] * n, out_specs=[HBM_SPEC] * n,
        scratch_shapes=_exchange_sems(n), name=name)(*arrays)


def _sum_slots(x, name):
    _, R, C = x.shape
    tr = max(t for t in range(8, min(SUM_BLOCK_ELEMS // C, R) + 1, 8) if R % t == 0)

    def body(x_ref, o_ref):
        acc = x_ref[0].astype(F32)
        for s in range(1, N_DEV):
            acc = acc + x_ref[s].astype(F32)
        o_ref[...] = acc

    return pl.pallas_call(
        body, grid=(R // tr,), in_specs=[pl.BlockSpec((N_DEV, tr, C), lambda i: (0, i, 0))],
        out_specs=pl.BlockSpec((tr, C), lambda i: (i, 0)), out_shape=jax.ShapeDtypeStruct((R, C), F32),
        compiler_params=_cp("parallel"), name=name)(x)


def _mm(a, b, mode, tm, tn, tk, out_dtype, name, ride=None):
    if mode == NN:
        (M, K), N = a.shape, b.shape[1]
    elif mode == NT:
        (M, K), N = a.shape, b.shape[0]
    else:
        (K, M), N = a.shape, b.shape[1]
    tm, tn, tk = min(tm, M), min(tn, N), min(tk, K)
    assert M % tm == 0 and N % tn == 0 and K % tk == 0, (name, M, N, K, tm, tn, tk)
    nk = K // tk
    if mode == NN:
        a_spec = pl.BlockSpec((tm, tk), lambda i, j, k: (i, k))
        b_spec = pl.BlockSpec((tk, tn), lambda i, j, k: (k, j))
    elif mode == NT:
        a_spec = pl.BlockSpec((tm, tk), lambda i, j, k: (i, k))
        b_spec = pl.BlockSpec((tn, tk), lambda i, j, k: (j, k))
    else:
        a_spec = pl.BlockSpec((tk, tm), lambda i, j, k: (k, i))
        b_spec = pl.BlockSpec((tk, tn), lambda i, j, k: (k, j))

    ride_arrays, ride_gather = ride if ride else ([], [])
    nr = len(ride_arrays)
    grid = (M // tm, N // tn, nk)

    own_acc = nk > 1 and out_dtype != F32

    def body(a_ref, b_ref, o_ref, *acc):
        p = _dot(a_ref[...].astype(BF16), b_ref[...].astype(BF16), mode)
        if nk == 1:
            o_ref[...] = p.astype(out_dtype)
        else:
            acc_ref = acc[0] if own_acc else o_ref
            k = pl.program_id(2)

            @pl.when(k == 0)
            def _():
                acc_ref[...] = p

            @pl.when(k > 0)
            def _():
                acc_ref[...] += p

            if own_acc:
                @pl.when(k == nk - 1)
                def _():
                    o_ref[...] = acc_ref[...].astype(out_dtype)

    outs = pl.pallas_call(
        _riding(body, 2, 1, ride_gather, grid), grid=grid, in_specs=[a_spec, b_spec] + [HBM_SPEC] * nr,
        out_specs=[pl.BlockSpec((tm, tn), lambda i, j, k: (i, j))] + [HBM_SPEC] * nr,
        out_shape=[jax.ShapeDtypeStruct((M, N), out_dtype)] + _exchange_shapes(ride_arrays, ride_gather),
        scratch_shapes=([pltpu.VMEM((tm, tn), F32)] if own_acc else []) + (_exchange_sems(nr) if nr else []),
        compiler_params=_cp(*(("arbitrary",) * 3 if nr else ("parallel", "parallel", "arbitrary"))), name=name)(a, b, *ride_arrays)
    return (outs[0], outs[1:]) if nr else outs[0]


def _ada_fwd(c_all, w_shard, b_shard):
    Nc = w_shard.shape[1]

    def body(c_ref, w_ref, b_ref, o_ref):
        cv = c_ref[...]
        o_ref[...] = _dot(cv * _sigmoid(cv), w_ref[...], NN, HI) + b_ref[...]

    return pl.pallas_call(body, out_shape=jax.ShapeDtypeStruct((N_DEV, Nc), F32), name="ada_fwd",
                          compiler_params=pltpu.CompilerParams(vmem_limit_bytes=VMEM_LIMIT))(c_all, w_shard, b_shard)


def _ada_bwd(c_all, dmod_shard):
    D, Nc = c_all.shape[1], dmod_shard.shape[1]

    def body(c_ref, d_ref, o_ref):
        cv = c_ref[...]
        o_ref[...] = _dot(cv * _sigmoid(cv), d_ref[...], TN, HI)

    return pl.pallas_call(body, out_shape=jax.ShapeDtypeStruct((D, Nc), F32), name="ada_bwd",
                          compiler_params=pltpu.CompilerParams(vmem_limit_bytes=VMEM_LIMIT))(c_all, dmod_shard)


def _row_spec(ts, D):
    return pl.BlockSpec((ts, D), lambda i: (i, 0))


def _vec_spec(D):
    return pl.BlockSpec((1, D), lambda i: (0, 0))


def _col_spec(D, ts):
    return pl.BlockSpec((D, ts), lambda i: (0, i))


def _rms_mod(x, ng, sc, sh, ts, name, ride=None):
    S, D = x.shape
    ride_arrays, ride_gather = ride if ride else ([], [])
    nr = len(ride_arrays)
    grid = (S // ts,)

    def body(x_ref, ng_ref, sc_ref, sh_ref, h_ref, ht_ref):
        xv = x_ref[...]
        r = lax.rsqrt(jnp.mean(xv * xv, axis=-1, keepdims=True) + EPS)
        h = xv * r * ng_ref[...] * (1.0 + sc_ref[...]) + sh_ref[...]
        h_ref[...] = h.astype(BF16)
        ht_ref[...] = h.T.astype(BF16)

    outs = pl.pallas_call(
        _riding(body, 4, 2, ride_gather, grid), grid=grid, in_specs=[_row_spec(ts, D)] + [_vec_spec(D)] * 3 + [HBM_SPEC] * nr,
        out_specs=[_row_spec(ts, D), _col_spec(D, ts)] + [HBM_SPEC] * nr,
        out_shape=[jax.ShapeDtypeStruct((S, D), BF16), jax.ShapeDtypeStruct((D, S), BF16)] + _exchange_shapes(ride_arrays, ride_gather),
        scratch_shapes=_exchange_sems(nr) if nr else [],
        compiler_params=_cp("arbitrary"), name=name)(x, ng, sc, sh, *ride_arrays)
    return outs[0], outs[1], outs[2:]


def _mm_rows(a, b, mode, tm, extras, extra_specs, out_shapes, out_specs, epilogue, name, ride=None):
    M, K = a.shape
    grid = (M // tm,)
    ride_arrays, ride_gather = ride if ride else ([], [])
    nr = len(ride_arrays)

    def body(a_ref, b_ref, *refs):
        epilogue(_dot(a_ref[...].astype(BF16), b_ref[...].astype(BF16), mode), pl.program_id(0), *refs)

    outs = pl.pallas_call(
        _riding(body, 2 + len(extras), len(out_shapes), ride_gather, grid), grid=grid,
        in_specs=[pl.BlockSpec((tm, K), lambda i: (i, 0)), pl.BlockSpec(b.shape, lambda i: (0, 0), pipeline_mode=pl.Buffered(1))]
        + list(extra_specs) + [HBM_SPEC] * nr,
        out_specs=list(out_specs) + [HBM_SPEC] * nr,
        out_shape=list(out_shapes) + _exchange_shapes(ride_arrays, ride_gather),
        scratch_shapes=_exchange_sems(nr) if nr else [],
        compiler_params=_cp("arbitrary"), name=name)(a, b, *extras, *ride_arrays)
    return outs[:len(out_shapes)], outs[len(out_shapes):]


def _accumulate(ref, part, step):
    @pl.when(step == 0)
    def _():
        ref[...] = part

    @pl.when(step > 0)
    def _():
        ref[...] += part


def _rows8(rows, width):
    return jnp.concatenate(rows + [jnp.zeros((8 - len(rows), width), F32)], axis=0)


def _mm_resid_rms_mod(a, w, x, g, ng, sc, sh, tm, name):
    S, D = x.shape

    def epilogue(t, step, x_ref, g_ref, ng_ref, sc_ref, sh_ref, t_ref, x2_ref, h_ref, ht_ref):
        t_ref[...] = t
        xv = x_ref[...] + g_ref[...] * t
        x2_ref[...] = xv
        r = lax.rsqrt(jnp.mean(xv * xv, axis=-1, keepdims=True) + EPS)
        h = xv * r * ng_ref[...] * (1.0 + sc_ref[...]) + sh_ref[...]
        h_ref[...] = h.astype(BF16)
        ht_ref[...] = h.T.astype(BF16)

    row, vec = _row_spec(tm, D), _vec_spec(D)
    full, half = jax.ShapeDtypeStruct((S, D), F32), jax.ShapeDtypeStruct((S, D), BF16)
    outs, _ = _mm_rows(a, w, NN, tm, [x, g, ng, sc, sh], [row] + [vec] * 4,
                       [full, full, half, jax.ShapeDtypeStruct((D, S), BF16)], [row, row, row, _col_spec(D, tm)], epilogue, name)
    return outs


def _mm_loss_resid(a, w, x2, g2, target, tm, name):
    S, D = x2.shape

    def epilogue(t, step, x_ref, y_ref, g_ref, dx_ref, dt_ref, sums_ref):
        gv = g_ref[...]
        e = x_ref[...] + gv * t - y_ref[...]
        dx = e * (1.0 / D)
        dx_ref[...] = dx
        dt_ref[...] = (dx * gv).astype(BF16)
        _accumulate(sums_ref, _rows8([jnp.sum(e * e, axis=0, keepdims=True), jnp.sum(dx * t, axis=0, keepdims=True)], D), step)

    row, vec = _row_spec(tm, D), _vec_spec(D)
    outs, _ = _mm_rows(a, w, NN, tm, [x2, target, g2], [row, row, vec],
                       [jax.ShapeDtypeStruct((S, D), F32), jax.ShapeDtypeStruct((S, D), BF16), jax.ShapeDtypeStruct((8, D), F32)],
                       [row, row, pl.BlockSpec((8, D), lambda i: (0, 0))], epilogue, name)
    return outs


def _mm_rms_mod_bwd(a, w, xin, dres, ng, sc, tm, name, t_prev=None, g_prev=None, ride=None):
    S, D = xin.shape
    chain = t_prev is not None

    def epilogue(dhv, step, *refs):
        if chain:
            x_ref, dr_ref, ng_ref, sc_ref, t_ref, g_ref, dx_ref, sums_ref, dt_ref = refs
        else:
            x_ref, dr_ref, ng_ref, sc_ref, dx_ref, sums_ref = refs
        xv = x_ref[...]
        r = lax.rsqrt(jnp.mean(xv * xv, axis=-1, keepdims=True) + EPS)
        xh = xv * r
        ngv, scv = ng_ref[...], sc_ref[...]
        dxh = dhv * (ngv * (1.0 + scv))
        dx = dr_ref[...] + r * (dxh - xh * jnp.mean(dxh * xh, axis=-1, keepdims=True))
        dx_ref[...] = dx
        dhx = dhv * xh
        rows = [jnp.sum(dhv, axis=0, keepdims=True), jnp.sum(dhx * ngv, axis=0, keepdims=True),
                jnp.sum(dhx * (1.0 + scv), axis=0, keepdims=True)]
        if chain:
            dt_ref[...] = (dx * g_ref[...]).astype(BF16)
            rows.append(jnp.sum(dx * t_ref[...], axis=0, keepdims=True))
        _accumulate(sums_ref, _rows8(rows, D), step)

    row, vec = _row_spec(tm, D), _vec_spec(D)
    extras = [xin, dres, ng, sc] + ([t_prev, g_prev] if chain else [])
    extra_specs = [row, row, vec, vec] + ([row, vec] if chain else [])
    out_shapes = [jax.ShapeDtypeStruct((S, D), F32), jax.ShapeDtypeStruct((8, D), F32)] + (
        [jax.ShapeDtypeStruct((S, D), BF16)] if chain else [])
    out_specs = [row, pl.BlockSpec((8, D), lambda i: (0, 0))] + ([row] if chain else [])
    return _mm_rows(a, w, NT, tm, extras, extra_specs, out_shapes, out_specs, epilogue, name, ride=ride)


def _gate_fwd(proj, wg_p, bg, ts):
    S = proj.shape[0]

    def body(glr_ref, w_ref, b_ref, la_ref):
        z = _dot(glr_ref[...], w_ref[...], NN, HI) + b_ref[...]
        la_ref[...] = (jnp.minimum(z, 0.0) - jnp.log(1.0 + jnp.exp(-jnp.abs(z)))) * (1.0 / GLA_TAU)

    return pl.pallas_call(
        body, grid=(S // ts,),
        in_specs=[pl.BlockSpec((ts, LANE), lambda i: (i, O_GLR // LANE)), pl.BlockSpec((LANE, GLA_QK), lambda i: (0, 0)),
                  pl.BlockSpec((1, GLA_QK), lambda i: (0, 0))],
        out_specs=pl.BlockSpec((ts, GLA_QK), lambda i: (i, 0)), out_shape=jax.ShapeDtypeStruct((S, GLA_QK), F32),
        compiler_params=_cp("parallel"), name="gla_gate_fwd")(proj, wg_p, bg)


def _gate_bwd(dla, la, proj, wg_p, ts):
    S = proj.shape[0]

    def body(dla_ref, la_ref, glr_ref, w_ref, dglr_ref, gw_ref, gb_ref):
        i = pl.program_id(0)
        dz = dla_ref[...] * (1.0 / GLA_TAU) * (1.0 - jnp.exp(GLA_TAU * la_ref[...]))
        dglr_ref[...] = _dot(dz, w_ref[...], NT, HI).astype(BF16)
        gw = _dot(glr_ref[...], dz, TN, HI)
        gb = jnp.concatenate([jnp.sum(dz, axis=0, keepdims=True), jnp.zeros((7, GLA_QK), F32)], axis=0)

        @pl.when(i == 0)
        def _():
            gw_ref[...] = gw
            gb_ref[...] = gb

        @pl.when(i > 0)
        def _():
            gw_ref[...] += gw
            gb_ref[...] += gb

    return pl.pallas_call(
        body, grid=(S // ts,),
        in_specs=[pl.BlockSpec((ts, GLA_QK), lambda i: (i, 0)), pl.BlockSpec((ts, GLA_QK), lambda i: (i, 0)),
                  pl.BlockSpec((ts, LANE), lambda i: (i, O_GLR // LANE)), pl.BlockSpec((LANE, GLA_QK), lambda i: (0, 0))],
        out_specs=[pl.BlockSpec((ts, LANE), lambda i: (i, 0)), pl.BlockSpec((LANE, GLA_QK), lambda i: (0, 0)),
                   pl.BlockSpec((8, GLA_QK), lambda i: (0, 0))],
        out_shape=[jax.ShapeDtypeStruct((S, LANE), BF16), jax.ShapeDtypeStruct((LANE, GLA_QK), F32),
                   jax.ShapeDtypeStruct((8, GLA_QK), F32)],
        compiler_params=_cp("arbitrary"), name="gla_gate_bwd")(dla, la, proj, wg_p)


def _tri(lower):
    r = lax.broadcasted_iota(jnp.int32, (GLA_CHUNK, GLA_CHUNK), 0)
    c = lax.broadcasted_iota(jnp.int32, (GLA_CHUNK, GLA_CHUNK), 1)
    return jnp.where((r >= c) if lower else (c >= r), 1.0, 0.0).astype(F32)


GLA_SUB = 16
GLA_NSUB = GLA_CHUNK // GLA_SUB
PAIR_QK = 2 * GLA_DK
PAIR_V = 2 * GLA_DV


def _band_selector():
    r = lax.broadcasted_iota(jnp.int32, (GLA_SUB * PAIR_QK, LANE), 0)
    c = lax.broadcasted_iota(jnp.int32, (GLA_SUB * PAIR_QK, LANE), 1)
    dist, head = r // PAIR_QK, (r % PAIR_QK) // GLA_DK
    return jnp.where(c == head * GLA_DK + (GLA_SUB - 1 - dist), 1.0, 0.0).astype(BF16)


def _flip_matrix():
    r = lax.broadcasted_iota(jnp.int32, (GLA_CHUNK, GLA_CHUNK), 0)
    c = lax.broadcasted_iota(jnp.int32, (GLA_CHUNK, GLA_CHUNK), 1)
    return jnp.where(r + c == GLA_CHUNK - 1, 1.0, 0.0).astype(BF16)


def _state_mask():
    r = lax.broadcasted_iota(jnp.int32, (PAIR_V, PAIR_QK), 0)
    c = lax.broadcasted_iota(jnp.int32, (PAIR_V, PAIR_QK), 1)
    return (r < GLA_DV) == (c < GLA_DK)


class _GlaChunk:
    def __init__(self, qs, kc, vc, g, sel):
        C = GLA_CHUNK
        self.qs, self.kc, self.vc = qs, kc, vc
        rows = lax.broadcasted_iota(jnp.int32, (C, 1), 0)
        lane = lax.broadcasted_iota(jnp.int32, (1, PAIR_QK), 1)
        self.rows, self.lane = rows, lane
        b = _dot(_tri(True), g, NN, HI)
        self.bl = b[C - 1:C, :]
        self.eb = jnp.exp(b)
        self.kdec = jnp.exp(self.bl - b)
        edge = lambda J: b[GLA_SUB * (J + 1):GLA_SUB * (J + 1) + 1, :]
        self.e_far = [jnp.exp(jnp.where(rows >= GLA_SUB * (J + 1), b - edge(J), NEG)) for J in range(GLA_NSUB - 1)]
        blk = rows // GLA_SUB
        bnext = edge(0)
        for J in range(1, GLA_NSUB - 1):
            bnext = jnp.where(blk == J, edge(J), bnext)
        self.e_khat = jnp.exp(jnp.where(blk < GLA_NSUB - 1, bnext - b, NEG))
        khat = kc * self.e_khat
        k2 = jnp.concatenate([jnp.where(lane < GLA_DK, khat, 0.0), jnp.where(lane >= GLA_DK, khat, 0.0)], axis=0)
        self.blk2 = jnp.concatenate([blk, blk], axis=0)
        self.m_far = jnp.concatenate([jnp.where(self.blk2 == J, k2, 0.0) for J in range(GLA_NSUB - 1)], axis=1).astype(BF16)
        self.qcat = jnp.concatenate([qs * e for e in self.e_far], axis=1).astype(BF16)
        a_far = _dot(self.qcat, self.m_far, NT)
        self.e_band, self.rk, hi_terms, lo_terms = [], [], [], []
        for d in range(GLA_SUB):
            rk = pltpu.roll(kc, d, 0) if d else kc
            rb = pltpu.roll(b, d, 0) if d else b
            e = jnp.exp(jnp.where(rows >= d, b - rb, NEG))
            self.e_band.append(e)
            self.rk.append(rk)
            t = (qs * e).astype(BF16).astype(F32) * rk.astype(BF16).astype(F32)
            hi = t.astype(BF16)
            hi_terms.append(hi)
            lo_terms.append((t - hi.astype(F32)).astype(BF16))
        band = _dot(jnp.concatenate(hi_terms, axis=1), sel, NN) + _dot(jnp.concatenate(lo_terms, axis=1), sel, NN)
        a_band = pltpu.roll(band, LANE - (GLA_SUB - 1), 1, stride=1, stride_axis=0)
        dist = rows - lane % GLA_DK
        self.far_mask = dist >= GLA_SUB
        self.band_mask = (dist >= 0) & (dist < GLA_SUB)
        self.a = (a_band + jnp.where(self.far_mask, a_far, 0.0)).astype(BF16)
        self.lane_v = lax.broadcasted_iota(jnp.int32, (1, PAIR_V), 1)
        self.v2 = jnp.concatenate([jnp.where(self.lane_v < GLA_DV, vc, 0.0), jnp.where(self.lane_v >= GLA_DV, vc, 0.0)],
                                  axis=0).astype(BF16)


def _gla_fwd(proj, la, tb, ride=None):
    S = proj.shape[0]
    C = GLA_CHUNK
    tb = min(tb, S)
    nbc = tb // C
    npair = GLA_HEADS // 2
    scale = GLA_DK ** -0.5

    def body(q_ref, k_ref, v_ref, la_ref, sel_ref, o_ref, st_ref, state):
        @pl.when(pl.program_id(1) == 0)
        def _():
            state[...] = jnp.zeros_like(state)

        def chunk(ci, carry):
            sl = pl.ds(pl.multiple_of(ci * C, C), C)
            ch = _GlaChunk(q_ref[sl, :] * scale, k_ref[sl, :], v_ref[sl, :], la_ref[sl, :], sel_ref[...])
            st = state[...]
            st_ref[0, ci] = st
            o_ref[sl, :] = _dot((ch.qs * ch.eb).astype(BF16), st.astype(BF16), NT) + _dot(ch.a, ch.v2, NN)
            upd = _dot(ch.vc.astype(BF16), (ch.kc * ch.kdec).astype(BF16), TN)
            state[...] = st * jnp.exp(ch.bl) + jnp.where(_state_mask(), upd, 0.0)
            return carry

        lax.fori_loop(0, nbc, chunk, 0, unroll=8)

    qspec = lambda off: pl.BlockSpec((tb, PAIR_QK), lambda p, i: (i, off // PAIR_QK + p))
    ride_arrays, ride_gather = ride if ride else ([], [])
    nr = len(ride_arrays)
    grid = (npair, S // tb)
    outs = pl.pallas_call(
        _riding(body, 5, 2, ride_gather, grid), grid=grid,
        in_specs=[qspec(O_GQ), qspec(O_GK), pl.BlockSpec((tb, PAIR_V), lambda p, i: (i, O_GV // PAIR_V + p)),
                  pl.BlockSpec((tb, PAIR_QK), lambda p, i: (i, p)),
                  pl.BlockSpec((GLA_SUB * PAIR_QK, LANE), lambda p, i: (0, 0))] + [HBM_SPEC] * nr,
        out_specs=[pl.BlockSpec((tb, PAIR_V), lambda p, i: (i, p)),
                   pl.BlockSpec((1, nbc, PAIR_V, PAIR_QK), lambda p, i: (p, i, 0, 0))] + [HBM_SPEC] * nr,
        out_shape=[jax.ShapeDtypeStruct((S, GLA_V), F32), jax.ShapeDtypeStruct((npair, S // C, PAIR_V, PAIR_QK), F32)]
        + _exchange_shapes(ride_arrays, ride_gather),
        scratch_shapes=[pltpu.VMEM((PAIR_V, PAIR_QK), F32)] + (_exchange_sems(nr) if nr else []),
        compiler_params=_cp("arbitrary", "arbitrary"), name="gla_fwd")(proj, proj, proj, la, _band_selector(), *ride_arrays)
    return outs[0], outs[1], outs[2:]


def _gla_bwd(proj, la, do, states, tb, ride=None):
    S = proj.shape[0]
    C = GLA_CHUNK
    tb = min(tb, S)
    nbc = tb // C
    nblk = S // tb
    npair = GLA_HEADS // 2
    scale = GLA_DK ** -0.5

    def body(q_ref, k_ref, v_ref, la_ref, do_ref, st_ref, sel_ref, selt_ref, dq_ref, dk_ref, dv_ref, dla_ref, dstate):
        @pl.when(pl.program_id(1) == 0)
        def _():
            dstate[...] = jnp.zeros_like(dstate)

        def chunk(cc, carry):
            ci = nbc - 1 - cc
            sl = pl.ds(pl.multiple_of(ci * C, C), C)
            ch = _GlaChunk(q_ref[sl, :] * scale, k_ref[sl, :], v_ref[sl, :], la_ref[sl, :], sel_ref[...])
            qs, kc, rows = ch.qs, ch.kc, ch.rows
            doc_b = do_ref[sl, :].astype(BF16)
            st = st_ref[0, ci]
            dst = dstate[...]
            dst_b = dst.astype(BF16)
            ebl = jnp.exp(ch.bl)
            dq = _dot(doc_b, st.astype(BF16), NN) * ch.eb
            dk = _dot(ch.vc.astype(BF16), dst_b, NN) * ch.kdec
            dv = _dot((kc * ch.kdec).astype(BF16), dst_b, NT)
            dbl = jnp.sum(dst * st, axis=0, keepdims=True) * ebl + jnp.sum(kc * dk, axis=0, keepdims=True)
            da = _dot(doc_b, ch.v2, NT)
            dv2 = _dot(ch.a, doc_b, TN)
            dv = dv + jnp.where(ch.lane_v < GLA_DV, dv2[:C], dv2[C:])
            da_far = jnp.where(ch.far_mask, da, 0.0).astype(BF16)
            dqcat = _dot(da_far, ch.m_far, NN)
            dm = _dot(da_far, ch.qcat, TN)
            dk2 = jnp.zeros((2 * C, PAIR_QK), F32)
            for J in range(GLA_NSUB - 1):
                dq = dq + dqcat[:, J * PAIR_QK:(J + 1) * PAIR_QK] * ch.e_far[J]
                dk2 = dk2 + jnp.where(ch.blk2 == J, dm[:, J * PAIR_QK:(J + 1) * PAIR_QK], 0.0)
            dk = dk + jnp.where(ch.lane < GLA_DK, dk2[:C], dk2[C:]) * ch.e_khat
            flip = _flip_matrix()
            da_band = _dot(flip, jnp.where(ch.band_mask, da, 0.0).astype(BF16), NN)
            dband = pltpu.roll(da_band, LANE - (C - GLA_SUB), 1, stride=1, stride_axis=0)
            dband = _dot(flip, dband.astype(BF16), NN)
            dterms = _dot(dband.astype(BF16), selt_ref[...], NN)
            for d in range(GLA_SUB):
                dt = dterms[:, d * PAIR_QK:(d + 1) * PAIR_QK]
                dq = dq + dt * (ch.rk[d] * ch.e_band[d])
                dkr = dt * (qs * ch.e_band[d])
                dk = dk + (pltpu.roll(dkr, C - d, 0) if d else dkr)
            db = qs * dq - kc * dk
            db = jnp.where(rows == C - 1, db + dbl, db)
            dq_ref[sl, :] = (dq * scale).astype(BF16)
            dk_ref[sl, :] = dk.astype(BF16)
            dv_ref[sl, :] = dv.astype(BF16)
            dla_ref[sl, :] = _dot(_tri(False), db, NN, HI)
            upd = _dot(doc_b, (qs * ch.eb).astype(BF16), TN)
            dstate[...] = dst * ebl + jnp.where(_state_mask(), upd, 0.0)
            return carry

        lax.fori_loop(0, nbc, chunk, 0, unroll=8)

    rev = lambda i: nblk - 1 - i
    qspec = lambda off: pl.BlockSpec((tb, PAIR_QK), lambda p, i: (rev(i), off // PAIR_QK + p))
    pair_qk = pl.BlockSpec((tb, PAIR_QK), lambda p, i: (rev(i), p))
    pair_v = pl.BlockSpec((tb, PAIR_V), lambda p, i: (rev(i), p))
    sel = _band_selector()
    ride_arrays, ride_gather = ride if ride else ([], [])
    nr = len(ride_arrays)
    grid = (npair, nblk)
    outs = pl.pallas_call(
        _riding(body, 8, 4, ride_gather, grid), grid=grid,
        in_specs=[qspec(O_GQ), qspec(O_GK), pl.BlockSpec((tb, PAIR_V), lambda p, i: (rev(i), O_GV // PAIR_V + p)),
                  pair_qk, pair_v, pl.BlockSpec((1, nbc, PAIR_V, PAIR_QK), lambda p, i: (p, rev(i), 0, 0)),
                  pl.BlockSpec((GLA_SUB * PAIR_QK, LANE), lambda p, i: (0, 0)),
                  pl.BlockSpec((LANE, GLA_SUB * PAIR_QK), lambda p, i: (0, 0))] + [HBM_SPEC] * nr,
        out_specs=[pair_qk, pair_qk, pair_v, pair_qk] + [HBM_SPEC] * nr,
        out_shape=[jax.ShapeDtypeStruct((S, GLA_QK), BF16), jax.ShapeDtypeStruct((S, GLA_QK), BF16),
                   jax.ShapeDtypeStruct((S, GLA_V), BF16), jax.ShapeDtypeStruct((S, GLA_QK), F32)]
        + _exchange_shapes(ride_arrays, ride_gather),
        scratch_shapes=[pltpu.VMEM((PAIR_V, PAIR_QK), F32)] + (_exchange_sems(nr) if nr else []),
        compiler_params=_cp("arbitrary", "arbitrary"), name="gla_bwd")(proj, proj, proj, la, do, states, sel, sel.T, *ride_arrays)
    return outs[0], outs[1], outs[2], outs[3], outs[4:]


def _gla_out(o, proj, gng, ts):
    S = o.shape[0]

    def body(o_ref, gr_ref, g_ref, y_ref):
        for h in range(GLA_HEADS):
            cols = slice(h * GLA_DV, (h + 1) * GLA_DV)
            ov, grv = o_ref[:, cols], gr_ref[:, cols]
            r = lax.rsqrt(jnp.mean(ov * ov, axis=-1, keepdims=True) + EPS)
            y_ref[:, cols] = (ov * r * g_ref[...] * (grv * _sigmoid(grv))).astype(BF16)

    return pl.pallas_call(
        body, grid=(S // ts,),
        in_specs=[pl.BlockSpec((ts, GLA_V), lambda i: (i, 0)), pl.BlockSpec((ts, GLA_V), lambda i: (i, O_GR // GLA_V)),
                  pl.BlockSpec((1, GLA_DV), lambda i: (0, 0))],
        out_specs=pl.BlockSpec((ts, GLA_V), lambda i: (i, 0)), out_shape=jax.ShapeDtypeStruct((S, GLA_V), BF16),
        compiler_params=_cp("parallel"), name="gla_out_fwd")(o, proj, gng)


def _mm_mixed_bwd(dt1, w_out, o, proj, gng, y_att, tm):
    S = o.shape[0]
    W = ATTN_DIM

    def epilogue(dm, step, o_ref, gr_ref, g_ref, y_ref, do_ref, dgr_ref, gg_ref, dy_ref, de_ref):
        gsum = jnp.zeros((1, GLA_DV), F32)
        for h in range(GLA_HEADS):
            cols = slice(h * GLA_DV, (h + 1) * GLA_DV)
            ov, grv, dy = o_ref[:, cols], gr_ref[:, cols], dm[:, cols]
            r = lax.rsqrt(jnp.mean(ov * ov, axis=-1, keepdims=True) + EPS)
            oh = ov * r
            sg = _sigmoid(grv)
            don = dy * (grv * sg)
            dgr_ref[:, cols] = (dy * (oh * g_ref[...]) * (sg * (1.0 + grv * (1.0 - sg)))).astype(BF16)
            gsum = gsum + jnp.sum(don * oh, axis=0, keepdims=True)
            doh = don * g_ref[...]
            do_ref[:, cols] = r * (doh - oh * jnp.mean(doh * oh, axis=-1, keepdims=True))
        _accumulate(gg_ref, _rows8([gsum], GLA_DV), step)
        dya = dm[:, GLA_V:]
        dy_ref[...] = dya
        de_ref[...] = _seg_sum(dya * y_ref[...], _seg_matrix(W, ATTN_HD, 1.0))

    half = pl.BlockSpec((tm, GLA_V), lambda i: (i, 0))
    outs, _ = _mm_rows(
        dt1, w_out, NT, tm, [o, proj, gng, y_att],
        [half, pl.BlockSpec((tm, GLA_V), lambda i: (i, O_GR // GLA_V)), pl.BlockSpec((1, GLA_DV), lambda i: (0, 0)), half],
        [jax.ShapeDtypeStruct((S, GLA_V), F32), jax.ShapeDtypeStruct((S, GLA_V), BF16), jax.ShapeDtypeStruct((8, GLA_DV), F32),
         jax.ShapeDtypeStruct((S, W), F32), jax.ShapeDtypeStruct((S, W), F32)],
        [half, half, pl.BlockSpec((8, GLA_DV), lambda i: (0, 0)), half, half], epilogue, "mm_dmixed")
    return outs


def _seg_matrix(width, seg, value):
    r = lax.broadcasted_iota(jnp.int32, (width, width), 0) // seg
    c = lax.broadcasted_iota(jnp.int32, (width, width), 1) // seg
    return jnp.where(r == c, value, 0.0).astype(BF16)


def _seg_sum(x, seg_matrix):
    hi = x.astype(BF16)
    lo = (x - hi.astype(F32)).astype(BF16)
    return _dot(hi, seg_matrix, NN) + _dot(lo, seg_matrix, NN)


def _head_norm(proj, qg, kg, ts):
    S = proj.shape[0]
    W = ATTN_DIM

    def body(q_ref, k_ref, qg_ref, kg_ref, qn_ref, kn_ref):
        seg = _seg_matrix(W, ATTN_HD, 1.0 / ATTN_HD)
        for x_ref, g_ref, o_ref, scale in ((q_ref, qg_ref, qn_ref, ATTN_HD ** -0.5), (k_ref, kg_ref, kn_ref, 1.0)):
            xv = x_ref[...]
            ms = _seg_sum(xv * xv, seg)
            o_ref[...] = xv * lax.rsqrt(ms + EPS) * (g_ref[...] * scale)

    blk = lambda off: pl.BlockSpec((ts, W), lambda i: (i, off // W))
    out = pl.BlockSpec((ts, W), lambda i: (i, 0))
    vec = pl.BlockSpec((1, W), lambda i: (0, 0))
    return pl.pallas_call(
        body, grid=(S // ts,), in_specs=[blk(O_AQ), blk(O_AK), vec, vec], out_specs=[out] * 2,
        out_shape=[jax.ShapeDtypeStruct((S, W), F32)] * 2, compiler_params=_cp("parallel"), name="attn_head_norm")(
            proj, proj, qg, kg)


def _slope(head):
    one = jnp.ones((1, 1), jnp.int32)
    return 1.0 / jnp.left_shift(one, one * (head + 1)).astype(F32)


ATTN_GROUP = 4


ATTN_TILE = max(DILATIONS) * ATTN_BLOCK


def _attn_rows(d, g, r, base=0):
    start = base + (g * d * ATTN_BLOCK if g >= 0 else ATTN_TILE - d * ATTN_BLOCK) + r
    return pl.ds(start, ATTN_BLOCK) if d == 1 else pl.ds(start, ATTN_BLOCK, stride=d)


def _for_blocks(d, G, fn):
    for g in range(G):
        if d <= ATTN_GROUP:
            for r in range(d):
                fn(g, r)
        else:
            def step(r, carry, g=g):
                fn(g, r)
                return carry
            lax.fori_loop(0, d, step, 0, unroll=ATTN_GROUP)


def _attn_specs(S):
    nb = S // ATTN_TILE

    def specs(off=0):
        return [pl.BlockSpec((ATTN_TILE, LANE), lambda hp, n: (n, off + hp)),
                pl.BlockSpec((ATTN_TILE, LANE), lambda hp, n: (jnp.maximum(n - 1, 0), off + hp)),
                pl.BlockSpec((ATTN_TILE, LANE), lambda hp, n: (jnp.minimum(n + 1, nb - 1), off + hp))]

    return nb, specs


def _attn_bias(d, hp, first_tile):
    B = ATTN_BLOCK
    iq = lax.broadcasted_iota(jnp.int32, (B, 2 * B), 0)
    ik = lax.broadcasted_iota(jnp.int32, (B, 2 * B), 1)
    rel = iq + B - ik
    window = (rel >= 0) & (rel <= B)
    relf = (d * rel).astype(F32)
    full = [jnp.where(window, -_slope(hp * 2 + h) * relf, NEG) for h in range(2)]
    edge = [jnp.where((ik >= B) | jnp.logical_not(first_tile), b, NEG) for b in full]
    return full, edge


def _attn_bias_t(d, hp, has_next):
    B = ATTN_BLOCK
    ik = lax.broadcasted_iota(jnp.int32, (B, B), 0)
    iq = lax.broadcasted_iota(jnp.int32, (B, B), 1)
    tiles = []
    for nxt in range(2):
        rel = iq - ik + nxt * B
        window = (rel >= 0) & (rel <= B)
        relf = (d * rel).astype(F32)
        tiles.append([jnp.where(window, -_slope(hp * 2 + h) * relf, NEG) for h in range(2)])
    tiles.append([jnp.where(has_next, b, NEG) for b in tiles[1]])
    return tiles


def _attn_fwd(qn, kn, proj):
    S, W = qn.shape
    T = ATTN_TILE
    nb, specs = _attn_specs(S)

    def body(q_ref, kp_ref, kc_ref, vp_ref, vc_ref, y_ref, l_ref, o_scr, l_scr):
        hp, n = pl.program_id(0), pl.program_id(1)
        lo = lax.broadcasted_iota(jnp.int32, (1, LANE), 1) < ATTN_HD
        for b, d in enumerate(DILATIONS):
            full, edge = _attn_bias(d, hp, n == 0)

            def sub(g, r, b=b, d=d, full=full, edge=edge):
                rows, before = _attn_rows(d, g, r), _attn_rows(d, g - 1, r)
                kb_ref, vb_ref = (kp_ref, vp_ref) if g == 0 else (kc_ref, vc_ref)
                bias = edge if g == 0 else full
                qv = q_ref[rows, :].astype(BF16)
                kv = jnp.concatenate([kb_ref[before, :], kc_ref[rows, :]], axis=0).astype(BF16)
                vv = jnp.concatenate([vb_ref[before, :], vc_ref[rows, :]], axis=0).astype(BF16)
                outs, lses = [], []
                for h in range(2):
                    qm = jnp.where(lo == (h == 0), qv, jnp.zeros_like(qv))
                    s = _dot(qm, kv, NT) + bias[h]
                    m = jnp.max(s, axis=-1, keepdims=True)
                    p = jnp.exp(s - m)
                    den = jnp.sum(p, axis=-1, keepdims=True)
                    outs.append(_dot(p.astype(BF16), vv, NN) / den)
                    lses.append(m + jnp.log(den))
                kept = _attn_rows(d, g, r, base=b * T)
                o_scr[kept, :] = jnp.where(lo, outs[0], outs[1])
                l_scr[kept, :] = jnp.where(lo, lses[0], lses[1])

            _for_blocks(d, T // (d * ATTN_BLOCK), sub)
        l1, l2, l3 = [l_scr[pl.ds(b * T, T), :] for b in range(len(DILATIONS))]
        o1, o2, o3 = [o_scr[pl.ds(b * T, T), :] for b in range(len(DILATIONS))]
        m = jnp.maximum(jnp.maximum(l1, l2), l3)
        e1, e2, e3 = jnp.exp(l1 - m), jnp.exp(l2 - m), jnp.exp(l3 - m)
        tot = e1 + e2 + e3
        y_ref[...] = (e1 * o1 + e2 * o2 + e3 * o3) / tot
        l_ref[...] = m + jnp.log(tot)

    cur, prev, _ = specs()
    vcur, vprev, _ = specs(O_AV // LANE)
    return pl.pallas_call(
        body, grid=(W // LANE, nb), in_specs=[cur, prev, cur, vprev, vcur], out_specs=[cur, cur],
        out_shape=[jax.ShapeDtypeStruct((S, W), F32)] * 2,
        scratch_shapes=[pltpu.VMEM((len(DILATIONS) * T, LANE), F32)] * 2,
        compiler_params=_cp("parallel", "arbitrary"), name="attn_fwd")(qn, kn, kn, proj, proj)


def _attn_mix(y_gla, y_att, ts):
    S, W = y_att.shape

    def body(yg, ya, mixed_ref, mixed_t_ref):
        y = ya[...]
        mixed_ref[:, :W] = yg[...]
        mixed_ref[:, W:] = y.astype(BF16)
        mixed_t_ref[:W, :] = yg[...].astype(F32).T.astype(BF16)
        mixed_t_ref[W:, :] = y.T.astype(BF16)

    spec = pl.BlockSpec((ts, W), lambda i: (i, 0))
    return pl.pallas_call(
        body, grid=(S // ts,), in_specs=[spec] * 2,
        out_specs=[pl.BlockSpec((ts, 2 * W), lambda i: (i, 0)), _col_spec(2 * W, ts)],
        out_shape=[jax.ShapeDtypeStruct((S, 2 * W), BF16), jax.ShapeDtypeStruct((2 * W, S), BF16)],
        compiler_params=_cp("parallel"), name="attn_mix")(y_gla, y_att)


def _attn_dq(qn, kn, proj, dy, lse, delta):
    S, W = qn.shape
    nb, specs = _attn_specs(S)

    def body(q_ref, kp_ref, kc_ref, vp_ref, vc_ref, dy_ref, l_ref, de_ref, dq_ref):
        hp, n = pl.program_id(0), pl.program_id(1)
        lo = lax.broadcasted_iota(jnp.int32, (1, LANE), 1) < ATTN_HD
        for b, d in enumerate(DILATIONS):
            _attn_dq_branch(b, d, _attn_bias(d, hp, n == 0), lo, q_ref, kp_ref, kc_ref, vp_ref, vc_ref, dy_ref, l_ref, de_ref, dq_ref)

    cur, prev, _ = specs()
    vcur, vprev, _ = specs(O_AV // LANE)
    return pl.pallas_call(
        body, grid=(W // LANE, nb), in_specs=[cur, prev, cur, vprev, vcur, cur, cur, cur], out_specs=cur,
        out_shape=jax.ShapeDtypeStruct((S, W), F32),
        compiler_params=_cp("parallel", "arbitrary"), name="attn_dq")(qn, kn, kn, proj, proj, dy, lse, delta)


def _attn_dq_branch(b, d, biases, lo, q_ref, kp_ref, kc_ref, vp_ref, vc_ref, dy_ref, l_ref, de_ref, dq_ref):
    full, edge = biases

    def sub(g, r):
        rows, before = _attn_rows(d, g, r), _attn_rows(d, g - 1, r)
        kb_ref, vb_ref = (kp_ref, vp_ref) if g == 0 else (kc_ref, vc_ref)
        bias = edge if g == 0 else full
        qv, dyv = q_ref[rows, :].astype(BF16), dy_ref[rows, :]
        lv, dev = l_ref[rows, :], de_ref[rows, :]
        kv = jnp.concatenate([kb_ref[before, :], kc_ref[rows, :]], axis=0).astype(BF16)
        vv = jnp.concatenate([vb_ref[before, :], vc_ref[rows, :]], axis=0).astype(BF16)
        outs = []
        for h in range(2):
            sel = lo == (h == 0)
            qm = jnp.where(sel, qv, jnp.zeros_like(qv))
            dym = jnp.where(sel, dyv, 0.0).astype(BF16)
            lse_h = lv[:, h * ATTN_HD:h * ATTN_HD + 1]
            del_h = dev[:, h * ATTN_HD:h * ATTN_HD + 1]
            p = jnp.exp(_dot(qm, kv, NT) + bias[h] - lse_h)
            ds = p * (_dot(dym, vv, NT) - del_h)
            outs.append(_dot(ds.astype(BF16), kv, NN) * (ATTN_HD ** -0.5))
        dq = jnp.where(lo, outs[0], outs[1])
        dq_ref[rows, :] = dq if b == 0 else dq_ref[rows, :] + dq

    _for_blocks(d, ATTN_TILE // (d * ATTN_BLOCK), sub)


def _attn_dkv(qn, kn, proj, dy, lse, delta):
    S, W = qn.shape
    nb, specs = _attn_specs(S)

    def body(k_ref, v_ref, qc_ref, qn_ref, dyc_ref, dyn_ref, lc_ref, ln_ref, dec_ref, den_ref, dk_ref, dv_ref):
        hp, n = pl.program_id(0), pl.program_id(1)
        lo = lax.broadcasted_iota(jnp.int32, (1, LANE), 1) < ATTN_HD
        cur_refs, next_refs = (qc_ref, dyc_ref, lc_ref, dec_ref), (qn_ref, dyn_ref, ln_ref, den_ref)
        for b, d in enumerate(DILATIONS):
            _attn_dkv_branch(b, d, _attn_bias_t(d, hp, n + 1 < nb), lo, k_ref, v_ref, cur_refs, next_refs, dk_ref, dv_ref)

    cur, _, nxt = specs()
    vcur, _, _ = specs(O_AV // LANE)
    return pl.pallas_call(
        body, grid=(W // LANE, nb), in_specs=[cur, vcur, cur, nxt, cur, nxt, cur, nxt, cur, nxt], out_specs=[cur, cur],
        out_shape=[jax.ShapeDtypeStruct((S, W), F32)] * 2,
        compiler_params=_cp("parallel", "arbitrary"), name="attn_dkv")(
            kn, proj, qn, qn, dy, dy, lse, lse, delta, delta)


def _attn_dkv_branch(b, d, biases, lo, k_ref, v_ref, cur_refs, next_refs, dk_ref, dv_ref):
    B = ATTN_BLOCK
    own, inner, outer = biases
    G = ATTN_TILE // (d * B)

    def sub(g, r):
        rows = _attn_rows(d, g, r)
        kv, vv = k_ref[rows, :].astype(BF16), v_ref[rows, :].astype(BF16)
        dk = jnp.zeros((B, LANE), F32)
        dv = jnp.zeros((B, LANE), F32)
        inside = g + 1 < G
        after = _attn_rows(d, g + 1 if inside else 0, r)
        for bias, qrows, (q_ref, dy_ref, l_ref, de_ref) in (
                (own, rows, cur_refs), (inner if inside else outer, after, cur_refs if inside else next_refs)):
            qv, dyv = q_ref[qrows, :].astype(BF16), dy_ref[qrows, :]
            lt, det = l_ref[qrows, :].T, de_ref[qrows, :].T
            for h in range(2):
                sel = lo == (h == 0)
                qm = jnp.where(sel, qv, jnp.zeros_like(qv))
                dym = jnp.where(sel, dyv, 0.0).astype(BF16)
                lse_h = lt[h * ATTN_HD:h * ATTN_HD + 1, :]
                del_h = det[h * ATTN_HD:h * ATTN_HD + 1, :]
                pt = jnp.exp(_dot(kv, qm, NT) + bias[h] - lse_h)
                dv = dv + _dot(pt.astype(BF16), dym, NN)
                dst = pt * (_dot(vv, dym, NT) - del_h)
                dk = dk + _dot(dst.astype(BF16), qm, NN)
        dk_ref[rows, :] = dk if b == 0 else dk_ref[rows, :] + dk
        dv_ref[rows, :] = dv if b == 0 else dv_ref[rows, :] + dv

    _for_blocks(d, G, sub)


def _attn_post(dq, dk, dv, proj, qg, kg, ts):
    S = proj.shape[0]
    W = ATTN_DIM

    def body(dq_ref, dk_ref, dv_ref, aq_ref, ak_ref, qg_ref, kg_ref, daq_ref, dak_ref, dav_ref, gg_ref):
        i = pl.program_id(0)
        seg = _seg_matrix(W, ATTN_HD, 1.0 / ATTN_HD)
        gsums = []
        for d_ref, x_ref, g_ref, o_ref in ((dq_ref, aq_ref, qg_ref, daq_ref), (dk_ref, ak_ref, kg_ref, dak_ref)):
            dy = d_ref[...]
            xv = x_ref[...]
            r = lax.rsqrt(_seg_sum(xv * xv, seg) + EPS)
            xh = xv * r
            dxh = dy * g_ref[...]
            o_ref[...] = (r * (dxh - xh * _seg_sum(dxh * xh, seg))).astype(BF16)
            gsums.append(jnp.sum(dy * xh, axis=0, keepdims=True))
        dav_ref[...] = dv_ref[...].astype(BF16)
        _accumulate(gg_ref, _rows8(gsums, W), i)

    row = pl.BlockSpec((ts, W), lambda i: (i, 0))
    blk = lambda off: pl.BlockSpec((ts, W), lambda i: (i, off // W))
    vec = pl.BlockSpec((1, W), lambda i: (0, 0))
    return pl.pallas_call(
        body, grid=(S // ts,), in_specs=[row] * 3 + [blk(O_AQ), blk(O_AK), vec, vec],
        out_specs=[row, row, row, pl.BlockSpec((8, W), lambda i: (0, 0))],
        out_shape=[jax.ShapeDtypeStruct((S, W), BF16)] * 3 + [jax.ShapeDtypeStruct((8, W), F32)],
        compiler_params=_cp("arbitrary"), name="attn_post")(dq, dk, dv, proj, proj, qg, kg)


def _shift_down(cur, halo, n):
    return pltpu.roll(jnp.concatenate([halo, cur], axis=0), n, 0)[8:]


def _shift_up(cur, halo, n):
    ts = cur.shape[0]
    return pltpu.roll(jnp.concatenate([cur, halo], axis=0), ts + 8 - n, 0)[:ts]


def _conv(cur, halo, w, b):
    return b + w[0:1, :] * _shift_down(cur, halo, 2) + w[1:2, :] * _shift_down(cur, halo, 1) + w[2:3, :] * cur


def _mm_up_swiglu(h2, w_up, conv_w8, conv_b, tm, tc, ride=None):
    S, D = h2.shape
    F = w_up.shape[1] // 2
    nc = F // tc
    grid = (S // tm, nc)
    ride_arrays, ride_gather = ride if ride else ([], [])
    nr = len(ride_arrays)

    def body(h_ref, bg_ref, bv_ref, wg_ref, wv_ref, cg_ref, cv_ref, u0_ref, a_ref, at_ref, halo):
        i, j = pl.program_id(0), pl.program_id(1)
        hv = h_ref[...]
        acts = []
        for h, (b_ref, w_ref, c_ref) in enumerate(((bg_ref, wg_ref, cg_ref), (bv_ref, wv_ref, cv_ref))):
            u = _dot(hv, b_ref[...], NN)
            u0_ref[h] = u
            acts.append(_conv(u, jnp.where(i == 0, 0.0, halo[j, h]), w_ref[...], c_ref[...]))
            halo[j, h] = u[tm - 8:, :]
        g, v = acts
        a = g * _sigmoid(g) * v
        a_ref[...] = a.astype(BF16)
        at_ref[...] = a.T.astype(BF16)

    wcol = lambda rows, off: pl.BlockSpec((rows, tc), lambda i, j: (0, j + off))
    outs = pl.pallas_call(
        _riding(body, 7, 3, ride_gather, grid), grid=grid,
        in_specs=[pl.BlockSpec((tm, D), lambda i, j: (i, 0)), wcol(D, 0), wcol(D, nc), wcol(8, 0), wcol(8, nc), wcol(1, 0), wcol(1, nc)]
        + [HBM_SPEC] * nr,
        out_specs=[pl.BlockSpec((2, tm, tc), lambda i, j: (0, i, j)), pl.BlockSpec((tm, tc), lambda i, j: (i, j)),
                   pl.BlockSpec((tc, tm), lambda i, j: (j, i))] + [HBM_SPEC] * nr,
        out_shape=[jax.ShapeDtypeStruct((2, S, F), F32), jax.ShapeDtypeStruct((S, F), BF16), jax.ShapeDtypeStruct((F, S), BF16)]
        + _exchange_shapes(ride_arrays, ride_gather),
        scratch_shapes=[pltpu.VMEM((nc, 2, 8, tc), F32)] + (_exchange_sems(nr) if nr else []),
        compiler_params=_cp("arbitrary", "arbitrary"), name="mm_up")(
            h2, w_up, w_up, conv_w8, conv_w8, conv_b, conv_b, *ride_arrays)
    return outs[0], outs[1], outs[2], outs[3:]


def _mm_da_du(dt2, w_down, u0, conv_w8, conv_b, tm, tc, ride=None):
    _, S, F = u0.shape
    D = dt2.shape[1]
    nc = F // tc
    hb = tm // 8
    grid = (nc, S // tm)
    ride_arrays, ride_gather = ride if ride else ([], [])
    nr = len(ride_arrays)

    def body(dt_ref, wd_ref, ug_ref, ugh_ref, uv_ref, uvh_ref, wg_ref, wv_ref, bg_ref, bv_ref, du_ref, sg_ref, sv_ref):
        i = pl.program_id(1)
        first = i == 0
        halves = []
        for u_ref, h_ref, w_ref, b_ref in ((ug_ref, ugh_ref, wg_ref, bg_ref), (uv_ref, uvh_ref, wv_ref, bv_ref)):
            u, halo, w = u_ref[...], jnp.where(first, 0.0, h_ref[...]), w_ref[...]
            s2, s1 = _shift_down(u, halo, 2), _shift_down(u, halo, 1)
            halves.append((b_ref[...] + w[0:1, :] * s2 + w[1:2, :] * s1 + w[2:3, :] * u, s2, s1, u))
        g, v = halves[0][0], halves[1][0]
        dav = _dot(dt_ref[...], wd_ref[...], NT)
        sig = _sigmoid(g)
        dus = (dav * v * (sig * (1.0 + g * (1.0 - sig))), dav * (g * sig))
        for h, (du, sums_ref) in enumerate(zip(dus, (sg_ref, sv_ref))):
            du_ref[h] = du
            _, s2, s1, u = halves[h]
            _accumulate(sums_ref, _rows8([jnp.sum(du * s2, axis=0, keepdims=True), jnp.sum(du * s1, axis=0, keepdims=True),
                                          jnp.sum(du * u, axis=0, keepdims=True), jnp.sum(du, axis=0, keepdims=True)], tc), i)

    main = lambda h: pl.BlockSpec((None, tm, tc), lambda j, i: (h, i, j))
    halo = lambda h: pl.BlockSpec((None, 8, tc), lambda j, i: (h, jnp.maximum(i * hb - 1, 0), j))
    wcol = lambda rows, off: pl.BlockSpec((rows, tc), lambda j, i: (0, j + off))
    sums_spec = pl.BlockSpec((8, tc), lambda j, i: (0, j))
    outs = pl.pallas_call(
        _riding(body, 10, 3, ride_gather, grid), grid=grid,
        in_specs=[pl.BlockSpec((tm, D), lambda j, i: (i, 0)), pl.BlockSpec((tc, D), lambda j, i: (j, 0)),
                  main(0), halo(0), main(1), halo(1), wcol(8, 0), wcol(8, nc), wcol(1, 0), wcol(1, nc)] + [HBM_SPEC] * nr,
        out_specs=[pl.BlockSpec((2, tm, tc), lambda j, i: (0, i, j)), sums_spec, sums_spec] + [HBM_SPEC] * nr,
        out_shape=[jax.ShapeDtypeStruct((2, S, F), F32), jax.ShapeDtypeStruct((8, F), F32), jax.ShapeDtypeStruct((8, F), F32)]
        + _exchange_shapes(ride_arrays, ride_gather),
        scratch_shapes=_exchange_sems(nr) if nr else [],
        compiler_params=_cp("arbitrary", "arbitrary"), name="mm_da")(
            dt2, w_down, u0, u0, u0, u0, conv_w8, conv_w8, conv_b, conv_b, *ride_arrays)
    return outs[0], outs[1], outs[2], outs[3:]


def _ffn_du0(du, conv_w8, ts, tc):
    _, S, F = du.shape
    nc = F // tc
    hb = ts // 8
    nrow = S // ts

    def body(du_ref, duh_ref, w_ref, o_ref):
        last = pl.program_id(0) == nrow - 1
        cur, halo, w = du_ref[...], jnp.where(last, 0.0, duh_ref[...]), w_ref[...]
        o_ref[...] = (w[2:3, :] * cur + w[1:2, :] * _shift_up(cur, halo, 1) + w[0:1, :] * _shift_up(cur, halo, 2)).astype(BF16)

    return pl.pallas_call(
        body, grid=(nrow, 2, nc),
        in_specs=[pl.BlockSpec((None, ts, tc), lambda i, h, j: (h, i, j)),
                  pl.BlockSpec((None, 8, tc), lambda i, h, j: (h, jnp.minimum((i + 1) * hb, S // 8 - 1), j)),
                  pl.BlockSpec((8, tc), lambda i, h, j: (0, h * nc + j))],
        out_specs=pl.BlockSpec((ts, tc), lambda i, h, j: (i, h * nc + j)), out_shape=jax.ShapeDtypeStruct((S, 2 * F), BF16),
        compiler_params=_cp("parallel", "parallel", "parallel"), name="ffn_du0")(du, du, conv_w8)


def _adamw(w, g, m, v, name, slots=False):
    shape = w.shape
    view = (math.prod(shape[:-1]), shape[-1])
    R, C = view
    limit = SUM_BLOCK_ELEMS // 2 if slots else SUM_BLOCK_ELEMS
    fits = [t for t in range(16, R + 1, 16) if R % t == 0 and t * C <= limit]
    tr = max(fits) if fits else R

    def body(w_ref, g_ref, m_ref, v_ref, *outs):
        if slots:
            gv = g_ref[0].astype(F32)
            for s in range(1, N_DEV):
                gv = gv + g_ref[s].astype(F32)
            outs[0][...] = gv
        else:
            gv = g_ref[...]
        d_ref, nm_ref, nv_ref = outs[-3:]
        nm = ADAM_B1 * m_ref[...] + (1.0 - ADAM_B1) * gv
        nv = ADAM_B2 * v_ref[...] + (1.0 - ADAM_B2) * (gv * gv)
        m_hat = nm / (1.0 - ADAM_B1 ** ADAM_STEP)
        v_hat = nv / (1.0 - ADAM_B2 ** ADAM_STEP)
        d_ref[...] = -ADAM_LR * (m_hat / (jnp.sqrt(v_hat) + ADAM_EPS) + ADAM_WD * w_ref[...])
        nm_ref[...] = nm
        nv_ref[...] = nv

    spec = pl.BlockSpec((tr, C), lambda i: (i, 0))
    g_spec = pl.BlockSpec((N_DEV, tr, C), lambda i: (0, i, 0)) if slots else spec
    n_out = 4 if slots else 3
    outs = pl.pallas_call(
        body, grid=(R // tr,), in_specs=[spec, g_spec, spec, spec], out_specs=[spec] * n_out,
        out_shape=[jax.ShapeDtypeStruct(view, F32)] * n_out, compiler_params=_cp("parallel"), name=name)(
            w.reshape(view), g if slots else g.reshape(view), m.reshape(view), v.reshape(view))
    outs = [o.reshape(shape) for o in outs]
    return outs if slots else [g.reshape(shape)] + outs


def _pad_rows8(a):
    return jnp.concatenate([a, jnp.zeros((8 - a.shape[0], a.shape[1]), a.dtype)], axis=0)


def _local_step(x, target, mod, n1g, w_in_s, conv_w_s, wg_s, bg, gng, qng, kng, w_out_s, n2g, w_up_s, conv_b, w_down_s):
    S, D = x.shape
    F = w_down_s.shape[0] * N_DEV
    cw_c, wg_c = conv_w_s.shape[1], wg_s.shape[1]
    ts = min(512, S)
    sh1, sc1, g1, sh2, sc2, g2 = [mod[i:i + 1] for i in range(6)]
    qg_t, kg_t = jnp.tile(qng, (1, ATTN_HEADS)), jnp.tile(kng, (1, ATTN_HEADS))

    small = jnp.concatenate([conv_w_s.reshape(1, -1), wg_s.reshape(1, -1)], axis=1)
    n_small = small.shape[1]
    small = jnp.pad(small, ((0, 0), (0, -n_small % LANE)))
    h1, h1_t, (g_in, g_small) = _rms_mod(x, n1g, sc1, sh1, ts, "rms_mod1", ride=([w_in_s, small], [True, True]))
    w_in_full = _cols_from_blocks(g_in)
    w_in_p = jnp.concatenate([w_in_full[:, :GLR_SRC], w_in_full[:, GLR_SRC + GLA_RANK:],
                              w_in_full[:, GLR_SRC:GLR_SRC + GLA_RANK], jnp.zeros((D, PROJ_W - O_GLR - GLA_RANK), BF16)], axis=1)
    g_small = g_small.reshape(N_DEV, -1)
    conv_w8 = _pad_rows8(jnp.stack([g_small[:, t * cw_c:(t + 1) * cw_c].reshape(-1) for t in range(3)]))
    wg_full = _cols_from_blocks(g_small[:, 3 * cw_c:n_small].reshape(N_DEV, GLA_RANK, wg_c))
    wg_p = jnp.concatenate([wg_full, jnp.zeros((LANE - GLA_RANK, wg_full.shape[1]), F32)], axis=0)
    proj, (g_out,) = _mm(h1, w_in_p, NN, 512, PROJ_W, 1024, F32, "mm_in", ride=([w_out_s], [True]))
    w_out = g_out.reshape(-1, D)
    la = _gate_fwd(proj, wg_p, bg, ts)
    o_gla, states, (g_up,) = _gla_fwd(proj, la, 512, ride=([w_up_s], [True]))
    w_up = _cols_from_blocks(g_up)
    y_gla = _gla_out(o_gla, proj, gng, ts)
    qn, kn = _head_norm(proj, qg_t, kg_t, ts)
    y_att, lse = _attn_fwd(qn, kn, proj)
    mixed, mixed_t = _attn_mix(y_gla, y_att, ts)
    t1, x2, h2, h2_t = _mm_resid_rms_mod(mixed, w_out, x, g1, n2g, sc2, sh2, ts, "mm_out")
    tc = 1408 if F % 1408 == 0 else F
    u0, a, a_t, (g_down,) = _mm_up_swiglu(h2, w_up, conv_w8, conv_b, ts, tc, ride=([w_down_s], [True]))
    w_down = g_down.reshape(F, D)
    dx3, dt2, sums3 = _mm_loss_resid(a, w_down, x2, g2, target, ts, "mm_down")
    loss_row, dg2 = sums3[0:1], sums3[1:2]

    g_w_down = _mm(a_t, dt2, NN, 1408, 1024, 2048, F32, "mm_gw_down")
    du, sums_g, sums_v, (r_down,) = _mm_da_du(dt2, w_down, u0, conv_w8, conv_b, ts, tc,
                                              ride=([g_w_down.reshape(N_DEV, -1, D)], [False]))
    g_conv_w = jnp.concatenate([sums_g[0:3], sums_v[0:3]], axis=1)
    g_conv_b = jnp.concatenate([sums_g[3:4], sums_v[3:4]], axis=1)
    du0 = _ffn_du0(du, conv_w8, min(256, S), tc)
    g_w_up = _mm(h2_t, du0, NN, 512, 2816, 2048, F32, "mm_gw_up")
    (dx2, sums2, dt1), _ = _mm_rms_mod_bwd(du0, w_up, x2, dx3, n2g, sc2, ts, "mm_dh2", t_prev=t1, g_prev=g1)
    dsh2, dsc2, g_n2g, dg1 = sums2[0:1], sums2[1:2], sums2[2:3], sums2[3:4]
    g_w_out = _mm(mixed_t, dt1, NN, 1024, 1024, 2048, F32, "mm_gw_out")
    do_gla, dgr, gng_sums, dy_att, delta = _mm_mixed_bwd(dt1, w_out, o_gla, proj, gng, y_att, ts)
    dgq, dgk, dgv, dla, (r_up, r_out) = _gla_bwd(
        proj, la, do_gla, states, 512, ride=([_col_blocks(g_w_up), g_w_out.reshape(N_DEV, -1, D)], [False, False]))
    dglr, g_wg_p, gb_sums = _gate_bwd(dla, la, proj, wg_p, ts)
    dqn = _attn_dq(qn, kn, proj, dy_att, lse, delta)
    dkn, dvn = _attn_dkv(qn, kn, proj, dy_att, lse, delta)
    daq, dak, dav, qk_sums = _attn_post(dqn, dkn, dvn, proj, qg_t, kg_t, ts)
    dproj = jnp.concatenate([dgq, dgk, dgv, dgr, daq, dak, dav, dglr, jnp.zeros((S, PROJ_W - O_GLR - LANE), BF16)], axis=1)
    g_w_in_p = _mm(h1_t, dproj, NN, 512, PROJ_W, 1024, F32, "mm_gw_in")
    g_w_in = jnp.concatenate([g_w_in_p[:, :GLR_SRC], g_w_in_p[:, O_GLR:O_GLR + GLA_RANK], g_w_in_p[:, GLR_SRC:O_GLR]], axis=1)
    (dx, sums1), (r_in,) = _mm_rms_mod_bwd(dproj, w_in_p, x, dx2, n1g, sc1, ts, "mm_dh1",
                                           ride=([_col_blocks(g_w_in).astype(BF16)], [False]))
    dsh1, dsc1, g_n1g = sums1[0:1], sums1[1:2], sums1[2:3]

    dmod = jnp.concatenate([dsh1, dsc1, dg1, dsh2, dsc2, dg2], axis=1)
    grads = dict(n1g=g_n1g, w_in=r_in, wg=g_wg_p[:GLA_RANK], bg=gb_sums[0:1], gng=gng_sums[0:1],
                 qng_lanes=qk_sums[0:1], kng_lanes=qk_sums[1:2], w_out=r_out, n2g=g_n2g, w_up=r_up,
                 conv_w=g_conv_w, conv_b=g_conv_b, w_down=r_down)
    return loss_row, dx, dmod, grads


def _col_blocks(a):
    R, W = a.shape
    return a.reshape(R, N_DEV, W // N_DEV).transpose(1, 0, 2)


def _cols_from_blocks(a):
    n, R, C = a.shape
    return a.transpose(1, 0, 2).reshape(R, n * C)


def kernel(x, c, w_ada, b_ada, norm1_g, w_in, gla_w_gate, gla_b_gate, gla_norm_g, q_norm_g, k_norm_g, w_out, norm2_g, w_up, conv_w, conv_b, w_down, loss_target, m_w_ada, m_b_ada, m_norm1_g, m_w_in, m_gla_w_gate, m_gla_b_gate, m_gla_norm_g, m_q_norm_g, m_k_norm_g, m_w_out, m_norm2_g, m_w_up, m_conv_w, m_conv_b, m_w_down, v_w_ada, v_b_ada, v_norm1_g, v_w_in, v_gla_w_gate, v_gla_b_gate, v_gla_norm_g, v_q_norm_g, v_k_norm_g, v_w_out, v_norm2_g, v_w_up, v_conv_w, v_conv_b, v_w_down):
    axes = ("x", "y", "c")
    me = 4 * lax.axis_index("x") + 2 * lax.axis_index("y") + lax.axis_index("c")
    S, D = x.shape[1], x.shape[2]
    x2d, tgt2d = x[0], loss_target[0]
    w_in_s, w_out_s, w_up_s, w_down_s, w_ada_s = w_in[0], w_out[0], w_up[0], w_down[0], w_ada[0]
    conv_w_s, wg_s = conv_w[0], gla_w_gate[0]
    in_c, up_c, ada_c, wg_c, cw_c = w_in_s.shape[1], w_up_s.shape[1], w_ada_s.shape[1], wg_s.shape[1], conv_w_s.shape[1]
    F = w_down_s.shape[0] * N_DEV

    g_c, = _exchange([c], [True], "gather_c")
    c_all = g_c.reshape(N_DEV, D)

    b_shard = lax.dynamic_slice(b_ada, (0, me * ada_c), (1, ada_c))
    mod_part = _ada_fwd(c_all, w_ada_s, b_shard)
    mod_recv, = _exchange([mod_part.reshape(N_DEV, 1, ada_c)], [False], "exchange_mod")
    mod = mod_recv.reshape(6, D)

    loss_row, dx, dmod, gr = _local_step(
        x2d, tgt2d, mod, norm1_g, w_in_s.astype(BF16), conv_w_s, wg_s, gla_b_gate, gla_norm_g, q_norm_g, k_norm_g,
        w_out_s.astype(BF16), norm2_g, w_up_s.astype(BF16), conv_b, w_down_s.astype(BF16))
    loss = lax.psum(0.5 / D * jnp.sum(loss_row), axes)

    parts = [dmod, gr["n1g"], gr["bg"], gr["gng"], gr["qng_lanes"], gr["kng_lanes"], gr["n2g"], gr["conv_b"],
             gr["wg"].reshape(1, -1), gr["conv_w"].reshape(1, -1)]
    sizes = [p.shape[1] for p in parts]
    packed = jnp.concatenate(parts, axis=1)
    packed = jnp.pad(packed, ((0, 0), (0, -packed.shape[1] % (8 * LANE))))
    gathered, = _exchange([packed.reshape(8, -1)], [True], "gather_small_grads")
    gathered = gathered.reshape(N_DEV, -1)
    total = _sum_slots(gathered.reshape(N_DEV, 8, -1), "sum_small_grads").reshape(1, -1)
    offs = [0]
    for s_ in sizes:
        offs.append(offs[-1] + s_)
    t_dmod, t_n1g, t_bg, t_gng, t_qng, t_kng, t_n2g, t_conv_b, t_wg, t_conv_w = [
        total[:, offs[i]:offs[i + 1]] for i in range(len(sizes))]
    g_b_ada = t_dmod
    g_qng = t_qng.reshape(ATTN_HEADS, ATTN_HD).sum(axis=0, keepdims=True)
    g_kng = t_kng.reshape(ATTN_HEADS, ATTN_HD).sum(axis=0, keepdims=True)
    g_wg = lax.dynamic_slice(t_wg.reshape(GLA_RANK, -1), (0, me * wg_c), (GLA_RANK, wg_c))
    g_conv_w = lax.dynamic_slice(t_conv_w.reshape(3, -1), (0, me * cw_c), (3, cw_c))
    dmod_shard = lax.dynamic_slice(gathered[:, :6 * D], (0, me * ada_c), (N_DEV, ada_c))
    g_w_ada = _ada_bwd(c_all, dmod_shard)

    g_w_in, g_w_out, g_w_up, g_w_down = gr["w_in"], gr["w_out"], gr["w_up"], gr["w_down"]
    in_slots = {"w_in", "w_out", "w_up", "w_down"}
    names = ["w_ada", "b_ada", "norm1_g", "w_in", "gla_w_gate", "gla_b_gate", "gla_norm_g", "q_norm_g", "k_norm_g",
             "w_out", "norm2_g", "w_up", "conv_w", "conv_b", "w_down"]
    ws = [w_ada, b_ada, norm1_g, w_in, gla_w_gate, gla_b_gate, gla_norm_g, q_norm_g, k_norm_g, w_out, norm2_g, w_up, conv_w, conv_b, w_down]
    ms = [m_w_ada, m_b_ada, m_norm1_g, m_w_in, m_gla_w_gate, m_gla_b_gate, m_gla_norm_g, m_q_norm_g, m_k_norm_g, m_w_out, m_norm2_g, m_w_up, m_conv_w, m_conv_b, m_w_down]
    vs = [v_w_ada, v_b_ada, v_norm1_g, v_w_in, v_gla_w_gate, v_gla_b_gate, v_gla_norm_g, v_q_norm_g, v_k_norm_g, v_w_out, v_norm2_g, v_w_up, v_conv_w, v_conv_b, v_w_down]
    gs = [g_w_ada, g_b_ada, t_n1g, g_w_in, g_wg, t_bg, t_gng, g_qng, g_kng, g_w_out, t_n2g, g_w_up, g_conv_w, t_conv_b, g_w_down]
    grads, deltas, new_ms, new_vs = [], [], [], []
    for nm, w, g, m, v in zip(names, ws, gs, ms, vs):
        g_, d_, m_, v_ = _adamw(w, g, m, v, "adamw_" + nm, slots=nm in in_slots)
        grads.append(g_)
        deltas.append(d_)
        new_ms.append(m_)
        new_vs.append(v_)
    return (loss, dx.reshape(x.shape), *grads, *deltas, *new_ms, *new_vs)
```

```python
import functools
import math

import jax
import jax.numpy as jnp
from jax import lax
from jax.experimental import pallas as pl
from jax.experimental.pallas import tpu as pltpu

F32, BF16 = jnp.float32, jnp.bfloat16
HI = lax.Precision.HIGHEST
EPS = 1e-6
NEG = -1e30

N_DEV = 8
GLA_HEADS, GLA_DK, GLA_DV, GLA_RANK, GLA_TAU, GLA_CHUNK = 4, 64, 128, 16, 16.0, 64
ATTN_HEADS, ATTN_HD, ATTN_BLOCK = 8, 64, 128
DILATIONS = (1, 4, 16)
GLA_QK, GLA_V, ATTN_DIM = GLA_HEADS * GLA_DK, GLA_HEADS * GLA_DV, ATTN_HEADS * ATTN_HD
O_GQ, O_GK, O_GV, O_GR, O_AQ, O_AK, O_AV, O_GLR = 0, 256, 512, 1024, 1536, 2048, 2560, 3072
PROJ_W = 3328
LANE = 128
GLR_SRC = 2 * GLA_QK + 2 * GLA_V

ADAM_LR, ADAM_B1, ADAM_B2, ADAM_EPS, ADAM_WD, ADAM_STEP = 0.001, 0.9, 0.999, 1e-08, 0.01, 10

VMEM_LIMIT = 56 * 1024 * 1024
SUM_BLOCK_ELEMS = 256 * 1024


def _cp(*sem):
    return pltpu.CompilerParams(dimension_semantics=sem, vmem_limit_bytes=VMEM_LIMIT)


def _dot(a, b, dims, precision=None):
    return lax.dot_general(a, b, (dims, ((), ())), preferred_element_type=F32, precision=precision)


NN, NT, TN = ((1,), (0,)), ((1,), (1,)), ((0,), (0,))


def _sigmoid(z):
    return 1.0 / (1.0 + jnp.exp(-z))


HBM_SPEC = pl.BlockSpec(memory_space=pltpu.HBM)


def _exchange_shapes(arrays, gather):
    return [jax.ShapeDtypeStruct((N_DEV,) + (a.shape if g else a.shape[1:]), a.dtype) for a, g in zip(arrays, gather)]


def _exchange_sems(n):
    return [pltpu.SemaphoreType.DMA((n * (N_DEV - 1),)), pltpu.SemaphoreType.DMA((n * (N_DEV - 1),)), pltpu.SemaphoreType.DMA((n,))]


def _exchange_copies(ins, outs, gather, send_sems, recv_sems, local_sems):
    x, y, c = lax.axis_index("x"), lax.axis_index("y"), lax.axis_index("c")
    me = 4 * x + 2 * y + c
    copies = []
    for a in range(len(ins)):
        for p in range(1, N_DEV):
            px, py, pc = x ^ (p >> 2), y ^ ((p >> 1) & 1), c ^ (p & 1)
            peer = 4 * px + 2 * py + pc
            k = a * (N_DEV - 1) + p - 1
            copies.append(pltpu.make_async_remote_copy(
                src_ref=ins[a] if gather[a] else ins[a].at[peer], dst_ref=outs[a].at[me],
                send_sem=send_sems.at[k], recv_sem=recv_sems.at[k],
                device_id=(px, py, pc), device_id_type=pl.DeviceIdType.MESH))
        copies.append(pltpu.make_async_copy(ins[a] if gather[a] else ins[a].at[me], outs[a].at[me], local_sems.at[a]))
    return copies


def _riding(body, n_in, n_out, gather, grid):
    nr = len(gather)
    if not nr:
        return body

    def wrapped(*refs):
        ins, r_ins = refs[:n_in], refs[n_in:n_in + nr]
        outs, r_outs = refs[n_in + nr:n_in + nr + n_out], refs[n_in + nr + n_out:n_in + 2 * nr + n_out]
        scratch = refs[n_in + 2 * nr + n_out:]
        first = last = None
        for t, steps in enumerate(grid):
            pid = pl.program_id(t)
            first = (pid == 0) if first is None else first & (pid == 0)
            last = (pid == steps - 1) if last is None else last & (pid == steps - 1)
        copies = _exchange_copies(r_ins, r_outs, gather, *scratch[-3:])

        @pl.when(first)
        def _():
            for cp in copies:
                cp.start()

        body(*ins, *outs, *scratch[:-3])

        @pl.when(last)
        def _():
            for cp in copies:
                cp.wait()

    return wrapped


def _exchange(arrays, gather, name):
    n = len(arrays)

    def body(*refs):
        copies = _exchange_copies(refs[:n], refs[n:2 * n], gather, *refs[2 * n:])
        for cp in copies:
            cp.start()
        for cp in copies:
            cp.wait()

    return pl.pallas_call(
        body, out_shape=_exchange_shapes(arrays, gather), in_specs=[HBM_SPEC] * n, out_specs=[HBM_SPEC] * n,
        scratch_shapes=_exchange_sems(n), name=name)(*arrays)


def _sum_slots(x, name):
    _, R, C = x.shape
    tr = max(t for t in range(8, min(SUM_BLOCK_ELEMS // C, R) + 1, 8) if R % t == 0)

    def body(x_ref, o_ref):
        acc = x_ref[0].astype(F32)
        for s in range(1, N_DEV):
            acc = acc + x_ref[s].astype(F32)
        o_ref[...] = acc

    return pl.pallas_call(
        body, grid=(R // tr,), in_specs=[pl.BlockSpec((N_DEV, tr, C), lambda i: (0, i, 0))],
        out_specs=pl.BlockSpec((tr, C), lambda i: (i, 0)), out_shape=jax.ShapeDtypeStruct((R, C), F32),
        compiler_params=_cp("parallel"), name=name)(x)


def _mm(a, b, mode, tm, tn, tk, out_dtype, name, ride=None):
    if mode == NN:
        (M, K), N = a.shape, b.shape[1]
    elif mode == NT:
        (M, K), N = a.shape, b.shape[0]
    else:
        (K, M), N = a.shape, b.shape[1]
    tm, tn, tk = min(tm, M), min(tn, N), min(tk, K)
    assert M % tm == 0 and N % tn == 0 and K % tk == 0, (name, M, N, K, tm, tn, tk)
    nk = K // tk
    if mode == NN:
        a_spec = pl.BlockSpec((tm, tk), lambda i, j, k: (i, k))
        b_spec = pl.BlockSpec((tk, tn), lambda i, j, k: (k, j))
    elif mode == NT:
        a_spec = pl.BlockSpec((tm, tk), lambda i, j, k: (i, k))
        b_spec = pl.BlockSpec((tn, tk), lambda i, j, k: (j, k))
    else:
        a_spec = pl.BlockSpec((tk, tm), lambda i, j, k: (k, i))
        b_spec = pl.BlockSpec((tk, tn), lambda i, j, k: (k, j))

    ride_arrays, ride_gather = ride if ride else ([], [])
    nr = len(ride_arrays)
    grid = (M // tm, N // tn, nk)

    own_acc = nk > 1 and out_dtype != F32

    def body(a_ref, b_ref, o_ref, *acc):
        p = _dot(a_ref[...].astype(BF16), b_ref[...].astype(BF16), mode)
        if nk == 1:
            o_ref[...] = p.astype(out_dtype)
        else:
            acc_ref = acc[0] if own_acc else o_ref
            k = pl.program_id(2)

            @pl.when(k == 0)
            def _():
                acc_ref[...] = p

            @pl.when(k > 0)
            def _():
                acc_ref[...] += p

            if own_acc:
                @pl.when(k == nk - 1)
                def _():
                    o_ref[...] = acc_ref[...].astype(out_dtype)

    outs = pl.pallas_call(
        _riding(body, 2, 1, ride_gather, grid), grid=grid, in_specs=[a_spec, b_spec] + [HBM_SPEC] * nr,
        out_specs=[pl.BlockSpec((tm, tn), lambda i, j, k: (i, j))] + [HBM_SPEC] * nr,
        out_shape=[jax.ShapeDtypeStruct((M, N), out_dtype)] + _exchange_shapes(ride_arrays, ride_gather),
        scratch_shapes=([pltpu.VMEM((tm, tn), F32)] if own_acc else []) + (_exchange_sems(nr) if nr else []),
        compiler_params=_cp(*(("arbitrary",) * 3 if nr else ("parallel", "parallel", "arbitrary"))), name=name)(a, b, *ride_arrays)
    return (outs[0], outs[1:]) if nr else outs[0]


def _ada_fwd(c_all, w_shard, b_shard):
    Nc = w_shard.shape[1]

    def body(c_ref, w_ref, b_ref, o_ref):
        cv = c_ref[...]
        o_ref[...] = _dot(cv * _sigmoid(cv), w_ref[...], NN, HI) + b_ref[...]

    return pl.pallas_call(body, out_shape=jax.ShapeDtypeStruct((N_DEV, Nc), F32), name="ada_fwd",
                          compiler_params=pltpu.CompilerParams(vmem_limit_bytes=VMEM_LIMIT))(c_all, w_shard, b_shard)


def _ada_bwd(c_all, dmod_shard):
    D, Nc = c_all.shape[1], dmod_shard.shape[1]

    def body(c_ref, d_ref, o_ref):
        cv = c_ref[...]
        o_ref[...] = _dot(cv * _sigmoid(cv), d_ref[...], TN, HI)

    return pl.pallas_call(body, out_shape=jax.ShapeDtypeStruct((D, Nc), F32), name="ada_bwd",
                          compiler_params=pltpu.CompilerParams(vmem_limit_bytes=VMEM_LIMIT))(c_all, dmod_shard)


def _row_spec(ts, D):
    return pl.BlockSpec((ts, D), lambda i: (i, 0))


def _vec_spec(D):
    return pl.BlockSpec((1, D), lambda i: (0, 0))


def _col_spec(D, ts):
    return pl.BlockSpec((D, ts), lambda i: (0, i))


def _rms_mod(x, ng, sc, sh, ts, name, ride=None):
    S, D = x.shape
    ride_arrays, ride_gather = ride if ride else ([], [])
    nr = len(ride_arrays)
    grid = (S // ts,)

    def body(x_ref, ng_ref, sc_ref, sh_ref, h_ref, ht_ref):
        xv = x_ref[...]
        r = lax.rsqrt(jnp.mean(xv * xv, axis=-1, keepdims=True) + EPS)
        h = xv * r * ng_ref[...] * (1.0 + sc_ref[...]) + sh_ref[...]
        h_ref[...] = h.astype(BF16)
        ht_ref[...] = h.T.astype(BF16)

    outs = pl.pallas_call(
        _riding(body, 4, 2, ride_gather, grid), grid=grid, in_specs=[_row_spec(ts, D)] + [_vec_spec(D)] * 3 + [HBM_SPEC] * nr,
        out_specs=[_row_spec(ts, D), _col_spec(D, ts)] + [HBM_SPEC] * nr,
        out_shape=[jax.ShapeDtypeStruct((S, D), BF16), jax.ShapeDtypeStruct((D, S), BF16)] + _exchange_shapes(ride_arrays, ride_gather),
        scratch_shapes=_exchange_sems(nr) if nr else [],
        compiler_params=_cp("arbitrary"), name=name)(x, ng, sc, sh, *ride_arrays)
    return outs[0], outs[1], outs[2:]


def _mm_rows(a, b, mode, tm, extras, extra_specs, out_shapes, out_specs, epilogue, name, ride=None):
    M, K = a.shape
    grid = (M // tm,)
    ride_arrays, ride_gather = ride if ride else ([], [])
    nr = len(ride_arrays)

    def body(a_ref, b_ref, *refs):
        epilogue(_dot(a_ref[...].astype(BF16), b_ref[...].astype(BF16), mode), pl.program_id(0), *refs)

    outs = pl.pallas_call(
        _riding(body, 2 + len(extras), len(out_shapes), ride_gather, grid), grid=grid,
        in_specs=[pl.BlockSpec((tm, K), lambda i: (i, 0)), pl.BlockSpec(b.shape, lambda i: (0, 0), pipeline_mode=pl.Buffered(1))]
        + list(extra_specs) + [HBM_SPEC] * nr,
        out_specs=list(out_specs) + [HBM_SPEC] * nr,
        out_shape=list(out_shapes) + _exchange_shapes(ride_arrays, ride_gather),
        scratch_shapes=_exchange_sems(nr) if nr else [],
        compiler_params=_cp("arbitrary"), name=name)(a, b, *extras, *ride_arrays)
    return outs[:len(out_shapes)], outs[len(out_shapes):]


def _accumulate(ref, part, step):
    @pl.when(step == 0)
    def _():
        ref[...] = part

    @pl.when(step > 0)
    def _():
        ref[...] += part


def _rows8(rows, width):
    return jnp.concatenate(rows + [jnp.zeros((8 - len(rows), width), F32)], axis=0)


def _mm_resid_rms_mod(a, w, x, g, ng, sc, sh, tm, name):
    S, D = x.shape

    def epilogue(t, step, x_ref, g_ref, ng_ref, sc_ref, sh_ref, t_ref, x2_ref, h_ref, ht_ref):
        t_ref[...] = t
        xv = x_ref[...] + g_ref[...] * t
        x2_ref[...] = xv
        r = lax.rsqrt(jnp.mean(xv * xv, axis=-1, keepdims=True) + EPS)
        h = xv * r * ng_ref[...] * (1.0 + sc_ref[...]) + sh_ref[...]
        h_ref[...] = h.astype(BF16)
        ht_ref[...] = h.T.astype(BF16)

    row, vec = _row_spec(tm, D), _vec_spec(D)
    full, half = jax.ShapeDtypeStruct((S, D), F32), jax.ShapeDtypeStruct((S, D), BF16)
    outs, _ = _mm_rows(a, w, NN, tm, [x, g, ng, sc, sh], [row] + [vec] * 4,
                       [full, full, half, jax.ShapeDtypeStruct((D, S), BF16)], [row, row, row, _col_spec(D, tm)], epilogue, name)
    return outs


def _mm_loss_resid(a, w, x2, g2, target, tm, name):
    S, D = x2.shape

    def epilogue(t, step, x_ref, y_ref, g_ref, dx_ref, dt_ref, sums_ref):
        gv = g_ref[...]
        e = x_ref[...] + gv * t - y_ref[...]
        dx = e * (1.0 / D)
        dx_ref[...] = dx
        dt_ref[...] = (dx * gv).astype(BF16)
        _accumulate(sums_ref, _rows8([jnp.sum(e * e, axis=0, keepdims=True), jnp.sum(dx * t, axis=0, keepdims=True)], D), step)

    row, vec = _row_spec(tm, D), _vec_spec(D)
    outs, _ = _mm_rows(a, w, NN, tm, [x2, target, g2], [row, row, vec],
                       [jax.ShapeDtypeStruct((S, D), F32), jax.ShapeDtypeStruct((S, D), BF16), jax.ShapeDtypeStruct((8, D), F32)],
                       [row, row, pl.BlockSpec((8, D), lambda i: (0, 0))], epilogue, name)
    return outs


def _mm_rms_mod_bwd(a, w, xin, dres, ng, sc, tm, name, t_prev=None, g_prev=None, ride=None):
    S, D = xin.shape
    chain = t_prev is not None

    def epilogue(dhv, step, *refs):
        if chain:
            x_ref, dr_ref, ng_ref, sc_ref, t_ref, g_ref, dx_ref, sums_ref, dt_ref = refs
        else:
            x_ref, dr_ref, ng_ref, sc_ref, dx_ref, sums_ref = refs
        xv = x_ref[...]
        r = lax.rsqrt(jnp.mean(xv * xv, axis=-1, keepdims=True) + EPS)
        xh = xv * r
        ngv, scv = ng_ref[...], sc_ref[...]
        dxh = dhv * (ngv * (1.0 + scv))
        dx = dr_ref[...] + r * (dxh - xh * jnp.mean(dxh * xh, axis=-1, keepdims=True))
        dx_ref[...] = dx
        dhx = dhv * xh
        rows = [jnp.sum(dhv, axis=0, keepdims=True), jnp.sum(dhx * ngv, axis=0, keepdims=True),
                jnp.sum(dhx * (1.0 + scv), axis=0, keepdims=True)]
        if chain:
            dt_ref[...] = (dx * g_ref[...]).astype(BF16)
            rows.append(jnp.sum(dx * t_ref[...], axis=0, keepdims=True))
        _accumulate(sums_ref, _rows8(rows, D), step)

    row, vec = _row_spec(tm, D), _vec_spec(D)
    extras = [xin, dres, ng, sc] + ([t_prev, g_prev] if chain else [])
    extra_specs = [row, row, vec, vec] + ([row, vec] if chain else [])
    out_shapes = [jax.ShapeDtypeStruct((S, D), F32), jax.ShapeDtypeStruct((8, D), F32)] + (
        [jax.ShapeDtypeStruct((S, D), BF16)] if chain else [])
    out_specs = [row, pl.BlockSpec((8, D), lambda i: (0, 0))] + ([row] if chain else [])
    return _mm_rows(a, w, NT, tm, extras, extra_specs, out_shapes, out_specs, epilogue, name, ride=ride)


def _gate_fwd(proj, wg_p, bg, ts):
    S = proj.shape[0]

    def body(glr_ref, w_ref, b_ref, la_ref):
        z = _dot(glr_ref[...], w_ref[...], NN, HI) + b_ref[...]
        la_ref[...] = (jnp.minimum(z, 0.0) - jnp.log(1.0 + jnp.exp(-jnp.abs(z)))) * (1.0 / GLA_TAU)

    return pl.pallas_call(
        body, grid=(S // ts,),
        in_specs=[pl.BlockSpec((ts, LANE), lambda i: (i, O_GLR // LANE)), pl.BlockSpec((LANE, GLA_QK), lambda i: (0, 0)),
                  pl.BlockSpec((1, GLA_QK), lambda i: (0, 0))],
        out_specs=pl.BlockSpec((ts, GLA_QK), lambda i: (i, 0)), out_shape=jax.ShapeDtypeStruct((S, GLA_QK), F32),
        compiler_params=_cp("parallel"), name="gla_gate_fwd")(proj, wg_p, bg)


def _gate_bwd(dla, la, proj, wg_p, ts):
    S = proj.shape[0]

    def body(dla_ref, la_ref, glr_ref, w_ref, dglr_ref, gw_ref, gb_ref):
        i = pl.program_id(0)
        dz = dla_ref[...] * (1.0 / GLA_TAU) * (1.0 - jnp.exp(GLA_TAU * la_ref[...]))
        dglr_ref[...] = _dot(dz, w_ref[...], NT, HI).astype(BF16)
        gw = _dot(glr_ref[...], dz, TN, HI)
        gb = jnp.concatenate([jnp.sum(dz, axis=0, keepdims=True), jnp.zeros((7, GLA_QK), F32)], axis=0)

        @pl.when(i == 0)
        def _():
            gw_ref[...] = gw
            gb_ref[...] = gb

        @pl.when(i > 0)
        def _():
            gw_ref[...] += gw
            gb_ref[...] += gb

    return pl.pallas_call(
        body, grid=(S // ts,),
        in_specs=[pl.BlockSpec((ts, GLA_QK), lambda i: (i, 0)), pl.BlockSpec((ts, GLA_QK), lambda i: (i, 0)),
                  pl.BlockSpec((ts, LANE), lambda i: (i, O_GLR // LANE)), pl.BlockSpec((LANE, GLA_QK), lambda i: (0, 0))],
        out_specs=[pl.BlockSpec((ts, LANE), lambda i: (i, 0)), pl.BlockSpec((LANE, GLA_QK), lambda i: (0, 0)),
                   pl.BlockSpec((8, GLA_QK), lambda i: (0, 0))],
        out_shape=[jax.ShapeDtypeStruct((S, LANE), BF16), jax.ShapeDtypeStruct((LANE, GLA_QK), F32),
                   jax.ShapeDtypeStruct((8, GLA_QK), F32)],
        compiler_params=_cp("arbitrary"), name="gla_gate_bwd")(dla, la, proj, wg_p)


def _tri(lower):
    r = lax.broadcasted_iota(jnp.int32, (GLA_CHUNK, GLA_CHUNK), 0)
    c = lax.broadcasted_iota(jnp.int32, (GLA_CHUNK, GLA_CHUNK), 1)
    return jnp.where((r >= c) if lower else (c >= r), 1.0, 0.0).astype(F32)


GLA_SUB = 16
GLA_NSUB = GLA_CHUNK // GLA_SUB
PAIR_QK = 2 * GLA_DK
PAIR_V = 2 * GLA_DV


def _band_selector():
    r = lax.broadcasted_iota(jnp.int32, (GLA_SUB * PAIR_QK, LANE), 0)
    c = lax.broadcasted_iota(jnp.int32, (GLA_SUB * PAIR_QK, LANE), 1)
    dist, head = r // PAIR_QK, (r % PAIR_QK) // GLA_DK
    return jnp.where(c == head * GLA_DK + (GLA_SUB - 1 - dist), 1.0, 0.0).astype(BF16)


def _flip_matrix():
    r = lax.broadcasted_iota(jnp.int32, (GLA_CHUNK, GLA_CHUNK), 0)
    c = lax.broadcasted_iota(jnp.int32, (GLA_CHUNK, GLA_CHUNK), 1)
    return jnp.where(r + c == GLA_CHUNK - 1, 1.0, 0.0).astype(BF16)


def _state_mask():
    r = lax.broadcasted_iota(jnp.int32, (PAIR_V, PAIR_QK), 0)
    c = lax.broadcasted_iota(jnp.int32, (PAIR_V, PAIR_QK), 1)
    return (r < GLA_DV) == (c < GLA_DK)


class _GlaChunk:
    def __init__(self, qs, kc, vc, g, sel):
        C = GLA_CHUNK
        self.qs, self.kc, self.vc = qs, kc, vc
        rows = lax.broadcasted_iota(jnp.int32, (C, 1), 0)
        lane = lax.broadcasted_iota(jnp.int32, (1, PAIR_QK), 1)
        self.rows, self.lane = rows, lane
        b = _dot(_tri(True), g, NN, HI)
        self.bl = b[C - 1:C, :]
        self.eb = jnp.exp(b)
        self.kdec = jnp.exp(self.bl - b)
        edge = lambda J: b[GLA_SUB * (J + 1):GLA_SUB * (J + 1) + 1, :]
        self.e_far = [jnp.exp(jnp.where(rows >= GLA_SUB * (J + 1), b - edge(J), NEG)) for J in range(GLA_NSUB - 1)]
        blk = rows // GLA_SUB
        bnext = edge(0)
        for J in range(1, GLA_NSUB - 1):
            bnext = jnp.where(blk == J, edge(J), bnext)
        self.e_khat = jnp.exp(jnp.where(blk < GLA_NSUB - 1, bnext - b, NEG))
        khat = kc * self.e_khat
        k2 = jnp.concatenate([jnp.where(lane < GLA_DK, khat, 0.0), jnp.where(lane >= GLA_DK, khat, 0.0)], axis=0)
        self.blk2 = jnp.concatenate([blk, blk], axis=0)
        self.m_far = jnp.concatenate([jnp.where(self.blk2 == J, k2, 0.0) for J in range(GLA_NSUB - 1)], axis=1).astype(BF16)
        self.qcat = jnp.concatenate([qs * e for e in self.e_far], axis=1).astype(BF16)
        a_far = _dot(self.qcat, self.m_far, NT)
        self.e_band, self.rk, hi_terms, lo_terms = [], [], [], []
        for d in range(GLA_SUB):
            rk = pltpu.roll(kc, d, 0) if d else kc
            rb = pltpu.roll(b, d, 0) if d else b
            e = jnp.exp(jnp.where(rows >= d, b - rb, NEG))
            self.e_band.append(e)
            self.rk.append(rk)
            t = (qs * e).astype(BF16).astype(F32) * rk.astype(BF16).astype(F32)
            hi = t.astype(BF16)
            hi_terms.append(hi)
            lo_terms.append((t - hi.astype(F32)).astype(BF16))
        band = _dot(jnp.concatenate(hi_terms, axis=1), sel, NN) + _dot(jnp.concatenate(lo_terms, axis=1), sel, NN)
        a_band = pltpu.roll(band, LANE - (GLA_SUB - 1), 1, stride=1, stride_axis=0)
        dist = rows - lane % GLA_DK
        self.far_mask = dist >= GLA_SUB
        self.band_mask = (dist >= 0) & (dist < GLA_SUB)
        self.a = (a_band + jnp.where(self.far_mask, a_far, 0.0)).astype(BF16)
        self.lane_v = lax.broadcasted_iota(jnp.int32, (1, PAIR_V), 1)
        self.v2 = jnp.concatenate([jnp.where(self.lane_v < GLA_DV, vc, 0.0), jnp.where(self.lane_v >= GLA_DV, vc, 0.0)],
                                  axis=0).astype(BF16)


def _gla_fwd(proj, la, tb, ride=None):
    S = proj.shape[0]
    C = GLA_CHUNK
    tb = min(tb, S)
    nbc = tb // C
    npair = GLA_HEADS // 2
    scale = GLA_DK ** -0.5

    def body(q_ref, k_ref, v_ref, la_ref, sel_ref, o_ref, st_ref, state):
        @pl.when(pl.program_id(1) == 0)
        def _():
            state[...] = jnp.zeros_like(state)

        def chunk(ci, carry):
            sl = pl.ds(pl.multiple_of(ci * C, C), C)
            ch = _GlaChunk(q_ref[sl, :] * scale, k_ref[sl, :], v_ref[sl, :], la_ref[sl, :], sel_ref[...])
            st = state[...]
            st_ref[0, ci] = st
            o_ref[sl, :] = _dot((ch.qs * ch.eb).astype(BF16), st.astype(BF16), NT) + _dot(ch.a, ch.v2, NN)
            upd = _dot(ch.vc.astype(BF16), (ch.kc * ch.kdec).astype(BF16), TN)
            state[...] = st * jnp.exp(ch.bl) + jnp.where(_state_mask(), upd, 0.0)
            return carry

        lax.fori_loop(0, nbc, chunk, 0, unroll=8)

    qspec = lambda off: pl.BlockSpec((tb, PAIR_QK), lambda p, i: (i, off // PAIR_QK + p))
    ride_arrays, ride_gather = ride if ride else ([], [])
    nr = len(ride_arrays)
    grid = (npair, S // tb)
    outs = pl.pallas_call(
        _riding(body, 5, 2, ride_gather, grid), grid=grid,
        in_specs=[qspec(O_GQ), qspec(O_GK), pl.BlockSpec((tb, PAIR_V), lambda p, i: (i, O_GV // PAIR_V + p)),
                  pl.BlockSpec((tb, PAIR_QK), lambda p, i: (i, p)),
                  pl.BlockSpec((GLA_SUB * PAIR_QK, LANE), lambda p, i: (0, 0))] + [HBM_SPEC] * nr,
        out_specs=[pl.BlockSpec((tb, PAIR_V), lambda p, i: (i, p)),
                   pl.BlockSpec((1, nbc, PAIR_V, PAIR_QK), lambda p, i: (p, i, 0, 0))] + [HBM_SPEC] * nr,
        out_shape=[jax.ShapeDtypeStruct((S, GLA_V), F32), jax.ShapeDtypeStruct((npair, S // C, PAIR_V, PAIR_QK), F32)]
        + _exchange_shapes(ride_arrays, ride_gather),
        scratch_shapes=[pltpu.VMEM((PAIR_V, PAIR_QK), F32)] + (_exchange_sems(nr) if nr else []),
        compiler_params=_cp("arbitrary", "arbitrary"), name="gla_fwd")(proj, proj, proj, la, _band_selector(), *ride_arrays)
    return outs[0], outs[1], outs[2:]


def _gla_bwd(proj, la, do, states, tb, ride=None):
    S = proj.shape[0]
    C = GLA_CHUNK
    tb = min(tb, S)
    nbc = tb // C
    nblk = S // tb
    npair = GLA_HEADS // 2
    scale = GLA_DK ** -0.5

    def body(q_ref, k_ref, v_ref, la_ref, do_ref, st_ref, sel_ref, selt_ref, dq_ref, dk_ref, dv_ref, dla_ref, dstate):
        @pl.when(pl.program_id(1) == 0)
        def _():
            dstate[...] = jnp.zeros_like(dstate)

        def chunk(cc, carry):
            ci = nbc - 1 - cc
            sl = pl.ds(pl.multiple_of(ci * C, C), C)
            ch = _GlaChunk(q_ref[sl, :] * scale, k_ref[sl, :], v_ref[sl, :], la_ref[sl, :], sel_ref[...])
            qs, kc, rows = ch.qs, ch.kc, ch.rows
            doc_b = do_ref[sl, :].astype(BF16)
            st = st_ref[0, ci]
            dst = dstate[...]
            dst_b = dst.astype(BF16)
            ebl = jnp.exp(ch.bl)
            dq = _dot(doc_b, st.astype(BF16), NN) * ch.eb
            dk = _dot(ch.vc.astype(BF16), dst_b, NN) * ch.kdec
            dv = _dot((kc * ch.kdec).astype(BF16), dst_b, NT)
            dbl = jnp.sum(dst * st, axis=0, keepdims=True) * ebl + jnp.sum(kc * dk, axis=0, keepdims=True)
            da = _dot(doc_b, ch.v2, NT)
            dv2 = _dot(ch.a, doc_b, TN)
            dv = dv + jnp.where(ch.lane_v < GLA_DV, dv2[:C], dv2[C:])
            da_far = jnp.where(ch.far_mask, da, 0.0).astype(BF16)
            dqcat = _dot(da_far, ch.m_far, NN)
            dm = _dot(da_far, ch.qcat, TN)
            dk2 = jnp.zeros((2 * C, PAIR_QK), F32)
            for J in range(GLA_NSUB - 1):
                dq = dq + dqcat[:, J * PAIR_QK:(J + 1) * PAIR_QK] * ch.e_far[J]
                dk2 = dk2 + jnp.where(ch.blk2 == J, dm[:, J * PAIR_QK:(J + 1) * PAIR_QK], 0.0)
            dk = dk + jnp.where(ch.lane < GLA_DK, dk2[:C], dk2[C:]) * ch.e_khat
            flip = _flip_matrix()
            da_band = _dot(flip, jnp.where(ch.band_mask, da, 0.0).astype(BF16), NN)
            dband = pltpu.roll(da_band, LANE - (C - GLA_SUB), 1, stride=1, stride_axis=0)
            dband = _dot(flip, dband.astype(BF16), NN)
            dterms = _dot(dband.astype(BF16), selt_ref[...], NN)
            for d in range(GLA_SUB):
                dt = dterms[:, d * PAIR_QK:(d + 1) * PAIR_QK]
                dq = dq + dt * (ch.rk[d] * ch.e_band[d])
                dkr = dt * (qs * ch.e_band[d])
                dk = dk + (pltpu.roll(dkr, C - d, 0) if d else dkr)
            db = qs * dq - kc * dk
            db = jnp.where(rows == C - 1, db + dbl, db)
            dq_ref[sl, :] = (dq * scale).astype(BF16)
            dk_ref[sl, :] = dk.astype(BF16)
            dv_ref[sl, :] = dv.astype(BF16)
            dla_ref[sl, :] = _dot(_tri(False), db, NN, HI)
            upd = _dot(doc_b, (qs * ch.eb).astype(BF16), TN)
            dstate[...] = dst * ebl + jnp.where(_state_mask(), upd, 0.0)
            return carry

        lax.fori_loop(0, nbc, chunk, 0, unroll=8)

    rev = lambda i: nblk - 1 - i
    qspec = lambda off: pl.BlockSpec((tb, PAIR_QK), lambda p, i: (rev(i), off // PAIR_QK + p))
    pair_qk = pl.BlockSpec((tb, PAIR_QK), lambda p, i: (rev(i), p))
    pair_v = pl.BlockSpec((tb, PAIR_V), lambda p, i: (rev(i), p))
    sel = _band_selector()
    ride_arrays, ride_gather = ride if ride else ([], [])
    nr = len(ride_arrays)
    grid = (npair, nblk)
    outs = pl.pallas_call(
        _riding(body, 8, 4, ride_gather, grid), grid=grid,
        in_specs=[qspec(O_GQ), qspec(O_GK), pl.BlockSpec((tb, PAIR_V), lambda p, i: (rev(i), O_GV // PAIR_V + p)),
                  pair_qk, pair_v, pl.BlockSpec((1, nbc, PAIR_V, PAIR_QK), lambda p, i: (p, rev(i), 0, 0)),
                  pl.BlockSpec((GLA_SUB * PAIR_QK, LANE), lambda p, i: (0, 0)),
                  pl.BlockSpec((LANE, GLA_SUB * PAIR_QK), lambda p, i: (0, 0))] + [HBM_SPEC] * nr,
        out_specs=[pair_qk, pair_qk, pair_v, pair_qk] + [HBM_SPEC] * nr,
        out_shape=[jax.ShapeDtypeStruct((S, GLA_QK), BF16), jax.ShapeDtypeStruct((S, GLA_QK), BF16),
                   jax.ShapeDtypeStruct((S, GLA_V), BF16), jax.ShapeDtypeStruct((S, GLA_QK), F32)]
        + _exchange_shapes(ride_arrays, ride_gather),
        scratch_shapes=[pltpu.VMEM((PAIR_V, PAIR_QK), F32)] + (_exchange_sems(nr) if nr else []),
        compiler_params=_cp("arbitrary", "arbitrary"), name="gla_bwd")(proj, proj, proj, la, do, states, sel, sel.T, *ride_arrays)
    return outs[0], outs[1], outs[2], outs[3], outs[4:]


def _gla_out(o, proj, gng, ts):
    S = o.shape[0]

    def body(o_ref, gr_ref, g_ref, y_ref):
        for h in range(GLA_HEADS):
            cols = slice(h * GLA_DV, (h + 1) * GLA_DV)
            ov, grv = o_ref[:, cols], gr_ref[:, cols]
            r = lax.rsqrt(jnp.mean(ov * ov, axis=-1, keepdims=True) + EPS)
            y_ref[:, cols] = (ov * r * g_ref[...] * (grv * _sigmoid(grv))).astype(BF16)

    return pl.pallas_call(
        body, grid=(S // ts,),
        in_specs=[pl.BlockSpec((ts, GLA_V), lambda i: (i, 0)), pl.BlockSpec((ts, GLA_V), lambda i: (i, O_GR // GLA_V)),
                  pl.BlockSpec((1, GLA_DV), lambda i: (0, 0))],
        out_specs=pl.BlockSpec((ts, GLA_V), lambda i: (i, 0)), out_shape=jax.ShapeDtypeStruct((S, GLA_V), BF16),
        compiler_params=_cp("parallel"), name="gla_out_fwd")(o, proj, gng)


def _mm_mixed_bwd(dt1, w_out, o, proj, gng, y_att, tm):
    S = o.shape[0]
    W = ATTN_DIM

    def epilogue(dm, step, o_ref, gr_ref, g_ref, y_ref, do_ref, dgr_ref, gg_ref, dy_ref, de_ref):
        gsum = jnp.zeros((1, GLA_DV), F32)
        for h in range(GLA_HEADS):
            cols = slice(h * GLA_DV, (h + 1) * GLA_DV)
            ov, grv, dy = o_ref[:, cols], gr_ref[:, cols], dm[:, cols]
            r = lax.rsqrt(jnp.mean(ov * ov, axis=-1, keepdims=True) + EPS)
            oh = ov * r
            sg = _sigmoid(grv)
            don = dy * (grv * sg)
            dgr_ref[:, cols] = (dy * (oh * g_ref[...]) * (sg * (1.0 + grv * (1.0 - sg)))).astype(BF16)
            gsum = gsum + jnp.sum(don * oh, axis=0, keepdims=True)
            doh = don * g_ref[...]
            do_ref[:, cols] = r * (doh - oh * jnp.mean(doh * oh, axis=-1, keepdims=True))
        _accumulate(gg_ref, _rows8([gsum], GLA_DV), step)
        dya = dm[:, GLA_V:]
        dy_ref[...] = dya
        de_ref[...] = _seg_sum(dya * y_ref[...], _seg_matrix(W, ATTN_HD, 1.0))

    half = pl.BlockSpec((tm, GLA_V), lambda i: (i, 0))
    outs, _ = _mm_rows(
        dt1, w_out, NT, tm, [o, proj, gng, y_att],
        [half, pl.BlockSpec((tm, GLA_V), lambda i: (i, O_GR // GLA_V)), pl.BlockSpec((1, GLA_DV), lambda i: (0, 0)), half],
        [jax.ShapeDtypeStruct((S, GLA_V), F32), jax.ShapeDtypeStruct((S, GLA_V), BF16), jax.ShapeDtypeStruct((8, GLA_DV), F32),
         jax.ShapeDtypeStruct((S, W), F32), jax.ShapeDtypeStruct((S, W), F32)],
        [half, half, pl.BlockSpec((8, GLA_DV), lambda i: (0, 0)), half, half], epilogue, "mm_dmixed")
    return outs


def _seg_matrix(width, seg, value):
    r = lax.broadcasted_iota(jnp.int32, (width, width), 0) // seg
    c = lax.broadcasted_iota(jnp.int32, (width, width), 1) // seg
    return jnp.where(r == c, value, 0.0).astype(BF16)


def _seg_sum(x, seg_matrix):
    hi = x.astype(BF16)
    lo = (x - hi.astype(F32)).astype(BF16)
    return _dot(hi, seg_matrix, NN) + _dot(lo, seg_matrix, NN)


def _head_norm(proj, qg, kg, ts):
    S = proj.shape[0]
    W = ATTN_DIM

    def body(q_ref, k_ref, qg_ref, kg_ref, qn_ref, kn_ref):
        seg = _seg_matrix(W, ATTN_HD, 1.0 / ATTN_HD)
        for x_ref, g_ref, o_ref, scale in ((q_ref, qg_ref, qn_ref, ATTN_HD ** -0.5), (k_ref, kg_ref, kn_ref, 1.0)):
            xv = x_ref[...]
            ms = _seg_sum(xv * xv, seg)
            o_ref[...] = xv * lax.rsqrt(ms + EPS) * (g_ref[...] * scale)

    blk = lambda off: pl.BlockSpec((ts, W), lambda i: (i, off // W))
    out = pl.BlockSpec((ts, W), lambda i: (i, 0))
    vec = pl.BlockSpec((1, W), lambda i: (0, 0))
    return pl.pallas_call(
        body, grid=(S // ts,), in_specs=[blk(O_AQ), blk(O_AK), vec, vec], out_specs=[out] * 2,
        out_shape=[jax.ShapeDtypeStruct((S, W), F32)] * 2, compiler_params=_cp("parallel"), name="attn_head_norm")(
            proj, proj, qg, kg)


def _slope(head):
    one = jnp.ones((1, 1), jnp.int32)
    return 1.0 / jnp.left_shift(one, one * (head + 1)).astype(F32)


ATTN_GROUP = 4


ATTN_TILE = max(DILATIONS) * ATTN_BLOCK


def _attn_rows(d, g, r, base=0):
    start = base + (g * d * ATTN_BLOCK if g >= 0 else ATTN_TILE - d * ATTN_BLOCK) + r
    return pl.ds(start, ATTN_BLOCK) if d == 1 else pl.ds(start, ATTN_BLOCK, stride=d)


def _for_blocks(d, G, fn):
    for g in range(G):
        if d <= ATTN_GROUP:
            for r in range(d):
                fn(g, r)
        else:
            def step(r, carry, g=g):
                fn(g, r)
                return carry
            lax.fori_loop(0, d, step, 0, unroll=ATTN_GROUP)


def _attn_specs(S):
    nb = S // ATTN_TILE

    def specs(off=0):
        return [pl.BlockSpec((ATTN_TILE, LANE), lambda hp, n: (n, off + hp)),
                pl.BlockSpec((ATTN_TILE, LANE), lambda hp, n: (jnp.maximum(n - 1, 0), off + hp)),
                pl.BlockSpec((ATTN_TILE, LANE), lambda hp, n: (jnp.minimum(n + 1, nb - 1), off + hp))]

    return nb, specs


def _attn_bias(d, hp, first_tile):
    B = ATTN_BLOCK
    iq = lax.broadcasted_iota(jnp.int32, (B, 2 * B), 0)
    ik = lax.broadcasted_iota(jnp.int32, (B, 2 * B), 1)
    rel = iq + B - ik
    window = (rel >= 0) & (rel <= B)
    relf = (d * rel).astype(F32)
    full = [jnp.where(window, -_slope(hp * 2 + h) * relf, NEG) for h in range(2)]
    edge = [jnp.where((ik >= B) | jnp.logical_not(first_tile), b, NEG) for b in full]
    return full, edge


def _attn_bias_t(d, hp, has_next):
    B = ATTN_BLOCK
    ik = lax.broadcasted_iota(jnp.int32, (B, B), 0)
    iq = lax.broadcasted_iota(jnp.int32, (B, B), 1)
    tiles = []
    for nxt in range(2):
        rel = iq - ik + nxt * B
        window = (rel >= 0) & (rel <= B)
        relf = (d * rel).astype(F32)
        tiles.append([jnp.where(window, -_slope(hp * 2 + h) * relf, NEG) for h in range(2)])
    tiles.append([jnp.where(has_next, b, NEG) for b in tiles[1]])
    return tiles


def _attn_fwd(qn, kn, proj):
    S, W = qn.shape
    T = ATTN_TILE
    nb, specs = _attn_specs(S)

    def body(q_ref, kp_ref, kc_ref, vp_ref, vc_ref, y_ref, l_ref, o_scr, l_scr):
        hp, n = pl.program_id(0), pl.program_id(1)
        lo = lax.broadcasted_iota(jnp.int32, (1, LANE), 1) < ATTN_HD
        for b, d in enumerate(DILATIONS):
            full, edge = _attn_bias(d, hp, n == 0)

            def sub(g, r, b=b, d=d, full=full, edge=edge):
                rows, before = _attn_rows(d, g, r), _attn_rows(d, g - 1, r)
                kb_ref, vb_ref = (kp_ref, vp_ref) if g == 0 else (kc_ref, vc_ref)
                bias = edge if g == 0 else full
                qv = q_ref[rows, :].astype(BF16)
                kv = jnp.concatenate([kb_ref[before, :], kc_ref[rows, :]], axis=0).astype(BF16)
                vv = jnp.concatenate([vb_ref[before, :], vc_ref[rows, :]], axis=0).astype(BF16)
                outs, lses = [], []
                for h in range(2):
                    qm = jnp.where(lo == (h == 0), qv, jnp.zeros_like(qv))
                    s = _dot(qm, kv, NT) + bias[h]
                    m = jnp.max(s, axis=-1, keepdims=True)
                    p = jnp.exp(s - m)
                    den = jnp.sum(p, axis=-1, keepdims=True)
                    outs.append(_dot(p.astype(BF16), vv, NN) / den)
                    lses.append(m + jnp.log(den))
                kept = _attn_rows(d, g, r, base=b * T)
                o_scr[kept, :] = jnp.where(lo, outs[0], outs[1])
                l_scr[kept, :] = jnp.where(lo, lses[0], lses[1])

            _for_blocks(d, T // (d * ATTN_BLOCK), sub)
        l1, l2, l3 = [l_scr[pl.ds(b * T, T), :] for b in range(len(DILATIONS))]
        o1, o2, o3 = [o_scr[pl.ds(b * T, T), :] for b in range(len(DILATIONS))]
        m = jnp.maximum(jnp.maximum(l1, l2), l3)
        e1, e2, e3 = jnp.exp(l1 - m), jnp.exp(l2 - m), jnp.exp(l3 - m)
        tot = e1 + e2 + e3
        y_ref[...] = (e1 * o1 + e2 * o2 + e3 * o3) / tot
        l_ref[...] = m + jnp.log(tot)

    cur, prev, _ = specs()
    vcur, vprev, _ = specs(O_AV // LANE)
    return pl.pallas_call(
        body, grid=(W // LANE, nb), in_specs=[cur, prev, cur, vprev, vcur], out_specs=[cur, cur],
        out_shape=[jax.ShapeDtypeStruct((S, W), F32)] * 2,
        scratch_shapes=[pltpu.VMEM((len(DILATIONS) * T, LANE), F32)] * 2,
        compiler_params=_cp("parallel", "arbitrary"), name="attn_fwd")(qn, kn, kn, proj, proj)


def _attn_mix(y_gla, y_att, ts):
    S, W = y_att.shape

    def body(yg, ya, mixed_ref, mixed_t_ref):
        y = ya[...]
        mixed_ref[:, :W] = yg[...]
        mixed_ref[:, W:] = y.astype(BF16)
        mixed_t_ref[:W, :] = yg[...].astype(F32).T.astype(BF16)
        mixed_t_ref[W:, :] = y.T.astype(BF16)

    spec = pl.BlockSpec((ts, W), lambda i: (i, 0))
    return pl.pallas_call(
        body, grid=(S // ts,), in_specs=[spec] * 2,
        out_specs=[pl.BlockSpec((ts, 2 * W), lambda i: (i, 0)), _col_spec(2 * W, ts)],
        out_shape=[jax.ShapeDtypeStruct((S, 2 * W), BF16), jax.ShapeDtypeStruct((2 * W, S), BF16)],
        compiler_params=_cp("parallel"), name="attn_mix")(y_gla, y_att)


def _attn_dq(qn, kn, proj, dy, lse, delta):
    S, W = qn.shape
    nb, specs = _attn_specs(S)

    def body(q_ref, kp_ref, kc_ref, vp_ref, vc_ref, dy_ref, l_ref, de_ref, dq_ref):
        hp, n = pl.program_id(0), pl.program_id(1)
        lo = lax.broadcasted_iota(jnp.int32, (1, LANE), 1) < ATTN_HD
        for b, d in enumerate(DILATIONS):
            _attn_dq_branch(b, d, _attn_bias(d, hp, n == 0), lo, q_ref, kp_ref, kc_ref, vp_ref, vc_ref, dy_ref, l_ref, de_ref, dq_ref)

    cur, prev, _ = specs()
    vcur, vprev, _ = specs(O_AV // LANE)
    return pl.pallas_call(
        body, grid=(W // LANE, nb), in_specs=[cur, prev, cur, vprev, vcur, cur, cur, cur], out_specs=cur,
        out_shape=jax.ShapeDtypeStruct((S, W), F32),
        compiler_params=_cp("parallel", "arbitrary"), name="attn_dq")(qn, kn, kn, proj, proj, dy, lse, delta)


def _attn_dq_branch(b, d, biases, lo, q_ref, kp_ref, kc_ref, vp_ref, vc_ref, dy_ref, l_ref, de_ref, dq_ref):
    full, edge = biases

    def sub(g, r):
        rows, before = _attn_rows(d, g, r), _attn_rows(d, g - 1, r)
        kb_ref, vb_ref = (kp_ref, vp_ref) if g == 0 else (kc_ref, vc_ref)
        bias = edge if g == 0 else full
        qv, dyv = q_ref[rows, :].astype(BF16), dy_ref[rows, :]
        lv, dev = l_ref[rows, :], de_ref[rows, :]
        kv = jnp.concatenate([kb_ref[before, :], kc_ref[rows, :]], axis=0).astype(BF16)
        vv = jnp.concatenate([vb_ref[before, :], vc_ref[rows, :]], axis=0).astype(BF16)
        outs = []
        for h in range(2):
            sel = lo == (h == 0)
            qm = jnp.where(sel, qv, jnp.zeros_like(qv))
            dym = jnp.where(sel, dyv, 0.0).astype(BF16)
            lse_h = lv[:, h * ATTN_HD:h * ATTN_HD + 1]
            del_h = dev[:, h * ATTN_HD:h * ATTN_HD + 1]
            p = jnp.exp(_dot(qm, kv, NT) + bias[h] - lse_h)
            ds = p * (_dot(dym, vv, NT) - del_h)
            outs.append(_dot(ds.astype(BF16), kv, NN) * (ATTN_HD ** -0.5))
        dq = jnp.where(lo, outs[0], outs[1])
        dq_ref[rows, :] = dq if b == 0 else dq_ref[rows, :] + dq

    _for_blocks(d, ATTN_TILE // (d * ATTN_BLOCK), sub)


def _attn_dkv(qn, kn, proj, dy, lse, delta):
    S, W = qn.shape
    nb, specs = _attn_specs(S)

    def body(k_ref, v_ref, qc_ref, qn_ref, dyc_ref, dyn_ref, lc_ref, ln_ref, dec_ref, den_ref, dk_ref, dv_ref):
        hp, n = pl.program_id(0), pl.program_id(1)
        lo = lax.broadcasted_iota(jnp.int32, (1, LANE), 1) < ATTN_HD
        cur_refs, next_refs = (qc_ref, dyc_ref, lc_ref, dec_ref), (qn_ref, dyn_ref, ln_ref, den_ref)
        for b, d in enumerate(DILATIONS):
            _attn_dkv_branch(b, d, _attn_bias_t(d, hp, n + 1 < nb), lo, k_ref, v_ref, cur_refs, next_refs, dk_ref, dv_ref)

    cur, _, nxt = specs()
    vcur, _, _ = specs(O_AV // LANE)
    return pl.pallas_call(
        body, grid=(W // LANE, nb), in_specs=[cur, vcur, cur, nxt, cur, nxt, cur, nxt, cur, nxt], out_specs=[cur, cur],
        out_shape=[jax.ShapeDtypeStruct((S, W), F32)] * 2,
        compiler_params=_cp("parallel", "arbitrary"), name="attn_dkv")(
            kn, proj, qn, qn, dy, dy, lse, lse, delta, delta)


def _attn_dkv_branch(b, d, biases, lo, k_ref, v_ref, cur_refs, next_refs, dk_ref, dv_ref):
    B = ATTN_BLOCK
    own, inner, outer = biases
    G = ATTN_TILE // (d * B)

    def sub(g, r):
        rows = _attn_rows(d, g, r)
        kv, vv = k_ref[rows, :].astype(BF16), v_ref[rows, :].astype(BF16)
        dk = jnp.zeros((B, LANE), F32)
        dv = jnp.zeros((B, LANE), F32)
        inside = g + 1 < G
        after = _attn_rows(d, g + 1 if inside else 0, r)
        for bias, qrows, (q_ref, dy_ref, l_ref, de_ref) in (
                (own, rows, cur_refs), (inner if inside else outer, after, cur_refs if inside else next_refs)):
            qv, dyv = q_ref[qrows, :].astype(BF16), dy_ref[qrows, :]
            lt, det = l_ref[qrows, :].T, de_ref[qrows, :].T
            for h in range(2):
                sel = lo == (h == 0)
                qm = jnp.where(sel, qv, jnp.zeros_like(qv))
                dym = jnp.where(sel, dyv, 0.0).astype(BF16)
                lse_h = lt[h * ATTN_HD:h * ATTN_HD + 1, :]
                del_h = det[h * ATTN_HD:h * ATTN_HD + 1, :]
                pt = jnp.exp(_dot(kv, qm, NT) + bias[h] - lse_h)
                dv = dv + _dot(pt.astype(BF16), dym, NN)
                dst = pt * (_dot(vv, dym, NT) - del_h)
                dk = dk + _dot(dst.astype(BF16), qm, NN)
        dk_ref[rows, :] = dk if b == 0 else dk_ref[rows, :] + dk
        dv_ref[rows, :] = dv if b == 0 else dv_ref[rows, :] + dv

    _for_blocks(d, G, sub)


def _attn_post(dq, dk, dv, proj, qg, kg, ts):
    S = proj.shape[0]
    W = ATTN_DIM

    def body(dq_ref, dk_ref, dv_ref, aq_ref, ak_ref, qg_ref, kg_ref, daq_ref, dak_ref, dav_ref, gg_ref):
        i = pl.program_id(0)
        seg = _seg_matrix(W, ATTN_HD, 1.0 / ATTN_HD)
        gsums = []
        for d_ref, x_ref, g_ref, o_ref in ((dq_ref, aq_ref, qg_ref, daq_ref), (dk_ref, ak_ref, kg_ref, dak_ref)):
            dy = d_ref[...]
            xv = x_ref[...]
            r = lax.rsqrt(_seg_sum(xv * xv, seg) + EPS)
            xh = xv * r
            dxh = dy * g_ref[...]
            o_ref[...] = (r * (dxh - xh * _seg_sum(dxh * xh, seg))).astype(BF16)
            gsums.append(jnp.sum(dy * xh, axis=0, keepdims=True))
        dav_ref[...] = dv_ref[...].astype(BF16)
        _accumulate(gg_ref, _rows8(gsums, W), i)

    row = pl.BlockSpec((ts, W), lambda i: (i, 0))
    blk = lambda off: pl.BlockSpec((ts, W), lambda i: (i, off // W))
    vec = pl.BlockSpec((1, W), lambda i: (0, 0))
    return pl.pallas_call(
        body, grid=(S // ts,), in_specs=[row] * 3 + [blk(O_AQ), blk(O_AK), vec, vec],
        out_specs=[row, row, row, pl.BlockSpec((8, W), lambda i: (0, 0))],
        out_shape=[jax.ShapeDtypeStruct((S, W), BF16)] * 3 + [jax.ShapeDtypeStruct((8, W), F32)],
        compiler_params=_cp("arbitrary"), name="attn_post")(dq, dk, dv, proj, proj, qg, kg)


def _shift_down(cur, halo, n):
    return pltpu.roll(jnp.concatenate([halo, cur], axis=0), n, 0)[8:]


def _shift_up(cur, halo, n):
    ts = cur.shape[0]
    return pltpu.roll(jnp.concatenate([cur, halo], axis=0), ts + 8 - n, 0)[:ts]


def _conv(cur, halo, w, b):
    return b + w[0:1, :] * _shift_down(cur, halo, 2) + w[1:2, :] * _shift_down(cur, halo, 1) + w[2:3, :] * cur


def _mm_up_swiglu(h2, w_up, conv_w8, conv_b, tm, tc, ride=None):
    S, D = h2.shape
    F = w_up.shape[1] // 2
    nc = F // tc
    grid = (S // tm, nc)
    ride_arrays, ride_gather = ride if ride else ([], [])
    nr = len(ride_arrays)

    def body(h_ref, bg_ref, bv_ref, wg_ref, wv_ref, cg_ref, cv_ref, u0_ref, a_ref, at_ref, halo):
        i, j = pl.program_id(0), pl.program_id(1)
        hv = h_ref[...]
        acts = []
        for h, (b_ref, w_ref, c_ref) in enumerate(((bg_ref, wg_ref, cg_ref), (bv_ref, wv_ref, cv_ref))):
            u = _dot(hv, b_ref[...], NN)
            u0_ref[h] = u
            acts.append(_conv(u, jnp.where(i == 0, 0.0, halo[j, h]), w_ref[...], c_ref[...]))
            halo[j, h] = u[tm - 8:, :]
        g, v = acts
        a = g * _sigmoid(g) * v
        a_ref[...] = a.astype(BF16)
        at_ref[...] = a.T.astype(BF16)

    wcol = lambda rows, off: pl.BlockSpec((rows, tc), lambda i, j: (0, j + off))
    outs = pl.pallas_call(
        _riding(body, 7, 3, ride_gather, grid), grid=grid,
        in_specs=[pl.BlockSpec((tm, D), lambda i, j: (i, 0)), wcol(D, 0), wcol(D, nc), wcol(8, 0), wcol(8, nc), wcol(1, 0), wcol(1, nc)]
        + [HBM_SPEC] * nr,
        out_specs=[pl.BlockSpec((2, tm, tc), lambda i, j: (0, i, j)), pl.BlockSpec((tm, tc), lambda i, j: (i, j)),
                   pl.BlockSpec((tc, tm), lambda i, j: (j, i))] + [HBM_SPEC] * nr,
        out_shape=[jax.ShapeDtypeStruct((2, S, F), F32), jax.ShapeDtypeStruct((S, F), BF16), jax.ShapeDtypeStruct((F, S), BF16)]
        + _exchange_shapes(ride_arrays, ride_gather),
        scratch_shapes=[pltpu.VMEM((nc, 2, 8, tc), F32)] + (_exchange_sems(nr) if nr else []),
        compiler_params=_cp("arbitrary", "arbitrary"), name="mm_up")(
            h2, w_up, w_up, conv_w8, conv_w8, conv_b, conv_b, *ride_arrays)
    return outs[0], outs[1], outs[2], outs[3:]


def _mm_da_du0(dt2, w_down, u0, conv_w8, conv_b, tm, tc, ride=None):
    _, S, F = u0.shape
    D = dt2.shape[1]
    hb = tm // 8
    nrow = S // tm
    grid = (nrow,)
    ride_arrays, ride_gather = ride if ride else ([], [])
    nr = len(ride_arrays)

    def body(dt_ref, wd_ref, ug_ref, ugh_ref, uv_ref, uvh_ref, w_ref, b_ref, o_ref, sg_ref, sv_ref, following):
        i = pl.program_id(0)
        at_start, at_end = i == nrow - 1, i == 0
        dt = dt_ref[...]
        for c in range(F // tc):
            sums = []
            halves = []
            for h, (u_ref, h_ref) in enumerate(((ug_ref, ugh_ref), (uv_ref, uvh_ref))):
                cols = slice(h * F + c * tc, h * F + (c + 1) * tc)
                u, halo, w = u_ref[:, c * tc:(c + 1) * tc], jnp.where(at_start, 0.0, h_ref[:, c * tc:(c + 1) * tc]), w_ref[:, cols]
                s2, s1 = _shift_down(u, halo, 2), _shift_down(u, halo, 1)
                halves.append((b_ref[:, cols] + w[0:1, :] * s2 + w[1:2, :] * s1 + w[2:3, :] * u, s2, s1, u, w, cols))
            g, v = halves[0][0], halves[1][0]
            dav = _dot(dt, wd_ref[c * tc:(c + 1) * tc, :], NT)
            sig = _sigmoid(g)
            dus = (dav * v * (sig * (1.0 + g * (1.0 - sig))), dav * (g * sig))
            for h, du in enumerate(dus):
                _, s2, s1, u, w, cols = halves[h]
                after = jnp.where(at_end, 0.0, following[h, :, c * tc:(c + 1) * tc])
                o_ref[:, cols] = (w[2:3, :] * du + w[1:2, :] * _shift_up(du, after, 1) + w[0:1, :] * _shift_up(du, after, 2)).astype(BF16)
                following[h, :, c * tc:(c + 1) * tc] = du[0:8, :]
                sums.append(_rows8([jnp.sum(du * s2, axis=0, keepdims=True), jnp.sum(du * s1, axis=0, keepdims=True),
                                    jnp.sum(du * u, axis=0, keepdims=True), jnp.sum(du, axis=0, keepdims=True)], tc))
            for sums_ref, part in zip((sg_ref, sv_ref), sums):
                @pl.when(i == 0)
                def _(sums_ref=sums_ref, part=part, c=c):
                    sums_ref[:, c * tc:(c + 1) * tc] = part

                @pl.when(i > 0)
                def _(sums_ref=sums_ref, part=part, c=c):
                    sums_ref[:, c * tc:(c + 1) * tc] += part

    rev = lambda i: nrow - 1 - i
    main = lambda h: pl.BlockSpec((None, tm, F), lambda i: (h, rev(i), 0))
    halo = lambda h: pl.BlockSpec((None, 8, F), lambda i: (h, jnp.maximum(rev(i) * hb - 1, 0), 0))
    whole = lambda a: pl.BlockSpec(a.shape, lambda i: (0,) * a.ndim, pipeline_mode=pl.Buffered(1))
    sums_spec = pl.BlockSpec((8, F), lambda i: (0, 0))
    outs = pl.pallas_call(
        _riding(body, 8, 3, ride_gather, grid), grid=grid,
        in_specs=[pl.BlockSpec((tm, D), lambda i: (rev(i), 0)), whole(w_down), main(0), halo(0), main(1), halo(1),
                  whole(conv_w8), whole(conv_b)] + [HBM_SPEC] * nr,
        out_specs=[pl.BlockSpec((tm, 2 * F), lambda i: (rev(i), 0)), sums_spec, sums_spec] + [HBM_SPEC] * nr,
        out_shape=[jax.ShapeDtypeStruct((S, 2 * F), BF16), jax.ShapeDtypeStruct((8, F), F32), jax.ShapeDtypeStruct((8, F), F32)]
        + _exchange_shapes(ride_arrays, ride_gather),
        scratch_shapes=[pltpu.VMEM((2, 8, F), F32)] + (_exchange_sems(nr) if nr else []),
        compiler_params=_cp("arbitrary"), name="mm_da")(
            dt2, w_down, u0, u0, u0, u0, conv_w8, conv_b, *ride_arrays)
    return outs[0], outs[1], outs[2], outs[3:]


def _adamw(w, g, m, v, name, slots=False):
    shape = w.shape
    view = (math.prod(shape[:-1]), shape[-1])
    R, C = view
    limit = SUM_BLOCK_ELEMS // 2 if slots else SUM_BLOCK_ELEMS
    fits = [t for t in range(16, R + 1, 16) if R % t == 0 and t * C <= limit]
    tr = max(fits) if fits else R

    def body(w_ref, g_ref, m_ref, v_ref, *outs):
        if slots:
            gv = g_ref[0].astype(F32)
            for s in range(1, N_DEV):
                gv = gv + g_ref[s].astype(F32)
            outs[0][...] = gv
        else:
            gv = g_ref[...]
        d_ref, nm_ref, nv_ref = outs[-3:]
        nm = ADAM_B1 * m_ref[...] + (1.0 - ADAM_B1) * gv
        nv = ADAM_B2 * v_ref[...] + (1.0 - ADAM_B2) * (gv * gv)
        m_hat = nm / (1.0 - ADAM_B1 ** ADAM_STEP)
        v_hat = nv / (1.0 - ADAM_B2 ** ADAM_STEP)
        d_ref[...] = -ADAM_LR * (m_hat / (jnp.sqrt(v_hat) + ADAM_EPS) + ADAM_WD * w_ref[...])
        nm_ref[...] = nm
        nv_ref[...] = nv

    spec = pl.BlockSpec((tr, C), lambda i: (i, 0))
    g_spec = pl.BlockSpec((N_DEV, tr, C), lambda i: (0, i, 0)) if slots else spec
    n_out = 4 if slots else 3
    outs = pl.pallas_call(
        body, grid=(R // tr,), in_specs=[spec, g_spec, spec, spec], out_specs=[spec] * n_out,
        out_shape=[jax.ShapeDtypeStruct(view, F32)] * n_out, compiler_params=_cp("parallel"), name=name)(
            w.reshape(view), g if slots else g.reshape(view), m.reshape(view), v.reshape(view))
    outs = [o.reshape(shape) for o in outs]
    return outs if slots else [g.reshape(shape)] + outs


def _pad_rows8(a):
    return jnp.concatenate([a, jnp.zeros((8 - a.shape[0], a.shape[1]), a.dtype)], axis=0)


def _local_step(x, target, mod, n1g, w_in_s, conv_w_s, wg_s, bg, gng, qng, kng, w_out_s, n2g, w_up_s, conv_b, w_down_s):
    S, D = x.shape
    F = w_down_s.shape[0] * N_DEV
    cw_c, wg_c = conv_w_s.shape[1], wg_s.shape[1]
    ts = min(512, S)
    sh1, sc1, g1, sh2, sc2, g2 = [mod[i:i + 1] for i in range(6)]
    qg_t, kg_t = jnp.tile(qng, (1, ATTN_HEADS)), jnp.tile(kng, (1, ATTN_HEADS))

    small = jnp.concatenate([conv_w_s.reshape(1, -1), wg_s.reshape(1, -1)], axis=1)
    n_small = small.shape[1]
    small = jnp.pad(small, ((0, 0), (0, -n_small % LANE)))
    h1, h1_t, (g_in, g_small) = _rms_mod(x, n1g, sc1, sh1, ts, "rms_mod1", ride=([w_in_s, small], [True, True]))
    w_in_full = _cols_from_blocks(g_in)
    w_in_p = jnp.concatenate([w_in_full[:, :GLR_SRC], w_in_full[:, GLR_SRC + GLA_RANK:],
                              w_in_full[:, GLR_SRC:GLR_SRC + GLA_RANK], jnp.zeros((D, PROJ_W - O_GLR - GLA_RANK), BF16)], axis=1)
    g_small = g_small.reshape(N_DEV, -1)
    conv_w8 = _pad_rows8(jnp.stack([g_small[:, t * cw_c:(t + 1) * cw_c].reshape(-1) for t in range(3)]))
    wg_full = _cols_from_blocks(g_small[:, 3 * cw_c:n_small].reshape(N_DEV, GLA_RANK, wg_c))
    wg_p = jnp.concatenate([wg_full, jnp.zeros((LANE - GLA_RANK, wg_full.shape[1]), F32)], axis=0)
    proj, (g_out,) = _mm(h1, w_in_p, NN, 512, PROJ_W, 1024, F32, "mm_in", ride=([w_out_s], [True]))
    w_out = g_out.reshape(-1, D)
    la = _gate_fwd(proj, wg_p, bg, ts)
    o_gla, states, (g_up,) = _gla_fwd(proj, la, 512, ride=([w_up_s], [True]))
    w_up = _cols_from_blocks(g_up)
    y_gla = _gla_out(o_gla, proj, gng, ts)
    qn, kn = _head_norm(proj, qg_t, kg_t, ts)
    y_att, lse = _attn_fwd(qn, kn, proj)
    mixed, mixed_t = _attn_mix(y_gla, y_att, ts)
    t1, x2, h2, h2_t = _mm_resid_rms_mod(mixed, w_out, x, g1, n2g, sc2, sh2, ts, "mm_out")
    tc = 1408 if F % 1408 == 0 else F
    u0, a, a_t, (g_down,) = _mm_up_swiglu(h2, w_up, conv_w8, conv_b, ts, tc, ride=([w_down_s], [True]))
    w_down = g_down.reshape(F, D)
    dx3, dt2, sums3 = _mm_loss_resid(a, w_down, x2, g2, target, ts, "mm_down")
    loss_row, dg2 = sums3[0:1], sums3[1:2]

    g_w_down = _mm(a_t, dt2, NN, 1408, 1024, 2048, F32, "mm_gw_down")
    du0, sums_g, sums_v, (r_down,) = _mm_da_du0(dt2, w_down, u0, conv_w8, conv_b, min(256, S), tc,
                                                ride=([g_w_down.reshape(N_DEV, -1, D)], [False]))
    g_conv_w = jnp.concatenate([sums_g[0:3], sums_v[0:3]], axis=1)
    g_conv_b = jnp.concatenate([sums_g[3:4], sums_v[3:4]], axis=1)
    g_w_up = _mm(h2_t, du0, NN, 512, 2816, 2048, F32, "mm_gw_up")
    (dx2, sums2, dt1), _ = _mm_rms_mod_bwd(du0, w_up, x2, dx3, n2g, sc2, ts, "mm_dh2", t_prev=t1, g_prev=g1)
    dsh2, dsc2, g_n2g, dg1 = sums2[0:1], sums2[1:2], sums2[2:3], sums2[3:4]
    g_w_out = _mm(mixed_t, dt1, NN, 1024, 1024, 2048, F32, "mm_gw_out")
    do_gla, dgr, gng_sums, dy_att, delta = _mm_mixed_bwd(dt1, w_out, o_gla, proj, gng, y_att, ts)
    dgq, dgk, dgv, dla, (r_up, r_out) = _gla_bwd(
        proj, la, do_gla, states, 512, ride=([_col_blocks(g_w_up), g_w_out.reshape(N_DEV, -1, D)], [False, False]))
    dglr, g_wg_p, gb_sums = _gate_bwd(dla, la, proj, wg_p, ts)
    dqn = _attn_dq(qn, kn, proj, dy_att, lse, delta)
    dkn, dvn = _attn_dkv(qn, kn, proj, dy_att, lse, delta)
    daq, dak, dav, qk_sums = _attn_post(dqn, dkn, dvn, proj, qg_t, kg_t, ts)
    dproj = jnp.concatenate([dgq, dgk, dgv, dgr, daq, dak, dav, dglr, jnp.zeros((S, PROJ_W - O_GLR - LANE), BF16)], axis=1)
    g_w_in_p = _mm(h1_t, dproj, NN, 512, PROJ_W, 1024, F32, "mm_gw_in")
    g_w_in = jnp.concatenate([g_w_in_p[:, :GLR_SRC], g_w_in_p[:, O_GLR:O_GLR + GLA_RANK], g_w_in_p[:, GLR_SRC:O_GLR]], axis=1)
    (dx, sums1), (r_in,) = _mm_rms_mod_bwd(dproj, w_in_p, x, dx2, n1g, sc1, ts, "mm_dh1",
                                           ride=([_col_blocks(g_w_in).astype(BF16)], [False]))
    dsh1, dsc1, g_n1g = sums1[0:1], sums1[1:2], sums1[2:3]

    dmod = jnp.concatenate([dsh1, dsc1, dg1, dsh2, dsc2, dg2], axis=1)
    grads = dict(n1g=g_n1g, w_in=r_in, wg=g_wg_p[:GLA_RANK], bg=gb_sums[0:1], gng=gng_sums[0:1],
                 qng_lanes=qk_sums[0:1], kng_lanes=qk_sums[1:2], w_out=r_out, n2g=g_n2g, w_up=r_up,
                 conv_w=g_conv_w, conv_b=g_conv_b, w_down=r_down)
    return loss_row, dx, dmod, grads


def _col_blocks(a):
    R, W = a.shape
    return a.reshape(R, N_DEV, W // N_DEV).transpose(1, 0, 2)


def _cols_from_blocks(a):
    n, R, C = a.shape
    return a.transpose(1, 0, 2).reshape(R, n * C)


def kernel(x, c, w_ada, b_ada, norm1_g, w_in, gla_w_gate, gla_b_gate, gla_norm_g, q_norm_g, k_norm_g, w_out, norm2_g, w_up, conv_w, conv_b, w_down, loss_target, m_w_ada, m_b_ada, m_norm1_g, m_w_in, m_gla_w_gate, m_gla_b_gate, m_gla_norm_g, m_q_norm_g, m_k_norm_g, m_w_out, m_norm2_g, m_w_up, m_conv_w, m_conv_b, m_w_down, v_w_ada, v_b_ada, v_norm1_g, v_w_in, v_gla_w_gate, v_gla_b_gate, v_gla_norm_g, v_q_norm_g, v_k_norm_g, v_w_out, v_norm2_g, v_w_up, v_conv_w, v_conv_b, v_w_down):
    axes = ("x", "y", "c")
    me = 4 * lax.axis_index("x") + 2 * lax.axis_index("y") + lax.axis_index("c")
    S, D = x.shape[1], x.shape[2]
    x2d, tgt2d = x[0], loss_target[0]
    w_in_s, w_out_s, w_up_s, w_down_s, w_ada_s = w_in[0], w_out[0], w_up[0], w_down[0], w_ada[0]
    conv_w_s, wg_s = conv_w[0], gla_w_gate[0]
    in_c, up_c, ada_c, wg_c, cw_c = w_in_s.shape[1], w_up_s.shape[1], w_ada_s.shape[1], wg_s.shape[1], conv_w_s.shape[1]
    F = w_down_s.shape[0] * N_DEV

    g_c, = _exchange([c], [True], "gather_c")
    c_all = g_c.reshape(N_DEV, D)

    b_shard = lax.dynamic_slice(b_ada, (0, me * ada_c), (1, ada_c))
    mod_part = _ada_fwd(c_all, w_ada_s, b_shard)
    mod_recv, = _exchange([mod_part.reshape(N_DEV, 1, ada_c)], [False], "exchange_mod")
    mod = mod_recv.reshape(6, D)

    loss_row, dx, dmod, gr = _local_step(
        x2d, tgt2d, mod, norm1_g, w_in_s.astype(BF16), conv_w_s, wg_s, gla_b_gate, gla_norm_g, q_norm_g, k_norm_g,
        w_out_s.astype(BF16), norm2_g, w_up_s.astype(BF16), conv_b, w_down_s.astype(BF16))
    loss = lax.psum(0.5 / D * jnp.sum(loss_row), axes)

    parts = [dmod, gr["n1g"], gr["bg"], gr["gng"], gr["qng_lanes"], gr["kng_lanes"], gr["n2g"], gr["conv_b"],
             gr["wg"].reshape(1, -1), gr["conv_w"].reshape(1, -1)]
    sizes = [p.shape[1] for p in parts]
    packed = jnp.concatenate(parts, axis=1)
    packed = jnp.pad(packed, ((0, 0), (0, -packed.shape[1] % (8 * LANE))))
    gathered, = _exchange([packed.reshape(8, -1)], [True], "gather_small_grads")
    gathered = gathered.reshape(N_DEV, -1)
    total = _sum_slots(gathered.reshape(N_DEV, 8, -1), "sum_small_grads").reshape(1, -1)
    offs = [0]
    for s_ in sizes:
        offs.append(offs[-1] + s_)
    t_dmod, t_n1g, t_bg, t_gng, t_qng, t_kng, t_n2g, t_conv_b, t_wg, t_conv_w = [
        total[:, offs[i]:offs[i + 1]] for i in range(len(sizes))]
    g_b_ada = t_dmod
    g_qng = t_qng.reshape(ATTN_HEADS, ATTN_HD).sum(axis=0, keepdims=True)
    g_kng = t_kng.reshape(ATTN_HEADS, ATTN_HD).sum(axis=0, keepdims=True)
    g_wg = lax.dynamic_slice(t_wg.reshape(GLA_RANK, -1), (0, me * wg_c), (GLA_RANK, wg_c))
    g_conv_w = lax.dynamic_slice(t_conv_w.reshape(3, -1), (0, me * cw_c), (3, cw_c))
    dmod_shard = lax.dynamic_slice(gathered[:, :6 * D], (0, me * ada_c), (N_DEV, ada_c))
    g_w_ada = _ada_bwd(c_all, dmod_shard)

    g_w_in, g_w_out, g_w_up, g_w_down = gr["w_in"], gr["w_out"], gr["w_up"], gr["w_down"]
    in_slots = {"w_in", "w_out", "w_up", "w_down"}
    names = ["w_ada", "b_ada", "norm1_g", "w_in", "gla_w_gate", "gla_b_gate", "gla_norm_g", "q_norm_g", "k_norm_g",
             "w_out", "norm2_g", "w_up", "conv_w", "conv_b", "w_down"]
    ws = [w_ada, b_ada, norm1_g, w_in, gla_w_gate, gla_b_gate, gla_norm_g, q_norm_g, k_norm_g, w_out, norm2_g, w_up, conv_w, conv_b, w_down]
    ms = [m_w_ada, m_b_ada, m_norm1_g, m_w_in, m_gla_w_gate, m_gla_b_gate, m_gla_norm_g, m_q_norm_g, m_k_norm_g, m_w_out, m_norm2_g, m_w_up, m_conv_w, m_conv_b, m_w_down]
    vs = [v_w_ada, v_b_ada, v_norm1_g, v_w_in, v_gla_w_gate, v_gla_b_gate, v_gla_norm_g, v_q_norm_g, v_k_norm_g, v_w_out, v_norm2_g, v_w_up, v_conv_w, v_conv_b, v_w_down]
    gs = [g_w_ada, g_b_ada, t_n1g, g_w_in, g_wg, t_bg, t_gng, g_qng, g_kng, g_w_out, t_n2g, g_w_up, g_conv_w, t_conv_b, g_w_down]
    grads, deltas, new_ms, new_vs = [], [], [], []
    for nm, w, g, m, v in zip(names, ws, gs, ms, vs):
        g_, d_, m_, v_ = _adamw(w, g, m, v, "adamw_" + nm, slots=nm in in_slots)
        grads.append(g_)
        deltas.append(d_)
        new_ms.append(m_)
        new_vs.append(v_)
    return (loss, dx.reshape(x.shape), *grads, *deltas, *new_ms, *new_vs)
```

```python
import functools
import math

import jax
import jax.numpy as jnp
from jax import lax
from jax.experimental import pallas as pl
from jax.experimental.pallas import tpu as pltpu

F32, BF16 = jnp.float32, jnp.bfloat16
HI = lax.Precision.HIGHEST
EPS = 1e-6
NEG = -1e30

N_DEV = 8
GLA_HEADS, GLA_DK, GLA_DV, GLA_RANK, GLA_TAU, GLA_CHUNK = 4, 64, 128, 16, 16.0, 64
ATTN_HEADS, ATTN_HD, ATTN_BLOCK = 8, 64, 128
DILATIONS = (1, 4, 16)
GLA_QK, GLA_V, ATTN_DIM = GLA_HEADS * GLA_DK, GLA_HEADS * GLA_DV, ATTN_HEADS * ATTN_HD
O_GQ, O_GK, O_GV, O_GR, O_AQ, O_AK, O_AV, O_GLR = 0, 256, 512, 1024, 1536, 2048, 2560, 3072
PROJ_W = 3328
LANE = 128
GLR_SRC = 2 * GLA_QK + 2 * GLA_V

ADAM_LR, ADAM_B1, ADAM_B2, ADAM_EPS, ADAM_WD, ADAM_STEP = 0.001, 0.9, 0.999, 1e-08, 0.01, 10

VMEM_LIMIT = 56 * 1024 * 1024
SUM_BLOCK_ELEMS = 256 * 1024


def _cp(*sem):
    return pltpu.CompilerParams(dimension_semantics=sem, vmem_limit_bytes=VMEM_LIMIT)


def _dot(a, b, dims, precision=None):
    return lax.dot_general(a, b, (dims, ((), ())), preferred_element_type=F32, precision=precision)


NN, NT, TN = ((1,), (0,)), ((1,), (1,)), ((0,), (0,))


def _sigmoid(z):
    return 1.0 / (1.0 + jnp.exp(-z))


HBM_SPEC = pl.BlockSpec(memory_space=pltpu.HBM)


def _exchange_shapes(arrays, gather):
    return [jax.ShapeDtypeStruct((N_DEV,) + (a.shape if g else a.shape[1:]), a.dtype) for a, g in zip(arrays, gather)]


def _exchange_sems(n):
    return [pltpu.SemaphoreType.DMA((n * (N_DEV - 1),)), pltpu.SemaphoreType.DMA((n * (N_DEV - 1),)), pltpu.SemaphoreType.DMA((n,))]


def _exchange_copies(ins, outs, gather, send_sems, recv_sems, local_sems):
    x, y, c = lax.axis_index("x"), lax.axis_index("y"), lax.axis_index("c")
    me = 4 * x + 2 * y + c
    copies = []
    for a in range(len(ins)):
        for p in range(1, N_DEV):
            px, py, pc = x ^ (p >> 2), y ^ ((p >> 1) & 1), c ^ (p & 1)
            peer = 4 * px + 2 * py + pc
            k = a * (N_DEV - 1) + p - 1
            copies.append(pltpu.make_async_remote_copy(
                src_ref=ins[a] if gather[a] else ins[a].at[peer], dst_ref=outs[a].at[me],
                send_sem=send_sems.at[k], recv_sem=recv_sems.at[k],
                device_id=(px, py, pc), device_id_type=pl.DeviceIdType.MESH))
        copies.append(pltpu.make_async_copy(ins[a] if gather[a] else ins[a].at[me], outs[a].at[me], local_sems.at[a]))
    return copies


def _riding(body, n_in, n_out, gather, grid):
    nr = len(gather)
    if not nr:
        return body

    def wrapped(*refs):
        ins, r_ins = refs[:n_in], refs[n_in:n_in + nr]
        outs, r_outs = refs[n_in + nr:n_in + nr + n_out], refs[n_in + nr + n_out:n_in + 2 * nr + n_out]
        scratch = refs[n_in + 2 * nr + n_out:]
        first = last = None
        for t, steps in enumerate(grid):
            pid = pl.program_id(t)
            first = (pid == 0) if first is None else first & (pid == 0)
            last = (pid == steps - 1) if last is None else last & (pid == steps - 1)
        copies = _exchange_copies(r_ins, r_outs, gather, *scratch[-3:])

        @pl.when(first)
        def _():
            for cp in copies:
                cp.start()

        body(*ins, *outs, *scratch[:-3])

        @pl.when(last)
        def _():
            for cp in copies:
                cp.wait()

    return wrapped


def _exchange(arrays, gather, name):
    n = len(arrays)

    def body(*refs):
        copies = _exchange_copies(refs[:n], refs[n:2 * n], gather, *refs[2 * n:])
        for cp in copies:
            cp.start()
        for cp in copies:
            cp.wait()

    return pl.pallas_call(
        body, out_shape=_exchange_shapes(arrays, gather), in_specs=[HBM_SPEC] * n, out_specs=[HBM_SPEC] * n,
        scratch_shapes=_exchange_sems(n), name=name)(*arrays)


def _sum_slots(x, name):
    _, R, C = x.shape
    tr = max(t for t in range(8, min(SUM_BLOCK_ELEMS // C, R) + 1, 8) if R % t == 0)

    def body(x_ref, o_ref):
        acc = x_ref[0].astype(F32)
        for s in range(1, N_DEV):
            acc = acc + x_ref[s].astype(F32)
        o_ref[...] = acc

    return pl.pallas_call(
        body, grid=(R // tr,), in_specs=[pl.BlockSpec((N_DEV, tr, C), lambda i: (0, i, 0))],
        out_specs=pl.BlockSpec((tr, C), lambda i: (i, 0)), out_shape=jax.ShapeDtypeStruct((R, C), F32),
        compiler_params=_cp("parallel"), name=name)(x)


def _mm(a, b, mode, tm, tn, tk, out_dtype, name, ride=None):
    if mode == NN:
        (M, K), N = a.shape, b.shape[1]
    elif mode == NT:
        (M, K), N = a.shape, b.shape[0]
    else:
        (K, M), N = a.shape, b.shape[1]
    tm, tn, tk = min(tm, M), min(tn, N), min(tk, K)
    assert M % tm == 0 and N % tn == 0 and K % tk == 0, (name, M, N, K, tm, tn, tk)
    nk = K // tk
    if mode == NN:
        a_spec = pl.BlockSpec((tm, tk), lambda i, j, k: (i, k))
        b_spec = pl.BlockSpec((tk, tn), lambda i, j, k: (k, j))
    elif mode == NT:
        a_spec = pl.BlockSpec((tm, tk), lambda i, j, k: (i, k))
        b_spec = pl.BlockSpec((tn, tk), lambda i, j, k: (j, k))
    else:
        a_spec = pl.BlockSpec((tk, tm), lambda i, j, k: (k, i))
        b_spec = pl.BlockSpec((tk, tn), lambda i, j, k: (k, j))

    ride_arrays, ride_gather = ride if ride else ([], [])
    nr = len(ride_arrays)
    grid = (M // tm, N // tn, nk)

    own_acc = nk > 1 and out_dtype != F32

    def body(a_ref, b_ref, o_ref, *acc):
        p = _dot(a_ref[...].astype(BF16), b_ref[...].astype(BF16), mode)
        if nk == 1:
            o_ref[...] = p.astype(out_dtype)
        else:
            acc_ref = acc[0] if own_acc else o_ref
            k = pl.program_id(2)

            @pl.when(k == 0)
            def _():
                acc_ref[...] = p

            @pl.when(k > 0)
            def _():
                acc_ref[...] += p

            if own_acc:
                @pl.when(k == nk - 1)
                def _():
                    o_ref[...] = acc_ref[...].astype(out_dtype)

    outs = pl.pallas_call(
        _riding(body, 2, 1, ride_gather, grid), grid=grid, in_specs=[a_spec, b_spec] + [HBM_SPEC] * nr,
        out_specs=[pl.BlockSpec((tm, tn), lambda i, j, k: (i, j))] + [HBM_SPEC] * nr,
        out_shape=[jax.ShapeDtypeStruct((M, N), out_dtype)] + _exchange_shapes(ride_arrays, ride_gather),
        scratch_shapes=([pltpu.VMEM((tm, tn), F32)] if own_acc else []) + (_exchange_sems(nr) if nr else []),
        compiler_params=_cp(*(("arbitrary",) * 3 if nr else ("parallel", "parallel", "arbitrary"))), name=name)(a, b, *ride_arrays)
    return (outs[0], outs[1:]) if nr else outs[0]


def _ada_fwd(c_all, w_shard, b_shard):
    Nc = w_shard.shape[1]

    def body(c_ref, w_ref, b_ref, o_ref):
        cv = c_ref[...]
        o_ref[...] = _dot(cv * _sigmoid(cv), w_ref[...], NN, HI) + b_ref[...]

    return pl.pallas_call(body, out_shape=jax.ShapeDtypeStruct((N_DEV, Nc), F32), name="ada_fwd",
                          compiler_params=pltpu.CompilerParams(vmem_limit_bytes=VMEM_LIMIT))(c_all, w_shard, b_shard)


def _ada_bwd(c_all, dmod_shard):
    D, Nc = c_all.shape[1], dmod_shard.shape[1]

    def body(c_ref, d_ref, o_ref):
        cv = c_ref[...]
        o_ref[...] = _dot(cv * _sigmoid(cv), d_ref[...], TN, HI)

    return pl.pallas_call(body, out_shape=jax.ShapeDtypeStruct((D, Nc), F32), name="ada_bwd",
                          compiler_params=pltpu.CompilerParams(vmem_limit_bytes=VMEM_LIMIT))(c_all, dmod_shard)


def _row_spec(ts, D):
    return pl.BlockSpec((ts, D), lambda i: (i, 0))


def _vec_spec(D):
    return pl.BlockSpec((1, D), lambda i: (0, 0))


def _col_spec(D, ts):
    return pl.BlockSpec((D, ts), lambda i: (0, i))


def _rms_mod(x, ng, sc, sh, ts, name, ride=None):
    S, D = x.shape
    ride_arrays, ride_gather = ride if ride else ([], [])
    nr = len(ride_arrays)
    grid = (S // ts,)

    def body(x_ref, ng_ref, sc_ref, sh_ref, h_ref, ht_ref):
        xv = x_ref[...]
        r = lax.rsqrt(jnp.mean(xv * xv, axis=-1, keepdims=True) + EPS)
        h = xv * r * ng_ref[...] * (1.0 + sc_ref[...]) + sh_ref[...]
        h_ref[...] = h.astype(BF16)
        ht_ref[...] = h.T.astype(BF16)

    outs = pl.pallas_call(
        _riding(body, 4, 2, ride_gather, grid), grid=grid, in_specs=[_row_spec(ts, D)] + [_vec_spec(D)] * 3 + [HBM_SPEC] * nr,
        out_specs=[_row_spec(ts, D), _col_spec(D, ts)] + [HBM_SPEC] * nr,
        out_shape=[jax.ShapeDtypeStruct((S, D), BF16), jax.ShapeDtypeStruct((D, S), BF16)] + _exchange_shapes(ride_arrays, ride_gather),
        scratch_shapes=_exchange_sems(nr) if nr else [],
        compiler_params=_cp("arbitrary"), name=name)(x, ng, sc, sh, *ride_arrays)
    return outs[0], outs[1], outs[2:]


def _mm_rows(a, b, mode, tm, extras, extra_specs, out_shapes, out_specs, epilogue, name, ride=None):
    M, K = a.shape
    grid = (M // tm,)
    ride_arrays, ride_gather = ride if ride else ([], [])
    nr = len(ride_arrays)

    def body(a_ref, b_ref, *refs):
        epilogue(_dot(a_ref[...].astype(BF16), b_ref[...].astype(BF16), mode), pl.program_id(0), *refs)

    outs = pl.pallas_call(
        _riding(body, 2 + len(extras), len(out_shapes), ride_gather, grid), grid=grid,
        in_specs=[pl.BlockSpec((tm, K), lambda i: (i, 0)), pl.BlockSpec(b.shape, lambda i: (0, 0), pipeline_mode=pl.Buffered(1))]
        + list(extra_specs) + [HBM_SPEC] * nr,
        out_specs=list(out_specs) + [HBM_SPEC] * nr,
        out_shape=list(out_shapes) + _exchange_shapes(ride_arrays, ride_gather),
        scratch_shapes=_exchange_sems(nr) if nr else [],
        compiler_params=_cp("arbitrary"), name=name)(a, b, *extras, *ride_arrays)
    return outs[:len(out_shapes)], outs[len(out_shapes):]


def _accumulate(ref, part, step):
    @pl.when(step == 0)
    def _():
        ref[...] = part

    @pl.when(step > 0)
    def _():
        ref[...] += part


def _rows8(rows, width):
    return jnp.concatenate(rows + [jnp.zeros((8 - len(rows), width), F32)], axis=0)


def _mm_resid_rms_mod(a, w, x, g, ng, sc, sh, tm, name):
    S, D = x.shape

    def epilogue(t, step, x_ref, g_ref, ng_ref, sc_ref, sh_ref, t_ref, x2_ref, h_ref, ht_ref):
        t_ref[...] = t
        xv = x_ref[...] + g_ref[...] * t
        x2_ref[...] = xv
        r = lax.rsqrt(jnp.mean(xv * xv, axis=-1, keepdims=True) + EPS)
        h = xv * r * ng_ref[...] * (1.0 + sc_ref[...]) + sh_ref[...]
        h_ref[...] = h.astype(BF16)
        ht_ref[...] = h.T.astype(BF16)

    row, vec = _row_spec(tm, D), _vec_spec(D)
    full, half = jax.ShapeDtypeStruct((S, D), F32), jax.ShapeDtypeStruct((S, D), BF16)
    outs, _ = _mm_rows(a, w, NN, tm, [x, g, ng, sc, sh], [row] + [vec] * 4,
                       [full, full, half, jax.ShapeDtypeStruct((D, S), BF16)], [row, row, row, _col_spec(D, tm)], epilogue, name)
    return outs


def _mm_loss_resid(a, w, x2, g2, target, tm, name):
    S, D = x2.shape

    def epilogue(t, step, x_ref, y_ref, g_ref, dx_ref, dt_ref, sums_ref):
        gv = g_ref[...]
        e = x_ref[...] + gv * t - y_ref[...]
        dx = e * (1.0 / D)
        dx_ref[...] = dx
        dt_ref[...] = (dx * gv).astype(BF16)
        _accumulate(sums_ref, _rows8([jnp.sum(e * e, axis=0, keepdims=True), jnp.sum(dx * t, axis=0, keepdims=True)], D), step)

    row, vec = _row_spec(tm, D), _vec_spec(D)
    outs, _ = _mm_rows(a, w, NN, tm, [x2, target, g2], [row, row, vec],
                       [jax.ShapeDtypeStruct((S, D), F32), jax.ShapeDtypeStruct((S, D), BF16), jax.ShapeDtypeStruct((8, D), F32)],
                       [row, row, pl.BlockSpec((8, D), lambda i: (0, 0))], epilogue, name)
    return outs


def _mm_rms_mod_bwd(a, w, xin, dres, ng, sc, tm, name, t_prev=None, g_prev=None, ride=None):
    S, D = xin.shape
    chain = t_prev is not None

    def epilogue(dhv, step, *refs):
        if chain:
            x_ref, dr_ref, ng_ref, sc_ref, t_ref, g_ref, dx_ref, sums_ref, dt_ref = refs
        else:
            x_ref, dr_ref, ng_ref, sc_ref, dx_ref, sums_ref = refs
        xv = x_ref[...]
        r = lax.rsqrt(jnp.mean(xv * xv, axis=-1, keepdims=True) + EPS)
        xh = xv * r
        ngv, scv = ng_ref[...], sc_ref[...]
        dxh = dhv * (ngv * (1.0 + scv))
        dx = dr_ref[...] + r * (dxh - xh * jnp.mean(dxh * xh, axis=-1, keepdims=True))
        dx_ref[...] = dx
        dhx = dhv * xh
        rows = [jnp.sum(dhv, axis=0, keepdims=True), jnp.sum(dhx * ngv, axis=0, keepdims=True),
                jnp.sum(dhx * (1.0 + scv), axis=0, keepdims=True)]
        if chain:
            dt_ref[...] = (dx * g_ref[...]).astype(BF16)
            rows.append(jnp.sum(dx * t_ref[...], axis=0, keepdims=True))
        _accumulate(sums_ref, _rows8(rows, D), step)

    row, vec = _row_spec(tm, D), _vec_spec(D)
    extras = [xin, dres, ng, sc] + ([t_prev, g_prev] if chain else [])
    extra_specs = [row, row, vec, vec] + ([row, vec] if chain else [])
    out_shapes = [jax.ShapeDtypeStruct((S, D), F32), jax.ShapeDtypeStruct((8, D), F32)] + (
        [jax.ShapeDtypeStruct((S, D), BF16)] if chain else [])
    out_specs = [row, pl.BlockSpec((8, D), lambda i: (0, 0))] + ([row] if chain else [])
    return _mm_rows(a, w, NT, tm, extras, extra_specs, out_shapes, out_specs, epilogue, name, ride=ride)


def _mm_in(h1, w_in_p, wg_p, bg, qg, kg, tm, ride=None):
    S = h1.shape[0]
    W = ATTN_DIM

    def epilogue(p, step, wg_ref, bg_ref, qg_ref, kg_ref, proj_ref, la_ref, qn_ref, kn_ref):
        proj_ref[...] = p
        z = _dot(p[:, O_GLR:O_GLR + LANE], wg_ref[...], NN, HI) + bg_ref[...]
        la_ref[...] = (jnp.minimum(z, 0.0) - jnp.log(1.0 + jnp.exp(-jnp.abs(z)))) * (1.0 / GLA_TAU)
        seg = _seg_matrix(W, ATTN_HD, 1.0 / ATTN_HD)
        for off, g_ref, o_ref, scale in ((O_AQ, qg_ref, qn_ref, ATTN_HD ** -0.5), (O_AK, kg_ref, kn_ref, 1.0)):
            xv = p[:, off:off + W]
            o_ref[...] = xv * lax.rsqrt(_seg_sum(xv * xv, seg) + EPS) * (g_ref[...] * scale)

    row = lambda w: pl.BlockSpec((tm, w), lambda i: (i, 0))
    const = lambda a: pl.BlockSpec(a.shape, lambda i: (0, 0))
    return _mm_rows(
        h1, w_in_p, NN, tm, [wg_p, bg, qg, kg], [const(wg_p), const(bg), const(qg), const(kg)],
        [jax.ShapeDtypeStruct((S, PROJ_W), F32), jax.ShapeDtypeStruct((S, GLA_QK), F32),
         jax.ShapeDtypeStruct((S, W), F32), jax.ShapeDtypeStruct((S, W), F32)],
        [row(PROJ_W), row(GLA_QK), row(W), row(W)], epilogue, "mm_in", ride=ride)


def _gate_bwd(dla, la, proj, wg_p, ts):
    S = proj.shape[0]

    def body(dla_ref, la_ref, glr_ref, w_ref, dglr_ref, gw_ref, gb_ref):
        i = pl.program_id(0)
        dz = dla_ref[...] * (1.0 / GLA_TAU) * (1.0 - jnp.exp(GLA_TAU * la_ref[...]))
        dglr_ref[...] = _dot(dz, w_ref[...], NT, HI).astype(BF16)
        gw = _dot(glr_ref[...], dz, TN, HI)
        gb = jnp.concatenate([jnp.sum(dz, axis=0, keepdims=True), jnp.zeros((7, GLA_QK), F32)], axis=0)

        @pl.when(i == 0)
        def _():
            gw_ref[...] = gw
            gb_ref[...] = gb

        @pl.when(i > 0)
        def _():
            gw_ref[...] += gw
            gb_ref[...] += gb

    return pl.pallas_call(
        body, grid=(S // ts,),
        in_specs=[pl.BlockSpec((ts, GLA_QK), lambda i: (i, 0)), pl.BlockSpec((ts, GLA_QK), lambda i: (i, 0)),
                  pl.BlockSpec((ts, LANE), lambda i: (i, O_GLR // LANE)), pl.BlockSpec((LANE, GLA_QK), lambda i: (0, 0))],
        out_specs=[pl.BlockSpec((ts, LANE), lambda i: (i, 0)), pl.BlockSpec((LANE, GLA_QK), lambda i: (0, 0)),
                   pl.BlockSpec((8, GLA_QK), lambda i: (0, 0))],
        out_shape=[jax.ShapeDtypeStruct((S, LANE), BF16), jax.ShapeDtypeStruct((LANE, GLA_QK), F32),
                   jax.ShapeDtypeStruct((8, GLA_QK), F32)],
        compiler_params=_cp("arbitrary"), name="gla_gate_bwd")(dla, la, proj, wg_p)


def _tri(lower):
    r = lax.broadcasted_iota(jnp.int32, (GLA_CHUNK, GLA_CHUNK), 0)
    c = lax.broadcasted_iota(jnp.int32, (GLA_CHUNK, GLA_CHUNK), 1)
    return jnp.where((r >= c) if lower else (c >= r), 1.0, 0.0).astype(F32)


GLA_SUB = 16
GLA_NSUB = GLA_CHUNK // GLA_SUB
PAIR_QK = 2 * GLA_DK
PAIR_V = 2 * GLA_DV


def _band_selector():
    r = lax.broadcasted_iota(jnp.int32, (GLA_SUB * PAIR_QK, LANE), 0)
    c = lax.broadcasted_iota(jnp.int32, (GLA_SUB * PAIR_QK, LANE), 1)
    dist, head = r // PAIR_QK, (r % PAIR_QK) // GLA_DK
    return jnp.where(c == head * GLA_DK + (GLA_SUB - 1 - dist), 1.0, 0.0).astype(BF16)


def _flip_matrix():
    r = lax.broadcasted_iota(jnp.int32, (GLA_CHUNK, GLA_CHUNK), 0)
    c = lax.broadcasted_iota(jnp.int32, (GLA_CHUNK, GLA_CHUNK), 1)
    return jnp.where(r + c == GLA_CHUNK - 1, 1.0, 0.0).astype(BF16)


def _state_mask():
    r = lax.broadcasted_iota(jnp.int32, (PAIR_V, PAIR_QK), 0)
    c = lax.broadcasted_iota(jnp.int32, (PAIR_V, PAIR_QK), 1)
    return (r < GLA_DV) == (c < GLA_DK)


class _GlaChunk:
    def __init__(self, qs, kc, vc, g, sel):
        C = GLA_CHUNK
        self.qs, self.kc, self.vc = qs, kc, vc
        rows = lax.broadcasted_iota(jnp.int32, (C, 1), 0)
        lane = lax.broadcasted_iota(jnp.int32, (1, PAIR_QK), 1)
        self.rows, self.lane = rows, lane
        b = _dot(_tri(True), g, NN, HI)
        self.bl = b[C - 1:C, :]
        self.eb = jnp.exp(b)
        self.kdec = jnp.exp(self.bl - b)
        edge = lambda J: b[GLA_SUB * (J + 1):GLA_SUB * (J + 1) + 1, :]
        self.e_far = [jnp.exp(jnp.where(rows >= GLA_SUB * (J + 1), b - edge(J), NEG)) for J in range(GLA_NSUB - 1)]
        blk = rows // GLA_SUB
        bnext = edge(0)
        for J in range(1, GLA_NSUB - 1):
            bnext = jnp.where(blk == J, edge(J), bnext)
        self.e_khat = jnp.exp(jnp.where(blk < GLA_NSUB - 1, bnext - b, NEG))
        khat = kc * self.e_khat
        k2 = jnp.concatenate([jnp.where(lane < GLA_DK, khat, 0.0), jnp.where(lane >= GLA_DK, khat, 0.0)], axis=0)
        self.blk2 = jnp.concatenate([blk, blk], axis=0)
        self.m_far = jnp.concatenate([jnp.where(self.blk2 == J, k2, 0.0) for J in range(GLA_NSUB - 1)], axis=1).astype(BF16)
        self.qcat = jnp.concatenate([qs * e for e in self.e_far], axis=1).astype(BF16)
        a_far = _dot(self.qcat, self.m_far, NT)
        self.e_band, self.rk, hi_terms, lo_terms = [], [], [], []
        for d in range(GLA_SUB):
            rk = pltpu.roll(kc, d, 0) if d else kc
            rb = pltpu.roll(b, d, 0) if d else b
            e = jnp.exp(jnp.where(rows >= d, b - rb, NEG))
            self.e_band.append(e)
            self.rk.append(rk)
            t = (qs * e).astype(BF16).astype(F32) * rk.astype(BF16).astype(F32)
            hi = t.astype(BF16)
            hi_terms.append(hi)
            lo_terms.append((t - hi.astype(F32)).astype(BF16))
        band = _dot(jnp.concatenate(hi_terms, axis=1), sel, NN) + _dot(jnp.concatenate(lo_terms, axis=1), sel, NN)
        a_band = pltpu.roll(band, LANE - (GLA_SUB - 1), 1, stride=1, stride_axis=0)
        dist = rows - lane % GLA_DK
        self.far_mask = dist >= GLA_SUB
        self.band_mask = (dist >= 0) & (dist < GLA_SUB)
        self.a = (a_band + jnp.where(self.far_mask, a_far, 0.0)).astype(BF16)
        self.lane_v = lax.broadcasted_iota(jnp.int32, (1, PAIR_V), 1)
        self.v2 = jnp.concatenate([jnp.where(self.lane_v < GLA_DV, vc, 0.0), jnp.where(self.lane_v >= GLA_DV, vc, 0.0)],
                                  axis=0).astype(BF16)


def _gla_fwd(proj, la, tb, ride=None):
    S = proj.shape[0]
    C = GLA_CHUNK
    tb = min(tb, S)
    nbc = tb // C
    npair = GLA_HEADS // 2
    scale = GLA_DK ** -0.5

    def body(q_ref, k_ref, v_ref, la_ref, sel_ref, o_ref, st_ref, state):
        @pl.when(pl.program_id(1) == 0)
        def _():
            state[...] = jnp.zeros_like(state)

        def chunk(ci, carry):
            sl = pl.ds(pl.multiple_of(ci * C, C), C)
            ch = _GlaChunk(q_ref[sl, :] * scale, k_ref[sl, :], v_ref[sl, :], la_ref[sl, :], sel_ref[...])
            st = state[...]
            st_ref[0, ci] = st
            o_ref[sl, :] = _dot((ch.qs * ch.eb).astype(BF16), st.astype(BF16), NT) + _dot(ch.a, ch.v2, NN)
            upd = _dot(ch.vc.astype(BF16), (ch.kc * ch.kdec).astype(BF16), TN)
            state[...] = st * jnp.exp(ch.bl) + jnp.where(_state_mask(), upd, 0.0)
            return carry

        lax.fori_loop(0, nbc, chunk, 0, unroll=8)

    qspec = lambda off: pl.BlockSpec((tb, PAIR_QK), lambda p, i: (i, off // PAIR_QK + p))
    ride_arrays, ride_gather = ride if ride else ([], [])
    nr = len(ride_arrays)
    grid = (npair, S // tb)
    outs = pl.pallas_call(
        _riding(body, 5, 2, ride_gather, grid), grid=grid,
        in_specs=[qspec(O_GQ), qspec(O_GK), pl.BlockSpec((tb, PAIR_V), lambda p, i: (i, O_GV // PAIR_V + p)),
                  pl.BlockSpec((tb, PAIR_QK), lambda p, i: (i, p)),
                  pl.BlockSpec((GLA_SUB * PAIR_QK, LANE), lambda p, i: (0, 0))] + [HBM_SPEC] * nr,
        out_specs=[pl.BlockSpec((tb, PAIR_V), lambda p, i: (i, p)),
                   pl.BlockSpec((1, nbc, PAIR_V, PAIR_QK), lambda p, i: (p, i, 0, 0))] + [HBM_SPEC] * nr,
        out_shape=[jax.ShapeDtypeStruct((S, GLA_V), F32), jax.ShapeDtypeStruct((npair, S // C, PAIR_V, PAIR_QK), F32)]
        + _exchange_shapes(ride_arrays, ride_gather),
        scratch_shapes=[pltpu.VMEM((PAIR_V, PAIR_QK), F32)] + (_exchange_sems(nr) if nr else []),
        compiler_params=_cp("arbitrary", "arbitrary"), name="gla_fwd")(proj, proj, proj, la, _band_selector(), *ride_arrays)
    return outs[0], outs[1], outs[2:]


def _gla_bwd(proj, la, do, states, tb, ride=None):
    S = proj.shape[0]
    C = GLA_CHUNK
    tb = min(tb, S)
    nbc = tb // C
    nblk = S // tb
    npair = GLA_HEADS // 2
    scale = GLA_DK ** -0.5

    def body(q_ref, k_ref, v_ref, la_ref, do_ref, st_ref, sel_ref, selt_ref, dq_ref, dk_ref, dv_ref, dla_ref, dstate):
        @pl.when(pl.program_id(1) == 0)
        def _():
            dstate[...] = jnp.zeros_like(dstate)

        def chunk(cc, carry):
            ci = nbc - 1 - cc
            sl = pl.ds(pl.multiple_of(ci * C, C), C)
            ch = _GlaChunk(q_ref[sl, :] * scale, k_ref[sl, :], v_ref[sl, :], la_ref[sl, :], sel_ref[...])
            qs, kc, rows = ch.qs, ch.kc, ch.rows
            doc_b = do_ref[sl, :].astype(BF16)
            st = st_ref[0, ci]
            dst = dstate[...]
            dst_b = dst.astype(BF16)
            ebl = jnp.exp(ch.bl)
            dq = _dot(doc_b, st.astype(BF16), NN) * ch.eb
            dk = _dot(ch.vc.astype(BF16), dst_b, NN) * ch.kdec
            dv = _dot((kc * ch.kdec).astype(BF16), dst_b, NT)
            dbl = jnp.sum(dst * st, axis=0, keepdims=True) * ebl + jnp.sum(kc * dk, axis=0, keepdims=True)
            da = _dot(doc_b, ch.v2, NT)
            dv2 = _dot(ch.a, doc_b, TN)
            dv = dv + jnp.where(ch.lane_v < GLA_DV, dv2[:C], dv2[C:])
            da_far = jnp.where(ch.far_mask, da, 0.0).astype(BF16)
            dqcat = _dot(da_far, ch.m_far, NN)
            dm = _dot(da_far, ch.qcat, TN)
            dk2 = jnp.zeros((2 * C, PAIR_QK), F32)
            for J in range(GLA_NSUB - 1):
                dq = dq + dqcat[:, J * PAIR_QK:(J + 1) * PAIR_QK] * ch.e_far[J]
                dk2 = dk2 + jnp.where(ch.blk2 == J, dm[:, J * PAIR_QK:(J + 1) * PAIR_QK], 0.0)
            dk = dk + jnp.where(ch.lane < GLA_DK, dk2[:C], dk2[C:]) * ch.e_khat
            flip = _flip_matrix()
            da_band = _dot(flip, jnp.where(ch.band_mask, da, 0.0).astype(BF16), NN)
            dband = pltpu.roll(da_band, LANE - (C - GLA_SUB), 1, stride=1, stride_axis=0)
            dband = _dot(flip, dband.astype(BF16), NN)
            dterms = _dot(dband.astype(BF16), selt_ref[...], NN)
            for d in range(GLA_SUB):
                dt = dterms[:, d * PAIR_QK:(d + 1) * PAIR_QK]
                dq = dq + dt * (ch.rk[d] * ch.e_band[d])
                dkr = dt * (qs * ch.e_band[d])
                dk = dk + (pltpu.roll(dkr, C - d, 0) if d else dkr)
            db = qs * dq - kc * dk
            db = jnp.where(rows == C - 1, db + dbl, db)
            dq_ref[sl, :] = (dq * scale).astype(BF16)
            dk_ref[sl, :] = dk.astype(BF16)
            dv_ref[sl, :] = dv.astype(BF16)
            dla_ref[sl, :] = _dot(_tri(False), db, NN, HI)
            upd = _dot(doc_b, (qs * ch.eb).astype(BF16), TN)
            dstate[...] = dst * ebl + jnp.where(_state_mask(), upd, 0.0)
            return carry

        lax.fori_loop(0, nbc, chunk, 0, unroll=8)

    rev = lambda i: nblk - 1 - i
    qspec = lambda off: pl.BlockSpec((tb, PAIR_QK), lambda p, i: (rev(i), off // PAIR_QK + p))
    pair_qk = pl.BlockSpec((tb, PAIR_QK), lambda p, i: (rev(i), p))
    pair_v = pl.BlockSpec((tb, PAIR_V), lambda p, i: (rev(i), p))
    sel = _band_selector()
    ride_arrays, ride_gather = ride if ride else ([], [])
    nr = len(ride_arrays)
    grid = (npair, nblk)
    outs = pl.pallas_call(
        _riding(body, 8, 4, ride_gather, grid), grid=grid,
        in_specs=[qspec(O_GQ), qspec(O_GK), pl.BlockSpec((tb, PAIR_V), lambda p, i: (rev(i), O_GV // PAIR_V + p)),
                  pair_qk, pair_v, pl.BlockSpec((1, nbc, PAIR_V, PAIR_QK), lambda p, i: (p, rev(i), 0, 0)),
                  pl.BlockSpec((GLA_SUB * PAIR_QK, LANE), lambda p, i: (0, 0)),
                  pl.BlockSpec((LANE, GLA_SUB * PAIR_QK), lambda p, i: (0, 0))] + [HBM_SPEC] * nr,
        out_specs=[pair_qk, pair_qk, pair_v, pair_qk] + [HBM_SPEC] * nr,
        out_shape=[jax.ShapeDtypeStruct((S, GLA_QK), BF16), jax.ShapeDtypeStruct((S, GLA_QK), BF16),
                   jax.ShapeDtypeStruct((S, GLA_V), BF16), jax.ShapeDtypeStruct((S, GLA_QK), F32)]
        + _exchange_shapes(ride_arrays, ride_gather),
        scratch_shapes=[pltpu.VMEM((PAIR_V, PAIR_QK), F32)] + (_exchange_sems(nr) if nr else []),
        compiler_params=_cp("arbitrary", "arbitrary"), name="gla_bwd")(proj, proj, proj, la, do, states, sel, sel.T, *ride_arrays)
    return outs[0], outs[1], outs[2], outs[3], outs[4:]


def _gla_out(o, proj, gng, ts):
    S = o.shape[0]

    def body(o_ref, gr_ref, g_ref, y_ref):
        for h in range(GLA_HEADS):
            cols = slice(h * GLA_DV, (h + 1) * GLA_DV)
            ov, grv = o_ref[:, cols], gr_ref[:, cols]
            r = lax.rsqrt(jnp.mean(ov * ov, axis=-1, keepdims=True) + EPS)
            y_ref[:, cols] = (ov * r * g_ref[...] * (grv * _sigmoid(grv))).astype(BF16)

    return pl.pallas_call(
        body, grid=(S // ts,),
        in_specs=[pl.BlockSpec((ts, GLA_V), lambda i: (i, 0)), pl.BlockSpec((ts, GLA_V), lambda i: (i, O_GR // GLA_V)),
                  pl.BlockSpec((1, GLA_DV), lambda i: (0, 0))],
        out_specs=pl.BlockSpec((ts, GLA_V), lambda i: (i, 0)), out_shape=jax.ShapeDtypeStruct((S, GLA_V), BF16),
        compiler_params=_cp("parallel"), name="gla_out_fwd")(o, proj, gng)


def _mm_mixed_bwd(dt1, w_out, o, proj, gng, y_att, tm):
    S = o.shape[0]
    W = ATTN_DIM

    def epilogue(dm, step, o_ref, gr_ref, g_ref, y_ref, do_ref, dgr_ref, gg_ref, dy_ref, de_ref):
        gsum = jnp.zeros((1, GLA_DV), F32)
        for h in range(GLA_HEADS):
            cols = slice(h * GLA_DV, (h + 1) * GLA_DV)
            ov, grv, dy = o_ref[:, cols], gr_ref[:, cols], dm[:, cols]
            r = lax.rsqrt(jnp.mean(ov * ov, axis=-1, keepdims=True) + EPS)
            oh = ov * r
            sg = _sigmoid(grv)
            don = dy * (grv * sg)
            dgr_ref[:, cols] = (dy * (oh * g_ref[...]) * (sg * (1.0 + grv * (1.0 - sg)))).astype(BF16)
            gsum = gsum + jnp.sum(don * oh, axis=0, keepdims=True)
            doh = don * g_ref[...]
            do_ref[:, cols] = r * (doh - oh * jnp.mean(doh * oh, axis=-1, keepdims=True))
        _accumulate(gg_ref, _rows8([gsum], GLA_DV), step)
        dya = dm[:, GLA_V:]
        dy_ref[...] = dya
        de_ref[...] = _seg_sum(dya * y_ref[...], _seg_matrix(W, ATTN_HD, 1.0))

    half = pl.BlockSpec((tm, GLA_V), lambda i: (i, 0))
    outs, _ = _mm_rows(
        dt1, w_out, NT, tm, [o, proj, gng, y_att],
        [half, pl.BlockSpec((tm, GLA_V), lambda i: (i, O_GR // GLA_V)), pl.BlockSpec((1, GLA_DV), lambda i: (0, 0)), half],
        [jax.ShapeDtypeStruct((S, GLA_V), F32), jax.ShapeDtypeStruct((S, GLA_V), BF16), jax.ShapeDtypeStruct((8, GLA_DV), F32),
         jax.ShapeDtypeStruct((S, W), F32), jax.ShapeDtypeStruct((S, W), F32)],
        [half, half, pl.BlockSpec((8, GLA_DV), lambda i: (0, 0)), half, half], epilogue, "mm_dmixed")
    return outs


def _seg_matrix(width, seg, value):
    r = lax.broadcasted_iota(jnp.int32, (width, width), 0) // seg
    c = lax.broadcasted_iota(jnp.int32, (width, width), 1) // seg
    return jnp.where(r == c, value, 0.0).astype(BF16)


def _seg_sum(x, seg_matrix):
    hi = x.astype(BF16)
    lo = (x - hi.astype(F32)).astype(BF16)
    return _dot(hi, seg_matrix, NN) + _dot(lo, seg_matrix, NN)


def _slope(head):
    one = jnp.ones((1, 1), jnp.int32)
    return 1.0 / jnp.left_shift(one, one * (head + 1)).astype(F32)


ATTN_GROUP = 4


ATTN_TILE = max(DILATIONS) * ATTN_BLOCK


def _attn_rows(d, g, r, base=0):
    start = base + (g * d * ATTN_BLOCK if g >= 0 else ATTN_TILE - d * ATTN_BLOCK) + r
    return pl.ds(start, ATTN_BLOCK) if d == 1 else pl.ds(start, ATTN_BLOCK, stride=d)


def _for_blocks(d, G, fn):
    for g in range(G):
        if d <= ATTN_GROUP:
            for r in range(d):
                fn(g, r)
        else:
            def step(r, carry, g=g):
                fn(g, r)
                return carry
            lax.fori_loop(0, d, step, 0, unroll=ATTN_GROUP)


def _attn_specs(S):
    nb = S // ATTN_TILE

    def specs(off=0):
        return [pl.BlockSpec((ATTN_TILE, LANE), lambda hp, n: (n, off + hp)),
                pl.BlockSpec((ATTN_TILE, LANE), lambda hp, n: (jnp.maximum(n - 1, 0), off + hp)),
                pl.BlockSpec((ATTN_TILE, LANE), lambda hp, n: (jnp.minimum(n + 1, nb - 1), off + hp))]

    return nb, specs


def _attn_bias(d, hp, first_tile):
    B = ATTN_BLOCK
    iq = lax.broadcasted_iota(jnp.int32, (B, 2 * B), 0)
    ik = lax.broadcasted_iota(jnp.int32, (B, 2 * B), 1)
    rel = iq + B - ik
    window = (rel >= 0) & (rel <= B)
    relf = (d * rel).astype(F32)
    full = [jnp.where(window, -_slope(hp * 2 + h) * relf, NEG) for h in range(2)]
    edge = [jnp.where((ik >= B) | jnp.logical_not(first_tile), b, NEG) for b in full]
    return full, edge


def _attn_bias_t(d, hp, has_next):
    B = ATTN_BLOCK
    ik = lax.broadcasted_iota(jnp.int32, (B, B), 0)
    iq = lax.broadcasted_iota(jnp.int32, (B, B), 1)
    tiles = []
    for nxt in range(2):
        rel = iq - ik + nxt * B
        window = (rel >= 0) & (rel <= B)
        relf = (d * rel).astype(F32)
        tiles.append([jnp.where(window, -_slope(hp * 2 + h) * relf, NEG) for h in range(2)])
    tiles.append([jnp.where(has_next, b, NEG) for b in tiles[1]])
    return tiles


def _attn_fwd(qn, kn, proj):
    S, W = qn.shape
    T = ATTN_TILE
    nb, specs = _attn_specs(S)

    def body(q_ref, kp_ref, kc_ref, vp_ref, vc_ref, y_ref, l_ref, o_scr, l_scr):
        hp, n = pl.program_id(0), pl.program_id(1)
        lo = lax.broadcasted_iota(jnp.int32, (1, LANE), 1) < ATTN_HD
        for b, d in enumerate(DILATIONS):
            full, edge = _attn_bias(d, hp, n == 0)

            def sub(g, r, b=b, d=d, full=full, edge=edge):
                rows, before = _attn_rows(d, g, r), _attn_rows(d, g - 1, r)
                kb_ref, vb_ref = (kp_ref, vp_ref) if g == 0 else (kc_ref, vc_ref)
                bias = edge if g == 0 else full
                qv = q_ref[rows, :].astype(BF16)
                kv = jnp.concatenate([kb_ref[before, :], kc_ref[rows, :]], axis=0).astype(BF16)
                vv = jnp.concatenate([vb_ref[before, :], vc_ref[rows, :]], axis=0).astype(BF16)
                outs, lses = [], []
                for h in range(2):
                    qm = jnp.where(lo == (h == 0), qv, jnp.zeros_like(qv))
                    s = _dot(qm, kv, NT) + bias[h]
                    m = jnp.max(s, axis=-1, keepdims=True)
                    p = jnp.exp(s - m)
                    den = jnp.sum(p, axis=-1, keepdims=True)
                    outs.append(_dot(p.astype(BF16), vv, NN) / den)
                    lses.append(m + jnp.log(den))
                kept = _attn_rows(d, g, r, base=b * T)
                o_scr[kept, :] = jnp.where(lo, outs[0], outs[1])
                l_scr[kept, :] = jnp.where(lo, lses[0], lses[1])

            _for_blocks(d, T // (d * ATTN_BLOCK), sub)
        l1, l2, l3 = [l_scr[pl.ds(b * T, T), :] for b in range(len(DILATIONS))]
        o1, o2, o3 = [o_scr[pl.ds(b * T, T), :] for b in range(len(DILATIONS))]
        m = jnp.maximum(jnp.maximum(l1, l2), l3)
        e1, e2, e3 = jnp.exp(l1 - m), jnp.exp(l2 - m), jnp.exp(l3 - m)
        tot = e1 + e2 + e3
        y_ref[...] = (e1 * o1 + e2 * o2 + e3 * o3) / tot
        l_ref[...] = m + jnp.log(tot)

    cur, prev, _ = specs()
    vcur, vprev, _ = specs(O_AV // LANE)
    return pl.pallas_call(
        body, grid=(W // LANE, nb), in_specs=[cur, prev, cur, vprev, vcur], out_specs=[cur, cur],
        out_shape=[jax.ShapeDtypeStruct((S, W), F32)] * 2,
        scratch_shapes=[pltpu.VMEM((len(DILATIONS) * T, LANE), F32)] * 2,
        compiler_params=_cp("parallel", "arbitrary"), name="attn_fwd")(qn, kn, kn, proj, proj)


def _attn_mix(y_gla, y_att, ts):
    S, W = y_att.shape

    def body(yg, ya, mixed_ref, mixed_t_ref):
        y = ya[...]
        mixed_ref[:, :W] = yg[...]
        mixed_ref[:, W:] = y.astype(BF16)
        mixed_t_ref[:W, :] = yg[...].astype(F32).T.astype(BF16)
        mixed_t_ref[W:, :] = y.T.astype(BF16)

    spec = pl.BlockSpec((ts, W), lambda i: (i, 0))
    return pl.pallas_call(
        body, grid=(S // ts,), in_specs=[spec] * 2,
        out_specs=[pl.BlockSpec((ts, 2 * W), lambda i: (i, 0)), _col_spec(2 * W, ts)],
        out_shape=[jax.ShapeDtypeStruct((S, 2 * W), BF16), jax.ShapeDtypeStruct((2 * W, S), BF16)],
        compiler_params=_cp("parallel"), name="attn_mix")(y_gla, y_att)


def _attn_dq(qn, kn, proj, dy, lse, delta):
    S, W = qn.shape
    nb, specs = _attn_specs(S)

    def body(q_ref, kp_ref, kc_ref, vp_ref, vc_ref, dy_ref, l_ref, de_ref, dq_ref):
        hp, n = pl.program_id(0), pl.program_id(1)
        lo = lax.broadcasted_iota(jnp.int32, (1, LANE), 1) < ATTN_HD
        for b, d in enumerate(DILATIONS):
            _attn_dq_branch(b, d, _attn_bias(d, hp, n == 0), lo, q_ref, kp_ref, kc_ref, vp_ref, vc_ref, dy_ref, l_ref, de_ref, dq_ref)

    cur, prev, _ = specs()
    vcur, vprev, _ = specs(O_AV // LANE)
    return pl.pallas_call(
        body, grid=(W // LANE, nb), in_specs=[cur, prev, cur, vprev, vcur, cur, cur, cur], out_specs=cur,
        out_shape=jax.ShapeDtypeStruct((S, W), F32),
        compiler_params=_cp("parallel", "arbitrary"), name="attn_dq")(qn, kn, kn, proj, proj, dy, lse, delta)


def _attn_dq_branch(b, d, biases, lo, q_ref, kp_ref, kc_ref, vp_ref, vc_ref, dy_ref, l_ref, de_ref, dq_ref):
    full, edge = biases

    def sub(g, r):
        rows, before = _attn_rows(d, g, r), _attn_rows(d, g - 1, r)
        kb_ref, vb_ref = (kp_ref, vp_ref) if g == 0 else (kc_ref, vc_ref)
        bias = edge if g == 0 else full
        qv, dyv = q_ref[rows, :].astype(BF16), dy_ref[rows, :]
        lv, dev = l_ref[rows, :], de_ref[rows, :]
        kv = jnp.concatenate([kb_ref[before, :], kc_ref[rows, :]], axis=0).astype(BF16)
        vv = jnp.concatenate([vb_ref[before, :], vc_ref[rows, :]], axis=0).astype(BF16)
        outs = []
        for h in range(2):
            sel = lo == (h == 0)
            qm = jnp.where(sel, qv, jnp.zeros_like(qv))
            dym = jnp.where(sel, dyv, 0.0).astype(BF16)
            lse_h = lv[:, h * ATTN_HD:h * ATTN_HD + 1]
            del_h = dev[:, h * ATTN_HD:h * ATTN_HD + 1]
            p = jnp.exp(_dot(qm, kv, NT) + bias[h] - lse_h)
            ds = p * (_dot(dym, vv, NT) - del_h)
            outs.append(_dot(ds.astype(BF16), kv, NN) * (ATTN_HD ** -0.5))
        dq = jnp.where(lo, outs[0], outs[1])
        dq_ref[rows, :] = dq if b == 0 else dq_ref[rows, :] + dq

    _for_blocks(d, ATTN_TILE // (d * ATTN_BLOCK), sub)


def _attn_dkv(qn, kn, proj, dy, lse, delta):
    S, W = qn.shape
    nb, specs = _attn_specs(S)

    def body(k_ref, v_ref, qc_ref, qn_ref, dyc_ref, dyn_ref, lc_ref, ln_ref, dec_ref, den_ref, dk_ref, dv_ref):
        hp, n = pl.program_id(0), pl.program_id(1)
        lo = lax.broadcasted_iota(jnp.int32, (1, LANE), 1) < ATTN_HD
        cur_refs, next_refs = (qc_ref, dyc_ref, lc_ref, dec_ref), (qn_ref, dyn_ref, ln_ref, den_ref)
        for b, d in enumerate(DILATIONS):
            _attn_dkv_branch(b, d, _attn_bias_t(d, hp, n + 1 < nb), lo, k_ref, v_ref, cur_refs, next_refs, dk_ref, dv_ref)

    cur, _, nxt = specs()
    vcur, _, _ = specs(O_AV // LANE)
    return pl.pallas_call(
        body, grid=(W // LANE, nb), in_specs=[cur, vcur, cur, nxt, cur, nxt, cur, nxt, cur, nxt], out_specs=[cur, cur],
        out_shape=[jax.ShapeDtypeStruct((S, W), F32)] * 2,
        compiler_params=_cp("parallel", "arbitrary"), name="attn_dkv")(
            kn, proj, qn, qn, dy, dy, lse, lse, delta, delta)


def _attn_dkv_branch(b, d, biases, lo, k_ref, v_ref, cur_refs, next_refs, dk_ref, dv_ref):
    B = ATTN_BLOCK
    own, inner, outer = biases
    G = ATTN_TILE // (d * B)

    def sub(g, r):
        rows = _attn_rows(d, g, r)
        kv, vv = k_ref[rows, :].astype(BF16), v_ref[rows, :].astype(BF16)
        dk = jnp.zeros((B, LANE), F32)
        dv = jnp.zeros((B, LANE), F32)
        inside = g + 1 < G
        after = _attn_rows(d, g + 1 if inside else 0, r)
        for bias, qrows, (q_ref, dy_ref, l_ref, de_ref) in (
                (own, rows, cur_refs), (inner if inside else outer, after, cur_refs if inside else next_refs)):
            qv, dyv = q_ref[qrows, :].astype(BF16), dy_ref[qrows, :]
            lt, det = l_ref[qrows, :].T, de_ref[qrows, :].T
            for h in range(2):
                sel = lo == (h == 0)
                qm = jnp.where(sel, qv, jnp.zeros_like(qv))
                dym = jnp.where(sel, dyv, 0.0).astype(BF16)
                lse_h = lt[h * ATTN_HD:h * ATTN_HD + 1, :]
                del_h = det[h * ATTN_HD:h * ATTN_HD + 1, :]
                pt = jnp.exp(_dot(kv, qm, NT) + bias[h] - lse_h)
                dv = dv + _dot(pt.astype(BF16), dym, NN)
                dst = pt * (_dot(vv, dym, NT) - del_h)
                dk = dk + _dot(dst.astype(BF16), qm, NN)
        dk_ref[rows, :] = dk if b == 0 else dk_ref[rows, :] + dk
        dv_ref[rows, :] = dv if b == 0 else dv_ref[rows, :] + dv

    _for_blocks(d, G, sub)


def _attn_post(dq, dk, dv, proj, qg, kg, ts):
    S = proj.shape[0]
    W = ATTN_DIM

    def body(dq_ref, dk_ref, dv_ref, aq_ref, ak_ref, qg_ref, kg_ref, daq_ref, dak_ref, dav_ref, gg_ref):
        i = pl.program_id(0)
        seg = _seg_matrix(W, ATTN_HD, 1.0 / ATTN_HD)
        gsums = []
        for d_ref, x_ref, g_ref, o_ref in ((dq_ref, aq_ref, qg_ref, daq_ref), (dk_ref, ak_ref, kg_ref, dak_ref)):
            dy = d_ref[...]
            xv = x_ref[...]
            r = lax.rsqrt(_seg_sum(xv * xv, seg) + EPS)
            xh = xv * r
            dxh = dy * g_ref[...]
            o_ref[...] = (r * (dxh - xh * _seg_sum(dxh * xh, seg))).astype(BF16)
            gsums.append(jnp.sum(dy * xh, axis=0, keepdims=True))
        dav_ref[...] = dv_ref[...].astype(BF16)
        _accumulate(gg_ref, _rows8(gsums, W), i)

    row = pl.BlockSpec((ts, W), lambda i: (i, 0))
    blk = lambda off: pl.BlockSpec((ts, W), lambda i: (i, off // W))
    vec = pl.BlockSpec((1, W), lambda i: (0, 0))
    return pl.pallas_call(
        body, grid=(S // ts,), in_specs=[row] * 3 + [blk(O_AQ), blk(O_AK), vec, vec],
        out_specs=[row, row, row, pl.BlockSpec((8, W), lambda i: (0, 0))],
        out_shape=[jax.ShapeDtypeStruct((S, W), BF16)] * 3 + [jax.ShapeDtypeStruct((8, W), F32)],
        compiler_params=_cp("arbitrary"), name="attn_post")(dq, dk, dv, proj, proj, qg, kg)


def _shift_down(cur, halo, n):
    return pltpu.roll(jnp.concatenate([halo, cur], axis=0), n, 0)[8:]


def _shift_up(cur, halo, n):
    ts = cur.shape[0]
    return pltpu.roll(jnp.concatenate([cur, halo], axis=0), ts + 8 - n, 0)[:ts]


def _conv(cur, halo, w, b):
    return b + w[0:1, :] * _shift_down(cur, halo, 2) + w[1:2, :] * _shift_down(cur, halo, 1) + w[2:3, :] * cur


def _mm_up_swiglu(h2, w_up, conv_w8, conv_b, tm, tc, ride=None):
    S, D = h2.shape
    F = w_up.shape[1] // 2
    nc = F // tc
    grid = (S // tm, nc)
    ride_arrays, ride_gather = ride if ride else ([], [])
    nr = len(ride_arrays)

    def body(h_ref, bg_ref, bv_ref, wg_ref, wv_ref, cg_ref, cv_ref, u0_ref, a_ref, at_ref, halo):
        i, j = pl.program_id(0), pl.program_id(1)
        hv = h_ref[...]
        acts = []
        for h, (b_ref, w_ref, c_ref) in enumerate(((bg_ref, wg_ref, cg_ref), (bv_ref, wv_ref, cv_ref))):
            u = _dot(hv, b_ref[...], NN)
            u0_ref[h] = u
            acts.append(_conv(u, jnp.where(i == 0, 0.0, halo[j, h]), w_ref[...], c_ref[...]))
            halo[j, h] = u[tm - 8:, :]
        g, v = acts
        a = g * _sigmoid(g) * v
        a_ref[...] = a.astype(BF16)
        at_ref[...] = a.T.astype(BF16)

    wcol = lambda rows, off: pl.BlockSpec((rows, tc), lambda i, j: (0, j + off))
    outs = pl.pallas_call(
        _riding(body, 7, 3, ride_gather, grid), grid=grid,
        in_specs=[pl.BlockSpec((tm, D), lambda i, j: (i, 0)), wcol(D, 0), wcol(D, nc), wcol(8, 0), wcol(8, nc), wcol(1, 0), wcol(1, nc)]
        + [HBM_SPEC] * nr,
        out_specs=[pl.BlockSpec((2, tm, tc), lambda i, j: (0, i, j)), pl.BlockSpec((tm, tc), lambda i, j: (i, j)),
                   pl.BlockSpec((tc, tm), lambda i, j: (j, i))] + [HBM_SPEC] * nr,
        out_shape=[jax.ShapeDtypeStruct((2, S, F), F32), jax.ShapeDtypeStruct((S, F), BF16), jax.ShapeDtypeStruct((F, S), BF16)]
        + _exchange_shapes(ride_arrays, ride_gather),
        scratch_shapes=[pltpu.VMEM((nc, 2, 8, tc), F32)] + (_exchange_sems(nr) if nr else []),
        compiler_params=_cp("arbitrary", "arbitrary"), name="mm_up")(
            h2, w_up, w_up, conv_w8, conv_w8, conv_b, conv_b, *ride_arrays)
    return outs[0], outs[1], outs[2], outs[3:]


def _mm_da_du0(dt2, w_down, u0, conv_w8, conv_b, tm, tc, ride=None):
    _, S, F = u0.shape
    D = dt2.shape[1]
    hb = tm // 8
    nrow = S // tm
    grid = (nrow,)
    ride_arrays, ride_gather = ride if ride else ([], [])
    nr = len(ride_arrays)

    def body(dt_ref, wd_ref, ug_ref, ugh_ref, uv_ref, uvh_ref, w_ref, b_ref, o_ref, sg_ref, sv_ref, following):
        i = pl.program_id(0)
        at_start, at_end = i == nrow - 1, i == 0
        dt = dt_ref[...]
        for c in range(F // tc):
            sums = []
            halves = []
            for h, (u_ref, h_ref) in enumerate(((ug_ref, ugh_ref), (uv_ref, uvh_ref))):
                cols = slice(h * F + c * tc, h * F + (c + 1) * tc)
                u, halo, w = u_ref[:, c * tc:(c + 1) * tc], jnp.where(at_start, 0.0, h_ref[:, c * tc:(c + 1) * tc]), w_ref[:, cols]
                s2, s1 = _shift_down(u, halo, 2), _shift_down(u, halo, 1)
                halves.append((b_ref[:, cols] + w[0:1, :] * s2 + w[1:2, :] * s1 + w[2:3, :] * u, s2, s1, u, w, cols))
            g, v = halves[0][0], halves[1][0]
            dav = _dot(dt, wd_ref[c * tc:(c + 1) * tc, :], NT)
            sig = _sigmoid(g)
            dus = (dav * v * (sig * (1.0 + g * (1.0 - sig))), dav * (g * sig))
            for h, du in enumerate(dus):
                _, s2, s1, u, w, cols = halves[h]
                after = jnp.where(at_end, 0.0, following[h, :, c * tc:(c + 1) * tc])
                o_ref[:, cols] = (w[2:3, :] * du + w[1:2, :] * _shift_up(du, after, 1) + w[0:1, :] * _shift_up(du, after, 2)).astype(BF16)
                following[h, :, c * tc:(c + 1) * tc] = du[0:8, :]
                sums.append(_rows8([jnp.sum(du * s2, axis=0, keepdims=True), jnp.sum(du * s1, axis=0, keepdims=True),
                                    jnp.sum(du * u, axis=0, keepdims=True), jnp.sum(du, axis=0, keepdims=True)], tc))
            for sums_ref, part in zip((sg_ref, sv_ref), sums):
                @pl.when(i == 0)
                def _(sums_ref=sums_ref, part=part, c=c):
                    sums_ref[:, c * tc:(c + 1) * tc] = part

                @pl.when(i > 0)
                def _(sums_ref=sums_ref, part=part, c=c):
                    sums_ref[:, c * tc:(c + 1) * tc] += part

    rev = lambda i: nrow - 1 - i
    main = lambda h: pl.BlockSpec((None, tm, F), lambda i: (h, rev(i), 0))
    halo = lambda h: pl.BlockSpec((None, 8, F), lambda i: (h, jnp.maximum(rev(i) * hb - 1, 0), 0))
    whole = lambda a: pl.BlockSpec(a.shape, lambda i: (0,) * a.ndim, pipeline_mode=pl.Buffered(1))
    sums_spec = pl.BlockSpec((8, F), lambda i: (0, 0))
    outs = pl.pallas_call(
        _riding(body, 8, 3, ride_gather, grid), grid=grid,
        in_specs=[pl.BlockSpec((tm, D), lambda i: (rev(i), 0)), whole(w_down), main(0), halo(0), main(1), halo(1),
                  whole(conv_w8), whole(conv_b)] + [HBM_SPEC] * nr,
        out_specs=[pl.BlockSpec((tm, 2 * F), lambda i: (rev(i), 0)), sums_spec, sums_spec] + [HBM_SPEC] * nr,
        out_shape=[jax.ShapeDtypeStruct((S, 2 * F), BF16), jax.ShapeDtypeStruct((8, F), F32), jax.ShapeDtypeStruct((8, F), F32)]
        + _exchange_shapes(ride_arrays, ride_gather),
        scratch_shapes=[pltpu.VMEM((2, 8, F), F32)] + (_exchange_sems(nr) if nr else []),
        compiler_params=_cp("arbitrary"), name="mm_da")(
            dt2, w_down, u0, u0, u0, u0, conv_w8, conv_b, *ride_arrays)
    return outs[0], outs[1], outs[2], outs[3:]


def _adamw(w, g, m, v, name, slots=False):
    shape = w.shape
    view = (math.prod(shape[:-1]), shape[-1])
    R, C = view
    limit = SUM_BLOCK_ELEMS // 2 if slots else SUM_BLOCK_ELEMS
    fits = [t for t in range(16, R + 1, 16) if R % t == 0 and t * C <= limit]
    tr = max(fits) if fits else R

    def body(w_ref, g_ref, m_ref, v_ref, *outs):
        if slots:
            gv = g_ref[0].astype(F32)
            for s in range(1, N_DEV):
                gv = gv + g_ref[s].astype(F32)
            outs[0][...] = gv
        else:
            gv = g_ref[...]
        d_ref, nm_ref, nv_ref = outs[-3:]
        nm = ADAM_B1 * m_ref[...] + (1.0 - ADAM_B1) * gv
        nv = ADAM_B2 * v_ref[...] + (1.0 - ADAM_B2) * (gv * gv)
        m_hat = nm / (1.0 - ADAM_B1 ** ADAM_STEP)
        v_hat = nv / (1.0 - ADAM_B2 ** ADAM_STEP)
        d_ref[...] = -ADAM_LR * (m_hat / (jnp.sqrt(v_hat) + ADAM_EPS) + ADAM_WD * w_ref[...])
        nm_ref[...] = nm
        nv_ref[...] = nv

    spec = pl.BlockSpec((tr, C), lambda i: (i, 0))
    g_spec = pl.BlockSpec((N_DEV, tr, C), lambda i: (0, i, 0)) if slots else spec
    n_out = 4 if slots else 3
    outs = pl.pallas_call(
        body, grid=(R // tr,), in_specs=[spec, g_spec, spec, spec], out_specs=[spec] * n_out,
        out_shape=[jax.ShapeDtypeStruct(view, F32)] * n_out, compiler_params=_cp("parallel"), name=name)(
            w.reshape(view), g if slots else g.reshape(view), m.reshape(view), v.reshape(view))
    outs = [o.reshape(shape) for o in outs]
    return outs if slots else [g.reshape(shape)] + outs


def _pad_rows8(a):
    return jnp.concatenate([a, jnp.zeros((8 - a.shape[0], a.shape[1]), a.dtype)], axis=0)


def _local_step(x, target, mod, n1g, w_in_s, conv_w_s, wg_s, bg, gng, qng, kng, w_out_s, n2g, w_up_s, conv_b, w_down_s):
    S, D = x.shape
    F = w_down_s.shape[0] * N_DEV
    cw_c, wg_c = conv_w_s.shape[1], wg_s.shape[1]
    ts = min(512, S)
    sh1, sc1, g1, sh2, sc2, g2 = [mod[i:i + 1] for i in range(6)]
    qg_t, kg_t = jnp.tile(qng, (1, ATTN_HEADS)), jnp.tile(kng, (1, ATTN_HEADS))

    small = jnp.concatenate([conv_w_s.reshape(1, -1), wg_s.reshape(1, -1)], axis=1)
    n_small = small.shape[1]
    small = jnp.pad(small, ((0, 0), (0, -n_small % LANE)))
    h1, h1_t, (g_in, g_small) = _rms_mod(x, n1g, sc1, sh1, ts, "rms_mod1", ride=([w_in_s, small], [True, True]))
    w_in_full = _cols_from_blocks(g_in)
    w_in_p = jnp.concatenate([w_in_full[:, :GLR_SRC], w_in_full[:, GLR_SRC + GLA_RANK:],
                              w_in_full[:, GLR_SRC:GLR_SRC + GLA_RANK], jnp.zeros((D, PROJ_W - O_GLR - GLA_RANK), BF16)], axis=1)
    g_small = g_small.reshape(N_DEV, -1)
    conv_w8 = _pad_rows8(jnp.stack([g_small[:, t * cw_c:(t + 1) * cw_c].reshape(-1) for t in range(3)]))
    wg_full = _cols_from_blocks(g_small[:, 3 * cw_c:n_small].reshape(N_DEV, GLA_RANK, wg_c))
    wg_p = jnp.concatenate([wg_full, jnp.zeros((LANE - GLA_RANK, wg_full.shape[1]), F32)], axis=0)
    (proj, la, qn, kn), (g_out,) = _mm_in(h1, w_in_p, wg_p, bg, qg_t, kg_t, ts, ride=([w_out_s], [True]))
    w_out = g_out.reshape(-1, D)
    o_gla, states, (g_up,) = _gla_fwd(proj, la, 512, ride=([w_up_s], [True]))
    w_up = _cols_from_blocks(g_up)
    y_gla = _gla_out(o_gla, proj, gng, ts)
    y_att, lse = _attn_fwd(qn, kn, proj)
    mixed, mixed_t = _attn_mix(y_gla, y_att, ts)
    t1, x2, h2, h2_t = _mm_resid_rms_mod(mixed, w_out, x, g1, n2g, sc2, sh2, ts, "mm_out")
    tc = 1408 if F % 1408 == 0 else F
    u0, a, a_t, (g_down,) = _mm_up_swiglu(h2, w_up, conv_w8, conv_b, ts, tc, ride=([w_down_s], [True]))
    w_down = g_down.reshape(F, D)
    dx3, dt2, sums3 = _mm_loss_resid(a, w_down, x2, g2, target, ts, "mm_down")
    loss_row, dg2 = sums3[0:1], sums3[1:2]

    g_w_down = _mm(a_t, dt2, NN, 1408, 1024, 2048, F32, "mm_gw_down")
    du0, sums_g, sums_v, (r_down,) = _mm_da_du0(dt2, w_down, u0, conv_w8, conv_b, min(256, S), tc,
                                                ride=([g_w_down.reshape(N_DEV, -1, D)], [False]))
    g_conv_w = jnp.concatenate([sums_g[0:3], sums_v[0:3]], axis=1)
    g_conv_b = jnp.concatenate([sums_g[3:4], sums_v[3:4]], axis=1)
    g_w_up = _mm(h2_t, du0, NN, 512, 2816, 2048, F32, "mm_gw_up")
    (dx2, sums2, dt1), _ = _mm_rms_mod_bwd(du0, w_up, x2, dx3, n2g, sc2, ts, "mm_dh2", t_prev=t1, g_prev=g1)
    dsh2, dsc2, g_n2g, dg1 = sums2[0:1], sums2[1:2], sums2[2:3], sums2[3:4]
    g_w_out = _mm(mixed_t, dt1, NN, 1024, 1024, 2048, F32, "mm_gw_out")
    do_gla, dgr, gng_sums, dy_att, delta = _mm_mixed_bwd(dt1, w_out, o_gla, proj, gng, y_att, ts)
    dgq, dgk, dgv, dla, (r_up, r_out) = _gla_bwd(
        proj, la, do_gla, states, 512, ride=([_col_blocks(g_w_up), g_w_out.reshape(N_DEV, -1, D)], [False, False]))
    dglr, g_wg_p, gb_sums = _gate_bwd(dla, la, proj, wg_p, ts)
    dqn = _attn_dq(qn, kn, proj, dy_att, lse, delta)
    dkn, dvn = _attn_dkv(qn, kn, proj, dy_att, lse, delta)
    daq, dak, dav, qk_sums = _attn_post(dqn, dkn, dvn, proj, qg_t, kg_t, ts)
    dproj = jnp.concatenate([dgq, dgk, dgv, dgr, daq, dak, dav, dglr, jnp.zeros((S, PROJ_W - O_GLR - LANE), BF16)], axis=1)
    g_w_in_p = _mm(h1_t, dproj, NN, 512, PROJ_W, 1024, F32, "mm_gw_in")
    g_w_in = jnp.concatenate([g_w_in_p[:, :GLR_SRC], g_w_in_p[:, O_GLR:O_GLR + GLA_RANK], g_w_in_p[:, GLR_SRC:O_GLR]], axis=1)
    (dx, sums1), (r_in,) = _mm_rms_mod_bwd(dproj, w_in_p, x, dx2, n1g, sc1, ts, "mm_dh1",
                                           ride=([_col_blocks(g_w_in).astype(BF16)], [False]))
    dsh1, dsc1, g_n1g = sums1[0:1], sums1[1:2], sums1[2:3]

    dmod = jnp.concatenate([dsh1, dsc1, dg1, dsh2, dsc2, dg2], axis=1)
    grads = dict(n1g=g_n1g, w_in=r_in, wg=g_wg_p[:GLA_RANK], bg=gb_sums[0:1], gng=gng_sums[0:1],
                 qng_lanes=qk_sums[0:1], kng_lanes=qk_sums[1:2], w_out=r_out, n2g=g_n2g, w_up=r_up,
                 conv_w=g_conv_w, conv_b=g_conv_b, w_down=r_down)
    return loss_row, dx, dmod, grads


def _col_blocks(a):
    R, W = a.shape
    return a.reshape(R, N_DEV, W // N_DEV).transpose(1, 0, 2)


def _cols_from_blocks(a):
    n, R, C = a.shape
    return a.transpose(1, 0, 2).reshape(R, n * C)


def kernel(x, c, w_ada, b_ada, norm1_g, w_in, gla_w_gate, gla_b_gate, gla_norm_g, q_norm_g, k_norm_g, w_out, norm2_g, w_up, conv_w, conv_b, w_down, loss_target, m_w_ada, m_b_ada, m_norm1_g, m_w_in, m_gla_w_gate, m_gla_b_gate, m_gla_norm_g, m_q_norm_g, m_k_norm_g, m_w_out, m_norm2_g, m_w_up, m_conv_w, m_conv_b, m_w_down, v_w_ada, v_b_ada, v_norm1_g, v_w_in, v_gla_w_gate, v_gla_b_gate, v_gla_norm_g, v_q_norm_g, v_k_norm_g, v_w_out, v_norm2_g, v_w_up, v_conv_w, v_conv_b, v_w_down):
    axes = ("x", "y", "c")
    me = 4 * lax.axis_index("x") + 2 * lax.axis_index("y") + lax.axis_index("c")
    S, D = x.shape[1], x.shape[2]
    x2d, tgt2d = x[0], loss_target[0]
    w_in_s, w_out_s, w_up_s, w_down_s, w_ada_s = w_in[0], w_out[0], w_up[0], w_down[0], w_ada[0]
    conv_w_s, wg_s = conv_w[0], gla_w_gate[0]
    in_c, up_c, ada_c, wg_c, cw_c = w_in_s.shape[1], w_up_s.shape[1], w_ada_s.shape[1], wg_s.shape[1], conv_w_s.shape[1]
    F = w_down_s.shape[0] * N_DEV

    g_c, = _exchange([c], [True], "gather_c")
    c_all = g_c.reshape(N_DEV, D)

    b_shard = lax.dynamic_slice(b_ada, (0, me * ada_c), (1, ada_c))
    mod_part = _ada_fwd(c_all, w_ada_s, b_shard)
    mod_recv, = _exchange([mod_part.reshape(N_DEV, 1, ada_c)], [False], "exchange_mod")
    mod = mod_recv.reshape(6, D)

    loss_row, dx, dmod, gr = _local_step(
        x2d, tgt2d, mod, norm1_g, w_in_s.astype(BF16), conv_w_s, wg_s, gla_b_gate, gla_norm_g, q_norm_g, k_norm_g,
        w_out_s.astype(BF16), norm2_g, w_up_s.astype(BF16), conv_b, w_down_s.astype(BF16))
    loss = lax.psum(0.5 / D * jnp.sum(loss_row), axes)

    parts = [dmod, gr["n1g"], gr["bg"], gr["gng"], gr["qng_lanes"], gr["kng_lanes"], gr["n2g"], gr["conv_b"],
             gr["wg"].reshape(1, -1), gr["conv_w"].reshape(1, -1)]
    sizes = [p.shape[1] for p in parts]
    packed = jnp.concatenate(parts, axis=1)
    packed = jnp.pad(packed, ((0, 0), (0, -packed.shape[1] % (8 * LANE))))
    gathered, = _exchange([packed.reshape(8, -1)], [True], "gather_small_grads")
    gathered = gathered.reshape(N_DEV, -1)
    total = _sum_slots(gathered.reshape(N_DEV, 8, -1), "sum_small_grads").reshape(1, -1)
    offs = [0]
    for s_ in sizes:
        offs.append(offs[-1] + s_)
    t_dmod, t_n1g, t_bg, t_gng, t_qng, t_kng, t_n2g, t_conv_b, t_wg, t_conv_w = [
        total[:, offs[i]:offs[i + 1]] for i in range(len(sizes))]
    g_b_ada = t_dmod
    g_qng = t_qng.reshape(ATTN_HEADS, ATTN_HD).sum(axis=0, keepdims=True)
    g_kng = t_kng.reshape(ATTN_HEADS, ATTN_HD).sum(axis=0, keepdims=True)
    g_wg = lax.dynamic_slice(t_wg.reshape(GLA_RANK, -1), (0, me * wg_c), (GLA_RANK, wg_c))
    g_conv_w = lax.dynamic_slice(t_conv_w.reshape(3, -1), (0, me * cw_c), (3, cw_c))
    dmod_shard = lax.dynamic_slice(gathered[:, :6 * D], (0, me * ada_c), (N_DEV, ada_c))
    g_w_ada = _ada_bwd(c_all, dmod_shard)

    g_w_in, g_w_out, g_w_up, g_w_down = gr["w_in"], gr["w_out"], gr["w_up"], gr["w_down"]
    in_slots = {"w_in", "w_out", "w_up", "w_down"}
    names = ["w_ada", "b_ada", "norm1_g", "w_in", "gla_w_gate", "gla_b_gate", "gla_norm_g", "q_norm_g", "k_norm_g",
             "w_out", "norm2_g", "w_up", "conv_w", "conv_b", "w_down"]
    ws = [w_ada, b_ada, norm1_g, w_in, gla_w_gate, gla_b_gate, gla_norm_g, q_norm_g, k_norm_g, w_out, norm2_g, w_up, conv_w, conv_b, w_down]
    ms = [m_w_ada, m_b_ada, m_norm1_g, m_w_in, m_gla_w_gate, m_gla_b_gate, m_gla_norm_g, m_q_norm_g, m_k_norm_g, m_w_out, m_norm2_g, m_w_up, m_conv_w, m_conv_b, m_w_down]
    vs = [v_w_ada, v_b_ada, v_norm1_g, v_w_in, v_gla_w_gate, v_gla_b_gate, v_gla_norm_g, v_q_norm_g, v_k_norm_g, v_w_out, v_norm2_g, v_w_up, v_conv_w, v_conv_b, v_w_down]
    gs = [g_w_ada, g_b_ada, t_n1g, g_w_in, g_wg, t_bg, t_gng, g_qng, g_kng, g_w_out, t_n2g, g_w_up, g_conv_w, t_conv_b, g_w_down]
    grads, deltas, new_ms, new_vs = [], [], [], []
    for nm, w, g, m, v in zip(names, ws, gs, ms, vs):
        g_, d_, m_, v_ = _adamw(w, g, m, v, "adamw_" + nm, slots=nm in in_slots)
        grads.append(g_)
        deltas.append(d_)
        new_ms.append(m_)
        new_vs.append(v_)
    return (loss, dx.reshape(x.shape), *grads, *deltas, *new_ms, *new_vs)
```

```python
import functools
import math

import jax
import jax.numpy as jnp
from jax import lax
from jax.experimental import pallas as pl
from jax.experimental.pallas import tpu as pltpu

F32, BF16 = jnp.float32, jnp.bfloat16
HI = lax.Precision.HIGHEST
EPS = 1e-6
NEG = -1e30

N_DEV = 8
GLA_HEADS, GLA_DK, GLA_DV, GLA_RANK, GLA_TAU, GLA_CHUNK = 4, 64, 128, 16, 16.0, 64
ATTN_HEADS, ATTN_HD, ATTN_BLOCK = 8, 64, 128
DILATIONS = (1, 4, 16)
GLA_QK, GLA_V, ATTN_DIM = GLA_HEADS * GLA_DK, GLA_HEADS * GLA_DV, ATTN_HEADS * ATTN_HD
O_GQ, O_GK, O_GV, O_GR, O_AQ, O_AK, O_AV, O_GLR = 0, 256, 512, 1024, 1536, 2048, 2560, 3072
PROJ_W = 3328
LANE = 128
GLR_SRC = 2 * GLA_QK + 2 * GLA_V

ADAM_LR, ADAM_B1, ADAM_B2, ADAM_EPS, ADAM_WD, ADAM_STEP = 0.001, 0.9, 0.999, 1e-08, 0.01, 10

VMEM_LIMIT = 56 * 1024 * 1024
SUM_BLOCK_ELEMS = 256 * 1024


def _cp(*sem):
    return pltpu.CompilerParams(dimension_semantics=sem, vmem_limit_bytes=VMEM_LIMIT)


def _dot(a, b, dims, precision=None):
    return lax.dot_general(a, b, (dims, ((), ())), preferred_element_type=F32, precision=precision)


NN, NT, TN = ((1,), (0,)), ((1,), (1,)), ((0,), (0,))


def _sigmoid(z):
    return 1.0 / (1.0 + jnp.exp(-z))


HBM_SPEC = pl.BlockSpec(memory_space=pltpu.HBM)


def _exchange_shapes(arrays, gather):
    return [jax.ShapeDtypeStruct((N_DEV,) + (a.shape if g else a.shape[1:]), a.dtype) for a, g in zip(arrays, gather)]


def _exchange_sems(n):
    return [pltpu.SemaphoreType.DMA((n * (N_DEV - 1),)), pltpu.SemaphoreType.DMA((n * (N_DEV - 1),)), pltpu.SemaphoreType.DMA((n,))]


VIA_SIBLING = "via sibling"


def _exchange_plan(ins, outs, gather, send_sems, recv_sems, local_sems):
    x, y, c = lax.axis_index("x"), lax.axis_index("y"), lax.axis_index("c")
    me = 4 * x + 2 * y + c
    start, relays, waits = [], [], []
    for a in range(len(ins)):
        if gather[a] == VIA_SIBLING:
            def copy(i, block, to, src=None, a=a):
                slot = outs[a].at[4 * block[0] + 2 * block[1] + block[2]]
                return pltpu.make_async_remote_copy(
                    src_ref=slot if src is None else src, dst_ref=slot, send_sem=send_sems.at[a * (N_DEV - 1) + i],
                    recv_sem=recv_sems.at[a * (N_DEV - 1) + i], device_id=to, device_id_type=pl.DeviceIdType.MESH)

            chips = [(1 - x, y), (x, 1 - y), (1 - x, 1 - y)]
            first = [copy(0, (x, y, c), (x, y, 1 - c), src=ins[a])]
            first += [copy(1 + j, (x, y, c), (*chip, c), src=ins[a]) for j, chip in enumerate(chips)]
            passed = [copy(4 + j, (*chip, c), (x, y, 1 - c)) for j, chip in enumerate(chips)]
            start += first
            relays += [(copy(1 + j, (*chip, c), (x, y, c)).wait_recv, passed[j]) for j, chip in enumerate(chips)]
            waits += [copy(0, (x, y, 1 - c), (x, y, c)).wait_recv]
            waits += [copy(4 + j, (*chip, 1 - c), (x, y, c)).wait_recv for j, chip in enumerate(chips)]
            waits += [cp.wait_send for cp in first + passed]
        else:
            for p in range(1, N_DEV):
                px, py, pc = x ^ (p >> 2), y ^ ((p >> 1) & 1), c ^ (p & 1)
                peer = 4 * px + 2 * py + pc
                k = a * (N_DEV - 1) + p - 1
                cp = pltpu.make_async_remote_copy(
                    src_ref=ins[a] if gather[a] else ins[a].at[peer], dst_ref=outs[a].at[me],
                    send_sem=send_sems.at[k], recv_sem=recv_sems.at[k],
                    device_id=(px, py, pc), device_id_type=pl.DeviceIdType.MESH)
                start.append(cp)
                waits.append(cp.wait)
        own = pltpu.make_async_copy(ins[a] if gather[a] else ins[a].at[me], outs[a].at[me], local_sems.at[a])
        start.append(own)
        waits.append(own.wait)
    return start, relays, waits


def _exchange_finish(relays, waits):
    for arrived, pass_on in relays:
        arrived()
        pass_on.start()
    for wait in waits:
        wait()


def _riding(body, n_in, n_out, gather, grid):
    nr = len(gather)
    if not nr:
        return body

    def wrapped(*refs):
        ins, r_ins = refs[:n_in], refs[n_in:n_in + nr]
        outs, r_outs = refs[n_in + nr:n_in + nr + n_out], refs[n_in + nr + n_out:n_in + 2 * nr + n_out]
        scratch = refs[n_in + 2 * nr + n_out:]
        first = last = None
        for t, steps in enumerate(grid):
            pid = pl.program_id(t)
            first = (pid == 0) if first is None else first & (pid == 0)
            last = (pid == steps - 1) if last is None else last & (pid == steps - 1)
        start, relays, waits = _exchange_plan(r_ins, r_outs, gather, *scratch[-3:])

        @pl.when(first)
        def _():
            for cp in start:
                cp.start()

        body(*ins, *outs, *scratch[:-3])

        @pl.when(last)
        def _():
            _exchange_finish(relays, waits)

    return wrapped


def _exchange(arrays, gather, name):
    n = len(arrays)

    def body(*refs):
        start, relays, waits = _exchange_plan(refs[:n], refs[n:2 * n], gather, *refs[2 * n:])
        for cp in start:
            cp.start()
        _exchange_finish(relays, waits)

    return pl.pallas_call(
        body, out_shape=_exchange_shapes(arrays, gather), in_specs=[HBM_SPEC] * n, out_specs=[HBM_SPEC] * n,
        scratch_shapes=_exchange_sems(n), name=name)(*arrays)


def _sum_slots(x, name):
    _, R, C = x.shape
    tr = max(t for t in range(8, min(SUM_BLOCK_ELEMS // C, R) + 1, 8) if R % t == 0)

    def body(x_ref, o_ref):
        acc = x_ref[0].astype(F32)
        for s in range(1, N_DEV):
            acc = acc + x_ref[s].astype(F32)
        o_ref[...] = acc

    return pl.pallas_call(
        body, grid=(R // tr,), in_specs=[pl.BlockSpec((N_DEV, tr, C), lambda i: (0, i, 0))],
        out_specs=pl.BlockSpec((tr, C), lambda i: (i, 0)), out_shape=jax.ShapeDtypeStruct((R, C), F32),
        compiler_params=_cp("parallel"), name=name)(x)


def _mm(a, b, mode, tm, tn, tk, out_dtype, name, ride=None):
    if mode == NN:
        (M, K), N = a.shape, b.shape[1]
    elif mode == NT:
        (M, K), N = a.shape, b.shape[0]
    else:
        (K, M), N = a.shape, b.shape[1]
    tm, tn, tk = min(tm, M), min(tn, N), min(tk, K)
    assert M % tm == 0 and N % tn == 0 and K % tk == 0, (name, M, N, K, tm, tn, tk)
    nk = K // tk
    if mode == NN:
        a_spec = pl.BlockSpec((tm, tk), lambda i, j, k: (i, k))
        b_spec = pl.BlockSpec((tk, tn), lambda i, j, k: (k, j))
    elif mode == NT:
        a_spec = pl.BlockSpec((tm, tk), lambda i, j, k: (i, k))
        b_spec = pl.BlockSpec((tn, tk), lambda i, j, k: (j, k))
    else:
        a_spec = pl.BlockSpec((tk, tm), lambda i, j, k: (k, i))
        b_spec = pl.BlockSpec((tk, tn), lambda i, j, k: (k, j))

    ride_arrays, ride_gather = ride if ride else ([], [])
    nr = len(ride_arrays)
    grid = (M // tm, N // tn, nk)

    own_acc = nk > 1 and out_dtype != F32

    def body(a_ref, b_ref, o_ref, *acc):
        p = _dot(a_ref[...].astype(BF16), b_ref[...].astype(BF16), mode)
        if nk == 1:
            o_ref[...] = p.astype(out_dtype)
        else:
            acc_ref = acc[0] if own_acc else o_ref
            k = pl.program_id(2)

            @pl.when(k == 0)
            def _():
                acc_ref[...] = p

            @pl.when(k > 0)
            def _():
                acc_ref[...] += p

            if own_acc:
                @pl.when(k == nk - 1)
                def _():
                    o_ref[...] = acc_ref[...].astype(out_dtype)

    outs = pl.pallas_call(
        _riding(body, 2, 1, ride_gather, grid), grid=grid, in_specs=[a_spec, b_spec] + [HBM_SPEC] * nr,
        out_specs=[pl.BlockSpec((tm, tn), lambda i, j, k: (i, j))] + [HBM_SPEC] * nr,
        out_shape=[jax.ShapeDtypeStruct((M, N), out_dtype)] + _exchange_shapes(ride_arrays, ride_gather),
        scratch_shapes=([pltpu.VMEM((tm, tn), F32)] if own_acc else []) + (_exchange_sems(nr) if nr else []),
        compiler_params=_cp(*(("arbitrary",) * 3 if nr else ("parallel", "parallel", "arbitrary"))), name=name)(a, b, *ride_arrays)
    return (outs[0], outs[1:]) if nr else outs[0]


def _ada_fwd(c_all, w_shard, b_shard):
    Nc = w_shard.shape[1]

    def body(c_ref, w_ref, b_ref, o_ref):
        cv = c_ref[...]
        o_ref[...] = _dot(cv * _sigmoid(cv), w_ref[...], NN, HI) + b_ref[...]

    return pl.pallas_call(body, out_shape=jax.ShapeDtypeStruct((N_DEV, Nc), F32), name="ada_fwd",
                          compiler_params=pltpu.CompilerParams(vmem_limit_bytes=VMEM_LIMIT))(c_all, w_shard, b_shard)


def _ada_bwd(c_all, dmod_shard):
    D, Nc = c_all.shape[1], dmod_shard.shape[1]

    def body(c_ref, d_ref, o_ref):
        cv = c_ref[...]
        o_ref[...] = _dot(cv * _sigmoid(cv), d_ref[...], TN, HI)

    return pl.pallas_call(body, out_shape=jax.ShapeDtypeStruct((D, Nc), F32), name="ada_bwd",
                          compiler_params=pltpu.CompilerParams(vmem_limit_bytes=VMEM_LIMIT))(c_all, dmod_shard)


def _row_spec(ts, D):
    return pl.BlockSpec((ts, D), lambda i: (i, 0))


def _vec_spec(D):
    return pl.BlockSpec((1, D), lambda i: (0, 0))


def _col_spec(D, ts):
    return pl.BlockSpec((D, ts), lambda i: (0, i))


def _rms_mod(x, ng, sc, sh, ts, name, ride=None):
    S, D = x.shape
    ride_arrays, ride_gather = ride if ride else ([], [])
    nr = len(ride_arrays)
    grid = (S // ts,)

    def body(x_ref, ng_ref, sc_ref, sh_ref, h_ref, ht_ref):
        xv = x_ref[...]
        r = lax.rsqrt(jnp.mean(xv * xv, axis=-1, keepdims=True) + EPS)
        h = xv * r * ng_ref[...] * (1.0 + sc_ref[...]) + sh_ref[...]
        h_ref[...] = h.astype(BF16)
        ht_ref[...] = h.T.astype(BF16)

    outs = pl.pallas_call(
        _riding(body, 4, 2, ride_gather, grid), grid=grid, in_specs=[_row_spec(ts, D)] + [_vec_spec(D)] * 3 + [HBM_SPEC] * nr,
        out_specs=[_row_spec(ts, D), _col_spec(D, ts)] + [HBM_SPEC] * nr,
        out_shape=[jax.ShapeDtypeStruct((S, D), BF16), jax.ShapeDtypeStruct((D, S), BF16)] + _exchange_shapes(ride_arrays, ride_gather),
        scratch_shapes=_exchange_sems(nr) if nr else [],
        compiler_params=_cp("arbitrary"), name=name)(x, ng, sc, sh, *ride_arrays)
    return outs[0], outs[1], outs[2:]


def _mm_rows(a, b, mode, tm, extras, extra_specs, out_shapes, out_specs, epilogue, name, ride=None):
    M, K = a.shape
    grid = (M // tm,)
    ride_arrays, ride_gather = ride if ride else ([], [])
    nr = len(ride_arrays)

    def body(a_ref, b_ref, *refs):
        epilogue(_dot(a_ref[...].astype(BF16), b_ref[...].astype(BF16), mode), pl.program_id(0), *refs)

    outs = pl.pallas_call(
        _riding(body, 2 + len(extras), len(out_shapes), ride_gather, grid), grid=grid,
        in_specs=[pl.BlockSpec((tm, K), lambda i: (i, 0)), pl.BlockSpec(b.shape, lambda i: (0, 0), pipeline_mode=pl.Buffered(1))]
        + list(extra_specs) + [HBM_SPEC] * nr,
        out_specs=list(out_specs) + [HBM_SPEC] * nr,
        out_shape=list(out_shapes) + _exchange_shapes(ride_arrays, ride_gather),
        scratch_shapes=_exchange_sems(nr) if nr else [],
        compiler_params=_cp("arbitrary"), name=name)(a, b, *extras, *ride_arrays)
    return outs[:len(out_shapes)], outs[len(out_shapes):]


def _accumulate(ref, part, step):
    @pl.when(step == 0)
    def _():
        ref[...] = part

    @pl.when(step > 0)
    def _():
        ref[...] += part


def _rows8(rows, width):
    return jnp.concatenate(rows + [jnp.zeros((8 - len(rows), width), F32)], axis=0)


def _mm_resid_rms_mod(a, w, x, g, ng, sc, sh, tm, name):
    S, D = x.shape

    def epilogue(t, step, x_ref, g_ref, ng_ref, sc_ref, sh_ref, t_ref, x2_ref, h_ref, ht_ref):
        t_ref[...] = t
        xv = x_ref[...] + g_ref[...] * t
        x2_ref[...] = xv
        r = lax.rsqrt(jnp.mean(xv * xv, axis=-1, keepdims=True) + EPS)
        h = xv * r * ng_ref[...] * (1.0 + sc_ref[...]) + sh_ref[...]
        h_ref[...] = h.astype(BF16)
        ht_ref[...] = h.T.astype(BF16)

    row, vec = _row_spec(tm, D), _vec_spec(D)
    full, half = jax.ShapeDtypeStruct((S, D), F32), jax.ShapeDtypeStruct((S, D), BF16)
    outs, _ = _mm_rows(a, w, NN, tm, [x, g, ng, sc, sh], [row] + [vec] * 4,
                       [full, full, half, jax.ShapeDtypeStruct((D, S), BF16)], [row, row, row, _col_spec(D, tm)], epilogue, name)
    return outs


def _mm_loss_resid(a, w, x2, g2, target, tm, name):
    S, D = x2.shape

    def epilogue(t, step, x_ref, y_ref, g_ref, dx_ref, dt_ref, sums_ref):
        gv = g_ref[...]
        e = x_ref[...] + gv * t - y_ref[...]
        dx = e * (1.0 / D)
        dx_ref[...] = dx
        dt_ref[...] = (dx * gv).astype(BF16)
        _accumulate(sums_ref, _rows8([jnp.sum(e * e, axis=0, keepdims=True), jnp.sum(dx * t, axis=0, keepdims=True)], D), step)

    row, vec = _row_spec(tm, D), _vec_spec(D)
    outs, _ = _mm_rows(a, w, NN, tm, [x2, target, g2], [row, row, vec],
                       [jax.ShapeDtypeStruct((S, D), F32), jax.ShapeDtypeStruct((S, D), BF16), jax.ShapeDtypeStruct((8, D), F32)],
                       [row, row, pl.BlockSpec((8, D), lambda i: (0, 0))], epilogue, name)
    return outs


def _mm_rms_mod_bwd(a, w, xin, dres, ng, sc, tm, name, t_prev=None, g_prev=None, ride=None):
    S, D = xin.shape
    chain = t_prev is not None

    def epilogue(dhv, step, *refs):
        if chain:
            x_ref, dr_ref, ng_ref, sc_ref, t_ref, g_ref, dx_ref, sums_ref, dt_ref = refs
        else:
            x_ref, dr_ref, ng_ref, sc_ref, dx_ref, sums_ref = refs
        xv = x_ref[...]
        r = lax.rsqrt(jnp.mean(xv * xv, axis=-1, keepdims=True) + EPS)
        xh = xv * r
        ngv, scv = ng_ref[...], sc_ref[...]
        dxh = dhv * (ngv * (1.0 + scv))
        dx = dr_ref[...] + r * (dxh - xh * jnp.mean(dxh * xh, axis=-1, keepdims=True))
        dx_ref[...] = dx
        dhx = dhv * xh
        rows = [jnp.sum(dhv, axis=0, keepdims=True), jnp.sum(dhx * ngv, axis=0, keepdims=True),
                jnp.sum(dhx * (1.0 + scv), axis=0, keepdims=True)]
        if chain:
            dt_ref[...] = (dx * g_ref[...]).astype(BF16)
            rows.append(jnp.sum(dx * t_ref[...], axis=0, keepdims=True))
        _accumulate(sums_ref, _rows8(rows, D), step)

    row, vec = _row_spec(tm, D), _vec_spec(D)
    extras = [xin, dres, ng, sc] + ([t_prev, g_prev] if chain else [])
    extra_specs = [row, row, vec, vec] + ([row, vec] if chain else [])
    out_shapes = [jax.ShapeDtypeStruct((S, D), F32), jax.ShapeDtypeStruct((8, D), F32)] + (
        [jax.ShapeDtypeStruct((S, D), BF16)] if chain else [])
    out_specs = [row, pl.BlockSpec((8, D), lambda i: (0, 0))] + ([row] if chain else [])
    return _mm_rows(a, w, NT, tm, extras, extra_specs, out_shapes, out_specs, epilogue, name, ride=ride)


def _mm_in(h1, w_in_p, wg_p, bg, qg, kg, tm, ride=None):
    S = h1.shape[0]
    W = ATTN_DIM

    def epilogue(p, step, wg_ref, bg_ref, qg_ref, kg_ref, proj_ref, la_ref, qn_ref, kn_ref):
        proj_ref[...] = p
        z = _dot(p[:, O_GLR:O_GLR + LANE], wg_ref[...], NN, HI) + bg_ref[...]
        la_ref[...] = (jnp.minimum(z, 0.0) - jnp.log(1.0 + jnp.exp(-jnp.abs(z)))) * (1.0 / GLA_TAU)
        seg = _seg_matrix(W, ATTN_HD, 1.0 / ATTN_HD)
        for off, g_ref, o_ref, scale in ((O_AQ, qg_ref, qn_ref, ATTN_HD ** -0.5), (O_AK, kg_ref, kn_ref, 1.0)):
            xv = p[:, off:off + W]
            o_ref[...] = xv * lax.rsqrt(_seg_sum(xv * xv, seg) + EPS) * (g_ref[...] * scale)

    row = lambda w: pl.BlockSpec((tm, w), lambda i: (i, 0))
    const = lambda a: pl.BlockSpec(a.shape, lambda i: (0, 0))
    return _mm_rows(
        h1, w_in_p, NN, tm, [wg_p, bg, qg, kg], [const(wg_p), const(bg), const(qg), const(kg)],
        [jax.ShapeDtypeStruct((S, PROJ_W), F32), jax.ShapeDtypeStruct((S, GLA_QK), F32),
         jax.ShapeDtypeStruct((S, W), F32), jax.ShapeDtypeStruct((S, W), F32)],
        [row(PROJ_W), row(GLA_QK), row(W), row(W)], epilogue, "mm_in", ride=ride)


def _gate_bwd(dla, la, proj, wg_p, ts):
    S = proj.shape[0]

    def body(dla_ref, la_ref, glr_ref, w_ref, dglr_ref, gw_ref, gb_ref):
        i = pl.program_id(0)
        dz = dla_ref[...] * (1.0 / GLA_TAU) * (1.0 - jnp.exp(GLA_TAU * la_ref[...]))
        dglr_ref[...] = _dot(dz, w_ref[...], NT, HI).astype(BF16)
        gw = _dot(glr_ref[...], dz, TN, HI)
        gb = jnp.concatenate([jnp.sum(dz, axis=0, keepdims=True), jnp.zeros((7, GLA_QK), F32)], axis=0)

        @pl.when(i == 0)
        def _():
            gw_ref[...] = gw
            gb_ref[...] = gb

        @pl.when(i > 0)
        def _():
            gw_ref[...] += gw
            gb_ref[...] += gb

    return pl.pallas_call(
        body, grid=(S // ts,),
        in_specs=[pl.BlockSpec((ts, GLA_QK), lambda i: (i, 0)), pl.BlockSpec((ts, GLA_QK), lambda i: (i, 0)),
                  pl.BlockSpec((ts, LANE), lambda i: (i, O_GLR // LANE)), pl.BlockSpec((LANE, GLA_QK), lambda i: (0, 0))],
        out_specs=[pl.BlockSpec((ts, LANE), lambda i: (i, 0)), pl.BlockSpec((LANE, GLA_QK), lambda i: (0, 0)),
                   pl.BlockSpec((8, GLA_QK), lambda i: (0, 0))],
        out_shape=[jax.ShapeDtypeStruct((S, LANE), BF16), jax.ShapeDtypeStruct((LANE, GLA_QK), F32),
                   jax.ShapeDtypeStruct((8, GLA_QK), F32)],
        compiler_params=_cp("arbitrary"), name="gla_gate_bwd")(dla, la, proj, wg_p)


def _tri(lower):
    r = lax.broadcasted_iota(jnp.int32, (GLA_CHUNK, GLA_CHUNK), 0)
    c = lax.broadcasted_iota(jnp.int32, (GLA_CHUNK, GLA_CHUNK), 1)
    return jnp.where((r >= c) if lower else (c >= r), 1.0, 0.0).astype(F32)


GLA_SUB = 16
GLA_NSUB = GLA_CHUNK // GLA_SUB
PAIR_QK = 2 * GLA_DK
PAIR_V = 2 * GLA_DV


def _band_selector():
    r = lax.broadcasted_iota(jnp.int32, (GLA_SUB * PAIR_QK, LANE), 0)
    c = lax.broadcasted_iota(jnp.int32, (GLA_SUB * PAIR_QK, LANE), 1)
    dist, head = r // PAIR_QK, (r % PAIR_QK) // GLA_DK
    return jnp.where(c == head * GLA_DK + (GLA_SUB - 1 - dist), 1.0, 0.0).astype(BF16)


def _flip_matrix():
    r = lax.broadcasted_iota(jnp.int32, (GLA_CHUNK, GLA_CHUNK), 0)
    c = lax.broadcasted_iota(jnp.int32, (GLA_CHUNK, GLA_CHUNK), 1)
    return jnp.where(r + c == GLA_CHUNK - 1, 1.0, 0.0).astype(BF16)


def _state_mask():
    r = lax.broadcasted_iota(jnp.int32, (PAIR_V, PAIR_QK), 0)
    c = lax.broadcasted_iota(jnp.int32, (PAIR_V, PAIR_QK), 1)
    return (r < GLA_DV) == (c < GLA_DK)


class _GlaChunk:
    def __init__(self, qs, kc, vc, g, sel):
        C = GLA_CHUNK
        self.qs, self.kc, self.vc = qs, kc, vc
        rows = lax.broadcasted_iota(jnp.int32, (C, 1), 0)
        lane = lax.broadcasted_iota(jnp.int32, (1, PAIR_QK), 1)
        self.rows, self.lane = rows, lane
        b = _dot(_tri(True), g, NN, HI)
        self.bl = b[C - 1:C, :]
        self.eb = jnp.exp(b)
        self.kdec = jnp.exp(self.bl - b)
        edge = lambda J: b[GLA_SUB * (J + 1):GLA_SUB * (J + 1) + 1, :]
        self.e_far = [jnp.exp(jnp.where(rows >= GLA_SUB * (J + 1), b - edge(J), NEG)) for J in range(GLA_NSUB - 1)]
        blk = rows // GLA_SUB
        bnext = edge(0)
        for J in range(1, GLA_NSUB - 1):
            bnext = jnp.where(blk == J, edge(J), bnext)
        self.e_khat = jnp.exp(jnp.where(blk < GLA_NSUB - 1, bnext - b, NEG))
        khat = kc * self.e_khat
        k2 = jnp.concatenate([jnp.where(lane < GLA_DK, khat, 0.0), jnp.where(lane >= GLA_DK, khat, 0.0)], axis=0)
        self.blk2 = jnp.concatenate([blk, blk], axis=0)
        self.m_far = jnp.concatenate([jnp.where(self.blk2 == J, k2, 0.0) for J in range(GLA_NSUB - 1)], axis=1).astype(BF16)
        self.qcat = jnp.concatenate([qs * e for e in self.e_far], axis=1).astype(BF16)
        a_far = _dot(self.qcat, self.m_far, NT)
        self.e_band, self.rk, hi_terms, lo_terms = [], [], [], []
        for d in range(GLA_SUB):
            rk = pltpu.roll(kc, d, 0) if d else kc
            rb = pltpu.roll(b, d, 0) if d else b
            e = jnp.exp(jnp.where(rows >= d, b - rb, NEG))
            self.e_band.append(e)
            self.rk.append(rk)
            t = (qs * e).astype(BF16).astype(F32) * rk.astype(BF16).astype(F32)
            hi = t.astype(BF16)
            hi_terms.append(hi)
            lo_terms.append((t - hi.astype(F32)).astype(BF16))
        band = _dot(jnp.concatenate(hi_terms, axis=1), sel, NN) + _dot(jnp.concatenate(lo_terms, axis=1), sel, NN)
        a_band = pltpu.roll(band, LANE - (GLA_SUB - 1), 1, stride=1, stride_axis=0)
        dist = rows - lane % GLA_DK
        self.far_mask = dist >= GLA_SUB
        self.band_mask = (dist >= 0) & (dist < GLA_SUB)
        self.a = (a_band + jnp.where(self.far_mask, a_far, 0.0)).astype(BF16)
        self.lane_v = lax.broadcasted_iota(jnp.int32, (1, PAIR_V), 1)
        self.v2 = jnp.concatenate([jnp.where(self.lane_v < GLA_DV, vc, 0.0), jnp.where(self.lane_v >= GLA_DV, vc, 0.0)],
                                  axis=0).astype(BF16)


def _gla_fwd(proj, la, tb, ride=None):
    S = proj.shape[0]
    C = GLA_CHUNK
    tb = min(tb, S)
    nbc = tb // C
    npair = GLA_HEADS // 2
    scale = GLA_DK ** -0.5

    def body(q_ref, k_ref, v_ref, la_ref, sel_ref, o_ref, st_ref, state):
        @pl.when(pl.program_id(1) == 0)
        def _():
            state[...] = jnp.zeros_like(state)

        def chunk(ci, carry):
            sl = pl.ds(pl.multiple_of(ci * C, C), C)
            ch = _GlaChunk(q_ref[sl, :] * scale, k_ref[sl, :], v_ref[sl, :], la_ref[sl, :], sel_ref[...])
            st = state[...]
            st_ref[0, ci] = st
            o_ref[sl, :] = _dot((ch.qs * ch.eb).astype(BF16), st.astype(BF16), NT) + _dot(ch.a, ch.v2, NN)
            upd = _dot(ch.vc.astype(BF16), (ch.kc * ch.kdec).astype(BF16), TN)
            state[...] = st * jnp.exp(ch.bl) + jnp.where(_state_mask(), upd, 0.0)
            return carry

        lax.fori_loop(0, nbc, chunk, 0, unroll=8)

    qspec = lambda off: pl.BlockSpec((tb, PAIR_QK), lambda p, i: (i, off // PAIR_QK + p))
    ride_arrays, ride_gather = ride if ride else ([], [])
    nr = len(ride_arrays)
    grid = (npair, S // tb)
    outs = pl.pallas_call(
        _riding(body, 5, 2, ride_gather, grid), grid=grid,
        in_specs=[qspec(O_GQ), qspec(O_GK), pl.BlockSpec((tb, PAIR_V), lambda p, i: (i, O_GV // PAIR_V + p)),
                  pl.BlockSpec((tb, PAIR_QK), lambda p, i: (i, p)),
                  pl.BlockSpec((GLA_SUB * PAIR_QK, LANE), lambda p, i: (0, 0))] + [HBM_SPEC] * nr,
        out_specs=[pl.BlockSpec((tb, PAIR_V), lambda p, i: (i, p)),
                   pl.BlockSpec((1, nbc, PAIR_V, PAIR_QK), lambda p, i: (p, i, 0, 0))] + [HBM_SPEC] * nr,
        out_shape=[jax.ShapeDtypeStruct((S, GLA_V), F32), jax.ShapeDtypeStruct((npair, S // C, PAIR_V, PAIR_QK), F32)]
        + _exchange_shapes(ride_arrays, ride_gather),
        scratch_shapes=[pltpu.VMEM((PAIR_V, PAIR_QK), F32)] + (_exchange_sems(nr) if nr else []),
        compiler_params=_cp("arbitrary", "arbitrary"), name="gla_fwd")(proj, proj, proj, la, _band_selector(), *ride_arrays)
    return outs[0], outs[1], outs[2:]


def _gla_bwd(proj, la, do, states, tb, ride=None):
    S = proj.shape[0]
    C = GLA_CHUNK
    tb = min(tb, S)
    nbc = tb // C
    nblk = S // tb
    npair = GLA_HEADS // 2
    scale = GLA_DK ** -0.5

    def body(q_ref, k_ref, v_ref, la_ref, do_ref, st_ref, sel_ref, selt_ref, dq_ref, dk_ref, dv_ref, dla_ref, dstate):
        @pl.when(pl.program_id(1) == 0)
        def _():
            dstate[...] = jnp.zeros_like(dstate)

        def chunk(cc, carry):
            ci = nbc - 1 - cc
            sl = pl.ds(pl.multiple_of(ci * C, C), C)
            ch = _GlaChunk(q_ref[sl, :] * scale, k_ref[sl, :], v_ref[sl, :], la_ref[sl, :], sel_ref[...])
            qs, kc, rows = ch.qs, ch.kc, ch.rows
            doc_b = do_ref[sl, :].astype(BF16)
            st = st_ref[0, ci]
            dst = dstate[...]
            dst_b = dst.astype(BF16)
            ebl = jnp.exp(ch.bl)
            dq = _dot(doc_b, st.astype(BF16), NN) * ch.eb
            dk = _dot(ch.vc.astype(BF16), dst_b, NN) * ch.kdec
            dv = _dot((kc * ch.kdec).astype(BF16), dst_b, NT)
            dbl = jnp.sum(dst * st, axis=0, keepdims=True) * ebl + jnp.sum(kc * dk, axis=0, keepdims=True)
            da = _dot(doc_b, ch.v2, NT)
            dv2 = _dot(ch.a, doc_b, TN)
            dv = dv + jnp.where(ch.lane_v < GLA_DV, dv2[:C], dv2[C:])
            da_far = jnp.where(ch.far_mask, da, 0.0).astype(BF16)
            dqcat = _dot(da_far, ch.m_far, NN)
            dm = _dot(da_far, ch.qcat, TN)
            dk2 = jnp.zeros((2 * C, PAIR_QK), F32)
            for J in range(GLA_NSUB - 1):
                dq = dq + dqcat[:, J * PAIR_QK:(J + 1) * PAIR_QK] * ch.e_far[J]
                dk2 = dk2 + jnp.where(ch.blk2 == J, dm[:, J * PAIR_QK:(J + 1) * PAIR_QK], 0.0)
            dk = dk + jnp.where(ch.lane < GLA_DK, dk2[:C], dk2[C:]) * ch.e_khat
            flip = _flip_matrix()
            da_band = _dot(flip, jnp.where(ch.band_mask, da, 0.0).astype(BF16), NN)
            dband = pltpu.roll(da_band, LANE - (C - GLA_SUB), 1, stride=1, stride_axis=0)
            dband = _dot(flip, dband.astype(BF16), NN)
            dterms = _dot(dband.astype(BF16), selt_ref[...], NN)
            for d in range(GLA_SUB):
                dt = dterms[:, d * PAIR_QK:(d + 1) * PAIR_QK]
                dq = dq + dt * (ch.rk[d] * ch.e_band[d])
                dkr = dt * (qs * ch.e_band[d])
                dk = dk + (pltpu.roll(dkr, C - d, 0) if d else dkr)
            db = qs * dq - kc * dk
            db = jnp.where(rows == C - 1, db + dbl, db)
            dq_ref[sl, :] = (dq * scale).astype(BF16)
            dk_ref[sl, :] = dk.astype(BF16)
            dv_ref[sl, :] = dv.astype(BF16)
            dla_ref[sl, :] = _dot(_tri(False), db, NN, HI)
            upd = _dot(doc_b, (qs * ch.eb).astype(BF16), TN)
            dstate[...] = dst * ebl + jnp.where(_state_mask(), upd, 0.0)
            return carry

        lax.fori_loop(0, nbc, chunk, 0, unroll=8)

    rev = lambda i: nblk - 1 - i
    qspec = lambda off: pl.BlockSpec((tb, PAIR_QK), lambda p, i: (rev(i), off // PAIR_QK + p))
    pair_qk = pl.BlockSpec((tb, PAIR_QK), lambda p, i: (rev(i), p))
    pair_v = pl.BlockSpec((tb, PAIR_V), lambda p, i: (rev(i), p))
    sel = _band_selector()
    ride_arrays, ride_gather = ride if ride else ([], [])
    nr = len(ride_arrays)
    grid = (npair, nblk)
    outs = pl.pallas_call(
        _riding(body, 8, 4, ride_gather, grid), grid=grid,
        in_specs=[qspec(O_GQ), qspec(O_GK), pl.BlockSpec((tb, PAIR_V), lambda p, i: (rev(i), O_GV // PAIR_V + p)),
                  pair_qk, pair_v, pl.BlockSpec((1, nbc, PAIR_V, PAIR_QK), lambda p, i: (p, rev(i), 0, 0)),
                  pl.BlockSpec((GLA_SUB * PAIR_QK, LANE), lambda p, i: (0, 0)),
                  pl.BlockSpec((LANE, GLA_SUB * PAIR_QK), lambda p, i: (0, 0))] + [HBM_SPEC] * nr,
        out_specs=[pair_qk, pair_qk, pair_v, pair_qk] + [HBM_SPEC] * nr,
        out_shape=[jax.ShapeDtypeStruct((S, GLA_QK), BF16), jax.ShapeDtypeStruct((S, GLA_QK), BF16),
                   jax.ShapeDtypeStruct((S, GLA_V), BF16), jax.ShapeDtypeStruct((S, GLA_QK), F32)]
        + _exchange_shapes(ride_arrays, ride_gather),
        scratch_shapes=[pltpu.VMEM((PAIR_V, PAIR_QK), F32)] + (_exchange_sems(nr) if nr else []),
        compiler_params=_cp("arbitrary", "arbitrary"), name="gla_bwd")(proj, proj, proj, la, do, states, sel, sel.T, *ride_arrays)
    return outs[0], outs[1], outs[2], outs[3], outs[4:]


def _gla_out(o, proj, gng, ts):
    S = o.shape[0]

    def body(o_ref, gr_ref, g_ref, y_ref):
        for h in range(GLA_HEADS):
            cols = slice(h * GLA_DV, (h + 1) * GLA_DV)
            ov, grv = o_ref[:, cols], gr_ref[:, cols]
            r = lax.rsqrt(jnp.mean(ov * ov, axis=-1, keepdims=True) + EPS)
            y_ref[:, cols] = (ov * r * g_ref[...] * (grv * _sigmoid(grv))).astype(BF16)

    return pl.pallas_call(
        body, grid=(S // ts,),
        in_specs=[pl.BlockSpec((ts, GLA_V), lambda i: (i, 0)), pl.BlockSpec((ts, GLA_V), lambda i: (i, O_GR // GLA_V)),
                  pl.BlockSpec((1, GLA_DV), lambda i: (0, 0))],
        out_specs=pl.BlockSpec((ts, GLA_V), lambda i: (i, 0)), out_shape=jax.ShapeDtypeStruct((S, GLA_V), BF16),
        compiler_params=_cp("parallel"), name="gla_out_fwd")(o, proj, gng)


def _mm_mixed_bwd(dt1, w_out, o, proj, gng, y_att, tm):
    S = o.shape[0]
    W = ATTN_DIM

    def epilogue(dm, step, o_ref, gr_ref, g_ref, y_ref, do_ref, dgr_ref, gg_ref, dy_ref, de_ref):
        gsum = jnp.zeros((1, GLA_DV), F32)
        for h in range(GLA_HEADS):
            cols = slice(h * GLA_DV, (h + 1) * GLA_DV)
            ov, grv, dy = o_ref[:, cols], gr_ref[:, cols], dm[:, cols]
            r = lax.rsqrt(jnp.mean(ov * ov, axis=-1, keepdims=True) + EPS)
            oh = ov * r
            sg = _sigmoid(grv)
            don = dy * (grv * sg)
            dgr_ref[:, cols] = (dy * (oh * g_ref[...]) * (sg * (1.0 + grv * (1.0 - sg)))).astype(BF16)
            gsum = gsum + jnp.sum(don * oh, axis=0, keepdims=True)
            doh = don * g_ref[...]
            do_ref[:, cols] = r * (doh - oh * jnp.mean(doh * oh, axis=-1, keepdims=True))
        _accumulate(gg_ref, _rows8([gsum], GLA_DV), step)
        dya = dm[:, GLA_V:]
        dy_ref[...] = dya
        de_ref[...] = _seg_sum(dya * y_ref[...], _seg_matrix(W, ATTN_HD, 1.0))

    half = pl.BlockSpec((tm, GLA_V), lambda i: (i, 0))
    outs, _ = _mm_rows(
        dt1, w_out, NT, tm, [o, proj, gng, y_att],
        [half, pl.BlockSpec((tm, GLA_V), lambda i: (i, O_GR // GLA_V)), pl.BlockSpec((1, GLA_DV), lambda i: (0, 0)), half],
        [jax.ShapeDtypeStruct((S, GLA_V), F32), jax.ShapeDtypeStruct((S, GLA_V), BF16), jax.ShapeDtypeStruct((8, GLA_DV), F32),
         jax.ShapeDtypeStruct((S, W), F32), jax.ShapeDtypeStruct((S, W), F32)],
        [half, half, pl.BlockSpec((8, GLA_DV), lambda i: (0, 0)), half, half], epilogue, "mm_dmixed")
    return outs


def _seg_matrix(width, seg, value):
    r = lax.broadcasted_iota(jnp.int32, (width, width), 0) // seg
    c = lax.broadcasted_iota(jnp.int32, (width, width), 1) // seg
    return jnp.where(r == c, value, 0.0).astype(BF16)


def _seg_sum(x, seg_matrix):
    hi = x.astype(BF16)
    lo = (x - hi.astype(F32)).astype(BF16)
    return _dot(hi, seg_matrix, NN) + _dot(lo, seg_matrix, NN)


ATTN_GROUP = 4


ATTN_TILE = max(DILATIONS) * ATTN_BLOCK


def _attn_rows(d, g, r, base=0):
    start = base + (g * d * ATTN_BLOCK if g >= 0 else ATTN_TILE - d * ATTN_BLOCK) + r
    return pl.ds(start, ATTN_BLOCK) if d == 1 else pl.ds(start, ATTN_BLOCK, stride=d)


def _for_blocks(d, G, fn):
    for g in range(G):
        if d <= ATTN_GROUP:
            for r in range(d):
                fn(g, r)
        else:
            def step(r, carry, g=g):
                fn(g, r)
                return carry
            lax.fori_loop(0, d, step, 0, unroll=ATTN_GROUP)


def _attn_specs(S):
    nb = S // ATTN_TILE

    def specs(off=0):
        return [pl.BlockSpec((ATTN_TILE, LANE), lambda hp, n: (n, off + hp)),
                pl.BlockSpec((ATTN_TILE, LANE), lambda hp, n: (jnp.maximum(n - 1, 0), off + hp)),
                pl.BlockSpec((ATTN_TILE, LANE), lambda hp, n: (jnp.minimum(n + 1, nb - 1), off + hp))]

    return nb, specs


def _attn_bias_tables():
    B = ATTN_BLOCK
    slope = (2.0 ** -(jnp.arange(ATTN_HEADS, dtype=F32) + 1.0))[None, :, None, None]
    dil = jnp.asarray(DILATIONS, F32)[:, None, None, None]

    def bias(rel):
        return jnp.where((rel >= 0) & (rel <= B), -slope * (dil * rel.astype(F32)), NEG)

    iq, ik = jnp.arange(B)[:, None], jnp.arange(2 * B)[None, :]
    full = bias(iq + B - ik)
    rows = jnp.stack([full, jnp.where(ik >= B, full, NEG)])
    ikt, iqt = jnp.arange(B)[:, None], jnp.arange(B)[None, :]
    cols = jnp.stack([bias(iqt - ikt + nxt * B) for nxt in range(2)], axis=2)
    return rows, cols


def _attn_row_bias(bias_ref, b, h, g, n):
    full = bias_ref[0, b, h]
    return jnp.where(n == 0, bias_ref[1, b, h], full) if g == 0 else full


def _attn_bias_specs():
    B = ATTN_BLOCK
    nd = len(DILATIONS)
    return (pl.BlockSpec((2, nd, 2, B, 2 * B), lambda hp, n: (0, 0, hp, 0, 0)),
            pl.BlockSpec((nd, 2, 2, B, B), lambda hp, n: (0, hp, 0, 0, 0)))


def _attn_fwd(qn, kn, proj, bias_rows):
    S, W = qn.shape
    T = ATTN_TILE
    nb, specs = _attn_specs(S)

    def body(q_ref, kp_ref, kc_ref, vp_ref, vc_ref, bias_ref, y_ref, l_ref, o_scr, l_scr):
        hp, n = pl.program_id(0), pl.program_id(1)
        lo = lax.broadcasted_iota(jnp.int32, (1, LANE), 1) < ATTN_HD
        for b, d in enumerate(DILATIONS):
            def sub(g, r, b=b, d=d):
                rows, before = _attn_rows(d, g, r), _attn_rows(d, g - 1, r)
                kb_ref, vb_ref = (kp_ref, vp_ref) if g == 0 else (kc_ref, vc_ref)
                qv = q_ref[rows, :].astype(BF16)
                kv = jnp.concatenate([kb_ref[before, :], kc_ref[rows, :]], axis=0).astype(BF16)
                vv = jnp.concatenate([vb_ref[before, :], vc_ref[rows, :]], axis=0).astype(BF16)
                outs, lses = [], []
                for h in range(2):
                    qm = jnp.where(lo == (h == 0), qv, jnp.zeros_like(qv))
                    s = _dot(qm, kv, NT) + _attn_row_bias(bias_ref, b, h, g, n)
                    m = jnp.max(s, axis=-1, keepdims=True)
                    p = jnp.exp(s - m)
                    den = jnp.sum(p, axis=-1, keepdims=True)
                    outs.append(_dot(p.astype(BF16), vv, NN) / den)
                    lses.append(m + jnp.log(den))
                kept = _attn_rows(d, g, r, base=b * T)
                o_scr[kept, :] = jnp.where(lo, outs[0], outs[1])
                l_scr[kept, :] = jnp.where(lo, lses[0], lses[1])

            _for_blocks(d, T // (d * ATTN_BLOCK), sub)
        l1, l2, l3 = [l_scr[pl.ds(b * T, T), :] for b in range(len(DILATIONS))]
        o1, o2, o3 = [o_scr[pl.ds(b * T, T), :] for b in range(len(DILATIONS))]
        m = jnp.maximum(jnp.maximum(l1, l2), l3)
        e1, e2, e3 = jnp.exp(l1 - m), jnp.exp(l2 - m), jnp.exp(l3 - m)
        tot = e1 + e2 + e3
        y_ref[...] = (e1 * o1 + e2 * o2 + e3 * o3) / tot
        l_ref[...] = m + jnp.log(tot)

    cur, prev, _ = specs()
    vcur, vprev, _ = specs(O_AV // LANE)
    return pl.pallas_call(
        body, grid=(W // LANE, nb), in_specs=[cur, prev, cur, vprev, vcur, _attn_bias_specs()[0]], out_specs=[cur, cur],
        out_shape=[jax.ShapeDtypeStruct((S, W), F32)] * 2,
        scratch_shapes=[pltpu.VMEM((len(DILATIONS) * T, LANE), F32)] * 2,
        compiler_params=_cp("parallel", "arbitrary"), name="attn_fwd")(qn, kn, kn, proj, proj, bias_rows)


def _attn_mix(y_gla, y_att, ts):
    S, W = y_att.shape

    def body(yg, ya, mixed_ref, mixed_t_ref):
        y = ya[...]
        mixed_ref[:, :W] = yg[...]
        mixed_ref[:, W:] = y.astype(BF16)
        mixed_t_ref[:W, :] = yg[...].astype(F32).T.astype(BF16)
        mixed_t_ref[W:, :] = y.T.astype(BF16)

    spec = pl.BlockSpec((ts, W), lambda i: (i, 0))
    return pl.pallas_call(
        body, grid=(S // ts,), in_specs=[spec] * 2,
        out_specs=[pl.BlockSpec((ts, 2 * W), lambda i: (i, 0)), _col_spec(2 * W, ts)],
        out_shape=[jax.ShapeDtypeStruct((S, 2 * W), BF16), jax.ShapeDtypeStruct((2 * W, S), BF16)],
        compiler_params=_cp("parallel"), name="attn_mix")(y_gla, y_att)


def _attn_dq(qn, kn, proj, dy, lse, delta, bias_rows):
    S, W = qn.shape
    nb, specs = _attn_specs(S)

    def body(q_ref, kp_ref, kc_ref, vp_ref, vc_ref, dy_ref, l_ref, de_ref, bias_ref, dq_ref):
        n = pl.program_id(1)
        lo = lax.broadcasted_iota(jnp.int32, (1, LANE), 1) < ATTN_HD
        for b, d in enumerate(DILATIONS):
            _attn_dq_branch(b, d, n, bias_ref, lo, q_ref, kp_ref, kc_ref, vp_ref, vc_ref, dy_ref, l_ref, de_ref, dq_ref)

    cur, prev, _ = specs()
    vcur, vprev, _ = specs(O_AV // LANE)
    return pl.pallas_call(
        body, grid=(W // LANE, nb), in_specs=[cur, prev, cur, vprev, vcur, cur, cur, cur, _attn_bias_specs()[0]], out_specs=cur,
        out_shape=jax.ShapeDtypeStruct((S, W), F32),
        compiler_params=_cp("parallel", "arbitrary"), name="attn_dq")(qn, kn, kn, proj, proj, dy, lse, delta, bias_rows)


def _attn_dq_branch(b, d, n, bias_ref, lo, q_ref, kp_ref, kc_ref, vp_ref, vc_ref, dy_ref, l_ref, de_ref, dq_ref):
    def sub(g, r):
        rows, before = _attn_rows(d, g, r), _attn_rows(d, g - 1, r)
        kb_ref, vb_ref = (kp_ref, vp_ref) if g == 0 else (kc_ref, vc_ref)
        qv, dyv = q_ref[rows, :].astype(BF16), dy_ref[rows, :]
        lv, dev = l_ref[rows, :], de_ref[rows, :]
        kv = jnp.concatenate([kb_ref[before, :], kc_ref[rows, :]], axis=0).astype(BF16)
        vv = jnp.concatenate([vb_ref[before, :], vc_ref[rows, :]], axis=0).astype(BF16)
        outs = []
        for h in range(2):
            sel = lo == (h == 0)
            qm = jnp.where(sel, qv, jnp.zeros_like(qv))
            dym = jnp.where(sel, dyv, 0.0).astype(BF16)
            lse_h = lv[:, h * ATTN_HD:h * ATTN_HD + 1]
            del_h = dev[:, h * ATTN_HD:h * ATTN_HD + 1]
            p = jnp.exp(_dot(qm, kv, NT) + _attn_row_bias(bias_ref, b, h, g, n) - lse_h)
            ds = p * (_dot(dym, vv, NT) - del_h)
            outs.append(_dot(ds.astype(BF16), kv, NN) * (ATTN_HD ** -0.5))
        dq = jnp.where(lo, outs[0], outs[1])
        dq_ref[rows, :] = dq if b == 0 else dq_ref[rows, :] + dq

    _for_blocks(d, ATTN_TILE // (d * ATTN_BLOCK), sub)


def _attn_dkv(qn, kn, proj, dy, lse, delta, bias_cols):
    S, W = qn.shape
    nb, specs = _attn_specs(S)

    def body(k_ref, v_ref, qc_ref, qn_ref, dyc_ref, dyn_ref, lc_ref, ln_ref, dec_ref, den_ref, bias_ref, dk_ref, dv_ref):
        n = pl.program_id(1)
        lo = lax.broadcasted_iota(jnp.int32, (1, LANE), 1) < ATTN_HD
        cur_refs, next_refs = (qc_ref, dyc_ref, lc_ref, dec_ref), (qn_ref, dyn_ref, ln_ref, den_ref)
        for b, d in enumerate(DILATIONS):
            _attn_dkv_branch(b, d, n + 1 < nb, bias_ref, lo, k_ref, v_ref, cur_refs, next_refs, dk_ref, dv_ref)

    cur, _, nxt = specs()
    vcur, _, _ = specs(O_AV // LANE)
    return pl.pallas_call(
        body, grid=(W // LANE, nb), in_specs=[cur, vcur, cur, nxt, cur, nxt, cur, nxt, cur, nxt, _attn_bias_specs()[1]],
        out_specs=[cur, cur], out_shape=[jax.ShapeDtypeStruct((S, W), F32)] * 2,
        compiler_params=_cp("parallel", "arbitrary"), name="attn_dkv")(
            kn, proj, qn, qn, dy, dy, lse, lse, delta, delta, bias_cols)


def _attn_dkv_branch(b, d, has_next, bias_ref, lo, k_ref, v_ref, cur_refs, next_refs, dk_ref, dv_ref):
    B = ATTN_BLOCK
    G = ATTN_TILE // (d * B)

    def sub(g, r):
        rows = _attn_rows(d, g, r)
        kv, vv = k_ref[rows, :].astype(BF16), v_ref[rows, :].astype(BF16)
        dk = jnp.zeros((B, LANE), F32)
        dv = jnp.zeros((B, LANE), F32)
        inside = g + 1 < G
        after = _attn_rows(d, g + 1 if inside else 0, r)
        for nxt, qrows, (q_ref, dy_ref, l_ref, de_ref) in ((0, rows, cur_refs), (1, after, cur_refs if inside else next_refs)):
            qv, dyv = q_ref[qrows, :].astype(BF16), dy_ref[qrows, :]
            lt, det = l_ref[qrows, :].T, de_ref[qrows, :].T
            for h in range(2):
                sel = lo == (h == 0)
                qm = jnp.where(sel, qv, jnp.zeros_like(qv))
                dym = jnp.where(sel, dyv, 0.0).astype(BF16)
                lse_h = lt[h * ATTN_HD:h * ATTN_HD + 1, :]
                del_h = det[h * ATTN_HD:h * ATTN_HD + 1, :]
                bias = bias_ref[b, h, nxt]
                if nxt and not inside:
                    bias = jnp.where(has_next, bias, NEG)
                pt = jnp.exp(_dot(kv, qm, NT) + bias - lse_h)
                dv = dv + _dot(pt.astype(BF16), dym, NN)
                dst = pt * (_dot(vv, dym, NT) - del_h)
                dk = dk + _dot(dst.astype(BF16), qm, NN)
        dk_ref[rows, :] = dk if b == 0 else dk_ref[rows, :] + dk
        dv_ref[rows, :] = dv if b == 0 else dv_ref[rows, :] + dv

    _for_blocks(d, G, sub)


def _attn_post(dq, dk, dv, proj, qg, kg, ts):
    S = proj.shape[0]
    W = ATTN_DIM

    def body(dq_ref, dk_ref, dv_ref, aq_ref, ak_ref, qg_ref, kg_ref, daq_ref, dak_ref, dav_ref, gg_ref):
        i = pl.program_id(0)
        seg = _seg_matrix(W, ATTN_HD, 1.0 / ATTN_HD)
        gsums = []
        for d_ref, x_ref, g_ref, o_ref in ((dq_ref, aq_ref, qg_ref, daq_ref), (dk_ref, ak_ref, kg_ref, dak_ref)):
            dy = d_ref[...]
            xv = x_ref[...]
            r = lax.rsqrt(_seg_sum(xv * xv, seg) + EPS)
            xh = xv * r
            dxh = dy * g_ref[...]
            o_ref[...] = (r * (dxh - xh * _seg_sum(dxh * xh, seg))).astype(BF16)
            gsums.append(jnp.sum(dy * xh, axis=0, keepdims=True))
        dav_ref[...] = dv_ref[...].astype(BF16)
        _accumulate(gg_ref, _rows8(gsums, W), i)

    row = pl.BlockSpec((ts, W), lambda i: (i, 0))
    blk = lambda off: pl.BlockSpec((ts, W), lambda i: (i, off // W))
    vec = pl.BlockSpec((1, W), lambda i: (0, 0))
    return pl.pallas_call(
        body, grid=(S // ts,), in_specs=[row] * 3 + [blk(O_AQ), blk(O_AK), vec, vec],
        out_specs=[row, row, row, pl.BlockSpec((8, W), lambda i: (0, 0))],
        out_shape=[jax.ShapeDtypeStruct((S, W), BF16)] * 3 + [jax.ShapeDtypeStruct((8, W), F32)],
        compiler_params=_cp("arbitrary"), name="attn_post")(dq, dk, dv, proj, proj, qg, kg)


def _shift_down(cur, halo, n):
    return pltpu.roll(jnp.concatenate([halo, cur], axis=0), n, 0)[8:]


def _shift_up(cur, halo, n):
    ts = cur.shape[0]
    return pltpu.roll(jnp.concatenate([cur, halo], axis=0), ts + 8 - n, 0)[:ts]


def _conv(cur, halo, w, b):
    return b + w[0:1, :] * _shift_down(cur, halo, 2) + w[1:2, :] * _shift_down(cur, halo, 1) + w[2:3, :] * cur


def _mm_up_swiglu(h2, w_up, conv_w8, conv_b, tm, tc, ride=None):
    S, D = h2.shape
    F = w_up.shape[1] // 2
    nc = F // tc
    grid = (S // tm, nc)
    ride_arrays, ride_gather = ride if ride else ([], [])
    nr = len(ride_arrays)

    def body(h_ref, bg_ref, bv_ref, wg_ref, wv_ref, cg_ref, cv_ref, u0_ref, a_ref, at_ref, halo):
        i, j = pl.program_id(0), pl.program_id(1)
        hv = h_ref[...]
        acts = []
        for h, (b_ref, w_ref, c_ref) in enumerate(((bg_ref, wg_ref, cg_ref), (bv_ref, wv_ref, cv_ref))):
            u = _dot(hv, b_ref[...], NN)
            u0_ref[h] = u
            acts.append(_conv(u, jnp.where(i == 0, 0.0, halo[j, h]), w_ref[...], c_ref[...]))
            halo[j, h] = u[tm - 8:, :]
        g, v = acts
        a = g * _sigmoid(g) * v
        a_ref[...] = a.astype(BF16)
        at_ref[...] = a.T.astype(BF16)

    wcol = lambda rows, off: pl.BlockSpec((rows, tc), lambda i, j: (0, j + off))
    outs = pl.pallas_call(
        _riding(body, 7, 3, ride_gather, grid), grid=grid,
        in_specs=[pl.BlockSpec((tm, D), lambda i, j: (i, 0)), wcol(D, 0), wcol(D, nc), wcol(8, 0), wcol(8, nc), wcol(1, 0), wcol(1, nc)]
        + [HBM_SPEC] * nr,
        out_specs=[pl.BlockSpec((2, tm, tc), lambda i, j: (0, i, j)), pl.BlockSpec((tm, tc), lambda i, j: (i, j)),
                   pl.BlockSpec((tc, tm), lambda i, j: (j, i))] + [HBM_SPEC] * nr,
        out_shape=[jax.ShapeDtypeStruct((2, S, F), F32), jax.ShapeDtypeStruct((S, F), BF16), jax.ShapeDtypeStruct((F, S), BF16)]
        + _exchange_shapes(ride_arrays, ride_gather),
        scratch_shapes=[pltpu.VMEM((nc, 2, 8, tc), F32)] + (_exchange_sems(nr) if nr else []),
        compiler_params=_cp("arbitrary", "arbitrary"), name="mm_up")(
            h2, w_up, w_up, conv_w8, conv_w8, conv_b, conv_b, *ride_arrays)
    return outs[0], outs[1], outs[2], outs[3:]


def _mm_da_du0(dt2, w_down, u0, conv_w8, conv_b, tm, tc, ride=None):
    _, S, F = u0.shape
    D = dt2.shape[1]
    hb = tm // 8
    nrow = S // tm
    grid = (nrow,)
    ride_arrays, ride_gather = ride if ride else ([], [])
    nr = len(ride_arrays)

    def body(dt_ref, wd_ref, ug_ref, ugh_ref, uv_ref, uvh_ref, w_ref, b_ref, o_ref, sg_ref, sv_ref, following):
        i = pl.program_id(0)
        at_start, at_end = i == nrow - 1, i == 0
        dt = dt_ref[...]
        for c in range(F // tc):
            sums = []
            halves = []
            for h, (u_ref, h_ref) in enumerate(((ug_ref, ugh_ref), (uv_ref, uvh_ref))):
                cols = slice(h * F + c * tc, h * F + (c + 1) * tc)
                u, halo, w = u_ref[:, c * tc:(c + 1) * tc], jnp.where(at_start, 0.0, h_ref[:, c * tc:(c + 1) * tc]), w_ref[:, cols]
                s2, s1 = _shift_down(u, halo, 2), _shift_down(u, halo, 1)
                halves.append((b_ref[:, cols] + w[0:1, :] * s2 + w[1:2, :] * s1 + w[2:3, :] * u, s2, s1, u, w, cols))
            g, v = halves[0][0], halves[1][0]
            dav = _dot(dt, wd_ref[c * tc:(c + 1) * tc, :], NT)
            sig = _sigmoid(g)
            dus = (dav * v * (sig * (1.0 + g * (1.0 - sig))), dav * (g * sig))
            for h, du in enumerate(dus):
                _, s2, s1, u, w, cols = halves[h]
                after = jnp.where(at_end, 0.0, following[h, :, c * tc:(c + 1) * tc])
                o_ref[:, cols] = (w[2:3, :] * du + w[1:2, :] * _shift_up(du, after, 1) + w[0:1, :] * _shift_up(du, after, 2)).astype(BF16)
                following[h, :, c * tc:(c + 1) * tc] = du[0:8, :]
                sums.append(_rows8([jnp.sum(du * s2, axis=0, keepdims=True), jnp.sum(du * s1, axis=0, keepdims=True),
                                    jnp.sum(du * u, axis=0, keepdims=True), jnp.sum(du, axis=0, keepdims=True)], tc))
            for sums_ref, part in zip((sg_ref, sv_ref), sums):
                @pl.when(i == 0)
                def _(sums_ref=sums_ref, part=part, c=c):
                    sums_ref[:, c * tc:(c + 1) * tc] = part

                @pl.when(i > 0)
                def _(sums_ref=sums_ref, part=part, c=c):
                    sums_ref[:, c * tc:(c + 1) * tc] += part

    rev = lambda i: nrow - 1 - i
    main = lambda h: pl.BlockSpec((None, tm, F), lambda i: (h, rev(i), 0))
    halo = lambda h: pl.BlockSpec((None, 8, F), lambda i: (h, jnp.maximum(rev(i) * hb - 1, 0), 0))
    whole = lambda a: pl.BlockSpec(a.shape, lambda i: (0,) * a.ndim, pipeline_mode=pl.Buffered(1))
    sums_spec = pl.BlockSpec((8, F), lambda i: (0, 0))
    outs = pl.pallas_call(
        _riding(body, 8, 3, ride_gather, grid), grid=grid,
        in_specs=[pl.BlockSpec((tm, D), lambda i: (rev(i), 0)), whole(w_down), main(0), halo(0), main(1), halo(1),
                  whole(conv_w8), whole(conv_b)] + [HBM_SPEC] * nr,
        out_specs=[pl.BlockSpec((tm, 2 * F), lambda i: (rev(i), 0)), sums_spec, sums_spec] + [HBM_SPEC] * nr,
        out_shape=[jax.ShapeDtypeStruct((S, 2 * F), BF16), jax.ShapeDtypeStruct((8, F), F32), jax.ShapeDtypeStruct((8, F), F32)]
        + _exchange_shapes(ride_arrays, ride_gather),
        scratch_shapes=[pltpu.VMEM((2, 8, F), F32)] + (_exchange_sems(nr) if nr else []),
        compiler_params=_cp("arbitrary"), name="mm_da")(
            dt2, w_down, u0, u0, u0, u0, conv_w8, conv_b, *ride_arrays)
    return outs[0], outs[1], outs[2], outs[3:]


def _adamw(w, g, m, v, name, slots=False):
    shape = w.shape
    view = (math.prod(shape[:-1]), shape[-1])
    R, C = view
    limit = SUM_BLOCK_ELEMS // 2 if slots else SUM_BLOCK_ELEMS
    fits = [t for t in range(16, R + 1, 16) if R % t == 0 and t * C <= limit]
    tr = max(fits) if fits else R

    def body(w_ref, g_ref, m_ref, v_ref, *outs):
        if slots:
            gv = g_ref[0].astype(F32)
            for s in range(1, N_DEV):
                gv = gv + g_ref[s].astype(F32)
            outs[0][...] = gv
        else:
            gv = g_ref[...]
        d_ref, nm_ref, nv_ref = outs[-3:]
        nm = ADAM_B1 * m_ref[...] + (1.0 - ADAM_B1) * gv
        nv = ADAM_B2 * v_ref[...] + (1.0 - ADAM_B2) * (gv * gv)
        m_hat = nm / (1.0 - ADAM_B1 ** ADAM_STEP)
        v_hat = nv / (1.0 - ADAM_B2 ** ADAM_STEP)
        d_ref[...] = -ADAM_LR * (m_hat / (jnp.sqrt(v_hat) + ADAM_EPS) + ADAM_WD * w_ref[...])
        nm_ref[...] = nm
        nv_ref[...] = nv

    spec = pl.BlockSpec((tr, C), lambda i: (i, 0))
    g_spec = pl.BlockSpec((N_DEV, tr, C), lambda i: (0, i, 0)) if slots else spec
    n_out = 4 if slots else 3
    outs = pl.pallas_call(
        body, grid=(R // tr,), in_specs=[spec, g_spec, spec, spec], out_specs=[spec] * n_out,
        out_shape=[jax.ShapeDtypeStruct(view, F32)] * n_out, compiler_params=_cp("parallel"), name=name)(
            w.reshape(view), g if slots else g.reshape(view), m.reshape(view), v.reshape(view))
    outs = [o.reshape(shape) for o in outs]
    return outs if slots else [g.reshape(shape)] + outs


def _pad_rows8(a):
    return jnp.concatenate([a, jnp.zeros((8 - a.shape[0], a.shape[1]), a.dtype)], axis=0)


def _local_step(x, target, mod, n1g, w_in_s, conv_w_s, wg_s, bg, gng, qng, kng, w_out_s, n2g, w_up_s, conv_b, w_down_s):
    S, D = x.shape
    F = w_down_s.shape[0] * N_DEV
    cw_c, wg_c = conv_w_s.shape[1], wg_s.shape[1]
    ts = min(512, S)
    sh1, sc1, g1, sh2, sc2, g2 = [mod[i:i + 1] for i in range(6)]
    qg_t, kg_t = jnp.tile(qng, (1, ATTN_HEADS)), jnp.tile(kng, (1, ATTN_HEADS))

    small = jnp.concatenate([conv_w_s.reshape(1, -1), wg_s.reshape(1, -1)], axis=1)
    n_small = small.shape[1]
    small = jnp.pad(small, ((0, 0), (0, -n_small % LANE)))
    h1, h1_t, (g_in, g_small) = _rms_mod(x, n1g, sc1, sh1, ts, "rms_mod1", ride=([w_in_s, small], [VIA_SIBLING, True]))
    w_in_full = _cols_from_blocks(g_in)
    w_in_p = jnp.concatenate([w_in_full[:, :GLR_SRC], w_in_full[:, GLR_SRC + GLA_RANK:],
                              w_in_full[:, GLR_SRC:GLR_SRC + GLA_RANK], jnp.zeros((D, PROJ_W - O_GLR - GLA_RANK), BF16)], axis=1)
    g_small = g_small.reshape(N_DEV, -1)
    conv_w8 = _pad_rows8(jnp.stack([g_small[:, t * cw_c:(t + 1) * cw_c].reshape(-1) for t in range(3)]))
    wg_full = _cols_from_blocks(g_small[:, 3 * cw_c:n_small].reshape(N_DEV, GLA_RANK, wg_c))
    wg_p = jnp.concatenate([wg_full, jnp.zeros((LANE - GLA_RANK, wg_full.shape[1]), F32)], axis=0)
    (proj, la, qn, kn), (g_out,) = _mm_in(h1, w_in_p, wg_p, bg, qg_t, kg_t, ts, ride=([w_out_s], [True]))
    w_out = g_out.reshape(-1, D)
    o_gla, states, (g_up,) = _gla_fwd(proj, la, 512, ride=([w_up_s], [True]))
    w_up = _cols_from_blocks(g_up)
    y_gla = _gla_out(o_gla, proj, gng, ts)
    bias_rows, bias_cols = _attn_bias_tables()
    y_att, lse = _attn_fwd(qn, kn, proj, bias_rows)
    mixed, mixed_t = _attn_mix(y_gla, y_att, ts)
    t1, x2, h2, h2_t = _mm_resid_rms_mod(mixed, w_out, x, g1, n2g, sc2, sh2, ts, "mm_out")
    tc = 1408 if F % 1408 == 0 else F
    u0, a, a_t, (g_down,) = _mm_up_swiglu(h2, w_up, conv_w8, conv_b, ts, tc, ride=([w_down_s], [True]))
    w_down = g_down.reshape(F, D)
    dx3, dt2, sums3 = _mm_loss_resid(a, w_down, x2, g2, target, ts, "mm_down")
    loss_row, dg2 = sums3[0:1], sums3[1:2]

    g_w_down = _mm(a_t, dt2, NN, 1408, 1024, 2048, F32, "mm_gw_down")
    du0, sums_g, sums_v, (r_down,) = _mm_da_du0(dt2, w_down, u0, conv_w8, conv_b, min(256, S), tc,
                                                ride=([g_w_down.reshape(N_DEV, -1, D)], [False]))
    g_conv_w = jnp.concatenate([sums_g[0:3], sums_v[0:3]], axis=1)
    g_conv_b = jnp.concatenate([sums_g[3:4], sums_v[3:4]], axis=1)
    g_w_up = _mm(h2_t, du0, NN, 512, 2816, 2048, F32, "mm_gw_up")
    (dx2, sums2, dt1), _ = _mm_rms_mod_bwd(du0, w_up, x2, dx3, n2g, sc2, ts, "mm_dh2", t_prev=t1, g_prev=g1)
    dsh2, dsc2, g_n2g, dg1 = sums2[0:1], sums2[1:2], sums2[2:3], sums2[3:4]
    g_w_out = _mm(mixed_t, dt1, NN, 1024, 1024, 2048, F32, "mm_gw_out")
    do_gla, dgr, gng_sums, dy_att, delta = _mm_mixed_bwd(dt1, w_out, o_gla, proj, gng, y_att, ts)
    dgq, dgk, dgv, dla, (r_up, r_out) = _gla_bwd(
        proj, la, do_gla, states, 512, ride=([_col_blocks(g_w_up), g_w_out.reshape(N_DEV, -1, D)], [False, False]))
    dglr, g_wg_p, gb_sums = _gate_bwd(dla, la, proj, wg_p, ts)
    dqn = _attn_dq(qn, kn, proj, dy_att, lse, delta, bias_rows)
    dkn, dvn = _attn_dkv(qn, kn, proj, dy_att, lse, delta, bias_cols)
    daq, dak, dav, qk_sums = _attn_post(dqn, dkn, dvn, proj, qg_t, kg_t, ts)
    dproj = jnp.concatenate([dgq, dgk, dgv, dgr, daq, dak, dav, dglr, jnp.zeros((S, PROJ_W - O_GLR - LANE), BF16)], axis=1)
    g_w_in_p = _mm(h1_t, dproj, NN, 512, PROJ_W, 1024, F32, "mm_gw_in")
    g_w_in = jnp.concatenate([g_w_in_p[:, :GLR_SRC], g_w_in_p[:, O_GLR:O_GLR + GLA_RANK], g_w_in_p[:, GLR_SRC:O_GLR]], axis=1)
    (dx, sums1), (r_in,) = _mm_rms_mod_bwd(dproj, w_in_p, x, dx2, n1g, sc1, ts, "mm_dh1",
                                           ride=([_col_blocks(g_w_in).astype(BF16)], [False]))
    dsh1, dsc1, g_n1g = sums1[0:1], sums1[1:2], sums1[2:3]

    dmod = jnp.concatenate([dsh1, dsc1, dg1, dsh2, dsc2, dg2], axis=1)
    grads = dict(n1g=g_n1g, w_in=r_in, wg=g_wg_p[:GLA_RANK], bg=gb_sums[0:1], gng=gng_sums[0:1],
                 qng_lanes=qk_sums[0:1], kng_lanes=qk_sums[1:2], w_out=r_out, n2g=g_n2g, w_up=r_up,
                 conv_w=g_conv_w, conv_b=g_conv_b, w_down=r_down)
    return loss_row, dx, dmod, grads


def _col_blocks(a):
    R, W = a.shape
    return a.reshape(R, N_DEV, W // N_DEV).transpose(1, 0, 2)


def _cols_from_blocks(a):
    n, R, C = a.shape
    return a.transpose(1, 0, 2).reshape(R, n * C)


def kernel(x, c, w_ada, b_ada, norm1_g, w_in, gla_w_gate, gla_b_gate, gla_norm_g, q_norm_g, k_norm_g, w_out, norm2_g, w_up, conv_w, conv_b, w_down, loss_target, m_w_ada, m_b_ada, m_norm1_g, m_w_in, m_gla_w_gate, m_gla_b_gate, m_gla_norm_g, m_q_norm_g, m_k_norm_g, m_w_out, m_norm2_g, m_w_up, m_conv_w, m_conv_b, m_w_down, v_w_ada, v_b_ada, v_norm1_g, v_w_in, v_gla_w_gate, v_gla_b_gate, v_gla_norm_g, v_q_norm_g, v_k_norm_g, v_w_out, v_norm2_g, v_w_up, v_conv_w, v_conv_b, v_w_down):
    axes = ("x", "y", "c")
    me = 4 * lax.axis_index("x") + 2 * lax.axis_index("y") + lax.axis_index("c")
    S, D = x.shape[1], x.shape[2]
    x2d, tgt2d = x[0], loss_target[0]
    w_in_s, w_out_s, w_up_s, w_down_s, w_ada_s = w_in[0], w_out[0], w_up[0], w_down[0], w_ada[0]
    conv_w_s, wg_s = conv_w[0], gla_w_gate[0]
    in_c, up_c, ada_c, wg_c, cw_c = w_in_s.shape[1], w_up_s.shape[1], w_ada_s.shape[1], wg_s.shape[1], conv_w_s.shape[1]
    F = w_down_s.shape[0] * N_DEV

    g_c, = _exchange([c], [True], "gather_c")
    c_all = g_c.reshape(N_DEV, D)

    b_shard = lax.dynamic_slice(b_ada, (0, me * ada_c), (1, ada_c))
    mod_part = _ada_fwd(c_all, w_ada_s, b_shard)
    mod_recv, = _exchange([mod_part.reshape(N_DEV, 1, ada_c)], [False], "exchange_mod")
    mod = mod_recv.reshape(6, D)

    loss_row, dx, dmod, gr = _local_step(
        x2d, tgt2d, mod, norm1_g, w_in_s.astype(BF16), conv_w_s, wg_s, gla_b_gate, gla_norm_g, q_norm_g, k_norm_g,
        w_out_s.astype(BF16), norm2_g, w_up_s.astype(BF16), conv_b, w_down_s.astype(BF16))
    loss = lax.psum(0.5 / D * jnp.sum(loss_row), axes)

    parts = [dmod, gr["n1g"], gr["bg"], gr["gng"], gr["qng_lanes"], gr["kng_lanes"], gr["n2g"], gr["conv_b"],
             gr["wg"].reshape(1, -1), gr["conv_w"].reshape(1, -1)]
    sizes = [p.shape[1] for p in parts]
    packed = jnp.concatenate(parts, axis=1)
    packed = jnp.pad(packed, ((0, 0), (0, -packed.shape[1] % (8 * LANE))))
    gathered, = _exchange([packed.reshape(8, -1)], [True], "gather_small_grads")
    gathered = gathered.reshape(N_DEV, -1)
    total = _sum_slots(gathered.reshape(N_DEV, 8, -1), "sum_small_grads").reshape(1, -1)
    offs = [0]
    for s_ in sizes:
        offs.append(offs[-1] + s_)
    t_dmod, t_n1g, t_bg, t_gng, t_qng, t_kng, t_n2g, t_conv_b, t_wg, t_conv_w = [
        total[:, offs[i]:offs[i + 1]] for i in range(len(sizes))]
    g_b_ada = t_dmod
    g_qng = t_qng.reshape(ATTN_HEADS, ATTN_HD).sum(axis=0, keepdims=True)
    g_kng = t_kng.reshape(ATTN_HEADS, ATTN_HD).sum(axis=0, keepdims=True)
    g_wg = lax.dynamic_slice(t_wg.reshape(GLA_RANK, -1), (0, me * wg_c), (GLA_RANK, wg_c))
    g_conv_w = lax.dynamic_slice(t_conv_w.reshape(3, -1), (0, me * cw_c), (3, cw_c))
    dmod_shard = lax.dynamic_slice(gathered[:, :6 * D], (0, me * ada_c), (N_DEV, ada_c))
    g_w_ada = _ada_bwd(c_all, dmod_shard)

    g_w_in, g_w_out, g_w_up, g_w_down = gr["w_in"], gr["w_out"], gr["w_up"], gr["w_down"]
    in_slots = {"w_in", "w_out", "w_up", "w_down"}
    names = ["w_ada", "b_ada", "norm1_g", "w_in", "gla_w_gate", "gla_b_gate", "gla_norm_g", "q_norm_g", "k_norm_g",
             "w_out", "norm2_g", "w_up", "conv_w", "conv_b", "w_down"]
    ws = [w_ada, b_ada, norm1_g, w_in, gla_w_gate, gla_b_gate, gla_norm_g, q_norm_g, k_norm_g, w_out, norm2_g, w_up, conv_w, conv_b, w_down]
    ms = [m_w_ada, m_b_ada, m_norm1_g, m_w_in, m_gla_w_gate, m_gla_b_gate, m_gla_norm_g, m_q_norm_g, m_k_norm_g, m_w_out, m_norm2_g, m_w_up, m_conv_w, m_conv_b, m_w_down]
    vs = [v_w_ada, v_b_ada, v_norm1_g, v_w_in, v_gla_w_gate, v_gla_b_gate, v_gla_norm_g, v_q_norm_g, v_k_norm_g, v_w_out, v_norm2_g, v_w_up, v_conv_w, v_conv_b, v_w_down]
    gs = [g_w_ada, g_b_ada, t_n1g, g_w_in, g_wg, t_bg, t_gng, g_qng, g_kng, g_w_out, t_n2g, g_w_up, g_conv_w, t_conv_b, g_w_down]
    grads, deltas, new_ms, new_vs = [], [], [], []
    for nm, w, g, m, v in zip(names, ws, gs, ms, vs):
        g_, d_, m_, v_ = _adamw(w, g, m, v, "adamw_" + nm, slots=nm in in_slots)
        grads.append(g_)
        deltas.append(d_)
        new_ms.append(m_)
        new_vs.append(v_)
    return (loss, dx.reshape(x.shape), *grads, *deltas, *new_ms, *new_vs)
```

```python
import functools
import math

import jax
import jax.numpy as jnp
from jax import lax
from jax.experimental import pallas as pl
from jax.experimental.pallas import tpu as pltpu

F32, BF16 = jnp.float32, jnp.bfloat16
HI = lax.Precision.HIGHEST
EPS = 1e-6
NEG = -1e30

N_DEV = 8
GLA_HEADS, GLA_DK, GLA_DV, GLA_RANK, GLA_TAU, GLA_CHUNK = 4, 64, 128, 16, 16.0, 64
ATTN_HEADS, ATTN_HD, ATTN_BLOCK = 8, 64, 128
DILATIONS = (1, 4, 16)
GLA_QK, GLA_V, ATTN_DIM = GLA_HEADS * GLA_DK, GLA_HEADS * GLA_DV, ATTN_HEADS * ATTN_HD
O_GQ, O_GK, O_GV, O_GR, O_AQ, O_AK, O_AV, O_GLR = 0, 256, 512, 1024, 1536, 2048, 2560, 3072
PROJ_W = 3328
LANE = 128
GLR_SRC = 2 * GLA_QK + 2 * GLA_V

ADAM_LR, ADAM_B1, ADAM_B2, ADAM_EPS, ADAM_WD, ADAM_STEP = 0.001, 0.9, 0.999, 1e-08, 0.01, 10

VMEM_LIMIT = 56 * 1024 * 1024
SUM_BLOCK_ELEMS = 256 * 1024


def _cp(*sem):
    return pltpu.CompilerParams(dimension_semantics=sem, vmem_limit_bytes=VMEM_LIMIT)


def _dot(a, b, dims, precision=None):
    return lax.dot_general(a, b, (dims, ((), ())), preferred_element_type=F32, precision=precision)


NN, NT, TN = ((1,), (0,)), ((1,), (1,)), ((0,), (0,))


def _sigmoid(z):
    return 1.0 / (1.0 + jnp.exp(-z))


HBM_SPEC = pl.BlockSpec(memory_space=pltpu.HBM)


def _exchange_shapes(arrays, gather):
    return [jax.ShapeDtypeStruct((N_DEV,) + (a.shape if g else a.shape[1:]), a.dtype) for a, g in zip(arrays, gather)]


def _exchange_sems(n):
    return [pltpu.SemaphoreType.DMA((n * (N_DEV - 1),)), pltpu.SemaphoreType.DMA((n * (N_DEV - 1),)), pltpu.SemaphoreType.DMA((n,))]


VIA_SIBLING = "via sibling"


def _exchange_plan(ins, outs, gather, send_sems, recv_sems, local_sems):
    x, y, c = lax.axis_index("x"), lax.axis_index("y"), lax.axis_index("c")
    me = 4 * x + 2 * y + c
    start, relays, waits = [], [], []
    for a in range(len(ins)):
        if gather[a] == VIA_SIBLING:
            def copy(i, block, to, src=None, a=a):
                slot = outs[a].at[4 * block[0] + 2 * block[1] + block[2]]
                return pltpu.make_async_remote_copy(
                    src_ref=slot if src is None else src, dst_ref=slot, send_sem=send_sems.at[a * (N_DEV - 1) + i],
                    recv_sem=recv_sems.at[a * (N_DEV - 1) + i], device_id=to, device_id_type=pl.DeviceIdType.MESH)

            chips = [(1 - x, y), (x, 1 - y), (1 - x, 1 - y)]
            first = [copy(0, (x, y, c), (x, y, 1 - c), src=ins[a])]
            first += [copy(1 + j, (x, y, c), (*chip, c), src=ins[a]) for j, chip in enumerate(chips)]
            passed = [copy(4 + j, (*chip, c), (x, y, 1 - c)) for j, chip in enumerate(chips)]
            start += first
            relays += [(copy(1 + j, (*chip, c), (x, y, c)).wait_recv, passed[j]) for j, chip in enumerate(chips)]
            waits += [copy(0, (x, y, 1 - c), (x, y, c)).wait_recv]
            waits += [copy(4 + j, (*chip, 1 - c), (x, y, c)).wait_recv for j, chip in enumerate(chips)]
            waits += [cp.wait_send for cp in first + passed]
        else:
            for p in range(1, N_DEV):
                px, py, pc = x ^ (p >> 2), y ^ ((p >> 1) & 1), c ^ (p & 1)
                peer = 4 * px + 2 * py + pc
                k = a * (N_DEV - 1) + p - 1
                cp = pltpu.make_async_remote_copy(
                    src_ref=ins[a] if gather[a] else ins[a].at[peer], dst_ref=outs[a].at[me],
                    send_sem=send_sems.at[k], recv_sem=recv_sems.at[k],
                    device_id=(px, py, pc), device_id_type=pl.DeviceIdType.MESH)
                start.append(cp)
                waits.append(cp.wait)
        own = pltpu.make_async_copy(ins[a] if gather[a] else ins[a].at[me], outs[a].at[me], local_sems.at[a])
        start.append(own)
        waits.append(own.wait)
    return start, relays, waits


def _exchange_finish(relays, waits):
    for arrived, pass_on in relays:
        arrived()
        pass_on.start()
    for wait in waits:
        wait()


def _riding(body, n_in, n_out, gather, grid):
    nr = len(gather)
    if not nr:
        return body

    def wrapped(*refs):
        ins, r_ins = refs[:n_in], refs[n_in:n_in + nr]
        outs, r_outs = refs[n_in + nr:n_in + nr + n_out], refs[n_in + nr + n_out:n_in + 2 * nr + n_out]
        scratch = refs[n_in + 2 * nr + n_out:]
        first = last = None
        for t, steps in enumerate(grid):
            pid = pl.program_id(t)
            first = (pid == 0) if first is None else first & (pid == 0)
            last = (pid == steps - 1) if last is None else last & (pid == steps - 1)
        start, relays, waits = _exchange_plan(r_ins, r_outs, gather, *scratch[-3:])

        @pl.when(first)
        def _():
            for cp in start:
                cp.start()

        body(*ins, *outs, *scratch[:-3])

        @pl.when(last)
        def _():
            _exchange_finish(relays, waits)

    return wrapped


def _exchange(arrays, gather, name):
    n = len(arrays)

    def body(*refs):
        start, relays, waits = _exchange_plan(refs[:n], refs[n:2 * n], gather, *refs[2 * n:])
        for cp in start:
            cp.start()
        _exchange_finish(relays, waits)

    return pl.pallas_call(
        body, out_shape=_exchange_shapes(arrays, gather), in_specs=[HBM_SPEC] * n, out_specs=[HBM_SPEC] * n,
        scratch_shapes=_exchange_sems(n), name=name)(*arrays)


def _sum_slots(x, name):
    _, R, C = x.shape
    tr = max(t for t in range(8, min(SUM_BLOCK_ELEMS // C, R) + 1, 8) if R % t == 0)

    def body(x_ref, o_ref):
        acc = x_ref[0].astype(F32)
        for s in range(1, N_DEV):
            acc = acc + x_ref[s].astype(F32)
        o_ref[...] = acc

    return pl.pallas_call(
        body, grid=(R // tr,), in_specs=[pl.BlockSpec((N_DEV, tr, C), lambda i: (0, i, 0))],
        out_specs=pl.BlockSpec((tr, C), lambda i: (i, 0)), out_shape=jax.ShapeDtypeStruct((R, C), F32),
        compiler_params=_cp("parallel"), name=name)(x)


def _mm(a, b, mode, tm, tn, tk, out_dtype, name, ride=None):
    if mode == NN:
        (M, K), N = a.shape, b.shape[1]
    elif mode == NT:
        (M, K), N = a.shape, b.shape[0]
    else:
        (K, M), N = a.shape, b.shape[1]
    tm, tn, tk = min(tm, M), min(tn, N), min(tk, K)
    assert M % tm == 0 and N % tn == 0 and K % tk == 0, (name, M, N, K, tm, tn, tk)
    nk = K // tk
    if mode == NN:
        a_spec = pl.BlockSpec((tm, tk), lambda i, j, k: (i, k))
        b_spec = pl.BlockSpec((tk, tn), lambda i, j, k: (k, j))
    elif mode == NT:
        a_spec = pl.BlockSpec((tm, tk), lambda i, j, k: (i, k))
        b_spec = pl.BlockSpec((tn, tk), lambda i, j, k: (j, k))
    else:
        a_spec = pl.BlockSpec((tk, tm), lambda i, j, k: (k, i))
        b_spec = pl.BlockSpec((tk, tn), lambda i, j, k: (k, j))

    ride_arrays, ride_gather = ride if ride else ([], [])
    nr = len(ride_arrays)
    grid = (M // tm, N // tn, nk)

    own_acc = nk > 1 and out_dtype != F32

    def body(a_ref, b_ref, o_ref, *acc):
        p = _dot(a_ref[...].astype(BF16), b_ref[...].astype(BF16), mode)
        if nk == 1:
            o_ref[...] = p.astype(out_dtype)
        else:
            acc_ref = acc[0] if own_acc else o_ref
            k = pl.program_id(2)

            @pl.when(k == 0)
            def _():
                acc_ref[...] = p

            @pl.when(k > 0)
            def _():
                acc_ref[...] += p

            if own_acc:
                @pl.when(k == nk - 1)
                def _():
                    o_ref[...] = acc_ref[...].astype(out_dtype)

    outs = pl.pallas_call(
        _riding(body, 2, 1, ride_gather, grid), grid=grid, in_specs=[a_spec, b_spec] + [HBM_SPEC] * nr,
        out_specs=[pl.BlockSpec((tm, tn), lambda i, j, k: (i, j))] + [HBM_SPEC] * nr,
        out_shape=[jax.ShapeDtypeStruct((M, N), out_dtype)] + _exchange_shapes(ride_arrays, ride_gather),
        scratch_shapes=([pltpu.VMEM((tm, tn), F32)] if own_acc else []) + (_exchange_sems(nr) if nr else []),
        compiler_params=_cp(*(("arbitrary",) * 3 if nr else ("parallel", "parallel", "arbitrary"))), name=name)(a, b, *ride_arrays)
    return (outs[0], outs[1:]) if nr else outs[0]


def _ada_fwd(c_all, w_shard, b_shard):
    Nc = w_shard.shape[1]

    def body(c_ref, w_ref, b_ref, o_ref):
        cv = c_ref[...]
        o_ref[...] = _dot(cv * _sigmoid(cv), w_ref[...], NN, HI) + b_ref[...]

    return pl.pallas_call(body, out_shape=jax.ShapeDtypeStruct((N_DEV, Nc), F32), name="ada_fwd",
                          compiler_params=pltpu.CompilerParams(vmem_limit_bytes=VMEM_LIMIT))(c_all, w_shard, b_shard)


def _ada_bwd(c_all, dmod_shard):
    D, Nc = c_all.shape[1], dmod_shard.shape[1]

    def body(c_ref, d_ref, o_ref):
        cv = c_ref[...]
        o_ref[...] = _dot(cv * _sigmoid(cv), d_ref[...], TN, HI)

    return pl.pallas_call(body, out_shape=jax.ShapeDtypeStruct((D, Nc), F32), name="ada_bwd",
                          compiler_params=pltpu.CompilerParams(vmem_limit_bytes=VMEM_LIMIT))(c_all, dmod_shard)


def _row_spec(ts, D):
    return pl.BlockSpec((ts, D), lambda i: (i, 0))


def _vec_spec(D):
    return pl.BlockSpec((1, D), lambda i: (0, 0))


def _col_spec(D, ts):
    return pl.BlockSpec((D, ts), lambda i: (0, i))


def _rms_mod(x, ng, sc, sh, ts, name, ride=None):
    S, D = x.shape
    ride_arrays, ride_gather = ride if ride else ([], [])
    nr = len(ride_arrays)
    grid = (S // ts,)

    def body(x_ref, ng_ref, sc_ref, sh_ref, h_ref, ht_ref):
        xv = x_ref[...]
        r = lax.rsqrt(jnp.mean(xv * xv, axis=-1, keepdims=True) + EPS)
        h = xv * r * ng_ref[...] * (1.0 + sc_ref[...]) + sh_ref[...]
        h_ref[...] = h.astype(BF16)
        ht_ref[...] = h.T.astype(BF16)

    outs = pl.pallas_call(
        _riding(body, 4, 2, ride_gather, grid), grid=grid, in_specs=[_row_spec(ts, D)] + [_vec_spec(D)] * 3 + [HBM_SPEC] * nr,
        out_specs=[_row_spec(ts, D), _col_spec(D, ts)] + [HBM_SPEC] * nr,
        out_shape=[jax.ShapeDtypeStruct((S, D), BF16), jax.ShapeDtypeStruct((D, S), BF16)] + _exchange_shapes(ride_arrays, ride_gather),
        scratch_shapes=_exchange_sems(nr) if nr else [],
        compiler_params=_cp("arbitrary"), name=name)(x, ng, sc, sh, *ride_arrays)
    return outs[0], outs[1], outs[2:]


def _mm_rows(a, b, mode, tm, extras, extra_specs, out_shapes, out_specs, epilogue, name, ride=None):
    M, K = a.shape
    grid = (M // tm,)
    ride_arrays, ride_gather = ride if ride else ([], [])
    nr = len(ride_arrays)

    def body(a_ref, b_ref, *refs):
        epilogue(_dot(a_ref[...].astype(BF16), b_ref[...].astype(BF16), mode), pl.program_id(0), *refs)

    outs = pl.pallas_call(
        _riding(body, 2 + len(extras), len(out_shapes), ride_gather, grid), grid=grid,
        in_specs=[pl.BlockSpec((tm, K), lambda i: (i, 0)), pl.BlockSpec(b.shape, lambda i: (0, 0), pipeline_mode=pl.Buffered(1))]
        + list(extra_specs) + [HBM_SPEC] * nr,
        out_specs=list(out_specs) + [HBM_SPEC] * nr,
        out_shape=list(out_shapes) + _exchange_shapes(ride_arrays, ride_gather),
        scratch_shapes=_exchange_sems(nr) if nr else [],
        compiler_params=_cp("arbitrary"), name=name)(a, b, *extras, *ride_arrays)
    return outs[:len(out_shapes)], outs[len(out_shapes):]


def _accumulate(ref, part, step):
    @pl.when(step == 0)
    def _():
        ref[...] = part

    @pl.when(step > 0)
    def _():
        ref[...] += part


def _rows8(rows, width):
    return jnp.concatenate(rows + [jnp.zeros((8 - len(rows), width), F32)], axis=0)


def _mm_resid_rms_mod(a, w, x, g, ng, sc, sh, tm, name):
    S, D = x.shape

    def epilogue(t, step, x_ref, g_ref, ng_ref, sc_ref, sh_ref, t_ref, x2_ref, h_ref, ht_ref):
        t_ref[...] = t
        xv = x_ref[...] + g_ref[...] * t
        x2_ref[...] = xv
        r = lax.rsqrt(jnp.mean(xv * xv, axis=-1, keepdims=True) + EPS)
        h = xv * r * ng_ref[...] * (1.0 + sc_ref[...]) + sh_ref[...]
        h_ref[...] = h.astype(BF16)
        ht_ref[...] = h.T.astype(BF16)

    row, vec = _row_spec(tm, D), _vec_spec(D)
    full, half = jax.ShapeDtypeStruct((S, D), F32), jax.ShapeDtypeStruct((S, D), BF16)
    outs, _ = _mm_rows(a, w, NN, tm, [x, g, ng, sc, sh], [row] + [vec] * 4,
                       [full, full, half, jax.ShapeDtypeStruct((D, S), BF16)], [row, row, row, _col_spec(D, tm)], epilogue, name)
    return outs


def _mm_loss_resid(a, w, x2, g2, target, tm, name):
    S, D = x2.shape

    def epilogue(t, step, x_ref, y_ref, g_ref, dx_ref, dt_ref, sums_ref):
        gv = g_ref[...]
        e = x_ref[...] + gv * t - y_ref[...]
        dx = e * (1.0 / D)
        dx_ref[...] = dx
        dt_ref[...] = (dx * gv).astype(BF16)
        _accumulate(sums_ref, _rows8([jnp.sum(e * e, axis=0, keepdims=True), jnp.sum(dx * t, axis=0, keepdims=True)], D), step)

    row, vec = _row_spec(tm, D), _vec_spec(D)
    outs, _ = _mm_rows(a, w, NN, tm, [x2, target, g2], [row, row, vec],
                       [jax.ShapeDtypeStruct((S, D), F32), jax.ShapeDtypeStruct((S, D), BF16), jax.ShapeDtypeStruct((8, D), F32)],
                       [row, row, pl.BlockSpec((8, D), lambda i: (0, 0))], epilogue, name)
    return outs


def _mm_rms_mod_bwd(a, w, xin, dres, ng, sc, tm, name, t_prev=None, g_prev=None, ride=None):
    S, D = xin.shape
    chain = t_prev is not None

    def epilogue(dhv, step, *refs):
        if chain:
            x_ref, dr_ref, ng_ref, sc_ref, t_ref, g_ref, dx_ref, sums_ref, dt_ref = refs
        else:
            x_ref, dr_ref, ng_ref, sc_ref, dx_ref, sums_ref = refs
        xv = x_ref[...]
        r = lax.rsqrt(jnp.mean(xv * xv, axis=-1, keepdims=True) + EPS)
        xh = xv * r
        ngv, scv = ng_ref[...], sc_ref[...]
        dxh = dhv * (ngv * (1.0 + scv))
        dx = dr_ref[...] + r * (dxh - xh * jnp.mean(dxh * xh, axis=-1, keepdims=True))
        dx_ref[...] = dx
        dhx = dhv * xh
        rows = [jnp.sum(dhv, axis=0, keepdims=True), jnp.sum(dhx * ngv, axis=0, keepdims=True),
                jnp.sum(dhx * (1.0 + scv), axis=0, keepdims=True)]
        if chain:
            dt_ref[...] = (dx * g_ref[...]).astype(BF16)
            rows.append(jnp.sum(dx * t_ref[...], axis=0, keepdims=True))
        _accumulate(sums_ref, _rows8(rows, D), step)

    row, vec = _row_spec(tm, D), _vec_spec(D)
    extras = [xin, dres, ng, sc] + ([t_prev, g_prev] if chain else [])
    extra_specs = [row, row, vec, vec] + ([row, vec] if chain else [])
    out_shapes = [jax.ShapeDtypeStruct((S, D), F32), jax.ShapeDtypeStruct((8, D), F32)] + (
        [jax.ShapeDtypeStruct((S, D), BF16)] if chain else [])
    out_specs = [row, pl.BlockSpec((8, D), lambda i: (0, 0))] + ([row] if chain else [])
    return _mm_rows(a, w, NT, tm, extras, extra_specs, out_shapes, out_specs, epilogue, name, ride=ride)


def _mm_in(h1, w_in_p, wg_p, bg, qg, kg, tm, ride=None):
    S = h1.shape[0]
    W = ATTN_DIM

    def epilogue(p, step, wg_ref, bg_ref, qg_ref, kg_ref, proj_ref, la_ref, qn_ref, kn_ref):
        proj_ref[...] = p
        z = _dot(p[:, O_GLR:O_GLR + LANE], wg_ref[...], NN, HI) + bg_ref[...]
        la_ref[...] = (jnp.minimum(z, 0.0) - jnp.log(1.0 + jnp.exp(-jnp.abs(z)))) * (1.0 / GLA_TAU)
        seg = _seg_matrix(W, ATTN_HD, 1.0 / ATTN_HD)
        for off, g_ref, o_ref, scale in ((O_AQ, qg_ref, qn_ref, ATTN_HD ** -0.5), (O_AK, kg_ref, kn_ref, 1.0)):
            xv = p[:, off:off + W]
            o_ref[...] = xv * lax.rsqrt(_seg_sum(xv * xv, seg) + EPS) * (g_ref[...] * scale)

    row = lambda w: pl.BlockSpec((tm, w), lambda i: (i, 0))
    const = lambda a: pl.BlockSpec(a.shape, lambda i: (0, 0))
    return _mm_rows(
        h1, w_in_p, NN, tm, [wg_p, bg, qg, kg], [const(wg_p), const(bg), const(qg), const(kg)],
        [jax.ShapeDtypeStruct((S, PROJ_W), F32), jax.ShapeDtypeStruct((S, GLA_QK), F32),
         jax.ShapeDtypeStruct((S, W), F32), jax.ShapeDtypeStruct((S, W), F32)],
        [row(PROJ_W), row(GLA_QK), row(W), row(W)], epilogue, "mm_in", ride=ride)


def _gate_bwd(dla, la, proj, wg_p, ts):
    S = proj.shape[0]

    def body(dla_ref, la_ref, glr_ref, w_ref, dglr_ref, gw_ref, gb_ref):
        i = pl.program_id(0)
        dz = dla_ref[...] * (1.0 / GLA_TAU) * (1.0 - jnp.exp(GLA_TAU * la_ref[...]))
        dglr_ref[...] = _dot(dz, w_ref[...], NT, HI).astype(BF16)
        gw = _dot(glr_ref[...], dz, TN, HI)
        gb = jnp.concatenate([jnp.sum(dz, axis=0, keepdims=True), jnp.zeros((7, GLA_QK), F32)], axis=0)

        @pl.when(i == 0)
        def _():
            gw_ref[...] = gw
            gb_ref[...] = gb

        @pl.when(i > 0)
        def _():
            gw_ref[...] += gw
            gb_ref[...] += gb

    return pl.pallas_call(
        body, grid=(S // ts,),
        in_specs=[pl.BlockSpec((ts, GLA_QK), lambda i: (i, 0)), pl.BlockSpec((ts, GLA_QK), lambda i: (i, 0)),
                  pl.BlockSpec((ts, LANE), lambda i: (i, O_GLR // LANE)), pl.BlockSpec((LANE, GLA_QK), lambda i: (0, 0))],
        out_specs=[pl.BlockSpec((ts, LANE), lambda i: (i, 0)), pl.BlockSpec((LANE, GLA_QK), lambda i: (0, 0)),
                   pl.BlockSpec((8, GLA_QK), lambda i: (0, 0))],
        out_shape=[jax.ShapeDtypeStruct((S, LANE), BF16), jax.ShapeDtypeStruct((LANE, GLA_QK), F32),
                   jax.ShapeDtypeStruct((8, GLA_QK), F32)],
        compiler_params=_cp("arbitrary"), name="gla_gate_bwd")(dla, la, proj, wg_p)


def _tri(lower):
    r = lax.broadcasted_iota(jnp.int32, (GLA_CHUNK, GLA_CHUNK), 0)
    c = lax.broadcasted_iota(jnp.int32, (GLA_CHUNK, GLA_CHUNK), 1)
    return jnp.where((r >= c) if lower else (c >= r), 1.0, 0.0).astype(F32)


GLA_SUB = 16
GLA_NSUB = GLA_CHUNK // GLA_SUB
PAIR_QK = 2 * GLA_DK
PAIR_V = 2 * GLA_DV


def _band_selector():
    r = lax.broadcasted_iota(jnp.int32, (GLA_SUB * PAIR_QK, LANE), 0)
    c = lax.broadcasted_iota(jnp.int32, (GLA_SUB * PAIR_QK, LANE), 1)
    dist, head = r // PAIR_QK, (r % PAIR_QK) // GLA_DK
    return jnp.where(c == head * GLA_DK + (GLA_SUB - 1 - dist), 1.0, 0.0).astype(BF16)


def _flip_matrix():
    r = lax.broadcasted_iota(jnp.int32, (GLA_CHUNK, GLA_CHUNK), 0)
    c = lax.broadcasted_iota(jnp.int32, (GLA_CHUNK, GLA_CHUNK), 1)
    return jnp.where(r + c == GLA_CHUNK - 1, 1.0, 0.0).astype(BF16)


def _state_mask():
    r = lax.broadcasted_iota(jnp.int32, (PAIR_V, PAIR_QK), 0)
    c = lax.broadcasted_iota(jnp.int32, (PAIR_V, PAIR_QK), 1)
    return (r < GLA_DV) == (c < GLA_DK)


class _GlaChunk:
    def __init__(self, qs, kc, vc, g, sel):
        C = GLA_CHUNK
        self.qs, self.kc, self.vc = qs, kc, vc
        rows = lax.broadcasted_iota(jnp.int32, (C, 1), 0)
        lane = lax.broadcasted_iota(jnp.int32, (1, PAIR_QK), 1)
        self.rows, self.lane = rows, lane
        b = _dot(_tri(True), g, NN, HI)
        self.bl = b[C - 1:C, :]
        self.eb = jnp.exp(b)
        self.kdec = jnp.exp(self.bl - b)
        edge = lambda J: b[GLA_SUB * (J + 1):GLA_SUB * (J + 1) + 1, :]
        self.e_far = [jnp.exp(jnp.where(rows >= GLA_SUB * (J + 1), b - edge(J), NEG)) for J in range(GLA_NSUB - 1)]
        blk = rows // GLA_SUB
        bnext = edge(0)
        for J in range(1, GLA_NSUB - 1):
            bnext = jnp.where(blk == J, edge(J), bnext)
        self.e_khat = jnp.exp(jnp.where(blk < GLA_NSUB - 1, bnext - b, NEG))
        khat = kc * self.e_khat
        k2 = jnp.concatenate([jnp.where(lane < GLA_DK, khat, 0.0), jnp.where(lane >= GLA_DK, khat, 0.0)], axis=0)
        self.blk2 = jnp.concatenate([blk, blk], axis=0)
        self.m_far = jnp.concatenate([jnp.where(self.blk2 == J, k2, 0.0) for J in range(GLA_NSUB - 1)], axis=1).astype(BF16)
        self.qcat = jnp.concatenate([qs * e for e in self.e_far], axis=1).astype(BF16)
        a_far = _dot(self.qcat, self.m_far, NT)
        self.e_band, self.rk, hi_terms, lo_terms = [], [], [], []
        for d in range(GLA_SUB):
            rk = pltpu.roll(kc, d, 0) if d else kc
            rb = pltpu.roll(b, d, 0) if d else b
            e = jnp.exp(jnp.where(rows >= d, b - rb, NEG))
            self.e_band.append(e)
            self.rk.append(rk)
            t = (qs * e).astype(BF16).astype(F32) * rk.astype(BF16).astype(F32)
            hi = t.astype(BF16)
            hi_terms.append(hi)
            lo_terms.append((t - hi.astype(F32)).astype(BF16))
        band = _dot(jnp.concatenate(hi_terms, axis=1), sel, NN) + _dot(jnp.concatenate(lo_terms, axis=1), sel, NN)
        a_band = pltpu.roll(band, LANE - (GLA_SUB - 1), 1, stride=1, stride_axis=0)
        dist = rows - lane % GLA_DK
        self.far_mask = dist >= GLA_SUB
        self.band_mask = (dist >= 0) & (dist < GLA_SUB)
        self.a = (a_band + jnp.where(self.far_mask, a_far, 0.0)).astype(BF16)
        self.lane_v = lax.broadcasted_iota(jnp.int32, (1, PAIR_V), 1)
        self.v2 = jnp.concatenate([jnp.where(self.lane_v < GLA_DV, vc, 0.0), jnp.where(self.lane_v >= GLA_DV, vc, 0.0)],
                                  axis=0).astype(BF16)


def _gla_fwd(proj, la, tb, ride=None):
    S = proj.shape[0]
    C = GLA_CHUNK
    tb = min(tb, S)
    nbc = tb // C
    npair = GLA_HEADS // 2
    scale = GLA_DK ** -0.5

    def body(q_ref, k_ref, v_ref, la_ref, sel_ref, o_ref, st_ref, state):
        @pl.when(pl.program_id(1) == 0)
        def _():
            state[...] = jnp.zeros_like(state)

        def chunk(ci, carry):
            sl = pl.ds(pl.multiple_of(ci * C, C), C)
            ch = _GlaChunk(q_ref[sl, :] * scale, k_ref[sl, :], v_ref[sl, :], la_ref[sl, :], sel_ref[...])
            st = state[...]
            st_ref[0, ci] = st
            o_ref[sl, :] = _dot((ch.qs * ch.eb).astype(BF16), st.astype(BF16), NT) + _dot(ch.a, ch.v2, NN)
            upd = _dot(ch.vc.astype(BF16), (ch.kc * ch.kdec).astype(BF16), TN)
            state[...] = st * jnp.exp(ch.bl) + jnp.where(_state_mask(), upd, 0.0)
            return carry

        lax.fori_loop(0, nbc, chunk, 0, unroll=8)

    qspec = lambda off: pl.BlockSpec((tb, PAIR_QK), lambda p, i: (i, off // PAIR_QK + p))
    ride_arrays, ride_gather = ride if ride else ([], [])
    nr = len(ride_arrays)
    grid = (npair, S // tb)
    outs = pl.pallas_call(
        _riding(body, 5, 2, ride_gather, grid), grid=grid,
        in_specs=[qspec(O_GQ), qspec(O_GK), pl.BlockSpec((tb, PAIR_V), lambda p, i: (i, O_GV // PAIR_V + p)),
                  pl.BlockSpec((tb, PAIR_QK), lambda p, i: (i, p)),
                  pl.BlockSpec((GLA_SUB * PAIR_QK, LANE), lambda p, i: (0, 0))] + [HBM_SPEC] * nr,
        out_specs=[pl.BlockSpec((tb, PAIR_V), lambda p, i: (i, p)),
                   pl.BlockSpec((1, nbc, PAIR_V, PAIR_QK), lambda p, i: (p, i, 0, 0))] + [HBM_SPEC] * nr,
        out_shape=[jax.ShapeDtypeStruct((S, GLA_V), F32), jax.ShapeDtypeStruct((npair, S // C, PAIR_V, PAIR_QK), F32)]
        + _exchange_shapes(ride_arrays, ride_gather),
        scratch_shapes=[pltpu.VMEM((PAIR_V, PAIR_QK), F32)] + (_exchange_sems(nr) if nr else []),
        compiler_params=_cp("arbitrary", "arbitrary"), name="gla_fwd")(proj, proj, proj, la, _band_selector(), *ride_arrays)
    return outs[0], outs[1], outs[2:]


def _gla_bwd(proj, la, do, states, tb, ride=None):
    S = proj.shape[0]
    C = GLA_CHUNK
    tb = min(tb, S)
    nbc = tb // C
    nblk = S // tb
    npair = GLA_HEADS // 2
    scale = GLA_DK ** -0.5

    def body(q_ref, k_ref, v_ref, la_ref, do_ref, st_ref, sel_ref, selt_ref, dq_ref, dk_ref, dv_ref, dla_ref, dstate):
        @pl.when(pl.program_id(1) == 0)
        def _():
            dstate[...] = jnp.zeros_like(dstate)

        def chunk(cc, carry):
            ci = nbc - 1 - cc
            sl = pl.ds(pl.multiple_of(ci * C, C), C)
            ch = _GlaChunk(q_ref[sl, :] * scale, k_ref[sl, :], v_ref[sl, :], la_ref[sl, :], sel_ref[...])
            qs, kc, rows = ch.qs, ch.kc, ch.rows
            doc_b = do_ref[sl, :].astype(BF16)
            st = st_ref[0, ci]
            dst = dstate[...]
            dst_b = dst.astype(BF16)
            ebl = jnp.exp(ch.bl)
            dq = _dot(doc_b, st.astype(BF16), NN) * ch.eb
            dk = _dot(ch.vc.astype(BF16), dst_b, NN) * ch.kdec
            dv = _dot((kc * ch.kdec).astype(BF16), dst_b, NT)
            dbl = jnp.sum(dst * st, axis=0, keepdims=True) * ebl + jnp.sum(kc * dk, axis=0, keepdims=True)
            da = _dot(doc_b, ch.v2, NT)
            dv2 = _dot(ch.a, doc_b, TN)
            dv = dv + jnp.where(ch.lane_v < GLA_DV, dv2[:C], dv2[C:])
            da_far = jnp.where(ch.far_mask, da, 0.0).astype(BF16)
            dqcat = _dot(da_far, ch.m_far, NN)
            dm = _dot(da_far, ch.qcat, TN)
            dk2 = jnp.zeros((2 * C, PAIR_QK), F32)
            for J in range(GLA_NSUB - 1):
                dq = dq + dqcat[:, J * PAIR_QK:(J + 1) * PAIR_QK] * ch.e_far[J]
                dk2 = dk2 + jnp.where(ch.blk2 == J, dm[:, J * PAIR_QK:(J + 1) * PAIR_QK], 0.0)
            dk = dk + jnp.where(ch.lane < GLA_DK, dk2[:C], dk2[C:]) * ch.e_khat
            flip = _flip_matrix()
            da_band = _dot(flip, jnp.where(ch.band_mask, da, 0.0).astype(BF16), NN)
            dband = pltpu.roll(da_band, LANE - (C - GLA_SUB), 1, stride=1, stride_axis=0)
            dband = _dot(flip, dband.astype(BF16), NN)
            dterms = _dot(dband.astype(BF16), selt_ref[...], NN)
            for d in range(GLA_SUB):
                dt = dterms[:, d * PAIR_QK:(d + 1) * PAIR_QK]
                dq = dq + dt * (ch.rk[d] * ch.e_band[d])
                dkr = dt * (qs * ch.e_band[d])
                dk = dk + (pltpu.roll(dkr, C - d, 0) if d else dkr)
            db = qs * dq - kc * dk
            db = jnp.where(rows == C - 1, db + dbl, db)
            dq_ref[sl, :] = (dq * scale).astype(BF16)
            dk_ref[sl, :] = dk.astype(BF16)
            dv_ref[sl, :] = dv.astype(BF16)
            dla_ref[sl, :] = _dot(_tri(False), db, NN, HI)
            upd = _dot(doc_b, (qs * ch.eb).astype(BF16), TN)
            dstate[...] = dst * ebl + jnp.where(_state_mask(), upd, 0.0)
            return carry

        lax.fori_loop(0, nbc, chunk, 0, unroll=8)

    rev = lambda i: nblk - 1 - i
    qspec = lambda off: pl.BlockSpec((tb, PAIR_QK), lambda p, i: (rev(i), off // PAIR_QK + p))
    pair_qk = pl.BlockSpec((tb, PAIR_QK), lambda p, i: (rev(i), p))
    pair_v = pl.BlockSpec((tb, PAIR_V), lambda p, i: (rev(i), p))
    sel = _band_selector()
    ride_arrays, ride_gather = ride if ride else ([], [])
    nr = len(ride_arrays)
    grid = (npair, nblk)
    outs = pl.pallas_call(
        _riding(body, 8, 4, ride_gather, grid), grid=grid,
        in_specs=[qspec(O_GQ), qspec(O_GK), pl.BlockSpec((tb, PAIR_V), lambda p, i: (rev(i), O_GV // PAIR_V + p)),
                  pair_qk, pair_v, pl.BlockSpec((1, nbc, PAIR_V, PAIR_QK), lambda p, i: (p, rev(i), 0, 0)),
                  pl.BlockSpec((GLA_SUB * PAIR_QK, LANE), lambda p, i: (0, 0)),
                  pl.BlockSpec((LANE, GLA_SUB * PAIR_QK), lambda p, i: (0, 0))] + [HBM_SPEC] * nr,
        out_specs=[pair_qk, pair_qk, pair_v, pair_qk] + [HBM_SPEC] * nr,
        out_shape=[jax.ShapeDtypeStruct((S, GLA_QK), BF16), jax.ShapeDtypeStruct((S, GLA_QK), BF16),
                   jax.ShapeDtypeStruct((S, GLA_V), BF16), jax.ShapeDtypeStruct((S, GLA_QK), F32)]
        + _exchange_shapes(ride_arrays, ride_gather),
        scratch_shapes=[pltpu.VMEM((PAIR_V, PAIR_QK), F32)] + (_exchange_sems(nr) if nr else []),
        compiler_params=_cp("arbitrary", "arbitrary"), name="gla_bwd")(proj, proj, proj, la, do, states, sel, sel.T, *ride_arrays)
    return outs[0], outs[1], outs[2], outs[3], outs[4:]


def _gla_out(o, proj, gng, ts):
    S = o.shape[0]

    def body(o_ref, gr_ref, g_ref, y_ref):
        for h in range(GLA_HEADS):
            cols = slice(h * GLA_DV, (h + 1) * GLA_DV)
            ov, grv = o_ref[:, cols], gr_ref[:, cols]
            r = lax.rsqrt(jnp.mean(ov * ov, axis=-1, keepdims=True) + EPS)
            y_ref[:, cols] = (ov * r * g_ref[...] * (grv * _sigmoid(grv))).astype(BF16)

    return pl.pallas_call(
        body, grid=(S // ts,),
        in_specs=[pl.BlockSpec((ts, GLA_V), lambda i: (i, 0)), pl.BlockSpec((ts, GLA_V), lambda i: (i, O_GR // GLA_V)),
                  pl.BlockSpec((1, GLA_DV), lambda i: (0, 0))],
        out_specs=pl.BlockSpec((ts, GLA_V), lambda i: (i, 0)), out_shape=jax.ShapeDtypeStruct((S, GLA_V), BF16),
        compiler_params=_cp("parallel"), name="gla_out_fwd")(o, proj, gng)


def _mm_mixed_bwd(dt1, w_out, o, proj, gng, y_att, tm):
    S = o.shape[0]
    W = ATTN_DIM

    def epilogue(dm, step, o_ref, gr_ref, g_ref, y_ref, do_ref, dgr_ref, gg_ref, dy_ref, de_ref):
        gsum = jnp.zeros((1, GLA_DV), F32)
        for h in range(GLA_HEADS):
            cols = slice(h * GLA_DV, (h + 1) * GLA_DV)
            ov, grv, dy = o_ref[:, cols], gr_ref[:, cols], dm[:, cols]
            r = lax.rsqrt(jnp.mean(ov * ov, axis=-1, keepdims=True) + EPS)
            oh = ov * r
            sg = _sigmoid(grv)
            don = dy * (grv * sg)
            dgr_ref[:, cols] = (dy * (oh * g_ref[...]) * (sg * (1.0 + grv * (1.0 - sg)))).astype(BF16)
            gsum = gsum + jnp.sum(don * oh, axis=0, keepdims=True)
            doh = don * g_ref[...]
            do_ref[:, cols] = r * (doh - oh * jnp.mean(doh * oh, axis=-1, keepdims=True))
        _accumulate(gg_ref, _rows8([gsum], GLA_DV), step)
        dya = dm[:, GLA_V:]
        dy_ref[...] = dya
        de_ref[...] = _seg_sum(dya * y_ref[...], _seg_matrix(W, ATTN_HD, 1.0))

    half = pl.BlockSpec((tm, GLA_V), lambda i: (i, 0))
    outs, _ = _mm_rows(
        dt1, w_out, NT, tm, [o, proj, gng, y_att],
        [half, pl.BlockSpec((tm, GLA_V), lambda i: (i, O_GR // GLA_V)), pl.BlockSpec((1, GLA_DV), lambda i: (0, 0)), half],
        [jax.ShapeDtypeStruct((S, GLA_V), F32), jax.ShapeDtypeStruct((S, GLA_V), BF16), jax.ShapeDtypeStruct((8, GLA_DV), F32),
         jax.ShapeDtypeStruct((S, W), F32), jax.ShapeDtypeStruct((S, W), F32)],
        [half, half, pl.BlockSpec((8, GLA_DV), lambda i: (0, 0)), half, half], epilogue, "mm_dmixed")
    return outs


def _seg_matrix(width, seg, value):
    r = lax.broadcasted_iota(jnp.int32, (width, width), 0) // seg
    c = lax.broadcasted_iota(jnp.int32, (width, width), 1) // seg
    return jnp.where(r == c, value, 0.0).astype(BF16)


def _seg_sum(x, seg_matrix):
    hi = x.astype(BF16)
    lo = (x - hi.astype(F32)).astype(BF16)
    return _dot(hi, seg_matrix, NN) + _dot(lo, seg_matrix, NN)


ATTN_GROUP = 4


ATTN_TILE = max(DILATIONS) * ATTN_BLOCK


def _attn_rows(d, g, r, base=0):
    start = base + (g * d * ATTN_BLOCK if g >= 0 else ATTN_TILE - d * ATTN_BLOCK) + r
    return pl.ds(start, ATTN_BLOCK) if d == 1 else pl.ds(start, ATTN_BLOCK, stride=d)


def _for_blocks(d, G, fn):
    for g in range(G):
        if d <= ATTN_GROUP:
            for r in range(d):
                fn(g, r)
        else:
            def step(r, carry, g=g):
                fn(g, r)
                return carry
            lax.fori_loop(0, d, step, 0, unroll=ATTN_GROUP)


def _attn_specs(S):
    nb = S // ATTN_TILE

    def specs(off=0):
        return [pl.BlockSpec((ATTN_TILE, LANE), lambda hp, n: (n, off + hp)),
                pl.BlockSpec((ATTN_TILE, LANE), lambda hp, n: (jnp.maximum(n - 1, 0), off + hp)),
                pl.BlockSpec((ATTN_TILE, LANE), lambda hp, n: (jnp.minimum(n + 1, nb - 1), off + hp))]

    return nb, specs


def _slope(head):
    one = jnp.ones((1, 1), jnp.int32)
    return 1.0 / jnp.left_shift(one, one * (head + 1)).astype(F32)


def _attn_bias(d, hp, first_tile):
    B = ATTN_BLOCK
    iq = lax.broadcasted_iota(jnp.int32, (B, 2 * B), 0)
    ik = lax.broadcasted_iota(jnp.int32, (B, 2 * B), 1)
    rel = iq + B - ik
    window = (rel >= 0) & (rel <= B)
    relf = (d * rel).astype(F32)
    full = [jnp.where(window, -_slope(hp * 2 + h) * relf, NEG) for h in range(2)]
    edge = [jnp.where((ik >= B) | jnp.logical_not(first_tile), b, NEG) for b in full]
    return full, edge


def _attn_bias_t(d, hp, has_next):
    B = ATTN_BLOCK
    ik = lax.broadcasted_iota(jnp.int32, (B, B), 0)
    iq = lax.broadcasted_iota(jnp.int32, (B, B), 1)
    tiles = []
    for nxt in range(2):
        rel = iq - ik + nxt * B
        window = (rel >= 0) & (rel <= B)
        relf = (d * rel).astype(F32)
        tiles.append([jnp.where(window, -_slope(hp * 2 + h) * relf, NEG) for h in range(2)])
    tiles.append([jnp.where(has_next, b, NEG) for b in tiles[1]])
    return tiles


def _attn_fwd(qn, kn, proj):
    S, W = qn.shape
    T = ATTN_TILE
    nb, specs = _attn_specs(S)

    def body(q_ref, kp_ref, kc_ref, vp_ref, vc_ref, y_ref, l_ref, o_scr, l_scr):
        hp, n = pl.program_id(0), pl.program_id(1)
        lo = lax.broadcasted_iota(jnp.int32, (1, LANE), 1) < ATTN_HD
        for b, d in enumerate(DILATIONS):
            full, edge = _attn_bias(d, hp, n == 0)

            def sub(g, r, b=b, d=d, full=full, edge=edge):
                rows, before = _attn_rows(d, g, r), _attn_rows(d, g - 1, r)
                kb_ref, vb_ref = (kp_ref, vp_ref) if g == 0 else (kc_ref, vc_ref)
                bias = edge if g == 0 else full
                qv = q_ref[rows, :].astype(BF16)
                kv = jnp.concatenate([kb_ref[before, :], kc_ref[rows, :]], axis=0).astype(BF16)
                vv = jnp.concatenate([vb_ref[before, :], vc_ref[rows, :]], axis=0).astype(BF16)
                outs, lses = [], []
                for h in range(2):
                    qm = jnp.where(lo == (h == 0), qv, jnp.zeros_like(qv))
                    s = _dot(qm, kv, NT) + bias[h]
                    m = jnp.max(s, axis=-1, keepdims=True)
                    p = jnp.exp(s - m)
                    den = jnp.sum(p, axis=-1, keepdims=True)
                    outs.append(_dot(p.astype(BF16), vv, NN) / den)
                    lses.append(m + jnp.log(den))
                kept = _attn_rows(d, g, r, base=b * T)
                o_scr[kept, :] = jnp.where(lo, outs[0], outs[1])
                l_scr[kept, :] = jnp.where(lo, lses[0], lses[1])

            _for_blocks(d, T // (d * ATTN_BLOCK), sub)
        l1, l2, l3 = [l_scr[pl.ds(b * T, T), :] for b in range(len(DILATIONS))]
        o1, o2, o3 = [o_scr[pl.ds(b * T, T), :] for b in range(len(DILATIONS))]
        m = jnp.maximum(jnp.maximum(l1, l2), l3)
        e1, e2, e3 = jnp.exp(l1 - m), jnp.exp(l2 - m), jnp.exp(l3 - m)
        tot = e1 + e2 + e3
        y_ref[...] = (e1 * o1 + e2 * o2 + e3 * o3) / tot
        l_ref[...] = m + jnp.log(tot)

    cur, prev, _ = specs()
    vcur, vprev, _ = specs(O_AV // LANE)
    return pl.pallas_call(
        body, grid=(W // LANE, nb), in_specs=[cur, prev, cur, vprev, vcur], out_specs=[cur, cur],
        out_shape=[jax.ShapeDtypeStruct((S, W), F32)] * 2,
        scratch_shapes=[pltpu.VMEM((len(DILATIONS) * T, LANE), F32)] * 2,
        compiler_params=_cp("parallel", "arbitrary"), name="attn_fwd")(qn, kn, kn, proj, proj)


def _attn_mix(y_gla, y_att, ts):
    S, W = y_att.shape

    def body(yg, ya, mixed_ref, mixed_t_ref):
        y = ya[...]
        mixed_ref[:, :W] = yg[...]
        mixed_ref[:, W:] = y.astype(BF16)
        mixed_t_ref[:W, :] = yg[...].astype(F32).T.astype(BF16)
        mixed_t_ref[W:, :] = y.T.astype(BF16)

    spec = pl.BlockSpec((ts, W), lambda i: (i, 0))
    return pl.pallas_call(
        body, grid=(S // ts,), in_specs=[spec] * 2,
        out_specs=[pl.BlockSpec((ts, 2 * W), lambda i: (i, 0)), _col_spec(2 * W, ts)],
        out_shape=[jax.ShapeDtypeStruct((S, 2 * W), BF16), jax.ShapeDtypeStruct((2 * W, S), BF16)],
        compiler_params=_cp("parallel"), name="attn_mix")(y_gla, y_att)


def _attn_dq(qn, kn, proj, dy, lse, delta):
    S, W = qn.shape
    nb, specs = _attn_specs(S)

    def body(q_ref, kp_ref, kc_ref, vp_ref, vc_ref, dy_ref, l_ref, de_ref, dq_ref):
        hp, n = pl.program_id(0), pl.program_id(1)
        lo = lax.broadcasted_iota(jnp.int32, (1, LANE), 1) < ATTN_HD
        for b, d in enumerate(DILATIONS):
            _attn_dq_branch(b, d, _attn_bias(d, hp, n == 0), lo, q_ref, kp_ref, kc_ref, vp_ref, vc_ref, dy_ref, l_ref, de_ref, dq_ref)

    cur, prev, _ = specs()
    vcur, vprev, _ = specs(O_AV // LANE)
    return pl.pallas_call(
        body, grid=(W // LANE, nb), in_specs=[cur, prev, cur, vprev, vcur, cur, cur, cur], out_specs=cur,
        out_shape=jax.ShapeDtypeStruct((S, W), F32),
        compiler_params=_cp("parallel", "arbitrary"), name="attn_dq")(qn, kn, kn, proj, proj, dy, lse, delta)


def _attn_dq_branch(b, d, biases, lo, q_ref, kp_ref, kc_ref, vp_ref, vc_ref, dy_ref, l_ref, de_ref, dq_ref):
    full, edge = biases

    def sub(g, r):
        rows, before = _attn_rows(d, g, r), _attn_rows(d, g - 1, r)
        kb_ref, vb_ref = (kp_ref, vp_ref) if g == 0 else (kc_ref, vc_ref)
        bias = edge if g == 0 else full
        qv, dyv = q_ref[rows, :].astype(BF16), dy_ref[rows, :]
        lv, dev = l_ref[rows, :], de_ref[rows, :]
        kv = jnp.concatenate([kb_ref[before, :], kc_ref[rows, :]], axis=0).astype(BF16)
        vv = jnp.concatenate([vb_ref[before, :], vc_ref[rows, :]], axis=0).astype(BF16)
        outs = []
        for h in range(2):
            sel = lo == (h == 0)
            qm = jnp.where(sel, qv, jnp.zeros_like(qv))
            dym = jnp.where(sel, dyv, 0.0).astype(BF16)
            lse_h = lv[:, h * ATTN_HD:h * ATTN_HD + 1]
            del_h = dev[:, h * ATTN_HD:h * ATTN_HD + 1]
            p = jnp.exp(_dot(qm, kv, NT) + bias[h] - lse_h)
            ds = p * (_dot(dym, vv, NT) - del_h)
            outs.append(_dot(ds.astype(BF16), kv, NN) * (ATTN_HD ** -0.5))
        dq = jnp.where(lo, outs[0], outs[1])
        dq_ref[rows, :] = dq if b == 0 else dq_ref[rows, :] + dq

    _for_blocks(d, ATTN_TILE // (d * ATTN_BLOCK), sub)


def _attn_dkv(qn, kn, proj, dy, lse, delta):
    S, W = qn.shape
    nb, specs = _attn_specs(S)

    def body(k_ref, v_ref, qc_ref, qn_ref, dyc_ref, dyn_ref, lc_ref, ln_ref, dec_ref, den_ref, dk_ref, dv_ref):
        hp, n = pl.program_id(0), pl.program_id(1)
        lo = lax.broadcasted_iota(jnp.int32, (1, LANE), 1) < ATTN_HD
        cur_refs, next_refs = (qc_ref, dyc_ref, lc_ref, dec_ref), (qn_ref, dyn_ref, ln_ref, den_ref)
        for b, d in enumerate(DILATIONS):
            _attn_dkv_branch(b, d, _attn_bias_t(d, hp, n + 1 < nb), lo, k_ref, v_ref, cur_refs, next_refs, dk_ref, dv_ref)

    cur, _, nxt = specs()
    vcur, _, _ = specs(O_AV // LANE)
    return pl.pallas_call(
        body, grid=(W // LANE, nb), in_specs=[cur, vcur, cur, nxt, cur, nxt, cur, nxt, cur, nxt], out_specs=[cur, cur],
        out_shape=[jax.ShapeDtypeStruct((S, W), F32)] * 2,
        compiler_params=_cp("parallel", "arbitrary"), name="attn_dkv")(
            kn, proj, qn, qn, dy, dy, lse, lse, delta, delta)


def _attn_dkv_branch(b, d, biases, lo, k_ref, v_ref, cur_refs, next_refs, dk_ref, dv_ref):
    B = ATTN_BLOCK
    own, inner, outer = biases
    G = ATTN_TILE // (d * B)

    def sub(g, r):
        rows = _attn_rows(d, g, r)
        kv, vv = k_ref[rows, :].astype(BF16), v_ref[rows, :].astype(BF16)
        dk = jnp.zeros((B, LANE), F32)
        dv = jnp.zeros((B, LANE), F32)
        inside = g + 1 < G
        after = _attn_rows(d, g + 1 if inside else 0, r)
        for bias, qrows, (q_ref, dy_ref, l_ref, de_ref) in (
                (own, rows, cur_refs), (inner if inside else outer, after, cur_refs if inside else next_refs)):
            qv, dyv = q_ref[qrows, :].astype(BF16), dy_ref[qrows, :]
            lt, det = l_ref[qrows, :].T, de_ref[qrows, :].T
            for h in range(2):
                sel = lo == (h == 0)
                qm = jnp.where(sel, qv, jnp.zeros_like(qv))
                dym = jnp.where(sel, dyv, 0.0).astype(BF16)
                lse_h = lt[h * ATTN_HD:h * ATTN_HD + 1, :]
                del_h = det[h * ATTN_HD:h * ATTN_HD + 1, :]
                pt = jnp.exp(_dot(kv, qm, NT) + bias[h] - lse_h)
                dv = dv + _dot(pt.astype(BF16), dym, NN)
                dst = pt * (_dot(vv, dym, NT) - del_h)
                dk = dk + _dot(dst.astype(BF16), qm, NN)
        dk_ref[rows, :] = dk if b == 0 else dk_ref[rows, :] + dk
        dv_ref[rows, :] = dv if b == 0 else dv_ref[rows, :] + dv

    _for_blocks(d, G, sub)


def _attn_post(dq, dk, dv, proj, qg, kg, ts):
    S = proj.shape[0]
    W = ATTN_DIM

    def body(dq_ref, dk_ref, dv_ref, aq_ref, ak_ref, qg_ref, kg_ref, daq_ref, dak_ref, dav_ref, gg_ref):
        i = pl.program_id(0)
        seg = _seg_matrix(W, ATTN_HD, 1.0 / ATTN_HD)
        gsums = []
        for d_ref, x_ref, g_ref, o_ref in ((dq_ref, aq_ref, qg_ref, daq_ref), (dk_ref, ak_ref, kg_ref, dak_ref)):
            dy = d_ref[...]
            xv = x_ref[...]
            r = lax.rsqrt(_seg_sum(xv * xv, seg) + EPS)
            xh = xv * r
            dxh = dy * g_ref[...]
            o_ref[...] = (r * (dxh - xh * _seg_sum(dxh * xh, seg))).astype(BF16)
            gsums.append(jnp.sum(dy * xh, axis=0, keepdims=True))
        dav_ref[...] = dv_ref[...].astype(BF16)
        _accumulate(gg_ref, _rows8(gsums, W), i)

    row = pl.BlockSpec((ts, W), lambda i: (i, 0))
    blk = lambda off: pl.BlockSpec((ts, W), lambda i: (i, off // W))
    vec = pl.BlockSpec((1, W), lambda i: (0, 0))
    return pl.pallas_call(
        body, grid=(S // ts,), in_specs=[row] * 3 + [blk(O_AQ), blk(O_AK), vec, vec],
        out_specs=[row, row, row, pl.BlockSpec((8, W), lambda i: (0, 0))],
        out_shape=[jax.ShapeDtypeStruct((S, W), BF16)] * 3 + [jax.ShapeDtypeStruct((8, W), F32)],
        compiler_params=_cp("arbitrary"), name="attn_post")(dq, dk, dv, proj, proj, qg, kg)


def _shift_down(cur, halo, n):
    return pltpu.roll(jnp.concatenate([halo, cur], axis=0), n, 0)[8:]


def _shift_up(cur, halo, n):
    ts = cur.shape[0]
    return pltpu.roll(jnp.concatenate([cur, halo], axis=0), ts + 8 - n, 0)[:ts]


def _conv(cur, halo, w, b):
    return b + w[0:1, :] * _shift_down(cur, halo, 2) + w[1:2, :] * _shift_down(cur, halo, 1) + w[2:3, :] * cur


def _mm_up_swiglu(h2, w_up, conv_w8, conv_b, tm, tc, ride=None):
    S, D = h2.shape
    F = w_up.shape[1] // 2
    nc = F // tc
    grid = (S // tm, nc)
    ride_arrays, ride_gather = ride if ride else ([], [])
    nr = len(ride_arrays)

    def body(h_ref, bg_ref, bv_ref, wg_ref, wv_ref, cg_ref, cv_ref, u0_ref, a_ref, at_ref, halo):
        i, j = pl.program_id(0), pl.program_id(1)
        hv = h_ref[...]
        acts = []
        for h, (b_ref, w_ref, c_ref) in enumerate(((bg_ref, wg_ref, cg_ref), (bv_ref, wv_ref, cv_ref))):
            u = _dot(hv, b_ref[...], NN)
            u0_ref[h] = u
            acts.append(_conv(u, jnp.where(i == 0, 0.0, halo[j, h]), w_ref[...], c_ref[...]))
            halo[j, h] = u[tm - 8:, :]
        g, v = acts
        a = g * _sigmoid(g) * v
        a_ref[...] = a.astype(BF16)
        at_ref[...] = a.T.astype(BF16)

    wcol = lambda rows, off: pl.BlockSpec((rows, tc), lambda i, j: (0, j + off))
    outs = pl.pallas_call(
        _riding(body, 7, 3, ride_gather, grid), grid=grid,
        in_specs=[pl.BlockSpec((tm, D), lambda i, j: (i, 0)), wcol(D, 0), wcol(D, nc), wcol(8, 0), wcol(8, nc), wcol(1, 0), wcol(1, nc)]
        + [HBM_SPEC] * nr,
        out_specs=[pl.BlockSpec((2, tm, tc), lambda i, j: (0, i, j)), pl.BlockSpec((tm, tc), lambda i, j: (i, j)),
                   pl.BlockSpec((tc, tm), lambda i, j: (j, i))] + [HBM_SPEC] * nr,
        out_shape=[jax.ShapeDtypeStruct((2, S, F), F32), jax.ShapeDtypeStruct((S, F), BF16), jax.ShapeDtypeStruct((F, S), BF16)]
        + _exchange_shapes(ride_arrays, ride_gather),
        scratch_shapes=[pltpu.VMEM((nc, 2, 8, tc), F32)] + (_exchange_sems(nr) if nr else []),
        compiler_params=_cp("arbitrary", "arbitrary"), name="mm_up")(
            h2, w_up, w_up, conv_w8, conv_w8, conv_b, conv_b, *ride_arrays)
    return outs[0], outs[1], outs[2], outs[3:]


def _mm_da_du0(dt2, w_down, u0, conv_w8, conv_b, tm, tc, ride=None):
    _, S, F = u0.shape
    D = dt2.shape[1]
    hb = tm // 8
    nrow = S // tm
    grid = (nrow,)
    ride_arrays, ride_gather = ride if ride else ([], [])
    nr = len(ride_arrays)

    def body(dt_ref, wd_ref, ug_ref, ugh_ref, uv_ref, uvh_ref, w_ref, b_ref, o_ref, sg_ref, sv_ref, following):
        i = pl.program_id(0)
        at_start, at_end = i == nrow - 1, i == 0
        dt = dt_ref[...]
        for c in range(F // tc):
            sums = []
            halves = []
            for h, (u_ref, h_ref) in enumerate(((ug_ref, ugh_ref), (uv_ref, uvh_ref))):
                cols = slice(h * F + c * tc, h * F + (c + 1) * tc)
                u, halo, w = u_ref[:, c * tc:(c + 1) * tc], jnp.where(at_start, 0.0, h_ref[:, c * tc:(c + 1) * tc]), w_ref[:, cols]
                s2, s1 = _shift_down(u, halo, 2), _shift_down(u, halo, 1)
                halves.append((b_ref[:, cols] + w[0:1, :] * s2 + w[1:2, :] * s1 + w[2:3, :] * u, s2, s1, u, w, cols))
            g, v = halves[0][0], halves[1][0]
            dav = _dot(dt, wd_ref[c * tc:(c + 1) * tc, :], NT)
            sig = _sigmoid(g)
            dus = (dav * v * (sig * (1.0 + g * (1.0 - sig))), dav * (g * sig))
            for h, du in enumerate(dus):
                _, s2, s1, u, w, cols = halves[h]
                after = jnp.where(at_end, 0.0, following[h, :, c * tc:(c + 1) * tc])
                o_ref[:, cols] = (w[2:3, :] * du + w[1:2, :] * _shift_up(du, after, 1) + w[0:1, :] * _shift_up(du, after, 2)).astype(BF16)
                following[h, :, c * tc:(c + 1) * tc] = du[0:8, :]
                sums.append(_rows8([jnp.sum(du * s2, axis=0, keepdims=True), jnp.sum(du * s1, axis=0, keepdims=True),
                                    jnp.sum(du * u, axis=0, keepdims=True), jnp.sum(du, axis=0, keepdims=True)], tc))
            for sums_ref, part in zip((sg_ref, sv_ref), sums):
                @pl.when(i == 0)
                def _(sums_ref=sums_ref, part=part, c=c):
                    sums_ref[:, c * tc:(c + 1) * tc] = part

                @pl.when(i > 0)
                def _(sums_ref=sums_ref, part=part, c=c):
                    sums_ref[:, c * tc:(c + 1) * tc] += part

    rev = lambda i: nrow - 1 - i
    main = lambda h: pl.BlockSpec((None, tm, F), lambda i: (h, rev(i), 0))
    halo = lambda h: pl.BlockSpec((None, 8, F), lambda i: (h, jnp.maximum(rev(i) * hb - 1, 0), 0))
    whole = lambda a: pl.BlockSpec(a.shape, lambda i: (0,) * a.ndim, pipeline_mode=pl.Buffered(1))
    sums_spec = pl.BlockSpec((8, F), lambda i: (0, 0))
    outs = pl.pallas_call(
        _riding(body, 8, 3, ride_gather, grid), grid=grid,
        in_specs=[pl.BlockSpec((tm, D), lambda i: (rev(i), 0)), whole(w_down), main(0), halo(0), main(1), halo(1),
                  whole(conv_w8), whole(conv_b)] + [HBM_SPEC] * nr,
        out_specs=[pl.BlockSpec((tm, 2 * F), lambda i: (rev(i), 0)), sums_spec, sums_spec] + [HBM_SPEC] * nr,
        out_shape=[jax.ShapeDtypeStruct((S, 2 * F), BF16), jax.ShapeDtypeStruct((8, F), F32), jax.ShapeDtypeStruct((8, F), F32)]
        + _exchange_shapes(ride_arrays, ride_gather),
        scratch_shapes=[pltpu.VMEM((2, 8, F), F32)] + (_exchange_sems(nr) if nr else []),
        compiler_params=_cp("arbitrary"), name="mm_da")(
            dt2, w_down, u0, u0, u0, u0, conv_w8, conv_b, *ride_arrays)
    return outs[0], outs[1], outs[2], outs[3:]


def _adamw(w, g, m, v, name, slots=False):
    shape = w.shape
    view = (math.prod(shape[:-1]), shape[-1])
    R, C = view
    limit = SUM_BLOCK_ELEMS // 2 if slots else SUM_BLOCK_ELEMS
    fits = [t for t in range(16, R + 1, 16) if R % t == 0 and t * C <= limit]
    tr = max(fits) if fits else R

    def body(w_ref, g_ref, m_ref, v_ref, *outs):
        if slots:
            gv = g_ref[0].astype(F32)
            for s in range(1, N_DEV):
                gv = gv + g_ref[s].astype(F32)
            outs[0][...] = gv
        else:
            gv = g_ref[...]
        d_ref, nm_ref, nv_ref = outs[-3:]
        nm = ADAM_B1 * m_ref[...] + (1.0 - ADAM_B1) * gv
        nv = ADAM_B2 * v_ref[...] + (1.0 - ADAM_B2) * (gv * gv)
        m_hat = nm / (1.0 - ADAM_B1 ** ADAM_STEP)
        v_hat = nv / (1.0 - ADAM_B2 ** ADAM_STEP)
        d_ref[...] = -ADAM_LR * (m_hat / (jnp.sqrt(v_hat) + ADAM_EPS) + ADAM_WD * w_ref[...])
        nm_ref[...] = nm
        nv_ref[...] = nv

    spec = pl.BlockSpec((tr, C), lambda i: (i, 0))
    g_spec = pl.BlockSpec((N_DEV, tr, C), lambda i: (0, i, 0)) if slots else spec
    n_out = 4 if slots else 3
    outs = pl.pallas_call(
        body, grid=(R // tr,), in_specs=[spec, g_spec, spec, spec], out_specs=[spec] * n_out,
        out_shape=[jax.ShapeDtypeStruct(view, F32)] * n_out, compiler_params=_cp("parallel"), name=name)(
            w.reshape(view), g if slots else g.reshape(view), m.reshape(view), v.reshape(view))
    outs = [o.reshape(shape) for o in outs]
    return outs if slots else [g.reshape(shape)] + outs


def _pad_rows8(a):
    return jnp.concatenate([a, jnp.zeros((8 - a.shape[0], a.shape[1]), a.dtype)], axis=0)


def _local_step(x, target, mod, n1g, w_in_s, conv_w_s, wg_s, bg, gng, qng, kng, w_out_s, n2g, w_up_s, conv_b, w_down_s):
    S, D = x.shape
    F = w_down_s.shape[0] * N_DEV
    cw_c, wg_c = conv_w_s.shape[1], wg_s.shape[1]
    ts = min(512, S)
    sh1, sc1, g1, sh2, sc2, g2 = [mod[i:i + 1] for i in range(6)]
    qg_t, kg_t = jnp.tile(qng, (1, ATTN_HEADS)), jnp.tile(kng, (1, ATTN_HEADS))

    small = jnp.concatenate([conv_w_s.reshape(1, -1), wg_s.reshape(1, -1)], axis=1)
    n_small = small.shape[1]
    small = jnp.pad(small, ((0, 0), (0, -n_small % LANE)))
    h1, h1_t, (g_in, g_small) = _rms_mod(x, n1g, sc1, sh1, ts, "rms_mod1", ride=([w_in_s, small], [VIA_SIBLING, True]))
    w_in_full = _cols_from_blocks(g_in)
    w_in_p = jnp.concatenate([w_in_full[:, :GLR_SRC], w_in_full[:, GLR_SRC + GLA_RANK:],
                              w_in_full[:, GLR_SRC:GLR_SRC + GLA_RANK], jnp.zeros((D, PROJ_W - O_GLR - GLA_RANK), BF16)], axis=1)
    g_small = g_small.reshape(N_DEV, -1)
    conv_w8 = _pad_rows8(jnp.stack([g_small[:, t * cw_c:(t + 1) * cw_c].reshape(-1) for t in range(3)]))
    wg_full = _cols_from_blocks(g_small[:, 3 * cw_c:n_small].reshape(N_DEV, GLA_RANK, wg_c))
    wg_p = jnp.concatenate([wg_full, jnp.zeros((LANE - GLA_RANK, wg_full.shape[1]), F32)], axis=0)
    (proj, la, qn, kn), (g_out,) = _mm_in(h1, w_in_p, wg_p, bg, qg_t, kg_t, ts, ride=([w_out_s], [True]))
    w_out = g_out.reshape(-1, D)
    o_gla, states, (g_up,) = _gla_fwd(proj, la, 512, ride=([w_up_s], [True]))
    w_up = _cols_from_blocks(g_up)
    y_gla = _gla_out(o_gla, proj, gng, ts)
    y_att, lse = _attn_fwd(qn, kn, proj)
    mixed, mixed_t = _attn_mix(y_gla, y_att, ts)
    t1, x2, h2, h2_t = _mm_resid_rms_mod(mixed, w_out, x, g1, n2g, sc2, sh2, ts, "mm_out")
    tc = 1408 if F % 1408 == 0 else F
    u0, a, a_t, (g_down,) = _mm_up_swiglu(h2, w_up, conv_w8, conv_b, ts, tc, ride=([w_down_s], [True]))
    w_down = g_down.reshape(F, D)
    dx3, dt2, sums3 = _mm_loss_resid(a, w_down, x2, g2, target, ts, "mm_down")
    loss_row, dg2 = sums3[0:1], sums3[1:2]

    g_w_down = _mm(a_t, dt2, NN, 1408, 1024, 2048, F32, "mm_gw_down")
    du0, sums_g, sums_v, (r_down,) = _mm_da_du0(dt2, w_down, u0, conv_w8, conv_b, min(256, S), tc,
                                                ride=([g_w_down.reshape(N_DEV, -1, D)], [False]))
    g_conv_w = jnp.concatenate([sums_g[0:3], sums_v[0:3]], axis=1)
    g_conv_b = jnp.concatenate([sums_g[3:4], sums_v[3:4]], axis=1)
    g_w_up = _mm(h2_t, du0, NN, 512, 2816, 2048, F32, "mm_gw_up")
    (dx2, sums2, dt1), _ = _mm_rms_mod_bwd(du0, w_up, x2, dx3, n2g, sc2, ts, "mm_dh2", t_prev=t1, g_prev=g1)
    dsh2, dsc2, g_n2g, dg1 = sums2[0:1], sums2[1:2], sums2[2:3], sums2[3:4]
    g_w_out = _mm(mixed_t, dt1, NN, 1024, 1024, 2048, F32, "mm_gw_out")
    do_gla, dgr, gng_sums, dy_att, delta = _mm_mixed_bwd(dt1, w_out, o_gla, proj, gng, y_att, ts)
    dgq, dgk, dgv, dla, (r_up, r_out) = _gla_bwd(
        proj, la, do_gla, states, 512, ride=([_col_blocks(g_w_up), g_w_out.reshape(N_DEV, -1, D)], [False, False]))
    dglr, g_wg_p, gb_sums = _gate_bwd(dla, la, proj, wg_p, ts)
    dqn = _attn_dq(qn, kn, proj, dy_att, lse, delta)
    dkn, dvn = _attn_dkv(qn, kn, proj, dy_att, lse, delta)
    daq, dak, dav, qk_sums = _attn_post(dqn, dkn, dvn, proj, qg_t, kg_t, ts)
    dproj = jnp.concatenate([dgq, dgk, dgv, dgr, daq, dak, dav, dglr, jnp.zeros((S, PROJ_W - O_GLR - LANE), BF16)], axis=1)
    g_w_in_p = _mm(h1_t, dproj, NN, 512, PROJ_W, 1024, F32, "mm_gw_in")
    g_w_in = jnp.concatenate([g_w_in_p[:, :GLR_SRC], g_w_in_p[:, O_GLR:O_GLR + GLA_RANK], g_w_in_p[:, GLR_SRC:O_GLR]], axis=1)
    (dx, sums1), (r_in,) = _mm_rms_mod_bwd(dproj, w_in_p, x, dx2, n1g, sc1, ts, "mm_dh1",
                                           ride=([_col_blocks(g_w_in).astype(BF16)], [False]))
    dsh1, dsc1, g_n1g = sums1[0:1], sums1[1:2], sums1[2:3]

    dmod = jnp.concatenate([dsh1, dsc1, dg1, dsh2, dsc2, dg2], axis=1)
    grads = dict(n1g=g_n1g, w_in=r_in, wg=g_wg_p[:GLA_RANK], bg=gb_sums[0:1], gng=gng_sums[0:1],
                 qng_lanes=qk_sums[0:1], kng_lanes=qk_sums[1:2], w_out=r_out, n2g=g_n2g, w_up=r_up,
                 conv_w=g_conv_w, conv_b=g_conv_b, w_down=r_down)
    return loss_row, dx, dmod, grads


def _col_blocks(a):
    R, W = a.shape
    return a.reshape(R, N_DEV, W // N_DEV).transpose(1, 0, 2)


def _cols_from_blocks(a):
    n, R, C = a.shape
    return a.transpose(1, 0, 2).reshape(R, n * C)


def kernel(x, c, w_ada, b_ada, norm1_g, w_in, gla_w_gate, gla_b_gate, gla_norm_g, q_norm_g, k_norm_g, w_out, norm2_g, w_up, conv_w, conv_b, w_down, loss_target, m_w_ada, m_b_ada, m_norm1_g, m_w_in, m_gla_w_gate, m_gla_b_gate, m_gla_norm_g, m_q_norm_g, m_k_norm_g, m_w_out, m_norm2_g, m_w_up, m_conv_w, m_conv_b, m_w_down, v_w_ada, v_b_ada, v_norm1_g, v_w_in, v_gla_w_gate, v_gla_b_gate, v_gla_norm_g, v_q_norm_g, v_k_norm_g, v_w_out, v_norm2_g, v_w_up, v_conv_w, v_conv_b, v_w_down):
    axes = ("x", "y", "c")
    me = 4 * lax.axis_index("x") + 2 * lax.axis_index("y") + lax.axis_index("c")
    S, D = x.shape[1], x.shape[2]
    x2d, tgt2d = x[0], loss_target[0]
    w_in_s, w_out_s, w_up_s, w_down_s, w_ada_s = w_in[0], w_out[0], w_up[0], w_down[0], w_ada[0]
    conv_w_s, wg_s = conv_w[0], gla_w_gate[0]
    in_c, up_c, ada_c, wg_c, cw_c = w_in_s.shape[1], w_up_s.shape[1], w_ada_s.shape[1], wg_s.shape[1], conv_w_s.shape[1]
    F = w_down_s.shape[0] * N_DEV

    g_c, = _exchange([c], [True], "gather_c")
    c_all = g_c.reshape(N_DEV, D)

    b_shard = lax.dynamic_slice(b_ada, (0, me * ada_c), (1, ada_c))
    mod_part = _ada_fwd(c_all, w_ada_s, b_shard)
    mod_recv, = _exchange([mod_part.reshape(N_DEV, 1, ada_c)], [False], "exchange_mod")
    mod = mod_recv.reshape(6, D)

    loss_row, dx, dmod, gr = _local_step(
        x2d, tgt2d, mod, norm1_g, w_in_s.astype(BF16), conv_w_s, wg_s, gla_b_gate, gla_norm_g, q_norm_g, k_norm_g,
        w_out_s.astype(BF16), norm2_g, w_up_s.astype(BF16), conv_b, w_down_s.astype(BF16))
    loss = lax.psum(0.5 / D * jnp.sum(loss_row), axes)

    parts = [dmod, gr["n1g"], gr["bg"], gr["gng"], gr["qng_lanes"], gr["kng_lanes"], gr["n2g"], gr["conv_b"],
             gr["wg"].reshape(1, -1), gr["conv_w"].reshape(1, -1)]
    sizes = [p.shape[1] for p in parts]
    packed = jnp.concatenate(parts, axis=1)
    packed = jnp.pad(packed, ((0, 0), (0, -packed.shape[1] % (8 * LANE))))
    gathered, = _exchange([packed.reshape(8, -1)], [True], "gather_small_grads")
    gathered = gathered.reshape(N_DEV, -1)
    total = _sum_slots(gathered.reshape(N_DEV, 8, -1), "sum_small_grads").reshape(1, -1)
    offs = [0]
    for s_ in sizes:
        offs.append(offs[-1] + s_)
    t_dmod, t_n1g, t_bg, t_gng, t_qng, t_kng, t_n2g, t_conv_b, t_wg, t_conv_w = [
        total[:, offs[i]:offs[i + 1]] for i in range(len(sizes))]
    g_b_ada = t_dmod
    g_qng = t_qng.reshape(ATTN_HEADS, ATTN_HD).sum(axis=0, keepdims=True)
    g_kng = t_kng.reshape(ATTN_HEADS, ATTN_HD).sum(axis=0, keepdims=True)
    g_wg = lax.dynamic_slice(t_wg.reshape(GLA_RANK, -1), (0, me * wg_c), (GLA_RANK, wg_c))
    g_conv_w = lax.dynamic_slice(t_conv_w.reshape(3, -1), (0, me * cw_c), (3, cw_c))
    dmod_shard = lax.dynamic_slice(gathered[:, :6 * D], (0, me * ada_c), (N_DEV, ada_c))
    g_w_ada = _ada_bwd(c_all, dmod_shard)

    g_w_in, g_w_out, g_w_up, g_w_down = gr["w_in"], gr["w_out"], gr["w_up"], gr["w_down"]
    in_slots = {"w_in", "w_out", "w_up", "w_down"}
    names = ["w_ada", "b_ada", "norm1_g", "w_in", "gla_w_gate", "gla_b_gate", "gla_norm_g", "q_norm_g", "k_norm_g",
             "w_out", "norm2_g", "w_up", "conv_w", "conv_b", "w_down"]
    ws = [w_ada, b_ada, norm1_g, w_in, gla_w_gate, gla_b_gate, gla_norm_g, q_norm_g, k_norm_g, w_out, norm2_g, w_up, conv_w, conv_b, w_down]
    ms = [m_w_ada, m_b_ada, m_norm1_g, m_w_in, m_gla_w_gate, m_gla_b_gate, m_gla_norm_g, m_q_norm_g, m_k_norm_g, m_w_out, m_norm2_g, m_w_up, m_conv_w, m_conv_b, m_w_down]
    vs = [v_w_ada, v_b_ada, v_norm1_g, v_w_in, v_gla_w_gate, v_gla_b_gate, v_gla_norm_g, v_q_norm_g, v_k_norm_g, v_w_out, v_norm2_g, v_w_up, v_conv_w, v_conv_b, v_w_down]
    gs = [g_w_ada, g_b_ada, t_n1g, g_w_in, g_wg, t_bg, t_gng, g_qng, g_kng, g_w_out, t_n2g, g_w_up, g_conv_w, t_conv_b, g_w_down]
    grads, deltas, new_ms, new_vs = [], [], [], []
    for nm, w, g, m, v in zip(names, ws, gs, ms, vs):
        g_, d_, m_, v_ = _adamw(w, g, m, v, "adamw_" + nm, slots=nm in in_slots)
        grads.append(g_)
        deltas.append(d_)
        new_ms.append(m_)
        new_vs.append(v_)
    return (loss, dx.reshape(x.shape), *grads, *deltas, *new_ms, *new_vs)
```

```python
import functools
import math

import jax
import jax.numpy as jnp
from jax import lax
from jax.experimental import pallas as pl
from jax.experimental.pallas import tpu as pltpu

F32, BF16 = jnp.float32, jnp.bfloat16
HI = lax.Precision.HIGHEST
EPS = 1e-6
NEG = -1e30

N_DEV = 8
GLA_HEADS, GLA_DK, GLA_DV, GLA_RANK, GLA_TAU, GLA_CHUNK = 4, 64, 128, 16, 16.0, 64
ATTN_HEADS, ATTN_HD, ATTN_BLOCK = 8, 64, 128
DILATIONS = (1, 4, 16)
GLA_QK, GLA_V, ATTN_DIM = GLA_HEADS * GLA_DK, GLA_HEADS * GLA_DV, ATTN_HEADS * ATTN_HD
O_GQ, O_GK, O_GV, O_GR, O_AQ, O_AK, O_AV, O_GLR = 0, 256, 512, 1024, 1536, 2048, 2560, 3072
PROJ_W = 3328
LANE = 128
GLR_SRC = 2 * GLA_QK + 2 * GLA_V

ADAM_LR, ADAM_B1, ADAM_B2, ADAM_EPS, ADAM_WD, ADAM_STEP = 0.001, 0.9, 0.999, 1e-08, 0.01, 10

VMEM_LIMIT = 56 * 1024 * 1024
SUM_BLOCK_ELEMS = 256 * 1024


def _cp(*sem):
    return pltpu.CompilerParams(dimension_semantics=sem, vmem_limit_bytes=VMEM_LIMIT)


def _dot(a, b, dims, precision=None):
    return lax.dot_general(a, b, (dims, ((), ())), preferred_element_type=F32, precision=precision)


NN, NT, TN = ((1,), (0,)), ((1,), (1,)), ((0,), (0,))


def _sigmoid(z):
    return 1.0 / (1.0 + jnp.exp(-z))


HBM_SPEC = pl.BlockSpec(memory_space=pltpu.HBM)


def _exchange_shapes(arrays, gather):
    return [jax.ShapeDtypeStruct((N_DEV,) + (a.shape if g else a.shape[1:]), a.dtype) for a, g in zip(arrays, gather)]


def _exchange_sems(n):
    return [pltpu.SemaphoreType.DMA((n * (N_DEV - 1),)), pltpu.SemaphoreType.DMA((n * (N_DEV - 1),)), pltpu.SemaphoreType.DMA((n,))]


VIA_SIBLING = "via sibling"


def _exchange_plan(ins, outs, gather, send_sems, recv_sems, local_sems):
    x, y, c = lax.axis_index("x"), lax.axis_index("y"), lax.axis_index("c")
    me = 4 * x + 2 * y + c
    start, relays, waits = [], [], []
    for a in range(len(ins)):
        if gather[a] == VIA_SIBLING:
            def copy(i, block, to, src=None, a=a):
                slot = outs[a].at[4 * block[0] + 2 * block[1] + block[2]]
                return pltpu.make_async_remote_copy(
                    src_ref=slot if src is None else src, dst_ref=slot, send_sem=send_sems.at[a * (N_DEV - 1) + i],
                    recv_sem=recv_sems.at[a * (N_DEV - 1) + i], device_id=to, device_id_type=pl.DeviceIdType.MESH)

            chips = [(1 - x, y), (x, 1 - y), (1 - x, 1 - y)]
            first = [copy(0, (x, y, c), (x, y, 1 - c), src=ins[a])]
            first += [copy(1 + j, (x, y, c), (*chip, c), src=ins[a]) for j, chip in enumerate(chips)]
            passed = [copy(4 + j, (*chip, c), (x, y, 1 - c)) for j, chip in enumerate(chips)]
            start += first
            relays += [(copy(1 + j, (*chip, c), (x, y, c)).wait_recv, passed[j]) for j, chip in enumerate(chips)]
            waits += [copy(0, (x, y, 1 - c), (x, y, c)).wait_recv]
            waits += [copy(4 + j, (*chip, 1 - c), (x, y, c)).wait_recv for j, chip in enumerate(chips)]
            waits += [cp.wait_send for cp in first + passed]
        else:
            for p in range(1, N_DEV):
                px, py, pc = x ^ (p >> 2), y ^ ((p >> 1) & 1), c ^ (p & 1)
                peer = 4 * px + 2 * py + pc
                k = a * (N_DEV - 1) + p - 1
                cp = pltpu.make_async_remote_copy(
                    src_ref=ins[a] if gather[a] else ins[a].at[peer], dst_ref=outs[a].at[me],
                    send_sem=send_sems.at[k], recv_sem=recv_sems.at[k],
                    device_id=(px, py, pc), device_id_type=pl.DeviceIdType.MESH)
                start.append(cp)
                waits.append(cp.wait)
        own = pltpu.make_async_copy(ins[a] if gather[a] else ins[a].at[me], outs[a].at[me], local_sems.at[a])
        start.append(own)
        waits.append(own.wait)
    return start, relays, waits


def _exchange_finish(relays, waits):
    for arrived, pass_on in relays:
        arrived()
        pass_on.start()
    for wait in waits:
        wait()


def _riding(body, n_in, n_out, gather, grid):
    nr = len(gather)
    if not nr:
        return body

    def wrapped(*refs):
        ins, r_ins = refs[:n_in], refs[n_in:n_in + nr]
        outs, r_outs = refs[n_in + nr:n_in + nr + n_out], refs[n_in + nr + n_out:n_in + 2 * nr + n_out]
        scratch = refs[n_in + 2 * nr + n_out:]
        first = last = None
        for t, steps in enumerate(grid):
            pid = pl.program_id(t)
            first = (pid == 0) if first is None else first & (pid == 0)
            last = (pid == steps - 1) if last is None else last & (pid == steps - 1)
        start, relays, waits = _exchange_plan(r_ins, r_outs, gather, *scratch[-3:])

        @pl.when(first)
        def _():
            for cp in start:
                cp.start()

        body(*ins, *outs, *scratch[:-3])

        @pl.when(last)
        def _():
            _exchange_finish(relays, waits)

    return wrapped


def _exchange(arrays, gather, name):
    n = len(arrays)

    def body(*refs):
        start, relays, waits = _exchange_plan(refs[:n], refs[n:2 * n], gather, *refs[2 * n:])
        for cp in start:
            cp.start()
        _exchange_finish(relays, waits)

    return pl.pallas_call(
        body, out_shape=_exchange_shapes(arrays, gather), in_specs=[HBM_SPEC] * n, out_specs=[HBM_SPEC] * n,
        scratch_shapes=_exchange_sems(n), name=name)(*arrays)


def _sum_slots(x, name):
    _, R, C = x.shape
    tr = max(t for t in range(8, min(SUM_BLOCK_ELEMS // C, R) + 1, 8) if R % t == 0)

    def body(x_ref, o_ref):
        acc = x_ref[0].astype(F32)
        for s in range(1, N_DEV):
            acc = acc + x_ref[s].astype(F32)
        o_ref[...] = acc

    return pl.pallas_call(
        body, grid=(R // tr,), in_specs=[pl.BlockSpec((N_DEV, tr, C), lambda i: (0, i, 0))],
        out_specs=pl.BlockSpec((tr, C), lambda i: (i, 0)), out_shape=jax.ShapeDtypeStruct((R, C), F32),
        compiler_params=_cp("parallel"), name=name)(x)


def _mm(a, b, mode, tm, tn, tk, out_dtype, name, ride=None):
    if mode == NN:
        (M, K), N = a.shape, b.shape[1]
    elif mode == NT:
        (M, K), N = a.shape, b.shape[0]
    else:
        (K, M), N = a.shape, b.shape[1]
    tm, tn, tk = min(tm, M), min(tn, N), min(tk, K)
    assert M % tm == 0 and N % tn == 0 and K % tk == 0, (name, M, N, K, tm, tn, tk)
    nk = K // tk
    if mode == NN:
        a_spec = pl.BlockSpec((tm, tk), lambda i, j, k: (i, k))
        b_spec = pl.BlockSpec((tk, tn), lambda i, j, k: (k, j))
    elif mode == NT:
        a_spec = pl.BlockSpec((tm, tk), lambda i, j, k: (i, k))
        b_spec = pl.BlockSpec((tn, tk), lambda i, j, k: (j, k))
    else:
        a_spec = pl.BlockSpec((tk, tm), lambda i, j, k: (k, i))
        b_spec = pl.BlockSpec((tk, tn), lambda i, j, k: (k, j))

    ride_arrays, ride_gather = ride if ride else ([], [])
    nr = len(ride_arrays)
    grid = (M // tm, N // tn, nk)

    own_acc = nk > 1 and out_dtype != F32

    def body(a_ref, b_ref, o_ref, *acc):
        p = _dot(a_ref[...].astype(BF16), b_ref[...].astype(BF16), mode)
        if nk == 1:
            o_ref[...] = p.astype(out_dtype)
        else:
            acc_ref = acc[0] if own_acc else o_ref
            k = pl.program_id(2)

            @pl.when(k == 0)
            def _():
                acc_ref[...] = p

            @pl.when(k > 0)
            def _():
                acc_ref[...] += p

            if own_acc:
                @pl.when(k == nk - 1)
                def _():
                    o_ref[...] = acc_ref[...].astype(out_dtype)

    outs = pl.pallas_call(
        _riding(body, 2, 1, ride_gather, grid), grid=grid, in_specs=[a_spec, b_spec] + [HBM_SPEC] * nr,
        out_specs=[pl.BlockSpec((tm, tn), lambda i, j, k: (i, j))] + [HBM_SPEC] * nr,
        out_shape=[jax.ShapeDtypeStruct((M, N), out_dtype)] + _exchange_shapes(ride_arrays, ride_gather),
        scratch_shapes=([pltpu.VMEM((tm, tn), F32)] if own_acc else []) + (_exchange_sems(nr) if nr else []),
        compiler_params=_cp(*(("arbitrary",) * 3 if nr else ("parallel", "parallel", "arbitrary"))), name=name)(a, b, *ride_arrays)
    return (outs[0], outs[1:]) if nr else outs[0]


def _ada_fwd(c_all, w_shard, b_shard):
    Nc = w_shard.shape[1]

    def body(c_ref, w_ref, b_ref, o_ref):
        cv = c_ref[...]
        o_ref[...] = _dot(cv * _sigmoid(cv), w_ref[...], NN, HI) + b_ref[...]

    return pl.pallas_call(body, out_shape=jax.ShapeDtypeStruct((N_DEV, Nc), F32), name="ada_fwd",
                          compiler_params=pltpu.CompilerParams(vmem_limit_bytes=VMEM_LIMIT))(c_all, w_shard, b_shard)


def _ada_bwd(c_all, dmod_shard):
    D, Nc = c_all.shape[1], dmod_shard.shape[1]

    def body(c_ref, d_ref, o_ref):
        cv = c_ref[...]
        o_ref[...] = _dot(cv * _sigmoid(cv), d_ref[...], TN, HI)

    return pl.pallas_call(body, out_shape=jax.ShapeDtypeStruct((D, Nc), F32), name="ada_bwd",
                          compiler_params=pltpu.CompilerParams(vmem_limit_bytes=VMEM_LIMIT))(c_all, dmod_shard)


def _row_spec(ts, D):
    return pl.BlockSpec((ts, D), lambda i: (i, 0))


def _vec_spec(D):
    return pl.BlockSpec((1, D), lambda i: (0, 0))


def _col_spec(D, ts):
    return pl.BlockSpec((D, ts), lambda i: (0, i))


def _rms_mod(x, ng, sc, sh, ts, name, ride=None):
    S, D = x.shape
    ride_arrays, ride_gather = ride if ride else ([], [])
    nr = len(ride_arrays)
    grid = (S // ts,)

    def body(x_ref, ng_ref, sc_ref, sh_ref, h_ref, ht_ref):
        xv = x_ref[...]
        r = lax.rsqrt(jnp.mean(xv * xv, axis=-1, keepdims=True) + EPS)
        h = xv * r * ng_ref[...] * (1.0 + sc_ref[...]) + sh_ref[...]
        h_ref[...] = h.astype(BF16)
        ht_ref[...] = h.T.astype(BF16)

    outs = pl.pallas_call(
        _riding(body, 4, 2, ride_gather, grid), grid=grid, in_specs=[_row_spec(ts, D)] + [_vec_spec(D)] * 3 + [HBM_SPEC] * nr,
        out_specs=[_row_spec(ts, D), _col_spec(D, ts)] + [HBM_SPEC] * nr,
        out_shape=[jax.ShapeDtypeStruct((S, D), BF16), jax.ShapeDtypeStruct((D, S), BF16)] + _exchange_shapes(ride_arrays, ride_gather),
        scratch_shapes=_exchange_sems(nr) if nr else [],
        compiler_params=_cp("arbitrary"), name=name)(x, ng, sc, sh, *ride_arrays)
    return outs[0], outs[1], outs[2:]


def _mm_rows(a, b, mode, tm, extras, extra_specs, out_shapes, out_specs, epilogue, name, ride=None):
    M, K = a.shape
    grid = (M // tm,)
    ride_arrays, ride_gather = ride if ride else ([], [])
    nr = len(ride_arrays)

    def body(a_ref, b_ref, *refs):
        epilogue(_dot(a_ref[...].astype(BF16), b_ref[...].astype(BF16), mode), pl.program_id(0), *refs)

    outs = pl.pallas_call(
        _riding(body, 2 + len(extras), len(out_shapes), ride_gather, grid), grid=grid,
        in_specs=[pl.BlockSpec((tm, K), lambda i: (i, 0)), pl.BlockSpec(b.shape, lambda i: (0, 0), pipeline_mode=pl.Buffered(1))]
        + list(extra_specs) + [HBM_SPEC] * nr,
        out_specs=list(out_specs) + [HBM_SPEC] * nr,
        out_shape=list(out_shapes) + _exchange_shapes(ride_arrays, ride_gather),
        scratch_shapes=_exchange_sems(nr) if nr else [],
        compiler_params=_cp("arbitrary"), name=name)(a, b, *extras, *ride_arrays)
    return outs[:len(out_shapes)], outs[len(out_shapes):]


def _accumulate(ref, part, step):
    @pl.when(step == 0)
    def _():
        ref[...] = part

    @pl.when(step > 0)
    def _():
        ref[...] += part


def _rows8(rows, width):
    return jnp.concatenate(rows + [jnp.zeros((8 - len(rows), width), F32)], axis=0)


def _mm_resid_rms_mod(a, w, x, g, ng, sc, sh, tm, name):
    S, D = x.shape

    def epilogue(t, step, x_ref, g_ref, ng_ref, sc_ref, sh_ref, t_ref, x2_ref, h_ref, ht_ref):
        t_ref[...] = t
        xv = x_ref[...] + g_ref[...] * t
        x2_ref[...] = xv
        r = lax.rsqrt(jnp.mean(xv * xv, axis=-1, keepdims=True) + EPS)
        h = xv * r * ng_ref[...] * (1.0 + sc_ref[...]) + sh_ref[...]
        h_ref[...] = h.astype(BF16)
        ht_ref[...] = h.T.astype(BF16)

    row, vec = _row_spec(tm, D), _vec_spec(D)
    full, half = jax.ShapeDtypeStruct((S, D), F32), jax.ShapeDtypeStruct((S, D), BF16)
    outs, _ = _mm_rows(a, w, NN, tm, [x, g, ng, sc, sh], [row] + [vec] * 4,
                       [full, full, half, jax.ShapeDtypeStruct((D, S), BF16)], [row, row, row, _col_spec(D, tm)], epilogue, name)
    return outs


def _mm_loss_resid(a, w, x2, g2, target, tm, name):
    S, D = x2.shape

    def epilogue(t, step, x_ref, y_ref, g_ref, dx_ref, dt_ref, sums_ref):
        gv = g_ref[...]
        e = x_ref[...] + gv * t - y_ref[...]
        dx = e * (1.0 / D)
        dx_ref[...] = dx
        dt_ref[...] = (dx * gv).astype(BF16)
        _accumulate(sums_ref, _rows8([jnp.sum(e * e, axis=0, keepdims=True), jnp.sum(dx * t, axis=0, keepdims=True)], D), step)

    row, vec = _row_spec(tm, D), _vec_spec(D)
    outs, _ = _mm_rows(a, w, NN, tm, [x2, target, g2], [row, row, vec],
                       [jax.ShapeDtypeStruct((S, D), F32), jax.ShapeDtypeStruct((S, D), BF16), jax.ShapeDtypeStruct((8, D), F32)],
                       [row, row, pl.BlockSpec((8, D), lambda i: (0, 0))], epilogue, name)
    return outs


def _mm_rms_mod_bwd(a, w, xin, dres, ng, sc, tm, name, t_prev=None, g_prev=None, ride=None):
    S, D = xin.shape
    chain = t_prev is not None

    def epilogue(dhv, step, *refs):
        if chain:
            x_ref, dr_ref, ng_ref, sc_ref, t_ref, g_ref, dx_ref, sums_ref, dt_ref = refs
        else:
            x_ref, dr_ref, ng_ref, sc_ref, dx_ref, sums_ref = refs
        xv = x_ref[...]
        r = lax.rsqrt(jnp.mean(xv * xv, axis=-1, keepdims=True) + EPS)
        xh = xv * r
        ngv, scv = ng_ref[...], sc_ref[...]
        dxh = dhv * (ngv * (1.0 + scv))
        dx = dr_ref[...] + r * (dxh - xh * jnp.mean(dxh * xh, axis=-1, keepdims=True))
        dx_ref[...] = dx
        dhx = dhv * xh
        rows = [jnp.sum(dhv, axis=0, keepdims=True), jnp.sum(dhx * ngv, axis=0, keepdims=True),
                jnp.sum(dhx * (1.0 + scv), axis=0, keepdims=True)]
        if chain:
            dt_ref[...] = (dx * g_ref[...]).astype(BF16)
            rows.append(jnp.sum(dx * t_ref[...], axis=0, keepdims=True))
        _accumulate(sums_ref, _rows8(rows, D), step)

    row, vec = _row_spec(tm, D), _vec_spec(D)
    extras = [xin, dres, ng, sc] + ([t_prev, g_prev] if chain else [])
    extra_specs = [row, row, vec, vec] + ([row, vec] if chain else [])
    out_shapes = [jax.ShapeDtypeStruct((S, D), F32), jax.ShapeDtypeStruct((8, D), F32)] + (
        [jax.ShapeDtypeStruct((S, D), BF16)] if chain else [])
    out_specs = [row, pl.BlockSpec((8, D), lambda i: (0, 0))] + ([row] if chain else [])
    return _mm_rows(a, w, NT, tm, extras, extra_specs, out_shapes, out_specs, epilogue, name, ride=ride)


def _mm_in(h1, w_in_p, wg_p, bg, qg, kg, tm, ride=None):
    S = h1.shape[0]
    W = ATTN_DIM

    def epilogue(p, step, wg_ref, bg_ref, qg_ref, kg_ref, proj_ref, la_ref, qn_ref, kn_ref):
        proj_ref[...] = p
        z = _dot(p[:, O_GLR:O_GLR + LANE], wg_ref[...], NN, HI) + bg_ref[...]
        la_ref[...] = (jnp.minimum(z, 0.0) - jnp.log(1.0 + jnp.exp(-jnp.abs(z)))) * (1.0 / GLA_TAU)
        seg = _seg_matrix(W, ATTN_HD, 1.0 / ATTN_HD)
        for off, g_ref, o_ref, scale in ((O_AQ, qg_ref, qn_ref, ATTN_HD ** -0.5), (O_AK, kg_ref, kn_ref, 1.0)):
            xv = p[:, off:off + W]
            o_ref[...] = xv * lax.rsqrt(_seg_sum(xv * xv, seg) + EPS) * (g_ref[...] * scale)

    row = lambda w: pl.BlockSpec((tm, w), lambda i: (i, 0))
    const = lambda a: pl.BlockSpec(a.shape, lambda i: (0, 0))
    return _mm_rows(
        h1, w_in_p, NN, tm, [wg_p, bg, qg, kg], [const(wg_p), const(bg), const(qg), const(kg)],
        [jax.ShapeDtypeStruct((S, PROJ_W), F32), jax.ShapeDtypeStruct((S, GLA_QK), F32),
         jax.ShapeDtypeStruct((S, W), F32), jax.ShapeDtypeStruct((S, W), F32)],
        [row(PROJ_W), row(GLA_QK), row(W), row(W)], epilogue, "mm_in", ride=ride)


def _gate_bwd(dla, la, proj, wg_p, ts):
    S = proj.shape[0]

    def body(dla_ref, la_ref, glr_ref, w_ref, dglr_ref, gw_ref, gb_ref):
        i = pl.program_id(0)
        dz = dla_ref[...] * (1.0 / GLA_TAU) * (1.0 - jnp.exp(GLA_TAU * la_ref[...]))
        dglr_ref[...] = _dot(dz, w_ref[...], NT, HI).astype(BF16)
        gw = _dot(glr_ref[...], dz, TN, HI)
        gb = jnp.concatenate([jnp.sum(dz, axis=0, keepdims=True), jnp.zeros((7, GLA_QK), F32)], axis=0)

        @pl.when(i == 0)
        def _():
            gw_ref[...] = gw
            gb_ref[...] = gb

        @pl.when(i > 0)
        def _():
            gw_ref[...] += gw
            gb_ref[...] += gb

    return pl.pallas_call(
        body, grid=(S // ts,),
        in_specs=[pl.BlockSpec((ts, GLA_QK), lambda i: (i, 0)), pl.BlockSpec((ts, GLA_QK), lambda i: (i, 0)),
                  pl.BlockSpec((ts, LANE), lambda i: (i, O_GLR // LANE)), pl.BlockSpec((LANE, GLA_QK), lambda i: (0, 0))],
        out_specs=[pl.BlockSpec((ts, LANE), lambda i: (i, 0)), pl.BlockSpec((LANE, GLA_QK), lambda i: (0, 0)),
                   pl.BlockSpec((8, GLA_QK), lambda i: (0, 0))],
        out_shape=[jax.ShapeDtypeStruct((S, LANE), BF16), jax.ShapeDtypeStruct((LANE, GLA_QK), F32),
                   jax.ShapeDtypeStruct((8, GLA_QK), F32)],
        compiler_params=_cp("arbitrary"), name="gla_gate_bwd")(dla, la, proj, wg_p)


def _tri(lower):
    r = lax.broadcasted_iota(jnp.int32, (GLA_CHUNK, GLA_CHUNK), 0)
    c = lax.broadcasted_iota(jnp.int32, (GLA_CHUNK, GLA_CHUNK), 1)
    return jnp.where((r >= c) if lower else (c >= r), 1.0, 0.0).astype(F32)


GLA_SUB = 16
GLA_NSUB = GLA_CHUNK // GLA_SUB
PAIR_QK = 2 * GLA_DK
PAIR_V = 2 * GLA_DV


def _band_selector():
    r = lax.broadcasted_iota(jnp.int32, (GLA_SUB * PAIR_QK, LANE), 0)
    c = lax.broadcasted_iota(jnp.int32, (GLA_SUB * PAIR_QK, LANE), 1)
    dist, head = r // PAIR_QK, (r % PAIR_QK) // GLA_DK
    return jnp.where(c == head * GLA_DK + (GLA_SUB - 1 - dist), 1.0, 0.0).astype(BF16)


def _flip_matrix():
    r = lax.broadcasted_iota(jnp.int32, (GLA_CHUNK, GLA_CHUNK), 0)
    c = lax.broadcasted_iota(jnp.int32, (GLA_CHUNK, GLA_CHUNK), 1)
    return jnp.where(r + c == GLA_CHUNK - 1, 1.0, 0.0).astype(BF16)


def _state_mask():
    r = lax.broadcasted_iota(jnp.int32, (PAIR_V, PAIR_QK), 0)
    c = lax.broadcasted_iota(jnp.int32, (PAIR_V, PAIR_QK), 1)
    return (r < GLA_DV) == (c < GLA_DK)


class _GlaChunk:
    def __init__(self, qs, kc, vc, g, sel, exact):
        C = GLA_CHUNK
        self.qs, self.kc, self.vc = qs, kc, vc
        rows = lax.broadcasted_iota(jnp.int32, (C, 1), 0)
        lane = lax.broadcasted_iota(jnp.int32, (1, PAIR_QK), 1)
        self.rows, self.lane = rows, lane
        b = _dot(_tri(True), g, NN, HI)
        self.bl = b[C - 1:C, :]
        self.eb = jnp.exp(b)
        self.kdec = jnp.exp(self.bl - b)
        edge = lambda J: b[GLA_SUB * (J + 1):GLA_SUB * (J + 1) + 1, :]
        self.e_far = [jnp.exp(jnp.where(rows >= GLA_SUB * (J + 1), b - edge(J), NEG)) for J in range(GLA_NSUB - 1)]
        blk = rows // GLA_SUB
        bnext = edge(0)
        for J in range(1, GLA_NSUB - 1):
            bnext = jnp.where(blk == J, edge(J), bnext)
        self.e_khat = jnp.exp(jnp.where(blk < GLA_NSUB - 1, bnext - b, NEG))
        khat = kc * self.e_khat
        k2 = jnp.concatenate([jnp.where(lane < GLA_DK, khat, 0.0), jnp.where(lane >= GLA_DK, khat, 0.0)], axis=0)
        self.blk2 = jnp.concatenate([blk, blk], axis=0)
        self.m_far = jnp.concatenate([jnp.where(self.blk2 == J, k2, 0.0) for J in range(GLA_NSUB - 1)], axis=1).astype(BF16)
        self.qcat = jnp.concatenate([qs * e for e in self.e_far], axis=1).astype(BF16)
        a_far = _dot(self.qcat, self.m_far, NT)
        self.e_band, self.rk, hi_terms, lo_terms = [], [], [], []
        for d in range(GLA_SUB):
            rk = pltpu.roll(kc, d, 0) if d else kc
            rb = pltpu.roll(b, d, 0) if d else b
            e = jnp.exp(jnp.where(rows >= d, b - rb, NEG))
            self.e_band.append(e)
            self.rk.append(rk)
            if exact:
                t = (qs * e).astype(BF16).astype(F32) * rk.astype(BF16).astype(F32)
                hi = t.astype(BF16)
                hi_terms.append(hi)
                lo_terms.append((t - hi.astype(F32)).astype(BF16))
            else:
                hi_terms.append((qs * rk * e).astype(BF16))
        band = _dot(jnp.concatenate(hi_terms, axis=1), sel, NN)
        if exact:
            band = band + _dot(jnp.concatenate(lo_terms, axis=1), sel, NN)
        a_band = pltpu.roll(band, LANE - (GLA_SUB - 1), 1, stride=1, stride_axis=0)
        dist = rows - lane % GLA_DK
        self.far_mask = dist >= GLA_SUB
        self.band_mask = (dist >= 0) & (dist < GLA_SUB)
        self.a = (a_band + jnp.where(self.far_mask, a_far, 0.0)).astype(BF16)
        self.lane_v = lax.broadcasted_iota(jnp.int32, (1, PAIR_V), 1)
        self.v2 = jnp.concatenate([jnp.where(self.lane_v < GLA_DV, vc, 0.0), jnp.where(self.lane_v >= GLA_DV, vc, 0.0)],
                                  axis=0).astype(BF16)


def _gla_fwd(proj, la, tb, ride=None):
    S = proj.shape[0]
    C = GLA_CHUNK
    tb = min(tb, S)
    nbc = tb // C
    npair = GLA_HEADS // 2
    scale = GLA_DK ** -0.5

    def body(q_ref, k_ref, v_ref, la_ref, sel_ref, o_ref, st_ref, state):
        @pl.when(pl.program_id(1) == 0)
        def _():
            state[...] = jnp.zeros_like(state)

        def chunk(ci):
            sl = pl.ds(ci * C, C)
            ch = _GlaChunk(q_ref[sl, :] * scale, k_ref[sl, :], v_ref[sl, :], la_ref[sl, :], sel_ref[...], exact=ci == 0)
            st = state[...]
            st_ref[0, ci] = st
            o_ref[sl, :] = _dot((ch.qs * ch.eb).astype(BF16), st.astype(BF16), NT) + _dot(ch.a, ch.v2, NN)
            upd = _dot(ch.vc.astype(BF16), (ch.kc * ch.kdec).astype(BF16), TN)
            state[...] = st * jnp.exp(ch.bl) + jnp.where(_state_mask(), upd, 0.0)

        for ci in range(nbc):
            chunk(ci)

    qspec = lambda off: pl.BlockSpec((tb, PAIR_QK), lambda p, i: (i, off // PAIR_QK + p))
    ride_arrays, ride_gather = ride if ride else ([], [])
    nr = len(ride_arrays)
    grid = (npair, S // tb)
    outs = pl.pallas_call(
        _riding(body, 5, 2, ride_gather, grid), grid=grid,
        in_specs=[qspec(O_GQ), qspec(O_GK), pl.BlockSpec((tb, PAIR_V), lambda p, i: (i, O_GV // PAIR_V + p)),
                  pl.BlockSpec((tb, PAIR_QK), lambda p, i: (i, p)),
                  pl.BlockSpec((GLA_SUB * PAIR_QK, LANE), lambda p, i: (0, 0))] + [HBM_SPEC] * nr,
        out_specs=[pl.BlockSpec((tb, PAIR_V), lambda p, i: (i, p)),
                   pl.BlockSpec((1, nbc, PAIR_V, PAIR_QK), lambda p, i: (p, i, 0, 0))] + [HBM_SPEC] * nr,
        out_shape=[jax.ShapeDtypeStruct((S, GLA_V), F32), jax.ShapeDtypeStruct((npair, S // C, PAIR_V, PAIR_QK), F32)]
        + _exchange_shapes(ride_arrays, ride_gather),
        scratch_shapes=[pltpu.VMEM((PAIR_V, PAIR_QK), F32)] + (_exchange_sems(nr) if nr else []),
        compiler_params=_cp("arbitrary", "arbitrary"), name="gla_fwd")(proj, proj, proj, la, _band_selector(), *ride_arrays)
    return outs[0], outs[1], outs[2:]


def _gla_bwd(proj, la, do, states, tb, ride=None):
    S = proj.shape[0]
    C = GLA_CHUNK
    tb = min(tb, S)
    nbc = tb // C
    nblk = S // tb
    npair = GLA_HEADS // 2
    scale = GLA_DK ** -0.5

    def body(q_ref, k_ref, v_ref, la_ref, do_ref, st_ref, sel_ref, selt_ref, dq_ref, dk_ref, dv_ref, dla_ref, dstate):
        @pl.when(pl.program_id(1) == 0)
        def _():
            dstate[...] = jnp.zeros_like(dstate)

        def chunk(ci):
            sl = pl.ds(ci * C, C)
            ch = _GlaChunk(q_ref[sl, :] * scale, k_ref[sl, :], v_ref[sl, :], la_ref[sl, :], sel_ref[...], exact=ci == 0)
            qs, kc, rows = ch.qs, ch.kc, ch.rows
            doc_b = do_ref[sl, :].astype(BF16)
            st = st_ref[0, ci]
            dst = dstate[...]
            dst_b = dst.astype(BF16)
            ebl = jnp.exp(ch.bl)
            dq = _dot(doc_b, st.astype(BF16), NN) * ch.eb
            dk = _dot(ch.vc.astype(BF16), dst_b, NN) * ch.kdec
            dv = _dot((kc * ch.kdec).astype(BF16), dst_b, NT)
            dbl = jnp.sum(dst * st, axis=0, keepdims=True) * ebl + jnp.sum(kc * dk, axis=0, keepdims=True)
            da = _dot(doc_b, ch.v2, NT)
            dv2 = _dot(ch.a, doc_b, TN)
            dv = dv + jnp.where(ch.lane_v < GLA_DV, dv2[:C], dv2[C:])
            da_far = jnp.where(ch.far_mask, da, 0.0).astype(BF16)
            dqcat = _dot(da_far, ch.m_far, NN)
            dm = _dot(da_far, ch.qcat, TN)
            dk2 = jnp.zeros((2 * C, PAIR_QK), F32)
            for J in range(GLA_NSUB - 1):
                dq = dq + dqcat[:, J * PAIR_QK:(J + 1) * PAIR_QK] * ch.e_far[J]
                dk2 = dk2 + jnp.where(ch.blk2 == J, dm[:, J * PAIR_QK:(J + 1) * PAIR_QK], 0.0)
            dk = dk + jnp.where(ch.lane < GLA_DK, dk2[:C], dk2[C:]) * ch.e_khat
            flip = _flip_matrix()
            da_band = _dot(flip, jnp.where(ch.band_mask, da, 0.0).astype(BF16), NN)
            dband = pltpu.roll(da_band, LANE - (C - GLA_SUB), 1, stride=1, stride_axis=0)
            dband = _dot(flip, dband.astype(BF16), NN)
            dterms = _dot(dband.astype(BF16), selt_ref[...], NN)
            for d in range(GLA_SUB):
                dt = dterms[:, d * PAIR_QK:(d + 1) * PAIR_QK]
                dq = dq + dt * (ch.rk[d] * ch.e_band[d])
                dkr = dt * (qs * ch.e_band[d])
                dk = dk + (pltpu.roll(dkr, C - d, 0) if d else dkr)
            db = qs * dq - kc * dk
            db = jnp.where(rows == C - 1, db + dbl, db)
            dq_ref[sl, :] = (dq * scale).astype(BF16)
            dk_ref[sl, :] = dk.astype(BF16)
            dv_ref[sl, :] = dv.astype(BF16)
            dla_ref[sl, :] = _dot(_tri(False), db, NN, HI)
            upd = _dot(doc_b, (qs * ch.eb).astype(BF16), TN)
            dstate[...] = dst * ebl + jnp.where(_state_mask(), upd, 0.0)

        for ci in reversed(range(nbc)):
            chunk(ci)

    rev = lambda i: nblk - 1 - i
    qspec = lambda off: pl.BlockSpec((tb, PAIR_QK), lambda p, i: (rev(i), off // PAIR_QK + p))
    pair_qk = pl.BlockSpec((tb, PAIR_QK), lambda p, i: (rev(i), p))
    pair_v = pl.BlockSpec((tb, PAIR_V), lambda p, i: (rev(i), p))
    sel = _band_selector()
    ride_arrays, ride_gather = ride if ride else ([], [])
    nr = len(ride_arrays)
    grid = (npair, nblk)
    outs = pl.pallas_call(
        _riding(body, 8, 4, ride_gather, grid), grid=grid,
        in_specs=[qspec(O_GQ), qspec(O_GK), pl.BlockSpec((tb, PAIR_V), lambda p, i: (rev(i), O_GV // PAIR_V + p)),
                  pair_qk, pair_v, pl.BlockSpec((1, nbc, PAIR_V, PAIR_QK), lambda p, i: (p, rev(i), 0, 0)),
                  pl.BlockSpec((GLA_SUB * PAIR_QK, LANE), lambda p, i: (0, 0)),
                  pl.BlockSpec((LANE, GLA_SUB * PAIR_QK), lambda p, i: (0, 0))] + [HBM_SPEC] * nr,
        out_specs=[pair_qk, pair_qk, pair_v, pair_qk] + [HBM_SPEC] * nr,
        out_shape=[jax.ShapeDtypeStruct((S, GLA_QK), BF16), jax.ShapeDtypeStruct((S, GLA_QK), BF16),
                   jax.ShapeDtypeStruct((S, GLA_V), BF16), jax.ShapeDtypeStruct((S, GLA_QK), F32)]
        + _exchange_shapes(ride_arrays, ride_gather),
        scratch_shapes=[pltpu.VMEM((PAIR_V, PAIR_QK), F32)] + (_exchange_sems(nr) if nr else []),
        compiler_params=_cp("arbitrary", "arbitrary"), name="gla_bwd")(proj, proj, proj, la, do, states, sel, sel.T, *ride_arrays)
    return outs[0], outs[1], outs[2], outs[3], outs[4:]


def _gla_out(o, proj, gng, ts):
    S = o.shape[0]

    def body(o_ref, gr_ref, g_ref, y_ref):
        for h in range(GLA_HEADS):
            cols = slice(h * GLA_DV, (h + 1) * GLA_DV)
            ov, grv = o_ref[:, cols], gr_ref[:, cols]
            r = lax.rsqrt(jnp.mean(ov * ov, axis=-1, keepdims=True) + EPS)
            y_ref[:, cols] = (ov * r * g_ref[...] * (grv * _sigmoid(grv))).astype(BF16)

    return pl.pallas_call(
        body, grid=(S // ts,),
        in_specs=[pl.BlockSpec((ts, GLA_V), lambda i: (i, 0)), pl.BlockSpec((ts, GLA_V), lambda i: (i, O_GR // GLA_V)),
                  pl.BlockSpec((1, GLA_DV), lambda i: (0, 0))],
        out_specs=pl.BlockSpec((ts, GLA_V), lambda i: (i, 0)), out_shape=jax.ShapeDtypeStruct((S, GLA_V), BF16),
        compiler_params=_cp("parallel"), name="gla_out_fwd")(o, proj, gng)


def _mm_mixed_bwd(dt1, w_out, o, proj, gng, y_att, tm):
    S = o.shape[0]
    W = ATTN_DIM

    def epilogue(dm, step, o_ref, gr_ref, g_ref, y_ref, do_ref, dgr_ref, gg_ref, dy_ref, de_ref):
        gsum = jnp.zeros((1, GLA_DV), F32)
        for h in range(GLA_HEADS):
            cols = slice(h * GLA_DV, (h + 1) * GLA_DV)
            ov, grv, dy = o_ref[:, cols], gr_ref[:, cols], dm[:, cols]
            r = lax.rsqrt(jnp.mean(ov * ov, axis=-1, keepdims=True) + EPS)
            oh = ov * r
            sg = _sigmoid(grv)
            don = dy * (grv * sg)
            dgr_ref[:, cols] = (dy * (oh * g_ref[...]) * (sg * (1.0 + grv * (1.0 - sg)))).astype(BF16)
            gsum = gsum + jnp.sum(don * oh, axis=0, keepdims=True)
            doh = don * g_ref[...]
            do_ref[:, cols] = r * (doh - oh * jnp.mean(doh * oh, axis=-1, keepdims=True))
        _accumulate(gg_ref, _rows8([gsum], GLA_DV), step)
        dya = dm[:, GLA_V:]
        dy_ref[...] = dya
        de_ref[...] = _seg_sum(dya * y_ref[...], _seg_matrix(W, ATTN_HD, 1.0))

    half = pl.BlockSpec((tm, GLA_V), lambda i: (i, 0))
    outs, _ = _mm_rows(
        dt1, w_out, NT, tm, [o, proj, gng, y_att],
        [half, pl.BlockSpec((tm, GLA_V), lambda i: (i, O_GR // GLA_V)), pl.BlockSpec((1, GLA_DV), lambda i: (0, 0)), half],
        [jax.ShapeDtypeStruct((S, GLA_V), F32), jax.ShapeDtypeStruct((S, GLA_V), BF16), jax.ShapeDtypeStruct((8, GLA_DV), F32),
         jax.ShapeDtypeStruct((S, W), F32), jax.ShapeDtypeStruct((S, W), F32)],
        [half, half, pl.BlockSpec((8, GLA_DV), lambda i: (0, 0)), half, half], epilogue, "mm_dmixed")
    return outs


def _seg_matrix(width, seg, value):
    r = lax.broadcasted_iota(jnp.int32, (width, width), 0) // seg
    c = lax.broadcasted_iota(jnp.int32, (width, width), 1) // seg
    return jnp.where(r == c, value, 0.0).astype(BF16)


def _seg_sum(x, seg_matrix):
    hi = x.astype(BF16)
    lo = (x - hi.astype(F32)).astype(BF16)
    return _dot(hi, seg_matrix, NN) + _dot(lo, seg_matrix, NN)


ATTN_GROUP = 4


ATTN_TILE = max(DILATIONS) * ATTN_BLOCK


def _attn_rows(d, g, r, base=0):
    start = base + (g * d * ATTN_BLOCK if g >= 0 else ATTN_TILE - d * ATTN_BLOCK) + r
    return pl.ds(start, ATTN_BLOCK) if d == 1 else pl.ds(start, ATTN_BLOCK, stride=d)


def _for_blocks(d, G, fn):
    for g in range(G):
        if d <= ATTN_GROUP:
            for r in range(d):
                fn(g, r)
        else:
            def step(r, carry, g=g):
                fn(g, r)
                return carry
            lax.fori_loop(0, d, step, 0, unroll=ATTN_GROUP)


def _attn_specs(S):
    nb = S // ATTN_TILE

    def specs(off=0):
        return [pl.BlockSpec((ATTN_TILE, LANE), lambda hp, n: (n, off + hp)),
                pl.BlockSpec((ATTN_TILE, LANE), lambda hp, n: (jnp.maximum(n - 1, 0), off + hp)),
                pl.BlockSpec((ATTN_TILE, LANE), lambda hp, n: (jnp.minimum(n + 1, nb - 1), off + hp))]

    return nb, specs


def _slope(head):
    one = jnp.ones((1, 1), jnp.int32)
    return 1.0 / jnp.left_shift(one, one * (head + 1)).astype(F32)


def _attn_bias(d, hp, first_tile):
    B = ATTN_BLOCK
    iq = lax.broadcasted_iota(jnp.int32, (B, 2 * B), 0)
    ik = lax.broadcasted_iota(jnp.int32, (B, 2 * B), 1)
    rel = iq + B - ik
    window = (rel >= 0) & (rel <= B)
    relf = (d * rel).astype(F32)
    full = [jnp.where(window, -_slope(hp * 2 + h) * relf, NEG) for h in range(2)]
    edge = [jnp.where((ik >= B) | jnp.logical_not(first_tile), b, NEG) for b in full]
    return full, edge


def _attn_bias_t(d, hp, has_next):
    B = ATTN_BLOCK
    ik = lax.broadcasted_iota(jnp.int32, (B, B), 0)
    iq = lax.broadcasted_iota(jnp.int32, (B, B), 1)
    tiles = []
    for nxt in range(2):
        rel = iq - ik + nxt * B
        window = (rel >= 0) & (rel <= B)
        relf = (d * rel).astype(F32)
        tiles.append([jnp.where(window, -_slope(hp * 2 + h) * relf, NEG) for h in range(2)])
    tiles.append([jnp.where(has_next, b, NEG) for b in tiles[1]])
    return tiles


def _attn_fwd(qn, kn, proj):
    S, W = qn.shape
    T = ATTN_TILE
    nb, specs = _attn_specs(S)

    def body(q_ref, kp_ref, kc_ref, vp_ref, vc_ref, y_ref, l_ref, o_scr, l_scr):
        hp, n = pl.program_id(0), pl.program_id(1)
        lo = lax.broadcasted_iota(jnp.int32, (1, LANE), 1) < ATTN_HD
        for b, d in enumerate(DILATIONS):
            full, edge = _attn_bias(d, hp, n == 0)

            def sub(g, r, b=b, d=d, full=full, edge=edge):
                rows, before = _attn_rows(d, g, r), _attn_rows(d, g - 1, r)
                kb_ref, vb_ref = (kp_ref, vp_ref) if g == 0 else (kc_ref, vc_ref)
                bias = edge if g == 0 else full
                qv = q_ref[rows, :].astype(BF16)
                kv = jnp.concatenate([kb_ref[before, :], kc_ref[rows, :]], axis=0).astype(BF16)
                vv = jnp.concatenate([vb_ref[before, :], vc_ref[rows, :]], axis=0).astype(BF16)
                outs, lses = [], []
                for h in range(2):
                    qm = jnp.where(lo == (h == 0), qv, jnp.zeros_like(qv))
                    s = _dot(qm, kv, NT) + bias[h]
                    m = jnp.max(s, axis=-1, keepdims=True)
                    p = jnp.exp(s - m)
                    den = jnp.sum(p, axis=-1, keepdims=True)
                    outs.append(_dot(p.astype(BF16), vv, NN) / den)
                    lses.append(m + jnp.log(den))
                kept = _attn_rows(d, g, r, base=b * T)
                o_scr[kept, :] = jnp.where(lo, outs[0], outs[1])
                l_scr[kept, :] = jnp.where(lo, lses[0], lses[1])

            _for_blocks(d, T // (d * ATTN_BLOCK), sub)
        l1, l2, l3 = [l_scr[pl.ds(b * T, T), :] for b in range(len(DILATIONS))]
        o1, o2, o3 = [o_scr[pl.ds(b * T, T), :] for b in range(len(DILATIONS))]
        m = jnp.maximum(jnp.maximum(l1, l2), l3)
        e1, e2, e3 = jnp.exp(l1 - m), jnp.exp(l2 - m), jnp.exp(l3 - m)
        tot = e1 + e2 + e3
        y_ref[...] = (e1 * o1 + e2 * o2 + e3 * o3) / tot
        l_ref[...] = m + jnp.log(tot)

    cur, prev, _ = specs()
    vcur, vprev, _ = specs(O_AV // LANE)
    return pl.pallas_call(
        body, grid=(W // LANE, nb), in_specs=[cur, prev, cur, vprev, vcur], out_specs=[cur, cur],
        out_shape=[jax.ShapeDtypeStruct((S, W), F32)] * 2,
        scratch_shapes=[pltpu.VMEM((len(DILATIONS) * T, LANE), F32)] * 2,
        compiler_params=_cp("parallel", "arbitrary"), name="attn_fwd")(qn, kn, kn, proj, proj)


def _attn_mix(y_gla, y_att, ts):
    S, W = y_att.shape

    def body(yg, ya, mixed_ref, mixed_t_ref):
        y = ya[...]
        mixed_ref[:, :W] = yg[...]
        mixed_ref[:, W:] = y.astype(BF16)
        mixed_t_ref[:W, :] = yg[...].astype(F32).T.astype(BF16)
        mixed_t_ref[W:, :] = y.T.astype(BF16)

    spec = pl.BlockSpec((ts, W), lambda i: (i, 0))
    return pl.pallas_call(
        body, grid=(S // ts,), in_specs=[spec] * 2,
        out_specs=[pl.BlockSpec((ts, 2 * W), lambda i: (i, 0)), _col_spec(2 * W, ts)],
        out_shape=[jax.ShapeDtypeStruct((S, 2 * W), BF16), jax.ShapeDtypeStruct((2 * W, S), BF16)],
        compiler_params=_cp("parallel"), name="attn_mix")(y_gla, y_att)


def _attn_dq(qn, kn, proj, dy, lse, delta):
    S, W = qn.shape
    nb, specs = _attn_specs(S)

    def body(q_ref, kp_ref, kc_ref, vp_ref, vc_ref, dy_ref, l_ref, de_ref, dq_ref):
        hp, n = pl.program_id(0), pl.program_id(1)
        lo = lax.broadcasted_iota(jnp.int32, (1, LANE), 1) < ATTN_HD
        for b, d in enumerate(DILATIONS):
            _attn_dq_branch(b, d, _attn_bias(d, hp, n == 0), lo, q_ref, kp_ref, kc_ref, vp_ref, vc_ref, dy_ref, l_ref, de_ref, dq_ref)

    cur, prev, _ = specs()
    vcur, vprev, _ = specs(O_AV // LANE)
    return pl.pallas_call(
        body, grid=(W // LANE, nb), in_specs=[cur, prev, cur, vprev, vcur, cur, cur, cur], out_specs=cur,
        out_shape=jax.ShapeDtypeStruct((S, W), F32),
        compiler_params=_cp("parallel", "arbitrary"), name="attn_dq")(qn, kn, kn, proj, proj, dy, lse, delta)


def _attn_dq_branch(b, d, biases, lo, q_ref, kp_ref, kc_ref, vp_ref, vc_ref, dy_ref, l_ref, de_ref, dq_ref):
    full, edge = biases

    def sub(g, r):
        rows, before = _attn_rows(d, g, r), _attn_rows(d, g - 1, r)
        kb_ref, vb_ref = (kp_ref, vp_ref) if g == 0 else (kc_ref, vc_ref)
        bias = edge if g == 0 else full
        qv, dyv = q_ref[rows, :].astype(BF16), dy_ref[rows, :]
        lv, dev = l_ref[rows, :], de_ref[rows, :]
        kv = jnp.concatenate([kb_ref[before, :], kc_ref[rows, :]], axis=0).astype(BF16)
        vv = jnp.concatenate([vb_ref[before, :], vc_ref[rows, :]], axis=0).astype(BF16)
        outs = []
        for h in range(2):
            sel = lo == (h == 0)
            qm = jnp.where(sel, qv, jnp.zeros_like(qv))
            dym = jnp.where(sel, dyv, 0.0).astype(BF16)
            lse_h = lv[:, h * ATTN_HD:h * ATTN_HD + 1]
            del_h = dev[:, h * ATTN_HD:h * ATTN_HD + 1]
            p = jnp.exp(_dot(qm, kv, NT) + bias[h] - lse_h)
            ds = p * (_dot(dym, vv, NT) - del_h)
            outs.append(_dot(ds.astype(BF16), kv, NN) * (ATTN_HD ** -0.5))
        dq = jnp.where(lo, outs[0], outs[1])
        dq_ref[rows, :] = dq if b == 0 else dq_ref[rows, :] + dq

    _for_blocks(d, ATTN_TILE // (d * ATTN_BLOCK), sub)


def _attn_dkv(qn, kn, proj, dy, lse, delta):
    S, W = qn.shape
    nb, specs = _attn_specs(S)

    def body(k_ref, v_ref, qc_ref, qn_ref, dyc_ref, dyn_ref, lc_ref, ln_ref, dec_ref, den_ref, dk_ref, dv_ref):
        hp, n = pl.program_id(0), pl.program_id(1)
        lo = lax.broadcasted_iota(jnp.int32, (1, LANE), 1) < ATTN_HD
        cur_refs, next_refs = (qc_ref, dyc_ref, lc_ref, dec_ref), (qn_ref, dyn_ref, ln_ref, den_ref)
        for b, d in enumerate(DILATIONS):
            _attn_dkv_branch(b, d, _attn_bias_t(d, hp, n + 1 < nb), lo, k_ref, v_ref, cur_refs, next_refs, dk_ref, dv_ref)

    cur, _, nxt = specs()
    vcur, _, _ = specs(O_AV // LANE)
    return pl.pallas_call(
        body, grid=(W // LANE, nb), in_specs=[cur, vcur, cur, nxt, cur, nxt, cur, nxt, cur, nxt], out_specs=[cur, cur],
        out_shape=[jax.ShapeDtypeStruct((S, W), F32)] * 2,
        compiler_params=_cp("parallel", "arbitrary"), name="attn_dkv")(
            kn, proj, qn, qn, dy, dy, lse, lse, delta, delta)


def _attn_dkv_branch(b, d, biases, lo, k_ref, v_ref, cur_refs, next_refs, dk_ref, dv_ref):
    B = ATTN_BLOCK
    own, inner, outer = biases
    G = ATTN_TILE // (d * B)

    def sub(g, r):
        rows = _attn_rows(d, g, r)
        kv, vv = k_ref[rows, :].astype(BF16), v_ref[rows, :].astype(BF16)
        dk = jnp.zeros((B, LANE), F32)
        dv = jnp.zeros((B, LANE), F32)
        inside = g + 1 < G
        after = _attn_rows(d, g + 1 if inside else 0, r)
        for bias, qrows, (q_ref, dy_ref, l_ref, de_ref) in (
                (own, rows, cur_refs), (inner if inside else outer, after, cur_refs if inside else next_refs)):
            qv, dyv = q_ref[qrows, :].astype(BF16), dy_ref[qrows, :]
            lt, det = l_ref[qrows, :].T, de_ref[qrows, :].T
            for h in range(2):
                sel = lo == (h == 0)
                qm = jnp.where(sel, qv, jnp.zeros_like(qv))
                dym = jnp.where(sel, dyv, 0.0).astype(BF16)
                lse_h = lt[h * ATTN_HD:h * ATTN_HD + 1, :]
                del_h = det[h * ATTN_HD:h * ATTN_HD + 1, :]
                pt = jnp.exp(_dot(kv, qm, NT) + bias[h] - lse_h)
                dv = dv + _dot(pt.astype(BF16), dym, NN)
                dst = pt * (_dot(vv, dym, NT) - del_h)
                dk = dk + _dot(dst.astype(BF16), qm, NN)
        dk_ref[rows, :] = dk if b == 0 else dk_ref[rows, :] + dk
        dv_ref[rows, :] = dv if b == 0 else dv_ref[rows, :] + dv

    _for_blocks(d, G, sub)


def _attn_post(dq, dk, dv, proj, qg, kg, ts):
    S = proj.shape[0]
    W = ATTN_DIM

    def body(dq_ref, dk_ref, dv_ref, aq_ref, ak_ref, qg_ref, kg_ref, daq_ref, dak_ref, dav_ref, gg_ref):
        i = pl.program_id(0)
        seg = _seg_matrix(W, ATTN_HD, 1.0 / ATTN_HD)
        gsums = []
        for d_ref, x_ref, g_ref, o_ref in ((dq_ref, aq_ref, qg_ref, daq_ref), (dk_ref, ak_ref, kg_ref, dak_ref)):
            dy = d_ref[...]
            xv = x_ref[...]
            r = lax.rsqrt(_seg_sum(xv * xv, seg) + EPS)
            xh = xv * r
            dxh = dy * g_ref[...]
            o_ref[...] = (r * (dxh - xh * _seg_sum(dxh * xh, seg))).astype(BF16)
            gsums.append(jnp.sum(dy * xh, axis=0, keepdims=True))
        dav_ref[...] = dv_ref[...].astype(BF16)
        _accumulate(gg_ref, _rows8(gsums, W), i)

    row = pl.BlockSpec((ts, W), lambda i: (i, 0))
    blk = lambda off: pl.BlockSpec((ts, W), lambda i: (i, off // W))
    vec = pl.BlockSpec((1, W), lambda i: (0, 0))
    return pl.pallas_call(
        body, grid=(S // ts,), in_specs=[row] * 3 + [blk(O_AQ), blk(O_AK), vec, vec],
        out_specs=[row, row, row, pl.BlockSpec((8, W), lambda i: (0, 0))],
        out_shape=[jax.ShapeDtypeStruct((S, W), BF16)] * 3 + [jax.ShapeDtypeStruct((8, W), F32)],
        compiler_params=_cp("arbitrary"), name="attn_post")(dq, dk, dv, proj, proj, qg, kg)


def _shift_down(cur, halo, n):
    return pltpu.roll(jnp.concatenate([halo, cur], axis=0), n, 0)[8:]


def _shift_up(cur, halo, n):
    ts = cur.shape[0]
    return pltpu.roll(jnp.concatenate([cur, halo], axis=0), ts + 8 - n, 0)[:ts]


def _conv(cur, halo, w, b):
    return b + w[0:1, :] * _shift_down(cur, halo, 2) + w[1:2, :] * _shift_down(cur, halo, 1) + w[2:3, :] * cur


def _mm_up_swiglu(h2, w_up, conv_w8, conv_b, tm, tc, ride=None):
    S, D = h2.shape
    F = w_up.shape[1] // 2
    nc = F // tc
    grid = (S // tm, nc)
    ride_arrays, ride_gather = ride if ride else ([], [])
    nr = len(ride_arrays)

    def body(h_ref, bg_ref, bv_ref, wg_ref, wv_ref, cg_ref, cv_ref, u0_ref, a_ref, at_ref, halo):
        i, j = pl.program_id(0), pl.program_id(1)
        hv = h_ref[...]
        acts = []
        for h, (b_ref, w_ref, c_ref) in enumerate(((bg_ref, wg_ref, cg_ref), (bv_ref, wv_ref, cv_ref))):
            u = _dot(hv, b_ref[...], NN)
            u0_ref[h] = u
            acts.append(_conv(u, jnp.where(i == 0, 0.0, halo[j, h]), w_ref[...], c_ref[...]))
            halo[j, h] = u[tm - 8:, :]
        g, v = acts
        a = g * _sigmoid(g) * v
        a_ref[...] = a.astype(BF16)
        at_ref[...] = a.T.astype(BF16)

    wcol = lambda rows, off: pl.BlockSpec((rows, tc), lambda i, j: (0, j + off))
    outs = pl.pallas_call(
        _riding(body, 7, 3, ride_gather, grid), grid=grid,
        in_specs=[pl.BlockSpec((tm, D), lambda i, j: (i, 0)), wcol(D, 0), wcol(D, nc), wcol(8, 0), wcol(8, nc), wcol(1, 0), wcol(1, nc)]
        + [HBM_SPEC] * nr,
        out_specs=[pl.BlockSpec((2, tm, tc), lambda i, j: (0, i, j)), pl.BlockSpec((tm, tc), lambda i, j: (i, j)),
                   pl.BlockSpec((tc, tm), lambda i, j: (j, i))] + [HBM_SPEC] * nr,
        out_shape=[jax.ShapeDtypeStruct((2, S, F), F32), jax.ShapeDtypeStruct((S, F), BF16), jax.ShapeDtypeStruct((F, S), BF16)]
        + _exchange_shapes(ride_arrays, ride_gather),
        scratch_shapes=[pltpu.VMEM((nc, 2, 8, tc), F32)] + (_exchange_sems(nr) if nr else []),
        compiler_params=_cp("arbitrary", "arbitrary"), name="mm_up")(
            h2, w_up, w_up, conv_w8, conv_w8, conv_b, conv_b, *ride_arrays)
    return outs[0], outs[1], outs[2], outs[3:]


def _mm_da_du0(dt2, w_down, u0, conv_w8, conv_b, tm, tc, ride=None):
    _, S, F = u0.shape
    D = dt2.shape[1]
    hb = tm // 8
    nrow = S // tm
    grid = (nrow,)
    ride_arrays, ride_gather = ride if ride else ([], [])
    nr = len(ride_arrays)

    def body(dt_ref, wd_ref, ug_ref, ugh_ref, uv_ref, uvh_ref, w_ref, b_ref, o_ref, sg_ref, sv_ref, following):
        i = pl.program_id(0)
        at_start, at_end = i == nrow - 1, i == 0
        dt = dt_ref[...]
        for c in range(F // tc):
            sums = []
            halves = []
            for h, (u_ref, h_ref) in enumerate(((ug_ref, ugh_ref), (uv_ref, uvh_ref))):
                cols = slice(h * F + c * tc, h * F + (c + 1) * tc)
                u, halo, w = u_ref[:, c * tc:(c + 1) * tc], jnp.where(at_start, 0.0, h_ref[:, c * tc:(c + 1) * tc]), w_ref[:, cols]
                s2, s1 = _shift_down(u, halo, 2), _shift_down(u, halo, 1)
                halves.append((b_ref[:, cols] + w[0:1, :] * s2 + w[1:2, :] * s1 + w[2:3, :] * u, s2, s1, u, w, cols))
            g, v = halves[0][0], halves[1][0]
            dav = _dot(dt, wd_ref[c * tc:(c + 1) * tc, :], NT)
            sig = _sigmoid(g)
            dus = (dav * v * (sig * (1.0 + g * (1.0 - sig))), dav * (g * sig))
            for h, du in enumerate(dus):
                _, s2, s1, u, w, cols = halves[h]
                after = jnp.where(at_end, 0.0, following[h, :, c * tc:(c + 1) * tc])
                o_ref[:, cols] = (w[2:3, :] * du + w[1:2, :] * _shift_up(du, after, 1) + w[0:1, :] * _shift_up(du, after, 2)).astype(BF16)
                following[h, :, c * tc:(c + 1) * tc] = du[0:8, :]
                sums.append(_rows8([jnp.sum(du * s2, axis=0, keepdims=True), jnp.sum(du * s1, axis=0, keepdims=True),
                                    jnp.sum(du * u, axis=0, keepdims=True), jnp.sum(du, axis=0, keepdims=True)], tc))
            for sums_ref, part in zip((sg_ref, sv_ref), sums):
                @pl.when(i == 0)
                def _(sums_ref=sums_ref, part=part, c=c):
                    sums_ref[:, c * tc:(c + 1) * tc] = part

                @pl.when(i > 0)
                def _(sums_ref=sums_ref, part=part, c=c):
                    sums_ref[:, c * tc:(c + 1) * tc] += part

    rev = lambda i: nrow - 1 - i
    main = lambda h: pl.BlockSpec((None, tm, F), lambda i: (h, rev(i), 0))
    halo = lambda h: pl.BlockSpec((None, 8, F), lambda i: (h, jnp.maximum(rev(i) * hb - 1, 0), 0))
    whole = lambda a: pl.BlockSpec(a.shape, lambda i: (0,) * a.ndim, pipeline_mode=pl.Buffered(1))
    sums_spec = pl.BlockSpec((8, F), lambda i: (0, 0))
    outs = pl.pallas_call(
        _riding(body, 8, 3, ride_gather, grid), grid=grid,
        in_specs=[pl.BlockSpec((tm, D), lambda i: (rev(i), 0)), whole(w_down), main(0), halo(0), main(1), halo(1),
                  whole(conv_w8), whole(conv_b)] + [HBM_SPEC] * nr,
        out_specs=[pl.BlockSpec((tm, 2 * F), lambda i: (rev(i), 0)), sums_spec, sums_spec] + [HBM_SPEC] * nr,
        out_shape=[jax.ShapeDtypeStruct((S, 2 * F), BF16), jax.ShapeDtypeStruct((8, F), F32), jax.ShapeDtypeStruct((8, F), F32)]
        + _exchange_shapes(ride_arrays, ride_gather),
        scratch_shapes=[pltpu.VMEM((2, 8, F), F32)] + (_exchange_sems(nr) if nr else []),
        compiler_params=_cp("arbitrary"), name="mm_da")(
            dt2, w_down, u0, u0, u0, u0, conv_w8, conv_b, *ride_arrays)
    return outs[0], outs[1], outs[2], outs[3:]


def _adamw(w, g, m, v, name, slots=False):
    shape = w.shape
    view = (math.prod(shape[:-1]), shape[-1])
    R, C = view
    limit = SUM_BLOCK_ELEMS // 2 if slots else SUM_BLOCK_ELEMS
    fits = [t for t in range(16, R + 1, 16) if R % t == 0 and t * C <= limit]
    tr = max(fits) if fits else R

    def body(w_ref, g_ref, m_ref, v_ref, *outs):
        if slots:
            gv = g_ref[0].astype(F32)
            for s in range(1, N_DEV):
                gv = gv + g_ref[s].astype(F32)
            outs[0][...] = gv
        else:
            gv = g_ref[...]
        d_ref, nm_ref, nv_ref = outs[-3:]
        nm = ADAM_B1 * m_ref[...] + (1.0 - ADAM_B1) * gv
        nv = ADAM_B2 * v_ref[...] + (1.0 - ADAM_B2) * (gv * gv)
        m_hat = nm / (1.0 - ADAM_B1 ** ADAM_STEP)
        v_hat = nv / (1.0 - ADAM_B2 ** ADAM_STEP)
        d_ref[...] = -ADAM_LR * (m_hat / (jnp.sqrt(v_hat) + ADAM_EPS) + ADAM_WD * w_ref[...])
        nm_ref[...] = nm
        nv_ref[...] = nv

    spec = pl.BlockSpec((tr, C), lambda i: (i, 0))
    g_spec = pl.BlockSpec((N_DEV, tr, C), lambda i: (0, i, 0)) if slots else spec
    n_out = 4 if slots else 3
    outs = pl.pallas_call(
        body, grid=(R // tr,), in_specs=[spec, g_spec, spec, spec], out_specs=[spec] * n_out,
        out_shape=[jax.ShapeDtypeStruct(view, F32)] * n_out, compiler_params=_cp("parallel"), name=name)(
            w.reshape(view), g if slots else g.reshape(view), m.reshape(view), v.reshape(view))
    outs = [o.reshape(shape) for o in outs]
    return outs if slots else [g.reshape(shape)] + outs


def _pad_rows8(a):
    return jnp.concatenate([a, jnp.zeros((8 - a.shape[0], a.shape[1]), a.dtype)], axis=0)


def _local_step(x, target, mod, n1g, w_in_s, conv_w_s, wg_s, bg, gng, qng, kng, w_out_s, n2g, w_up_s, conv_b, w_down_s):
    S, D = x.shape
    F = w_down_s.shape[0] * N_DEV
    cw_c, wg_c = conv_w_s.shape[1], wg_s.shape[1]
    ts = min(512, S)
    sh1, sc1, g1, sh2, sc2, g2 = [mod[i:i + 1] for i in range(6)]
    qg_t, kg_t = jnp.tile(qng, (1, ATTN_HEADS)), jnp.tile(kng, (1, ATTN_HEADS))

    small = jnp.concatenate([conv_w_s.reshape(1, -1), wg_s.reshape(1, -1)], axis=1)
    n_small = small.shape[1]
    small = jnp.pad(small, ((0, 0), (0, -n_small % LANE)))
    h1, h1_t, (g_in, g_small) = _rms_mod(x, n1g, sc1, sh1, ts, "rms_mod1", ride=([w_in_s, small], [VIA_SIBLING, True]))
    w_in_full = _cols_from_blocks(g_in)
    w_in_p = jnp.concatenate([w_in_full[:, :GLR_SRC], w_in_full[:, GLR_SRC + GLA_RANK:],
                              w_in_full[:, GLR_SRC:GLR_SRC + GLA_RANK], jnp.zeros((D, PROJ_W - O_GLR - GLA_RANK), BF16)], axis=1)
    g_small = g_small.reshape(N_DEV, -1)
    conv_w8 = _pad_rows8(jnp.stack([g_small[:, t * cw_c:(t + 1) * cw_c].reshape(-1) for t in range(3)]))
    wg_full = _cols_from_blocks(g_small[:, 3 * cw_c:n_small].reshape(N_DEV, GLA_RANK, wg_c))
    wg_p = jnp.concatenate([wg_full, jnp.zeros((LANE - GLA_RANK, wg_full.shape[1]), F32)], axis=0)
    (proj, la, qn, kn), (g_out,) = _mm_in(h1, w_in_p, wg_p, bg, qg_t, kg_t, ts, ride=([w_out_s], [True]))
    w_out = g_out.reshape(-1, D)
    o_gla, states, (g_up,) = _gla_fwd(proj, la, 512, ride=([w_up_s], [True]))
    w_up = _cols_from_blocks(g_up)
    y_gla = _gla_out(o_gla, proj, gng, ts)
    y_att, lse = _attn_fwd(qn, kn, proj)
    mixed, mixed_t = _attn_mix(y_gla, y_att, ts)
    t1, x2, h2, h2_t = _mm_resid_rms_mod(mixed, w_out, x, g1, n2g, sc2, sh2, ts, "mm_out")
    tc = 1408 if F % 1408 == 0 else F
    u0, a, a_t, (g_down,) = _mm_up_swiglu(h2, w_up, conv_w8, conv_b, ts, tc, ride=([w_down_s], [True]))
    w_down = g_down.reshape(F, D)
    dx3, dt2, sums3 = _mm_loss_resid(a, w_down, x2, g2, target, ts, "mm_down")
    loss_row, dg2 = sums3[0:1], sums3[1:2]

    g_w_down = _mm(a_t, dt2, NN, 1408, 1024, 2048, F32, "mm_gw_down")
    du0, sums_g, sums_v, (r_down,) = _mm_da_du0(dt2, w_down, u0, conv_w8, conv_b, min(256, S), tc,
                                                ride=([g_w_down.reshape(N_DEV, -1, D)], [False]))
    g_conv_w = jnp.concatenate([sums_g[0:3], sums_v[0:3]], axis=1)
    g_conv_b = jnp.concatenate([sums_g[3:4], sums_v[3:4]], axis=1)
    g_w_up = _mm(h2_t, du0, NN, 512, 2816, 2048, F32, "mm_gw_up")
    (dx2, sums2, dt1), _ = _mm_rms_mod_bwd(du0, w_up, x2, dx3, n2g, sc2, ts, "mm_dh2", t_prev=t1, g_prev=g1)
    dsh2, dsc2, g_n2g, dg1 = sums2[0:1], sums2[1:2], sums2[2:3], sums2[3:4]
    g_w_out = _mm(mixed_t, dt1, NN, 1024, 1024, 2048, F32, "mm_gw_out")
    do_gla, dgr, gng_sums, dy_att, delta = _mm_mixed_bwd(dt1, w_out, o_gla, proj, gng, y_att, ts)
    dgq, dgk, dgv, dla, (r_up, r_out) = _gla_bwd(
        proj, la, do_gla, states, 512, ride=([_col_blocks(g_w_up), g_w_out.reshape(N_DEV, -1, D)], [False, False]))
    dglr, g_wg_p, gb_sums = _gate_bwd(dla, la, proj, wg_p, ts)
    dqn = _attn_dq(qn, kn, proj, dy_att, lse, delta)
    dkn, dvn = _attn_dkv(qn, kn, proj, dy_att, lse, delta)
    daq, dak, dav, qk_sums = _attn_post(dqn, dkn, dvn, proj, qg_t, kg_t, ts)
    dproj = jnp.concatenate([dgq, dgk, dgv, dgr, daq, dak, dav, dglr, jnp.zeros((S, PROJ_W - O_GLR - LANE), BF16)], axis=1)
    g_w_in_p = _mm(h1_t, dproj, NN, 512, PROJ_W, 1024, F32, "mm_gw_in")
    g_w_in = jnp.concatenate([g_w_in_p[:, :GLR_SRC], g_w_in_p[:, O_GLR:O_GLR + GLA_RANK], g_w_in_p[:, GLR_SRC:O_GLR]], axis=1)
    (dx, sums1), (r_in,) = _mm_rms_mod_bwd(dproj, w_in_p, x, dx2, n1g, sc1, ts, "mm_dh1",
                                           ride=([_col_blocks(g_w_in).astype(BF16)], [False]))
    dsh1, dsc1, g_n1g = sums1[0:1], sums1[1:2], sums1[2:3]

    dmod = jnp.concatenate([dsh1, dsc1, dg1, dsh2, dsc2, dg2], axis=1)
    grads = dict(n1g=g_n1g, w_in=r_in, wg=g_wg_p[:GLA_RANK], bg=gb_sums[0:1], gng=gng_sums[0:1],
                 qng_lanes=qk_sums[0:1], kng_lanes=qk_sums[1:2], w_out=r_out, n2g=g_n2g, w_up=r_up,
                 conv_w=g_conv_w, conv_b=g_conv_b, w_down=r_down)
    return loss_row, dx, dmod, grads


def _col_blocks(a):
    R, W = a.shape
    return a.reshape(R, N_DEV, W // N_DEV).transpose(1, 0, 2)


def _cols_from_blocks(a):
    n, R, C = a.shape
    return a.transpose(1, 0, 2).reshape(R, n * C)


def kernel(x, c, w_ada, b_ada, norm1_g, w_in, gla_w_gate, gla_b_gate, gla_norm_g, q_norm_g, k_norm_g, w_out, norm2_g, w_up, conv_w, conv_b, w_down, loss_target, m_w_ada, m_b_ada, m_norm1_g, m_w_in, m_gla_w_gate, m_gla_b_gate, m_gla_norm_g, m_q_norm_g, m_k_norm_g, m_w_out, m_norm2_g, m_w_up, m_conv_w, m_conv_b, m_w_down, v_w_ada, v_b_ada, v_norm1_g, v_w_in, v_gla_w_gate, v_gla_b_gate, v_gla_norm_g, v_q_norm_g, v_k_norm_g, v_w_out, v_norm2_g, v_w_up, v_conv_w, v_conv_b, v_w_down):
    axes = ("x", "y", "c")
    me = 4 * lax.axis_index("x") + 2 * lax.axis_index("y") + lax.axis_index("c")
    S, D = x.shape[1], x.shape[2]
    x2d, tgt2d = x[0], loss_target[0]
    w_in_s, w_out_s, w_up_s, w_down_s, w_ada_s = w_in[0], w_out[0], w_up[0], w_down[0], w_ada[0]
    conv_w_s, wg_s = conv_w[0], gla_w_gate[0]
    in_c, up_c, ada_c, wg_c, cw_c = w_in_s.shape[1], w_up_s.shape[1], w_ada_s.shape[1], wg_s.shape[1], conv_w_s.shape[1]
    F = w_down_s.shape[0] * N_DEV

    g_c, = _exchange([c], [True], "gather_c")
    c_all = g_c.reshape(N_DEV, D)

    b_shard = lax.dynamic_slice(b_ada, (0, me * ada_c), (1, ada_c))
    mod_part = _ada_fwd(c_all, w_ada_s, b_shard)
    mod_recv, = _exchange([mod_part.reshape(N_DEV, 1, ada_c)], [False], "exchange_mod")
    mod = mod_recv.reshape(6, D)

    loss_row, dx, dmod, gr = _local_step(
        x2d, tgt2d, mod, norm1_g, w_in_s.astype(BF16), conv_w_s, wg_s, gla_b_gate, gla_norm_g, q_norm_g, k_norm_g,
        w_out_s.astype(BF16), norm2_g, w_up_s.astype(BF16), conv_b, w_down_s.astype(BF16))
    loss = lax.psum(0.5 / D * jnp.sum(loss_row), axes)

    parts = [dmod, gr["n1g"], gr["bg"], gr["gng"], gr["qng_lanes"], gr["kng_lanes"], gr["n2g"], gr["conv_b"],
             gr["wg"].reshape(1, -1), gr["conv_w"].reshape(1, -1)]
    sizes = [p.shape[1] for p in parts]
    packed = jnp.concatenate(parts, axis=1)
    packed = jnp.pad(packed, ((0, 0), (0, -packed.shape[1] % (8 * LANE))))
    gathered, = _exchange([packed.reshape(8, -1)], [True], "gather_small_grads")
    gathered = gathered.reshape(N_DEV, -1)
    total = _sum_slots(gathered.reshape(N_DEV, 8, -1), "sum_small_grads").reshape(1, -1)
    offs = [0]
    for s_ in sizes:
        offs.append(offs[-1] + s_)
    t_dmod, t_n1g, t_bg, t_gng, t_qng, t_kng, t_n2g, t_conv_b, t_wg, t_conv_w = [
        total[:, offs[i]:offs[i + 1]] for i in range(len(sizes))]
    g_b_ada = t_dmod
    g_qng = t_qng.reshape(ATTN_HEADS, ATTN_HD).sum(axis=0, keepdims=True)
    g_kng = t_kng.reshape(ATTN_HEADS, ATTN_HD).sum(axis=0, keepdims=True)
    g_wg = lax.dynamic_slice(t_wg.reshape(GLA_RANK, -1), (0, me * wg_c), (GLA_RANK, wg_c))
    g_conv_w = lax.dynamic_slice(t_conv_w.reshape(3, -1), (0, me * cw_c), (3, cw_c))
    dmod_shard = lax.dynamic_slice(gathered[:, :6 * D], (0, me * ada_c), (N_DEV, ada_c))
    g_w_ada = _ada_bwd(c_all, dmod_shard)

    g_w_in, g_w_out, g_w_up, g_w_down = gr["w_in"], gr["w_out"], gr["w_up"], gr["w_down"]
    in_slots = {"w_in", "w_out", "w_up", "w_down"}
    names = ["w_ada", "b_ada", "norm1_g", "w_in", "gla_w_gate", "gla_b_gate", "gla_norm_g", "q_norm_g", "k_norm_g",
             "w_out", "norm2_g", "w_up", "conv_w", "conv_b", "w_down"]
    ws = [w_ada, b_ada, norm1_g, w_in, gla_w_gate, gla_b_gate, gla_norm_g, q_norm_g, k_norm_g, w_out, norm2_g, w_up, conv_w, conv_b, w_down]
    ms = [m_w_ada, m_b_ada, m_norm1_g, m_w_in, m_gla_w_gate, m_gla_b_gate, m_gla_norm_g, m_q_norm_g, m_k_norm_g, m_w_out, m_norm2_g, m_w_up, m_conv_w, m_conv_b, m_w_down]
    vs = [v_w_ada, v_b_ada, v_norm1_g, v_w_in, v_gla_w_gate, v_gla_b_gate, v_gla_norm_g, v_q_norm_g, v_k_norm_g, v_w_out, v_norm2_g, v_w_up, v_conv_w, v_conv_b, v_w_down]
    gs = [g_w_ada, g_b_ada, t_n1g, g_w_in, g_wg, t_bg, t_gng, g_qng, g_kng, g_w_out, t_n2g, g_w_up, g_conv_w, t_conv_b, g_w_down]
    grads, deltas, new_ms, new_vs = [], [], [], []
    for nm, w, g, m, v in zip(names, ws, gs, ms, vs):
        g_, d_, m_, v_ = _adamw(w, g, m, v, "adamw_" + nm, slots=nm in in_slots)
        grads.append(g_)
        deltas.append(d_)
        new_ms.append(m_)
        new_vs.append(v_)
    return (loss, dx.reshape(x.shape), *grads, *deltas, *new_ms, *new_vs)
```

```python
import functools
import math

import jax
import jax.numpy as jnp
from jax import lax
from jax.experimental import pallas as pl
from jax.experimental.pallas import tpu as pltpu

F32, BF16 = jnp.float32, jnp.bfloat16
HI = lax.Precision.HIGHEST
EPS = 1e-6
NEG = -1e30

N_DEV = 8
GLA_HEADS, GLA_DK, GLA_DV, GLA_RANK, GLA_TAU, GLA_CHUNK = 4, 64, 128, 16, 16.0, 64
ATTN_HEADS, ATTN_HD, ATTN_BLOCK = 8, 64, 128
DILATIONS = (1, 4, 16)
GLA_QK, GLA_V, ATTN_DIM = GLA_HEADS * GLA_DK, GLA_HEADS * GLA_DV, ATTN_HEADS * ATTN_HD
O_GQ, O_GK, O_GV, O_GR, O_AQ, O_AK, O_AV, O_GLR = 0, 256, 512, 1024, 1536, 2048, 2560, 3072
PROJ_W = 3328
LANE = 128
GLR_SRC = 2 * GLA_QK + 2 * GLA_V

ADAM_LR, ADAM_B1, ADAM_B2, ADAM_EPS, ADAM_WD, ADAM_STEP = 0.001, 0.9, 0.999, 1e-08, 0.01, 10

VMEM_LIMIT = 56 * 1024 * 1024
SUM_BLOCK_ELEMS = 256 * 1024


def _cp(*sem):
    return pltpu.CompilerParams(dimension_semantics=sem, vmem_limit_bytes=VMEM_LIMIT)


def _dot(a, b, dims, precision=None):
    return lax.dot_general(a, b, (dims, ((), ())), preferred_element_type=F32, precision=precision)


NN, NT, TN = ((1,), (0,)), ((1,), (1,)), ((0,), (0,))


def _sigmoid(z):
    return 1.0 / (1.0 + jnp.exp(-z))


HBM_SPEC = pl.BlockSpec(memory_space=pltpu.HBM)


def _exchange_shapes(arrays, gather):
    return [jax.ShapeDtypeStruct((N_DEV,) + (a.shape if g else a.shape[1:]), a.dtype) for a, g in zip(arrays, gather)]


def _exchange_sems(n):
    return [pltpu.SemaphoreType.DMA((n * (N_DEV - 1),)), pltpu.SemaphoreType.DMA((n * (N_DEV - 1),)), pltpu.SemaphoreType.DMA((n,))]


VIA_SIBLING = "via sibling"


def _exchange_plan(ins, outs, gather, send_sems, recv_sems, local_sems):
    x, y, c = lax.axis_index("x"), lax.axis_index("y"), lax.axis_index("c")
    me = 4 * x + 2 * y + c
    start, relays, waits = [], [], []
    for a in range(len(ins)):
        if gather[a] == VIA_SIBLING:
            def copy(i, block, to, src=None, a=a):
                slot = outs[a].at[4 * block[0] + 2 * block[1] + block[2]]
                return pltpu.make_async_remote_copy(
                    src_ref=slot if src is None else src, dst_ref=slot, send_sem=send_sems.at[a * (N_DEV - 1) + i],
                    recv_sem=recv_sems.at[a * (N_DEV - 1) + i], device_id=to, device_id_type=pl.DeviceIdType.MESH)

            chips = [(1 - x, y), (x, 1 - y), (1 - x, 1 - y)]
            first = [copy(0, (x, y, c), (x, y, 1 - c), src=ins[a])]
            first += [copy(1 + j, (x, y, c), (*chip, c), src=ins[a]) for j, chip in enumerate(chips)]
            passed = [copy(4 + j, (*chip, c), (x, y, 1 - c)) for j, chip in enumerate(chips)]
            start += first
            relays += [(copy(1 + j, (*chip, c), (x, y, c)).wait_recv, passed[j]) for j, chip in enumerate(chips)]
            waits += [copy(0, (x, y, 1 - c), (x, y, c)).wait_recv]
            waits += [copy(4 + j, (*chip, 1 - c), (x, y, c)).wait_recv for j, chip in enumerate(chips)]
            waits += [cp.wait_send for cp in first + passed]
        else:
            for p in range(1, N_DEV):
                px, py, pc = x ^ (p >> 2), y ^ ((p >> 1) & 1), c ^ (p & 1)
                peer = 4 * px + 2 * py + pc
                k = a * (N_DEV - 1) + p - 1
                cp = pltpu.make_async_remote_copy(
                    src_ref=ins[a] if gather[a] else ins[a].at[peer], dst_ref=outs[a].at[me],
                    send_sem=send_sems.at[k], recv_sem=recv_sems.at[k],
                    device_id=(px, py, pc), device_id_type=pl.DeviceIdType.MESH)
                start.append(cp)
                waits.append(cp.wait)
        own = pltpu.make_async_copy(ins[a] if gather[a] else ins[a].at[me], outs[a].at[me], local_sems.at[a])
        start.append(own)
        waits.append(own.wait)
    return start, relays, waits


def _exchange_finish(relays, waits):
    for arrived, pass_on in relays:
        arrived()
        pass_on.start()
    for wait in waits:
        wait()


def _riding(body, n_in, n_out, gather, grid):
    nr = len(gather)
    if not nr:
        return body

    def wrapped(*refs):
        ins, r_ins = refs[:n_in], refs[n_in:n_in + nr]
        outs, r_outs = refs[n_in + nr:n_in + nr + n_out], refs[n_in + nr + n_out:n_in + 2 * nr + n_out]
        scratch = refs[n_in + 2 * nr + n_out:]
        first = last = None
        for t, steps in enumerate(grid):
            pid = pl.program_id(t)
            first = (pid == 0) if first is None else first & (pid == 0)
            last = (pid == steps - 1) if last is None else last & (pid == steps - 1)
        start, relays, waits = _exchange_plan(r_ins, r_outs, gather, *scratch[-3:])

        @pl.when(first)
        def _():
            for cp in start:
                cp.start()

        body(*ins, *outs, *scratch[:-3])

        @pl.when(last)
        def _():
            _exchange_finish(relays, waits)

    return wrapped


def _exchange(arrays, gather, name):
    n = len(arrays)

    def body(*refs):
        start, relays, waits = _exchange_plan(refs[:n], refs[n:2 * n], gather, *refs[2 * n:])
        for cp in start:
            cp.start()
        _exchange_finish(relays, waits)

    return pl.pallas_call(
        body, out_shape=_exchange_shapes(arrays, gather), in_specs=[HBM_SPEC] * n, out_specs=[HBM_SPEC] * n,
        scratch_shapes=_exchange_sems(n), name=name)(*arrays)


def _sum_slots(x, name):
    _, R, C = x.shape
    tr = max(t for t in range(8, min(SUM_BLOCK_ELEMS // C, R) + 1, 8) if R % t == 0)

    def body(x_ref, o_ref):
        acc = x_ref[0].astype(F32)
        for s in range(1, N_DEV):
            acc = acc + x_ref[s].astype(F32)
        o_ref[...] = acc

    return pl.pallas_call(
        body, grid=(R // tr,), in_specs=[pl.BlockSpec((N_DEV, tr, C), lambda i: (0, i, 0))],
        out_specs=pl.BlockSpec((tr, C), lambda i: (i, 0)), out_shape=jax.ShapeDtypeStruct((R, C), F32),
        compiler_params=_cp("parallel"), name=name)(x)


def _mm(a, b, mode, tm, tn, tk, out_dtype, name, ride=None):
    if mode == NN:
        (M, K), N = a.shape, b.shape[1]
    elif mode == NT:
        (M, K), N = a.shape, b.shape[0]
    else:
        (K, M), N = a.shape, b.shape[1]
    tm, tn, tk = min(tm, M), min(tn, N), min(tk, K)
    assert M % tm == 0 and N % tn == 0 and K % tk == 0, (name, M, N, K, tm, tn, tk)
    nk = K // tk
    if mode == NN:
        a_spec = pl.BlockSpec((tm, tk), lambda i, j, k: (i, k))
        b_spec = pl.BlockSpec((tk, tn), lambda i, j, k: (k, j))
    elif mode == NT:
        a_spec = pl.BlockSpec((tm, tk), lambda i, j, k: (i, k))
        b_spec = pl.BlockSpec((tn, tk), lambda i, j, k: (j, k))
    else:
        a_spec = pl.BlockSpec((tk, tm), lambda i, j, k: (k, i))
        b_spec = pl.BlockSpec((tk, tn), lambda i, j, k: (k, j))

    ride_arrays, ride_gather = ride if ride else ([], [])
    nr = len(ride_arrays)
    grid = (M // tm, N // tn, nk)

    own_acc = nk > 1 and out_dtype != F32

    def body(a_ref, b_ref, o_ref, *acc):
        p = _dot(a_ref[...].astype(BF16), b_ref[...].astype(BF16), mode)
        if nk == 1:
            o_ref[...] = p.astype(out_dtype)
        else:
            acc_ref = acc[0] if own_acc else o_ref
            k = pl.program_id(2)

            @pl.when(k == 0)
            def _():
                acc_ref[...] = p

            @pl.when(k > 0)
            def _():
                acc_ref[...] += p

            if own_acc:
                @pl.when(k == nk - 1)
                def _():
                    o_ref[...] = acc_ref[...].astype(out_dtype)

    outs = pl.pallas_call(
        _riding(body, 2, 1, ride_gather, grid), grid=grid, in_specs=[a_spec, b_spec] + [HBM_SPEC] * nr,
        out_specs=[pl.BlockSpec((tm, tn), lambda i, j, k: (i, j))] + [HBM_SPEC] * nr,
        out_shape=[jax.ShapeDtypeStruct((M, N), out_dtype)] + _exchange_shapes(ride_arrays, ride_gather),
        scratch_shapes=([pltpu.VMEM((tm, tn), F32)] if own_acc else []) + (_exchange_sems(nr) if nr else []),
        compiler_params=_cp(*(("arbitrary",) * 3 if nr else ("parallel", "parallel", "arbitrary"))), name=name)(a, b, *ride_arrays)
    return (outs[0], outs[1:]) if nr else outs[0]


def _ada_fwd(c_all, w_shard, b_shard):
    Nc = w_shard.shape[1]

    def body(c_ref, w_ref, b_ref, o_ref):
        cv = c_ref[...]
        o_ref[...] = _dot(cv * _sigmoid(cv), w_ref[...], NN, HI) + b_ref[...]

    return pl.pallas_call(body, out_shape=jax.ShapeDtypeStruct((N_DEV, Nc), F32), name="ada_fwd",
                          compiler_params=pltpu.CompilerParams(vmem_limit_bytes=VMEM_LIMIT))(c_all, w_shard, b_shard)


def _ada_bwd(c_all, dmod_shard):
    D, Nc = c_all.shape[1], dmod_shard.shape[1]

    def body(c_ref, d_ref, o_ref):
        cv = c_ref[...]
        o_ref[...] = _dot(cv * _sigmoid(cv), d_ref[...], TN, HI)

    return pl.pallas_call(body, out_shape=jax.ShapeDtypeStruct((D, Nc), F32), name="ada_bwd",
                          compiler_params=pltpu.CompilerParams(vmem_limit_bytes=VMEM_LIMIT))(c_all, dmod_shard)


def _row_spec(ts, D):
    return pl.BlockSpec((ts, D), lambda i: (i, 0))


def _vec_spec(D):
    return pl.BlockSpec((1, D), lambda i: (0, 0))


def _col_spec(D, ts):
    return pl.BlockSpec((D, ts), lambda i: (0, i))


def _rms_mod(x, ng, sc, sh, ts, name, ride=None):
    S, D = x.shape
    ride_arrays, ride_gather = ride if ride else ([], [])
    nr = len(ride_arrays)
    grid = (S // ts,)

    def body(x_ref, ng_ref, sc_ref, sh_ref, h_ref, ht_ref):
        xv = x_ref[...]
        r = lax.rsqrt(jnp.mean(xv * xv, axis=-1, keepdims=True) + EPS)
        h = xv * r * ng_ref[...] * (1.0 + sc_ref[...]) + sh_ref[...]
        h_ref[...] = h.astype(BF16)
        ht_ref[...] = h.T.astype(BF16)

    outs = pl.pallas_call(
        _riding(body, 4, 2, ride_gather, grid), grid=grid, in_specs=[_row_spec(ts, D)] + [_vec_spec(D)] * 3 + [HBM_SPEC] * nr,
        out_specs=[_row_spec(ts, D), _col_spec(D, ts)] + [HBM_SPEC] * nr,
        out_shape=[jax.ShapeDtypeStruct((S, D), BF16), jax.ShapeDtypeStruct((D, S), BF16)] + _exchange_shapes(ride_arrays, ride_gather),
        scratch_shapes=_exchange_sems(nr) if nr else [],
        compiler_params=_cp("arbitrary"), name=name)(x, ng, sc, sh, *ride_arrays)
    return outs[0], outs[1], outs[2:]


def _mm_rows(a, b, mode, tm, extras, extra_specs, out_shapes, out_specs, epilogue, name, ride=None):
    M, K = a.shape
    grid = (M // tm,)
    ride_arrays, ride_gather = ride if ride else ([], [])
    nr = len(ride_arrays)

    def body(a_ref, b_ref, *refs):
        epilogue(_dot(a_ref[...].astype(BF16), b_ref[...].astype(BF16), mode), pl.program_id(0), *refs)

    outs = pl.pallas_call(
        _riding(body, 2 + len(extras), len(out_shapes), ride_gather, grid), grid=grid,
        in_specs=[pl.BlockSpec((tm, K), lambda i: (i, 0)), pl.BlockSpec(b.shape, lambda i: (0, 0), pipeline_mode=pl.Buffered(1))]
        + list(extra_specs) + [HBM_SPEC] * nr,
        out_specs=list(out_specs) + [HBM_SPEC] * nr,
        out_shape=list(out_shapes) + _exchange_shapes(ride_arrays, ride_gather),
        scratch_shapes=_exchange_sems(nr) if nr else [],
        compiler_params=_cp("arbitrary"), name=name)(a, b, *extras, *ride_arrays)
    return outs[:len(out_shapes)], outs[len(out_shapes):]


def _accumulate(ref, part, step):
    @pl.when(step == 0)
    def _():
        ref[...] = part

    @pl.when(step > 0)
    def _():
        ref[...] += part


def _rows8(rows, width):
    return jnp.concatenate(rows + [jnp.zeros((8 - len(rows), width), F32)], axis=0)


def _mm_resid_rms_mod(a, w, x, g, ng, sc, sh, tm, name):
    S, D = x.shape

    def epilogue(t, step, x_ref, g_ref, ng_ref, sc_ref, sh_ref, t_ref, x2_ref, h_ref, ht_ref):
        t_ref[...] = t
        xv = x_ref[...] + g_ref[...] * t
        x2_ref[...] = xv
        r = lax.rsqrt(jnp.mean(xv * xv, axis=-1, keepdims=True) + EPS)
        h = xv * r * ng_ref[...] * (1.0 + sc_ref[...]) + sh_ref[...]
        h_ref[...] = h.astype(BF16)
        ht_ref[...] = h.T.astype(BF16)

    row, vec = _row_spec(tm, D), _vec_spec(D)
    full, half = jax.ShapeDtypeStruct((S, D), F32), jax.ShapeDtypeStruct((S, D), BF16)
    outs, _ = _mm_rows(a, w, NN, tm, [x, g, ng, sc, sh], [row] + [vec] * 4,
                       [full, full, half, jax.ShapeDtypeStruct((D, S), BF16)], [row, row, row, _col_spec(D, tm)], epilogue, name)
    return outs


def _mm_loss_resid(a, w, x2, g2, target, tm, name):
    S, D = x2.shape

    def epilogue(t, step, x_ref, y_ref, g_ref, dx_ref, dt_ref, sums_ref):
        gv = g_ref[...]
        e = x_ref[...] + gv * t - y_ref[...]
        dx = e * (1.0 / D)
        dx_ref[...] = dx
        dt_ref[...] = (dx * gv).astype(BF16)
        _accumulate(sums_ref, _rows8([jnp.sum(e * e, axis=0, keepdims=True), jnp.sum(dx * t, axis=0, keepdims=True)], D), step)

    row, vec = _row_spec(tm, D), _vec_spec(D)
    outs, _ = _mm_rows(a, w, NN, tm, [x2, target, g2], [row, row, vec],
                       [jax.ShapeDtypeStruct((S, D), F32), jax.ShapeDtypeStruct((S, D), BF16), jax.ShapeDtypeStruct((8, D), F32)],
                       [row, row, pl.BlockSpec((8, D), lambda i: (0, 0))], epilogue, name)
    return outs


def _mm_rms_mod_bwd(a, w, xin, dres, ng, sc, tm, name, t_prev=None, g_prev=None, ride=None):
    S, D = xin.shape
    chain = t_prev is not None

    def epilogue(dhv, step, *refs):
        if chain:
            x_ref, dr_ref, ng_ref, sc_ref, t_ref, g_ref, dx_ref, sums_ref, dt_ref = refs
        else:
            x_ref, dr_ref, ng_ref, sc_ref, dx_ref, sums_ref = refs
        xv = x_ref[...]
        r = lax.rsqrt(jnp.mean(xv * xv, axis=-1, keepdims=True) + EPS)
        xh = xv * r
        ngv, scv = ng_ref[...], sc_ref[...]
        dxh = dhv * (ngv * (1.0 + scv))
        dx = dr_ref[...] + r * (dxh - xh * jnp.mean(dxh * xh, axis=-1, keepdims=True))
        dx_ref[...] = dx
        dhx = dhv * xh
        rows = [jnp.sum(dhv, axis=0, keepdims=True), jnp.sum(dhx * ngv, axis=0, keepdims=True),
                jnp.sum(dhx * (1.0 + scv), axis=0, keepdims=True)]
        if chain:
            dt_ref[...] = (dx * g_ref[...]).astype(BF16)
            rows.append(jnp.sum(dx * t_ref[...], axis=0, keepdims=True))
        _accumulate(sums_ref, _rows8(rows, D), step)

    row, vec = _row_spec(tm, D), _vec_spec(D)
    extras = [xin, dres, ng, sc] + ([t_prev, g_prev] if chain else [])
    extra_specs = [row, row, vec, vec] + ([row, vec] if chain else [])
    out_shapes = [jax.ShapeDtypeStruct((S, D), F32), jax.ShapeDtypeStruct((8, D), F32)] + (
        [jax.ShapeDtypeStruct((S, D), BF16)] if chain else [])
    out_specs = [row, pl.BlockSpec((8, D), lambda i: (0, 0))] + ([row] if chain else [])
    return _mm_rows(a, w, NT, tm, extras, extra_specs, out_shapes, out_specs, epilogue, name, ride=ride)


def _mm_in(h1, w_in_p, wg_p, bg, qg, kg, tm, ride=None):
    S = h1.shape[0]
    W = ATTN_DIM

    def epilogue(p, step, wg_ref, bg_ref, qg_ref, kg_ref, proj_ref, la_ref, qn_ref, kn_ref):
        proj_ref[...] = p
        z = _dot(p[:, O_GLR:O_GLR + LANE], wg_ref[...], NN, HI) + bg_ref[...]
        la_ref[...] = (jnp.minimum(z, 0.0) - jnp.log(1.0 + jnp.exp(-jnp.abs(z)))) * (1.0 / GLA_TAU)
        seg = _seg_matrix(W, ATTN_HD, 1.0 / ATTN_HD)
        for off, g_ref, o_ref, scale in ((O_AQ, qg_ref, qn_ref, ATTN_HD ** -0.5), (O_AK, kg_ref, kn_ref, 1.0)):
            xv = p[:, off:off + W]
            o_ref[...] = xv * lax.rsqrt(_seg_sum(xv * xv, seg) + EPS) * (g_ref[...] * scale)

    row = lambda w: pl.BlockSpec((tm, w), lambda i: (i, 0))
    const = lambda a: pl.BlockSpec(a.shape, lambda i: (0, 0))
    return _mm_rows(
        h1, w_in_p, NN, tm, [wg_p, bg, qg, kg], [const(wg_p), const(bg), const(qg), const(kg)],
        [jax.ShapeDtypeStruct((S, PROJ_W), F32), jax.ShapeDtypeStruct((S, GLA_QK), F32),
         jax.ShapeDtypeStruct((S, W), F32), jax.ShapeDtypeStruct((S, W), F32)],
        [row(PROJ_W), row(GLA_QK), row(W), row(W)], epilogue, "mm_in", ride=ride)


def _gate_bwd(dla, la, proj, wg_p, ts):
    S = proj.shape[0]

    def body(dla_ref, la_ref, glr_ref, w_ref, dglr_ref, gw_ref, gb_ref):
        i = pl.program_id(0)
        dz = dla_ref[...] * (1.0 / GLA_TAU) * (1.0 - jnp.exp(GLA_TAU * la_ref[...]))
        dglr_ref[...] = _dot(dz, w_ref[...], NT, HI).astype(BF16)
        gw = _dot(glr_ref[...], dz, TN, HI)
        gb = jnp.concatenate([jnp.sum(dz, axis=0, keepdims=True), jnp.zeros((7, GLA_QK), F32)], axis=0)

        @pl.when(i == 0)
        def _():
            gw_ref[...] = gw
            gb_ref[...] = gb

        @pl.when(i > 0)
        def _():
            gw_ref[...] += gw
            gb_ref[...] += gb

    return pl.pallas_call(
        body, grid=(S // ts,),
        in_specs=[pl.BlockSpec((ts, GLA_QK), lambda i: (i, 0)), pl.BlockSpec((ts, GLA_QK), lambda i: (i, 0)),
                  pl.BlockSpec((ts, LANE), lambda i: (i, O_GLR // LANE)), pl.BlockSpec((LANE, GLA_QK), lambda i: (0, 0))],
        out_specs=[pl.BlockSpec((ts, LANE), lambda i: (i, 0)), pl.BlockSpec((LANE, GLA_QK), lambda i: (0, 0)),
                   pl.BlockSpec((8, GLA_QK), lambda i: (0, 0))],
        out_shape=[jax.ShapeDtypeStruct((S, LANE), BF16), jax.ShapeDtypeStruct((LANE, GLA_QK), F32),
                   jax.ShapeDtypeStruct((8, GLA_QK), F32)],
        compiler_params=_cp("arbitrary"), name="gla_gate_bwd")(dla, la, proj, wg_p)


def _tri(lower):
    r = lax.broadcasted_iota(jnp.int32, (GLA_CHUNK, GLA_CHUNK), 0)
    c = lax.broadcasted_iota(jnp.int32, (GLA_CHUNK, GLA_CHUNK), 1)
    return jnp.where((r >= c) if lower else (c >= r), 1.0, 0.0).astype(F32)


GLA_SUB = 16
GLA_NSUB = GLA_CHUNK // GLA_SUB
PAIR_QK = 2 * GLA_DK
PAIR_V = 2 * GLA_DV


def _band_selector():
    r = lax.broadcasted_iota(jnp.int32, (GLA_SUB * PAIR_QK, LANE), 0)
    c = lax.broadcasted_iota(jnp.int32, (GLA_SUB * PAIR_QK, LANE), 1)
    dist, head = r // PAIR_QK, (r % PAIR_QK) // GLA_DK
    return jnp.where(c == head * GLA_DK + (GLA_SUB - 1 - dist), 1.0, 0.0).astype(BF16)


def _flip_matrix():
    r = lax.broadcasted_iota(jnp.int32, (GLA_CHUNK, GLA_CHUNK), 0)
    c = lax.broadcasted_iota(jnp.int32, (GLA_CHUNK, GLA_CHUNK), 1)
    return jnp.where(r + c == GLA_CHUNK - 1, 1.0, 0.0).astype(BF16)


def _state_mask():
    r = lax.broadcasted_iota(jnp.int32, (PAIR_V, PAIR_QK), 0)
    c = lax.broadcasted_iota(jnp.int32, (PAIR_V, PAIR_QK), 1)
    return (r < GLA_DV) == (c < GLA_DK)


class _GlaChunk:
    def __init__(self, qs, kc, vc, g, sel, exact):
        C = GLA_CHUNK
        self.qs, self.kc, self.vc = qs, kc, vc
        rows = lax.broadcasted_iota(jnp.int32, (C, 1), 0)
        lane = lax.broadcasted_iota(jnp.int32, (1, PAIR_QK), 1)
        self.rows, self.lane = rows, lane
        b = _dot(_tri(True), g, NN, HI)
        self.bl = b[C - 1:C, :]
        self.eb = jnp.exp(b)
        self.kdec = jnp.exp(self.bl - b)
        edge = lambda J: b[GLA_SUB * (J + 1):GLA_SUB * (J + 1) + 1, :]
        self.e_far = [jnp.exp(jnp.where(rows >= GLA_SUB * (J + 1), b - edge(J), NEG)) for J in range(GLA_NSUB - 1)]
        blk = rows // GLA_SUB
        bnext = edge(0)
        for J in range(1, GLA_NSUB - 1):
            bnext = jnp.where(blk == J, edge(J), bnext)
        self.e_khat = jnp.exp(jnp.where(blk < GLA_NSUB - 1, bnext - b, NEG))
        khat = kc * self.e_khat
        k2 = jnp.concatenate([jnp.where(lane < GLA_DK, khat, 0.0), jnp.where(lane >= GLA_DK, khat, 0.0)], axis=0)
        self.blk2 = jnp.concatenate([blk, blk], axis=0)
        self.m_far = jnp.concatenate([jnp.where(self.blk2 == J, k2, 0.0) for J in range(GLA_NSUB - 1)], axis=1).astype(BF16)
        self.qcat = jnp.concatenate([qs * e for e in self.e_far], axis=1).astype(BF16)
        a_far = _dot(self.qcat, self.m_far, NT)
        self.e_band, self.rk, hi_terms, lo_terms = [], [], [], []
        for d in range(GLA_SUB):
            rk = pltpu.roll(kc, d, 0) if d else kc
            rb = pltpu.roll(b, d, 0) if d else b
            e = jnp.exp(jnp.where(rows >= d, b - rb, NEG))
            self.e_band.append(e)
            self.rk.append(rk)
            if exact:
                t = (qs * e).astype(BF16).astype(F32) * rk.astype(BF16).astype(F32)
                hi = t.astype(BF16)
                hi_terms.append(hi)
                lo_terms.append((t - hi.astype(F32)).astype(BF16))
            else:
                hi_terms.append((qs * rk * e).astype(BF16))
        band = _dot(jnp.concatenate(hi_terms, axis=1), sel, NN)
        if exact:
            band = band + _dot(jnp.concatenate(lo_terms, axis=1), sel, NN)
        a_band = pltpu.roll(band, LANE - (GLA_SUB - 1), 1, stride=1, stride_axis=0)
        dist = rows - lane % GLA_DK
        self.far_mask = dist >= GLA_SUB
        self.band_mask = (dist >= 0) & (dist < GLA_SUB)
        self.a = (a_band + jnp.where(self.far_mask, a_far, 0.0)).astype(BF16)
        self.lane_v = lax.broadcasted_iota(jnp.int32, (1, PAIR_V), 1)
        self.v2 = jnp.concatenate([jnp.where(self.lane_v < GLA_DV, vc, 0.0), jnp.where(self.lane_v >= GLA_DV, vc, 0.0)],
                                  axis=0).astype(BF16)


def _gla_fwd(proj, la, tb, ride=None):
    S = proj.shape[0]
    C = GLA_CHUNK
    tb = min(tb, S)
    nbc = tb // C
    npair = GLA_HEADS // 2
    scale = GLA_DK ** -0.5

    def body(q_ref, k_ref, v_ref, la_ref, sel_ref, o_ref, st_ref, state):
        @pl.when(pl.program_id(1) == 0)
        def _():
            state[...] = jnp.zeros_like(state)

        def chunk(ci):
            sl = pl.ds(ci * C, C)
            ch = _GlaChunk(q_ref[sl, :] * scale, k_ref[sl, :], v_ref[sl, :], la_ref[sl, :], sel_ref[...], exact=ci == 0)
            st = state[...]
            st_ref[0, ci] = st
            o_ref[sl, :] = _dot((ch.qs * ch.eb).astype(BF16), st.astype(BF16), NT) + _dot(ch.a, ch.v2, NN)
            upd = _dot(ch.vc.astype(BF16), (ch.kc * ch.kdec).astype(BF16), TN)
            state[...] = st * jnp.exp(ch.bl) + jnp.where(_state_mask(), upd, 0.0)

        for ci in range(nbc):
            chunk(ci)

    qspec = lambda off: pl.BlockSpec((tb, PAIR_QK), lambda p, i: (i, off // PAIR_QK + p))
    ride_arrays, ride_gather = ride if ride else ([], [])
    nr = len(ride_arrays)
    grid = (npair, S // tb)
    outs = pl.pallas_call(
        _riding(body, 5, 2, ride_gather, grid), grid=grid,
        in_specs=[qspec(O_GQ), qspec(O_GK), pl.BlockSpec((tb, PAIR_V), lambda p, i: (i, O_GV // PAIR_V + p)),
                  pl.BlockSpec((tb, PAIR_QK), lambda p, i: (i, p)),
                  pl.BlockSpec((GLA_SUB * PAIR_QK, LANE), lambda p, i: (0, 0))] + [HBM_SPEC] * nr,
        out_specs=[pl.BlockSpec((tb, PAIR_V), lambda p, i: (i, p)),
                   pl.BlockSpec((1, nbc, PAIR_V, PAIR_QK), lambda p, i: (p, i, 0, 0))] + [HBM_SPEC] * nr,
        out_shape=[jax.ShapeDtypeStruct((S, GLA_V), F32), jax.ShapeDtypeStruct((npair, S // C, PAIR_V, PAIR_QK), F32)]
        + _exchange_shapes(ride_arrays, ride_gather),
        scratch_shapes=[pltpu.VMEM((PAIR_V, PAIR_QK), F32)] + (_exchange_sems(nr) if nr else []),
        compiler_params=_cp("arbitrary", "arbitrary"), name="gla_fwd")(proj, proj, proj, la, _band_selector(), *ride_arrays)
    return outs[0], outs[1], outs[2:]


def _gla_bwd(proj, la, do, states, tb, ride=None):
    S = proj.shape[0]
    C = GLA_CHUNK
    tb = min(tb, S)
    nbc = tb // C
    nblk = S // tb
    npair = GLA_HEADS // 2
    scale = GLA_DK ** -0.5

    def body(q_ref, k_ref, v_ref, la_ref, do_ref, st_ref, sel_ref, selt_ref, dq_ref, dk_ref, dv_ref, dla_ref, dstate):
        @pl.when(pl.program_id(1) == 0)
        def _():
            dstate[...] = jnp.zeros_like(dstate)

        def chunk(ci):
            sl = pl.ds(ci * C, C)
            ch = _GlaChunk(q_ref[sl, :] * scale, k_ref[sl, :], v_ref[sl, :], la_ref[sl, :], sel_ref[...], exact=ci == 0)
            qs, kc, rows = ch.qs, ch.kc, ch.rows
            doc_b = do_ref[sl, :].astype(BF16)
            st = st_ref[0, ci]
            dst = dstate[...]
            dst_b = dst.astype(BF16)
            ebl = jnp.exp(ch.bl)
            dq = _dot(doc_b, st.astype(BF16), NN) * ch.eb
            dk = _dot(ch.vc.astype(BF16), dst_b, NN) * ch.kdec
            dv = _dot((kc * ch.kdec).astype(BF16), dst_b, NT)
            dbl = jnp.sum(dst * st, axis=0, keepdims=True) * ebl + jnp.sum(kc * dk, axis=0, keepdims=True)
            da = _dot(doc_b, ch.v2, NT)
            dv2 = _dot(ch.a, doc_b, TN)
            dv = dv + jnp.where(ch.lane_v < GLA_DV, dv2[:C], dv2[C:])
            da_far = jnp.where(ch.far_mask, da, 0.0).astype(BF16)
            dqcat = _dot(da_far, ch.m_far, NN)
            dm = _dot(da_far, ch.qcat, TN)
            dk2 = jnp.zeros((2 * C, PAIR_QK), F32)
            for J in range(GLA_NSUB - 1):
                dq = dq + dqcat[:, J * PAIR_QK:(J + 1) * PAIR_QK] * ch.e_far[J]
                dk2 = dk2 + jnp.where(ch.blk2 == J, dm[:, J * PAIR_QK:(J + 1) * PAIR_QK], 0.0)
            dk = dk + jnp.where(ch.lane < GLA_DK, dk2[:C], dk2[C:]) * ch.e_khat
            flip = _flip_matrix()
            da_band = _dot(flip, jnp.where(ch.band_mask, da, 0.0).astype(BF16), NN)
            dband = pltpu.roll(da_band, LANE - (C - GLA_SUB), 1, stride=1, stride_axis=0)
            dband = _dot(flip, dband.astype(BF16), NN)
            dterms = _dot(dband.astype(BF16), selt_ref[...], NN)
            for d in range(GLA_SUB):
                dt = dterms[:, d * PAIR_QK:(d + 1) * PAIR_QK]
                dq = dq + dt * (ch.rk[d] * ch.e_band[d])
                dkr = dt * (qs * ch.e_band[d])
                dk = dk + (pltpu.roll(dkr, C - d, 0) if d else dkr)
            db = qs * dq - kc * dk
            db = jnp.where(rows == C - 1, db + dbl, db)
            dq_ref[sl, :] = (dq * scale).astype(BF16)
            dk_ref[sl, :] = dk.astype(BF16)
            dv_ref[sl, :] = dv.astype(BF16)
            dla_ref[sl, :] = _dot(_tri(False), db, NN, HI)
            upd = _dot(doc_b, (qs * ch.eb).astype(BF16), TN)
            dstate[...] = dst * ebl + jnp.where(_state_mask(), upd, 0.0)

        for ci in reversed(range(nbc)):
            chunk(ci)

    rev = lambda i: nblk - 1 - i
    qspec = lambda off: pl.BlockSpec((tb, PAIR_QK), lambda p, i: (rev(i), off // PAIR_QK + p))
    pair_qk = pl.BlockSpec((tb, PAIR_QK), lambda p, i: (rev(i), p))
    pair_v = pl.BlockSpec((tb, PAIR_V), lambda p, i: (rev(i), p))
    sel = _band_selector()
    ride_arrays, ride_gather = ride if ride else ([], [])
    nr = len(ride_arrays)
    grid = (npair, nblk)
    outs = pl.pallas_call(
        _riding(body, 8, 4, ride_gather, grid), grid=grid,
        in_specs=[qspec(O_GQ), qspec(O_GK), pl.BlockSpec((tb, PAIR_V), lambda p, i: (rev(i), O_GV // PAIR_V + p)),
                  pair_qk, pair_v, pl.BlockSpec((1, nbc, PAIR_V, PAIR_QK), lambda p, i: (p, rev(i), 0, 0)),
                  pl.BlockSpec((GLA_SUB * PAIR_QK, LANE), lambda p, i: (0, 0)),
                  pl.BlockSpec((LANE, GLA_SUB * PAIR_QK), lambda p, i: (0, 0))] + [HBM_SPEC] * nr,
        out_specs=[pair_qk, pair_qk, pair_v, pair_qk] + [HBM_SPEC] * nr,
        out_shape=[jax.ShapeDtypeStruct((S, GLA_QK), BF16), jax.ShapeDtypeStruct((S, GLA_QK), BF16),
                   jax.ShapeDtypeStruct((S, GLA_V), BF16), jax.ShapeDtypeStruct((S, GLA_QK), F32)]
        + _exchange_shapes(ride_arrays, ride_gather),
        scratch_shapes=[pltpu.VMEM((PAIR_V, PAIR_QK), F32)] + (_exchange_sems(nr) if nr else []),
        compiler_params=_cp("arbitrary", "arbitrary"), name="gla_bwd")(proj, proj, proj, la, do, states, sel, sel.T, *ride_arrays)
    return outs[0], outs[1], outs[2], outs[3], outs[4:]


def _gla_out(o, proj, gng, ts):
    S = o.shape[0]

    def body(o_ref, gr_ref, g_ref, y_ref):
        for h in range(GLA_HEADS):
            cols = slice(h * GLA_DV, (h + 1) * GLA_DV)
            ov, grv = o_ref[:, cols], gr_ref[:, cols]
            r = lax.rsqrt(jnp.mean(ov * ov, axis=-1, keepdims=True) + EPS)
            y_ref[:, cols] = (ov * r * g_ref[...] * (grv * _sigmoid(grv))).astype(BF16)

    return pl.pallas_call(
        body, grid=(S // ts,),
        in_specs=[pl.BlockSpec((ts, GLA_V), lambda i: (i, 0)), pl.BlockSpec((ts, GLA_V), lambda i: (i, O_GR // GLA_V)),
                  pl.BlockSpec((1, GLA_DV), lambda i: (0, 0))],
        out_specs=pl.BlockSpec((ts, GLA_V), lambda i: (i, 0)), out_shape=jax.ShapeDtypeStruct((S, GLA_V), BF16),
        compiler_params=_cp("parallel"), name="gla_out_fwd")(o, proj, gng)


def _mm_mixed_bwd(dt1, w_out, o, proj, gng, y_att, tm):
    S = o.shape[0]
    W = ATTN_DIM

    def epilogue(dm, step, o_ref, gr_ref, g_ref, y_ref, do_ref, dgr_ref, gg_ref, dy_ref, de_ref):
        gsum = jnp.zeros((1, GLA_DV), F32)
        for h in range(GLA_HEADS):
            cols = slice(h * GLA_DV, (h + 1) * GLA_DV)
            ov, grv, dy = o_ref[:, cols], gr_ref[:, cols], dm[:, cols]
            r = lax.rsqrt(jnp.mean(ov * ov, axis=-1, keepdims=True) + EPS)
            oh = ov * r
            sg = _sigmoid(grv)
            don = dy * (grv * sg)
            dgr_ref[:, cols] = (dy * (oh * g_ref[...]) * (sg * (1.0 + grv * (1.0 - sg)))).astype(BF16)
            gsum = gsum + jnp.sum(don * oh, axis=0, keepdims=True)
            doh = don * g_ref[...]
            do_ref[:, cols] = r * (doh - oh * jnp.mean(doh * oh, axis=-1, keepdims=True))
        _accumulate(gg_ref, _rows8([gsum], GLA_DV), step)
        dya = dm[:, GLA_V:]
        dy_ref[...] = dya
        de_ref[...] = _seg_sum(dya * y_ref[...], _seg_matrix(W, ATTN_HD, 1.0))

    half = pl.BlockSpec((tm, GLA_V), lambda i: (i, 0))
    outs, _ = _mm_rows(
        dt1, w_out, NT, tm, [o, proj, gng, y_att],
        [half, pl.BlockSpec((tm, GLA_V), lambda i: (i, O_GR // GLA_V)), pl.BlockSpec((1, GLA_DV), lambda i: (0, 0)), half],
        [jax.ShapeDtypeStruct((S, GLA_V), F32), jax.ShapeDtypeStruct((S, GLA_V), BF16), jax.ShapeDtypeStruct((8, GLA_DV), F32),
         jax.ShapeDtypeStruct((S, W), F32), jax.ShapeDtypeStruct((S, W), F32)],
        [half, half, pl.BlockSpec((8, GLA_DV), lambda i: (0, 0)), half, half], epilogue, "mm_dmixed")
    return outs


def _seg_matrix(width, seg, value):
    r = lax.broadcasted_iota(jnp.int32, (width, width), 0) // seg
    c = lax.broadcasted_iota(jnp.int32, (width, width), 1) // seg
    return jnp.where(r == c, value, 0.0).astype(BF16)


def _seg_sum(x, seg_matrix):
    hi = x.astype(BF16)
    lo = (x - hi.astype(F32)).astype(BF16)
    return _dot(hi, seg_matrix, NN) + _dot(lo, seg_matrix, NN)


ATTN_GROUP = 4


ATTN_TILE = max(DILATIONS) * ATTN_BLOCK


def _attn_rows(d, g, r, base=0):
    start = base + (g * d * ATTN_BLOCK if g >= 0 else ATTN_TILE - d * ATTN_BLOCK) + r
    return pl.ds(start, ATTN_BLOCK) if d == 1 else pl.ds(start, ATTN_BLOCK, stride=d)


def _for_blocks(d, G, fn):
    for g in range(G):
        if d <= ATTN_GROUP:
            for r in range(d):
                fn(g, r)
        else:
            def step(r, carry, g=g):
                fn(g, r)
                return carry
            lax.fori_loop(0, d, step, 0, unroll=ATTN_GROUP)


def _attn_specs(S):
    nb = S // ATTN_TILE

    def specs(off=0):
        return [pl.BlockSpec((ATTN_TILE, LANE), lambda hp, n: (n, off + hp)),
                pl.BlockSpec((ATTN_TILE, LANE), lambda hp, n: (jnp.maximum(n - 1, 0), off + hp)),
                pl.BlockSpec((ATTN_TILE, LANE), lambda hp, n: (jnp.minimum(n + 1, nb - 1), off + hp))]

    return nb, specs


def _slope(head):
    one = jnp.ones((1, 1), jnp.int32)
    return 1.0 / jnp.left_shift(one, one * (head + 1)).astype(F32)


def _attn_bias(d, hp, first_tile):
    B = ATTN_BLOCK
    iq = lax.broadcasted_iota(jnp.int32, (B, 2 * B), 0)
    ik = lax.broadcasted_iota(jnp.int32, (B, 2 * B), 1)
    rel = iq + B - ik
    window = (rel >= 0) & (rel <= B)
    relf = (d * rel).astype(F32)
    full = [jnp.where(window, -_slope(hp * 2 + h) * relf, NEG) for h in range(2)]
    edge = [jnp.where((ik >= B) | jnp.logical_not(first_tile), b, NEG) for b in full]
    return full, edge


def _attn_bias_t(d, hp, has_next):
    B = ATTN_BLOCK
    ik = lax.broadcasted_iota(jnp.int32, (B, B), 0)
    iq = lax.broadcasted_iota(jnp.int32, (B, B), 1)
    tiles = []
    for nxt in range(2):
        rel = iq - ik + nxt * B
        window = (rel >= 0) & (rel <= B)
        relf = (d * rel).astype(F32)
        tiles.append([jnp.where(window, -_slope(hp * 2 + h) * relf, NEG) for h in range(2)])
    tiles.append([jnp.where(has_next, b, NEG) for b in tiles[1]])
    return tiles


def _attn_fwd(qn, kn, proj):
    S, W = qn.shape
    T = ATTN_TILE
    nb, specs = _attn_specs(S)

    def body(q_ref, kp_ref, kc_ref, vp_ref, vc_ref, y_ref, l_ref, o_scr, l_scr):
        hp, n = pl.program_id(0), pl.program_id(1)
        lo = lax.broadcasted_iota(jnp.int32, (1, LANE), 1) < ATTN_HD
        for b, d in enumerate(DILATIONS):
            full, edge = _attn_bias(d, hp, n == 0)

            def sub(g, r, b=b, d=d, full=full, edge=edge):
                rows, before = _attn_rows(d, g, r), _attn_rows(d, g - 1, r)
                kb_ref, vb_ref = (kp_ref, vp_ref) if g == 0 else (kc_ref, vc_ref)
                bias = edge if g == 0 else full
                qv = q_ref[rows, :].astype(BF16)
                kv = jnp.concatenate([kb_ref[before, :], kc_ref[rows, :]], axis=0).astype(BF16)
                vv = jnp.concatenate([vb_ref[before, :], vc_ref[rows, :]], axis=0).astype(BF16)
                outs, lses = [], []
                for h in range(2):
                    qm = jnp.where(lo == (h == 0), qv, jnp.zeros_like(qv))
                    s = _dot(qm, kv, NT) + bias[h]
                    m = jnp.max(s, axis=-1, keepdims=True)
                    p = jnp.exp(s - m)
                    den = jnp.sum(p, axis=-1, keepdims=True)
                    outs.append(_dot(p.astype(BF16), vv, NN) / den)
                    lses.append(m + jnp.log(den))
                kept = _attn_rows(d, g, r, base=b * T)
                o_scr[kept, :] = jnp.where(lo, outs[0], outs[1])
                l_scr[kept, :] = jnp.where(lo, lses[0], lses[1])

            _for_blocks(d, T // (d * ATTN_BLOCK), sub)
        l1, l2, l3 = [l_scr[pl.ds(b * T, T), :] for b in range(len(DILATIONS))]
        o1, o2, o3 = [o_scr[pl.ds(b * T, T), :] for b in range(len(DILATIONS))]
        m = jnp.maximum(jnp.maximum(l1, l2), l3)
        e1, e2, e3 = jnp.exp(l1 - m), jnp.exp(l2 - m), jnp.exp(l3 - m)
        tot = e1 + e2 + e3
        y_ref[...] = (e1 * o1 + e2 * o2 + e3 * o3) / tot
        l_ref[...] = m + jnp.log(tot)

    cur, prev, _ = specs()
    vcur, vprev, _ = specs(O_AV // LANE)
    return pl.pallas_call(
        body, grid=(W // LANE, nb), in_specs=[cur, prev, cur, vprev, vcur], out_specs=[cur, cur],
        out_shape=[jax.ShapeDtypeStruct((S, W), F32)] * 2,
        scratch_shapes=[pltpu.VMEM((len(DILATIONS) * T, LANE), F32)] * 2,
        compiler_params=_cp("parallel", "arbitrary"), name="attn_fwd")(qn, kn, kn, proj, proj)


def _attn_mix(y_gla, y_att, ts):
    S, W = y_att.shape

    def body(yg, ya, mixed_ref, mixed_t_ref):
        y = ya[...]
        mixed_ref[:, :W] = yg[...]
        mixed_ref[:, W:] = y.astype(BF16)
        mixed_t_ref[:W, :] = yg[...].astype(F32).T.astype(BF16)
        mixed_t_ref[W:, :] = y.T.astype(BF16)

    spec = pl.BlockSpec((ts, W), lambda i: (i, 0))
    return pl.pallas_call(
        body, grid=(S // ts,), in_specs=[spec] * 2,
        out_specs=[pl.BlockSpec((ts, 2 * W), lambda i: (i, 0)), _col_spec(2 * W, ts)],
        out_shape=[jax.ShapeDtypeStruct((S, 2 * W), BF16), jax.ShapeDtypeStruct((2 * W, S), BF16)],
        compiler_params=_cp("parallel"), name="attn_mix")(y_gla, y_att)


def _attn_dq(qn, kn, proj, dy, lse, delta):
    S, W = qn.shape
    nb, specs = _attn_specs(S)

    def body(q_ref, kp_ref, kc_ref, vp_ref, vc_ref, dy_ref, l_ref, de_ref, dq_ref):
        hp, n = pl.program_id(0), pl.program_id(1)
        lo = lax.broadcasted_iota(jnp.int32, (1, LANE), 1) < ATTN_HD
        for b, d in enumerate(DILATIONS):
            _attn_dq_branch(b, d, _attn_bias(d, hp, n == 0), lo, q_ref, kp_ref, kc_ref, vp_ref, vc_ref, dy_ref, l_ref, de_ref, dq_ref)

    cur, prev, _ = specs()
    vcur, vprev, _ = specs(O_AV // LANE)
    return pl.pallas_call(
        body, grid=(W // LANE, nb), in_specs=[cur, prev, cur, vprev, vcur, cur, cur, cur], out_specs=cur,
        out_shape=jax.ShapeDtypeStruct((S, W), F32),
        compiler_params=_cp("parallel", "arbitrary"), name="attn_dq")(qn, kn, kn, proj, proj, dy, lse, delta)


def _attn_dq_branch(b, d, biases, lo, q_ref, kp_ref, kc_ref, vp_ref, vc_ref, dy_ref, l_ref, de_ref, dq_ref):
    full, edge = biases

    def sub(g, r):
        rows, before = _attn_rows(d, g, r), _attn_rows(d, g - 1, r)
        kb_ref, vb_ref = (kp_ref, vp_ref) if g == 0 else (kc_ref, vc_ref)
        bias = edge if g == 0 else full
        qv, dyv = q_ref[rows, :].astype(BF16), dy_ref[rows, :]
        lv, dev = l_ref[rows, :], de_ref[rows, :]
        kv = jnp.concatenate([kb_ref[before, :], kc_ref[rows, :]], axis=0).astype(BF16)
        vv = jnp.concatenate([vb_ref[before, :], vc_ref[rows, :]], axis=0).astype(BF16)
        outs = []
        for h in range(2):
            sel = lo == (h == 0)
            qm = jnp.where(sel, qv, jnp.zeros_like(qv))
            dym = jnp.where(sel, dyv, 0.0).astype(BF16)
            lse_h = lv[:, h * ATTN_HD:h * ATTN_HD + 1]
            del_h = dev[:, h * ATTN_HD:h * ATTN_HD + 1]
            p = jnp.exp(_dot(qm, kv, NT) + bias[h] - lse_h)
            ds = p * (_dot(dym, vv, NT) - del_h)
            outs.append(_dot(ds.astype(BF16), kv, NN) * (ATTN_HD ** -0.5))
        dq = jnp.where(lo, outs[0], outs[1])
        dq_ref[rows, :] = dq if b == 0 else dq_ref[rows, :] + dq

    _for_blocks(d, ATTN_TILE // (d * ATTN_BLOCK), sub)


def _attn_dkv(qn, kn, proj, dy, lse, delta):
    S, W = qn.shape
    nb, specs = _attn_specs(S)

    def body(k_ref, v_ref, qc_ref, qn_ref, dyc_ref, dyn_ref, lc_ref, ln_ref, dec_ref, den_ref, dk_ref, dv_ref):
        hp, n = pl.program_id(0), pl.program_id(1)
        lo = lax.broadcasted_iota(jnp.int32, (1, LANE), 1) < ATTN_HD
        cur_refs, next_refs = (qc_ref, dyc_ref, lc_ref, dec_ref), (qn_ref, dyn_ref, ln_ref, den_ref)
        for b, d in enumerate(DILATIONS):
            _attn_dkv_branch(b, d, _attn_bias_t(d, hp, n + 1 < nb), lo, k_ref, v_ref, cur_refs, next_refs, dk_ref, dv_ref)

    cur, _, nxt = specs()
    vcur, _, _ = specs(O_AV // LANE)
    return pl.pallas_call(
        body, grid=(W // LANE, nb), in_specs=[cur, vcur, cur, nxt, cur, nxt, cur, nxt, cur, nxt], out_specs=[cur, cur],
        out_shape=[jax.ShapeDtypeStruct((S, W), F32)] * 2,
        compiler_params=_cp("parallel", "arbitrary"), name="attn_dkv")(
            kn, proj, qn, qn, dy, dy, lse, lse, delta, delta)


def _attn_dkv_branch(b, d, biases, lo, k_ref, v_ref, cur_refs, next_refs, dk_ref, dv_ref):
    B = ATTN_BLOCK
    own, inner, outer = biases
    G = ATTN_TILE // (d * B)

    def sub(g, r):
        rows = _attn_rows(d, g, r)
        kv, vv = k_ref[rows, :].astype(BF16), v_ref[rows, :].astype(BF16)
        dk = jnp.zeros((B, LANE), F32)
        dv = jnp.zeros((B, LANE), F32)
        inside = g + 1 < G
        after = _attn_rows(d, g + 1 if inside else 0, r)
        for bias, qrows, (q_ref, dy_ref, l_ref, de_ref) in (
                (own, rows, cur_refs), (inner if inside else outer, after, cur_refs if inside else next_refs)):
            qv, dyv = q_ref[qrows, :].astype(BF16), dy_ref[qrows, :]
            lt, det = l_ref[qrows, :].T, de_ref[qrows, :].T
            for h in range(2):
                sel = lo == (h == 0)
                qm = jnp.where(sel, qv, jnp.zeros_like(qv))
                dym = jnp.where(sel, dyv, 0.0).astype(BF16)
                lse_h = lt[h * ATTN_HD:h * ATTN_HD + 1, :]
                del_h = det[h * ATTN_HD:h * ATTN_HD + 1, :]
                pt = jnp.exp(_dot(kv, qm, NT) + bias[h] - lse_h)
                dv = dv + _dot(pt.astype(BF16), dym, NN)
                dst = pt * (_dot(vv, dym, NT) - del_h)
                dk = dk + _dot(dst.astype(BF16), qm, NN)
        dk_ref[rows, :] = dk if b == 0 else dk_ref[rows, :] + dk
        dv_ref[rows, :] = dv if b == 0 else dv_ref[rows, :] + dv

    _for_blocks(d, G, sub)


def _attn_post(dq, dk, dv, proj, qg, kg, ts):
    S = proj.shape[0]
    W = ATTN_DIM

    def body(dq_ref, dk_ref, dv_ref, aq_ref, ak_ref, qg_ref, kg_ref, daq_ref, dak_ref, dav_ref, gg_ref):
        i = pl.program_id(0)
        seg = _seg_matrix(W, ATTN_HD, 1.0 / ATTN_HD)
        gsums = []
        for d_ref, x_ref, g_ref, o_ref in ((dq_ref, aq_ref, qg_ref, daq_ref), (dk_ref, ak_ref, kg_ref, dak_ref)):
            dy = d_ref[...]
            xv = x_ref[...]
            r = lax.rsqrt(_seg_sum(xv * xv, seg) + EPS)
            xh = xv * r
            dxh = dy * g_ref[...]
            o_ref[...] = (r * (dxh - xh * _seg_sum(dxh * xh, seg))).astype(BF16)
            gsums.append(jnp.sum(dy * xh, axis=0, keepdims=True))
        dav_ref[...] = dv_ref[...].astype(BF16)
        _accumulate(gg_ref, _rows8(gsums, W), i)

    row = pl.BlockSpec((ts, W), lambda i: (i, 0))
    blk = lambda off: pl.BlockSpec((ts, W), lambda i: (i, off // W))
    vec = pl.BlockSpec((1, W), lambda i: (0, 0))
    return pl.pallas_call(
        body, grid=(S // ts,), in_specs=[row] * 3 + [blk(O_AQ), blk(O_AK), vec, vec],
        out_specs=[row, row, row, pl.BlockSpec((8, W), lambda i: (0, 0))],
        out_shape=[jax.ShapeDtypeStruct((S, W), BF16)] * 3 + [jax.ShapeDtypeStruct((8, W), F32)],
        compiler_params=_cp("arbitrary"), name="attn_post")(dq, dk, dv, proj, proj, qg, kg)


def _shift_down(cur, halo, n):
    return pltpu.roll(jnp.concatenate([halo, cur], axis=0), n, 0)[8:]


def _shift_up(cur, halo, n):
    ts = cur.shape[0]
    return pltpu.roll(jnp.concatenate([cur, halo], axis=0), ts + 8 - n, 0)[:ts]


def _conv(cur, halo, w, b):
    return b + w[0:1, :] * _shift_down(cur, halo, 2) + w[1:2, :] * _shift_down(cur, halo, 1) + w[2:3, :] * cur


def _mm_up_swiglu(h2, w_up, conv_w8, conv_b, tm, tc, ride=None):
    S, D = h2.shape
    F = w_up.shape[1] // 2
    nc = F // tc
    grid = (S // tm, nc)
    ride_arrays, ride_gather = ride if ride else ([], [])
    nr = len(ride_arrays)

    def body(h_ref, bg_ref, bv_ref, wg_ref, wv_ref, cg_ref, cv_ref, u0_ref, a_ref, at_ref, halo):
        i, j = pl.program_id(0), pl.program_id(1)
        hv = h_ref[...]
        acts = []
        for h, (b_ref, w_ref, c_ref) in enumerate(((bg_ref, wg_ref, cg_ref), (bv_ref, wv_ref, cv_ref))):
            u = _dot(hv, b_ref[...], NN)
            u0_ref[h] = u
            acts.append(_conv(u, jnp.where(i == 0, 0.0, halo[j, h]), w_ref[...], c_ref[...]))
            halo[j, h] = u[tm - 8:, :]
        g, v = acts
        a = g * _sigmoid(g) * v
        a_ref[...] = a.astype(BF16)
        at_ref[...] = a.T.astype(BF16)

    wcol = lambda rows, off: pl.BlockSpec((rows, tc), lambda i, j: (0, j + off))
    outs = pl.pallas_call(
        _riding(body, 7, 3, ride_gather, grid), grid=grid,
        in_specs=[pl.BlockSpec((tm, D), lambda i, j: (i, 0)), wcol(D, 0), wcol(D, nc), wcol(8, 0), wcol(8, nc), wcol(1, 0), wcol(1, nc)]
        + [HBM_SPEC] * nr,
        out_specs=[pl.BlockSpec((2, tm, tc), lambda i, j: (0, i, j)), pl.BlockSpec((tm, tc), lambda i, j: (i, j)),
                   pl.BlockSpec((tc, tm), lambda i, j: (j, i))] + [HBM_SPEC] * nr,
        out_shape=[jax.ShapeDtypeStruct((2, S, F), F32), jax.ShapeDtypeStruct((S, F), BF16), jax.ShapeDtypeStruct((F, S), BF16)]
        + _exchange_shapes(ride_arrays, ride_gather),
        scratch_shapes=[pltpu.VMEM((nc, 2, 8, tc), F32)] + (_exchange_sems(nr) if nr else []),
        compiler_params=_cp("arbitrary", "arbitrary"), name="mm_up")(
            h2, w_up, w_up, conv_w8, conv_w8, conv_b, conv_b, *ride_arrays)
    return outs[0], outs[1], outs[2], outs[3:]


def _mm_da_du0(dt2, w_down, u0, conv_w8, conv_b, tm, tc, ride=None):
    _, S, F = u0.shape
    D = dt2.shape[1]
    hb = tm // 8
    nrow = S // tm
    grid = (nrow,)
    ride_arrays, ride_gather = ride if ride else ([], [])
    nr = len(ride_arrays)

    def body(dt_ref, wd_ref, ug_ref, ugh_ref, uv_ref, uvh_ref, w_ref, b_ref, o_ref, sg_ref, sv_ref, following):
        i = pl.program_id(0)
        at_start, at_end = i == nrow - 1, i == 0
        dt = dt_ref[...]
        for c in range(F // tc):
            sums = []
            halves = []
            for h, (u_ref, h_ref) in enumerate(((ug_ref, ugh_ref), (uv_ref, uvh_ref))):
                cols = slice(h * F + c * tc, h * F + (c + 1) * tc)
                u, halo, w = u_ref[:, c * tc:(c + 1) * tc], jnp.where(at_start, 0.0, h_ref[:, c * tc:(c + 1) * tc]), w_ref[:, cols]
                s2, s1 = _shift_down(u, halo, 2), _shift_down(u, halo, 1)
                halves.append((b_ref[:, cols] + w[0:1, :] * s2 + w[1:2, :] * s1 + w[2:3, :] * u, s2, s1, u, w, cols))
            g, v = halves[0][0], halves[1][0]
            dav = _dot(dt, wd_ref[c * tc:(c + 1) * tc, :], NT)
            sig = _sigmoid(g)
            dus = (dav * v * (sig * (1.0 + g * (1.0 - sig))), dav * (g * sig))
            for h, du in enumerate(dus):
                _, s2, s1, u, w, cols = halves[h]
                after = jnp.where(at_end, 0.0, following[h, :, c * tc:(c + 1) * tc])
                o_ref[:, cols] = (w[2:3, :] * du + w[1:2, :] * _shift_up(du, after, 1) + w[0:1, :] * _shift_up(du, after, 2)).astype(BF16)
                following[h, :, c * tc:(c + 1) * tc] = du[0:8, :]
                sums.append(_rows8([jnp.sum(du * s2, axis=0, keepdims=True), jnp.sum(du * s1, axis=0, keepdims=True),
                                    jnp.sum(du * u, axis=0, keepdims=True), jnp.sum(du, axis=0, keepdims=True)], tc))
            for sums_ref, part in zip((sg_ref, sv_ref), sums):
                @pl.when(i == 0)
                def _(sums_ref=sums_ref, part=part, c=c):
                    sums_ref[:, c * tc:(c + 1) * tc] = part

                @pl.when(i > 0)
                def _(sums_ref=sums_ref, part=part, c=c):
                    sums_ref[:, c * tc:(c + 1) * tc] += part

    rev = lambda i: nrow - 1 - i
    main = lambda h: pl.BlockSpec((None, tm, F), lambda i: (h, rev(i), 0))
    halo = lambda h: pl.BlockSpec((None, 8, F), lambda i: (h, jnp.maximum(rev(i) * hb - 1, 0), 0))
    whole = lambda a: pl.BlockSpec(a.shape, lambda i: (0,) * a.ndim, pipeline_mode=pl.Buffered(1))
    sums_spec = pl.BlockSpec((8, F), lambda i: (0, 0))
    outs = pl.pallas_call(
        _riding(body, 8, 3, ride_gather, grid), grid=grid,
        in_specs=[pl.BlockSpec((tm, D), lambda i: (rev(i), 0)), whole(w_down), main(0), halo(0), main(1), halo(1),
                  whole(conv_w8), whole(conv_b)] + [HBM_SPEC] * nr,
        out_specs=[pl.BlockSpec((tm, 2 * F), lambda i: (rev(i), 0)), sums_spec, sums_spec] + [HBM_SPEC] * nr,
        out_shape=[jax.ShapeDtypeStruct((S, 2 * F), BF16), jax.ShapeDtypeStruct((8, F), F32), jax.ShapeDtypeStruct((8, F), F32)]
        + _exchange_shapes(ride_arrays, ride_gather),
        scratch_shapes=[pltpu.VMEM((2, 8, F), F32)] + (_exchange_sems(nr) if nr else []),
        compiler_params=_cp("arbitrary"), name="mm_da")(
            dt2, w_down, u0, u0, u0, u0, conv_w8, conv_b, *ride_arrays)
    return outs[0], outs[1], outs[2], outs[3:]


def _adamw(w, g, m, v, name, slots=False):
    shape = w.shape
    view = (math.prod(shape[:-1]), shape[-1])
    R, C = view
    limit = SUM_BLOCK_ELEMS // 2 if slots else SUM_BLOCK_ELEMS
    fits = [t for t in range(16, R + 1, 16) if R % t == 0 and t * C <= limit]
    tr = max(fits) if fits else R

    def body(w_ref, g_ref, m_ref, v_ref, *outs):
        if slots:
            gv = g_ref[0].astype(F32)
            for s in range(1, N_DEV):
                gv = gv + g_ref[s].astype(F32)
            outs[0][...] = gv
        else:
            gv = g_ref[...]
        d_ref, nm_ref, nv_ref = outs[-3:]
        nm = ADAM_B1 * m_ref[...] + (1.0 - ADAM_B1) * gv
        nv = ADAM_B2 * v_ref[...] + (1.0 - ADAM_B2) * (gv * gv)
        m_hat = nm / (1.0 - ADAM_B1 ** ADAM_STEP)
        v_hat = nv / (1.0 - ADAM_B2 ** ADAM_STEP)
        d_ref[...] = -ADAM_LR * (m_hat / (jnp.sqrt(v_hat) + ADAM_EPS) + ADAM_WD * w_ref[...])
        nm_ref[...] = nm
        nv_ref[...] = nv

    spec = pl.BlockSpec((tr, C), lambda i: (i, 0))
    g_spec = pl.BlockSpec((N_DEV, tr, C), lambda i: (0, i, 0)) if slots else spec
    n_out = 4 if slots else 3
    outs = pl.pallas_call(
        body, grid=(R // tr,), in_specs=[spec, g_spec, spec, spec], out_specs=[spec] * n_out,
        out_shape=[jax.ShapeDtypeStruct(view, F32)] * n_out, compiler_params=_cp("parallel"), name=name)(
            w.reshape(view), g if slots else g.reshape(view), m.reshape(view), v.reshape(view))
    outs = [o.reshape(shape) for o in outs]
    return outs if slots else [g.reshape(shape)] + outs


def _pad_rows8(a):
    return jnp.concatenate([a, jnp.zeros((8 - a.shape[0], a.shape[1]), a.dtype)], axis=0)


def _local_step(x, target, mod, n1g, w_in_s, conv_w_s, wg_s, bg, gng, qng, kng, w_out_s, n2g, w_up_s, conv_b, w_down_s):
    S, D = x.shape
    F = w_down_s.shape[0] * N_DEV
    cw_c, wg_c = conv_w_s.shape[1], wg_s.shape[1]
    ts = min(512, S)
    sh1, sc1, g1, sh2, sc2, g2 = [mod[i:i + 1] for i in range(6)]
    qg_t, kg_t = jnp.tile(qng, (1, ATTN_HEADS)), jnp.tile(kng, (1, ATTN_HEADS))

    small = jnp.concatenate([conv_w_s.reshape(1, -1), wg_s.reshape(1, -1)], axis=1)
    n_small = small.shape[1]
    small = jnp.pad(small, ((0, 0), (0, -n_small % LANE)))
    h1, h1_t, (g_in, g_small) = _rms_mod(x, n1g, sc1, sh1, ts, "rms_mod1", ride=([w_in_s, small], [VIA_SIBLING, True]))
    w_in_full = _cols_from_blocks(g_in)
    w_in_p = jnp.concatenate([w_in_full[:, :GLR_SRC], w_in_full[:, GLR_SRC + GLA_RANK:],
                              w_in_full[:, GLR_SRC:GLR_SRC + GLA_RANK], jnp.zeros((D, PROJ_W - O_GLR - GLA_RANK), BF16)], axis=1)
    g_small = g_small.reshape(N_DEV, -1)
    conv_w8 = _pad_rows8(jnp.stack([g_small[:, t * cw_c:(t + 1) * cw_c].reshape(-1) for t in range(3)]))
    wg_full = _cols_from_blocks(g_small[:, 3 * cw_c:n_small].reshape(N_DEV, GLA_RANK, wg_c))
    wg_p = jnp.concatenate([wg_full, jnp.zeros((LANE - GLA_RANK, wg_full.shape[1]), F32)], axis=0)
    (proj, la, qn, kn), (g_out,) = _mm_in(h1, w_in_p, wg_p, bg, qg_t, kg_t, ts, ride=([w_out_s], [True]))
    w_out = g_out.reshape(-1, D)
    o_gla, states, (g_up,) = _gla_fwd(proj, la, 512, ride=([w_up_s], [True]))
    w_up = _cols_from_blocks(g_up)
    y_gla = _gla_out(o_gla, proj, gng, ts)
    y_att, lse = _attn_fwd(qn, kn, proj)
    mixed, mixed_t = _attn_mix(y_gla, y_att, ts)
    t1, x2, h2, h2_t = _mm_resid_rms_mod(mixed, w_out, x, g1, n2g, sc2, sh2, ts, "mm_out")
    tc = 1408 if F % 1408 == 0 else F
    u0, a, a_t, (g_down,) = _mm_up_swiglu(h2, w_up, conv_w8, conv_b, ts, tc, ride=([w_down_s], [True]))
    w_down = g_down.reshape(F, D)
    dx3, dt2, sums3 = _mm_loss_resid(a, w_down, x2, g2, target, ts, "mm_down")
    loss_row, dg2 = sums3[0:1], sums3[1:2]

    g_w_down = _mm(a_t, dt2, NN, 1408, 1024, 2048, F32, "mm_gw_down")
    du0, sums_g, sums_v, (r_down,) = _mm_da_du0(dt2, w_down, u0, conv_w8, conv_b, min(256, S), tc,
                                                ride=([g_w_down.reshape(N_DEV, -1, D)], [False]))
    g_conv_w = jnp.concatenate([sums_g[0:3], sums_v[0:3]], axis=1)
    g_conv_b = jnp.concatenate([sums_g[3:4], sums_v[3:4]], axis=1)
    g_w_up = _mm(h2_t, du0, NN, 512, 2816, 2048, F32, "mm_gw_up")
    (dx2, sums2, dt1), _ = _mm_rms_mod_bwd(du0, w_up, x2, dx3, n2g, sc2, ts, "mm_dh2", t_prev=t1, g_prev=g1)
    dsh2, dsc2, g_n2g, dg1 = sums2[0:1], sums2[1:2], sums2[2:3], sums2[3:4]
    g_w_out = _mm(mixed_t, dt1, NN, 1024, 1024, 2048, F32, "mm_gw_out")
    do_gla, dgr, gng_sums, dy_att, delta = _mm_mixed_bwd(dt1, w_out, o_gla, proj, gng, y_att, ts)
    dgq, dgk, dgv, dla, (r_up, r_out) = _gla_bwd(
        proj, la, do_gla, states, 512, ride=([_col_blocks(g_w_up), g_w_out.reshape(N_DEV, -1, D)], [False, False]))
    dglr, g_wg_p, gb_sums = _gate_bwd(dla, la, proj, wg_p, ts)
    dqn = _attn_dq(qn, kn, proj, dy_att, lse, delta)
    dkn, dvn = _attn_dkv(qn, kn, proj, dy_att, lse, delta)
    daq, dak, dav, qk_sums = _attn_post(dqn, dkn, dvn, proj, qg_t, kg_t, ts)
    dproj = jnp.concatenate([dgq, dgk, dgv, dgr, daq, dak, dav, dglr, jnp.zeros((S, PROJ_W - O_GLR - LANE), BF16)], axis=1)
    g_w_in_p = _mm(h1_t, dproj, NN, 512, PROJ_W, 1024, BF16, "mm_gw_in")
    g_w_in = jnp.concatenate([g_w_in_p[:, :GLR_SRC], g_w_in_p[:, O_GLR:O_GLR + GLA_RANK], g_w_in_p[:, GLR_SRC:O_GLR]], axis=1)
    (dx, sums1), (r_in,) = _mm_rms_mod_bwd(dproj, w_in_p, x, dx2, n1g, sc1, ts, "mm_dh1",
                                           ride=([_col_blocks(g_w_in)], [False]))
    dsh1, dsc1, g_n1g = sums1[0:1], sums1[1:2], sums1[2:3]

    dmod = jnp.concatenate([dsh1, dsc1, dg1, dsh2, dsc2, dg2], axis=1)
    grads = dict(n1g=g_n1g, w_in=r_in, wg=g_wg_p[:GLA_RANK], bg=gb_sums[0:1], gng=gng_sums[0:1],
                 qng_lanes=qk_sums[0:1], kng_lanes=qk_sums[1:2], w_out=r_out, n2g=g_n2g, w_up=r_up,
                 conv_w=g_conv_w, conv_b=g_conv_b, w_down=r_down)
    return loss_row, dx, dmod, grads


def _col_blocks(a):
    R, W = a.shape
    return a.reshape(R, N_DEV, W // N_DEV).transpose(1, 0, 2)


def _cols_from_blocks(a):
    n, R, C = a.shape
    return a.transpose(1, 0, 2).reshape(R, n * C)


def kernel(x, c, w_ada, b_ada, norm1_g, w_in, gla_w_gate, gla_b_gate, gla_norm_g, q_norm_g, k_norm_g, w_out, norm2_g, w_up, conv_w, conv_b, w_down, loss_target, m_w_ada, m_b_ada, m_norm1_g, m_w_in, m_gla_w_gate, m_gla_b_gate, m_gla_norm_g, m_q_norm_g, m_k_norm_g, m_w_out, m_norm2_g, m_w_up, m_conv_w, m_conv_b, m_w_down, v_w_ada, v_b_ada, v_norm1_g, v_w_in, v_gla_w_gate, v_gla_b_gate, v_gla_norm_g, v_q_norm_g, v_k_norm_g, v_w_out, v_norm2_g, v_w_up, v_conv_w, v_conv_b, v_w_down):
    axes = ("x", "y", "c")
    me = 4 * lax.axis_index("x") + 2 * lax.axis_index("y") + lax.axis_index("c")
    D = x.shape[2]
    x2d, tgt2d = x[0], loss_target[0]
    w_in_s, w_out_s, w_up_s, w_down_s, w_ada_s = w_in[0], w_out[0], w_up[0], w_down[0], w_ada[0]
    conv_w_s, wg_s = conv_w[0], gla_w_gate[0]
    ada_c, wg_c, cw_c = w_ada_s.shape[1], wg_s.shape[1], conv_w_s.shape[1]

    g_c, = _exchange([c], [True], "gather_c")
    c_all = g_c.reshape(N_DEV, D)

    b_shard = lax.dynamic_slice(b_ada, (0, me * ada_c), (1, ada_c))
    mod_part = _ada_fwd(c_all, w_ada_s, b_shard)
    mod_recv, = _exchange([mod_part.reshape(N_DEV, 1, ada_c)], [False], "exchange_mod")
    mod = mod_recv.reshape(6, D)

    loss_row, dx, dmod, gr = _local_step(
        x2d, tgt2d, mod, norm1_g, w_in_s.astype(BF16), conv_w_s, wg_s, gla_b_gate, gla_norm_g, q_norm_g, k_norm_g,
        w_out_s.astype(BF16), norm2_g, w_up_s.astype(BF16), conv_b, w_down_s.astype(BF16))
    loss = lax.psum(0.5 / D * jnp.sum(loss_row), axes)

    parts = [dmod, gr["n1g"], gr["bg"], gr["gng"], gr["qng_lanes"], gr["kng_lanes"], gr["n2g"], gr["conv_b"],
             gr["wg"].reshape(1, -1), gr["conv_w"].reshape(1, -1)]
    sizes = [p.shape[1] for p in parts]
    packed = jnp.concatenate(parts, axis=1)
    packed = jnp.pad(packed, ((0, 0), (0, -packed.shape[1] % (8 * LANE))))
    gathered, = _exchange([packed.reshape(8, -1)], [True], "gather_small_grads")
    gathered = gathered.reshape(N_DEV, -1)
    total = _sum_slots(gathered.reshape(N_DEV, 8, -1), "sum_small_grads").reshape(1, -1)
    offs = [0]
    for s_ in sizes:
        offs.append(offs[-1] + s_)
    t_dmod, t_n1g, t_bg, t_gng, t_qng, t_kng, t_n2g, t_conv_b, t_wg, t_conv_w = [
        total[:, offs[i]:offs[i + 1]] for i in range(len(sizes))]
    g_b_ada = t_dmod
    g_qng = t_qng.reshape(ATTN_HEADS, ATTN_HD).sum(axis=0, keepdims=True)
    g_kng = t_kng.reshape(ATTN_HEADS, ATTN_HD).sum(axis=0, keepdims=True)
    g_wg = lax.dynamic_slice(t_wg.reshape(GLA_RANK, -1), (0, me * wg_c), (GLA_RANK, wg_c))
    g_conv_w = lax.dynamic_slice(t_conv_w.reshape(3, -1), (0, me * cw_c), (3, cw_c))
    dmod_shard = lax.dynamic_slice(gathered[:, :6 * D], (0, me * ada_c), (N_DEV, ada_c))
    g_w_ada = _ada_bwd(c_all, dmod_shard)

    g_w_in, g_w_out, g_w_up, g_w_down = gr["w_in"], gr["w_out"], gr["w_up"], gr["w_down"]
    in_slots = {"w_in", "w_out", "w_up", "w_down"}
    names = ["w_ada", "b_ada", "norm1_g", "w_in", "gla_w_gate", "gla_b_gate", "gla_norm_g", "q_norm_g", "k_norm_g",
             "w_out", "norm2_g", "w_up", "conv_w", "conv_b", "w_down"]
    ws = [w_ada, b_ada, norm1_g, w_in, gla_w_gate, gla_b_gate, gla_norm_g, q_norm_g, k_norm_g, w_out, norm2_g, w_up, conv_w, conv_b, w_down]
    ms = [m_w_ada, m_b_ada, m_norm1_g, m_w_in, m_gla_w_gate, m_gla_b_gate, m_gla_norm_g, m_q_norm_g, m_k_norm_g, m_w_out, m_norm2_g, m_w_up, m_conv_w, m_conv_b, m_w_down]
    vs = [v_w_ada, v_b_ada, v_norm1_g, v_w_in, v_gla_w_gate, v_gla_b_gate, v_gla_norm_g, v_q_norm_g, v_k_norm_g, v_w_out, v_norm2_g, v_w_up, v_conv_w, v_conv_b, v_w_down]
    gs = [g_w_ada, g_b_ada, t_n1g, g_w_in, g_wg, t_bg, t_gng, g_qng, g_kng, g_w_out, t_n2g, g_w_up, g_conv_w, t_conv_b, g_w_down]
    grads, deltas, new_ms, new_vs = [], [], [], []
    for nm, w, g, m, v in zip(names, ws, gs, ms, vs):
        g_, d_, m_, v_ = _adamw(w, g, m, v, "adamw_" + nm, slots=nm in in_slots)
        grads.append(g_)
        deltas.append(d_)
        new_ms.append(m_)
        new_vs.append(v_)
    return (loss, dx.reshape(x.shape), *grads, *deltas, *new_ms, *new_vs)
```

```python
import functools
import math

import jax
import jax.numpy as jnp
from jax import lax
from jax.experimental import pallas as pl
from jax.experimental.pallas import tpu as pltpu

F32, BF16 = jnp.float32, jnp.bfloat16
HI = lax.Precision.HIGHEST
EPS = 1e-6
NEG = -1e30

N_DEV = 8
GLA_HEADS, GLA_DK, GLA_DV, GLA_RANK, GLA_TAU, GLA_CHUNK = 4, 64, 128, 16, 16.0, 64
ATTN_HEADS, ATTN_HD, ATTN_BLOCK = 8, 64, 128
DILATIONS = (1, 4, 16)
GLA_QK, GLA_V, ATTN_DIM = GLA_HEADS * GLA_DK, GLA_HEADS * GLA_DV, ATTN_HEADS * ATTN_HD
O_GQ, O_GK, O_GV, O_GR, O_AQ, O_AK, O_AV, O_GLR = 0, 256, 512, 1024, 1536, 2048, 2560, 3072
PROJ_W = 3328
LANE = 128
GLR_SRC = 2 * GLA_QK + 2 * GLA_V

ADAM_LR, ADAM_B1, ADAM_B2, ADAM_EPS, ADAM_WD, ADAM_STEP = 0.001, 0.9, 0.999, 1e-08, 0.01, 10

VMEM_LIMIT = 56 * 1024 * 1024
SUM_BLOCK_ELEMS = 256 * 1024


def _cp(*sem):
    return pltpu.CompilerParams(dimension_semantics=sem, vmem_limit_bytes=VMEM_LIMIT)


def _dot(a, b, dims, precision=None):
    return lax.dot_general(a, b, (dims, ((), ())), preferred_element_type=F32, precision=precision)


NN, NT, TN = ((1,), (0,)), ((1,), (1,)), ((0,), (0,))


def _sigmoid(z):
    return 1.0 / (1.0 + jnp.exp(-z))


HBM_SPEC = pl.BlockSpec(memory_space=pltpu.HBM)


def _exchange_shapes(arrays, gather):
    return [jax.ShapeDtypeStruct((N_DEV,) + (a.shape if g else a.shape[1:]), a.dtype) for a, g in zip(arrays, gather)]


def _exchange_sems(n):
    return [pltpu.SemaphoreType.DMA((n * (N_DEV - 1),)), pltpu.SemaphoreType.DMA((n * (N_DEV - 1),)), pltpu.SemaphoreType.DMA((n,))]


VIA_SIBLING = "via sibling"


def _exchange_plan(ins, outs, gather, send_sems, recv_sems, local_sems):
    x, y, c = lax.axis_index("x"), lax.axis_index("y"), lax.axis_index("c")
    me = 4 * x + 2 * y + c
    start, relays, waits = [], [], []
    for a in range(len(ins)):
        if gather[a] == VIA_SIBLING:
            def copy(i, block, to, src=None, a=a):
                slot = outs[a].at[4 * block[0] + 2 * block[1] + block[2]]
                return pltpu.make_async_remote_copy(
                    src_ref=slot if src is None else src, dst_ref=slot, send_sem=send_sems.at[a * (N_DEV - 1) + i],
                    recv_sem=recv_sems.at[a * (N_DEV - 1) + i], device_id=to, device_id_type=pl.DeviceIdType.MESH)

            chips = [(1 - x, y), (x, 1 - y), (1 - x, 1 - y)]
            first = [copy(0, (x, y, c), (x, y, 1 - c), src=ins[a])]
            first += [copy(1 + j, (x, y, c), (*chip, c), src=ins[a]) for j, chip in enumerate(chips)]
            passed = [copy(4 + j, (*chip, c), (x, y, 1 - c)) for j, chip in enumerate(chips)]
            start += first
            relays += [(copy(1 + j, (*chip, c), (x, y, c)).wait_recv, passed[j]) for j, chip in enumerate(chips)]
            waits += [copy(0, (x, y, 1 - c), (x, y, c)).wait_recv]
            waits += [copy(4 + j, (*chip, 1 - c), (x, y, c)).wait_recv for j, chip in enumerate(chips)]
            waits += [cp.wait_send for cp in first + passed]
        else:
            for p in range(1, N_DEV):
                px, py, pc = x ^ (p >> 2), y ^ ((p >> 1) & 1), c ^ (p & 1)
                peer = 4 * px + 2 * py + pc
                k = a * (N_DEV - 1) + p - 1
                cp = pltpu.make_async_remote_copy(
                    src_ref=ins[a] if gather[a] else ins[a].at[peer], dst_ref=outs[a].at[me],
                    send_sem=send_sems.at[k], recv_sem=recv_sems.at[k],
                    device_id=(px, py, pc), device_id_type=pl.DeviceIdType.MESH)
                start.append(cp)
                waits.append(cp.wait)
        own = pltpu.make_async_copy(ins[a] if gather[a] else ins[a].at[me], outs[a].at[me], local_sems.at[a])
        start.append(own)
        waits.append(own.wait)
    return start, relays, waits


def _exchange_finish(relays, waits):
    for arrived, pass_on in relays:
        arrived()
        pass_on.start()
    for wait in waits:
        wait()


def _riding(body, n_in, n_out, gather, grid):
    nr = len(gather)
    if not nr:
        return body

    def wrapped(*refs):
        ins, r_ins = refs[:n_in], refs[n_in:n_in + nr]
        outs, r_outs = refs[n_in + nr:n_in + nr + n_out], refs[n_in + nr + n_out:n_in + 2 * nr + n_out]
        scratch = refs[n_in + 2 * nr + n_out:]
        first = last = None
        for t, steps in enumerate(grid):
            pid = pl.program_id(t)
            first = (pid == 0) if first is None else first & (pid == 0)
            last = (pid == steps - 1) if last is None else last & (pid == steps - 1)
        start, relays, waits = _exchange_plan(r_ins, r_outs, gather, *scratch[-3:])

        @pl.when(first)
        def _():
            for cp in start:
                cp.start()

        body(*ins, *outs, *scratch[:-3])

        @pl.when(last)
        def _():
            _exchange_finish(relays, waits)

    return wrapped


def _exchange(arrays, gather, name):
    n = len(arrays)

    def body(*refs):
        start, relays, waits = _exchange_plan(refs[:n], refs[n:2 * n], gather, *refs[2 * n:])
        for cp in start:
            cp.start()
        _exchange_finish(relays, waits)

    return pl.pallas_call(
        body, out_shape=_exchange_shapes(arrays, gather), in_specs=[HBM_SPEC] * n, out_specs=[HBM_SPEC] * n,
        scratch_shapes=_exchange_sems(n), name=name)(*arrays)


def _sum_slots(x, name):
    _, R, C = x.shape
    tr = max(t for t in range(8, min(SUM_BLOCK_ELEMS // C, R) + 1, 8) if R % t == 0)

    def body(x_ref, o_ref):
        acc = x_ref[0].astype(F32)
        for s in range(1, N_DEV):
            acc = acc + x_ref[s].astype(F32)
        o_ref[...] = acc

    return pl.pallas_call(
        body, grid=(R // tr,), in_specs=[pl.BlockSpec((N_DEV, tr, C), lambda i: (0, i, 0))],
        out_specs=pl.BlockSpec((tr, C), lambda i: (i, 0)), out_shape=jax.ShapeDtypeStruct((R, C), F32),
        compiler_params=_cp("parallel"), name=name)(x)


def _mm(a, b, mode, tm, tn, tk, out_dtype, name, ride=None):
    if mode == NN:
        (M, K), N = a.shape, b.shape[1]
    elif mode == NT:
        (M, K), N = a.shape, b.shape[0]
    else:
        (K, M), N = a.shape, b.shape[1]
    tm, tn, tk = min(tm, M), min(tn, N), min(tk, K)
    assert M % tm == 0 and N % tn == 0 and K % tk == 0, (name, M, N, K, tm, tn, tk)
    nk = K // tk
    if mode == NN:
        a_spec = pl.BlockSpec((tm, tk), lambda i, j, k: (i, k))
        b_spec = pl.BlockSpec((tk, tn), lambda i, j, k: (k, j))
    elif mode == NT:
        a_spec = pl.BlockSpec((tm, tk), lambda i, j, k: (i, k))
        b_spec = pl.BlockSpec((tn, tk), lambda i, j, k: (j, k))
    else:
        a_spec = pl.BlockSpec((tk, tm), lambda i, j, k: (k, i))
        b_spec = pl.BlockSpec((tk, tn), lambda i, j, k: (k, j))

    ride_arrays, ride_gather = ride if ride else ([], [])
    nr = len(ride_arrays)
    grid = (M // tm, N // tn, nk)

    own_acc = nk > 1 and out_dtype != F32

    def body(a_ref, b_ref, o_ref, *acc):
        p = _dot(a_ref[...].astype(BF16), b_ref[...].astype(BF16), mode)
        if nk == 1:
            o_ref[...] = p.astype(out_dtype)
        else:
            acc_ref = acc[0] if own_acc else o_ref
            k = pl.program_id(2)

            @pl.when(k == 0)
            def _():
                acc_ref[...] = p

            @pl.when(k > 0)
            def _():
                acc_ref[...] += p

            if own_acc:
                @pl.when(k == nk - 1)
                def _():
                    o_ref[...] = acc_ref[...].astype(out_dtype)

    outs = pl.pallas_call(
        _riding(body, 2, 1, ride_gather, grid), grid=grid, in_specs=[a_spec, b_spec] + [HBM_SPEC] * nr,
        out_specs=[pl.BlockSpec((tm, tn), lambda i, j, k: (i, j))] + [HBM_SPEC] * nr,
        out_shape=[jax.ShapeDtypeStruct((M, N), out_dtype)] + _exchange_shapes(ride_arrays, ride_gather),
        scratch_shapes=([pltpu.VMEM((tm, tn), F32)] if own_acc else []) + (_exchange_sems(nr) if nr else []),
        compiler_params=_cp(*(("arbitrary",) * 3 if nr else ("parallel", "parallel", "arbitrary"))), name=name)(a, b, *ride_arrays)
    return (outs[0], outs[1:]) if nr else outs[0]


def _ada_fwd(c_all, w_shard, b_shard):
    Nc = w_shard.shape[1]

    def body(c_ref, w_ref, b_ref, o_ref):
        cv = c_ref[...]
        o_ref[...] = _dot(cv * _sigmoid(cv), w_ref[...], NN, HI) + b_ref[...]

    return pl.pallas_call(body, out_shape=jax.ShapeDtypeStruct((N_DEV, Nc), F32), name="ada_fwd",
                          compiler_params=pltpu.CompilerParams(vmem_limit_bytes=VMEM_LIMIT))(c_all, w_shard, b_shard)


def _ada_bwd(c_all, dmod_shard):
    D, Nc = c_all.shape[1], dmod_shard.shape[1]

    def body(c_ref, d_ref, o_ref):
        cv = c_ref[...]
        o_ref[...] = _dot(cv * _sigmoid(cv), d_ref[...], TN, HI)

    return pl.pallas_call(body, out_shape=jax.ShapeDtypeStruct((D, Nc), F32), name="ada_bwd",
                          compiler_params=pltpu.CompilerParams(vmem_limit_bytes=VMEM_LIMIT))(c_all, dmod_shard)


def _row_spec(ts, D):
    return pl.BlockSpec((ts, D), lambda i: (i, 0))


def _vec_spec(D):
    return pl.BlockSpec((1, D), lambda i: (0, 0))


def _col_spec(D, ts):
    return pl.BlockSpec((D, ts), lambda i: (0, i))


def _rms_mod(x, ng, sc, sh, ts, name, ride=None):
    S, D = x.shape
    ride_arrays, ride_gather = ride if ride else ([], [])
    nr = len(ride_arrays)
    grid = (S // ts,)

    def body(x_ref, ng_ref, sc_ref, sh_ref, h_ref, ht_ref):
        xv = x_ref[...]
        r = lax.rsqrt(jnp.mean(xv * xv, axis=-1, keepdims=True) + EPS)
        h = xv * r * ng_ref[...] * (1.0 + sc_ref[...]) + sh_ref[...]
        h_ref[...] = h.astype(BF16)
        ht_ref[...] = h.T.astype(BF16)

    outs = pl.pallas_call(
        _riding(body, 4, 2, ride_gather, grid), grid=grid, in_specs=[_row_spec(ts, D)] + [_vec_spec(D)] * 3 + [HBM_SPEC] * nr,
        out_specs=[_row_spec(ts, D), _col_spec(D, ts)] + [HBM_SPEC] * nr,
        out_shape=[jax.ShapeDtypeStruct((S, D), BF16), jax.ShapeDtypeStruct((D, S), BF16)] + _exchange_shapes(ride_arrays, ride_gather),
        scratch_shapes=_exchange_sems(nr) if nr else [],
        compiler_params=_cp("arbitrary"), name=name)(x, ng, sc, sh, *ride_arrays)
    return outs[0], outs[1], outs[2:]


def _mm_rows(a, b, mode, tm, extras, extra_specs, out_shapes, out_specs, epilogue, name, ride=None):
    M, K = a.shape
    grid = (M // tm,)
    ride_arrays, ride_gather = ride if ride else ([], [])
    nr = len(ride_arrays)

    def body(a_ref, b_ref, *refs):
        epilogue(_dot(a_ref[...].astype(BF16), b_ref[...].astype(BF16), mode), pl.program_id(0), *refs)

    outs = pl.pallas_call(
        _riding(body, 2 + len(extras), len(out_shapes), ride_gather, grid), grid=grid,
        in_specs=[pl.BlockSpec((tm, K), lambda i: (i, 0)), pl.BlockSpec(b.shape, lambda i: (0, 0), pipeline_mode=pl.Buffered(1))]
        + list(extra_specs) + [HBM_SPEC] * nr,
        out_specs=list(out_specs) + [HBM_SPEC] * nr,
        out_shape=list(out_shapes) + _exchange_shapes(ride_arrays, ride_gather),
        scratch_shapes=_exchange_sems(nr) if nr else [],
        compiler_params=_cp("arbitrary"), name=name)(a, b, *extras, *ride_arrays)
    return outs[:len(out_shapes)], outs[len(out_shapes):]


def _accumulate(ref, part, step):
    @pl.when(step == 0)
    def _():
        ref[...] = part

    @pl.when(step > 0)
    def _():
        ref[...] += part


def _rows8(rows, width):
    return jnp.concatenate(rows + [jnp.zeros((8 - len(rows), width), F32)], axis=0)


def _mm_resid_rms_mod(a, w, x, g, ng, sc, sh, tm, name):
    S, D = x.shape

    def epilogue(t, step, x_ref, g_ref, ng_ref, sc_ref, sh_ref, t_ref, x2_ref, h_ref, ht_ref):
        t_ref[...] = t
        xv = x_ref[...] + g_ref[...] * t
        x2_ref[...] = xv
        r = lax.rsqrt(jnp.mean(xv * xv, axis=-1, keepdims=True) + EPS)
        h = xv * r * ng_ref[...] * (1.0 + sc_ref[...]) + sh_ref[...]
        h_ref[...] = h.astype(BF16)
        ht_ref[...] = h.T.astype(BF16)

    row, vec = _row_spec(tm, D), _vec_spec(D)
    full, half = jax.ShapeDtypeStruct((S, D), F32), jax.ShapeDtypeStruct((S, D), BF16)
    outs, _ = _mm_rows(a, w, NN, tm, [x, g, ng, sc, sh], [row] + [vec] * 4,
                       [full, full, half, jax.ShapeDtypeStruct((D, S), BF16)], [row, row, row, _col_spec(D, tm)], epilogue, name)
    return outs


def _mm_loss_resid(a, w, x2, g2, target, tm, name):
    S, D = x2.shape

    def epilogue(t, step, x_ref, y_ref, g_ref, dx_ref, dt_ref, sums_ref):
        gv = g_ref[...]
        e = x_ref[...] + gv * t - y_ref[...]
        dx = e * (1.0 / D)
        dx_ref[...] = dx
        dt_ref[...] = (dx * gv).astype(BF16)
        _accumulate(sums_ref, _rows8([jnp.sum(e * e, axis=0, keepdims=True), jnp.sum(dx * t, axis=0, keepdims=True)], D), step)

    row, vec = _row_spec(tm, D), _vec_spec(D)
    outs, _ = _mm_rows(a, w, NN, tm, [x2, target, g2], [row, row, vec],
                       [jax.ShapeDtypeStruct((S, D), F32), jax.ShapeDtypeStruct((S, D), BF16), jax.ShapeDtypeStruct((8, D), F32)],
                       [row, row, pl.BlockSpec((8, D), lambda i: (0, 0))], epilogue, name)
    return outs


def _mm_rms_mod_bwd(a, w, xin, dres, ng, sc, tm, name, t_prev=None, g_prev=None, ride=None):
    S, D = xin.shape
    chain = t_prev is not None

    def epilogue(dhv, step, *refs):
        if chain:
            x_ref, dr_ref, ng_ref, sc_ref, t_ref, g_ref, dx_ref, sums_ref, dt_ref = refs
        else:
            x_ref, dr_ref, ng_ref, sc_ref, dx_ref, sums_ref = refs
        xv = x_ref[...]
        r = lax.rsqrt(jnp.mean(xv * xv, axis=-1, keepdims=True) + EPS)
        xh = xv * r
        ngv, scv = ng_ref[...], sc_ref[...]
        dxh = dhv * (ngv * (1.0 + scv))
        dx = dr_ref[...] + r * (dxh - xh * jnp.mean(dxh * xh, axis=-1, keepdims=True))
        dx_ref[...] = dx
        dhx = dhv * xh
        rows = [jnp.sum(dhv, axis=0, keepdims=True), jnp.sum(dhx * ngv, axis=0, keepdims=True),
                jnp.sum(dhx * (1.0 + scv), axis=0, keepdims=True)]
        if chain:
            dt_ref[...] = (dx * g_ref[...]).astype(BF16)
            rows.append(jnp.sum(dx * t_ref[...], axis=0, keepdims=True))
        _accumulate(sums_ref, _rows8(rows, D), step)

    row, vec = _row_spec(tm, D), _vec_spec(D)
    extras = [xin, dres, ng, sc] + ([t_prev, g_prev] if chain else [])
    extra_specs = [row, row, vec, vec] + ([row, vec] if chain else [])
    out_shapes = [jax.ShapeDtypeStruct((S, D), F32), jax.ShapeDtypeStruct((8, D), F32)] + (
        [jax.ShapeDtypeStruct((S, D), BF16)] if chain else [])
    out_specs = [row, pl.BlockSpec((8, D), lambda i: (0, 0))] + ([row] if chain else [])
    return _mm_rows(a, w, NT, tm, extras, extra_specs, out_shapes, out_specs, epilogue, name, ride=ride)


def _mm_in(h1, w_in_p, wg_p, bg, qg, kg, tm, ride=None):
    S = h1.shape[0]
    W = ATTN_DIM

    def epilogue(p, step, wg_ref, bg_ref, qg_ref, kg_ref, proj_ref, la_ref, qn_ref, kn_ref):
        proj_ref[...] = p
        z = _dot(p[:, O_GLR:O_GLR + LANE], wg_ref[...], NN, HI) + bg_ref[...]
        la_ref[...] = (jnp.minimum(z, 0.0) - jnp.log(1.0 + jnp.exp(-jnp.abs(z)))) * (1.0 / GLA_TAU)
        seg = _seg_matrix(W, ATTN_HD, 1.0 / ATTN_HD)
        for off, g_ref, o_ref, scale in ((O_AQ, qg_ref, qn_ref, ATTN_HD ** -0.5), (O_AK, kg_ref, kn_ref, 1.0)):
            xv = p[:, off:off + W]
            o_ref[...] = xv * lax.rsqrt(_seg_sum(xv * xv, seg) + EPS) * (g_ref[...] * scale)

    row = lambda w: pl.BlockSpec((tm, w), lambda i: (i, 0))
    const = lambda a: pl.BlockSpec(a.shape, lambda i: (0, 0))
    return _mm_rows(
        h1, w_in_p, NN, tm, [wg_p, bg, qg, kg], [const(wg_p), const(bg), const(qg), const(kg)],
        [jax.ShapeDtypeStruct((S, PROJ_W), F32), jax.ShapeDtypeStruct((S, GLA_QK), F32),
         jax.ShapeDtypeStruct((S, W), F32), jax.ShapeDtypeStruct((S, W), F32)],
        [row(PROJ_W), row(GLA_QK), row(W), row(W)], epilogue, "mm_in", ride=ride)


def _gate_bwd(dla, la, proj, wg_p, ts):
    S = proj.shape[0]

    def body(dla_ref, la_ref, glr_ref, w_ref, dglr_ref, gw_ref, gb_ref):
        i = pl.program_id(0)
        dz = dla_ref[...] * (1.0 / GLA_TAU) * (1.0 - jnp.exp(GLA_TAU * la_ref[...]))
        dglr_ref[...] = _dot(dz, w_ref[...], NT, HI).astype(BF16)
        gw = _dot(glr_ref[...], dz, TN, HI)
        gb = jnp.concatenate([jnp.sum(dz, axis=0, keepdims=True), jnp.zeros((7, GLA_QK), F32)], axis=0)

        @pl.when(i == 0)
        def _():
            gw_ref[...] = gw
            gb_ref[...] = gb

        @pl.when(i > 0)
        def _():
            gw_ref[...] += gw
            gb_ref[...] += gb

    return pl.pallas_call(
        body, grid=(S // ts,),
        in_specs=[pl.BlockSpec((ts, GLA_QK), lambda i: (i, 0)), pl.BlockSpec((ts, GLA_QK), lambda i: (i, 0)),
                  pl.BlockSpec((ts, LANE), lambda i: (i, O_GLR // LANE)), pl.BlockSpec((LANE, GLA_QK), lambda i: (0, 0))],
        out_specs=[pl.BlockSpec((ts, LANE), lambda i: (i, 0)), pl.BlockSpec((LANE, GLA_QK), lambda i: (0, 0)),
                   pl.BlockSpec((8, GLA_QK), lambda i: (0, 0))],
        out_shape=[jax.ShapeDtypeStruct((S, LANE), BF16), jax.ShapeDtypeStruct((LANE, GLA_QK), F32),
                   jax.ShapeDtypeStruct((8, GLA_QK), F32)],
        compiler_params=_cp("arbitrary"), name="gla_gate_bwd")(dla, la, proj, wg_p)


def _tri(lower):
    r = lax.broadcasted_iota(jnp.int32, (GLA_CHUNK, GLA_CHUNK), 0)
    c = lax.broadcasted_iota(jnp.int32, (GLA_CHUNK, GLA_CHUNK), 1)
    return jnp.where((r >= c) if lower else (c >= r), 1.0, 0.0).astype(F32)


GLA_SUB = 16
GLA_NSUB = GLA_CHUNK // GLA_SUB
PAIR_QK = 2 * GLA_DK
PAIR_V = 2 * GLA_DV


def _band_selector():
    r = lax.broadcasted_iota(jnp.int32, (GLA_SUB * PAIR_QK, LANE), 0)
    c = lax.broadcasted_iota(jnp.int32, (GLA_SUB * PAIR_QK, LANE), 1)
    dist, head = r // PAIR_QK, (r % PAIR_QK) // GLA_DK
    return jnp.where(c == head * GLA_DK + (GLA_SUB - 1 - dist), 1.0, 0.0).astype(BF16)


def _flip_matrix():
    r = lax.broadcasted_iota(jnp.int32, (GLA_CHUNK, GLA_CHUNK), 0)
    c = lax.broadcasted_iota(jnp.int32, (GLA_CHUNK, GLA_CHUNK), 1)
    return jnp.where(r + c == GLA_CHUNK - 1, 1.0, 0.0).astype(BF16)


def _state_mask():
    r = lax.broadcasted_iota(jnp.int32, (PAIR_V, PAIR_QK), 0)
    c = lax.broadcasted_iota(jnp.int32, (PAIR_V, PAIR_QK), 1)
    return (r < GLA_DV) == (c < GLA_DK)


class _GlaChunk:
    def __init__(self, qs, kc, vc, g, sel, exact):
        C = GLA_CHUNK
        self.qs, self.kc, self.vc = qs, kc, vc
        rows = lax.broadcasted_iota(jnp.int32, (C, 1), 0)
        lane = lax.broadcasted_iota(jnp.int32, (1, PAIR_QK), 1)
        self.rows, self.lane = rows, lane
        b = _dot(_tri(True), g, NN, HI)
        self.bl = b[C - 1:C, :]
        self.eb = jnp.exp(b)
        self.kdec = jnp.exp(self.bl - b)
        edge = lambda J: b[GLA_SUB * (J + 1):GLA_SUB * (J + 1) + 1, :]
        self.e_far = [jnp.exp(jnp.where(rows >= GLA_SUB * (J + 1), b - edge(J), NEG)) for J in range(GLA_NSUB - 1)]
        blk = rows // GLA_SUB
        bnext = edge(0)
        for J in range(1, GLA_NSUB - 1):
            bnext = jnp.where(blk == J, edge(J), bnext)
        self.e_khat = jnp.exp(jnp.where(blk < GLA_NSUB - 1, bnext - b, NEG))
        khat = kc * self.e_khat
        k2 = jnp.concatenate([jnp.where(lane < GLA_DK, khat, 0.0), jnp.where(lane >= GLA_DK, khat, 0.0)], axis=0)
        self.blk2 = jnp.concatenate([blk, blk], axis=0)
        self.m_far = jnp.concatenate([jnp.where(self.blk2 == J, k2, 0.0) for J in range(GLA_NSUB - 1)], axis=1).astype(BF16)
        self.qcat = jnp.concatenate([qs * e for e in self.e_far], axis=1).astype(BF16)
        a_far = _dot(self.qcat, self.m_far, NT)
        self.e_band, self.rk, hi_terms, lo_terms = [], [], [], []
        for d in range(GLA_SUB):
            rk = pltpu.roll(kc, d, 0) if d else kc
            rb = pltpu.roll(b, d, 0) if d else b
            e = jnp.exp(jnp.where(rows >= d, b - rb, NEG))
            self.e_band.append(e)
            self.rk.append(rk)
            if exact:
                t = (qs * e).astype(BF16).astype(F32) * rk.astype(BF16).astype(F32)
                hi = t.astype(BF16)
                hi_terms.append(hi)
                lo_terms.append((t - hi.astype(F32)).astype(BF16))
            else:
                hi_terms.append((qs * rk * e).astype(BF16))
        band = _dot(jnp.concatenate(hi_terms, axis=1), sel, NN)
        if exact:
            band = band + _dot(jnp.concatenate(lo_terms, axis=1), sel, NN)
        a_band = pltpu.roll(band, LANE - (GLA_SUB - 1), 1, stride=1, stride_axis=0)
        dist = rows - lane % GLA_DK
        self.far_mask = dist >= GLA_SUB
        self.band_mask = (dist >= 0) & (dist < GLA_SUB)
        self.a = (a_band + jnp.where(self.far_mask, a_far, 0.0)).astype(BF16)
        self.lane_v = lax.broadcasted_iota(jnp.int32, (1, PAIR_V), 1)
        self.v2 = jnp.concatenate([jnp.where(self.lane_v < GLA_DV, vc, 0.0), jnp.where(self.lane_v >= GLA_DV, vc, 0.0)],
                                  axis=0).astype(BF16)


def _gla_fwd(proj, la, tb, ride=None):
    S = proj.shape[0]
    C = GLA_CHUNK
    tb = min(tb, S)
    nbc = tb // C
    npair = GLA_HEADS // 2
    scale = GLA_DK ** -0.5

    def body(q_ref, k_ref, v_ref, la_ref, sel_ref, o_ref, st_ref, state):
        @pl.when(pl.program_id(1) == 0)
        def _():
            state[...] = jnp.zeros_like(state)

        def chunk(ci):
            sl = pl.ds(ci * C, C)
            ch = _GlaChunk(q_ref[sl, :] * scale, k_ref[sl, :], v_ref[sl, :], la_ref[sl, :], sel_ref[...], exact=ci == 0)
            st = state[...]
            st_ref[0, ci] = st
            o_ref[sl, :] = _dot((ch.qs * ch.eb).astype(BF16), st.astype(BF16), NT) + _dot(ch.a, ch.v2, NN)
            upd = _dot(ch.vc.astype(BF16), (ch.kc * ch.kdec).astype(BF16), TN)
            state[...] = st * jnp.exp(ch.bl) + jnp.where(_state_mask(), upd, 0.0)

        for ci in range(nbc):
            chunk(ci)

    qspec = lambda off: pl.BlockSpec((tb, PAIR_QK), lambda p, i: (i, off // PAIR_QK + p))
    ride_arrays, ride_gather = ride if ride else ([], [])
    nr = len(ride_arrays)
    grid = (npair, S // tb)
    outs = pl.pallas_call(
        _riding(body, 5, 2, ride_gather, grid), grid=grid,
        in_specs=[qspec(O_GQ), qspec(O_GK), pl.BlockSpec((tb, PAIR_V), lambda p, i: (i, O_GV // PAIR_V + p)),
                  pl.BlockSpec((tb, PAIR_QK), lambda p, i: (i, p)),
                  pl.BlockSpec((GLA_SUB * PAIR_QK, LANE), lambda p, i: (0, 0))] + [HBM_SPEC] * nr,
        out_specs=[pl.BlockSpec((tb, PAIR_V), lambda p, i: (i, p)),
                   pl.BlockSpec((1, nbc, PAIR_V, PAIR_QK), lambda p, i: (p, i, 0, 0))] + [HBM_SPEC] * nr,
        out_shape=[jax.ShapeDtypeStruct((S, GLA_V), F32), jax.ShapeDtypeStruct((npair, S // C, PAIR_V, PAIR_QK), F32)]
        + _exchange_shapes(ride_arrays, ride_gather),
        scratch_shapes=[pltpu.VMEM((PAIR_V, PAIR_QK), F32)] + (_exchange_sems(nr) if nr else []),
        compiler_params=_cp("arbitrary", "arbitrary"), name="gla_fwd")(proj, proj, proj, la, _band_selector(), *ride_arrays)
    return outs[0], outs[1], outs[2:]


def _gla_bwd(proj, la, do, states, tb, ride=None):
    S = proj.shape[0]
    C = GLA_CHUNK
    tb = min(tb, S)
    nbc = tb // C
    nblk = S // tb
    npair = GLA_HEADS // 2
    scale = GLA_DK ** -0.5

    def body(q_ref, k_ref, v_ref, la_ref, do_ref, st_ref, sel_ref, selt_ref, dq_ref, dk_ref, dv_ref, dla_ref, dstate):
        @pl.when(pl.program_id(1) == 0)
        def _():
            dstate[...] = jnp.zeros_like(dstate)

        def chunk(ci):
            sl = pl.ds(ci * C, C)
            ch = _GlaChunk(q_ref[sl, :] * scale, k_ref[sl, :], v_ref[sl, :], la_ref[sl, :], sel_ref[...], exact=ci == 0)
            qs, kc, rows = ch.qs, ch.kc, ch.rows
            doc_b = do_ref[sl, :].astype(BF16)
            st = st_ref[0, ci]
            dst = dstate[...]
            dst_b = dst.astype(BF16)
            ebl = jnp.exp(ch.bl)
            dq = _dot(doc_b, st.astype(BF16), NN) * ch.eb
            dk = _dot(ch.vc.astype(BF16), dst_b, NN) * ch.kdec
            dv = _dot((kc * ch.kdec).astype(BF16), dst_b, NT)
            dbl = jnp.sum(dst * st, axis=0, keepdims=True) * ebl + jnp.sum(kc * dk, axis=0, keepdims=True)
            da = _dot(doc_b, ch.v2, NT)
            dv2 = _dot(ch.a, doc_b, TN)
            dv = dv + jnp.where(ch.lane_v < GLA_DV, dv2[:C], dv2[C:])
            da_far = jnp.where(ch.far_mask, da, 0.0).astype(BF16)
            dqcat = _dot(da_far, ch.m_far, NN)
            dm = _dot(da_far, ch.qcat, TN)
            dk2 = jnp.zeros((2 * C, PAIR_QK), F32)
            for J in range(GLA_NSUB - 1):
                dq = dq + dqcat[:, J * PAIR_QK:(J + 1) * PAIR_QK] * ch.e_far[J]
                dk2 = dk2 + jnp.where(ch.blk2 == J, dm[:, J * PAIR_QK:(J + 1) * PAIR_QK], 0.0)
            dk = dk + jnp.where(ch.lane < GLA_DK, dk2[:C], dk2[C:]) * ch.e_khat
            flip = _flip_matrix()
            da_band = _dot(flip, jnp.where(ch.band_mask, da, 0.0).astype(BF16), NN)
            dband = pltpu.roll(da_band, LANE - (C - GLA_SUB), 1, stride=1, stride_axis=0)
            dband = _dot(flip, dband.astype(BF16), NN)
            dterms = _dot(dband.astype(BF16), selt_ref[...], NN)
            for d in range(GLA_SUB):
                dt = dterms[:, d * PAIR_QK:(d + 1) * PAIR_QK]
                dq = dq + dt * (ch.rk[d] * ch.e_band[d])
                dkr = dt * (qs * ch.e_band[d])
                dk = dk + (pltpu.roll(dkr, C - d, 0) if d else dkr)
            db = qs * dq - kc * dk
            db = jnp.where(rows == C - 1, db + dbl, db)
            dq_ref[sl, :] = (dq * scale).astype(BF16)
            dk_ref[sl, :] = dk.astype(BF16)
            dv_ref[sl, :] = dv.astype(BF16)
            dla_ref[sl, :] = _dot(_tri(False), db, NN, HI)
            upd = _dot(doc_b, (qs * ch.eb).astype(BF16), TN)
            dstate[...] = dst * ebl + jnp.where(_state_mask(), upd, 0.0)

        for ci in reversed(range(nbc)):
            chunk(ci)

    rev = lambda i: nblk - 1 - i
    qspec = lambda off: pl.BlockSpec((tb, PAIR_QK), lambda p, i: (rev(i), off // PAIR_QK + p))
    pair_qk = pl.BlockSpec((tb, PAIR_QK), lambda p, i: (rev(i), p))
    pair_v = pl.BlockSpec((tb, PAIR_V), lambda p, i: (rev(i), p))
    sel = _band_selector()
    ride_arrays, ride_gather = ride if ride else ([], [])
    nr = len(ride_arrays)
    grid = (npair, nblk)
    outs = pl.pallas_call(
        _riding(body, 8, 4, ride_gather, grid), grid=grid,
        in_specs=[qspec(O_GQ), qspec(O_GK), pl.BlockSpec((tb, PAIR_V), lambda p, i: (rev(i), O_GV // PAIR_V + p)),
                  pair_qk, pair_v, pl.BlockSpec((1, nbc, PAIR_V, PAIR_QK), lambda p, i: (p, rev(i), 0, 0)),
                  pl.BlockSpec((GLA_SUB * PAIR_QK, LANE), lambda p, i: (0, 0)),
                  pl.BlockSpec((LANE, GLA_SUB * PAIR_QK), lambda p, i: (0, 0))] + [HBM_SPEC] * nr,
        out_specs=[pair_qk, pair_qk, pair_v, pair_qk] + [HBM_SPEC] * nr,
        out_shape=[jax.ShapeDtypeStruct((S, GLA_QK), BF16), jax.ShapeDtypeStruct((S, GLA_QK), BF16),
                   jax.ShapeDtypeStruct((S, GLA_V), BF16), jax.ShapeDtypeStruct((S, GLA_QK), F32)]
        + _exchange_shapes(ride_arrays, ride_gather),
        scratch_shapes=[pltpu.VMEM((PAIR_V, PAIR_QK), F32)] + (_exchange_sems(nr) if nr else []),
        compiler_params=_cp("arbitrary", "arbitrary"), name="gla_bwd")(proj, proj, proj, la, do, states, sel, sel.T, *ride_arrays)
    return outs[0], outs[1], outs[2], outs[3], outs[4:]


def _gla_out(o, proj, gng, ts):
    S = o.shape[0]

    def body(o_ref, gr_ref, g_ref, y_ref):
        for h in range(GLA_HEADS):
            cols = slice(h * GLA_DV, (h + 1) * GLA_DV)
            ov, grv = o_ref[:, cols], gr_ref[:, cols]
            r = lax.rsqrt(jnp.mean(ov * ov, axis=-1, keepdims=True) + EPS)
            y_ref[:, cols] = (ov * r * g_ref[...] * (grv * _sigmoid(grv))).astype(BF16)

    return pl.pallas_call(
        body, grid=(S // ts,),
        in_specs=[pl.BlockSpec((ts, GLA_V), lambda i: (i, 0)), pl.BlockSpec((ts, GLA_V), lambda i: (i, O_GR // GLA_V)),
                  pl.BlockSpec((1, GLA_DV), lambda i: (0, 0))],
        out_specs=pl.BlockSpec((ts, GLA_V), lambda i: (i, 0)), out_shape=jax.ShapeDtypeStruct((S, GLA_V), BF16),
        compiler_params=_cp("parallel"), name="gla_out_fwd")(o, proj, gng)


def _mm_mixed_bwd(dt1, w_out, o, proj, gng, y_att, tm):
    S = o.shape[0]
    W = ATTN_DIM

    def epilogue(dm, step, o_ref, gr_ref, g_ref, y_ref, do_ref, dgr_ref, gg_ref, dy_ref, de_ref):
        gsum = jnp.zeros((1, GLA_DV), F32)
        for h in range(GLA_HEADS):
            cols = slice(h * GLA_DV, (h + 1) * GLA_DV)
            ov, grv, dy = o_ref[:, cols], gr_ref[:, cols], dm[:, cols]
            r = lax.rsqrt(jnp.mean(ov * ov, axis=-1, keepdims=True) + EPS)
            oh = ov * r
            sg = _sigmoid(grv)
            don = dy * (grv * sg)
            dgr_ref[:, cols] = (dy * (oh * g_ref[...]) * (sg * (1.0 + grv * (1.0 - sg)))).astype(BF16)
            gsum = gsum + jnp.sum(don * oh, axis=0, keepdims=True)
            doh = don * g_ref[...]
            do_ref[:, cols] = r * (doh - oh * jnp.mean(doh * oh, axis=-1, keepdims=True))
        _accumulate(gg_ref, _rows8([gsum], GLA_DV), step)
        dya = dm[:, GLA_V:]
        dy_ref[...] = dya
        de_ref[...] = _seg_sum(dya * y_ref[...], _seg_matrix(W, ATTN_HD, 1.0))

    half = pl.BlockSpec((tm, GLA_V), lambda i: (i, 0))
    outs, _ = _mm_rows(
        dt1, w_out, NT, tm, [o, proj, gng, y_att],
        [half, pl.BlockSpec((tm, GLA_V), lambda i: (i, O_GR // GLA_V)), pl.BlockSpec((1, GLA_DV), lambda i: (0, 0)), half],
        [jax.ShapeDtypeStruct((S, GLA_V), F32), jax.ShapeDtypeStruct((S, GLA_V), BF16), jax.ShapeDtypeStruct((8, GLA_DV), F32),
         jax.ShapeDtypeStruct((S, W), F32), jax.ShapeDtypeStruct((S, W), F32)],
        [half, half, pl.BlockSpec((8, GLA_DV), lambda i: (0, 0)), half, half], epilogue, "mm_dmixed")
    return outs


def _seg_matrix(width, seg, value):
    r = lax.broadcasted_iota(jnp.int32, (width, width), 0) // seg
    c = lax.broadcasted_iota(jnp.int32, (width, width), 1) // seg
    return jnp.where(r == c, value, 0.0).astype(BF16)


def _seg_sum(x, seg_matrix):
    hi = x.astype(BF16)
    lo = (x - hi.astype(F32)).astype(BF16)
    return _dot(hi, seg_matrix, NN) + _dot(lo, seg_matrix, NN)


ATTN_GROUP = 4


ATTN_TILE = max(DILATIONS) * ATTN_BLOCK


def _attn_rows(d, g, r, base=0):
    start = base + (g * d * ATTN_BLOCK if g >= 0 else ATTN_TILE - d * ATTN_BLOCK) + r
    return pl.ds(start, ATTN_BLOCK) if d == 1 else pl.ds(start, ATTN_BLOCK, stride=d)


def _for_blocks(d, G, fn):
    for g in range(G):
        if d <= ATTN_GROUP:
            for r in range(d):
                fn(g, r)
        else:
            def step(r, carry, g=g):
                fn(g, r)
                return carry
            lax.fori_loop(0, d, step, 0, unroll=ATTN_GROUP)


def _attn_specs(S):
    nb = S // ATTN_TILE

    def specs(off=0):
        return [pl.BlockSpec((ATTN_TILE, LANE), lambda hp, n: (n, off + hp)),
                pl.BlockSpec((ATTN_TILE, LANE), lambda hp, n: (jnp.maximum(n - 1, 0), off + hp)),
                pl.BlockSpec((ATTN_TILE, LANE), lambda hp, n: (jnp.minimum(n + 1, nb - 1), off + hp))]

    return nb, specs


def _slope(head):
    one = jnp.ones((1, 1), jnp.int32)
    return 1.0 / jnp.left_shift(one, one * (head + 1)).astype(F32)


def _attn_bias(d, hp, first_tile):
    B = ATTN_BLOCK
    iq = lax.broadcasted_iota(jnp.int32, (B, 2 * B), 0)
    ik = lax.broadcasted_iota(jnp.int32, (B, 2 * B), 1)
    rel = iq + B - ik
    window = (rel >= 0) & (rel <= B)
    relf = (d * rel).astype(F32)
    full = [jnp.where(window, -_slope(hp * 2 + h) * relf, NEG) for h in range(2)]
    edge = [jnp.where((ik >= B) | jnp.logical_not(first_tile), b, NEG) for b in full]
    return full, edge


def _attn_bias_t(d, hp, has_next):
    B = ATTN_BLOCK
    ik = lax.broadcasted_iota(jnp.int32, (B, B), 0)
    iq = lax.broadcasted_iota(jnp.int32, (B, B), 1)
    tiles = []
    for nxt in range(2):
        rel = iq - ik + nxt * B
        window = (rel >= 0) & (rel <= B)
        relf = (d * rel).astype(F32)
        tiles.append([jnp.where(window, -_slope(hp * 2 + h) * relf, NEG) for h in range(2)])
    tiles.append([jnp.where(has_next, b, NEG) for b in tiles[1]])
    return tiles


def _attn_fwd(qn, kn, proj):
    S, W = qn.shape
    T = ATTN_TILE
    nb, specs = _attn_specs(S)

    def body(q_ref, kp_ref, kc_ref, vp_ref, vc_ref, y_ref, l_ref, o_scr, l_scr):
        hp, n = pl.program_id(0), pl.program_id(1)
        lo = lax.broadcasted_iota(jnp.int32, (1, LANE), 1) < ATTN_HD
        for b, d in enumerate(DILATIONS):
            full, edge = _attn_bias(d, hp, n == 0)

            def sub(g, r, b=b, d=d, full=full, edge=edge):
                rows, before = _attn_rows(d, g, r), _attn_rows(d, g - 1, r)
                kb_ref, vb_ref = (kp_ref, vp_ref) if g == 0 else (kc_ref, vc_ref)
                bias = edge if g == 0 else full
                qv = q_ref[rows, :].astype(BF16)
                kv = jnp.concatenate([kb_ref[before, :], kc_ref[rows, :]], axis=0).astype(BF16)
                vv = jnp.concatenate([vb_ref[before, :], vc_ref[rows, :]], axis=0).astype(BF16)
                outs, lses = [], []
                for h in range(2):
                    qm = jnp.where(lo == (h == 0), qv, jnp.zeros_like(qv))
                    s = _dot(qm, kv, NT) + bias[h]
                    m = jnp.max(s, axis=-1, keepdims=True)
                    p = jnp.exp(s - m)
                    den = jnp.sum(p, axis=-1, keepdims=True)
                    outs.append(_dot(p.astype(BF16), vv, NN) / den)
                    lses.append(m + jnp.log(den))
                kept = _attn_rows(d, g, r, base=b * T)
                o_scr[kept, :] = jnp.where(lo, outs[0], outs[1])
                l_scr[kept, :] = jnp.where(lo, lses[0], lses[1])

            _for_blocks(d, T // (d * ATTN_BLOCK), sub)
        l1, l2, l3 = [l_scr[pl.ds(b * T, T), :] for b in range(len(DILATIONS))]
        o1, o2, o3 = [o_scr[pl.ds(b * T, T), :] for b in range(len(DILATIONS))]
        m = jnp.maximum(jnp.maximum(l1, l2), l3)
        e1, e2, e3 = jnp.exp(l1 - m), jnp.exp(l2 - m), jnp.exp(l3 - m)
        tot = e1 + e2 + e3
        y_ref[...] = (e1 * o1 + e2 * o2 + e3 * o3) / tot
        l_ref[...] = m + jnp.log(tot)

    cur, prev, _ = specs()
    vcur, vprev, _ = specs(O_AV // LANE)
    return pl.pallas_call(
        body, grid=(W // LANE, nb), in_specs=[cur, prev, cur, vprev, vcur], out_specs=[cur, cur],
        out_shape=[jax.ShapeDtypeStruct((S, W), F32)] * 2,
        scratch_shapes=[pltpu.VMEM((len(DILATIONS) * T, LANE), F32)] * 2,
        compiler_params=_cp("parallel", "arbitrary"), name="attn_fwd")(qn, kn, kn, proj, proj)


def _attn_mix(y_gla, y_att, ts):
    S, W = y_att.shape

    def body(yg, ya, mixed_ref, mixed_t_ref):
        y = ya[...]
        mixed_ref[:, :W] = yg[...]
        mixed_ref[:, W:] = y.astype(BF16)
        mixed_t_ref[:W, :] = yg[...].astype(F32).T.astype(BF16)
        mixed_t_ref[W:, :] = y.T.astype(BF16)

    spec = pl.BlockSpec((ts, W), lambda i: (i, 0))
    return pl.pallas_call(
        body, grid=(S // ts,), in_specs=[spec] * 2,
        out_specs=[pl.BlockSpec((ts, 2 * W), lambda i: (i, 0)), _col_spec(2 * W, ts)],
        out_shape=[jax.ShapeDtypeStruct((S, 2 * W), BF16), jax.ShapeDtypeStruct((2 * W, S), BF16)],
        compiler_params=_cp("parallel"), name="attn_mix")(y_gla, y_att)


def _attn_dq(qn, kn, proj, dy, lse, delta):
    S, W = qn.shape
    nb, specs = _attn_specs(S)

    def body(q_ref, kp_ref, kc_ref, vp_ref, vc_ref, dy_ref, l_ref, de_ref, dq_ref):
        hp, n = pl.program_id(0), pl.program_id(1)
        lo = lax.broadcasted_iota(jnp.int32, (1, LANE), 1) < ATTN_HD
        for b, d in enumerate(DILATIONS):
            _attn_dq_branch(b, d, _attn_bias(d, hp, n == 0), lo, q_ref, kp_ref, kc_ref, vp_ref, vc_ref, dy_ref, l_ref, de_ref, dq_ref)

    cur, prev, _ = specs()
    vcur, vprev, _ = specs(O_AV // LANE)
    return pl.pallas_call(
        body, grid=(W // LANE, nb), in_specs=[cur, prev, cur, vprev, vcur, cur, cur, cur], out_specs=cur,
        out_shape=jax.ShapeDtypeStruct((S, W), F32),
        compiler_params=_cp("parallel", "arbitrary"), name="attn_dq")(qn, kn, kn, proj, proj, dy, lse, delta)


def _attn_dq_branch(b, d, biases, lo, q_ref, kp_ref, kc_ref, vp_ref, vc_ref, dy_ref, l_ref, de_ref, dq_ref):
    full, edge = biases

    def sub(g, r):
        rows, before = _attn_rows(d, g, r), _attn_rows(d, g - 1, r)
        kb_ref, vb_ref = (kp_ref, vp_ref) if g == 0 else (kc_ref, vc_ref)
        bias = edge if g == 0 else full
        qv, dyv = q_ref[rows, :].astype(BF16), dy_ref[rows, :]
        lv, dev = l_ref[rows, :], de_ref[rows, :]
        kv = jnp.concatenate([kb_ref[before, :], kc_ref[rows, :]], axis=0).astype(BF16)
        vv = jnp.concatenate([vb_ref[before, :], vc_ref[rows, :]], axis=0).astype(BF16)
        outs = []
        for h in range(2):
            sel = lo == (h == 0)
            qm = jnp.where(sel, qv, jnp.zeros_like(qv))
            dym = jnp.where(sel, dyv, 0.0).astype(BF16)
            lse_h = lv[:, h * ATTN_HD:h * ATTN_HD + 1]
            del_h = dev[:, h * ATTN_HD:h * ATTN_HD + 1]
            p = jnp.exp(_dot(qm, kv, NT) + bias[h] - lse_h)
            ds = p * (_dot(dym, vv, NT) - del_h)
            outs.append(_dot(ds.astype(BF16), kv, NN) * (ATTN_HD ** -0.5))
        dq = jnp.where(lo, outs[0], outs[1])
        dq_ref[rows, :] = dq if b == 0 else dq_ref[rows, :] + dq

    _for_blocks(d, ATTN_TILE // (d * ATTN_BLOCK), sub)


def _attn_dkv(qn, kn, proj, dy, lse, delta):
    S, W = qn.shape
    nb, specs = _attn_specs(S)

    def body(k_ref, v_ref, qc_ref, qn_ref, dyc_ref, dyn_ref, lc_ref, ln_ref, dec_ref, den_ref, dk_ref, dv_ref):
        hp, n = pl.program_id(0), pl.program_id(1)
        lo = lax.broadcasted_iota(jnp.int32, (1, LANE), 1) < ATTN_HD
        cur_refs, next_refs = (qc_ref, dyc_ref, lc_ref, dec_ref), (qn_ref, dyn_ref, ln_ref, den_ref)
        for b, d in enumerate(DILATIONS):
            _attn_dkv_branch(b, d, _attn_bias_t(d, hp, n + 1 < nb), lo, k_ref, v_ref, cur_refs, next_refs, dk_ref, dv_ref)

    cur, _, nxt = specs()
    vcur, _, _ = specs(O_AV // LANE)
    return pl.pallas_call(
        body, grid=(W // LANE, nb), in_specs=[cur, vcur, cur, nxt, cur, nxt, cur, nxt, cur, nxt], out_specs=[cur, cur],
        out_shape=[jax.ShapeDtypeStruct((S, W), F32)] * 2,
        compiler_params=_cp("parallel", "arbitrary"), name="attn_dkv")(
            kn, proj, qn, qn, dy, dy, lse, lse, delta, delta)


def _attn_dkv_branch(b, d, biases, lo, k_ref, v_ref, cur_refs, next_refs, dk_ref, dv_ref):
    B = ATTN_BLOCK
    own, inner, outer = biases
    G = ATTN_TILE // (d * B)

    def sub(g, r):
        rows = _attn_rows(d, g, r)
        kv, vv = k_ref[rows, :].astype(BF16), v_ref[rows, :].astype(BF16)
        dk = jnp.zeros((B, LANE), F32)
        dv = jnp.zeros((B, LANE), F32)
        inside = g + 1 < G
        after = _attn_rows(d, g + 1 if inside else 0, r)
        for bias, qrows, (q_ref, dy_ref, l_ref, de_ref) in (
                (own, rows, cur_refs), (inner if inside else outer, after, cur_refs if inside else next_refs)):
            qv, dyv = q_ref[qrows, :].astype(BF16), dy_ref[qrows, :]
            lt, det = l_ref[qrows, :].T, de_ref[qrows, :].T
            for h in range(2):
                sel = lo == (h == 0)
                qm = jnp.where(sel, qv, jnp.zeros_like(qv))
                dym = jnp.where(sel, dyv, 0.0).astype(BF16)
                lse_h = lt[h * ATTN_HD:h * ATTN_HD + 1, :]
                del_h = det[h * ATTN_HD:h * ATTN_HD + 1, :]
                pt = jnp.exp(_dot(kv, qm, NT) + bias[h] - lse_h)
                dv = dv + _dot(pt.astype(BF16), dym, NN)
                dst = pt * (_dot(vv, dym, NT) - del_h)
                dk = dk + _dot(dst.astype(BF16), qm, NN)
        dk_ref[rows, :] = dk if b == 0 else dk_ref[rows, :] + dk
        dv_ref[rows, :] = dv if b == 0 else dv_ref[rows, :] + dv

    _for_blocks(d, G, sub)


def _attn_post(dq, dk, dv, proj, qg, kg, pieces, ts):
    S = proj.shape[0]
    W = ATTN_DIM
    n = len(pieces)

    def body(dq_ref, dk_ref, dv_ref, aq_ref, ak_ref, qg_ref, kg_ref, *refs):
        piece_refs, (dp_ref, gg_ref) = refs[:n], refs[n:]
        i = pl.program_id(0)
        seg = _seg_matrix(W, ATTN_HD, 1.0 / ATTN_HD)
        gsums = []
        for d_ref, x_ref, g_ref, off in ((dq_ref, aq_ref, qg_ref, O_AQ), (dk_ref, ak_ref, kg_ref, O_AK)):
            dy = d_ref[...]
            xv = x_ref[...]
            r = lax.rsqrt(_seg_sum(xv * xv, seg) + EPS)
            xh = xv * r
            dxh = dy * g_ref[...]
            dp_ref[:, off:off + W] = (r * (dxh - xh * _seg_sum(dxh * xh, seg))).astype(BF16)
            gsums.append(jnp.sum(dy * xh, axis=0, keepdims=True))
        dp_ref[:, O_AV:O_AV + W] = dv_ref[...].astype(BF16)
        for p_ref, (p, off) in zip(piece_refs, pieces):
            dp_ref[:, off:off + p.shape[1]] = p_ref[...]
        dp_ref[:, O_GLR + LANE:] = jnp.zeros((ts, PROJ_W - O_GLR - LANE), BF16)
        _accumulate(gg_ref, _rows8(gsums, W), i)

    row = lambda w: pl.BlockSpec((ts, w), lambda i: (i, 0))
    blk = lambda off: pl.BlockSpec((ts, W), lambda i: (i, off // W))
    vec = pl.BlockSpec((1, W), lambda i: (0, 0))
    return pl.pallas_call(
        body, grid=(S // ts,),
        in_specs=[row(W)] * 3 + [blk(O_AQ), blk(O_AK), vec, vec] + [row(p.shape[1]) for p, _ in pieces],
        out_specs=[row(PROJ_W), pl.BlockSpec((8, W), lambda i: (0, 0))],
        out_shape=[jax.ShapeDtypeStruct((S, PROJ_W), BF16), jax.ShapeDtypeStruct((8, W), F32)],
        compiler_params=_cp("arbitrary"), name="attn_post")(dq, dk, dv, proj, proj, qg, kg, *[p for p, _ in pieces])


def _shift_down(cur, halo, n):
    return pltpu.roll(jnp.concatenate([halo, cur], axis=0), n, 0)[8:]


def _shift_up(cur, halo, n):
    ts = cur.shape[0]
    return pltpu.roll(jnp.concatenate([cur, halo], axis=0), ts + 8 - n, 0)[:ts]


def _conv(cur, halo, w, b):
    return b + w[0:1, :] * _shift_down(cur, halo, 2) + w[1:2, :] * _shift_down(cur, halo, 1) + w[2:3, :] * cur


def _mm_up_swiglu(h2, w_up, conv_w8, conv_b, tm, tc, ride=None):
    S, D = h2.shape
    F = w_up.shape[1] // 2
    nc = F // tc
    grid = (S // tm, nc)
    ride_arrays, ride_gather = ride if ride else ([], [])
    nr = len(ride_arrays)

    def body(h_ref, bg_ref, bv_ref, wg_ref, wv_ref, cg_ref, cv_ref, u0_ref, a_ref, at_ref, halo):
        i, j = pl.program_id(0), pl.program_id(1)
        hv = h_ref[...]
        acts = []
        for h, (b_ref, w_ref, c_ref) in enumerate(((bg_ref, wg_ref, cg_ref), (bv_ref, wv_ref, cv_ref))):
            u = _dot(hv, b_ref[...], NN)
            u0_ref[h] = u
            acts.append(_conv(u, jnp.where(i == 0, 0.0, halo[j, h]), w_ref[...], c_ref[...]))
            halo[j, h] = u[tm - 8:, :]
        g, v = acts
        a = g * _sigmoid(g) * v
        a_ref[...] = a.astype(BF16)
        at_ref[...] = a.T.astype(BF16)

    wcol = lambda rows, off: pl.BlockSpec((rows, tc), lambda i, j: (0, j + off))
    outs = pl.pallas_call(
        _riding(body, 7, 3, ride_gather, grid), grid=grid,
        in_specs=[pl.BlockSpec((tm, D), lambda i, j: (i, 0)), wcol(D, 0), wcol(D, nc), wcol(8, 0), wcol(8, nc), wcol(1, 0), wcol(1, nc)]
        + [HBM_SPEC] * nr,
        out_specs=[pl.BlockSpec((2, tm, tc), lambda i, j: (0, i, j)), pl.BlockSpec((tm, tc), lambda i, j: (i, j)),
                   pl.BlockSpec((tc, tm), lambda i, j: (j, i))] + [HBM_SPEC] * nr,
        out_shape=[jax.ShapeDtypeStruct((2, S, F), F32), jax.ShapeDtypeStruct((S, F), BF16), jax.ShapeDtypeStruct((F, S), BF16)]
        + _exchange_shapes(ride_arrays, ride_gather),
        scratch_shapes=[pltpu.VMEM((nc, 2, 8, tc), F32)] + (_exchange_sems(nr) if nr else []),
        compiler_params=_cp("arbitrary", "arbitrary"), name="mm_up")(
            h2, w_up, w_up, conv_w8, conv_w8, conv_b, conv_b, *ride_arrays)
    return outs[0], outs[1], outs[2], outs[3:]


def _mm_da_du0(dt2, w_down, u0, conv_w8, conv_b, tm, tc, ride=None):
    _, S, F = u0.shape
    D = dt2.shape[1]
    hb = tm // 8
    nrow = S // tm
    grid = (nrow,)
    ride_arrays, ride_gather = ride if ride else ([], [])
    nr = len(ride_arrays)

    def body(dt_ref, wd_ref, ug_ref, ugh_ref, uv_ref, uvh_ref, w_ref, b_ref, o_ref, sg_ref, sv_ref, following):
        i = pl.program_id(0)
        at_start, at_end = i == nrow - 1, i == 0
        dt = dt_ref[...]
        for c in range(F // tc):
            sums = []
            halves = []
            for h, (u_ref, h_ref) in enumerate(((ug_ref, ugh_ref), (uv_ref, uvh_ref))):
                cols = slice(h * F + c * tc, h * F + (c + 1) * tc)
                u, halo, w = u_ref[:, c * tc:(c + 1) * tc], jnp.where(at_start, 0.0, h_ref[:, c * tc:(c + 1) * tc]), w_ref[:, cols]
                s2, s1 = _shift_down(u, halo, 2), _shift_down(u, halo, 1)
                halves.append((b_ref[:, cols] + w[0:1, :] * s2 + w[1:2, :] * s1 + w[2:3, :] * u, s2, s1, u, w, cols))
            g, v = halves[0][0], halves[1][0]
            dav = _dot(dt, wd_ref[c * tc:(c + 1) * tc, :], NT)
            sig = _sigmoid(g)
            dus = (dav * v * (sig * (1.0 + g * (1.0 - sig))), dav * (g * sig))
            for h, du in enumerate(dus):
                _, s2, s1, u, w, cols = halves[h]
                after = jnp.where(at_end, 0.0, following[h, :, c * tc:(c + 1) * tc])
                o_ref[:, cols] = (w[2:3, :] * du + w[1:2, :] * _shift_up(du, after, 1) + w[0:1, :] * _shift_up(du, after, 2)).astype(BF16)
                following[h, :, c * tc:(c + 1) * tc] = du[0:8, :]
                sums.append(_rows8([jnp.sum(du * s2, axis=0, keepdims=True), jnp.sum(du * s1, axis=0, keepdims=True),
                                    jnp.sum(du * u, axis=0, keepdims=True), jnp.sum(du, axis=0, keepdims=True)], tc))
            for sums_ref, part in zip((sg_ref, sv_ref), sums):
                @pl.when(i == 0)
                def _(sums_ref=sums_ref, part=part, c=c):
                    sums_ref[:, c * tc:(c + 1) * tc] = part

                @pl.when(i > 0)
                def _(sums_ref=sums_ref, part=part, c=c):
                    sums_ref[:, c * tc:(c + 1) * tc] += part

    rev = lambda i: nrow - 1 - i
    main = lambda h: pl.BlockSpec((None, tm, F), lambda i: (h, rev(i), 0))
    halo = lambda h: pl.BlockSpec((None, 8, F), lambda i: (h, jnp.maximum(rev(i) * hb - 1, 0), 0))
    whole = lambda a: pl.BlockSpec(a.shape, lambda i: (0,) * a.ndim, pipeline_mode=pl.Buffered(1))
    sums_spec = pl.BlockSpec((8, F), lambda i: (0, 0))
    outs = pl.pallas_call(
        _riding(body, 8, 3, ride_gather, grid), grid=grid,
        in_specs=[pl.BlockSpec((tm, D), lambda i: (rev(i), 0)), whole(w_down), main(0), halo(0), main(1), halo(1),
                  whole(conv_w8), whole(conv_b)] + [HBM_SPEC] * nr,
        out_specs=[pl.BlockSpec((tm, 2 * F), lambda i: (rev(i), 0)), sums_spec, sums_spec] + [HBM_SPEC] * nr,
        out_shape=[jax.ShapeDtypeStruct((S, 2 * F), BF16), jax.ShapeDtypeStruct((8, F), F32), jax.ShapeDtypeStruct((8, F), F32)]
        + _exchange_shapes(ride_arrays, ride_gather),
        scratch_shapes=[pltpu.VMEM((2, 8, F), F32)] + (_exchange_sems(nr) if nr else []),
        compiler_params=_cp("arbitrary"), name="mm_da")(
            dt2, w_down, u0, u0, u0, u0, conv_w8, conv_b, *ride_arrays)
    return outs[0], outs[1], outs[2], outs[3:]


def _adamw(w, g, m, v, name, slots=False):
    shape = w.shape
    view = (math.prod(shape[:-1]), shape[-1])
    R, C = view
    limit = SUM_BLOCK_ELEMS // 2 if slots else SUM_BLOCK_ELEMS
    fits = [t for t in range(16, R + 1, 16) if R % t == 0 and t * C <= limit]
    tr = max(fits) if fits else R

    def body(w_ref, g_ref, m_ref, v_ref, *outs):
        if slots:
            gv = g_ref[0].astype(F32)
            for s in range(1, N_DEV):
                gv = gv + g_ref[s].astype(F32)
            outs[0][...] = gv
        else:
            gv = g_ref[...]
        d_ref, nm_ref, nv_ref = outs[-3:]
        nm = ADAM_B1 * m_ref[...] + (1.0 - ADAM_B1) * gv
        nv = ADAM_B2 * v_ref[...] + (1.0 - ADAM_B2) * (gv * gv)
        m_hat = nm / (1.0 - ADAM_B1 ** ADAM_STEP)
        v_hat = nv / (1.0 - ADAM_B2 ** ADAM_STEP)
        d_ref[...] = -ADAM_LR * (m_hat / (jnp.sqrt(v_hat) + ADAM_EPS) + ADAM_WD * w_ref[...])
        nm_ref[...] = nm
        nv_ref[...] = nv

    spec = pl.BlockSpec((tr, C), lambda i: (i, 0))
    g_spec = pl.BlockSpec((N_DEV, tr, C), lambda i: (0, i, 0)) if slots else spec
    n_out = 4 if slots else 3
    outs = pl.pallas_call(
        body, grid=(R // tr,), in_specs=[spec, g_spec, spec, spec], out_specs=[spec] * n_out,
        out_shape=[jax.ShapeDtypeStruct(view, F32)] * n_out, compiler_params=_cp("parallel"), name=name)(
            w.reshape(view), g if slots else g.reshape(view), m.reshape(view), v.reshape(view))
    outs = [o.reshape(shape) for o in outs]
    return outs if slots else [g.reshape(shape)] + outs


def _pad_rows8(a):
    return jnp.concatenate([a, jnp.zeros((8 - a.shape[0], a.shape[1]), a.dtype)], axis=0)


def _local_step(x, target, mod, n1g, w_in_s, conv_w_s, wg_s, bg, gng, qng, kng, w_out_s, n2g, w_up_s, conv_b, w_down_s):
    S, D = x.shape
    F = w_down_s.shape[0] * N_DEV
    cw_c, wg_c = conv_w_s.shape[1], wg_s.shape[1]
    ts = min(512, S)
    sh1, sc1, g1, sh2, sc2, g2 = [mod[i:i + 1] for i in range(6)]
    qg_t, kg_t = jnp.tile(qng, (1, ATTN_HEADS)), jnp.tile(kng, (1, ATTN_HEADS))

    small = jnp.concatenate([conv_w_s.reshape(1, -1), wg_s.reshape(1, -1)], axis=1)
    n_small = small.shape[1]
    small = jnp.pad(small, ((0, 0), (0, -n_small % LANE)))
    h1, h1_t, (g_in, g_small) = _rms_mod(x, n1g, sc1, sh1, ts, "rms_mod1", ride=([w_in_s, small], [VIA_SIBLING, True]))
    w_in_full = _cols_from_blocks(g_in)
    w_in_p = jnp.concatenate([w_in_full[:, :GLR_SRC], w_in_full[:, GLR_SRC + GLA_RANK:],
                              w_in_full[:, GLR_SRC:GLR_SRC + GLA_RANK], jnp.zeros((D, PROJ_W - O_GLR - GLA_RANK), BF16)], axis=1)
    g_small = g_small.reshape(N_DEV, -1)
    conv_w8 = _pad_rows8(jnp.stack([g_small[:, t * cw_c:(t + 1) * cw_c].reshape(-1) for t in range(3)]))
    wg_full = _cols_from_blocks(g_small[:, 3 * cw_c:n_small].reshape(N_DEV, GLA_RANK, wg_c))
    wg_p = jnp.concatenate([wg_full, jnp.zeros((LANE - GLA_RANK, wg_full.shape[1]), F32)], axis=0)
    (proj, la, qn, kn), (g_out,) = _mm_in(h1, w_in_p, wg_p, bg, qg_t, kg_t, ts, ride=([w_out_s], [True]))
    w_out = g_out.reshape(-1, D)
    o_gla, states, (g_up,) = _gla_fwd(proj, la, 512, ride=([w_up_s], [True]))
    w_up = _cols_from_blocks(g_up)
    y_gla = _gla_out(o_gla, proj, gng, ts)
    y_att, lse = _attn_fwd(qn, kn, proj)
    mixed, mixed_t = _attn_mix(y_gla, y_att, ts)
    t1, x2, h2, h2_t = _mm_resid_rms_mod(mixed, w_out, x, g1, n2g, sc2, sh2, ts, "mm_out")
    tc = 1408 if F % 1408 == 0 else F
    u0, a, a_t, (g_down,) = _mm_up_swiglu(h2, w_up, conv_w8, conv_b, ts, tc, ride=([w_down_s], [True]))
    w_down = g_down.reshape(F, D)
    dx3, dt2, sums3 = _mm_loss_resid(a, w_down, x2, g2, target, ts, "mm_down")
    loss_row, dg2 = sums3[0:1], sums3[1:2]

    g_w_down = _mm(a_t, dt2, NN, 1408, 1024, 2048, F32, "mm_gw_down")
    du0, sums_g, sums_v, (r_down,) = _mm_da_du0(dt2, w_down, u0, conv_w8, conv_b, min(256, S), tc,
                                                ride=([g_w_down.reshape(N_DEV, -1, D)], [False]))
    g_conv_w = jnp.concatenate([sums_g[0:3], sums_v[0:3]], axis=1)
    g_conv_b = jnp.concatenate([sums_g[3:4], sums_v[3:4]], axis=1)
    g_w_up = _mm(h2_t, du0, NN, 512, 2816, 2048, F32, "mm_gw_up")
    (dx2, sums2, dt1), _ = _mm_rms_mod_bwd(du0, w_up, x2, dx3, n2g, sc2, ts, "mm_dh2", t_prev=t1, g_prev=g1)
    dsh2, dsc2, g_n2g, dg1 = sums2[0:1], sums2[1:2], sums2[2:3], sums2[3:4]
    g_w_out = _mm(mixed_t, dt1, NN, 1024, 1024, 2048, F32, "mm_gw_out")
    do_gla, dgr, gng_sums, dy_att, delta = _mm_mixed_bwd(dt1, w_out, o_gla, proj, gng, y_att, ts)
    dgq, dgk, dgv, dla, (r_up, r_out) = _gla_bwd(
        proj, la, do_gla, states, 512, ride=([_col_blocks(g_w_up), g_w_out.reshape(N_DEV, -1, D)], [False, False]))
    dglr, g_wg_p, gb_sums = _gate_bwd(dla, la, proj, wg_p, ts)
    dqn = _attn_dq(qn, kn, proj, dy_att, lse, delta)
    dkn, dvn = _attn_dkv(qn, kn, proj, dy_att, lse, delta)
    dproj, qk_sums = _attn_post(dqn, dkn, dvn, proj, qg_t, kg_t,
                                [(dgq, O_GQ), (dgk, O_GK), (dgv, O_GV), (dgr, O_GR), (dglr, O_GLR)], ts)
    g_w_in_p = _mm(h1_t, dproj, NN, 512, PROJ_W, 1024, BF16, "mm_gw_in")
    g_w_in = jnp.concatenate([g_w_in_p[:, :GLR_SRC], g_w_in_p[:, O_GLR:O_GLR + GLA_RANK], g_w_in_p[:, GLR_SRC:O_GLR]], axis=1)
    (dx, sums1), (r_in,) = _mm_rms_mod_bwd(dproj, w_in_p, x, dx2, n1g, sc1, ts, "mm_dh1",
                                           ride=([_col_blocks(g_w_in)], [False]))
    dsh1, dsc1, g_n1g = sums1[0:1], sums1[1:2], sums1[2:3]

    dmod = jnp.concatenate([dsh1, dsc1, dg1, dsh2, dsc2, dg2], axis=1)
    grads = dict(n1g=g_n1g, w_in=r_in, wg=g_wg_p[:GLA_RANK], bg=gb_sums[0:1], gng=gng_sums[0:1],
                 qng_lanes=qk_sums[0:1], kng_lanes=qk_sums[1:2], w_out=r_out, n2g=g_n2g, w_up=r_up,
                 conv_w=g_conv_w, conv_b=g_conv_b, w_down=r_down)
    return loss_row, dx, dmod, grads


def _col_blocks(a):
    R, W = a.shape
    return a.reshape(R, N_DEV, W // N_DEV).transpose(1, 0, 2)


def _cols_from_blocks(a):
    n, R, C = a.shape
    return a.transpose(1, 0, 2).reshape(R, n * C)


def kernel(x, c, w_ada, b_ada, norm1_g, w_in, gla_w_gate, gla_b_gate, gla_norm_g, q_norm_g, k_norm_g, w_out, norm2_g, w_up, conv_w, conv_b, w_down, loss_target, m_w_ada, m_b_ada, m_norm1_g, m_w_in, m_gla_w_gate, m_gla_b_gate, m_gla_norm_g, m_q_norm_g, m_k_norm_g, m_w_out, m_norm2_g, m_w_up, m_conv_w, m_conv_b, m_w_down, v_w_ada, v_b_ada, v_norm1_g, v_w_in, v_gla_w_gate, v_gla_b_gate, v_gla_norm_g, v_q_norm_g, v_k_norm_g, v_w_out, v_norm2_g, v_w_up, v_conv_w, v_conv_b, v_w_down):
    axes = ("x", "y", "c")
    me = 4 * lax.axis_index("x") + 2 * lax.axis_index("y") + lax.axis_index("c")
    D = x.shape[2]
    x2d, tgt2d = x[0], loss_target[0]
    w_in_s, w_out_s, w_up_s, w_down_s, w_ada_s = w_in[0], w_out[0], w_up[0], w_down[0], w_ada[0]
    conv_w_s, wg_s = conv_w[0], gla_w_gate[0]
    ada_c, wg_c, cw_c = w_ada_s.shape[1], wg_s.shape[1], conv_w_s.shape[1]

    g_c, = _exchange([c], [True], "gather_c")
    c_all = g_c.reshape(N_DEV, D)

    b_shard = lax.dynamic_slice(b_ada, (0, me * ada_c), (1, ada_c))
    mod_part = _ada_fwd(c_all, w_ada_s, b_shard)
    mod_recv, = _exchange([mod_part.reshape(N_DEV, 1, ada_c)], [False], "exchange_mod")
    mod = mod_recv.reshape(6, D)

    loss_row, dx, dmod, gr = _local_step(
        x2d, tgt2d, mod, norm1_g, w_in_s.astype(BF16), conv_w_s, wg_s, gla_b_gate, gla_norm_g, q_norm_g, k_norm_g,
        w_out_s.astype(BF16), norm2_g, w_up_s.astype(BF16), conv_b, w_down_s.astype(BF16))
    loss = lax.psum(0.5 / D * jnp.sum(loss_row), axes)

    parts = [dmod, gr["n1g"], gr["bg"], gr["gng"], gr["qng_lanes"], gr["kng_lanes"], gr["n2g"], gr["conv_b"],
             gr["wg"].reshape(1, -1), gr["conv_w"].reshape(1, -1)]
    sizes = [p.shape[1] for p in parts]
    packed = jnp.concatenate(parts, axis=1)
    packed = jnp.pad(packed, ((0, 0), (0, -packed.shape[1] % (8 * LANE))))
    gathered, = _exchange([packed.reshape(8, -1)], [True], "gather_small_grads")
    gathered = gathered.reshape(N_DEV, -1)
    total = _sum_slots(gathered.reshape(N_DEV, 8, -1), "sum_small_grads").reshape(1, -1)
    offs = [0]
    for s_ in sizes:
        offs.append(offs[-1] + s_)
    t_dmod, t_n1g, t_bg, t_gng, t_qng, t_kng, t_n2g, t_conv_b, t_wg, t_conv_w = [
        total[:, offs[i]:offs[i + 1]] for i in range(len(sizes))]
    g_b_ada = t_dmod
    g_qng = t_qng.reshape(ATTN_HEADS, ATTN_HD).sum(axis=0, keepdims=True)
    g_kng = t_kng.reshape(ATTN_HEADS, ATTN_HD).sum(axis=0, keepdims=True)
    g_wg = lax.dynamic_slice(t_wg.reshape(GLA_RANK, -1), (0, me * wg_c), (GLA_RANK, wg_c))
    g_conv_w = lax.dynamic_slice(t_conv_w.reshape(3, -1), (0, me * cw_c), (3, cw_c))
    dmod_shard = lax.dynamic_slice(gathered[:, :6 * D], (0, me * ada_c), (N_DEV, ada_c))
    g_w_ada = _ada_bwd(c_all, dmod_shard)

    g_w_in, g_w_out, g_w_up, g_w_down = gr["w_in"], gr["w_out"], gr["w_up"], gr["w_down"]
    in_slots = {"w_in", "w_out", "w_up", "w_down"}
    names = ["w_ada", "b_ada", "norm1_g", "w_in", "gla_w_gate", "gla_b_gate", "gla_norm_g", "q_norm_g", "k_norm_g",
             "w_out", "norm2_g", "w_up", "conv_w", "conv_b", "w_down"]
    ws = [w_ada, b_ada, norm1_g, w_in, gla_w_gate, gla_b_gate, gla_norm_g, q_norm_g, k_norm_g, w_out, norm2_g, w_up, conv_w, conv_b, w_down]
    ms = [m_w_ada, m_b_ada, m_norm1_g, m_w_in, m_gla_w_gate, m_gla_b_gate, m_gla_norm_g, m_q_norm_g, m_k_norm_g, m_w_out, m_norm2_g, m_w_up, m_conv_w, m_conv_b, m_w_down]
    vs = [v_w_ada, v_b_ada, v_norm1_g, v_w_in, v_gla_w_gate, v_gla_b_gate, v_gla_norm_g, v_q_norm_g, v_k_norm_g, v_w_out, v_norm2_g, v_w_up, v_conv_w, v_conv_b, v_w_down]
    gs = [g_w_ada, g_b_ada, t_n1g, g_w_in, g_wg, t_bg, t_gng, g_qng, g_kng, g_w_out, t_n2g, g_w_up, g_conv_w, t_conv_b, g_w_down]
    grads, deltas, new_ms, new_vs = [], [], [], []
    for nm, w, g, m, v in zip(names, ws, gs, ms, vs):
        g_, d_, m_, v_ = _adamw(w, g, m, v, "adamw_" + nm, slots=nm in in_slots)
        grads.append(g_)
        deltas.append(d_)
        new_ms.append(m_)
        new_vs.append(v_)
    return (loss, dx.reshape(x.shape), *grads, *deltas, *new_ms, *new_vs)
```
